```python
import math
import jax, jax.numpy as jnp
from jax import lax
import numpy as np

D_MODEL = 2048
BATCH = 8
SEQ = 2048
DEPTH = 4

CHUNK = 64
Q_BLOCK = 128
MIX_W = D_MODEL // 2
N_BRANCH = 3
MLA_HEADS = 8
MLA_NOPE = 128
MLA_ROPE = 64
MLA_QK = MLA_NOPE + MLA_ROPE
MLA_V = MIX_W // MLA_HEADS
Q_LORA = D_MODEL // 4
KV_LORA = D_MODEL // 8
ROPE_THETA = 10000.0
FOX_DH = 128
FOX_HEADS = MIX_W // FOX_DH
CH_DH = 128
CH_HEADS = MIX_W // CH_DH
LEFT_CHUNKS = 8
BAND = (LEFT_CHUNKS + 1) * CHUNK
REL_CLIP = 128
N_REL = 2 * REL_CLIP + 1
MEM_LEN = 256
X_HEADS = 4
X_DH = 128
D_FF = 4 * D_MODEL
EPS = 1e-6
NEG = -1e30

SPLIT_SIZES = (Q_LORA, KV_LORA, MLA_ROPE, 3 * FOX_HEADS * FOX_DH, FOX_HEADS,
               3 * CH_HEADS * CH_DH, N_BRANCH * D_MODEL)
D_IN = sum(SPLIT_SIZES)
SPLIT_CUTS = tuple(int(c) for c in np.cumsum(SPLIT_SIZES)[:-1])

kernel_name = 'hybrid_mla_fox_chunkrel_gated_encoder'


def rms_norm(x, g):
    xf = x.astype(jnp.float32)
    y = xf * lax.rsqrt(jnp.mean(xf * xf, axis=-1, keepdims=True) + EPS)
    return (y * g.astype(jnp.float32)).astype(x.dtype)


def rope_tables(seq):
    pos = jnp.arange(seq, dtype=jnp.float32)
    inv = ROPE_THETA ** (-jnp.arange(0, MLA_ROPE, 2, dtype=jnp.float32) / MLA_ROPE)
    ang = pos[:, None] * inv[None, :]
    return jnp.cos(ang), jnp.sin(ang)


def apply_rope(x, cos, sin):
    x1, x2 = jnp.split(x, 2, axis=-1)
    c = cos.astype(x.dtype)[None, :, None, :]
    s = sin.astype(x.dtype)[None, :, None, :]
    return jnp.concatenate([x1 * c - x2 * s, x1 * s + x2 * c], axis=-1)


def block_sweep_attention(q, k, v, cum=None):
    B, S, H, Dk = q.shape
    scale = Dk ** -0.5
    nq = S // Q_BLOCK
    qb = jnp.moveaxis(q.reshape(B, nq, Q_BLOCK, H, Dk), 1, 0)
    k_pos = jnp.arange(S)
    if cum is None:
        xs = (jnp.arange(nq), qb)
    else:
        cb = jnp.moveaxis(cum.reshape(B, nq, Q_BLOCK, H), 1, 0)
        cum_k = jnp.transpose(cum, (0, 2, 1))[:, :, None, :]
        xs = (jnp.arange(nq), qb, cb)

    def body(xs_):
        i, q_i = xs_[0], xs_[1]
        q_pos = i * Q_BLOCK + jnp.arange(Q_BLOCK)
        s = jnp.einsum('bqhd,bkhd->bhqk', q_i, k).astype(jnp.float32) * scale
        if cum is None:
            allowed = (k_pos // CHUNK)[None, :] <= (q_pos // CHUNK)[:, None]
        else:
            allowed = k_pos[None, :] <= q_pos[:, None]
            s = s + jnp.transpose(xs_[2], (0, 2, 1))[..., None] - cum_k
        s = jnp.where(allowed[None, None], s, NEG)
        p = jax.nn.softmax(s, axis=-1).astype(v.dtype)
        return jnp.einsum('bhqk,bkhd->bqhd', p, v)

    out = lax.map(body, xs)
    return jnp.moveaxis(out, 0, 1).reshape(B, S, H, v.shape[-1])


def mla_branch(c_q_raw, c_kv_raw, k_r_raw, g_cq, w_uq, g_ckv, w_ukv, g_qn, g_kn, cos, sin):
    B, S, _ = c_q_raw.shape
    q = (rms_norm(c_q_raw, g_cq) @ w_uq).reshape(B, S, MLA_HEADS, MLA_QK)
    kv = (rms_norm(c_kv_raw, g_ckv) @ w_ukv).reshape(B, S, MLA_HEADS, MLA_NOPE + MLA_V)
    k_nope, v = kv[..., :MLA_NOPE], kv[..., MLA_NOPE:]
    k_rope = jnp.broadcast_to(k_r_raw[:, :, None, :], (B, S, MLA_HEADS, MLA_ROPE))
    k = jnp.concatenate([k_nope, k_rope], axis=-1)
    q = rms_norm(q, g_qn)
    k = rms_norm(k, g_kn)
    q = jnp.concatenate([q[..., :MLA_NOPE], apply_rope(q[..., MLA_NOPE:], cos, sin)], axis=-1)
    k = jnp.concatenate([k[..., :MLA_NOPE], apply_rope(k[..., MLA_NOPE:], cos, sin)], axis=-1)
    o = block_sweep_attention(q, k, v)
    return o.reshape(B, S, MIX_W)


def fox_branch(qkv, f_logit, b_f, g_qn, g_kn):
    B, S, _ = qkv.shape
    qkv = qkv.reshape(B, S, 3, FOX_HEADS, FOX_DH)
    q = rms_norm(qkv[:, :, 0], g_qn)
    k = rms_norm(qkv[:, :, 1], g_kn)
    v = qkv[:, :, 2]
    log_f = jax.nn.log_sigmoid(f_logit.astype(jnp.float32) + b_f.astype(jnp.float32))
    cum = jnp.cumsum(log_f, axis=1)
    o = block_sweep_attention(q, k, v, cum)
    return o.reshape(B, S, MIX_W)


def chunk_band_branch(qkv, rel_bias, g_qn, g_kn):
    B, S, _ = qkv.shape
    n_chunks = S // CHUNK
    pad = LEFT_CHUNKS * CHUNK
    qkv = qkv.reshape(B, S, 3, CH_HEADS, CH_DH)
    q = rms_norm(qkv[:, :, 0], g_qn)
    k = rms_norm(qkv[:, :, 1], g_kn)
    v = qkv[:, :, 2]
    band_idx = (jnp.arange(n_chunks) * CHUNK)[:, None] + jnp.arange(BAND)[None, :]
    kp = jnp.pad(k, ((0, 0), (pad, 0), (0, 0), (0, 0)))
    vp = jnp.pad(v, ((0, 0), (pad, 0), (0, 0), (0, 0)))
    kb = kp[:, band_idx]
    vb = vp[:, band_idx]
    qc = q.reshape(B, n_chunks, CHUNK, CH_HEADS, CH_DH)
    s = jnp.einsum('bcqhd,bckhd->bchqk', qc, kb).astype(jnp.float32) * (CH_DH ** -0.5)
    rel = (jnp.arange(CHUNK)[:, None] + pad) - jnp.arange(BAND)[None, :]
    bias = rel_bias.astype(jnp.float32)[:, jnp.clip(rel, -REL_CLIP, REL_CLIP) + REL_CLIP]
    valid = band_idx >= pad
    s = jnp.where(valid[None, :, None, None, :], s + bias[None, None], NEG)
    p = jax.nn.softmax(s, axis=-1).astype(v.dtype)
    o = jnp.einsum('bchqk,bckhd->bcqhd', p, vb)
    return o.reshape(B, S, MIX_W)


def memory_cross_attention(h, mem_n, w_xq, w_xkv, g_qn, g_kn, w_xo):
    B, S, _ = h.shape
    M = mem_n.shape[1]
    q = rms_norm((h @ w_xq).reshape(B, S, X_HEADS, X_DH), g_qn)
    kv = (mem_n @ w_xkv).reshape(B, M, 2, X_HEADS, X_DH)
    k = rms_norm(kv[:, :, 0], g_kn)
    v = kv[:, :, 1]
    s = jnp.einsum('bshd,bmhd->bhsm', q, k).astype(jnp.float32) * (X_DH ** -0.5)
    p = jax.nn.softmax(s, axis=-1).astype(v.dtype)
    o = jnp.einsum('bhsm,bmhd->bshd', p, v).reshape(B, S, X_HEADS * X_DH)
    return o @ w_xo


def _fwd_setup_inputs(seed: int = 0) -> dict:
    key = jax.random.key(seed)
    ks = jax.random.split(key, 32)
    f32 = jnp.float32

    def nrm(k, shape, scale):
        return jax.random.normal(k, shape, f32) * scale

    def gain(k, shape):
        return 1.0 + 0.02 * jax.random.normal(k, shape, f32)

    L = DEPTH
    return {
        'x': nrm(ks[0], (BATCH, SEQ, D_MODEL), 1.0),
        'mem': nrm(ks[1], (BATCH, MEM_LEN, D_MODEL), 1.0),
        'g_mix': gain(ks[2], (L, D_MODEL)),
        'w_in': nrm(ks[3], (L, D_MODEL, D_IN), D_MODEL ** -0.5),
        'g_cq': gain(ks[4], (L, Q_LORA)),
        'w_uq': nrm(ks[5], (L, Q_LORA, MLA_HEADS * MLA_QK), Q_LORA ** -0.5),
        'g_ckv': gain(ks[6], (L, KV_LORA)),
        'w_ukv': nrm(ks[7], (L, KV_LORA, MLA_HEADS * (MLA_NOPE + MLA_V)), KV_LORA ** -0.5),
        'g_mla_q': gain(ks[8], (L, MLA_QK)),
        'g_mla_k': gain(ks[9], (L, MLA_QK)),
        'b_f': 3.0 + 0.1 * jax.random.normal(ks[10], (L, FOX_HEADS), f32),
        'g_fox_q': gain(ks[11], (L, FOX_DH)),
        'g_fox_k': gain(ks[12], (L, FOX_DH)),
        'rel_bias': nrm(ks[13], (L, CH_HEADS, N_REL), 0.5),
        'g_ch_q': gain(ks[14], (L, CH_DH)),
        'g_ch_k': gain(ks[15], (L, CH_DH)),
        'w_br': nrm(ks[16], (L, N_BRANCH, MIX_W, D_MODEL), MIX_W ** -0.5),
        'w_out': nrm(ks[17], (L, D_MODEL, D_MODEL), D_MODEL ** -0.5),
        'g_cross': gain(ks[18], (L, D_MODEL)),
        'g_mem': gain(ks[19], (L, D_MODEL)),
        'w_xq': nrm(ks[20], (L, D_MODEL, X_HEADS * X_DH), D_MODEL ** -0.5),
        'w_xkv': nrm(ks[21], (L, D_MODEL, 2 * X_HEADS * X_DH), D_MODEL ** -0.5),
        'g_x_q': gain(ks[22], (L, X_DH)),
        'g_x_k': gain(ks[23], (L, X_DH)),
        'w_xo': nrm(ks[24], (L, X_HEADS * X_DH, D_MODEL), (X_HEADS * X_DH) ** -0.5),
        'g_mlp': gain(ks[25], (L, D_MODEL)),
        'w_1': nrm(ks[26], (L, D_MODEL, D_FF), D_MODEL ** -0.5),
        'w_2': nrm(ks[27], (L, D_FF, D_MODEL), D_FF ** -0.5),
    }


def _fwd_reference(x, mem, g_mix, w_in, g_cq, w_uq, g_ckv, w_ukv, g_mla_q, g_mla_k, b_f,
              g_fox_q, g_fox_k, rel_bias, g_ch_q, g_ch_k, w_br, w_out, g_cross, g_mem,
              w_xq, w_xkv, g_x_q, g_x_k, w_xo, g_mlp, w_1, w_2):
    B, S, _ = x.shape
    cos, sin = rope_tables(S)
    for l in range(DEPTH):
        h = rms_norm(x, g_mix[l])
        z = h @ w_in[l]
        c_q, c_kv, k_r, fox_qkv, fox_f, ch_qkv, gate_logits = jnp.split(z, SPLIT_CUTS, axis=-1)
        y_a = mla_branch(c_q, c_kv, k_r, g_cq[l], w_uq[l], g_ckv[l], w_ukv[l],
                         g_mla_q[l], g_mla_k[l], cos, sin)
        y_b = fox_branch(fox_qkv, fox_f, b_f[l], g_fox_q[l], g_fox_k[l])
        y_c = chunk_band_branch(ch_qkv, rel_bias[l], g_ch_q[l], g_ch_k[l])
        ys = jnp.stack([y_a, y_b, y_c], axis=2)
        proj = jnp.einsum('bsnc,ncd->bsnd', ys, w_br[l])
        gates = jax.nn.sigmoid(gate_logits.astype(jnp.float32)).astype(x.dtype)
        gates = gates.reshape(B, S, N_BRANCH, D_MODEL)
        merged = jnp.einsum('bsnd,bsnd->bsd', gates, proj)
        x = x + merged @ w_out[l]
        x = x + memory_cross_attention(rms_norm(x, g_cross[l]), rms_norm(mem, g_mem[l]),
                                       w_xq[l], w_xkv[l], g_x_q[l], g_x_k[l], w_xo[l])
        hm = rms_norm(x, g_mlp[l])
        x = x + jnp.square(jax.nn.relu(hm @ w_1[l])) @ w_2[l]
    return x


import jax as _jax
import jax.numpy as _jnp

TWIN_FORMAT = 'train_step'
FWD_PARAMS = ['x', 'mem', 'g_mix', 'w_in', 'g_cq', 'w_uq', 'g_ckv', 'w_ukv', 'g_mla_q', 'g_mla_k', 'b_f', 'g_fox_q', 'g_fox_k', 'rel_bias', 'g_ch_q', 'g_ch_k', 'w_br', 'w_out', 'g_cross', 'g_mem', 'w_xq', 'w_xkv', 'g_x_q', 'g_x_k', 'w_xo', 'g_mlp', 'w_1', 'w_2']
TWIN_WEIGHTS = ['g_mix', 'w_in', 'g_cq', 'w_uq', 'g_ckv', 'w_ukv', 'g_mla_q', 'g_mla_k', 'b_f', 'g_fox_q', 'g_fox_k', 'rel_bias', 'g_ch_q', 'g_ch_k', 'w_br', 'w_out', 'g_cross', 'g_mem', 'w_xq', 'w_xkv', 'g_x_q', 'g_x_k', 'w_xo', 'g_mlp', 'w_1', 'w_2']
TWIN_DIFF_INPUT = 'x'
TWIN_INPUTS = ['x', 'mem', 'g_mix', 'w_in', 'g_cq', 'w_uq', 'g_ckv', 'w_ukv', 'g_mla_q', 'g_mla_k', 'b_f', 'g_fox_q', 'g_fox_k', 'rel_bias', 'g_ch_q', 'g_ch_k', 'w_br', 'w_out', 'g_cross', 'g_mem', 'w_xq', 'w_xkv', 'g_x_q', 'g_x_k', 'w_xo', 'g_mlp', 'w_1', 'w_2', 'loss_target', 'm_g_mix', 'm_w_in', 'm_g_cq', 'm_w_uq', 'm_g_ckv', 'm_w_ukv', 'm_g_mla_q', 'm_g_mla_k', 'm_b_f', 'm_g_fox_q', 'm_g_fox_k', 'm_rel_bias', 'm_g_ch_q', 'm_g_ch_k', 'm_w_br', 'm_w_out', 'm_g_cross', 'm_g_mem', 'm_w_xq', 'm_w_xkv', 'm_g_x_q', 'm_g_x_k', 'm_w_xo', 'm_g_mlp', 'm_w_1', 'm_w_2', 'v_g_mix', 'v_w_in', 'v_g_cq', 'v_w_uq', 'v_g_ckv', 'v_w_ukv', 'v_g_mla_q', 'v_g_mla_k', 'v_b_f', 'v_g_fox_q', 'v_g_fox_k', 'v_rel_bias', 'v_g_ch_q', 'v_g_ch_k', 'v_w_br', 'v_w_out', 'v_g_cross', 'v_g_mem', 'v_w_xq', 'v_w_xkv', 'v_g_x_q', 'v_g_x_k', 'v_w_xo', 'v_g_mlp', 'v_w_1', 'v_w_2']
TWIN_OUTPUTS = ['loss', 'grad_x', 'grad_g_mix', 'grad_w_in', 'grad_g_cq', 'grad_w_uq', 'grad_g_ckv', 'grad_w_ukv', 'grad_g_mla_q', 'grad_g_mla_k', 'grad_b_f', 'grad_g_fox_q', 'grad_g_fox_k', 'grad_rel_bias', 'grad_g_ch_q', 'grad_g_ch_k', 'grad_w_br', 'grad_w_out', 'grad_g_cross', 'grad_g_mem', 'grad_w_xq', 'grad_w_xkv', 'grad_g_x_q', 'grad_g_x_k', 'grad_w_xo', 'grad_g_mlp', 'grad_w_1', 'grad_w_2', 'delta_g_mix', 'delta_w_in', 'delta_g_cq', 'delta_w_uq', 'delta_g_ckv', 'delta_w_ukv', 'delta_g_mla_q', 'delta_g_mla_k', 'delta_b_f', 'delta_g_fox_q', 'delta_g_fox_k', 'delta_rel_bias', 'delta_g_ch_q', 'delta_g_ch_k', 'delta_w_br', 'delta_w_out', 'delta_g_cross', 'delta_g_mem', 'delta_w_xq', 'delta_w_xkv', 'delta_g_x_q', 'delta_g_x_k', 'delta_w_xo', 'delta_g_mlp', 'delta_w_1', 'delta_w_2', 'new_m_g_mix', 'new_m_w_in', 'new_m_g_cq', 'new_m_w_uq', 'new_m_g_ckv', 'new_m_w_ukv', 'new_m_g_mla_q', 'new_m_g_mla_k', 'new_m_b_f', 'new_m_g_fox_q', 'new_m_g_fox_k', 'new_m_rel_bias', 'new_m_g_ch_q', 'new_m_g_ch_k', 'new_m_w_br', 'new_m_w_out', 'new_m_g_cross', 'new_m_g_mem', 'new_m_w_xq', 'new_m_w_xkv', 'new_m_g_x_q', 'new_m_g_x_k', 'new_m_w_xo', 'new_m_g_mlp', 'new_m_w_1', 'new_m_w_2', 'new_v_g_mix', 'new_v_w_in', 'new_v_g_cq', 'new_v_w_uq', 'new_v_g_ckv', 'new_v_w_ukv', 'new_v_g_mla_q', 'new_v_g_mla_k', 'new_v_b_f', 'new_v_g_fox_q', 'new_v_g_fox_k', 'new_v_rel_bias', 'new_v_g_ch_q', 'new_v_g_ch_k', 'new_v_w_br', 'new_v_w_out', 'new_v_g_cross', 'new_v_g_mem', 'new_v_w_xq', 'new_v_w_xkv', 'new_v_g_x_q', 'new_v_g_x_k', 'new_v_w_xo', 'new_v_g_mlp', 'new_v_w_1', 'new_v_w_2']
TWIN_LEAF_KINDS = {'loss': 'loss', 'grad_x': 'grad_x', 'grad_g_mix': 'grad_w', 'grad_w_in': 'grad_w', 'grad_g_cq': 'grad_w', 'grad_w_uq': 'grad_w', 'grad_g_ckv': 'grad_w', 'grad_w_ukv': 'grad_w', 'grad_g_mla_q': 'grad_w', 'grad_g_mla_k': 'grad_w', 'grad_b_f': 'grad_w', 'grad_g_fox_q': 'grad_w', 'grad_g_fox_k': 'grad_w', 'grad_rel_bias': 'grad_w', 'grad_g_ch_q': 'grad_w', 'grad_g_ch_k': 'grad_w', 'grad_w_br': 'grad_w', 'grad_w_out': 'grad_w', 'grad_g_cross': 'grad_w', 'grad_g_mem': 'grad_w', 'grad_w_xq': 'grad_w', 'grad_w_xkv': 'grad_w', 'grad_g_x_q': 'grad_w', 'grad_g_x_k': 'grad_w', 'grad_w_xo': 'grad_w', 'grad_g_mlp': 'grad_w', 'grad_w_1': 'grad_w', 'grad_w_2': 'grad_w', 'delta_g_mix': 'delta_w', 'delta_w_in': 'delta_w', 'delta_g_cq': 'delta_w', 'delta_w_uq': 'delta_w', 'delta_g_ckv': 'delta_w', 'delta_w_ukv': 'delta_w', 'delta_g_mla_q': 'delta_w', 'delta_g_mla_k': 'delta_w', 'delta_b_f': 'delta_w', 'delta_g_fox_q': 'delta_w', 'delta_g_fox_k': 'delta_w', 'delta_rel_bias': 'delta_w', 'delta_g_ch_q': 'delta_w', 'delta_g_ch_k': 'delta_w', 'delta_w_br': 'delta_w', 'delta_w_out': 'delta_w', 'delta_g_cross': 'delta_w', 'delta_g_mem': 'delta_w', 'delta_w_xq': 'delta_w', 'delta_w_xkv': 'delta_w', 'delta_g_x_q': 'delta_w', 'delta_g_x_k': 'delta_w', 'delta_w_xo': 'delta_w', 'delta_g_mlp': 'delta_w', 'delta_w_1': 'delta_w', 'delta_w_2': 'delta_w', 'new_m_g_mix': 'new_m', 'new_m_w_in': 'new_m', 'new_m_g_cq': 'new_m', 'new_m_w_uq': 'new_m', 'new_m_g_ckv': 'new_m', 'new_m_w_ukv': 'new_m', 'new_m_g_mla_q': 'new_m', 'new_m_g_mla_k': 'new_m', 'new_m_b_f': 'new_m', 'new_m_g_fox_q': 'new_m', 'new_m_g_fox_k': 'new_m', 'new_m_rel_bias': 'new_m', 'new_m_g_ch_q': 'new_m', 'new_m_g_ch_k': 'new_m', 'new_m_w_br': 'new_m', 'new_m_w_out': 'new_m', 'new_m_g_cross': 'new_m', 'new_m_g_mem': 'new_m', 'new_m_w_xq': 'new_m', 'new_m_w_xkv': 'new_m', 'new_m_g_x_q': 'new_m', 'new_m_g_x_k': 'new_m', 'new_m_w_xo': 'new_m', 'new_m_g_mlp': 'new_m', 'new_m_w_1': 'new_m', 'new_m_w_2': 'new_m', 'new_v_g_mix': 'new_v', 'new_v_w_in': 'new_v', 'new_v_g_cq': 'new_v', 'new_v_w_uq': 'new_v', 'new_v_g_ckv': 'new_v', 'new_v_w_ukv': 'new_v', 'new_v_g_mla_q': 'new_v', 'new_v_g_mla_k': 'new_v', 'new_v_b_f': 'new_v', 'new_v_g_fox_q': 'new_v', 'new_v_g_fox_k': 'new_v', 'new_v_rel_bias': 'new_v', 'new_v_g_ch_q': 'new_v', 'new_v_g_ch_k': 'new_v', 'new_v_w_br': 'new_v', 'new_v_w_out': 'new_v', 'new_v_g_cross': 'new_v', 'new_v_g_mem': 'new_v', 'new_v_w_xq': 'new_v', 'new_v_w_xkv': 'new_v', 'new_v_g_x_q': 'new_v', 'new_v_g_x_k': 'new_v', 'new_v_w_xo': 'new_v', 'new_v_g_mlp': 'new_v', 'new_v_w_1': 'new_v', 'new_v_w_2': 'new_v'}


def _forward(args):
    return _fwd_reference(*[args[k] for k in FWD_PARAMS])


def _output_shape():
    out = _jax.eval_shape(lambda: _forward(_fwd_setup_inputs(0)))
    return out.shape, out.dtype

N_MICROBATCH = 1
ADAM_LR = 0.001
ADAM_B1 = 0.9
ADAM_B2 = 0.999
ADAM_EPS = 1e-08
ADAM_WD = 0.01
ADAM_STEP = 10
PER_EXAMPLE_BATCH_AXIS = {'x': 0, 'mem': 0, 'loss_target': 0}
SHARED_INPUTS = []
_WEIGHT_DTYPES = {'g_mix': _jnp.float32, 'w_in': _jnp.float32, 'g_cq': _jnp.float32, 'w_uq': _jnp.float32, 'g_ckv': _jnp.float32, 'w_ukv': _jnp.float32, 'g_mla_q': _jnp.float32, 'g_mla_k': _jnp.float32, 'b_f': _jnp.float32, 'g_fox_q': _jnp.float32, 'g_fox_k': _jnp.float32, 'rel_bias': _jnp.float32, 'g_ch_q': _jnp.float32, 'g_ch_k': _jnp.float32, 'w_br': _jnp.float32, 'w_out': _jnp.float32, 'g_cross': _jnp.float32, 'g_mem': _jnp.float32, 'w_xq': _jnp.float32, 'w_xkv': _jnp.float32, 'g_x_q': _jnp.float32, 'g_x_k': _jnp.float32, 'w_xo': _jnp.float32, 'g_mlp': _jnp.float32, 'w_1': _jnp.float32, 'w_2': _jnp.float32}
MOMENT_SCALE = {'g_mix': 4.967250e+00, 'w_in': 1.849998e+00, 'g_cq': 2.042616e-01, 'w_uq': 1.227414e-01, 'g_ckv': 8.694775e+00, 'w_ukv': 2.654862e+00, 'g_mla_q': 3.543035e-01, 'g_mla_k': 3.556621e-01, 'b_f': 1.435973e+01, 'g_fox_q': 1.117499e+00, 'g_fox_k': 1.115317e+00, 'rel_bias': 3.826637e-02, 'g_ch_q': 2.271098e-01, 'g_ch_k': 2.273907e-01, 'w_br': 2.622751e+00, 'w_out': 4.511875e+00, 'g_cross': 2.153061e-01, 'g_mem': 7.212420e-01, 'w_xq': 4.052155e-01, 'w_xkv': 9.990326e-01, 'g_x_q': 1.285292e+00, 'g_x_k': 1.285837e+00, 'w_xo': 6.552973e-01, 'g_mlp': 2.536520e+01, 'w_1': 2.666790e+00, 'w_2': 9.237594e+00}


def _to_microbatches(a, axis):
    t = _jnp.moveaxis(a, axis, 0)
    t = t.reshape((N_MICROBATCH, t.shape[0] // N_MICROBATCH) + t.shape[1:])
    return _jnp.moveaxis(t, 1, axis + 1)


def setup_inputs(seed: int = 0) -> dict:
    inp = _fwd_setup_inputs(seed)
    key = _jax.random.fold_in(_jax.random.key(seed), 7919)
    shape, _ = _output_shape()
    out = dict(inp)
    out["loss_target"] = _jax.random.normal(_jax.random.fold_in(key, 0), shape, _jnp.float32)
    for i, name in enumerate(TWIN_WEIGHTS):
        w = inp[name].astype(_jnp.float32)
        if MOMENT_SCALE is None:
            s = _jnp.sqrt(_jnp.mean(_jnp.square(w)) + 1e-30)
        else:
            s = MOMENT_SCALE[name]
        km, kv = _jax.random.split(_jax.random.fold_in(key, i + 1))
        out[name] = w
        out["m_" + name] = s * _jax.random.normal(km, w.shape, _jnp.float32)
        out["v_" + name] = (s * s) * _jax.random.uniform(kv, w.shape, _jnp.float32, 0.5, 1.5)
    if N_MICROBATCH > 1:
        for name, axis in PER_EXAMPLE_BATCH_AXIS.items():
            out[name] = _to_microbatches(out[name], axis)
    return {'x': out['x'], 'mem': out['mem'], 'g_mix': out['g_mix'], 'w_in': out['w_in'], 'g_cq': out['g_cq'], 'w_uq': out['w_uq'], 'g_ckv': out['g_ckv'], 'w_ukv': out['w_ukv'], 'g_mla_q': out['g_mla_q'], 'g_mla_k': out['g_mla_k'], 'b_f': out['b_f'], 'g_fox_q': out['g_fox_q'], 'g_fox_k': out['g_fox_k'], 'rel_bias': out['rel_bias'], 'g_ch_q': out['g_ch_q'], 'g_ch_k': out['g_ch_k'], 'w_br': out['w_br'], 'w_out': out['w_out'], 'g_cross': out['g_cross'], 'g_mem': out['g_mem'], 'w_xq': out['w_xq'], 'w_xkv': out['w_xkv'], 'g_x_q': out['g_x_q'], 'g_x_k': out['g_x_k'], 'w_xo': out['w_xo'], 'g_mlp': out['g_mlp'], 'w_1': out['w_1'], 'w_2': out['w_2'], 'loss_target': out['loss_target'], 'm_g_mix': out['m_g_mix'], 'm_w_in': out['m_w_in'], 'm_g_cq': out['m_g_cq'], 'm_w_uq': out['m_w_uq'], 'm_g_ckv': out['m_g_ckv'], 'm_w_ukv': out['m_w_ukv'], 'm_g_mla_q': out['m_g_mla_q'], 'm_g_mla_k': out['m_g_mla_k'], 'm_b_f': out['m_b_f'], 'm_g_fox_q': out['m_g_fox_q'], 'm_g_fox_k': out['m_g_fox_k'], 'm_rel_bias': out['m_rel_bias'], 'm_g_ch_q': out['m_g_ch_q'], 'm_g_ch_k': out['m_g_ch_k'], 'm_w_br': out['m_w_br'], 'm_w_out': out['m_w_out'], 'm_g_cross': out['m_g_cross'], 'm_g_mem': out['m_g_mem'], 'm_w_xq': out['m_w_xq'], 'm_w_xkv': out['m_w_xkv'], 'm_g_x_q': out['m_g_x_q'], 'm_g_x_k': out['m_g_x_k'], 'm_w_xo': out['m_w_xo'], 'm_g_mlp': out['m_g_mlp'], 'm_w_1': out['m_w_1'], 'm_w_2': out['m_w_2'], 'v_g_mix': out['v_g_mix'], 'v_w_in': out['v_w_in'], 'v_g_cq': out['v_g_cq'], 'v_w_uq': out['v_w_uq'], 'v_g_ckv': out['v_g_ckv'], 'v_w_ukv': out['v_w_ukv'], 'v_g_mla_q': out['v_g_mla_q'], 'v_g_mla_k': out['v_g_mla_k'], 'v_b_f': out['v_b_f'], 'v_g_fox_q': out['v_g_fox_q'], 'v_g_fox_k': out['v_g_fox_k'], 'v_rel_bias': out['v_rel_bias'], 'v_g_ch_q': out['v_g_ch_q'], 'v_g_ch_k': out['v_g_ch_k'], 'v_w_br': out['v_w_br'], 'v_w_out': out['v_w_out'], 'v_g_cross': out['v_g_cross'], 'v_g_mem': out['v_g_mem'], 'v_w_xq': out['v_w_xq'], 'v_w_xkv': out['v_w_xkv'], 'v_g_x_q': out['v_g_x_q'], 'v_g_x_k': out['v_g_x_k'], 'v_w_xo': out['v_w_xo'], 'v_g_mlp': out['v_g_mlp'], 'v_w_1': out['v_w_1'], 'v_w_2': out['v_w_2']}


def _loss(weights, diff, rest, loss_target):
    with _jax.named_scope("forward"):
        args = {**rest, TWIN_DIFF_INPUT: diff, **{k: w.astype(_WEIGHT_DTYPES[k]) for k, w in weights.items()}}
        y = _forward(args)
    with _jax.named_scope("loss_head"):
        err = _jnp.square(y.astype(_jnp.float32) - loss_target)
        return 0.5 * _jnp.sum(_jnp.mean(err, axis=-1)) if err.ndim else 0.5 * err


def _adamw(w, g, m, v):
    m = ADAM_B1 * m + (1.0 - ADAM_B1) * g
    v = ADAM_B2 * v + (1.0 - ADAM_B2) * _jnp.square(g)
    m_hat = m / (1.0 - ADAM_B1 ** ADAM_STEP)
    v_hat = v / (1.0 - ADAM_B2 ** ADAM_STEP)
    delta = -ADAM_LR * (m_hat / (_jnp.sqrt(v_hat) + ADAM_EPS) + ADAM_WD * w)
    return delta, m, v


def reference(x, mem, g_mix, w_in, g_cq, w_uq, g_ckv, w_ukv, g_mla_q, g_mla_k, b_f, g_fox_q, g_fox_k, rel_bias, g_ch_q, g_ch_k, w_br, w_out, g_cross, g_mem, w_xq, w_xkv, g_x_q, g_x_k, w_xo, g_mlp, w_1, w_2, loss_target, m_g_mix, m_w_in, m_g_cq, m_w_uq, m_g_ckv, m_w_ukv, m_g_mla_q, m_g_mla_k, m_b_f, m_g_fox_q, m_g_fox_k, m_rel_bias, m_g_ch_q, m_g_ch_k, m_w_br, m_w_out, m_g_cross, m_g_mem, m_w_xq, m_w_xkv, m_g_x_q, m_g_x_k, m_w_xo, m_g_mlp, m_w_1, m_w_2, v_g_mix, v_w_in, v_g_cq, v_w_uq, v_g_ckv, v_w_ukv, v_g_mla_q, v_g_mla_k, v_b_f, v_g_fox_q, v_g_fox_k, v_rel_bias, v_g_ch_q, v_g_ch_k, v_w_br, v_w_out, v_g_cross, v_g_mem, v_w_xq, v_w_xkv, v_g_x_q, v_g_x_k, v_w_xo, v_g_mlp, v_w_1, v_w_2):
    given = dict(x=x, mem=mem, g_mix=g_mix, w_in=w_in, g_cq=g_cq, w_uq=w_uq, g_ckv=g_ckv, w_ukv=w_ukv, g_mla_q=g_mla_q, g_mla_k=g_mla_k, b_f=b_f, g_fox_q=g_fox_q, g_fox_k=g_fox_k, rel_bias=rel_bias, g_ch_q=g_ch_q, g_ch_k=g_ch_k, w_br=w_br, w_out=w_out, g_cross=g_cross, g_mem=g_mem, w_xq=w_xq, w_xkv=w_xkv, g_x_q=g_x_q, g_x_k=g_x_k, w_xo=w_xo, g_mlp=g_mlp, w_1=w_1, w_2=w_2, loss_target=loss_target, m_g_mix=m_g_mix, m_w_in=m_w_in, m_g_cq=m_g_cq, m_w_uq=m_w_uq, m_g_ckv=m_g_ckv, m_w_ukv=m_w_ukv, m_g_mla_q=m_g_mla_q, m_g_mla_k=m_g_mla_k, m_b_f=m_b_f, m_g_fox_q=m_g_fox_q, m_g_fox_k=m_g_fox_k, m_rel_bias=m_rel_bias, m_g_ch_q=m_g_ch_q, m_g_ch_k=m_g_ch_k, m_w_br=m_w_br, m_w_out=m_w_out, m_g_cross=m_g_cross, m_g_mem=m_g_mem, m_w_xq=m_w_xq, m_w_xkv=m_w_xkv, m_g_x_q=m_g_x_q, m_g_x_k=m_g_x_k, m_w_xo=m_w_xo, m_g_mlp=m_g_mlp, m_w_1=m_w_1, m_w_2=m_w_2, v_g_mix=v_g_mix, v_w_in=v_w_in, v_g_cq=v_g_cq, v_w_uq=v_w_uq, v_g_ckv=v_g_ckv, v_w_ukv=v_w_ukv, v_g_mla_q=v_g_mla_q, v_g_mla_k=v_g_mla_k, v_b_f=v_b_f, v_g_fox_q=v_g_fox_q, v_g_fox_k=v_g_fox_k, v_rel_bias=v_rel_bias, v_g_ch_q=v_g_ch_q, v_g_ch_k=v_g_ch_k, v_w_br=v_w_br, v_w_out=v_w_out, v_g_cross=v_g_cross, v_g_mem=v_g_mem, v_w_xq=v_w_xq, v_w_xkv=v_w_xkv, v_g_x_q=v_g_x_q, v_g_x_k=v_g_x_k, v_w_xo=v_w_xo, v_g_mlp=v_g_mlp, v_w_1=v_w_1, v_w_2=v_w_2)
    weights = {n: given[n] for n in TWIN_WEIGHTS}
    shared = {n: given[n] for n in SHARED_INPUTS}
    per_example = {n: given[n] for n in ['x', 'mem']}
    grad_fn = _jax.value_and_grad(_loss, argnums=(0, 1))

    def one_microbatch(ex, loss_target):
        ex = dict(ex)
        diff = ex.pop(TWIN_DIFF_INPUT)
        return grad_fn(weights, diff, {**shared, **ex}, loss_target)

    if N_MICROBATCH == 1:
        loss, (grad_w, grad_x) = one_microbatch(per_example, given["loss_target"])
    else:
        def body(carry, xs):
            loss_sum, grad_sum = carry
            l_k, (gw_k, gx_k) = one_microbatch(xs[0], xs[1])
            with _jax.named_scope("update"):
                return (loss_sum + l_k, _jax.tree.map(_jnp.add, grad_sum, gw_k)), gx_k

        init = (_jnp.zeros((), _jnp.float32), _jax.tree.map(_jnp.zeros_like, weights))
        (loss, grad_w), grad_x = _jax.lax.scan(body, init, (per_example, given["loss_target"]))
    with _jax.named_scope("update"):
        delta_w, new_m, new_v = {}, {}, {}
        for n in TWIN_WEIGHTS:
            delta_w[n], new_m[n], new_v[n] = _adamw(weights[n], grad_w[n], given["m_" + n], given["v_" + n])
    return (loss, grad_x, *[grad_w[n] for n in TWIN_WEIGHTS], *[delta_w[n] for n in TWIN_WEIGHTS],
            *[new_m[n] for n in TWIN_WEIGHTS], *[new_v[n] for n in TWIN_WEIGHTS])
```

```python
import functools

import numpy as np
import jax
import jax.numpy as jnp
from jax import lax
from jax.experimental import pallas as pl
from jax.experimental.pallas import tpu as pltpu

F32 = jnp.float32
BF16 = jnp.bfloat16
MXU_DTYPE = jnp.bfloat16
MESH = pl.DeviceIdType.MESH

D_MODEL = 2048
MIX_W = 1024
N_HEADS = 8
DH = 128
MLA_NOPE = 128
MLA_ROPE = 64
MLA_QK = MLA_NOPE + MLA_ROPE
MLA_PAD = 256
Q_LORA = 512
KV_LORA = 256
CHUNK = 64
LEFT_CHUNKS = 8
REL_CLIP = 128
N_REL = 2 * REL_CLIP + 1
X_HEADS = 4
MEM_LEN = 256
D_FF = 8192
ROPE_THETA = 10000.0
EPS = 1e-6
NEG = -1e30

Z_CQ, Z_CKV, Z_KR, Z_FF = 0, 512, 768, 896
ZS_W = 1024
Z_FOX = 1024
Z_CH = Z_FOX + 3 * MIX_W
Z_GATE = Z_CH + 3 * MIX_W
Z_TOT = Z_GATE + 3 * D_MODEL
W_IN_CUTS = (0, 512, 768, 832, 3904, 3912, 6984, 13128)

ADAM_LR, ADAM_B1, ADAM_B2, ADAM_EPS, ADAM_WD, ADAM_STEP = 0.001, 0.9, 0.999, 1e-08, 0.01, 10

VMEM_LIMIT_BYTES = 56 * 1024 * 1024
PACK_W = 512
PACK_ORDER = ("w_uq", "w_ukv", "w_br", "w_out", "w_xq", "w_xkv", "w_xo", "w_1", "w_2", "w_in")
SMALL_ORDER = ("g_mix", "g_cq", "g_ckv", "g_mla_q", "g_mla_k", "b_f", "g_fox_q", "g_fox_k", "rel_bias",
               "g_ch_q", "g_ch_k", "g_cross", "g_mem", "g_x_q", "g_x_k", "g_mlp")
WEIGHT_ORDER = ("g_mix", "w_in", "g_cq", "w_uq", "g_ckv", "w_ukv", "g_mla_q", "g_mla_k", "b_f", "g_fox_q",
                "g_fox_k", "rel_bias", "g_ch_q", "g_ch_k", "w_br", "w_out", "g_cross", "g_mem", "w_xq",
                "w_xkv", "g_x_q", "g_x_k", "w_xo", "g_mlp", "w_1", "w_2")


def _pick(n, prefs):
    for p in prefs:
        if n % p == 0:
            return p
    raise ValueError(f"no block size among {prefs} divides {n}")


def _pcall(body, *, name, out_shape, in_specs, out_specs, grid=(), scratch=(), aliases=None):
    return pl.pallas_call(
        body, out_shape=out_shape, grid=grid, in_specs=in_specs, out_specs=out_specs,
        scratch_shapes=scratch, name=name, interpret=False,
        input_output_aliases=aliases or {},
        compiler_params=pltpu.CompilerParams(vmem_limit_bytes=VMEM_LIMIT_BYTES))


def _sds(shape, dtype):
    return jax.ShapeDtypeStruct(tuple(shape), dtype)


def _mx(v):
    return v.astype(MXU_DTYPE)


def mm_nn(a, b3, *, name, out_dtype, a_col0=0, res=None, relu2=False):
    M = a.shape[0]
    nb, K, Ns = b3.shape
    N = nb * Ns
    tm = _pick(M, (1024, 512, 256, 128))
    tk = _pick(K, (2048, 1024, 512, 256))
    tn = _pick(Ns, (512, 256, 128))
    assert a_col0 % tk == 0
    nk, nbs, ka0 = K // tk, Ns // tn, a_col0 // tk
    n_out = 2 if relu2 else 1

    def body(*refs):
        a_ref, b_ref = refs[0], refs[1]
        pos = 2
        res_ref = None
        if res is not None:
            res_ref = refs[pos]
            pos += 1
        outs = refs[pos:pos + n_out]
        acc_ref = refs[pos + n_out] if nk > 1 else None
        part = jnp.dot(_mx(a_ref[...]), _mx(b_ref[...]), preferred_element_type=F32)

        def finish(acc):
            if res_ref is not None:
                acc = acc + res_ref[...]
            outs[0][...] = acc.astype(outs[0].dtype)
            if relu2:
                r = jnp.maximum(acc, 0.0)
                outs[1][...] = (r * r).astype(outs[1].dtype)

        if nk == 1:
            finish(part)
        else:
            k = pl.program_id(2)

            @pl.when(k == 0)
            def _():
                acc_ref[...] = part

            @pl.when(k > 0)
            def _():
                acc_ref[...] += part

            @pl.when(k == nk - 1)
            def _():
                finish(acc_ref[...])

    in_specs = [pl.BlockSpec((tm, tk), lambda i, j, k: (i, ka0 + k)),
                pl.BlockSpec((None, tk, tn), lambda i, j, k: (j // nbs, k, j % nbs))]
    args = [a, b3]
    if res is not None:
        in_specs.append(pl.BlockSpec((tm, tn), lambda i, j, k: (i, j)))
        args.append(res)
    o_spec = pl.BlockSpec((tm, tn), lambda i, j, k: (i, j))
    if relu2:
        out_shape, out_specs = (_sds((M, N), out_dtype), _sds((M, N), out_dtype)), (o_spec, o_spec)
    else:
        out_shape, out_specs = _sds((M, N), out_dtype), o_spec
    scratch = (pltpu.VMEM((tm, tn), F32),) if nk > 1 else ()
    return _pcall(body, name=name, out_shape=out_shape, grid=(M // tm, N // tn, nk),
                  in_specs=in_specs, out_specs=out_specs, scratch=scratch)(*args)


def mm_nt(a, b3, *, name, out_dtype, a_col0=0, res=None, relu_mul=None):
    M = a.shape[0]
    nb, K, Ns = b3.shape
    tm = _pick(M, (1024, 512, 256, 128))
    tk = _pick(K, (1024, 512, 256))
    tn = _pick(Ns, (1024, 512, 256, 128))
    assert a_col0 % tn == 0
    nbs = Ns // tn
    nn, a0 = nb * nbs, a_col0 // tn

    def body(*refs):
        a_ref, b_ref = refs[0], refs[1]
        pos = 2
        mul_ref = res_ref = None
        if relu_mul is not None:
            mul_ref = refs[pos]
            pos += 1
        if res is not None:
            res_ref = refs[pos]
            pos += 1
        o_ref = refs[pos]
        acc_ref = refs[pos + 1] if nn > 1 else None
        part = lax.dot_general(_mx(a_ref[...]), _mx(b_ref[...]), (((1,), (1,)), ((), ())),
                               preferred_element_type=F32)

        def finish(acc):
            if mul_ref is not None:
                acc = acc * (2.0 * jnp.maximum(mul_ref[...].astype(F32), 0.0))
            if res_ref is not None:
                acc = acc + res_ref[...]
            o_ref[...] = acc.astype(o_ref.dtype)

        if nn == 1:
            finish(part)
        else:
            j = pl.program_id(2)

            @pl.when(j == 0)
            def _():
                acc_ref[...] = part

            @pl.when(j > 0)
            def _():
                acc_ref[...] += part

            @pl.when(j == nn - 1)
            def _():
                finish(acc_ref[...])

    in_specs = [pl.BlockSpec((tm, tn), lambda i, kk, j: (i, a0 + j)),
                pl.BlockSpec((None, tk, tn), lambda i, kk, j: (j // nbs, kk, j % nbs))]
    args = [a, b3]
    for extra in (relu_mul, res):
        if extra is not None:
            in_specs.append(pl.BlockSpec((tm, tk), lambda i, kk, j: (i, kk)))
            args.append(extra)
    scratch = (pltpu.VMEM((tm, tk), F32),) if nn > 1 else ()
    return _pcall(body, name=name, out_shape=_sds((M, K), out_dtype), grid=(M // tm, K // tk, nn),
                  in_specs=in_specs, out_specs=pl.BlockSpec((tm, tk), lambda i, kk, j: (i, kk)),
                  scratch=scratch)(*args)


def mm_tn(a, c, *, nb, name, out_dtype, K=None, N=None, a_col0=0, c_col0=0):
    M = a.shape[0]
    K = K or a.shape[1]
    N = N or c.shape[1]
    Ns = N // nb
    tm = _pick(M, (2048, 1024, 512, 256))
    tk = _pick(K, (512, 256))
    tn = _pick(Ns, (512, 256, 128))
    assert a_col0 % tk == 0 and c_col0 % tn == 0
    nm, nbs, a0, c0 = M // tm, Ns // tn, a_col0 // tk, c_col0 // tn

    def body(*refs):
        a_ref, c_ref, o_ref = refs[:3]
        acc_ref = refs[3] if nm > 1 else None
        part = lax.dot_general(_mx(a_ref[...]), _mx(c_ref[...]), (((0,), (0,)), ((), ())),
                               preferred_element_type=F32)
        if nm == 1:
            o_ref[...] = part.astype(o_ref.dtype)
        else:
            m = pl.program_id(2)

            @pl.when(m == 0)
            def _():
                acc_ref[...] = part

            @pl.when(m > 0)
            def _():
                acc_ref[...] += part

            @pl.when(m == nm - 1)
            def _():
                o_ref[...] = acc_ref[...].astype(o_ref.dtype)

    scratch = (pltpu.VMEM((tk, tn), F32),) if nm > 1 else ()
    return _pcall(
        body, name=name, out_shape=_sds((nb, K, Ns), out_dtype), grid=(K // tk, N // tn, nm),
        in_specs=[pl.BlockSpec((tm, tk), lambda kk, j, m: (m, a0 + kk)),
                  pl.BlockSpec((tm, tn), lambda kk, j, m: (m, c0 + j))],
        out_specs=pl.BlockSpec((None, tk, tn), lambda kk, j, m: (j // nbs, kk, j % nbs)),
        scratch=scratch)(a, c)


def rms_fwd(x, g, *, name, col0=0, width=None, out_dtype=BF16):
    R = x.shape[0]
    width = width or x.shape[1]
    assert col0 % width == 0
    cb = col0 // width
    tr = _pick(R, (512, 256, 128))

    def body(x_ref, g_ref, o_ref):
        xf = x_ref[...].astype(F32)
        r = lax.rsqrt(jnp.mean(xf * xf, axis=1, keepdims=True) + EPS)
        o_ref[...] = (xf * r * g_ref[...]).astype(o_ref.dtype)

    return _pcall(body, name=name, out_shape=_sds((R, width), out_dtype), grid=(R // tr,),
                  in_specs=[pl.BlockSpec((tr, width), lambda i: (i, cb)),
                            pl.BlockSpec((1, width), lambda i: (0, 0))],
                  out_specs=pl.BlockSpec((tr, width), lambda i: (i, 0)))(x, g)


def rms_bwd(x, g, dy, *, name, col0=0, width=None, res=None, dx_dtype=F32, need_dx=True):
    R = x.shape[0]
    width = width or x.shape[1]
    cb = col0 // width
    tr = _pick(R, (512, 256, 128))

    def body(*refs):
        x_ref, g_ref, dy_ref = refs[:3]
        pos = 3
        res_ref = None
        if res is not None:
            res_ref = refs[pos]
            pos += 1
        dx_ref = None
        if need_dx:
            dx_ref = refs[pos]
            pos += 1
        dg_ref = refs[pos]
        xf = x_ref[...].astype(F32)
        dyf = dy_ref[...].astype(F32)
        r = lax.rsqrt(jnp.mean(xf * xf, axis=1, keepdims=True) + EPS)
        xh = xf * r
        if need_dx:
            gy = dyf * g_ref[...]
            dx = r * (gy - xh * jnp.mean(gy * xh, axis=1, keepdims=True))
            if res_ref is not None:
                dx = dx + res_ref[...]
            dx_ref[...] = dx.astype(dx_ref.dtype)
        part = jnp.sum(dyf * xh, axis=0, keepdims=True)

        @pl.when(pl.program_id(0) == 0)
        def _():
            dg_ref[...] = part

        @pl.when(pl.program_id(0) > 0)
        def _():
            dg_ref[...] += part

    in_specs = [pl.BlockSpec((tr, width), lambda i: (i, cb)),
                pl.BlockSpec((1, width), lambda i: (0, 0)),
                pl.BlockSpec((tr, width), lambda i: (i, 0))]
    args = [x, g, dy]
    if res is not None:
        in_specs.append(pl.BlockSpec((tr, width), lambda i: (i, 0)))
        args.append(res)
    dg_shape, dg_spec = _sds((1, width), F32), pl.BlockSpec((1, width), lambda i: (0, 0))
    if need_dx:
        out_shape = (_sds((R, width), dx_dtype), dg_shape)
        out_specs = (pl.BlockSpec((tr, width), lambda i: (i, 0)), dg_spec)
    else:
        out_shape, out_specs = dg_shape, dg_spec
    out = _pcall(body, name=name, out_shape=out_shape, grid=(R // tr,), in_specs=in_specs,
                 out_specs=out_specs)(*args)
    return out if need_dx else (None, out)


def _rope_apply(y, c, sa, sb):
    return y * c + pltpu.roll(y, 96, 1) * sa + pltpu.roll(y, 32, 1) * sb


def _rope_transpose(dy, c, sa, sb):
    return dy * c + pltpu.roll(dy * sa, 32, 1) + pltpu.roll(dy * sb, 96, 1)


def rope_tables(seq):
    pos = jnp.arange(seq, dtype=F32)
    inv = ROPE_THETA ** (-jnp.arange(0, MLA_ROPE, 2, dtype=F32) / MLA_ROPE)
    ang = pos[:, None] * inv[None, :]
    cos, sin = jnp.cos(ang), jnp.sin(ang)
    z32, z64 = jnp.zeros_like(cos), jnp.zeros((seq, 64), F32)
    c = jnp.concatenate([cos, cos, z64], axis=1)
    sa = jnp.concatenate([-sin, z32, z64], axis=1)
    sb = jnp.concatenate([z32, sin, z64], axis=1)
    return c, sa, sb


def _head_vec(part_refs):
    xs = [p[...].astype(F32) for p in part_refs]
    return xs[0] if len(xs) == 1 else jnp.concatenate(xs, axis=1)


def prep_fwd(parts, g, *, name, n_heads, n_real, rope=None):
    rows = parts[0][0].shape[0]
    dh = sum(w for _, w, _ in parts)
    tr = _pick(rows, (512, 256, 128))
    npart = len(parts)

    def body(*refs):
        part_refs, g_ref = refs[:npart], refs[npart]
        pos = npart + 1
        if rope is not None:
            c_ref, sa_ref, sb_ref = refs[pos:pos + 3]
            pos += 3
        o_ref = refs[pos]
        x = _head_vec(part_refs)
        r = lax.rsqrt(jnp.sum(x * x, axis=1, keepdims=True) * (1.0 / n_real) + EPS)
        y = x * r * g_ref[...]
        if rope is not None:
            yr = _rope_apply(y[:, dh - 128:], c_ref[...], sa_ref[...], sb_ref[...])
            y = jnp.concatenate([y[:, :dh - 128], yr], axis=1)
        o_ref[...] = y.astype(o_ref.dtype)

    in_specs, args = [], []
    for arr, w, fn in parts:
        in_specs.append(pl.BlockSpec((tr, w), functools.partial(lambda h, i, fn: (i, fn(h)), fn=fn)))
        args.append(arr)
    in_specs.append(pl.BlockSpec((1, dh), lambda h, i: (0, 0)))
    args.append(g)
    if rope is not None:
        for t in rope:
            in_specs.append(pl.BlockSpec((tr, 128), lambda h, i: (i, 0)))
            args.append(t)
    return _pcall(body, name=name, out_shape=_sds((n_heads, rows, dh), BF16), grid=(n_heads, rows // tr),
                  in_specs=in_specs, out_specs=pl.BlockSpec((None, tr, dh), lambda h, i: (h, i, 0)))(*args)


def _norm_bwd(x, g, dyn, n_real):
    r = lax.rsqrt(jnp.sum(x * x, axis=1, keepdims=True) * (1.0 / n_real) + EPS)
    xh = x * r
    gy = dyn * g
    dx = r * (gy - xh * (jnp.sum(gy * xh, axis=1, keepdims=True) * (1.0 / n_real)))
    return dx, jnp.sum(dyn * xh, axis=0, keepdims=True)


def prep_bwd_q(src, dy, g, *, name, n_heads, dh, n_real, rope=None, out_dtype=BF16):
    rows = src.shape[0]
    tr = _pick(rows, (512, 256, 128))

    def body(*refs):
        x_ref, dy_ref, g_ref = refs[:3]
        pos = 3
        if rope is not None:
            c_ref, sa_ref, sb_ref = refs[pos:pos + 3]
            pos += 3
        dx_ref, dg_ref = refs[pos], refs[pos + 1]
        dyn = dy_ref[...].astype(F32)
        if rope is not None:
            dr = _rope_transpose(dyn[:, dh - 128:], c_ref[...], sa_ref[...], sb_ref[...])
            dyn = jnp.concatenate([dyn[:, :dh - 128], dr], axis=1)
        dx, dg = _norm_bwd(x_ref[...].astype(F32), g_ref[...], dyn, n_real)
        dx_ref[...] = dx.astype(dx_ref.dtype)
        first = jnp.logical_and(pl.program_id(0) == 0, pl.program_id(1) == 0)

        @pl.when(first)
        def _():
            dg_ref[...] = dg

        @pl.when(jnp.logical_not(first))
        def _():
            dg_ref[...] += dg

    in_specs = [pl.BlockSpec((tr, dh), lambda i, h: (i, h)),
                pl.BlockSpec((None, tr, dh), lambda i, h: (h, i, 0)),
                pl.BlockSpec((1, dh), lambda i, h: (0, 0))]
    args = [src, dy, g]
    if rope is not None:
        for t in rope:
            in_specs.append(pl.BlockSpec((tr, 128), lambda i, h: (i, 0)))
            args.append(t)
    return _pcall(body, name=name, out_shape=(_sds((rows, n_heads * dh), out_dtype), _sds((1, dh), F32)),
                  grid=(rows // tr, n_heads), in_specs=in_specs,
                  out_specs=(pl.BlockSpec((tr, dh), lambda i, h: (i, h)),
                             pl.BlockSpec((1, dh), lambda i, h: (0, 0))))(*args)


def prep_bwd_mla_k(kv_raw, zs, dkf, dv, g, rope, *, name):
    rows = kv_raw.shape[0]
    tr = _pick(rows, (512, 256, 128))

    def body(kn_ref, kr_ref, dy_ref, dv_ref, g_ref, c_ref, sa_ref, sb_ref, dkv_ref, dkr_ref, dg_ref):
        h = pl.program_id(1)
        x = jnp.concatenate([kn_ref[...].astype(F32), kr_ref[...].astype(F32)], axis=1)
        dyn = dy_ref[...].astype(F32)
        dr = _rope_transpose(dyn[:, 128:], c_ref[...], sa_ref[...], sb_ref[...])
        dyn = jnp.concatenate([dyn[:, :128], dr], axis=1)
        dx, dg = _norm_bwd(x, g_ref[...], dyn, MLA_QK)
        dkv_ref[...] = jnp.concatenate([dx[:, :128], dv_ref[...].astype(F32)], axis=1).astype(dkv_ref.dtype)

        @pl.when(h == 0)
        def _():
            dkr_ref[...] = dx[:, 128:]

        @pl.when(h > 0)
        def _():
            dkr_ref[...] += dx[:, 128:]

        first = jnp.logical_and(pl.program_id(0) == 0, h == 0)

        @pl.when(first)
        def _():
            dg_ref[...] = dg

        @pl.when(jnp.logical_not(first))
        def _():
            dg_ref[...] += dg

    tab = pl.BlockSpec((tr, 128), lambda i, h: (i, 0))
    return _pcall(
        body, name=name,
        out_shape=(_sds((rows, N_HEADS * 256), BF16), _sds((rows, 128), F32), _sds((1, MLA_PAD), F32)),
        grid=(rows // tr, N_HEADS),
        in_specs=[pl.BlockSpec((tr, 128), lambda i, h: (i, 2 * h)),
                  pl.BlockSpec((tr, 128), lambda i, h: (i, Z_KR // 128)),
                  pl.BlockSpec((None, tr, MLA_PAD), lambda i, h: (h, i, 0)),
                  pl.BlockSpec((None, tr, 128), lambda i, h: (h, i, 0)),
                  pl.BlockSpec((1, MLA_PAD), lambda i, h: (0, 0)), tab, tab, tab],
        out_specs=(pl.BlockSpec((tr, 256), lambda i, h: (i, h)),
                   pl.BlockSpec((tr, 128), lambda i, h: (i, 0)),
                   pl.BlockSpec((1, MLA_PAD), lambda i, h: (0, 0))))(kv_raw, zs, dkf, dv, g, *rope)


def prep_bwd_groups(src, base_blk, dys, gs, *, name, n_heads, kinds, out_dtype=BF16):
    rows = src.shape[0]
    ng = len(kinds)
    J = ng * n_heads
    tr = _pick(rows, (512, 256, 128))
    gstack = jnp.stack([gs[k] if kinds[k] == "norm" else jnp.ones((1, DH), F32) for k in range(ng)])

    def body(*refs):
        x_ref = refs[0]
        dy_refs = refs[1:1 + ng]
        g_ref, dx_ref, dg_ref = refs[1 + ng:4 + ng]
        j, i = pl.program_id(0), pl.program_id(1)
        grp = j // n_heads
        dy = dy_refs[0][...].astype(F32)
        for k in range(1, ng):
            dy = jnp.where(grp == k, dy_refs[k][...].astype(F32), dy)
        dx, dg = _norm_bwd(x_ref[...].astype(F32), g_ref[...], dy, DH)
        is_copy = functools.reduce(jnp.logical_or, [grp == k for k in range(ng) if kinds[k] == "copy"],
                                   jnp.bool_(False))
        dx_ref[...] = jnp.where(is_copy, dy, dx).astype(dx_ref.dtype)
        dg = jnp.where(is_copy, jnp.zeros_like(dg), dg)
        first = jnp.logical_and(j % n_heads == 0, i == 0)

        @pl.when(first)
        def _():
            dg_ref[...] = dg

        @pl.when(jnp.logical_not(first))
        def _():
            dg_ref[...] += dg

    in_specs = [pl.BlockSpec((tr, DH), lambda j, i: (i, base_blk + j))]
    for k in range(ng):
        in_specs.append(pl.BlockSpec(
            (None, tr, DH),
            functools.partial(lambda j, i, k: (jnp.clip(j - k * n_heads, 0, n_heads - 1), i, 0), k=k)))
    in_specs.append(pl.BlockSpec((None, 1, DH), lambda j, i: (j // n_heads, 0, 0)))
    return _pcall(body, name=name, out_shape=(_sds((rows, J * DH), out_dtype), _sds((ng, 1, DH), F32)),
                  grid=(J, rows // tr), in_specs=in_specs,
                  out_specs=(pl.BlockSpec((tr, DH), lambda j, i: (i, j)),
                             pl.BlockSpec((None, 1, DH), lambda j, i: (j // n_heads, 0, 0))))(src, *dys, gstack)


def _attn_cfg(mode, sq, sk):
    if mode == "chunk":
        tq = 128
        win = min((LEFT_CHUNKS + 2) * CHUNK, sk)
    else:
        tq = _pick(sq, (256, 128))
        win = sk
    scale = (MLA_QK if mode == "mla" else DH) ** -0.5
    return tq, win, scale


def _attn_scores(mode, i, tq, win, sk, scale, q, k_ref, cq_ref, ck_ref, t_ref):
    if mode == "chunk":
        start = pl.multiple_of(jnp.clip((i - LEFT_CHUNKS // 2) * 128, 0, sk - win), 128)
        kk = k_ref[pl.ds(start, win), :]
    else:
        start = 0
        kk = k_ref[...]
    s = lax.dot_general(q, kk, (((1,), (1,)), ((), ())), preferred_element_type=F32) * scale
    if mode == "cross":
        return s, start
    t_pos = i * tq + lax.broadcasted_iota(jnp.int32, (tq, win), 0)
    s_pos = start + lax.broadcasted_iota(jnp.int32, (tq, win), 1)
    if mode == "fox":
        s = s + cq_ref[...] - ck_ref[...]
        allowed = s_pos <= t_pos
    else:
        qc, kc = t_pos // CHUNK, s_pos // CHUNK
        allowed = kc <= qc
        if mode == "chunk":
            allowed = jnp.logical_and(allowed, kc >= qc - LEFT_CHUNKS)
            tiles = []
            for w in range(win // 128):
                delta = i - (start // 128 + w)
                tiles.append(jnp.where(delta == 0, t_ref[0], jnp.where(delta == 1, t_ref[1], t_ref[2])))
            s = s + jnp.concatenate(tiles, axis=1)
    return jnp.where(allowed, s, NEG), start


def attn_fwd(q, k, v_arr, v_blk, *, mode, name, n_heads, cq=None, ck=None, tiles=None):
    _, sq, dk = q.shape
    sk = k.shape[1]
    tq, win, scale = _attn_cfg(mode, sq, sk)

    def body(*refs):
        q_ref, k_ref, v_ref = refs[:3]
        pos = 3
        cq_ref = ck_ref = t_ref = None
        if mode == "fox":
            cq_ref, ck_ref = refs[pos:pos + 2]
            pos += 2
        if mode == "chunk":
            t_ref = refs[pos]
            pos += 1
        o_ref, lse_ref = refs[pos], refs[pos + 1]
        i = pl.program_id(1)
        s, start = _attn_scores(mode, i, tq, win, sk, scale, q_ref[...], k_ref, cq_ref, ck_ref, t_ref)
        vv = v_ref[pl.ds(start, win), :] if mode == "chunk" else v_ref[...]
        m = jnp.max(s, axis=1, keepdims=True)
        e = jnp.exp(s - m)
        l = jnp.sum(e, axis=1, keepdims=True)
        p = e * (1.0 / l)
        o_ref[...] = jnp.dot(_mx(p), _mx(vv), preferred_element_type=F32).astype(o_ref.dtype)
        lse_ref[...] = m + jnp.log(l)

    in_specs = [pl.BlockSpec((None, tq, dk), lambda h, i: (h, i, 0)),
                pl.BlockSpec((None, sk, dk), lambda h, i: (h, 0, 0)),
                pl.BlockSpec((sk, DH), lambda h, i: (0, v_blk(h)))]
    args = [q, k, v_arr]
    if mode == "fox":
        in_specs += [pl.BlockSpec((None, tq, 1), lambda h, i: (h, i, 0)),
                     pl.BlockSpec((None, 1, sk), lambda h, i: (h, 0, 0))]
        args += [cq, ck]
    if mode == "chunk":
        in_specs.append(pl.BlockSpec((None, 3, 128, 128), lambda h, i: (h, 0, 0, 0)))
        args.append(tiles)
    return _pcall(body, name=name,
                  out_shape=(_sds((sq, n_heads * DH), BF16), _sds((n_heads, sq, 1), F32)),
                  grid=(n_heads, sq // tq), in_specs=in_specs,
                  out_specs=(pl.BlockSpec((tq, DH), lambda h, i: (i, h)),
                             pl.BlockSpec((None, tq, 1), lambda h, i: (h, i, 0))))(*args)


def attn_bwd(q, k, v_arr, v_blk, o, do, lse, *, mode, name, n_heads, cq=None, ck=None, tiles=None):
    _, sq, dk = q.shape
    sk = k.shape[1]
    tq, win, scale = _attn_cfg(mode, sq, sk)
    n_extra = {"fox": 2, "chunk": 1}.get(mode, 0)

    def body(*refs):
        q_ref, k_ref, v_ref, o_ref, do_ref, lse_ref = refs[:6]
        pos = 6
        cq_ref = ck_ref = t_ref = None
        if mode == "fox":
            cq_ref, ck_ref = refs[pos:pos + 2]
            pos += 2
        if mode == "chunk":
            t_ref = refs[pos]
            pos += 1
        dq_ref, dk_ref, dv_ref = refs[pos:pos + 3]
        extra = refs[pos + 3:pos + 3 + n_extra]
        i = pl.program_id(1)
        q = q_ref[...]
        do = do_ref[...]
        s, start = _attn_scores(mode, i, tq, win, sk, scale, q, k_ref, cq_ref, ck_ref, t_ref)
        if mode == "chunk":
            rows = pl.ds(start, win)
            kk, vv = k_ref[rows, :], v_ref[rows, :]
        else:
            kk, vv = k_ref[...], v_ref[...]
        p = jnp.exp(s - lse_ref[...])
        drow = jnp.sum(do.astype(F32) * o_ref[...].astype(F32), axis=1, keepdims=True)
        dp = lax.dot_general(do, _mx(vv), (((1,), (1,)), ((), ())), preferred_element_type=F32)
        ds = p * (dp - drow)
        dsb = _mx(ds)
        dq_ref[...] = (jnp.dot(dsb, kk, preferred_element_type=F32) * scale).astype(dq_ref.dtype)
        dk_part = lax.dot_general(dsb, q, (((0,), (0,)), ((), ())), preferred_element_type=F32) * scale
        dv_part = lax.dot_general(_mx(p), do, (((0,), (0,)), ((), ())), preferred_element_type=F32)

        @pl.when(i == 0)
        def _():
            dk_ref[...] = jnp.zeros_like(dk_ref)
            dv_ref[...] = jnp.zeros_like(dv_ref)
            if mode == "fox":
                extra[1][...] = jnp.zeros_like(extra[1])
            if mode == "chunk":
                extra[0][...] = jnp.zeros_like(extra[0])

        if mode == "chunk":
            dk_ref[rows, :] += dk_part
            dv_ref[rows, :] += dv_part
            dt_ref = extra[0]
            for w in range(win // 128):
                delta = i - (start // 128 + w)
                tile = ds[:, w * 128:(w + 1) * 128]
                zero = jnp.zeros_like(tile)
                dt_ref[0] += jnp.where(delta == 0, tile, zero)
                dt_ref[1] += jnp.where(delta == 1, tile, zero)
                dt_ref[2] += jnp.where(delta >= 2, tile, zero)
        else:
            dk_ref[...] += dk_part
            dv_ref[...] += dv_part
        if mode == "fox":
            extra[0][...] = jnp.sum(ds, axis=1, keepdims=True)
            extra[1][...] -= jnp.sum(ds, axis=0, keepdims=True)

    in_specs = [pl.BlockSpec((None, tq, dk), lambda h, i: (h, i, 0)),
                pl.BlockSpec((None, sk, dk), lambda h, i: (h, 0, 0)),
                pl.BlockSpec((sk, DH), lambda h, i: (0, v_blk(h))),
                pl.BlockSpec((tq, DH), lambda h, i: (i, h)),
                pl.BlockSpec((tq, DH), lambda h, i: (i, h)),
                pl.BlockSpec((None, tq, 1), lambda h, i: (h, i, 0))]
    args = [q, k, v_arr, o, do, lse]
    out_shape = [_sds((n_heads, sq, dk), F32), _sds((n_heads, sk, dk), F32), _sds((n_heads, sk, DH), F32)]
    out_specs = [pl.BlockSpec((None, tq, dk), lambda h, i: (h, i, 0)),
                 pl.BlockSpec((None, sk, dk), lambda h, i: (h, 0, 0)),
                 pl.BlockSpec((None, sk, DH), lambda h, i: (h, 0, 0))]
    if mode == "fox":
        in_specs += [pl.BlockSpec((None, tq, 1), lambda h, i: (h, i, 0)),
                     pl.BlockSpec((None, 1, sk), lambda h, i: (h, 0, 0))]
        args += [cq, ck]
        out_shape += [_sds((n_heads, sq, 1), F32), _sds((n_heads, 1, sk), F32)]
        out_specs += [pl.BlockSpec((None, tq, 1), lambda h, i: (h, i, 0)),
                      pl.BlockSpec((None, 1, sk), lambda h, i: (h, 0, 0))]
    if mode == "chunk":
        in_specs.append(pl.BlockSpec((None, 3, 128, 128), lambda h, i: (h, 0, 0, 0)))
        args.append(tiles)
        out_shape.append(_sds((n_heads, 3, 128, 128), F32))
        out_specs.append(pl.BlockSpec((None, 3, 128, 128), lambda h, i: (h, 0, 0, 0)))
    return _pcall(body, name=name, out_shape=tuple(out_shape), grid=(n_heads, sq // tq),
                  in_specs=in_specs, out_specs=tuple(out_specs))(*args)


def _rel_index_tiles():
    a = lax.broadcasted_iota(jnp.int32, (128, 128), 0)
    b = lax.broadcasted_iota(jnp.int32, (128, 128), 1)
    d0 = a - b + REL_CLIP
    d1 = jnp.minimum(a - b + 128, REL_CLIP) + REL_CLIP
    return d0, d1


def relbias_tiles(rel_bias):
    nh = rel_bias.shape[0]

    def body(rb_ref, t_ref):
        h = pl.program_id(0)
        d0, d1 = _rel_index_tiles()

        def step(r, carry):
            t0, t1 = carry
            val = rb_ref[h, r]
            return jnp.where(d0 == r, val, t0), jnp.where(d1 == r, val, t1)

        zero = jnp.zeros((128, 128), F32)
        t0, t1 = lax.fori_loop(0, N_REL, step, (zero, zero))
        t_ref[0] = t0
        t_ref[1] = t1
        t_ref[2] = jnp.full((128, 128), rb_ref[h, N_REL - 1], F32)

    return _pcall(body, name="relbias_tiles", out_shape=_sds((nh, 3, 128, 128), F32), grid=(nh,),
                  in_specs=[pl.BlockSpec(memory_space=pltpu.SMEM)],
                  out_specs=pl.BlockSpec((None, 3, 128, 128), lambda h: (h, 0, 0, 0)))(rel_bias)


def relbias_tiles_bwd(dtiles):
    nh = dtiles.shape[0]

    def body(dt_ref, o_ref):
        d0, d1 = _rel_index_tiles()
        lane = lax.broadcasted_iota(jnp.int32, (1, 384), 1)
        t0, t1 = dt_ref[0], dt_ref[1]

        def step(r, acc):
            v = jnp.sum(jnp.where(d0 == r, t0, 0.0)) + jnp.sum(jnp.where(d1 == r, t1, 0.0))
            return acc + jnp.where(lane == r, v, 0.0)

        acc = lax.fori_loop(0, N_REL, step, jnp.zeros((1, 384), F32))
        o_ref[...] = acc + jnp.where(lane == N_REL - 1, jnp.sum(dt_ref[2]), 0.0)

    return _pcall(body, name="relbias_tiles_bwd", out_shape=_sds((nh, 1, 384), F32), grid=(nh,),
                  in_specs=[pl.BlockSpec((None, 3, 128, 128), lambda h: (h, 0, 0, 0))],
                  out_specs=pl.BlockSpec((None, 1, 384), lambda h: (h, 0, 0)))(dtiles)


CUM_BLK = 256


def _tri(n, lower):
    r = lax.broadcasted_iota(jnp.int32, (n, n), 0)
    c = lax.broadcasted_iota(jnp.int32, (n, n), 1)
    return jnp.where(r >= c if lower else r <= c, 1.0, 0.0).astype(F32)


def fox_cum_fwd(zs, bf):
    S = zs.shape[0]
    tb = min(CUM_BLK, S)

    def body(f_ref, b_ref, cum_ref, cumt_ref, carry_ref):
        @pl.when(pl.program_id(0) == 0)
        def _():
            carry_ref[...] = jnp.zeros_like(carry_ref)

        x = f_ref[...] + b_ref[...]
        lane = lax.broadcasted_iota(jnp.int32, x.shape, 1)
        logf = jnp.where(lane < N_HEADS, jnp.minimum(x, 0.0) - jnp.log(1.0 + jnp.exp(-jnp.abs(x))), 0.0)
        cum = jnp.dot(_tri(tb, True), logf, preferred_element_type=F32,
                      precision=lax.Precision.HIGHEST) + carry_ref[...]
        carry_ref[...] = cum[tb - 1:tb, :]
        cum_ref[...] = cum
        cumt_ref[...] = cum.T

    return _pcall(body, name="fox_cum_fwd", out_shape=(_sds((S, 128), F32), _sds((128, S), F32)),
                  grid=(S // tb,),
                  in_specs=[pl.BlockSpec((tb, 128), lambda i: (i, Z_FF // 128)),
                            pl.BlockSpec((1, 128), lambda i: (0, 0))],
                  out_specs=(pl.BlockSpec((tb, 128), lambda i: (i, 0)),
                             pl.BlockSpec((128, tb), lambda i: (0, i))),
                  scratch=(pltpu.VMEM((1, 128), F32),))(zs, bf)


def fox_cum_bwd(zs, bf, dcum):
    S = zs.shape[0]
    tb = min(CUM_BLK, S)
    nblk = S // tb

    def body(f_ref, b_ref, d_ref, df_ref, db_ref, carry_ref):
        @pl.when(pl.program_id(0) == 0)
        def _():
            carry_ref[...] = jnp.zeros_like(carry_ref)
            db_ref[...] = jnp.zeros_like(db_ref)

        d = d_ref[...]
        dlogf = jnp.dot(_tri(tb, False), d, preferred_element_type=F32,
                        precision=lax.Precision.HIGHEST) + carry_ref[...]
        carry_ref[...] += jnp.sum(d, axis=0, keepdims=True)
        x = f_ref[...] + b_ref[...]
        lane = lax.broadcasted_iota(jnp.int32, x.shape, 1)
        dx = jnp.where(lane < N_HEADS, dlogf / (1.0 + jnp.exp(x)), 0.0)
        df_ref[...] = dx
        db_ref[...] += jnp.sum(dx, axis=0, keepdims=True)

    return _pcall(body, name="fox_cum_bwd", out_shape=(_sds((S, 128), F32), _sds((1, 128), F32)),
                  grid=(nblk,),
                  in_specs=[pl.BlockSpec((tb, 128), lambda i: (nblk - 1 - i, Z_FF // 128)),
                            pl.BlockSpec((1, 128), lambda i: (0, 0)),
                            pl.BlockSpec((tb, 128), lambda i: (nblk - 1 - i, 0))],
                  out_specs=(pl.BlockSpec((tb, 128), lambda i: (nblk - 1 - i, 0)),
                             pl.BlockSpec((1, 128), lambda i: (0, 0))),
                  scratch=(pltpu.VMEM((1, 128), F32),))(zs, bf, dcum)


def _sigmoid(x):
    return 1.0 / (1.0 + jnp.exp(-x))


def merge_fwd(z, projs):
    S = z.shape[0]
    tr, tc = _pick(S, (512, 256, 128)), 512
    nbc = D_MODEL // tc
    g0 = Z_GATE // tc

    def body(g0_ref, g1_ref, g2_ref, p0_ref, p1_ref, p2_ref, o_ref):
        acc = _sigmoid(g0_ref[...]) * p0_ref[...]
        acc += _sigmoid(g1_ref[...]) * p1_ref[...]
        acc += _sigmoid(g2_ref[...]) * p2_ref[...]
        o_ref[...] = acc.astype(o_ref.dtype)

    gspecs = [pl.BlockSpec((tr, tc), functools.partial(lambda i, j, n: (i, g0 + n * nbc + j), n=n))
              for n in range(3)]
    pspec = pl.BlockSpec((tr, tc), lambda i, j: (i, j))
    return _pcall(body, name="merge_fwd", out_shape=_sds((S, D_MODEL), BF16), grid=(S // tr, nbc),
                  in_specs=gspecs + [pspec] * 3, out_specs=pspec)(z, z, z, *projs)


def merge_bwd(z, projs, dmerged):
    S = z.shape[0]
    tr, tc = _pick(S, (512, 256, 128)), 512
    nbc = D_MODEL // tc
    g0 = Z_GATE // tc

    def body(g0_ref, g1_ref, g2_ref, p0_ref, p1_ref, p2_ref, dm_ref, dg0, dg1, dg2, dp0, dp1, dp2):
        dm = dm_ref[...]
        for g_ref, p_ref, dg_ref, dp_ref in ((g0_ref, p0_ref, dg0, dp0), (g1_ref, p1_ref, dg1, dp1),
                                             (g2_ref, p2_ref, dg2, dp2)):
            sg = _sigmoid(g_ref[...])
            dp_ref[...] = (dm * sg).astype(dp_ref.dtype)
            dg_ref[...] = (dm * p_ref[...] * sg * (1.0 - sg)).astype(dg_ref.dtype)

    gspecs = [pl.BlockSpec((tr, tc), functools.partial(lambda i, j, n: (i, g0 + n * nbc + j), n=n))
              for n in range(3)]
    pspec = pl.BlockSpec((tr, tc), lambda i, j: (i, j))
    out = _pcall(body, name="merge_bwd", out_shape=tuple(_sds((S, D_MODEL), BF16) for _ in range(6)),
                 grid=(S // tr, nbc), in_specs=gspecs + [pspec] * 4,
                 out_specs=tuple([pspec] * 6))(z, z, z, *projs, dmerged)
    return out[:3], out[3:]


def loss_head(y, target):
    S, D = y.shape
    tr = _pick(S, (256, 128))

    def body(y_ref, t_ref, dy_ref, l_ref):
        e = y_ref[...] - t_ref[...]
        dy_ref[...] = e * (1.0 / D)
        part = jnp.sum(jnp.sum(e * e, axis=1, keepdims=True), axis=0, keepdims=True) * (0.5 / D)

        @pl.when(pl.program_id(0) == 0)
        def _():
            l_ref[...] = part

        @pl.when(pl.program_id(0) > 0)
        def _():
            l_ref[...] += part

    spec = pl.BlockSpec((tr, D), lambda i: (i, 0))
    return _pcall(body, name="loss_head", out_shape=(_sds((S, D), F32), _sds((1, 1), F32)), grid=(S // tr,),
                  in_specs=[spec, spec], out_specs=(spec, pl.BlockSpec((1, 1), lambda i: (0, 0))))(y, target)


def adamw(w, g, m, v, *, name):
    R, C = w.shape
    tr = _pick(R, (512, 256, 128, 64, 32, 16, 8))
    while tr * C * 4 > 2 * 1024 * 1024 and tr % 16 == 0:
        tr //= 2
    c1 = 1.0 / (1.0 - ADAM_B1 ** ADAM_STEP)
    c2 = 1.0 / (1.0 - ADAM_B2 ** ADAM_STEP)

    def body(w_ref, g_ref, m_ref, v_ref, d_ref, nm_ref, nv_ref):
        g_ = g_ref[...]
        nm = ADAM_B1 * m_ref[...] + (1.0 - ADAM_B1) * g_
        nv = ADAM_B2 * v_ref[...] + (1.0 - ADAM_B2) * (g_ * g_)
        nm_ref[...] = nm
        nv_ref[...] = nv
        d_ref[...] = -ADAM_LR * ((nm * c1) / (jnp.sqrt(nv * c2) + ADAM_EPS) + ADAM_WD * w_ref[...])

    spec = pl.BlockSpec((tr, C), lambda i: (i, 0))
    return _pcall(body, name=name, out_shape=tuple(_sds((R, C), F32) for _ in range(3)), grid=(R // tr,),
                  in_specs=[spec] * 4, out_specs=(spec,) * 3)(w, g, m, v)


SHARD_SHAPES = {
    "w_uq": (Q_LORA, 384), "w_ukv": (KV_LORA, 512), "w_br": (3, MIX_W, 512), "w_out": (512, D_MODEL),
    "w_xq": (512, 512), "w_xkv": (512, 1024), "w_xo": (512, 512), "w_1": (D_MODEL, 2048),
    "w_2": (2048, D_MODEL), "w_in": (D_MODEL, 3282),
}
PACK_ROWS = {n: int(np.prod(SHARD_SHAPES[n])) // PACK_W for n in PACK_ORDER}
PACK_OFF = dict(zip(PACK_ORDER, np.cumsum([0] + [PACK_ROWS[n] for n in PACK_ORDER[:-1]]).tolist()))
PACK_USED = sum(PACK_ROWS.values())
PACK_R = -(-PACK_USED // 512) * 512
PACK_R2 = PACK_R // 2


def pack_shards(parts, dtype):
    lead = parts[PACK_ORDER[0]].shape[:-len(SHARD_SHAPES[PACK_ORDER[0]])]
    segs = [parts[n].astype(dtype).reshape(lead + (PACK_ROWS[n], PACK_W)) for n in PACK_ORDER]
    segs.append(jnp.zeros(lead + (PACK_R - PACK_USED, PACK_W), dtype))
    return jnp.concatenate(segs, axis=-2)


def unpack_shards(packed):
    lead = packed.shape[:-2]
    return {n: packed[..., PACK_OFF[n]:PACK_OFF[n] + PACK_ROWS[n], :].reshape(lead + SHARD_SHAPES[n])
            for n in PACK_ORDER}


def _cols_from_shards(g):
    return jnp.transpose(g, (1, 0, 2)).reshape(g.shape[1], 4 * g.shape[2])


def _cols_to_shards(w):
    return jnp.transpose(w.reshape(w.shape[0], 4, w.shape[1] // 4), (1, 0, 2))


def full_weights(g):
    c = W_IN_CUTS
    w_in = _cols_from_shards(g["w_in"])
    zeros = lambda n: jnp.zeros((D_MODEL, n), w_in.dtype)
    w_in_p = jnp.concatenate([w_in[:, c[0]:c[3]], zeros(64), w_in[:, c[4]:c[5]], zeros(120),
                              w_in[:, c[3]:c[4]], w_in[:, c[5]:c[7]]], axis=1)
    w_uq = _cols_from_shards(g["w_uq"]).reshape(Q_LORA, N_HEADS, MLA_QK)
    w_uq_p = jnp.pad(w_uq, ((0, 0), (0, 0), (0, MLA_PAD - MLA_QK))).reshape(1, Q_LORA, N_HEADS * MLA_PAD)
    return dict(
        in_p=w_in_p[None], uq_p=w_uq_p, ukv=g["w_ukv"], br=[g["w_br"][:, n] for n in range(3)],
        out=g["w_out"].reshape(1, D_MODEL, D_MODEL), xq=g["w_xq"].reshape(1, D_MODEL, 512),
        xkv=g["w_xkv"].reshape(1, D_MODEL, 1024), xo=g["w_xo"], w1=g["w_1"],
        w2=g["w_2"].reshape(1, D_FF, D_MODEL))


def shard_grads(dw):
    dp = dw["in_p"][0]
    d_in = jnp.concatenate([dp[:, 0:832], dp[:, Z_FOX:Z_CH], dp[:, Z_FF:Z_FF + 8], dp[:, Z_CH:Z_TOT]], axis=1)
    d_uq = dw["uq_p"].reshape(Q_LORA, N_HEADS, MLA_PAD)[:, :, :MLA_QK].reshape(Q_LORA, N_HEADS * MLA_QK)
    return {
        "w_in": _cols_to_shards(d_in), "w_uq": _cols_to_shards(d_uq), "w_ukv": dw["ukv"],
        "w_br": jnp.stack(dw["br"], axis=1), "w_out": dw["out"].reshape(4, 512, D_MODEL),
        "w_xq": dw["xq"].reshape(4, 512, 512), "w_xkv": dw["xkv"].reshape(4, 512, 1024), "w_xo": dw["xo"],
        "w_1": dw["w1"], "w_2": dw["w2"].reshape(4, 2048, D_MODEL)}


def layer_params(d, l):
    row = lambda v: v.reshape(1, -1).astype(F32)
    padto = lambda v, n: jnp.pad(row(v), ((0, 0), (0, n - v.shape[-1])))
    return dict(
        g_mix=row(d["g_mix"][l]), g_cq=row(d["g_cq"][l]), g_ckv=row(d["g_ckv"][l]),
        g_mla_q=padto(d["g_mla_q"][l], MLA_PAD), g_mla_k=padto(d["g_mla_k"][l], MLA_PAD),
        b_f=padto(d["b_f"][l], 128), g_fox_q=row(d["g_fox_q"][l]), g_fox_k=row(d["g_fox_k"][l]),
        rel_bias=d["rel_bias"][l].astype(F32), g_ch_q=row(d["g_ch_q"][l]), g_ch_k=row(d["g_ch_k"][l]),
        g_cross=row(d["g_cross"][l]), g_mem=row(d["g_mem"][l]), g_x_q=row(d["g_x_q"][l]),
        g_x_k=row(d["g_x_k"][l]), g_mlp=row(d["g_mlp"][l]))


FOX_B = Z_FOX // 128
CH_B = Z_CH // 128


def layer_fwd(x, mem, W, P, rope):
    S = x.shape[0]
    s = {}
    s["h"] = rms_fwd(x, P["g_mix"], name="rms_d")
    z = s["z"] = mm_nn(s["h"], W["in_p"], name="mm_in", out_dtype=F32)
    s["cq_n"] = rms_fwd(z, P["g_cq"], col0=Z_CQ, width=Q_LORA, name="rms_cq")
    s["ckv_n"] = rms_fwd(z, P["g_ckv"], col0=Z_CKV, width=KV_LORA, name="rms_ckv")
    s["q_raw"] = mm_nn(s["cq_n"], W["uq_p"], name="mm_uq", out_dtype=F32)
    s["kv_raw"] = mm_nn(s["ckv_n"], W["ukv"], name="mm_ukv", out_dtype=F32)
    s["qa"] = prep_fwd([(s["q_raw"], MLA_PAD, lambda h: h)], P["g_mla_q"], name="prep_mla_q",
                       n_heads=N_HEADS, n_real=MLA_QK, rope=rope)
    s["ka"] = prep_fwd([(s["kv_raw"], 128, lambda h: 2 * h), (z, 128, lambda h: Z_KR // 128)], P["g_mla_k"],
                       name="prep_mla_k", n_heads=N_HEADS, n_real=MLA_QK, rope=rope)
    s["o_a"], s["lse_a"] = attn_fwd(s["qa"], s["ka"], s["kv_raw"], lambda h: 2 * h + 1, mode="mla",
                                    name="attn_mla", n_heads=N_HEADS)
    s["qb"] = prep_fwd([(z, DH, lambda h: FOX_B + h)], P["g_fox_q"], name="prep_h", n_heads=N_HEADS, n_real=DH)
    s["kb"] = prep_fwd([(z, DH, lambda h: FOX_B + N_HEADS + h)], P["g_fox_k"], name="prep_h",
                       n_heads=N_HEADS, n_real=DH)
    _, cum_t = fox_cum_fwd(z, P["b_f"])
    s["cq"] = cum_t[:N_HEADS].reshape(N_HEADS, S, 1)
    s["ck"] = cum_t[:N_HEADS].reshape(N_HEADS, 1, S)
    s["o_b"], s["lse_b"] = attn_fwd(s["qb"], s["kb"], z, lambda h: FOX_B + 2 * N_HEADS + h, mode="fox",
                                    name="attn_fox", n_heads=N_HEADS, cq=s["cq"], ck=s["ck"])
    s["qc"] = prep_fwd([(z, DH, lambda h: CH_B + h)], P["g_ch_q"], name="prep_h", n_heads=N_HEADS, n_real=DH)
    s["kc"] = prep_fwd([(z, DH, lambda h: CH_B + N_HEADS + h)], P["g_ch_k"], name="prep_h",
                       n_heads=N_HEADS, n_real=DH)
    s["tiles"] = relbias_tiles(P["rel_bias"])
    s["o_c"], s["lse_c"] = attn_fwd(s["qc"], s["kc"], z, lambda h: CH_B + 2 * N_HEADS + h, mode="chunk",
                                    name="attn_chunk", n_heads=N_HEADS, tiles=s["tiles"])
    s["projs"] = [mm_nn(o, W["br"][n], name="mm_br", out_dtype=F32)
                  for n, o in enumerate((s["o_a"], s["o_b"], s["o_c"]))]
    s["merged"] = merge_fwd(z, s["projs"])
    x1 = s["x1"] = mm_nn(s["merged"], W["out"], name="mm_out", out_dtype=F32, res=x)
    s["hq"] = rms_fwd(x1, P["g_cross"], name="rms_d")
    s["xq_raw"] = mm_nn(s["hq"], W["xq"], name="mm_xq", out_dtype=F32)
    s["mem_n"] = rms_fwd(mem, P["g_mem"], name="rms_mem")
    s["mkv"] = mm_nn(s["mem_n"], W["xkv"], name="mm_xkv", out_dtype=F32)
    s["qx"] = prep_fwd([(s["xq_raw"], DH, lambda h: h)], P["g_x_q"], name="prep_xq", n_heads=X_HEADS, n_real=DH)
    s["kx"] = prep_fwd([(s["mkv"], DH, lambda h: h)], P["g_x_k"], name="prep_xk", n_heads=X_HEADS, n_real=DH)
    s["o_x"], s["lse_x"] = attn_fwd(s["qx"], s["kx"], s["mkv"], lambda h: X_HEADS + h, mode="cross",
                                    name="attn_cross", n_heads=X_HEADS)
    x2 = s["x2"] = mm_nn(s["o_x"], W["xo"], name="mm_xo", out_dtype=F32, res=x1)
    s["hm"] = rms_fwd(x2, P["g_mlp"], name="rms_d")
    s["a1"], s["act"] = mm_nn(s["hm"], W["w1"], name="mm_w1", out_dtype=BF16, relu2=True)
    x3 = mm_nn(s["act"], W["w2"], name="mm_w2", out_dtype=F32, res=x2)
    return x3, s


def layer_bwd(g, x, mem, W, P, rope, s):
    S = x.shape[0]
    z = s["z"]
    dw, ds = {}, {}
    da1 = mm_nt(g, W["w2"], name="mm_w2_dx", out_dtype=BF16, relu_mul=s["a1"])
    dw["w2"] = mm_tn(s["act"], g, nb=1, name="mm_w2_dw", out_dtype=BF16)
    dhm = mm_nt(da1, W["w1"], name="mm_w1_dx", out_dtype=F32)
    dw["w1"] = mm_tn(s["hm"], da1, nb=4, name="mm_w1_dw", out_dtype=BF16)
    g2, ds["g_mlp"] = rms_bwd(s["x2"], P["g_mlp"], dhm, name="rms_d_bwd", res=g)
    do_x = mm_nt(g2, W["xo"], name="mm_xo_dx", out_dtype=BF16)
    dw["xo"] = mm_tn(s["o_x"], g2, nb=4, name="mm_xo_dw", out_dtype=BF16)
    dqx, dkx, dvx = attn_bwd(s["qx"], s["kx"], s["mkv"], lambda h: X_HEADS + h, s["o_x"], do_x, s["lse_x"],
                             mode="cross", name="attn_cross_bwd", n_heads=X_HEADS)
    dxq_raw, ds["g_x_q"] = prep_bwd_q(s["xq_raw"], dqx, P["g_x_q"], name="prep_xq_bwd", n_heads=X_HEADS,
                                      dh=DH, n_real=DH)
    dmkv, dgk = prep_bwd_groups(s["mkv"], 0, [dkx, dvx], [P["g_x_k"], None], name="prep_xkv_bwd",
                                n_heads=X_HEADS, kinds=("norm", "copy"))
    ds["g_x_k"] = dgk[0]
    dhq = mm_nt(dxq_raw, W["xq"], name="mm_xq_dx", out_dtype=F32)
    dw["xq"] = mm_tn(s["hq"], dxq_raw, nb=1, name="mm_xq_dw", out_dtype=BF16)
    dmem_n = mm_nt(dmkv, W["xkv"], name="mm_xkv_dx", out_dtype=F32)
    dw["xkv"] = mm_tn(s["mem_n"], dmkv, nb=1, name="mm_xkv_dw", out_dtype=BF16)
    _, ds["g_mem"] = rms_bwd(mem, P["g_mem"], dmem_n, name="rms_mem_bwd", need_dx=False)
    g1, ds["g_cross"] = rms_bwd(s["x1"], P["g_cross"], dhq, name="rms_d_bwd", res=g2)
    dmerged = mm_nt(g1, W["out"], name="mm_out_dx", out_dtype=F32)
    dw["out"] = mm_tn(s["merged"], g1, nb=1, name="mm_out_dw", out_dtype=BF16)
    dgl, dproj = merge_bwd(z, s["projs"], dmerged)
    outs = (s["o_a"], s["o_b"], s["o_c"])
    do = [mm_nt(dproj[n], W["br"][n], name="mm_br_dx", out_dtype=BF16) for n in range(3)]
    dw["br"] = [mm_tn(outs[n], dproj[n], nb=4, name="mm_br_dw", out_dtype=BF16) for n in range(3)]
    dqc, dkc, dvc, dtiles = attn_bwd(s["qc"], s["kc"], z, lambda h: CH_B + 2 * N_HEADS + h, s["o_c"], do[2],
                                     s["lse_c"], mode="chunk", name="attn_chunk_bwd", n_heads=N_HEADS,
                                     tiles=s["tiles"])
    d_ch, dg_ch = prep_bwd_groups(z, CH_B, [dqc, dkc, dvc], [P["g_ch_q"], P["g_ch_k"], None],
                                  name="prep_h_bwd", n_heads=N_HEADS, kinds=("norm", "norm", "copy"))
    ds["g_ch_q"], ds["g_ch_k"] = dg_ch[0], dg_ch[1]
    ds["rel_bias"] = relbias_tiles_bwd(dtiles)[:, 0, :N_REL]
    dqb, dkb, dvb, dcq, dck = attn_bwd(s["qb"], s["kb"], z, lambda h: FOX_B + 2 * N_HEADS + h, s["o_b"], do[1],
                                       s["lse_b"], mode="fox", name="attn_fox_bwd", n_heads=N_HEADS,
                                       cq=s["cq"], ck=s["ck"])
    d_fox, dg_fox = prep_bwd_groups(z, FOX_B, [dqb, dkb, dvb], [P["g_fox_q"], P["g_fox_k"], None],
                                    name="prep_h_bwd", n_heads=N_HEADS, kinds=("norm", "norm", "copy"))
    ds["g_fox_q"], ds["g_fox_k"] = dg_fox[0], dg_fox[1]
    dcum = jnp.pad((dcq[:, :, 0] + dck[:, 0, :]).T, ((0, 0), (0, 128 - N_HEADS)))
    dff, dbf = fox_cum_bwd(z, P["b_f"], dcum)
    ds["b_f"] = dbf[:, :N_HEADS]
    dqa, dka, dva = attn_bwd(s["qa"], s["ka"], s["kv_raw"], lambda h: 2 * h + 1, s["o_a"], do[0], s["lse_a"],
                             mode="mla", name="attn_mla_bwd", n_heads=N_HEADS)
    dq_raw, dgq = prep_bwd_q(s["q_raw"], dqa, P["g_mla_q"], name="prep_mla_q_bwd", n_heads=N_HEADS,
                             dh=MLA_PAD, n_real=MLA_QK, rope=rope)
    dkv_raw, dkr, dgk = prep_bwd_mla_k(s["kv_raw"], z, dka, dva, P["g_mla_k"], rope, name="prep_mla_k_bwd")
    ds["g_mla_q"], ds["g_mla_k"] = dgq[:, :MLA_QK], dgk[:, :MLA_QK]
    dcq_n = mm_nt(dq_raw, W["uq_p"], name="mm_uq_dx", out_dtype=F32)
    dw["uq_p"] = mm_tn(s["cq_n"], dq_raw, nb=1, name="mm_uq_dw", out_dtype=BF16)
    dckv_n = mm_nt(dkv_raw, W["ukv"], name="mm_ukv_dx", out_dtype=F32)
    dw["ukv"] = mm_tn(s["ckv_n"], dkv_raw, nb=4, name="mm_ukv_dw", out_dtype=BF16)
    d_cq, ds["g_cq"] = rms_bwd(z, P["g_cq"], dcq_n, name="rms_cq_bwd", col0=Z_CQ, width=Q_LORA, dx_dtype=BF16)
    d_ckv, ds["g_ckv"] = rms_bwd(z, P["g_ckv"], dckv_n, name="rms_ckv_bwd", col0=Z_CKV, width=KV_LORA,
                                 dx_dtype=BF16)
    dz = jnp.concatenate([d_cq, d_ckv, dkr.astype(BF16), dff.astype(BF16), d_fox, d_ch, *dgl], axis=1)
    dh = mm_nt(dz, W["in_p"], name="mm_in_dx", out_dtype=F32)
    dw["in_p"] = mm_tn(s["h"], dz, nb=1, name="mm_in_dw", out_dtype=BF16)
    g0, ds["g_mix"] = rms_bwd(x, P["g_mix"], dh, name="rms_d_bwd", res=g1)
    return g0, dw, ds


def local_step(x, mem, target, Ws, Ps):
    rope = rope_tables(x.shape[0])
    saved, xs = [], [x]
    for W, P in zip(Ws, Ps):
        y, s = layer_fwd(xs[-1], mem, W, P, rope)
        xs.append(y)
        saved.append(s)
    g, loss = loss_head(xs[-1], target)
    dws, dss = [], []
    for l in reversed(range(len(Ws))):
        g, dw, ds = layer_bwd(g, xs[l], mem, Ws[l], Ps[l], rope, saved[l])
        dws.append(dw)
        dss.append(ds)
    return loss, g, dws[::-1], dss[::-1]


HBM_SPEC = pl.BlockSpec(memory_space=pltpu.HBM)


def _place():
    return lax.axis_index("x"), lax.axis_index("y"), lax.axis_index("c")


def _other_chips(x, y):
    return [(1 - x, y), (x, 1 - y), (1 - x, 1 - y)]


def _remote(src, dst, send_sems, recv_sems, k, to):
    return pltpu.make_async_remote_copy(src_ref=src, dst_ref=dst, send_sem=send_sems.at[k],
                                        recv_sem=recv_sems.at[k], device_id=to, device_id_type=MESH)


def _comm_call(body, *, name, out_shape, n_in, n_sem, n_local=1, aliases=None):
    return pl.pallas_call(
        body, out_shape=out_shape, in_specs=[HBM_SPEC] * n_in, out_specs=HBM_SPEC,
        scratch_shapes=[pltpu.SemaphoreType.DMA((n_sem,)), pltpu.SemaphoreType.DMA((n_sem,)),
                        pltpu.SemaphoreType.DMA((n_local,))],
        name=name, interpret=False, input_output_aliases=aliases or {})


def allgather_weights(pack):
    R, Wd = pack.shape
    R2 = R // 2

    def body(src, out, send_sems, recv_sems, local_sem):
        x, y, c = _place()
        sibling = (x, y, 1 - c)
        chips = _other_chips(x, y)

        def half(shard, hc):
            return out.at[shard, pl.ds(hc * R2, R2), :]

        mine = pltpu.make_async_copy(src, out.at[2 * x + y], local_sem.at[0])
        mine.start()
        my_half = src.at[pl.ds(c * R2, R2), :]
        first = [_remote(my_half, half(2 * x + y, c), send_sems, recv_sems, j, (*chip, c))
                 for j, chip in enumerate(chips)]
        for cp in first:
            cp.start()
        passed = []
        for j, (cx, cy) in enumerate(chips):
            landed = half(2 * cx + cy, c)
            _remote(my_half, landed, send_sems, recv_sems, j, (cx, cy, c)).wait_recv()
            fw = _remote(landed, landed, send_sems, recv_sems, 3 + j, sibling)
            fw.start()
            passed.append(fw)
        for j, (cx, cy) in enumerate(chips):
            theirs = half(2 * cx + cy, 1 - c)
            _remote(theirs, theirs, send_sems, recv_sems, 3 + j, sibling).wait_recv()
        for cp in first + passed:
            cp.wait_send()
        mine.wait()

    return _comm_call(body, name="allgather_weights", out_shape=_sds((4, R, Wd), pack.dtype),
                      n_in=1, n_sem=6)(pack)


def rs_to_sibling(part):
    _, R, Wd = part.shape
    R2 = R // 2

    def body(src, keep, got, send_sems, recv_sems, local_sems):
        x, y, c = _place()
        sends, locs = [], []
        for s_ in range(4):
            lc = pltpu.make_async_copy(src.at[s_, pl.ds(c * R2, R2), :], keep.at[s_], local_sems.at[s_])
            lc.start()
            locs.append(lc)
            cp = _remote(src.at[s_, pl.ds((1 - c) * R2, R2), :], got.at[s_], send_sems, recv_sems, s_,
                         (x, y, 1 - c))
            cp.start()
            sends.append(cp)
        for cp in sends:
            cp.wait()
        for lc in locs:
            lc.wait()

    half = _sds((4, R2, Wd), part.dtype)
    return pl.pallas_call(
        body, out_shape=(half, half), in_specs=[HBM_SPEC], out_specs=(HBM_SPEC, HBM_SPEC),
        scratch_shapes=[pltpu.SemaphoreType.DMA((4,)), pltpu.SemaphoreType.DMA((4,)),
                        pltpu.SemaphoreType.DMA((4,))],
        name="rs_to_sibling", interpret=False)(part)


def rs_to_owner(q):
    _, R2, Wd = q.shape

    def body(src, out, send_sems, recv_sems, local_sem):
        x, y, c = _place()
        me = 2 * x + y
        mine = pltpu.make_async_copy(src.at[me], out.at[me], local_sem.at[0])
        mine.start()
        sends = [_remote(src.at[2 * cx + cy], out.at[me], send_sems, recv_sems, j, (cx, cy, c))
                 for j, (cx, cy) in enumerate(_other_chips(x, y))]
        for cp in sends:
            cp.start()
        for cp in sends:
            cp.wait()
        mine.wait()

    return _comm_call(body, name="rs_to_owner", out_shape=_sds((4, R2, Wd), q.dtype), n_in=1, n_sem=3)(q)


def rs_share_halves(t):
    R2, Wd = t.shape

    def body(src, out, send_sems, recv_sems, local_sem):
        x, y, c = _place()
        mine = pltpu.make_async_copy(src, out.at[c], local_sem.at[0])
        mine.start()
        cp = _remote(src, out.at[c], send_sems, recv_sems, 0, (x, y, 1 - c))
        cp.start()
        cp.wait()
        mine.wait()

    return _comm_call(body, name="rs_share_halves", out_shape=_sds((2, R2, Wd), t.dtype), n_in=1, n_sem=1)(t)


def add_pair(a, b, *, out_dtype):
    n, R2, Wd = a.shape
    tr = _pick(R2, (512, 256))

    def body(a_ref, b_ref, o_ref):
        o_ref[...] = (a_ref[...].astype(F32) + b_ref[...].astype(F32)).astype(o_ref.dtype)

    spec = pl.BlockSpec((None, tr, Wd), lambda s_, i: (s_, i, 0))
    return _pcall(body, name="rs_add_pair", out_shape=_sds(a.shape, out_dtype), grid=(n, R2 // tr),
                  in_specs=[spec, spec], out_specs=spec)(a, b)


def sum_slots(r):
    _, R2, Wd = r.shape
    tr = _pick(R2, (512, 256))

    def body(r0, r1, r2, r3, o_ref):
        o_ref[...] = ((r0[...].astype(F32) + r1[...].astype(F32)) + r2[...].astype(F32)) + r3[...].astype(F32)

    specs = [pl.BlockSpec((None, tr, Wd), functools.partial(lambda i, s_: (s_, i, 0), s_=s_)) for s_ in range(4)]
    return _pcall(body, name="rs_sum_slots", out_shape=_sds((R2, Wd), F32), grid=(R2 // tr,),
                  in_specs=specs, out_specs=pl.BlockSpec((tr, Wd), lambda i: (i, 0)))(r, r, r, r)


def reduce_scatter_grads(part):
    keep, got = rs_to_sibling(part)
    chip_sum = add_pair(keep, got, out_dtype=part.dtype)
    slots = rs_to_owner(chip_sum)
    return rs_share_halves(sum_slots(slots)).reshape(part.shape[1], part.shape[2])


def allreduce_small(v):
    Rs = v.shape[0]

    def body(v_ref, o_ref, buf, send_sems, recv_sems):
        x, y, c = _place()
        me = 4 * x + 2 * y + c
        buf[me] = v_ref[...]
        flips = [(fx, fy, fc) for fx in (0, 1) for fy in (0, 1) for fc in (0, 1)][1:]
        sends = []
        for k, (fx, fy, fc) in enumerate(flips):
            to = ((1 - x) if fx else x, (1 - y) if fy else y, (1 - c) if fc else c)
            cp = _remote(v_ref, buf.at[me], send_sems, recv_sems, k, to)
            cp.start()
            sends.append(cp)
        for cp in sends:
            cp.wait()
        acc = buf[0]
        for d in range(1, 8):
            acc = acc + buf[d]
        o_ref[...] = acc

    vm = pl.BlockSpec(memory_space=pltpu.VMEM)
    return pl.pallas_call(
        body, out_shape=_sds((Rs, 128), F32), in_specs=[vm], out_specs=vm,
        scratch_shapes=[pltpu.VMEM((8, Rs, 128), F32), pltpu.SemaphoreType.DMA((7,)),
                        pltpu.SemaphoreType.DMA((7,))],
        name="allreduce_small", interpret=False)(v)


INPUT_NAMES = (("x", "mem") + WEIGHT_ORDER + ("loss_target",) + tuple("m_" + n for n in WEIGHT_ORDER)
               + tuple("v_" + n for n in WEIGHT_ORDER))


def _pack_small(vals, n_layers, extra=None):
    flat = jnp.concatenate([vals[n].reshape(n_layers, -1).astype(F32) for n in SMALL_ORDER], axis=1).reshape(-1)
    if extra is not None:
        flat = jnp.concatenate([flat, extra.reshape(-1)])
    n = flat.shape[0]
    rows = -(-n // 1024) * 8
    return jnp.pad(flat, (0, rows * 128 - n)).reshape(rows, 128)


def _unpack_small(packed, like, n_layers):
    per_layer = sum(int(np.prod(like[n].shape[1:])) for n in SMALL_ORDER)
    body = packed.reshape(-1)[:n_layers * per_layer].reshape(n_layers, per_layer)
    out, off = {}, 0
    for n in SMALL_ORDER:
        k = int(np.prod(like[n].shape[1:]))
        out[n] = body[:, off:off + k].reshape(like[n].shape)
        off += k
    return out, packed.reshape(-1)[n_layers * per_layer]


def kernel(x, mem, g_mix, w_in, g_cq, w_uq, g_ckv, w_ukv, g_mla_q, g_mla_k, b_f, g_fox_q, g_fox_k, rel_bias, g_ch_q, g_ch_k, w_br, w_out, g_cross, g_mem, w_xq, w_xkv, g_x_q, g_x_k, w_xo, g_mlp, w_1, w_2, loss_target, m_g_mix, m_w_in, m_g_cq, m_w_uq, m_g_ckv, m_w_ukv, m_g_mla_q, m_g_mla_k, m_b_f, m_g_fox_q, m_g_fox_k, m_rel_bias, m_g_ch_q, m_g_ch_k, m_w_br, m_w_out, m_g_cross, m_g_mem, m_w_xq, m_w_xkv, m_g_x_q, m_g_x_k, m_w_xo, m_g_mlp, m_w_1, m_w_2, v_g_mix, v_w_in, v_g_cq, v_w_uq, v_g_ckv, v_w_ukv, v_g_mla_q, v_g_mla_k, v_b_f, v_g_fox_q, v_g_fox_k, v_rel_bias, v_g_ch_q, v_g_ch_k, v_w_br, v_w_out, v_g_cross, v_g_mem, v_w_xq, v_w_xkv, v_g_x_q, v_g_x_k, v_w_xo, v_g_mlp, v_w_1, v_w_2):
    d = dict(zip(INPUT_NAMES, (x, mem, g_mix, w_in, g_cq, w_uq, g_ckv, w_ukv, g_mla_q, g_mla_k, b_f, g_fox_q, g_fox_k, rel_bias, g_ch_q, g_ch_k, w_br, w_out, g_cross, g_mem, w_xq, w_xkv, g_x_q, g_x_k, w_xo, g_mlp, w_1, w_2, loss_target, m_g_mix, m_w_in, m_g_cq, m_w_uq, m_g_ckv, m_w_ukv, m_g_mla_q, m_g_mla_k, m_b_f, m_g_fox_q, m_g_fox_k, m_rel_bias, m_g_ch_q, m_g_ch_k, m_w_br, m_w_out, m_g_cross, m_g_mem, m_w_xq, m_w_xkv, m_g_x_q, m_g_x_k, m_w_xo, m_g_mlp, m_w_1, m_w_2, v_g_mix, v_w_in, v_g_cq, v_w_uq, v_g_ckv, v_w_ukv, v_g_mla_q, v_g_mla_k, v_b_f, v_g_fox_q, v_g_fox_k, v_rel_bias, v_g_ch_q, v_g_ch_k, v_w_br, v_w_out, v_g_cross, v_g_mem, v_w_xq, v_w_xkv, v_g_x_q, v_g_x_k, v_w_xo, v_g_mlp, v_w_1, v_w_2)))
    n_layers = g_mix.shape[0]
    assert x.shape[0] == 1 and x.shape[2] == D_MODEL and mem.shape[1:] == (MEM_LEN, D_MODEL)
    for n in PACK_ORDER:
        assert d[n].shape[1:] == SHARD_SHAPES[n], (n, d[n].shape)

    packed = pack_shards({n: d[n] for n in PACK_ORDER}, BF16)
    Ws = [full_weights(unpack_shards(allgather_weights(packed[l]))) for l in range(n_layers)]
    Ps = [layer_params(d, l) for l in range(n_layers)]

    loss, dx, dws, dss = local_step(x[0], mem[0], loss_target[0], Ws, Ps)

    big = []
    for l in range(n_layers):
        part = pack_shards(shard_grads(dws[l]), BF16)
        big.append(unpack_shards(reduce_scatter_grads(part)))
    grads = {n: jnp.stack([big[l][n] for l in range(n_layers)]) for n in PACK_ORDER}

    small_local = {n: jnp.stack([dss[l][n].reshape(d[n].shape[1:]) for l in range(n_layers)]) for n in SMALL_ORDER}
    small_sum, loss_sum = _unpack_small(allreduce_small(_pack_small(small_local, n_layers, extra=loss)),
                                        {n: d[n] for n in SMALL_ORDER}, n_layers)
    grads.update(small_sum)

    delta, new_m, new_v = {}, {}, {}
    for n in PACK_ORDER:
        shp = d[n].shape
        two_d = lambda a: a.reshape(-1, shp[-1])
        dl, nm, nv = adamw(two_d(d[n]), two_d(grads[n]), two_d(d["m_" + n]), two_d(d["v_" + n]), name="adamw_" + n)
        delta[n], new_m[n], new_v[n] = dl.reshape(shp), nm.reshape(shp), nv.reshape(shp)
    like = {n: d[n] for n in SMALL_ORDER}
    sm = adamw(_pack_small(like, n_layers), _pack_small(small_sum, n_layers),
               _pack_small({n: d["m_" + n] for n in SMALL_ORDER}, n_layers),
               _pack_small({n: d["v_" + n] for n in SMALL_ORDER}, n_layers), name="adamw_small")
    for res, src in zip((delta, new_m, new_v), sm):
        res.update(_unpack_small(src, like, n_layers)[0])

    return (loss_sum, dx[None], *[grads[n] for n in WEIGHT_ORDER], *[delta[n] for n in WEIGHT_ORDER],
            *[new_m[n] for n in WEIGHT_ORDER], *[new_v[n] for n in WEIGHT_ORDER])
```

```python
import functools

import numpy as np
import jax
import jax.numpy as jnp
from jax import lax
from jax.experimental import pallas as pl
from jax.experimental.pallas import tpu as pltpu

F32 = jnp.float32
BF16 = jnp.bfloat16
MXU_DTYPE = jnp.bfloat16
MESH = pl.DeviceIdType.MESH

D_MODEL = 2048
MIX_W = 1024
N_HEADS = 8
DH = 128
MLA_NOPE = 128
MLA_ROPE = 64
MLA_QK = MLA_NOPE + MLA_ROPE
MLA_PAD = 256
Q_LORA = 512
KV_LORA = 256
CHUNK = 64
LEFT_CHUNKS = 8
REL_CLIP = 128
N_REL = 2 * REL_CLIP + 1
X_HEADS = 4
MEM_LEN = 256
D_FF = 8192
ROPE_THETA = 10000.0
EPS = 1e-6
NEG = -1e30

Z_CQ, Z_CKV, Z_KR, Z_FF = 0, 512, 768, 896
ZS_W = 1024
Z_FOX = 1024
Z_CH = Z_FOX + 3 * MIX_W
Z_GATE = Z_CH + 3 * MIX_W
Z_TOT = Z_GATE + 3 * D_MODEL
W_IN_CUTS = (0, 512, 768, 832, 3904, 3912, 6984, 13128)

ADAM_LR, ADAM_B1, ADAM_B2, ADAM_EPS, ADAM_WD, ADAM_STEP = 0.001, 0.9, 0.999, 1e-08, 0.01, 10

VMEM_LIMIT_BYTES = 56 * 1024 * 1024
PACK_ORDER = ("w_uq", "w_ukv", "w_br", "w_out", "w_xq", "w_xkv", "w_xo", "w_1", "w_2", "w_in")
SMALL_ORDER = ("g_mix", "g_cq", "g_ckv", "g_mla_q", "g_mla_k", "b_f", "g_fox_q", "g_fox_k", "rel_bias",
               "g_ch_q", "g_ch_k", "g_cross", "g_mem", "g_x_q", "g_x_k", "g_mlp")
WEIGHT_ORDER = ("g_mix", "w_in", "g_cq", "w_uq", "g_ckv", "w_ukv", "g_mla_q", "g_mla_k", "b_f", "g_fox_q",
                "g_fox_k", "rel_bias", "g_ch_q", "g_ch_k", "w_br", "w_out", "g_cross", "g_mem", "w_xq",
                "w_xkv", "g_x_q", "g_x_k", "w_xo", "g_mlp", "w_1", "w_2")


def _pick(n, prefs):
    for p in prefs:
        if n % p == 0:
            return p
    raise ValueError(f"no block size among {prefs} divides {n}")


def _pcall(body, *, name, out_shape, in_specs, out_specs, grid=(), scratch=(), aliases=None):
    return pl.pallas_call(
        body, out_shape=out_shape, grid=grid, in_specs=in_specs, out_specs=out_specs,
        scratch_shapes=scratch, name=name, interpret=False,
        input_output_aliases=aliases or {},
        compiler_params=pltpu.CompilerParams(vmem_limit_bytes=VMEM_LIMIT_BYTES))


def _sds(shape, dtype):
    return jax.ShapeDtypeStruct(tuple(shape), dtype)


def _mx(v):
    return v.astype(MXU_DTYPE)


def mm_nn(a, b3, *, name, out_dtype, a_col0=0, res=None, relu2=False):
    M = a.shape[0]
    nb, K, Ns = b3.shape
    N = nb * Ns
    tm = _pick(M, (1024, 512, 256, 128))
    tk = _pick(K, (2048, 1024, 512, 256))
    tn = _pick(Ns, (512, 256, 128))
    assert a_col0 % tk == 0
    nk, nbs, ka0 = K // tk, Ns // tn, a_col0 // tk
    n_out = 2 if relu2 else 1

    def body(*refs):
        a_ref, b_ref = refs[0], refs[1]
        pos = 2
        res_ref = None
        if res is not None:
            res_ref = refs[pos]
            pos += 1
        outs = refs[pos:pos + n_out]
        acc_ref = refs[pos + n_out] if nk > 1 else None
        part = jnp.dot(_mx(a_ref[...]), _mx(b_ref[...]), preferred_element_type=F32)

        def finish(acc):
            if res_ref is not None:
                acc = acc + res_ref[...]
            outs[0][...] = acc.astype(outs[0].dtype)
            if relu2:
                r = jnp.maximum(acc, 0.0)
                outs[1][...] = (r * r).astype(outs[1].dtype)

        if nk == 1:
            finish(part)
        else:
            k = pl.program_id(2)

            @pl.when(k == 0)
            def _():
                acc_ref[...] = part

            @pl.when(k > 0)
            def _():
                acc_ref[...] += part

            @pl.when(k == nk - 1)
            def _():
                finish(acc_ref[...])

    in_specs = [pl.BlockSpec((tm, tk), lambda i, j, k: (i, ka0 + k)),
                pl.BlockSpec((None, tk, tn), lambda i, j, k: (j // nbs, k, j % nbs))]
    args = [a, b3]
    if res is not None:
        in_specs.append(pl.BlockSpec((tm, tn), lambda i, j, k: (i, j)))
        args.append(res)
    o_spec = pl.BlockSpec((tm, tn), lambda i, j, k: (i, j))
    if relu2:
        out_shape, out_specs = (_sds((M, N), out_dtype), _sds((M, N), out_dtype)), (o_spec, o_spec)
    else:
        out_shape, out_specs = _sds((M, N), out_dtype), o_spec
    scratch = (pltpu.VMEM((tm, tn), F32),) if nk > 1 else ()
    return _pcall(body, name=name, out_shape=out_shape, grid=(M // tm, N // tn, nk),
                  in_specs=in_specs, out_specs=out_specs, scratch=scratch)(*args)


def mm_nt(a, b3, *, name, out_dtype, a_col0=0, res=None, relu_mul=None):
    M = a.shape[0]
    nb, K, Ns = b3.shape
    tm = _pick(M, (1024, 512, 256, 128))
    tk = _pick(K, (1024, 512, 256))
    tn = _pick(Ns, (1024, 512, 256, 128))
    assert a_col0 % tn == 0
    nbs = Ns // tn
    nn, a0 = nb * nbs, a_col0 // tn

    def body(*refs):
        a_ref, b_ref = refs[0], refs[1]
        pos = 2
        mul_ref = res_ref = None
        if relu_mul is not None:
            mul_ref = refs[pos]
            pos += 1
        if res is not None:
            res_ref = refs[pos]
            pos += 1
        o_ref = refs[pos]
        acc_ref = refs[pos + 1] if nn > 1 else None
        part = lax.dot_general(_mx(a_ref[...]), _mx(b_ref[...]), (((1,), (1,)), ((), ())),
                               preferred_element_type=F32)

        def finish(acc):
            if mul_ref is not None:
                acc = acc * (2.0 * jnp.maximum(mul_ref[...].astype(F32), 0.0))
            if res_ref is not None:
                acc = acc + res_ref[...]
            o_ref[...] = acc.astype(o_ref.dtype)

        if nn == 1:
            finish(part)
        else:
            j = pl.program_id(2)

            @pl.when(j == 0)
            def _():
                acc_ref[...] = part

            @pl.when(j > 0)
            def _():
                acc_ref[...] += part

            @pl.when(j == nn - 1)
            def _():
                finish(acc_ref[...])

    in_specs = [pl.BlockSpec((tm, tn), lambda i, kk, j: (i, a0 + j)),
                pl.BlockSpec((None, tk, tn), lambda i, kk, j: (j // nbs, kk, j % nbs))]
    args = [a, b3]
    for extra in (relu_mul, res):
        if extra is not None:
            in_specs.append(pl.BlockSpec((tm, tk), lambda i, kk, j: (i, kk)))
            args.append(extra)
    scratch = (pltpu.VMEM((tm, tk), F32),) if nn > 1 else ()
    return _pcall(body, name=name, out_shape=_sds((M, K), out_dtype), grid=(M // tm, K // tk, nn),
                  in_specs=in_specs, out_specs=pl.BlockSpec((tm, tk), lambda i, kk, j: (i, kk)),
                  scratch=scratch)(*args)


def mm_tn(a, c, *, nb, name, out_dtype, K=None, N=None, a_col0=0, c_col0=0):
    M = a.shape[0]
    K = K or a.shape[1]
    N = N or c.shape[1]
    Ns = N // nb
    tm = _pick(M, (2048, 1024, 512, 256))
    tk = _pick(K, (512, 256))
    tn = _pick(Ns, (512, 256, 128))
    assert a_col0 % tk == 0 and c_col0 % tn == 0
    nm, nbs, a0, c0 = M // tm, Ns // tn, a_col0 // tk, c_col0 // tn

    def body(*refs):
        a_ref, c_ref, o_ref = refs[:3]
        acc_ref = refs[3] if nm > 1 else None
        part = lax.dot_general(_mx(a_ref[...]), _mx(c_ref[...]), (((0,), (0,)), ((), ())),
                               preferred_element_type=F32)
        if nm == 1:
            o_ref[...] = part.astype(o_ref.dtype)
        else:
            m = pl.program_id(2)

            @pl.when(m == 0)
            def _():
                acc_ref[...] = part

            @pl.when(m > 0)
            def _():
                acc_ref[...] += part

            @pl.when(m == nm - 1)
            def _():
                o_ref[...] = acc_ref[...].astype(o_ref.dtype)

    scratch = (pltpu.VMEM((tk, tn), F32),) if nm > 1 else ()
    return _pcall(
        body, name=name, out_shape=_sds((nb, K, Ns), out_dtype), grid=(K // tk, N // tn, nm),
        in_specs=[pl.BlockSpec((tm, tk), lambda kk, j, m: (m, a0 + kk)),
                  pl.BlockSpec((tm, tn), lambda kk, j, m: (m, c0 + j))],
        out_specs=pl.BlockSpec((None, tk, tn), lambda kk, j, m: (j // nbs, kk, j % nbs)),
        scratch=scratch)(a, c)


def rms_fwd(x, g, *, name, col0=0, width=None, out_dtype=BF16):
    R = x.shape[0]
    width = width or x.shape[1]
    assert col0 % width == 0
    cb = col0 // width
    tr = _pick(R, (512, 256, 128))

    def body(x_ref, g_ref, o_ref):
        xf = x_ref[...].astype(F32)
        r = lax.rsqrt(jnp.mean(xf * xf, axis=1, keepdims=True) + EPS)
        o_ref[...] = (xf * r * g_ref[...]).astype(o_ref.dtype)

    return _pcall(body, name=name, out_shape=_sds((R, width), out_dtype), grid=(R // tr,),
                  in_specs=[pl.BlockSpec((tr, width), lambda i: (i, cb)),
                            pl.BlockSpec((1, width), lambda i: (0, 0))],
                  out_specs=pl.BlockSpec((tr, width), lambda i: (i, 0)))(x, g)


def rms_bwd(x, g, dy, *, name, col0=0, width=None, res=None, dx_dtype=F32, need_dx=True):
    R = x.shape[0]
    width = width or x.shape[1]
    cb = col0 // width
    tr = _pick(R, (512, 256, 128))

    def body(*refs):
        x_ref, g_ref, dy_ref = refs[:3]
        pos = 3
        res_ref = None
        if res is not None:
            res_ref = refs[pos]
            pos += 1
        dx_ref = None
        if need_dx:
            dx_ref = refs[pos]
            pos += 1
        dg_ref = refs[pos]
        xf = x_ref[...].astype(F32)
        dyf = dy_ref[...].astype(F32)
        r = lax.rsqrt(jnp.mean(xf * xf, axis=1, keepdims=True) + EPS)
        xh = xf * r
        if need_dx:
            gy = dyf * g_ref[...]
            dx = r * (gy - xh * jnp.mean(gy * xh, axis=1, keepdims=True))
            if res_ref is not None:
                dx = dx + res_ref[...]
            dx_ref[...] = dx.astype(dx_ref.dtype)
        part = jnp.sum(dyf * xh, axis=0, keepdims=True)

        @pl.when(pl.program_id(0) == 0)
        def _():
            dg_ref[...] = part

        @pl.when(pl.program_id(0) > 0)
        def _():
            dg_ref[...] += part

    in_specs = [pl.BlockSpec((tr, width), lambda i: (i, cb)),
                pl.BlockSpec((1, width), lambda i: (0, 0)),
                pl.BlockSpec((tr, width), lambda i: (i, 0))]
    args = [x, g, dy]
    if res is not None:
        in_specs.append(pl.BlockSpec((tr, width), lambda i: (i, 0)))
        args.append(res)
    dg_shape, dg_spec = _sds((1, width), F32), pl.BlockSpec((1, width), lambda i: (0, 0))
    if need_dx:
        out_shape = (_sds((R, width), dx_dtype), dg_shape)
        out_specs = (pl.BlockSpec((tr, width), lambda i: (i, 0)), dg_spec)
    else:
        out_shape, out_specs = dg_shape, dg_spec
    out = _pcall(body, name=name, out_shape=out_shape, grid=(R // tr,), in_specs=in_specs,
                 out_specs=out_specs)(*args)
    return out if need_dx else (None, out)


def _rope_apply(y, c, sa, sb):
    return y * c + pltpu.roll(y, 96, 1) * sa + pltpu.roll(y, 32, 1) * sb


def _rope_transpose(dy, c, sa, sb):
    return dy * c + pltpu.roll(dy * sa, 32, 1) + pltpu.roll(dy * sb, 96, 1)


def rope_tables(seq):
    pos = jnp.arange(seq, dtype=F32)
    inv = ROPE_THETA ** (-jnp.arange(0, MLA_ROPE, 2, dtype=F32) / MLA_ROPE)
    ang = pos[:, None] * inv[None, :]
    cos, sin = jnp.cos(ang), jnp.sin(ang)
    z32, z64 = jnp.zeros_like(cos), jnp.zeros((seq, 64), F32)
    c = jnp.concatenate([cos, cos, z64], axis=1)
    sa = jnp.concatenate([-sin, z32, z64], axis=1)
    sb = jnp.concatenate([z32, sin, z64], axis=1)
    return c, sa, sb


def _head_vec(part_refs):
    xs = [p[...].astype(F32) for p in part_refs]
    return xs[0] if len(xs) == 1 else jnp.concatenate(xs, axis=1)


def prep_fwd(parts, g, *, name, n_heads, n_real, rope=None):
    rows = parts[0][0].shape[0]
    dh = sum(w for _, w, _ in parts)
    tr = _pick(rows, (512, 256, 128))
    npart = len(parts)

    def body(*refs):
        part_refs, g_ref = refs[:npart], refs[npart]
        pos = npart + 1
        if rope is not None:
            c_ref, sa_ref, sb_ref = refs[pos:pos + 3]
            pos += 3
        o_ref = refs[pos]
        x = _head_vec(part_refs)
        r = lax.rsqrt(jnp.sum(x * x, axis=1, keepdims=True) * (1.0 / n_real) + EPS)
        y = x * r * g_ref[...]
        if rope is not None:
            yr = _rope_apply(y[:, dh - 128:], c_ref[...], sa_ref[...], sb_ref[...])
            y = jnp.concatenate([y[:, :dh - 128], yr], axis=1)
        o_ref[...] = y.astype(o_ref.dtype)

    in_specs, args = [], []
    for arr, w, fn in parts:
        in_specs.append(pl.BlockSpec((tr, w), functools.partial(lambda h, i, fn: (i, fn(h)), fn=fn)))
        args.append(arr)
    in_specs.append(pl.BlockSpec((1, dh), lambda h, i: (0, 0)))
    args.append(g)
    if rope is not None:
        for t in rope:
            in_specs.append(pl.BlockSpec((tr, 128), lambda h, i: (i, 0)))
            args.append(t)
    return _pcall(body, name=name, out_shape=_sds((n_heads, rows, dh), BF16), grid=(n_heads, rows // tr),
                  in_specs=in_specs, out_specs=pl.BlockSpec((None, tr, dh), lambda h, i: (h, i, 0)))(*args)


def _norm_bwd(x, g, dyn, n_real):
    r = lax.rsqrt(jnp.sum(x * x, axis=1, keepdims=True) * (1.0 / n_real) + EPS)
    xh = x * r
    gy = dyn * g
    dx = r * (gy - xh * (jnp.sum(gy * xh, axis=1, keepdims=True) * (1.0 / n_real)))
    return dx, jnp.sum(dyn * xh, axis=0, keepdims=True)


def prep_bwd_q(src, dy, g, *, name, n_heads, dh, n_real, rope=None, out_dtype=BF16):
    rows = src.shape[0]
    tr = _pick(rows, (512, 256, 128))

    def body(*refs):
        x_ref, dy_ref, g_ref = refs[:3]
        pos = 3
        if rope is not None:
            c_ref, sa_ref, sb_ref = refs[pos:pos + 3]
            pos += 3
        dx_ref, dg_ref = refs[pos], refs[pos + 1]
        dyn = dy_ref[...].astype(F32)
        if rope is not None:
            dr = _rope_transpose(dyn[:, dh - 128:], c_ref[...], sa_ref[...], sb_ref[...])
            dyn = jnp.concatenate([dyn[:, :dh - 128], dr], axis=1)
        dx, dg = _norm_bwd(x_ref[...].astype(F32), g_ref[...], dyn, n_real)
        dx_ref[...] = dx.astype(dx_ref.dtype)
        first = jnp.logical_and(pl.program_id(0) == 0, pl.program_id(1) == 0)

        @pl.when(first)
        def _():
            dg_ref[...] = dg

        @pl.when(jnp.logical_not(first))
        def _():
            dg_ref[...] += dg

    in_specs = [pl.BlockSpec((tr, dh), lambda i, h: (i, h)),
                pl.BlockSpec((None, tr, dh), lambda i, h: (h, i, 0)),
                pl.BlockSpec((1, dh), lambda i, h: (0, 0))]
    args = [src, dy, g]
    if rope is not None:
        for t in rope:
            in_specs.append(pl.BlockSpec((tr, 128), lambda i, h: (i, 0)))
            args.append(t)
    return _pcall(body, name=name, out_shape=(_sds((rows, n_heads * dh), out_dtype), _sds((1, dh), F32)),
                  grid=(rows // tr, n_heads), in_specs=in_specs,
                  out_specs=(pl.BlockSpec((tr, dh), lambda i, h: (i, h)),
                             pl.BlockSpec((1, dh), lambda i, h: (0, 0))))(*args)


def prep_bwd_mla_k(kv_raw, zs, dkf, dv, g, rope, *, name):
    rows = kv_raw.shape[0]
    tr = _pick(rows, (512, 256, 128))

    def body(kn_ref, kr_ref, dy_ref, dv_ref, g_ref, c_ref, sa_ref, sb_ref, dkv_ref, dkr_ref, dg_ref):
        h = pl.program_id(1)
        x = jnp.concatenate([kn_ref[...].astype(F32), kr_ref[...].astype(F32)], axis=1)
        dyn = dy_ref[...].astype(F32)
        dr = _rope_transpose(dyn[:, 128:], c_ref[...], sa_ref[...], sb_ref[...])
        dyn = jnp.concatenate([dyn[:, :128], dr], axis=1)
        dx, dg = _norm_bwd(x, g_ref[...], dyn, MLA_QK)
        dkv_ref[...] = jnp.concatenate([dx[:, :128], dv_ref[...].astype(F32)], axis=1).astype(dkv_ref.dtype)

        @pl.when(h == 0)
        def _():
            dkr_ref[...] = dx[:, 128:]

        @pl.when(h > 0)
        def _():
            dkr_ref[...] += dx[:, 128:]

        first = jnp.logical_and(pl.program_id(0) == 0, h == 0)

        @pl.when(first)
        def _():
            dg_ref[...] = dg

        @pl.when(jnp.logical_not(first))
        def _():
            dg_ref[...] += dg

    tab = pl.BlockSpec((tr, 128), lambda i, h: (i, 0))
    return _pcall(
        body, name=name,
        out_shape=(_sds((rows, N_HEADS * 256), BF16), _sds((rows, 128), F32), _sds((1, MLA_PAD), F32)),
        grid=(rows // tr, N_HEADS),
        in_specs=[pl.BlockSpec((tr, 128), lambda i, h: (i, 2 * h)),
                  pl.BlockSpec((tr, 128), lambda i, h: (i, Z_KR // 128)),
                  pl.BlockSpec((None, tr, MLA_PAD), lambda i, h: (h, i, 0)),
                  pl.BlockSpec((None, tr, 128), lambda i, h: (h, i, 0)),
                  pl.BlockSpec((1, MLA_PAD), lambda i, h: (0, 0)), tab, tab, tab],
        out_specs=(pl.BlockSpec((tr, 256), lambda i, h: (i, h)),
                   pl.BlockSpec((tr, 128), lambda i, h: (i, 0)),
                   pl.BlockSpec((1, MLA_PAD), lambda i, h: (0, 0))))(kv_raw, zs, dkf, dv, g, *rope)


def prep_bwd_groups(src, base_blk, dys, gs, *, name, n_heads, kinds, out_dtype=BF16):
    rows = src.shape[0]
    ng = len(kinds)
    J = ng * n_heads
    tr = _pick(rows, (512, 256, 128))
    gstack = jnp.stack([gs[k] if kinds[k] == "norm" else jnp.ones((1, DH), F32) for k in range(ng)])

    def body(*refs):
        x_ref = refs[0]
        dy_refs = refs[1:1 + ng]
        g_ref, dx_ref, dg_ref = refs[1 + ng:4 + ng]
        j, i = pl.program_id(0), pl.program_id(1)
        grp = j // n_heads
        dy = dy_refs[0][...].astype(F32)
        for k in range(1, ng):
            dy = jnp.where(grp == k, dy_refs[k][...].astype(F32), dy)
        dx, dg = _norm_bwd(x_ref[...].astype(F32), g_ref[...], dy, DH)
        is_copy = functools.reduce(jnp.logical_or, [grp == k for k in range(ng) if kinds[k] == "copy"],
                                   jnp.bool_(False))
        dx_ref[...] = jnp.where(is_copy, dy, dx).astype(dx_ref.dtype)
        dg = jnp.where(is_copy, jnp.zeros_like(dg), dg)
        first = jnp.logical_and(j % n_heads == 0, i == 0)

        @pl.when(first)
        def _():
            dg_ref[...] = dg

        @pl.when(jnp.logical_not(first))
        def _():
            dg_ref[...] += dg

    in_specs = [pl.BlockSpec((tr, DH), lambda j, i: (i, base_blk + j))]
    for k in range(ng):
        in_specs.append(pl.BlockSpec(
            (None, tr, DH),
            functools.partial(lambda j, i, k: (jnp.clip(j - k * n_heads, 0, n_heads - 1), i, 0), k=k)))
    in_specs.append(pl.BlockSpec((None, 1, DH), lambda j, i: (j // n_heads, 0, 0)))
    return _pcall(body, name=name, out_shape=(_sds((rows, J * DH), out_dtype), _sds((ng, 1, DH), F32)),
                  grid=(J, rows // tr), in_specs=in_specs,
                  out_specs=(pl.BlockSpec((tr, DH), lambda j, i: (i, j)),
                             pl.BlockSpec((None, 1, DH), lambda j, i: (j // n_heads, 0, 0))))(src, *dys, gstack)


def _attn_cfg(mode, sq, sk):
    if mode == "chunk":
        tq = 128
        win = min((LEFT_CHUNKS + 2) * CHUNK, sk)
    else:
        tq = _pick(sq, (256, 128))
        win = sk
    scale = (MLA_QK if mode == "mla" else DH) ** -0.5
    return tq, win, scale


def _attn_scores(mode, i, tq, win, sk, scale, q, k_ref, cq_ref, ck_ref, t_ref):
    if mode == "chunk":
        start = pl.multiple_of(jnp.clip((i - LEFT_CHUNKS // 2) * 128, 0, sk - win), 128)
        kk = k_ref[pl.ds(start, win), :]
    else:
        start = 0
        kk = k_ref[...]
    s = lax.dot_general(q, kk, (((1,), (1,)), ((), ())), preferred_element_type=F32) * scale
    if mode == "cross":
        return s, start
    t_pos = i * tq + lax.broadcasted_iota(jnp.int32, (tq, win), 0)
    s_pos = start + lax.broadcasted_iota(jnp.int32, (tq, win), 1)
    if mode == "fox":
        s = s + cq_ref[...] - ck_ref[...]
        allowed = s_pos <= t_pos
    else:
        qc, kc = t_pos // CHUNK, s_pos // CHUNK
        allowed = kc <= qc
        if mode == "chunk":
            allowed = jnp.logical_and(allowed, kc >= qc - LEFT_CHUNKS)
            tiles = []
            for w in range(win // 128):
                delta = i - (start // 128 + w)
                tiles.append(jnp.where(delta == 0, t_ref[0], jnp.where(delta == 1, t_ref[1], t_ref[2])))
            s = s + jnp.concatenate(tiles, axis=1)
    return jnp.where(allowed, s, NEG), start


def attn_fwd(q, k, v_arr, v_blk, *, mode, name, n_heads, cq=None, ck=None, tiles=None):
    _, sq, dk = q.shape
    sk = k.shape[1]
    tq, win, scale = _attn_cfg(mode, sq, sk)

    def body(*refs):
        q_ref, k_ref, v_ref = refs[:3]
        pos = 3
        cq_ref = ck_ref = t_ref = None
        if mode == "fox":
            cq_ref, ck_ref = refs[pos:pos + 2]
            pos += 2
        if mode == "chunk":
            t_ref = refs[pos]
            pos += 1
        o_ref, lse_ref = refs[pos], refs[pos + 1]
        i = pl.program_id(1)
        s, start = _attn_scores(mode, i, tq, win, sk, scale, q_ref[...], k_ref, cq_ref, ck_ref, t_ref)
        vv = v_ref[pl.ds(start, win), :] if mode == "chunk" else v_ref[...]
        m = jnp.max(s, axis=1, keepdims=True)
        e = jnp.exp(s - m)
        l = jnp.sum(e, axis=1, keepdims=True)
        p = e * (1.0 / l)
        o_ref[...] = jnp.dot(_mx(p), _mx(vv), preferred_element_type=F32).astype(o_ref.dtype)
        lse_ref[...] = m + jnp.log(l)

    in_specs = [pl.BlockSpec((None, tq, dk), lambda h, i: (h, i, 0)),
                pl.BlockSpec((None, sk, dk), lambda h, i: (h, 0, 0)),
                pl.BlockSpec((sk, DH), lambda h, i: (0, v_blk(h)))]
    args = [q, k, v_arr]
    if mode == "fox":
        in_specs += [pl.BlockSpec((None, tq, 1), lambda h, i: (h, i, 0)),
                     pl.BlockSpec((None, 1, sk), lambda h, i: (h, 0, 0))]
        args += [cq, ck]
    if mode == "chunk":
        in_specs.append(pl.BlockSpec((3, None, 128, 128), lambda h, i: (0, h, 0, 0)))
        args.append(tiles)
    return _pcall(body, name=name,
                  out_shape=(_sds((sq, n_heads * DH), BF16), _sds((n_heads, sq, 1), F32)),
                  grid=(n_heads, sq // tq), in_specs=in_specs,
                  out_specs=(pl.BlockSpec((tq, DH), lambda h, i: (i, h)),
                             pl.BlockSpec((None, tq, 1), lambda h, i: (h, i, 0))))(*args)


def attn_bwd(q, k, v_arr, v_blk, o, do, lse, *, mode, name, n_heads, cq=None, ck=None, tiles=None):
    _, sq, dk = q.shape
    sk = k.shape[1]
    tq, win, scale = _attn_cfg(mode, sq, sk)
    n_extra = {"fox": 2, "chunk": 1}.get(mode, 0)

    def body(*refs):
        q_ref, k_ref, v_ref, o_ref, do_ref, lse_ref = refs[:6]
        pos = 6
        cq_ref = ck_ref = t_ref = None
        if mode == "fox":
            cq_ref, ck_ref = refs[pos:pos + 2]
            pos += 2
        if mode == "chunk":
            t_ref = refs[pos]
            pos += 1
        dq_ref, dk_ref, dv_ref = refs[pos:pos + 3]
        extra = refs[pos + 3:pos + 3 + n_extra]
        i = pl.program_id(1)
        q = q_ref[...]
        do = do_ref[...]
        s, start = _attn_scores(mode, i, tq, win, sk, scale, q, k_ref, cq_ref, ck_ref, t_ref)
        if mode == "chunk":
            rows = pl.ds(start, win)
            kk, vv = k_ref[rows, :], v_ref[rows, :]
        else:
            kk, vv = k_ref[...], v_ref[...]
        p = jnp.exp(s - lse_ref[...])
        drow = jnp.sum(do.astype(F32) * o_ref[...].astype(F32), axis=1, keepdims=True)
        dp = lax.dot_general(do, _mx(vv), (((1,), (1,)), ((), ())), preferred_element_type=F32)
        ds = p * (dp - drow)
        dsb = _mx(ds)
        dq_ref[...] = (jnp.dot(dsb, kk, preferred_element_type=F32) * scale).astype(dq_ref.dtype)
        dk_part = lax.dot_general(dsb, q, (((0,), (0,)), ((), ())), preferred_element_type=F32) * scale
        dv_part = lax.dot_general(_mx(p), do, (((0,), (0,)), ((), ())), preferred_element_type=F32)

        @pl.when(i == 0)
        def _():
            dk_ref[...] = jnp.zeros_like(dk_ref)
            dv_ref[...] = jnp.zeros_like(dv_ref)
            if mode == "fox":
                extra[1][...] = jnp.zeros_like(extra[1])
            if mode == "chunk":
                extra[0][...] = jnp.zeros_like(extra[0])

        if mode == "chunk":
            dk_ref[rows, :] += dk_part
            dv_ref[rows, :] += dv_part
            dt_ref = extra[0]
            for w in range(win // 128):
                delta = i - (start // 128 + w)
                tile = ds[:, w * 128:(w + 1) * 128]
                zero = jnp.zeros_like(tile)
                dt_ref[0] += jnp.where(delta == 0, tile, zero)
                dt_ref[1] += jnp.where(delta == 1, tile, zero)
                dt_ref[2] += jnp.where(delta >= 2, tile, zero)
        else:
            dk_ref[...] += dk_part
            dv_ref[...] += dv_part
        if mode == "fox":
            extra[0][...] = jnp.sum(ds, axis=1, keepdims=True)
            extra[1][...] -= jnp.sum(ds, axis=0, keepdims=True)

    in_specs = [pl.BlockSpec((None, tq, dk), lambda h, i: (h, i, 0)),
                pl.BlockSpec((None, sk, dk), lambda h, i: (h, 0, 0)),
                pl.BlockSpec((sk, DH), lambda h, i: (0, v_blk(h))),
                pl.BlockSpec((tq, DH), lambda h, i: (i, h)),
                pl.BlockSpec((tq, DH), lambda h, i: (i, h)),
                pl.BlockSpec((None, tq, 1), lambda h, i: (h, i, 0))]
    args = [q, k, v_arr, o, do, lse]
    out_shape = [_sds((n_heads, sq, dk), F32), _sds((n_heads, sk, dk), F32), _sds((n_heads, sk, DH), F32)]
    out_specs = [pl.BlockSpec((None, tq, dk), lambda h, i: (h, i, 0)),
                 pl.BlockSpec((None, sk, dk), lambda h, i: (h, 0, 0)),
                 pl.BlockSpec((None, sk, DH), lambda h, i: (h, 0, 0))]
    if mode == "fox":
        in_specs += [pl.BlockSpec((None, tq, 1), lambda h, i: (h, i, 0)),
                     pl.BlockSpec((None, 1, sk), lambda h, i: (h, 0, 0))]
        args += [cq, ck]
        out_shape += [_sds((n_heads, sq, 1), F32), _sds((n_heads, 1, sk), F32)]
        out_specs += [pl.BlockSpec((None, tq, 1), lambda h, i: (h, i, 0)),
                      pl.BlockSpec((None, 1, sk), lambda h, i: (h, 0, 0))]
    if mode == "chunk":
        in_specs.append(pl.BlockSpec((3, None, 128, 128), lambda h, i: (0, h, 0, 0)))
        args.append(tiles)
        out_shape.append(_sds((3, n_heads, 128, 128), F32))
        out_specs.append(pl.BlockSpec((3, None, 128, 128), lambda h, i: (0, h, 0, 0)))
    return _pcall(body, name=name, out_shape=tuple(out_shape), grid=(n_heads, sq // tq),
                  in_specs=in_specs, out_specs=tuple(out_specs))(*args)


REL_LANES = 384
REL_KBLK = 2048


def _rel_onehot(t, k):
    rho = k * REL_KBLK + lax.broadcasted_iota(jnp.int32, (REL_KBLK, REL_LANES), 0)
    lane = lax.broadcasted_iota(jnp.int32, (REL_KBLK, REL_LANES), 1)
    diff = lax.shift_right_logical(rho, 7) - jnp.bitwise_and(rho, 127)
    idx = jnp.where(t == 0, diff + REL_CLIP,
                    jnp.where(t == 1, jnp.minimum(diff + 128, REL_CLIP) + REL_CLIP, N_REL - 1))
    return jnp.where(idx == lane, 1.0, 0.0).astype(F32)


def relbias_tiles(rel_bias):
    nh = rel_bias.shape[0]
    rb = jnp.pad(rel_bias, ((0, 0), (0, REL_LANES - N_REL)))

    def body(rb_ref, o_ref):
        e = _rel_onehot(pl.program_id(0), pl.program_id(1))
        o_ref[...] = lax.dot_general(rb_ref[...], e, (((1,), (1,)), ((), ())), preferred_element_type=F32,
                                     precision=lax.Precision.HIGHEST)

    flat = _pcall(body, name="relbias_tiles", out_shape=_sds((3, nh, 128 * 128), F32),
                  grid=(3, 128 * 128 // REL_KBLK),
                  in_specs=[pl.BlockSpec((nh, REL_LANES), lambda t, k: (0, 0))],
                  out_specs=pl.BlockSpec((None, nh, REL_KBLK), lambda t, k: (t, 0, k)))(rb)
    return flat.reshape(3, nh, 128, 128)


def relbias_tiles_bwd(dtiles):
    nh = dtiles.shape[1]

    def body(dt_ref, o_ref):
        t, k = pl.program_id(0), pl.program_id(1)
        part = jnp.dot(dt_ref[...], _rel_onehot(t, k), preferred_element_type=F32,
                       precision=lax.Precision.HIGHEST)
        first = jnp.logical_and(t == 0, k == 0)

        @pl.when(first)
        def _():
            o_ref[...] = part

        @pl.when(jnp.logical_not(first))
        def _():
            o_ref[...] += part

    out = _pcall(body, name="relbias_tiles_bwd", out_shape=_sds((nh, REL_LANES), F32),
                 grid=(3, 128 * 128 // REL_KBLK),
                 in_specs=[pl.BlockSpec((None, nh, REL_KBLK), lambda t, k: (t, 0, k))],
                 out_specs=pl.BlockSpec((nh, REL_LANES), lambda t, k: (0, 0)))(dtiles.reshape(3, nh, 128 * 128))
    return out[:, :N_REL]


CUM_BLK = 256


def _tri(n, lower):
    r = lax.broadcasted_iota(jnp.int32, (n, n), 0)
    c = lax.broadcasted_iota(jnp.int32, (n, n), 1)
    return jnp.where(r >= c if lower else r <= c, 1.0, 0.0).astype(F32)


def fox_cum_fwd(zs, bf):
    S = zs.shape[0]
    tb = min(CUM_BLK, S)

    def body(f_ref, b_ref, cum_ref, cumt_ref, carry_ref):
        @pl.when(pl.program_id(0) == 0)
        def _():
            carry_ref[...] = jnp.zeros_like(carry_ref)

        x = f_ref[...] + b_ref[...]
        lane = lax.broadcasted_iota(jnp.int32, x.shape, 1)
        logf = jnp.where(lane < N_HEADS, jnp.minimum(x, 0.0) - jnp.log(1.0 + jnp.exp(-jnp.abs(x))), 0.0)
        cum = jnp.dot(_tri(tb, True), logf, preferred_element_type=F32,
                      precision=lax.Precision.HIGHEST) + carry_ref[...]
        carry_ref[...] = cum[tb - 1:tb, :]
        cum_ref[...] = cum
        cumt_ref[...] = cum.T

    return _pcall(body, name="fox_cum_fwd", out_shape=(_sds((S, 128), F32), _sds((128, S), F32)),
                  grid=(S // tb,),
                  in_specs=[pl.BlockSpec((tb, 128), lambda i: (i, Z_FF // 128)),
                            pl.BlockSpec((1, 128), lambda i: (0, 0))],
                  out_specs=(pl.BlockSpec((tb, 128), lambda i: (i, 0)),
                             pl.BlockSpec((128, tb), lambda i: (0, i))),
                  scratch=(pltpu.VMEM((1, 128), F32),))(zs, bf)


def fox_cum_bwd(zs, bf, dcum):
    S = zs.shape[0]
    tb = min(CUM_BLK, S)
    nblk = S // tb

    def body(f_ref, b_ref, d_ref, df_ref, db_ref, carry_ref):
        @pl.when(pl.program_id(0) == 0)
        def _():
            carry_ref[...] = jnp.zeros_like(carry_ref)
            db_ref[...] = jnp.zeros_like(db_ref)

        d = d_ref[...]
        dlogf = jnp.dot(_tri(tb, False), d, preferred_element_type=F32,
                        precision=lax.Precision.HIGHEST) + carry_ref[...]
        carry_ref[...] += jnp.sum(d, axis=0, keepdims=True)
        x = f_ref[...] + b_ref[...]
        lane = lax.broadcasted_iota(jnp.int32, x.shape, 1)
        dx = jnp.where(lane < N_HEADS, dlogf / (1.0 + jnp.exp(x)), 0.0)
        df_ref[...] = dx
        db_ref[...] += jnp.sum(dx, axis=0, keepdims=True)

    return _pcall(body, name="fox_cum_bwd", out_shape=(_sds((S, 128), F32), _sds((1, 128), F32)),
                  grid=(nblk,),
                  in_specs=[pl.BlockSpec((tb, 128), lambda i: (nblk - 1 - i, Z_FF // 128)),
                            pl.BlockSpec((1, 128), lambda i: (0, 0)),
                            pl.BlockSpec((tb, 128), lambda i: (nblk - 1 - i, 0))],
                  out_specs=(pl.BlockSpec((tb, 128), lambda i: (nblk - 1 - i, 0)),
                             pl.BlockSpec((1, 128), lambda i: (0, 0))),
                  scratch=(pltpu.VMEM((1, 128), F32),))(zs, bf, dcum)


def _sigmoid(x):
    return 1.0 / (1.0 + jnp.exp(-x))


def merge_fwd(z, projs):
    S = z.shape[0]
    tr, tc = _pick(S, (512, 256, 128)), 512
    nbc = D_MODEL // tc
    g0 = Z_GATE // tc

    def body(g0_ref, g1_ref, g2_ref, p0_ref, p1_ref, p2_ref, o_ref):
        acc = _sigmoid(g0_ref[...]) * p0_ref[...]
        acc += _sigmoid(g1_ref[...]) * p1_ref[...]
        acc += _sigmoid(g2_ref[...]) * p2_ref[...]
        o_ref[...] = acc.astype(o_ref.dtype)

    gspecs = [pl.BlockSpec((tr, tc), functools.partial(lambda i, j, n: (i, g0 + n * nbc + j), n=n))
              for n in range(3)]
    pspec = pl.BlockSpec((tr, tc), lambda i, j: (i, j))
    return _pcall(body, name="merge_fwd", out_shape=_sds((S, D_MODEL), BF16), grid=(S // tr, nbc),
                  in_specs=gspecs + [pspec] * 3, out_specs=pspec)(z, z, z, *projs)


def merge_bwd(z, projs, dmerged):
    S = z.shape[0]
    tr, tc = _pick(S, (512, 256, 128)), 512
    nbc = D_MODEL // tc
    g0 = Z_GATE // tc

    def body(g0_ref, g1_ref, g2_ref, p0_ref, p1_ref, p2_ref, dm_ref, dg0, dg1, dg2, dp0, dp1, dp2):
        dm = dm_ref[...]
        for g_ref, p_ref, dg_ref, dp_ref in ((g0_ref, p0_ref, dg0, dp0), (g1_ref, p1_ref, dg1, dp1),
                                             (g2_ref, p2_ref, dg2, dp2)):
            sg = _sigmoid(g_ref[...])
            dp_ref[...] = (dm * sg).astype(dp_ref.dtype)
            dg_ref[...] = (dm * p_ref[...] * sg * (1.0 - sg)).astype(dg_ref.dtype)

    gspecs = [pl.BlockSpec((tr, tc), functools.partial(lambda i, j, n: (i, g0 + n * nbc + j), n=n))
              for n in range(3)]
    pspec = pl.BlockSpec((tr, tc), lambda i, j: (i, j))
    out = _pcall(body, name="merge_bwd", out_shape=tuple(_sds((S, D_MODEL), BF16) for _ in range(6)),
                 grid=(S // tr, nbc), in_specs=gspecs + [pspec] * 4,
                 out_specs=tuple([pspec] * 6))(z, z, z, *projs, dmerged)
    return out[:3], out[3:]


def loss_head(y, target):
    S, D = y.shape
    tr = _pick(S, (256, 128))

    def body(y_ref, t_ref, dy_ref, l_ref):
        e = y_ref[...] - t_ref[...]
        dy_ref[...] = e * (1.0 / D)
        part = jnp.sum(jnp.sum(e * e, axis=1, keepdims=True), axis=0, keepdims=True) * (0.5 / D)

        @pl.when(pl.program_id(0) == 0)
        def _():
            l_ref[...] = part

        @pl.when(pl.program_id(0) > 0)
        def _():
            l_ref[...] += part

    spec = pl.BlockSpec((tr, D), lambda i: (i, 0))
    return _pcall(body, name="loss_head", out_shape=(_sds((S, D), F32), _sds((1, 1), F32)), grid=(S // tr,),
                  in_specs=[spec, spec], out_specs=(spec, pl.BlockSpec((1, 1), lambda i: (0, 0))))(y, target)


ADAMW_BLOCK_BYTES = 1024 * 1024


def adamw(w, gs, m, v, *, name):
    L, R, C = w.shape
    tr = _pick(R, (512, 256, 128, 64, 32, 16, 8))
    while tr * C * 4 > ADAMW_BLOCK_BYTES and tr % 16 == 0:
        tr //= 2
    c1 = 1.0 - ADAM_B1 ** ADAM_STEP
    c2 = 1.0 - ADAM_B2 ** ADAM_STEP

    def body(*refs):
        w_ref, m_ref, v_ref = refs[:3]
        g_refs = refs[3:3 + L]
        go_ref, d_ref, nm_ref, nv_ref = refs[3 + L:]
        l = pl.program_id(0)
        g_ = g_refs[0][...]
        for k in range(1, L):
            g_ = jnp.where(l == k, g_refs[k][...], g_)
        nm = ADAM_B1 * m_ref[...] + (1.0 - ADAM_B1) * g_
        nv = ADAM_B2 * v_ref[...] + (1.0 - ADAM_B2) * (g_ * g_)
        go_ref[...] = g_
        nm_ref[...] = nm
        nv_ref[...] = nv
        d_ref[...] = -ADAM_LR * ((nm / c1) / (jnp.sqrt(nv / c2) + ADAM_EPS) + ADAM_WD * w_ref[...])

    spec = pl.BlockSpec((None, tr, C), lambda l, i: (l, i, 0))
    gspecs = [pl.BlockSpec((tr, C), functools.partial(lambda l, i, k: (jnp.where(l == k, i, 0), 0), k=k))
              for k in range(L)]
    return _pcall(body, name=name, out_shape=tuple(_sds((L, R, C), F32) for _ in range(4)),
                  grid=(L, R // tr), in_specs=[spec] * 3 + gspecs, out_specs=(spec,) * 4)(w, m, v, *gs)


SHARD_SHAPES = {
    "w_uq": (Q_LORA, 384), "w_ukv": (KV_LORA, 512), "w_br": (3, MIX_W, 512), "w_out": (512, D_MODEL),
    "w_xq": (512, 512), "w_xkv": (512, 1024), "w_xo": (512, 512), "w_1": (D_MODEL, 2048),
    "w_2": (2048, D_MODEL), "w_in": (D_MODEL, 3282),
}
def _rows2d(a, name):
    shp = SHARD_SHAPES[name]
    return a.reshape(a.shape[:a.ndim - len(shp)] + (-1, shp[-1]))


def _cols_from_shards(g):
    return jnp.transpose(g, (1, 0, 2)).reshape(g.shape[1], 4 * g.shape[2])


def _cols_to_shards(w):
    return jnp.transpose(w.reshape(w.shape[0], 4, w.shape[1] // 4), (1, 0, 2))


def full_weights(g):
    c = W_IN_CUTS
    w_in = _cols_from_shards(g["w_in"])
    zeros = lambda n: jnp.zeros((D_MODEL, n), w_in.dtype)
    w_in_p = jnp.concatenate([w_in[:, c[0]:c[3]], zeros(64), w_in[:, c[4]:c[5]], zeros(120),
                              w_in[:, c[3]:c[4]], w_in[:, c[5]:c[7]]], axis=1)
    w_uq = _cols_from_shards(g["w_uq"]).reshape(Q_LORA, N_HEADS, MLA_QK)
    w_uq_p = jnp.pad(w_uq, ((0, 0), (0, 0), (0, MLA_PAD - MLA_QK))).reshape(1, Q_LORA, N_HEADS * MLA_PAD)
    return dict(
        in_p=w_in_p[None], uq_p=w_uq_p, ukv=g["w_ukv"], br=[g["w_br"][:, n] for n in range(3)],
        out=g["w_out"].reshape(1, D_MODEL, D_MODEL), xq=g["w_xq"].reshape(1, D_MODEL, 512),
        xkv=g["w_xkv"].reshape(1, D_MODEL, 1024), xo=g["w_xo"], w1=g["w_1"],
        w2=g["w_2"].reshape(1, D_FF, D_MODEL))


def shard_grads(dw):
    dp = dw["in_p"][0]
    d_in = jnp.concatenate([dp[:, 0:832], dp[:, Z_FOX:Z_CH], dp[:, Z_FF:Z_FF + 8], dp[:, Z_CH:Z_TOT]], axis=1)
    d_uq = dw["uq_p"].reshape(Q_LORA, N_HEADS, MLA_PAD)[:, :, :MLA_QK].reshape(Q_LORA, N_HEADS * MLA_QK)
    return {
        "w_in": _cols_to_shards(d_in), "w_uq": _cols_to_shards(d_uq), "w_ukv": dw["ukv"],
        "w_br": jnp.stack(dw["br"], axis=1), "w_out": dw["out"].reshape(4, 512, D_MODEL),
        "w_xq": dw["xq"].reshape(4, 512, 512), "w_xkv": dw["xkv"].reshape(4, 512, 1024), "w_xo": dw["xo"],
        "w_1": dw["w1"], "w_2": dw["w2"].reshape(4, 2048, D_MODEL)}


def layer_params(d, l):
    row = lambda v: v.reshape(1, -1).astype(F32)
    padto = lambda v, n: jnp.pad(row(v), ((0, 0), (0, n - v.shape[-1])))
    return dict(
        g_mix=row(d["g_mix"][l]), g_cq=row(d["g_cq"][l]), g_ckv=row(d["g_ckv"][l]),
        g_mla_q=padto(d["g_mla_q"][l], MLA_PAD), g_mla_k=padto(d["g_mla_k"][l], MLA_PAD),
        b_f=padto(d["b_f"][l], 128), g_fox_q=row(d["g_fox_q"][l]), g_fox_k=row(d["g_fox_k"][l]),
        rel_bias=d["rel_bias"][l].astype(F32), g_ch_q=row(d["g_ch_q"][l]), g_ch_k=row(d["g_ch_k"][l]),
        g_cross=row(d["g_cross"][l]), g_mem=row(d["g_mem"][l]), g_x_q=row(d["g_x_q"][l]),
        g_x_k=row(d["g_x_k"][l]), g_mlp=row(d["g_mlp"][l]))


FOX_B = Z_FOX // 128
CH_B = Z_CH // 128


def layer_fwd(x, mem, W, P, rope):
    S = x.shape[0]
    s = {}
    s["h"] = rms_fwd(x, P["g_mix"], name="rms_d")
    z = s["z"] = mm_nn(s["h"], W["in_p"], name="mm_in", out_dtype=F32)
    s["cq_n"] = rms_fwd(z, P["g_cq"], col0=Z_CQ, width=Q_LORA, name="rms_cq")
    s["ckv_n"] = rms_fwd(z, P["g_ckv"], col0=Z_CKV, width=KV_LORA, name="rms_ckv")
    s["q_raw"] = mm_nn(s["cq_n"], W["uq_p"], name="mm_uq", out_dtype=F32)
    s["kv_raw"] = mm_nn(s["ckv_n"], W["ukv"], name="mm_ukv", out_dtype=F32)
    s["qa"] = prep_fwd([(s["q_raw"], MLA_PAD, lambda h: h)], P["g_mla_q"], name="prep_mla_q",
                       n_heads=N_HEADS, n_real=MLA_QK, rope=rope)
    s["ka"] = prep_fwd([(s["kv_raw"], 128, lambda h: 2 * h), (z, 128, lambda h: Z_KR // 128)], P["g_mla_k"],
                       name="prep_mla_k", n_heads=N_HEADS, n_real=MLA_QK, rope=rope)
    s["o_a"], s["lse_a"] = attn_fwd(s["qa"], s["ka"], s["kv_raw"], lambda h: 2 * h + 1, mode="mla",
                                    name="attn_mla", n_heads=N_HEADS)
    s["qb"] = prep_fwd([(z, DH, lambda h: FOX_B + h)], P["g_fox_q"], name="prep_h", n_heads=N_HEADS, n_real=DH)
    s["kb"] = prep_fwd([(z, DH, lambda h: FOX_B + N_HEADS + h)], P["g_fox_k"], name="prep_h",
                       n_heads=N_HEADS, n_real=DH)
    _, cum_t = fox_cum_fwd(z, P["b_f"])
    s["cq"] = cum_t[:N_HEADS].reshape(N_HEADS, S, 1)
    s["ck"] = cum_t[:N_HEADS].reshape(N_HEADS, 1, S)
    s["o_b"], s["lse_b"] = attn_fwd(s["qb"], s["kb"], z, lambda h: FOX_B + 2 * N_HEADS + h, mode="fox",
                                    name="attn_fox", n_heads=N_HEADS, cq=s["cq"], ck=s["ck"])
    s["qc"] = prep_fwd([(z, DH, lambda h: CH_B + h)], P["g_ch_q"], name="prep_h", n_heads=N_HEADS, n_real=DH)
    s["kc"] = prep_fwd([(z, DH, lambda h: CH_B + N_HEADS + h)], P["g_ch_k"], name="prep_h",
                       n_heads=N_HEADS, n_real=DH)
    s["tiles"] = relbias_tiles(P["rel_bias"])
    s["o_c"], s["lse_c"] = attn_fwd(s["qc"], s["kc"], z, lambda h: CH_B + 2 * N_HEADS + h, mode="chunk",
                                    name="attn_chunk", n_heads=N_HEADS, tiles=s["tiles"])
    s["projs"] = [mm_nn(o, W["br"][n], name="mm_br", out_dtype=F32)
                  for n, o in enumerate((s["o_a"], s["o_b"], s["o_c"]))]
    s["merged"] = merge_fwd(z, s["projs"])
    x1 = s["x1"] = mm_nn(s["merged"], W["out"], name="mm_out", out_dtype=F32, res=x)
    s["hq"] = rms_fwd(x1, P["g_cross"], name="rms_d")
    s["xq_raw"] = mm_nn(s["hq"], W["xq"], name="mm_xq", out_dtype=F32)
    s["mem_n"] = rms_fwd(mem, P["g_mem"], name="rms_mem")
    s["mkv"] = mm_nn(s["mem_n"], W["xkv"], name="mm_xkv", out_dtype=F32)
    s["qx"] = prep_fwd([(s["xq_raw"], DH, lambda h: h)], P["g_x_q"], name="prep_xq", n_heads=X_HEADS, n_real=DH)
    s["kx"] = prep_fwd([(s["mkv"], DH, lambda h: h)], P["g_x_k"], name="prep_xk", n_heads=X_HEADS, n_real=DH)
    s["o_x"], s["lse_x"] = attn_fwd(s["qx"], s["kx"], s["mkv"], lambda h: X_HEADS + h, mode="cross",
                                    name="attn_cross", n_heads=X_HEADS)
    x2 = s["x2"] = mm_nn(s["o_x"], W["xo"], name="mm_xo", out_dtype=F32, res=x1)
    s["hm"] = rms_fwd(x2, P["g_mlp"], name="rms_d")
    s["a1"], s["act"] = mm_nn(s["hm"], W["w1"], name="mm_w1", out_dtype=BF16, relu2=True)
    x3 = mm_nn(s["act"], W["w2"], name="mm_w2", out_dtype=F32, res=x2)
    return x3, s


def layer_bwd(g, x, mem, W, P, rope, s):
    S = x.shape[0]
    z = s["z"]
    dw, ds = {}, {}
    da1 = mm_nt(g, W["w2"], name="mm_w2_dx", out_dtype=BF16, relu_mul=s["a1"])
    dw["w2"] = mm_tn(s["act"], g, nb=1, name="mm_w2_dw", out_dtype=BF16)
    dhm = mm_nt(da1, W["w1"], name="mm_w1_dx", out_dtype=F32)
    dw["w1"] = mm_tn(s["hm"], da1, nb=4, name="mm_w1_dw", out_dtype=BF16)
    g2, ds["g_mlp"] = rms_bwd(s["x2"], P["g_mlp"], dhm, name="rms_d_bwd", res=g)
    do_x = mm_nt(g2, W["xo"], name="mm_xo_dx", out_dtype=BF16)
    dw["xo"] = mm_tn(s["o_x"], g2, nb=4, name="mm_xo_dw", out_dtype=BF16)
    dqx, dkx, dvx = attn_bwd(s["qx"], s["kx"], s["mkv"], lambda h: X_HEADS + h, s["o_x"], do_x, s["lse_x"],
                             mode="cross", name="attn_cross_bwd", n_heads=X_HEADS)
    dxq_raw, ds["g_x_q"] = prep_bwd_q(s["xq_raw"], dqx, P["g_x_q"], name="prep_xq_bwd", n_heads=X_HEADS,
                                      dh=DH, n_real=DH)
    dmkv, dgk = prep_bwd_groups(s["mkv"], 0, [dkx, dvx], [P["g_x_k"], None], name="prep_xkv_bwd",
                                n_heads=X_HEADS, kinds=("norm", "copy"))
    ds["g_x_k"] = dgk[0]
    dhq = mm_nt(dxq_raw, W["xq"], name="mm_xq_dx", out_dtype=F32)
    dw["xq"] = mm_tn(s["hq"], dxq_raw, nb=1, name="mm_xq_dw", out_dtype=BF16)
    dmem_n = mm_nt(dmkv, W["xkv"], name="mm_xkv_dx", out_dtype=F32)
    dw["xkv"] = mm_tn(s["mem_n"], dmkv, nb=1, name="mm_xkv_dw", out_dtype=BF16)
    _, ds["g_mem"] = rms_bwd(mem, P["g_mem"], dmem_n, name="rms_mem_bwd", need_dx=False)
    g1, ds["g_cross"] = rms_bwd(s["x1"], P["g_cross"], dhq, name="rms_d_bwd", res=g2)
    dmerged = mm_nt(g1, W["out"], name="mm_out_dx", out_dtype=F32)
    dw["out"] = mm_tn(s["merged"], g1, nb=1, name="mm_out_dw", out_dtype=BF16)
    dgl, dproj = merge_bwd(z, s["projs"], dmerged)
    outs = (s["o_a"], s["o_b"], s["o_c"])
    do = [mm_nt(dproj[n], W["br"][n], name="mm_br_dx", out_dtype=BF16) for n in range(3)]
    dw["br"] = [mm_tn(outs[n], dproj[n], nb=4, name="mm_br_dw", out_dtype=BF16) for n in range(3)]
    dqc, dkc, dvc, dtiles = attn_bwd(s["qc"], s["kc"], z, lambda h: CH_B + 2 * N_HEADS + h, s["o_c"], do[2],
                                     s["lse_c"], mode="chunk", name="attn_chunk_bwd", n_heads=N_HEADS,
                                     tiles=s["tiles"])
    d_ch, dg_ch = prep_bwd_groups(z, CH_B, [dqc, dkc, dvc], [P["g_ch_q"], P["g_ch_k"], None],
                                  name="prep_h_bwd", n_heads=N_HEADS, kinds=("norm", "norm", "copy"))
    ds["g_ch_q"], ds["g_ch_k"] = dg_ch[0], dg_ch[1]
    ds["rel_bias"] = relbias_tiles_bwd(dtiles)
    dqb, dkb, dvb, dcq, dck = attn_bwd(s["qb"], s["kb"], z, lambda h: FOX_B + 2 * N_HEADS + h, s["o_b"], do[1],
                                       s["lse_b"], mode="fox", name="attn_fox_bwd", n_heads=N_HEADS,
                                       cq=s["cq"], ck=s["ck"])
    d_fox, dg_fox = prep_bwd_groups(z, FOX_B, [dqb, dkb, dvb], [P["g_fox_q"], P["g_fox_k"], None],
                                    name="prep_h_bwd", n_heads=N_HEADS, kinds=("norm", "norm", "copy"))
    ds["g_fox_q"], ds["g_fox_k"] = dg_fox[0], dg_fox[1]
    dcum = jnp.pad((dcq[:, :, 0] + dck[:, 0, :]).T, ((0, 0), (0, 128 - N_HEADS)))
    dff, dbf = fox_cum_bwd(z, P["b_f"], dcum)
    ds["b_f"] = dbf[:, :N_HEADS]
    dqa, dka, dva = attn_bwd(s["qa"], s["ka"], s["kv_raw"], lambda h: 2 * h + 1, s["o_a"], do[0], s["lse_a"],
                             mode="mla", name="attn_mla_bwd", n_heads=N_HEADS)
    dq_raw, dgq = prep_bwd_q(s["q_raw"], dqa, P["g_mla_q"], name="prep_mla_q_bwd", n_heads=N_HEADS,
                             dh=MLA_PAD, n_real=MLA_QK, rope=rope)
    dkv_raw, dkr, dgk = prep_bwd_mla_k(s["kv_raw"], z, dka, dva, P["g_mla_k"], rope, name="prep_mla_k_bwd")
    ds["g_mla_q"], ds["g_mla_k"] = dgq[:, :MLA_QK], dgk[:, :MLA_QK]
    dcq_n = mm_nt(dq_raw, W["uq_p"], name="mm_uq_dx", out_dtype=F32)
    dw["uq_p"] = mm_tn(s["cq_n"], dq_raw, nb=1, name="mm_uq_dw", out_dtype=BF16)
    dckv_n = mm_nt(dkv_raw, W["ukv"], name="mm_ukv_dx", out_dtype=F32)
    dw["ukv"] = mm_tn(s["ckv_n"], dkv_raw, nb=4, name="mm_ukv_dw", out_dtype=BF16)
    d_cq, ds["g_cq"] = rms_bwd(z, P["g_cq"], dcq_n, name="rms_cq_bwd", col0=Z_CQ, width=Q_LORA, dx_dtype=BF16)
    d_ckv, ds["g_ckv"] = rms_bwd(z, P["g_ckv"], dckv_n, name="rms_ckv_bwd", col0=Z_CKV, width=KV_LORA,
                                 dx_dtype=BF16)
    dz = jnp.concatenate([d_cq, d_ckv, dkr.astype(BF16), dff.astype(BF16), d_fox, d_ch, *dgl], axis=1)
    dh = mm_nt(dz, W["in_p"], name="mm_in_dx", out_dtype=F32)
    dw["in_p"] = mm_tn(s["h"], dz, nb=1, name="mm_in_dw", out_dtype=BF16)
    g0, ds["g_mix"] = rms_bwd(x, P["g_mix"], dh, name="rms_d_bwd", res=g1)
    return g0, dw, ds


def local_step(x, mem, target, Ws, Ps):
    rope = rope_tables(x.shape[0])
    saved, xs = [], [x]
    for W, P in zip(Ws, Ps):
        y, s = layer_fwd(xs[-1], mem, W, P, rope)
        xs.append(y)
        saved.append(s)
    g, loss = loss_head(xs[-1], target)
    dws, dss = [], []
    for l in reversed(range(len(Ws))):
        g, dw, ds = layer_bwd(g, xs[l], mem, Ws[l], Ps[l], rope, saved[l])
        dws.append(dw)
        dss.append(ds)
    return loss, g, dws[::-1], dss[::-1]


HBM_SPEC = pl.BlockSpec(memory_space=pltpu.HBM)


def _place():
    return lax.axis_index("x"), lax.axis_index("y"), lax.axis_index("c")


def _other_chips(x, y):
    return [(1 - x, y), (x, 1 - y), (1 - x, 1 - y)]


def _remote(src, dst, send_sems, recv_sems, k, to):
    return pltpu.make_async_remote_copy(src_ref=src, dst_ref=dst, send_sem=send_sems.at[k],
                                        recv_sem=recv_sems.at[k], device_id=to, device_id_type=MESH)


def _comm_call(body, *, name, ins, out_shapes, n_sem):
    return pl.pallas_call(
        body, out_shape=tuple(out_shapes), in_specs=[HBM_SPEC] * len(ins),
        out_specs=tuple([HBM_SPEC] * len(out_shapes)),
        scratch_shapes=[pltpu.SemaphoreType.DMA((n_sem,)), pltpu.SemaphoreType.DMA((n_sem,))],
        name=name, interpret=False)(*ins)


def allgather_weights(shards):
    n = len(shards)

    def body(*refs):
        srcs, outs = refs[:n], refs[n:2 * n]
        send_sems, recv_sems = refs[2 * n:]
        x, y, c = _place()
        sibling = (x, y, 1 - c)
        chips = _other_chips(x, y)

        def half(i, shard, hc):
            h = srcs[i].shape[0] // 2
            return outs[i].at[shard, pl.ds(hc * h, h), :]

        first, passed = [], []
        for i in range(n):
            h = srcs[i].shape[0] // 2
            my_half = srcs[i].at[pl.ds(c * h, h), :]
            for j, chip in enumerate(chips):
                cp = _remote(my_half, half(i, 2 * x + y, c), send_sems, recv_sems, 6 * i + j, (*chip, c))
                cp.start()
                first.append(cp)
        for i in range(n):
            for j, (cx, cy) in enumerate(chips):
                landed = half(i, 2 * cx + cy, c)
                _remote(landed, landed, send_sems, recv_sems, 6 * i + j, (cx, cy, c)).wait_recv()
                fw = _remote(landed, landed, send_sems, recv_sems, 6 * i + 3 + j, sibling)
                fw.start()
                passed.append(fw)
        for i in range(n):
            for j, (cx, cy) in enumerate(chips):
                theirs = half(i, 2 * cx + cy, 1 - c)
                _remote(theirs, theirs, send_sems, recv_sems, 6 * i + 3 + j, sibling).wait_recv()
        for cp in first + passed:
            cp.wait_send()

    return _comm_call(body, name="allgather_weights", ins=shards,
                      out_shapes=[_sds((4,) + s.shape, s.dtype) for s in shards], n_sem=6 * n)


def rs_to_sibling(parts):
    n = len(parts)

    def body(*refs):
        srcs, outs = refs[:n], refs[n:2 * n]
        send_sems, recv_sems = refs[2 * n:]
        x, y, c = _place()
        sends = []
        for i in range(n):
            h = srcs[i].shape[1] // 2
            cp = _remote(srcs[i].at[:, pl.ds((1 - c) * h, h), :], outs[i], send_sems, recv_sems, i, (x, y, 1 - c))
            cp.start()
            sends.append(cp)
        for cp in sends:
            cp.wait()

    return _comm_call(body, name="rs_to_sibling", ins=parts, n_sem=n,
                      out_shapes=[_sds((4, p.shape[1] // 2, p.shape[2]), p.dtype) for p in parts])


def rs_to_owner(qs):
    n = len(qs)

    def body(*refs):
        srcs, outs = refs[:n], refs[n:2 * n]
        send_sems, recv_sems = refs[2 * n:]
        x, y, c = _place()
        sends = []
        for i in range(n):
            for j, (cx, cy) in enumerate(_other_chips(x, y)):
                cp = _remote(srcs[i].at[2 * cx + cy], outs[i].at[2 * x + y], send_sems, recv_sems, 3 * i + j,
                             (cx, cy, c))
                cp.start()
                sends.append(cp)
        for cp in sends:
            cp.wait()

    return _comm_call(body, name="rs_to_owner", ins=qs, n_sem=3 * n,
                      out_shapes=[_sds(q.shape, q.dtype) for q in qs])


def rs_share_halves(ts):
    n = len(ts)

    def body(*refs):
        srcs, outs = refs[:n], refs[n:2 * n]
        send_sems, recv_sems = refs[2 * n:]
        x, y, c = _place()
        sends = []
        for i in range(n):
            cp = _remote(srcs[i], outs[i].at[c], send_sems, recv_sems, i, (x, y, 1 - c))
            cp.start()
            sends.append(cp)
        for cp in sends:
            cp.wait()

    return _comm_call(body, name="rs_share_halves", ins=ts, n_sem=n,
                      out_shapes=[_sds((2,) + t.shape, t.dtype) for t in ts])


def add_pair(part, got, half_idx):
    _, a, b = part.shape
    h = a // 2
    tr = _pick(h, (512, 256, 128))

    def body(c_ref, p_ref, g_ref, o_ref):
        o_ref[...] = (p_ref[...].astype(F32) + g_ref[...].astype(F32)).astype(o_ref.dtype)

    spec = pl.BlockSpec((None, tr, b), lambda s_, i, c_ref: (s_, i, 0))
    grid_spec = pltpu.PrefetchScalarGridSpec(
        num_scalar_prefetch=1, grid=(4, h // tr),
        in_specs=[pl.BlockSpec((None, None, tr, b), lambda s_, i, c_ref: (s_, c_ref[0], i, 0)), spec],
        out_specs=spec)
    return pl.pallas_call(
        body, out_shape=_sds((4, h, b), part.dtype), grid_spec=grid_spec, name="rs_add_pair", interpret=False,
        compiler_params=pltpu.CompilerParams(vmem_limit_bytes=VMEM_LIMIT_BYTES))(
            half_idx, part.reshape(4, 2, h, b), got)


def sum_slots(r):
    _, h, b = r.shape
    tr = _pick(h, (512, 256, 128))

    def body(r0, r1, r2, r3, o_ref):
        o_ref[...] = ((r0[...].astype(F32) + r1[...].astype(F32)) + r2[...].astype(F32)) + r3[...].astype(F32)

    specs = [pl.BlockSpec((None, tr, b), functools.partial(lambda i, s_: (s_, i, 0), s_=s_)) for s_ in range(4)]
    return _pcall(body, name="rs_sum_slots", out_shape=_sds((h, b), F32), grid=(h // tr,),
                  in_specs=specs, out_specs=pl.BlockSpec((tr, b), lambda i: (i, 0)))(r, r, r, r)


def reduce_scatter_grads(parts, chip, core):
    half_idx = core.reshape(1).astype(jnp.int32)
    gots = rs_to_sibling(parts)
    chip_sums = [add_pair(p, g, half_idx) for p, g in zip(parts, gots)]
    slots = rs_to_owner(chip_sums)
    slots = [lax.dynamic_update_slice(s_, lax.dynamic_index_in_dim(q, chip, 0, keepdims=True), (chip, 0, 0))
             for s_, q in zip(slots, chip_sums)]
    halves = [sum_slots(s_) for s_ in slots]
    both = rs_share_halves(halves)
    both = [lax.dynamic_update_slice(o, t[None], (core, 0, 0)) for o, t in zip(both, halves)]
    return [o.reshape(p.shape[1], p.shape[2]) for o, p in zip(both, parts)]


def allreduce_small(v):
    Rs = v.shape[0]

    def body(v_ref, o_ref, buf, send_sems, recv_sems):
        x, y, c = _place()
        me = 4 * x + 2 * y + c
        buf[me] = v_ref[...]
        flips = [(fx, fy, fc) for fx in (0, 1) for fy in (0, 1) for fc in (0, 1)][1:]
        sends = []
        for k, (fx, fy, fc) in enumerate(flips):
            to = ((1 - x) if fx else x, (1 - y) if fy else y, (1 - c) if fc else c)
            cp = _remote(v_ref, buf.at[me], send_sems, recv_sems, k, to)
            cp.start()
            sends.append(cp)
        for cp in sends:
            cp.wait()
        acc = buf[0]
        for d in range(1, 8):
            acc = acc + buf[d]
        o_ref[...] = acc

    vm = pl.BlockSpec(memory_space=pltpu.VMEM)
    return pl.pallas_call(
        body, out_shape=_sds((Rs, 128), F32), in_specs=[vm], out_specs=vm,
        scratch_shapes=[pltpu.VMEM((8, Rs, 128), F32), pltpu.SemaphoreType.DMA((7,)),
                        pltpu.SemaphoreType.DMA((7,))],
        name="allreduce_small", interpret=False)(v)


INPUT_NAMES = (("x", "mem") + WEIGHT_ORDER + ("loss_target",) + tuple("m_" + n for n in WEIGHT_ORDER)
               + tuple("v_" + n for n in WEIGHT_ORDER))


def _pack_small(vals, n_layers, extra=None):
    flat = jnp.concatenate([vals[n].reshape(n_layers, -1).astype(F32) for n in SMALL_ORDER], axis=1).reshape(-1)
    if extra is not None:
        flat = jnp.concatenate([flat, extra.reshape(-1)])
    n = flat.shape[0]
    rows = -(-n // 1024) * 8
    return jnp.pad(flat, (0, rows * 128 - n)).reshape(rows, 128)


def _unpack_small(packed, like, n_layers):
    per_layer = sum(int(np.prod(like[n].shape[1:])) for n in SMALL_ORDER)
    body = packed.reshape(-1)[:n_layers * per_layer].reshape(n_layers, per_layer)
    out, off = {}, 0
    for n in SMALL_ORDER:
        k = int(np.prod(like[n].shape[1:]))
        out[n] = body[:, off:off + k].reshape(like[n].shape)
        off += k
    return out, packed.reshape(-1)[n_layers * per_layer]


def kernel(x, mem, g_mix, w_in, g_cq, w_uq, g_ckv, w_ukv, g_mla_q, g_mla_k, b_f, g_fox_q, g_fox_k, rel_bias, g_ch_q, g_ch_k, w_br, w_out, g_cross, g_mem, w_xq, w_xkv, g_x_q, g_x_k, w_xo, g_mlp, w_1, w_2, loss_target, m_g_mix, m_w_in, m_g_cq, m_w_uq, m_g_ckv, m_w_ukv, m_g_mla_q, m_g_mla_k, m_b_f, m_g_fox_q, m_g_fox_k, m_rel_bias, m_g_ch_q, m_g_ch_k, m_w_br, m_w_out, m_g_cross, m_g_mem, m_w_xq, m_w_xkv, m_g_x_q, m_g_x_k, m_w_xo, m_g_mlp, m_w_1, m_w_2, v_g_mix, v_w_in, v_g_cq, v_w_uq, v_g_ckv, v_w_ukv, v_g_mla_q, v_g_mla_k, v_b_f, v_g_fox_q, v_g_fox_k, v_rel_bias, v_g_ch_q, v_g_ch_k, v_w_br, v_w_out, v_g_cross, v_g_mem, v_w_xq, v_w_xkv, v_g_x_q, v_g_x_k, v_w_xo, v_g_mlp, v_w_1, v_w_2):
    d = dict(zip(INPUT_NAMES, (x, mem, g_mix, w_in, g_cq, w_uq, g_ckv, w_ukv, g_mla_q, g_mla_k, b_f, g_fox_q, g_fox_k, rel_bias, g_ch_q, g_ch_k, w_br, w_out, g_cross, g_mem, w_xq, w_xkv, g_x_q, g_x_k, w_xo, g_mlp, w_1, w_2, loss_target, m_g_mix, m_w_in, m_g_cq, m_w_uq, m_g_ckv, m_w_ukv, m_g_mla_q, m_g_mla_k, m_b_f, m_g_fox_q, m_g_fox_k, m_rel_bias, m_g_ch_q, m_g_ch_k, m_w_br, m_w_out, m_g_cross, m_g_mem, m_w_xq, m_w_xkv, m_g_x_q, m_g_x_k, m_w_xo, m_g_mlp, m_w_1, m_w_2, v_g_mix, v_w_in, v_g_cq, v_w_uq, v_g_ckv, v_w_ukv, v_g_mla_q, v_g_mla_k, v_b_f, v_g_fox_q, v_g_fox_k, v_rel_bias, v_g_ch_q, v_g_ch_k, v_w_br, v_w_out, v_g_cross, v_g_mem, v_w_xq, v_w_xkv, v_g_x_q, v_g_x_k, v_w_xo, v_g_mlp, v_w_1, v_w_2)))
    n_layers = g_mix.shape[0]
    assert x.shape[0] == 1 and x.shape[2] == D_MODEL and mem.shape[1:] == (MEM_LEN, D_MODEL)
    for n in PACK_ORDER:
        assert d[n].shape[1:] == SHARD_SHAPES[n], (n, d[n].shape)

    chip = 2 * lax.axis_index("x") + lax.axis_index("y")
    core = lax.axis_index("c")

    Ws = []
    for l in range(n_layers):
        mine = [_rows2d(d[n][l].astype(BF16), n) for n in PACK_ORDER]
        theirs = allgather_weights(mine)
        full = [lax.dynamic_update_slice(t, m_[None], (chip, 0, 0)) for t, m_ in zip(theirs, mine)]
        Ws.append(full_weights({n: f.reshape((4,) + SHARD_SHAPES[n]) for n, f in zip(PACK_ORDER, full)}))
    Ps = [layer_params(d, l) for l in range(n_layers)]

    loss, dx, dws, dss = local_step(x[0], mem[0], loss_target[0], Ws, Ps)

    big = []
    for l in range(n_layers):
        sg = shard_grads(dws[l])
        big.append(reduce_scatter_grads([_rows2d(sg[n], n) for n in PACK_ORDER], chip, core))

    small_local = {n: jnp.stack([dss[l][n].reshape(d[n].shape[1:]) for l in range(n_layers)]) for n in SMALL_ORDER}
    small_sum, loss_sum = _unpack_small(allreduce_small(_pack_small(small_local, n_layers, extra=loss)),
                                        {n: d[n] for n in SMALL_ORDER}, n_layers)

    grads, delta, new_m, new_v = {}, {}, {}, {}
    for i, n in enumerate(PACK_ORDER):
        outs = adamw(_rows2d(d[n], n), [big[l][i] for l in range(n_layers)], _rows2d(d["m_" + n], n),
                     _rows2d(d["v_" + n], n), name="adamw_" + n)
        grads[n], delta[n], new_m[n], new_v[n] = (o.reshape(d[n].shape) for o in outs)
    like = {n: d[n] for n in SMALL_ORDER}
    sm = adamw(_pack_small(like, n_layers)[None], [_pack_small(small_sum, n_layers)],
               _pack_small({n: d["m_" + n] for n in SMALL_ORDER}, n_layers)[None],
               _pack_small({n: d["v_" + n] for n in SMALL_ORDER}, n_layers)[None], name="adamw_small")
    for res, src in zip((grads, delta, new_m, new_v), sm):
        res.update(_unpack_small(src[0], like, n_layers)[0])

    return (loss_sum, dx[None], *[grads[n] for n in WEIGHT_ORDER], *[delta[n] for n in WEIGHT_ORDER],
            *[new_m[n] for n in WEIGHT_ORDER], *[new_v[n] for n in WEIGHT_ORDER])
```

```python
import functools

import numpy as np
import jax
import jax.numpy as jnp
from jax import lax
from jax.experimental import pallas as pl
from jax.experimental.pallas import tpu as pltpu

F32 = jnp.float32
BF16 = jnp.bfloat16
MXU_DTYPE = jnp.bfloat16
MESH = pl.DeviceIdType.MESH

D_MODEL = 2048
MIX_W = 1024
N_HEADS = 8
DH = 128
MLA_NOPE = 128
MLA_ROPE = 64
MLA_QK = MLA_NOPE + MLA_ROPE
MLA_PAD = 256
Q_LORA = 512
KV_LORA = 256
CHUNK = 64
LEFT_CHUNKS = 8
REL_CLIP = 128
N_REL = 2 * REL_CLIP + 1
X_HEADS = 4
MEM_LEN = 256
D_FF = 8192
ROPE_THETA = 10000.0
EPS = 1e-6
NEG = -1e30

Z_CQ, Z_CKV, Z_KR, Z_FF = 0, 512, 768, 896
ZS_W = 1024
Z_FOX = 1024
Z_CH = Z_FOX + 3 * MIX_W
Z_GATE = Z_CH + 3 * MIX_W
Z_TOT = Z_GATE + 3 * D_MODEL
W_IN_CUTS = (0, 512, 768, 832, 3904, 3912, 6984, 13128)

ADAM_LR, ADAM_B1, ADAM_B2, ADAM_EPS, ADAM_WD, ADAM_STEP = 0.001, 0.9, 0.999, 1e-08, 0.01, 10

VMEM_LIMIT_BYTES = 56 * 1024 * 1024
PACK_ORDER = ("w_uq", "w_ukv", "w_br", "w_out", "w_xq", "w_xkv", "w_xo", "w_1", "w_2", "w_in")
SMALL_ORDER = ("g_mix", "g_cq", "g_ckv", "g_mla_q", "g_mla_k", "b_f", "g_fox_q", "g_fox_k", "rel_bias",
               "g_ch_q", "g_ch_k", "g_cross", "g_mem", "g_x_q", "g_x_k", "g_mlp")
WEIGHT_ORDER = ("g_mix", "w_in", "g_cq", "w_uq", "g_ckv", "w_ukv", "g_mla_q", "g_mla_k", "b_f", "g_fox_q",
                "g_fox_k", "rel_bias", "g_ch_q", "g_ch_k", "w_br", "w_out", "g_cross", "g_mem", "w_xq",
                "w_xkv", "g_x_q", "g_x_k", "w_xo", "g_mlp", "w_1", "w_2")


def _pick(n, prefs):
    for p in prefs:
        if n % p == 0:
            return p
    raise ValueError(f"no block size among {prefs} divides {n}")


def _pcall(body, *, name, out_shape, in_specs, out_specs, grid=(), scratch=(), aliases=None):
    return pl.pallas_call(
        body, out_shape=out_shape, grid=grid, in_specs=in_specs, out_specs=out_specs,
        scratch_shapes=scratch, name=name, interpret=False,
        input_output_aliases=aliases or {},
        compiler_params=pltpu.CompilerParams(vmem_limit_bytes=VMEM_LIMIT_BYTES))


def _sds(shape, dtype):
    return jax.ShapeDtypeStruct(tuple(shape), dtype)


def _mx(v):
    return v.astype(MXU_DTYPE)


def mm_nn(a, b3, *, name, out_dtype, a_col0=0, res=None, relu2=False):
    M = a.shape[0]
    nb, K, Ns = b3.shape
    N = nb * Ns
    tm = _pick(M, (1024, 512, 256, 128))
    tk = _pick(K, (2048, 1024, 512, 256))
    tn = _pick(Ns, (512, 256, 128))
    assert a_col0 % tk == 0
    nk, nbs, ka0 = K // tk, Ns // tn, a_col0 // tk
    n_out = 2 if relu2 else 1

    def body(*refs):
        a_ref, b_ref = refs[0], refs[1]
        pos = 2
        res_ref = None
        if res is not None:
            res_ref = refs[pos]
            pos += 1
        outs = refs[pos:pos + n_out]
        acc_ref = refs[pos + n_out] if nk > 1 else None
        part = jnp.dot(_mx(a_ref[...]), _mx(b_ref[...]), preferred_element_type=F32)

        def finish(acc):
            if res_ref is not None:
                acc = acc + res_ref[...]
            outs[0][...] = acc.astype(outs[0].dtype)
            if relu2:
                r = jnp.maximum(acc, 0.0)
                outs[1][...] = (r * r).astype(outs[1].dtype)

        if nk == 1:
            finish(part)
        else:
            k = pl.program_id(2)

            @pl.when(k == 0)
            def _():
                acc_ref[...] = part

            @pl.when(k > 0)
            def _():
                acc_ref[...] += part

            @pl.when(k == nk - 1)
            def _():
                finish(acc_ref[...])

    in_specs = [pl.BlockSpec((tm, tk), lambda i, j, k: (i, ka0 + k)),
                pl.BlockSpec((None, tk, tn), lambda i, j, k: (j // nbs, k, j % nbs))]
    args = [a, b3]
    if res is not None:
        in_specs.append(pl.BlockSpec((tm, tn), lambda i, j, k: (i, j)))
        args.append(res)
    o_spec = pl.BlockSpec((tm, tn), lambda i, j, k: (i, j))
    if relu2:
        out_shape, out_specs = (_sds((M, N), out_dtype), _sds((M, N), out_dtype)), (o_spec, o_spec)
    else:
        out_shape, out_specs = _sds((M, N), out_dtype), o_spec
    scratch = (pltpu.VMEM((tm, tn), F32),) if nk > 1 else ()
    return _pcall(body, name=name, out_shape=out_shape, grid=(M // tm, N // tn, nk),
                  in_specs=in_specs, out_specs=out_specs, scratch=scratch)(*args)


def mm_nt(a, b3, *, name, out_dtype, a_col0=0, res=None, relu_mul=None):
    M = a.shape[0]
    nb, K, Ns = b3.shape
    tm = _pick(M, (1024, 512, 256, 128))
    tk = _pick(K, (1024, 512, 256))
    tn = _pick(Ns, (1024, 512, 256, 128))
    assert a_col0 % tn == 0
    nbs = Ns // tn
    nn, a0 = nb * nbs, a_col0 // tn

    def body(*refs):
        a_ref, b_ref = refs[0], refs[1]
        pos = 2
        mul_ref = res_ref = None
        if relu_mul is not None:
            mul_ref = refs[pos]
            pos += 1
        if res is not None:
            res_ref = refs[pos]
            pos += 1
        o_ref = refs[pos]
        acc_ref = refs[pos + 1] if nn > 1 else None
        part = lax.dot_general(_mx(a_ref[...]), _mx(b_ref[...]), (((1,), (1,)), ((), ())),
                               preferred_element_type=F32)

        def finish(acc):
            if mul_ref is not None:
                acc = acc * (2.0 * jnp.maximum(mul_ref[...].astype(F32), 0.0))
            if res_ref is not None:
                acc = acc + res_ref[...]
            o_ref[...] = acc.astype(o_ref.dtype)

        if nn == 1:
            finish(part)
        else:
            j = pl.program_id(2)

            @pl.when(j == 0)
            def _():
                acc_ref[...] = part

            @pl.when(j > 0)
            def _():
                acc_ref[...] += part

            @pl.when(j == nn - 1)
            def _():
                finish(acc_ref[...])

    in_specs = [pl.BlockSpec((tm, tn), lambda i, kk, j: (i, a0 + j)),
                pl.BlockSpec((None, tk, tn), lambda i, kk, j: (j // nbs, kk, j % nbs))]
    args = [a, b3]
    for extra in (relu_mul, res):
        if extra is not None:
            in_specs.append(pl.BlockSpec((tm, tk), lambda i, kk, j: (i, kk)))
            args.append(extra)
    scratch = (pltpu.VMEM((tm, tk), F32),) if nn > 1 else ()
    return _pcall(body, name=name, out_shape=_sds((M, K), out_dtype), grid=(M // tm, K // tk, nn),
                  in_specs=in_specs, out_specs=pl.BlockSpec((tm, tk), lambda i, kk, j: (i, kk)),
                  scratch=scratch)(*args)


def mm_tn(a, c, *, nb, name, out_dtype, K=None, N=None, a_col0=0, c_col0=0):
    M = a.shape[0]
    K = K or a.shape[1]
    N = N or c.shape[1]
    Ns = N // nb
    tm = _pick(M, (2048, 1024, 512, 256))
    tk = _pick(K, (512, 256))
    tn = _pick(Ns, (512, 256, 128))
    assert a_col0 % tk == 0 and c_col0 % tn == 0
    nm, nbs, a0, c0 = M // tm, Ns // tn, a_col0 // tk, c_col0 // tn

    def body(*refs):
        a_ref, c_ref, o_ref = refs[:3]
        acc_ref = refs[3] if nm > 1 else None
        part = lax.dot_general(_mx(a_ref[...]), _mx(c_ref[...]), (((0,), (0,)), ((), ())),
                               preferred_element_type=F32)
        if nm == 1:
            o_ref[...] = part.astype(o_ref.dtype)
        else:
            m = pl.program_id(2)

            @pl.when(m == 0)
            def _():
                acc_ref[...] = part

            @pl.when(m > 0)
            def _():
                acc_ref[...] += part

            @pl.when(m == nm - 1)
            def _():
                o_ref[...] = acc_ref[...].astype(o_ref.dtype)

    scratch = (pltpu.VMEM((tk, tn), F32),) if nm > 1 else ()
    return _pcall(
        body, name=name, out_shape=_sds((nb, K, Ns), out_dtype), grid=(K // tk, N // tn, nm),
        in_specs=[pl.BlockSpec((tm, tk), lambda kk, j, m: (m, a0 + kk)),
                  pl.BlockSpec((tm, tn), lambda kk, j, m: (m, c0 + j))],
        out_specs=pl.BlockSpec((None, tk, tn), lambda kk, j, m: (j // nbs, kk, j % nbs)),
        scratch=scratch)(a, c)


def rms_fwd(x, g, *, name, col0=0, width=None, out_dtype=BF16):
    R = x.shape[0]
    width = width or x.shape[1]
    assert col0 % width == 0
    cb = col0 // width
    tr = _pick(R, (512, 256, 128))

    def body(x_ref, g_ref, o_ref):
        xf = x_ref[...].astype(F32)
        r = lax.rsqrt(jnp.mean(xf * xf, axis=1, keepdims=True) + EPS)
        o_ref[...] = (xf * r * g_ref[...]).astype(o_ref.dtype)

    return _pcall(body, name=name, out_shape=_sds((R, width), out_dtype), grid=(R // tr,),
                  in_specs=[pl.BlockSpec((tr, width), lambda i: (i, cb)),
                            pl.BlockSpec((1, width), lambda i: (0, 0))],
                  out_specs=pl.BlockSpec((tr, width), lambda i: (i, 0)))(x, g)


def rms_bwd(x, g, dy, *, name, col0=0, width=None, res=None, dx_dtype=F32, need_dx=True):
    R = x.shape[0]
    width = width or x.shape[1]
    cb = col0 // width
    tr = _pick(R, (512, 256, 128))

    def body(*refs):
        x_ref, g_ref, dy_ref = refs[:3]
        pos = 3
        res_ref = None
        if res is not None:
            res_ref = refs[pos]
            pos += 1
        dx_ref = None
        if need_dx:
            dx_ref = refs[pos]
            pos += 1
        dg_ref = refs[pos]
        xf = x_ref[...].astype(F32)
        dyf = dy_ref[...].astype(F32)
        r = lax.rsqrt(jnp.mean(xf * xf, axis=1, keepdims=True) + EPS)
        xh = xf * r
        if need_dx:
            gy = dyf * g_ref[...]
            dx = r * (gy - xh * jnp.mean(gy * xh, axis=1, keepdims=True))
            if res_ref is not None:
                dx = dx + res_ref[...]
            dx_ref[...] = dx.astype(dx_ref.dtype)
        part = jnp.sum(dyf * xh, axis=0, keepdims=True)

        @pl.when(pl.program_id(0) == 0)
        def _():
            dg_ref[...] = part

        @pl.when(pl.program_id(0) > 0)
        def _():
            dg_ref[...] += part

    in_specs = [pl.BlockSpec((tr, width), lambda i: (i, cb)),
                pl.BlockSpec((1, width), lambda i: (0, 0)),
                pl.BlockSpec((tr, width), lambda i: (i, 0))]
    args = [x, g, dy]
    if res is not None:
        in_specs.append(pl.BlockSpec((tr, width), lambda i: (i, 0)))
        args.append(res)
    dg_shape, dg_spec = _sds((1, width), F32), pl.BlockSpec((1, width), lambda i: (0, 0))
    if need_dx:
        out_shape = (_sds((R, width), dx_dtype), dg_shape)
        out_specs = (pl.BlockSpec((tr, width), lambda i: (i, 0)), dg_spec)
    else:
        out_shape, out_specs = dg_shape, dg_spec
    out = _pcall(body, name=name, out_shape=out_shape, grid=(R // tr,), in_specs=in_specs,
                 out_specs=out_specs)(*args)
    return out if need_dx else (None, out)


def _rope_apply(y, c, sa, sb):
    return y * c + pltpu.roll(y, 96, 1) * sa + pltpu.roll(y, 32, 1) * sb


def _rope_transpose(dy, c, sa, sb):
    return dy * c + pltpu.roll(dy * sa, 32, 1) + pltpu.roll(dy * sb, 96, 1)


def rope_tables(seq):
    pos = jnp.arange(seq, dtype=F32)
    inv = ROPE_THETA ** (-jnp.arange(0, MLA_ROPE, 2, dtype=F32) / MLA_ROPE)
    ang = pos[:, None] * inv[None, :]
    cos, sin = jnp.cos(ang), jnp.sin(ang)
    z32, z64 = jnp.zeros_like(cos), jnp.zeros((seq, 64), F32)
    c = jnp.concatenate([cos, cos, z64], axis=1)
    sa = jnp.concatenate([-sin, z32, z64], axis=1)
    sb = jnp.concatenate([z32, sin, z64], axis=1)
    return c, sa, sb


def _head_vec(part_refs):
    xs = [p[...].astype(F32) for p in part_refs]
    return xs[0] if len(xs) == 1 else jnp.concatenate(xs, axis=1)


def prep_fwd(parts, g, *, name, n_heads, n_real, rope=None):
    rows = parts[0][0].shape[0]
    dh = sum(w for _, w, _ in parts)
    tr = _pick(rows, (512, 256, 128))
    npart = len(parts)

    def body(*refs):
        part_refs, g_ref = refs[:npart], refs[npart]
        pos = npart + 1
        if rope is not None:
            c_ref, sa_ref, sb_ref = refs[pos:pos + 3]
            pos += 3
        o_ref = refs[pos]
        x = _head_vec(part_refs)
        r = lax.rsqrt(jnp.sum(x * x, axis=1, keepdims=True) * (1.0 / n_real) + EPS)
        y = x * r * g_ref[...]
        if rope is not None:
            yr = _rope_apply(y[:, dh - 128:], c_ref[...], sa_ref[...], sb_ref[...])
            y = jnp.concatenate([y[:, :dh - 128], yr], axis=1)
        o_ref[...] = y.astype(o_ref.dtype)

    in_specs, args = [], []
    for arr, w, fn in parts:
        in_specs.append(pl.BlockSpec((tr, w), functools.partial(lambda h, i, fn: (i, fn(h)), fn=fn)))
        args.append(arr)
    in_specs.append(pl.BlockSpec((1, dh), lambda h, i: (0, 0)))
    args.append(g)
    if rope is not None:
        for t in rope:
            in_specs.append(pl.BlockSpec((tr, 128), lambda h, i: (i, 0)))
            args.append(t)
    return _pcall(body, name=name, out_shape=_sds((n_heads, rows, dh), BF16), grid=(n_heads, rows // tr),
                  in_specs=in_specs, out_specs=pl.BlockSpec((None, tr, dh), lambda h, i: (h, i, 0)))(*args)


def _norm_bwd(x, g, dyn, n_real):
    r = lax.rsqrt(jnp.sum(x * x, axis=1, keepdims=True) * (1.0 / n_real) + EPS)
    xh = x * r
    gy = dyn * g
    dx = r * (gy - xh * (jnp.sum(gy * xh, axis=1, keepdims=True) * (1.0 / n_real)))
    return dx, jnp.sum(dyn * xh, axis=0, keepdims=True)


def prep_bwd_q(src, dy, g, *, name, n_heads, dh, n_real, rope=None, out_dtype=BF16):
    rows = src.shape[0]
    tr = _pick(rows, (512, 256, 128))

    def body(*refs):
        x_ref, dy_ref, g_ref = refs[:3]
        pos = 3
        if rope is not None:
            c_ref, sa_ref, sb_ref = refs[pos:pos + 3]
            pos += 3
        dx_ref, dg_ref = refs[pos], refs[pos + 1]
        dyn = dy_ref[...].astype(F32)
        if rope is not None:
            dr = _rope_transpose(dyn[:, dh - 128:], c_ref[...], sa_ref[...], sb_ref[...])
            dyn = jnp.concatenate([dyn[:, :dh - 128], dr], axis=1)
        dx, dg = _norm_bwd(x_ref[...].astype(F32), g_ref[...], dyn, n_real)
        dx_ref[...] = dx.astype(dx_ref.dtype)
        first = jnp.logical_and(pl.program_id(0) == 0, pl.program_id(1) == 0)

        @pl.when(first)
        def _():
            dg_ref[...] = dg

        @pl.when(jnp.logical_not(first))
        def _():
            dg_ref[...] += dg

    in_specs = [pl.BlockSpec((tr, dh), lambda i, h: (i, h)),
                pl.BlockSpec((None, tr, dh), lambda i, h: (h, i, 0)),
                pl.BlockSpec((1, dh), lambda i, h: (0, 0))]
    args = [src, dy, g]
    if rope is not None:
        for t in rope:
            in_specs.append(pl.BlockSpec((tr, 128), lambda i, h: (i, 0)))
            args.append(t)
    return _pcall(body, name=name, out_shape=(_sds((rows, n_heads * dh), out_dtype), _sds((1, dh), F32)),
                  grid=(rows // tr, n_heads), in_specs=in_specs,
                  out_specs=(pl.BlockSpec((tr, dh), lambda i, h: (i, h)),
                             pl.BlockSpec((1, dh), lambda i, h: (0, 0))))(*args)


def prep_bwd_mla_k(kv_raw, zs, dkf, dv, g, rope, *, name):
    rows = kv_raw.shape[0]
    tr = _pick(rows, (512, 256, 128))

    def body(kn_ref, kr_ref, dy_ref, dv_ref, g_ref, c_ref, sa_ref, sb_ref, dkv_ref, dkr_ref, dg_ref):
        h = pl.program_id(1)
        x = jnp.concatenate([kn_ref[...].astype(F32), kr_ref[...].astype(F32)], axis=1)
        dyn = dy_ref[...].astype(F32)
        dr = _rope_transpose(dyn[:, 128:], c_ref[...], sa_ref[...], sb_ref[...])
        dyn = jnp.concatenate([dyn[:, :128], dr], axis=1)
        dx, dg = _norm_bwd(x, g_ref[...], dyn, MLA_QK)
        dkv_ref[...] = jnp.concatenate([dx[:, :128], dv_ref[...].astype(F32)], axis=1).astype(dkv_ref.dtype)

        @pl.when(h == 0)
        def _():
            dkr_ref[...] = dx[:, 128:]

        @pl.when(h > 0)
        def _():
            dkr_ref[...] += dx[:, 128:]

        first = jnp.logical_and(pl.program_id(0) == 0, h == 0)

        @pl.when(first)
        def _():
            dg_ref[...] = dg

        @pl.when(jnp.logical_not(first))
        def _():
            dg_ref[...] += dg

    tab = pl.BlockSpec((tr, 128), lambda i, h: (i, 0))
    return _pcall(
        body, name=name,
        out_shape=(_sds((rows, N_HEADS * 256), BF16), _sds((rows, 128), F32), _sds((1, MLA_PAD), F32)),
        grid=(rows // tr, N_HEADS),
        in_specs=[pl.BlockSpec((tr, 128), lambda i, h: (i, 2 * h)),
                  pl.BlockSpec((tr, 128), lambda i, h: (i, Z_KR // 128)),
                  pl.BlockSpec((None, tr, MLA_PAD), lambda i, h: (h, i, 0)),
                  pl.BlockSpec((None, tr, 128), lambda i, h: (h, i, 0)),
                  pl.BlockSpec((1, MLA_PAD), lambda i, h: (0, 0)), tab, tab, tab],
        out_specs=(pl.BlockSpec((tr, 256), lambda i, h: (i, h)),
                   pl.BlockSpec((tr, 128), lambda i, h: (i, 0)),
                   pl.BlockSpec((1, MLA_PAD), lambda i, h: (0, 0))))(kv_raw, zs, dkf, dv, g, *rope)


def prep_bwd_groups(src, base_blk, dys, gs, *, name, n_heads, kinds, out_dtype=BF16):
    rows = src.shape[0]
    ng = len(kinds)
    J = ng * n_heads
    tr = _pick(rows, (512, 256, 128))
    gstack = jnp.stack([gs[k] if kinds[k] == "norm" else jnp.ones((1, DH), F32) for k in range(ng)])

    def body(*refs):
        x_ref = refs[0]
        dy_refs = refs[1:1 + ng]
        g_ref, dx_ref, dg_ref = refs[1 + ng:4 + ng]
        j, i = pl.program_id(0), pl.program_id(1)
        grp = j // n_heads
        dy = dy_refs[0][...].astype(F32)
        for k in range(1, ng):
            dy = jnp.where(grp == k, dy_refs[k][...].astype(F32), dy)
        dx, dg = _norm_bwd(x_ref[...].astype(F32), g_ref[...], dy, DH)
        is_copy = functools.reduce(jnp.logical_or, [grp == k for k in range(ng) if kinds[k] == "copy"],
                                   jnp.bool_(False))
        dx_ref[...] = jnp.where(is_copy, dy, dx).astype(dx_ref.dtype)
        dg = jnp.where(is_copy, jnp.zeros_like(dg), dg)
        first = jnp.logical_and(j % n_heads == 0, i == 0)

        @pl.when(first)
        def _():
            dg_ref[...] = dg

        @pl.when(jnp.logical_not(first))
        def _():
            dg_ref[...] += dg

    in_specs = [pl.BlockSpec((tr, DH), lambda j, i: (i, base_blk + j))]
    for k in range(ng):
        in_specs.append(pl.BlockSpec(
            (None, tr, DH),
            functools.partial(lambda j, i, k: (jnp.clip(j - k * n_heads, 0, n_heads - 1), i, 0), k=k)))
    in_specs.append(pl.BlockSpec((None, 1, DH), lambda j, i: (j // n_heads, 0, 0)))
    return _pcall(body, name=name, out_shape=(_sds((rows, J * DH), out_dtype), _sds((ng, 1, DH), F32)),
                  grid=(J, rows // tr), in_specs=in_specs,
                  out_specs=(pl.BlockSpec((tr, DH), lambda j, i: (i, j)),
                             pl.BlockSpec((None, 1, DH), lambda j, i: (j // n_heads, 0, 0))))(src, *dys, gstack)


def _attn_cfg(mode, sq, sk):
    if mode == "chunk":
        tq = 128
        win = min((LEFT_CHUNKS + 2) * CHUNK, sk)
    else:
        tq = _pick(sq, (256, 128))
        win = sk
    scale = (MLA_QK if mode == "mla" else DH) ** -0.5
    return tq, win, scale


def _attn_scores(mode, i, tq, win, sk, scale, q, k_ref, cq_ref, ck_ref, t_ref):
    if mode == "chunk":
        start = pl.multiple_of(jnp.clip((i - LEFT_CHUNKS // 2) * 128, 0, sk - win), 128)
        kk = k_ref[pl.ds(start, win), :]
    else:
        start = 0
        kk = k_ref[...]
    s = lax.dot_general(q, kk, (((1,), (1,)), ((), ())), preferred_element_type=F32) * scale
    if mode == "cross":
        return s, start
    t_pos = i * tq + lax.broadcasted_iota(jnp.int32, (tq, win), 0)
    s_pos = start + lax.broadcasted_iota(jnp.int32, (tq, win), 1)
    if mode == "fox":
        s = s + cq_ref[...] - ck_ref[...]
        allowed = s_pos <= t_pos
    else:
        qc, kc = t_pos // CHUNK, s_pos // CHUNK
        allowed = kc <= qc
        if mode == "chunk":
            allowed = jnp.logical_and(allowed, kc >= qc - LEFT_CHUNKS)
            tiles = []
            for w in range(win // 128):
                delta = i - (start // 128 + w)
                tiles.append(jnp.where(delta == 0, t_ref[0], jnp.where(delta == 1, t_ref[1], t_ref[2])))
            s = s + jnp.concatenate(tiles, axis=1)
    return jnp.where(allowed, s, NEG), start


def attn_fwd(q, k, v_arr, v_blk, *, mode, name, n_heads, cq=None, ck=None, tiles=None):
    _, sq, dk = q.shape
    sk = k.shape[1]
    tq, win, scale = _attn_cfg(mode, sq, sk)

    def body(*refs):
        q_ref, k_ref, v_ref = refs[:3]
        pos = 3
        cq_ref = ck_ref = t_ref = None
        if mode == "fox":
            cq_ref, ck_ref = refs[pos:pos + 2]
            pos += 2
        if mode == "chunk":
            t_ref = refs[pos]
            pos += 1
        o_ref, lse_ref = refs[pos], refs[pos + 1]
        i = pl.program_id(1)
        s, start = _attn_scores(mode, i, tq, win, sk, scale, q_ref[...], k_ref, cq_ref, ck_ref, t_ref)
        vv = v_ref[pl.ds(start, win), :] if mode == "chunk" else v_ref[...]
        m = jnp.max(s, axis=1, keepdims=True)
        e = jnp.exp(s - m)
        l = jnp.sum(e, axis=1, keepdims=True)
        p = e * (1.0 / l)
        o_ref[...] = jnp.dot(_mx(p), _mx(vv), preferred_element_type=F32).astype(o_ref.dtype)
        lse_ref[...] = m + jnp.log(l)

    in_specs = [pl.BlockSpec((None, tq, dk), lambda h, i: (h, i, 0)),
                pl.BlockSpec((None, sk, dk), lambda h, i: (h, 0, 0)),
                pl.BlockSpec((sk, DH), lambda h, i: (0, v_blk(h)))]
    args = [q, k, v_arr]
    if mode == "fox":
        in_specs += [pl.BlockSpec((None, tq, 1), lambda h, i: (h, i, 0)),
                     pl.BlockSpec((None, 1, sk), lambda h, i: (h, 0, 0))]
        args += [cq, ck]
    if mode == "chunk":
        in_specs.append(pl.BlockSpec((3, None, 128, 128), lambda h, i: (0, h, 0, 0)))
        args.append(tiles)
    return _pcall(body, name=name,
                  out_shape=(_sds((sq, n_heads * DH), BF16), _sds((n_heads, sq, 1), F32)),
                  grid=(n_heads, sq // tq), in_specs=in_specs,
                  out_specs=(pl.BlockSpec((tq, DH), lambda h, i: (i, h)),
                             pl.BlockSpec((None, tq, 1), lambda h, i: (h, i, 0))))(*args)


def attn_bwd(q, k, v_arr, v_blk, o, do, lse, *, mode, name, n_heads, cq=None, ck=None, tiles=None):
    _, sq, dk = q.shape
    sk = k.shape[1]
    tq, win, scale = _attn_cfg(mode, sq, sk)
    n_extra = {"fox": 2, "chunk": 1}.get(mode, 0)

    def body(*refs):
        q_ref, k_ref, v_ref, o_ref, do_ref, lse_ref = refs[:6]
        pos = 6
        cq_ref = ck_ref = t_ref = None
        if mode == "fox":
            cq_ref, ck_ref = refs[pos:pos + 2]
            pos += 2
        if mode == "chunk":
            t_ref = refs[pos]
            pos += 1
        dq_ref, dk_ref, dv_ref = refs[pos:pos + 3]
        extra = refs[pos + 3:pos + 3 + n_extra]
        i = pl.program_id(1)
        q = q_ref[...]
        do = do_ref[...]
        s, start = _attn_scores(mode, i, tq, win, sk, scale, q, k_ref, cq_ref, ck_ref, t_ref)
        if mode == "chunk":
            rows = pl.ds(start, win)
            kk, vv = k_ref[rows, :], v_ref[rows, :]
        else:
            kk, vv = k_ref[...], v_ref[...]
        p = jnp.exp(s - lse_ref[...])
        drow = jnp.sum(do.astype(F32) * o_ref[...].astype(F32), axis=1, keepdims=True)
        dp = lax.dot_general(do, _mx(vv), (((1,), (1,)), ((), ())), preferred_element_type=F32)
        ds = p * (dp - drow)
        dsb = _mx(ds)
        dq_ref[...] = (jnp.dot(dsb, kk, preferred_element_type=F32) * scale).astype(dq_ref.dtype)
        dk_part = lax.dot_general(dsb, q, (((0,), (0,)), ((), ())), preferred_element_type=F32) * scale
        dv_part = lax.dot_general(_mx(p), do, (((0,), (0,)), ((), ())), preferred_element_type=F32)

        @pl.when(i == 0)
        def _():
            dk_ref[...] = jnp.zeros_like(dk_ref)
            dv_ref[...] = jnp.zeros_like(dv_ref)
            if mode == "fox":
                extra[1][...] = jnp.zeros_like(extra[1])
            if mode == "chunk":
                extra[0][...] = jnp.zeros_like(extra[0])

        if mode == "chunk":
            dk_ref[rows, :] += dk_part
            dv_ref[rows, :] += dv_part
            dt_ref = extra[0]
            for w in range(win // 128):
                delta = i - (start // 128 + w)
                tile = ds[:, w * 128:(w + 1) * 128]
                zero = jnp.zeros_like(tile)
                dt_ref[0] += jnp.where(delta == 0, tile, zero)
                dt_ref[1] += jnp.where(delta == 1, tile, zero)
                dt_ref[2] += jnp.where(delta >= 2, tile, zero)
        else:
            dk_ref[...] += dk_part
            dv_ref[...] += dv_part
        if mode == "fox":
            extra[0][...] = jnp.sum(ds, axis=1, keepdims=True)
            extra[1][...] -= jnp.sum(ds, axis=0, keepdims=True)

    in_specs = [pl.BlockSpec((None, tq, dk), lambda h, i: (h, i, 0)),
                pl.BlockSpec((None, sk, dk), lambda h, i: (h, 0, 0)),
                pl.BlockSpec((sk, DH), lambda h, i: (0, v_blk(h))),
                pl.BlockSpec((tq, DH), lambda h, i: (i, h)),
                pl.BlockSpec((tq, DH), lambda h, i: (i, h)),
                pl.BlockSpec((None, tq, 1), lambda h, i: (h, i, 0))]
    args = [q, k, v_arr, o, do, lse]
    out_shape = [_sds((n_heads, sq, dk), F32), _sds((n_heads, sk, dk), F32), _sds((n_heads, sk, DH), F32)]
    out_specs = [pl.BlockSpec((None, tq, dk), lambda h, i: (h, i, 0)),
                 pl.BlockSpec((None, sk, dk), lambda h, i: (h, 0, 0)),
                 pl.BlockSpec((None, sk, DH), lambda h, i: (h, 0, 0))]
    if mode == "fox":
        in_specs += [pl.BlockSpec((None, tq, 1), lambda h, i: (h, i, 0)),
                     pl.BlockSpec((None, 1, sk), lambda h, i: (h, 0, 0))]
        args += [cq, ck]
        out_shape += [_sds((n_heads, sq, 1), F32), _sds((n_heads, 1, sk), F32)]
        out_specs += [pl.BlockSpec((None, tq, 1), lambda h, i: (h, i, 0)),
                      pl.BlockSpec((None, 1, sk), lambda h, i: (h, 0, 0))]
    if mode == "chunk":
        in_specs.append(pl.BlockSpec((3, None, 128, 128), lambda h, i: (0, h, 0, 0)))
        args.append(tiles)
        out_shape.append(_sds((3, n_heads, 128, 128), F32))
        out_specs.append(pl.BlockSpec((3, None, 128, 128), lambda h, i: (0, h, 0, 0)))
    return _pcall(body, name=name, out_shape=tuple(out_shape), grid=(n_heads, sq // tq),
                  in_specs=in_specs, out_specs=tuple(out_specs))(*args)


REL_LANES = 384
REL_KBLK = 2048


def _rel_onehot(t, k):
    rho = k * REL_KBLK + lax.broadcasted_iota(jnp.int32, (REL_KBLK, REL_LANES), 0)
    lane = lax.broadcasted_iota(jnp.int32, (REL_KBLK, REL_LANES), 1)
    diff = lax.shift_right_logical(rho, 7) - jnp.bitwise_and(rho, 127)
    idx = jnp.where(t == 0, diff + REL_CLIP,
                    jnp.where(t == 1, jnp.minimum(diff + 128, REL_CLIP) + REL_CLIP, N_REL - 1))
    return jnp.where(idx == lane, 1.0, 0.0).astype(F32)


def relbias_tiles(rel_bias):
    nh = rel_bias.shape[0]
    rb = jnp.pad(rel_bias, ((0, 0), (0, REL_LANES - N_REL)))

    def body(rb_ref, o_ref):
        e = _rel_onehot(pl.program_id(0), pl.program_id(1))
        o_ref[...] = lax.dot_general(rb_ref[...], e, (((1,), (1,)), ((), ())), preferred_element_type=F32,
                                     precision=lax.Precision.HIGHEST)

    flat = _pcall(body, name="relbias_tiles", out_shape=_sds((3, nh, 128 * 128), F32),
                  grid=(3, 128 * 128 // REL_KBLK),
                  in_specs=[pl.BlockSpec((nh, REL_LANES), lambda t, k: (0, 0))],
                  out_specs=pl.BlockSpec((None, nh, REL_KBLK), lambda t, k: (t, 0, k)))(rb)
    return flat.reshape(3, nh, 128, 128)


def relbias_tiles_bwd(dtiles):
    nh = dtiles.shape[1]

    def body(dt_ref, o_ref):
        t, k = pl.program_id(0), pl.program_id(1)
        part = jnp.dot(dt_ref[...], _rel_onehot(t, k), preferred_element_type=F32,
                       precision=lax.Precision.HIGHEST)
        first = jnp.logical_and(t == 0, k == 0)

        @pl.when(first)
        def _():
            o_ref[...] = part

        @pl.when(jnp.logical_not(first))
        def _():
            o_ref[...] += part

    out = _pcall(body, name="relbias_tiles_bwd", out_shape=_sds((nh, REL_LANES), F32),
                 grid=(3, 128 * 128 // REL_KBLK),
                 in_specs=[pl.BlockSpec((None, nh, REL_KBLK), lambda t, k: (t, 0, k))],
                 out_specs=pl.BlockSpec((nh, REL_LANES), lambda t, k: (0, 0)))(dtiles.reshape(3, nh, 128 * 128))
    return out[:, :N_REL]


CUM_BLK = 256


def _tri(n, lower):
    r = lax.broadcasted_iota(jnp.int32, (n, n), 0)
    c = lax.broadcasted_iota(jnp.int32, (n, n), 1)
    return jnp.where(r >= c if lower else r <= c, 1.0, 0.0).astype(F32)


def fox_cum_fwd(zs, bf):
    S = zs.shape[0]
    tb = min(CUM_BLK, S)

    def body(f_ref, b_ref, cum_ref, cumt_ref, carry_ref):
        @pl.when(pl.program_id(0) == 0)
        def _():
            carry_ref[...] = jnp.zeros_like(carry_ref)

        x = f_ref[...] + b_ref[...]
        lane = lax.broadcasted_iota(jnp.int32, x.shape, 1)
        logf = jnp.where(lane < N_HEADS, jnp.minimum(x, 0.0) - jnp.log(1.0 + jnp.exp(-jnp.abs(x))), 0.0)
        cum = jnp.dot(_tri(tb, True), logf, preferred_element_type=F32,
                      precision=lax.Precision.HIGHEST) + carry_ref[...]
        carry_ref[...] = cum[tb - 1:tb, :]
        cum_ref[...] = cum
        cumt_ref[...] = cum.T

    return _pcall(body, name="fox_cum_fwd", out_shape=(_sds((S, 128), F32), _sds((128, S), F32)),
                  grid=(S // tb,),
                  in_specs=[pl.BlockSpec((tb, 128), lambda i: (i, Z_FF // 128)),
                            pl.BlockSpec((1, 128), lambda i: (0, 0))],
                  out_specs=(pl.BlockSpec((tb, 128), lambda i: (i, 0)),
                             pl.BlockSpec((128, tb), lambda i: (0, i))),
                  scratch=(pltpu.VMEM((1, 128), F32),))(zs, bf)


def fox_cum_bwd(zs, bf, dcum):
    S = zs.shape[0]
    tb = min(CUM_BLK, S)
    nblk = S // tb

    def body(f_ref, b_ref, d_ref, df_ref, db_ref, carry_ref):
        @pl.when(pl.program_id(0) == 0)
        def _():
            carry_ref[...] = jnp.zeros_like(carry_ref)
            db_ref[...] = jnp.zeros_like(db_ref)

        d = d_ref[...]
        dlogf = jnp.dot(_tri(tb, False), d, preferred_element_type=F32,
                        precision=lax.Precision.HIGHEST) + carry_ref[...]
        carry_ref[...] += jnp.sum(d, axis=0, keepdims=True)
        x = f_ref[...] + b_ref[...]
        lane = lax.broadcasted_iota(jnp.int32, x.shape, 1)
        dx = jnp.where(lane < N_HEADS, dlogf / (1.0 + jnp.exp(x)), 0.0)
        df_ref[...] = dx
        db_ref[...] += jnp.sum(dx, axis=0, keepdims=True)

    return _pcall(body, name="fox_cum_bwd", out_shape=(_sds((S, 128), F32), _sds((1, 128), F32)),
                  grid=(nblk,),
                  in_specs=[pl.BlockSpec((tb, 128), lambda i: (nblk - 1 - i, Z_FF // 128)),
                            pl.BlockSpec((1, 128), lambda i: (0, 0)),
                            pl.BlockSpec((tb, 128), lambda i: (nblk - 1 - i, 0))],
                  out_specs=(pl.BlockSpec((tb, 128), lambda i: (nblk - 1 - i, 0)),
                             pl.BlockSpec((1, 128), lambda i: (0, 0))),
                  scratch=(pltpu.VMEM((1, 128), F32),))(zs, bf, dcum)


def _sigmoid(x):
    return 1.0 / (1.0 + jnp.exp(-x))


def merge_fwd(z, projs):
    S = z.shape[0]
    tr, tc = _pick(S, (512, 256, 128)), 512
    nbc = D_MODEL // tc
    g0 = Z_GATE // tc

    def body(g0_ref, g1_ref, g2_ref, p0_ref, p1_ref, p2_ref, o_ref):
        acc = _sigmoid(g0_ref[...]) * p0_ref[...]
        acc += _sigmoid(g1_ref[...]) * p1_ref[...]
        acc += _sigmoid(g2_ref[...]) * p2_ref[...]
        o_ref[...] = acc.astype(o_ref.dtype)

    gspecs = [pl.BlockSpec((tr, tc), functools.partial(lambda i, j, n: (i, g0 + n * nbc + j), n=n))
              for n in range(3)]
    pspec = pl.BlockSpec((tr, tc), lambda i, j: (i, j))
    return _pcall(body, name="merge_fwd", out_shape=_sds((S, D_MODEL), BF16), grid=(S // tr, nbc),
                  in_specs=gspecs + [pspec] * 3, out_specs=pspec)(z, z, z, *projs)


def merge_bwd(z, projs, dmerged):
    S = z.shape[0]
    tr, tc = _pick(S, (512, 256, 128)), 512
    nbc = D_MODEL // tc
    g0 = Z_GATE // tc

    def body(g0_ref, g1_ref, g2_ref, p0_ref, p1_ref, p2_ref, dm_ref, dg0, dg1, dg2, dp0, dp1, dp2):
        dm = dm_ref[...]
        for g_ref, p_ref, dg_ref, dp_ref in ((g0_ref, p0_ref, dg0, dp0), (g1_ref, p1_ref, dg1, dp1),
                                             (g2_ref, p2_ref, dg2, dp2)):
            sg = _sigmoid(g_ref[...])
            dp_ref[...] = (dm * sg).astype(dp_ref.dtype)
            dg_ref[...] = (dm * p_ref[...] * sg * (1.0 - sg)).astype(dg_ref.dtype)

    gspecs = [pl.BlockSpec((tr, tc), functools.partial(lambda i, j, n: (i, g0 + n * nbc + j), n=n))
              for n in range(3)]
    pspec = pl.BlockSpec((tr, tc), lambda i, j: (i, j))
    out = _pcall(body, name="merge_bwd", out_shape=tuple(_sds((S, D_MODEL), BF16) for _ in range(6)),
                 grid=(S // tr, nbc), in_specs=gspecs + [pspec] * 4,
                 out_specs=tuple([pspec] * 6))(z, z, z, *projs, dmerged)
    return out[:3], out[3:]


def loss_head(y, target):
    S, D = y.shape
    tr = _pick(S, (256, 128))

    def body(y_ref, t_ref, dy_ref, l_ref):
        e = y_ref[...] - t_ref[...]
        dy_ref[...] = e * (1.0 / D)
        part = jnp.sum(jnp.sum(e * e, axis=1, keepdims=True), axis=0, keepdims=True) * (0.5 / D)

        @pl.when(pl.program_id(0) == 0)
        def _():
            l_ref[...] = part

        @pl.when(pl.program_id(0) > 0)
        def _():
            l_ref[...] += part

    spec = pl.BlockSpec((tr, D), lambda i: (i, 0))
    return _pcall(body, name="loss_head", out_shape=(_sds((S, D), F32), _sds((1, 1), F32)), grid=(S // tr,),
                  in_specs=[spec, spec], out_specs=(spec, pl.BlockSpec((1, 1), lambda i: (0, 0))))(y, target)


ADAMW_BLOCK_BYTES = 1024 * 1024


def adamw(w, gs, m, v, *, name):
    L, R, C = w.shape
    tr = _pick(R, (512, 256, 128, 64, 32, 16, 8))
    while tr * C * 4 > ADAMW_BLOCK_BYTES and tr % 16 == 0:
        tr //= 2
    c1 = 1.0 - ADAM_B1 ** ADAM_STEP
    c2 = 1.0 - ADAM_B2 ** ADAM_STEP

    def body(*refs):
        w_ref, m_ref, v_ref = refs[:3]
        g_refs = refs[3:3 + L]
        go_ref, d_ref, nm_ref, nv_ref = refs[3 + L:]
        l = pl.program_id(0)
        g_ = g_refs[0][...]
        for k in range(1, L):
            g_ = jnp.where(l == k, g_refs[k][...], g_)
        nm = ADAM_B1 * m_ref[...] + (1.0 - ADAM_B1) * g_
        nv = ADAM_B2 * v_ref[...] + (1.0 - ADAM_B2) * (g_ * g_)
        go_ref[...] = g_
        nm_ref[...] = nm
        nv_ref[...] = nv
        d_ref[...] = -ADAM_LR * ((nm / c1) / (jnp.sqrt(nv / c2) + ADAM_EPS) + ADAM_WD * w_ref[...])

    spec = pl.BlockSpec((None, tr, C), lambda l, i: (l, i, 0))
    gspecs = [pl.BlockSpec((tr, C), functools.partial(lambda l, i, k: (jnp.where(l == k, i, 0), 0), k=k))
              for k in range(L)]
    return _pcall(body, name=name, out_shape=tuple(_sds((L, R, C), F32) for _ in range(4)),
                  grid=(L, R // tr), in_specs=[spec] * 3 + gspecs, out_specs=(spec,) * 4)(w, m, v, *gs)


SHARD_SHAPES = {
    "w_uq": (Q_LORA, 384), "w_ukv": (KV_LORA, 512), "w_br": (3, MIX_W, 512), "w_out": (512, D_MODEL),
    "w_xq": (512, 512), "w_xkv": (512, 1024), "w_xo": (512, 512), "w_1": (D_MODEL, 2048),
    "w_2": (2048, D_MODEL), "w_in": (D_MODEL, 3282),
}
def _rows2d(a, name):
    shp = SHARD_SHAPES[name]
    return a.reshape(a.shape[:a.ndim - len(shp)] + (-1, shp[-1]))


def _cols_from_shards(g):
    return jnp.transpose(g, (1, 0, 2)).reshape(g.shape[1], 4 * g.shape[2])


def _cols_to_shards(w):
    return jnp.transpose(w.reshape(w.shape[0], 4, w.shape[1] // 4), (1, 0, 2))


def full_weights(g):
    c = W_IN_CUTS
    w_in = _cols_from_shards(g["w_in"])
    zeros = lambda n: jnp.zeros((D_MODEL, n), w_in.dtype)
    w_in_p = jnp.concatenate([w_in[:, c[0]:c[3]], zeros(64), w_in[:, c[4]:c[5]], zeros(120),
                              w_in[:, c[3]:c[4]], w_in[:, c[5]:c[7]]], axis=1)
    w_uq = _cols_from_shards(g["w_uq"]).reshape(Q_LORA, N_HEADS, MLA_QK)
    w_uq_p = jnp.pad(w_uq, ((0, 0), (0, 0), (0, MLA_PAD - MLA_QK))).reshape(1, Q_LORA, N_HEADS * MLA_PAD)
    return dict(
        in_p=w_in_p[None], uq_p=w_uq_p, ukv=g["w_ukv"], br=[g["w_br"][:, n] for n in range(3)],
        out=g["w_out"].reshape(1, D_MODEL, D_MODEL), xq=g["w_xq"].reshape(1, D_MODEL, 512),
        xkv=g["w_xkv"].reshape(1, D_MODEL, 1024), xo=g["w_xo"], w1=g["w_1"],
        w2=g["w_2"].reshape(1, D_FF, D_MODEL))


def shard_grads(dw):
    dp = dw["in_p"][0]
    d_in = jnp.concatenate([dp[:, 0:832], dp[:, Z_FOX:Z_CH], dp[:, Z_FF:Z_FF + 8], dp[:, Z_CH:Z_TOT]], axis=1)
    d_uq = dw["uq_p"].reshape(Q_LORA, N_HEADS, MLA_PAD)[:, :, :MLA_QK].reshape(Q_LORA, N_HEADS * MLA_QK)
    return {
        "w_in": _cols_to_shards(d_in), "w_uq": _cols_to_shards(d_uq), "w_ukv": dw["ukv"],
        "w_br": jnp.stack(dw["br"], axis=1), "w_out": dw["out"].reshape(4, 512, D_MODEL),
        "w_xq": dw["xq"].reshape(4, 512, 512), "w_xkv": dw["xkv"].reshape(4, 512, 1024), "w_xo": dw["xo"],
        "w_1": dw["w1"], "w_2": dw["w2"].reshape(4, 2048, D_MODEL)}


def layer_params(d, l):
    row = lambda v: v.reshape(1, -1).astype(F32)
    padto = lambda v, n: jnp.pad(row(v), ((0, 0), (0, n - v.shape[-1])))
    return dict(
        g_mix=row(d["g_mix"][l]), g_cq=row(d["g_cq"][l]), g_ckv=row(d["g_ckv"][l]),
        g_mla_q=padto(d["g_mla_q"][l], MLA_PAD), g_mla_k=padto(d["g_mla_k"][l], MLA_PAD),
        b_f=padto(d["b_f"][l], 128), g_fox_q=row(d["g_fox_q"][l]), g_fox_k=row(d["g_fox_k"][l]),
        rel_bias=d["rel_bias"][l].astype(F32), g_ch_q=row(d["g_ch_q"][l]), g_ch_k=row(d["g_ch_k"][l]),
        g_cross=row(d["g_cross"][l]), g_mem=row(d["g_mem"][l]), g_x_q=row(d["g_x_q"][l]),
        g_x_k=row(d["g_x_k"][l]), g_mlp=row(d["g_mlp"][l]))


FOX_B = Z_FOX // 128
CH_B = Z_CH // 128


def layer_fwd(x, mem, W, P, rope):
    S = x.shape[0]
    s = {}
    s["h"] = rms_fwd(x, P["g_mix"], name="rms_d")
    z = s["z"] = mm_nn(s["h"], W["in_p"], name="mm_in", out_dtype=F32)
    s["cq_n"] = rms_fwd(z, P["g_cq"], col0=Z_CQ, width=Q_LORA, name="rms_cq")
    s["ckv_n"] = rms_fwd(z, P["g_ckv"], col0=Z_CKV, width=KV_LORA, name="rms_ckv")
    s["q_raw"] = mm_nn(s["cq_n"], W["uq_p"], name="mm_uq", out_dtype=F32)
    s["kv_raw"] = mm_nn(s["ckv_n"], W["ukv"], name="mm_ukv", out_dtype=F32)
    s["qa"] = prep_fwd([(s["q_raw"], MLA_PAD, lambda h: h)], P["g_mla_q"], name="prep_mla_q",
                       n_heads=N_HEADS, n_real=MLA_QK, rope=rope)
    s["ka"] = prep_fwd([(s["kv_raw"], 128, lambda h: 2 * h), (z, 128, lambda h: Z_KR // 128)], P["g_mla_k"],
                       name="prep_mla_k", n_heads=N_HEADS, n_real=MLA_QK, rope=rope)
    s["o_a"], s["lse_a"] = attn_fwd(s["qa"], s["ka"], s["kv_raw"], lambda h: 2 * h + 1, mode="mla",
                                    name="attn_mla", n_heads=N_HEADS)
    s["qb"] = prep_fwd([(z, DH, lambda h: FOX_B + h)], P["g_fox_q"], name="prep_h", n_heads=N_HEADS, n_real=DH)
    s["kb"] = prep_fwd([(z, DH, lambda h: FOX_B + N_HEADS + h)], P["g_fox_k"], name="prep_h",
                       n_heads=N_HEADS, n_real=DH)
    _, cum_t = fox_cum_fwd(z, P["b_f"])
    s["cq"] = cum_t[:N_HEADS].reshape(N_HEADS, S, 1)
    s["ck"] = cum_t[:N_HEADS].reshape(N_HEADS, 1, S)
    s["o_b"], s["lse_b"] = attn_fwd(s["qb"], s["kb"], z, lambda h: FOX_B + 2 * N_HEADS + h, mode="fox",
                                    name="attn_fox", n_heads=N_HEADS, cq=s["cq"], ck=s["ck"])
    s["qc"] = prep_fwd([(z, DH, lambda h: CH_B + h)], P["g_ch_q"], name="prep_h", n_heads=N_HEADS, n_real=DH)
    s["kc"] = prep_fwd([(z, DH, lambda h: CH_B + N_HEADS + h)], P["g_ch_k"], name="prep_h",
                       n_heads=N_HEADS, n_real=DH)
    s["tiles"] = relbias_tiles(P["rel_bias"])
    s["o_c"], s["lse_c"] = attn_fwd(s["qc"], s["kc"], z, lambda h: CH_B + 2 * N_HEADS + h, mode="chunk",
                                    name="attn_chunk", n_heads=N_HEADS, tiles=s["tiles"])
    s["projs"] = [mm_nn(o, W["br"][n], name="mm_br", out_dtype=F32)
                  for n, o in enumerate((s["o_a"], s["o_b"], s["o_c"]))]
    s["merged"] = merge_fwd(z, s["projs"])
    x1 = s["x1"] = mm_nn(s["merged"], W["out"], name="mm_out", out_dtype=F32, res=x)
    s["hq"] = rms_fwd(x1, P["g_cross"], name="rms_d")
    s["xq_raw"] = mm_nn(s["hq"], W["xq"], name="mm_xq", out_dtype=F32)
    s["mem_n"] = rms_fwd(mem, P["g_mem"], name="rms_mem")
    s["mkv"] = mm_nn(s["mem_n"], W["xkv"], name="mm_xkv", out_dtype=F32)
    s["qx"] = prep_fwd([(s["xq_raw"], DH, lambda h: h)], P["g_x_q"], name="prep_xq", n_heads=X_HEADS, n_real=DH)
    s["kx"] = prep_fwd([(s["mkv"], DH, lambda h: h)], P["g_x_k"], name="prep_xk", n_heads=X_HEADS, n_real=DH)
    s["o_x"], s["lse_x"] = attn_fwd(s["qx"], s["kx"], s["mkv"], lambda h: X_HEADS + h, mode="cross",
                                    name="attn_cross", n_heads=X_HEADS)
    x2 = s["x2"] = mm_nn(s["o_x"], W["xo"], name="mm_xo", out_dtype=F32, res=x1)
    s["hm"] = rms_fwd(x2, P["g_mlp"], name="rms_d")
    s["a1"], s["act"] = mm_nn(s["hm"], W["w1"], name="mm_w1", out_dtype=BF16, relu2=True)
    x3 = mm_nn(s["act"], W["w2"], name="mm_w2", out_dtype=F32, res=x2)
    return x3, s


def layer_bwd(g, x, mem, W, P, rope, s):
    S = x.shape[0]
    z = s["z"]
    dw, ds = {}, {}
    da1 = mm_nt(g, W["w2"], name="mm_w2_dx", out_dtype=BF16, relu_mul=s["a1"])
    dw["w2"] = mm_tn(s["act"], g, nb=1, name="mm_w2_dw", out_dtype=BF16)
    dhm = mm_nt(da1, W["w1"], name="mm_w1_dx", out_dtype=F32)
    dw["w1"] = mm_tn(s["hm"], da1, nb=4, name="mm_w1_dw", out_dtype=BF16)
    g2, ds["g_mlp"] = rms_bwd(s["x2"], P["g_mlp"], dhm, name="rms_d_bwd", res=g)
    do_x = mm_nt(g2, W["xo"], name="mm_xo_dx", out_dtype=BF16)
    dw["xo"] = mm_tn(s["o_x"], g2, nb=4, name="mm_xo_dw", out_dtype=BF16)
    dqx, dkx, dvx = attn_bwd(s["qx"], s["kx"], s["mkv"], lambda h: X_HEADS + h, s["o_x"], do_x, s["lse_x"],
                             mode="cross", name="attn_cross_bwd", n_heads=X_HEADS)
    dxq_raw, ds["g_x_q"] = prep_bwd_q(s["xq_raw"], dqx, P["g_x_q"], name="prep_xq_bwd", n_heads=X_HEADS,
                                      dh=DH, n_real=DH)
    dmkv, dgk = prep_bwd_groups(s["mkv"], 0, [dkx, dvx], [P["g_x_k"], None], name="prep_xkv_bwd",
                                n_heads=X_HEADS, kinds=("norm", "copy"))
    ds["g_x_k"] = dgk[0]
    dhq = mm_nt(dxq_raw, W["xq"], name="mm_xq_dx", out_dtype=F32)
    dw["xq"] = mm_tn(s["hq"], dxq_raw, nb=1, name="mm_xq_dw", out_dtype=BF16)
    dmem_n = mm_nt(dmkv, W["xkv"], name="mm_xkv_dx", out_dtype=F32)
    dw["xkv"] = mm_tn(s["mem_n"], dmkv, nb=1, name="mm_xkv_dw", out_dtype=BF16)
    _, ds["g_mem"] = rms_bwd(mem, P["g_mem"], dmem_n, name="rms_mem_bwd", need_dx=False)
    g1, ds["g_cross"] = rms_bwd(s["x1"], P["g_cross"], dhq, name="rms_d_bwd", res=g2)
    dmerged = mm_nt(g1, W["out"], name="mm_out_dx", out_dtype=F32)
    dw["out"] = mm_tn(s["merged"], g1, nb=1, name="mm_out_dw", out_dtype=BF16)
    dgl, dproj = merge_bwd(z, s["projs"], dmerged)
    outs = (s["o_a"], s["o_b"], s["o_c"])
    do = [mm_nt(dproj[n], W["br"][n], name="mm_br_dx", out_dtype=BF16) for n in range(3)]
    dw["br"] = [mm_tn(outs[n], dproj[n], nb=4, name="mm_br_dw", out_dtype=BF16) for n in range(3)]
    dqc, dkc, dvc, dtiles = attn_bwd(s["qc"], s["kc"], z, lambda h: CH_B + 2 * N_HEADS + h, s["o_c"], do[2],
                                     s["lse_c"], mode="chunk", name="attn_chunk_bwd", n_heads=N_HEADS,
                                     tiles=s["tiles"])
    d_ch, dg_ch = prep_bwd_groups(z, CH_B, [dqc, dkc, dvc], [P["g_ch_q"], P["g_ch_k"], None],
                                  name="prep_h_bwd", n_heads=N_HEADS, kinds=("norm", "norm", "copy"))
    ds["g_ch_q"], ds["g_ch_k"] = dg_ch[0], dg_ch[1]
    ds["rel_bias"] = relbias_tiles_bwd(dtiles)
    dqb, dkb, dvb, dcq, dck = attn_bwd(s["qb"], s["kb"], z, lambda h: FOX_B + 2 * N_HEADS + h, s["o_b"], do[1],
                                       s["lse_b"], mode="fox", name="attn_fox_bwd", n_heads=N_HEADS,
                                       cq=s["cq"], ck=s["ck"])
    d_fox, dg_fox = prep_bwd_groups(z, FOX_B, [dqb, dkb, dvb], [P["g_fox_q"], P["g_fox_k"], None],
                                    name="prep_h_bwd", n_heads=N_HEADS, kinds=("norm", "norm", "copy"))
    ds["g_fox_q"], ds["g_fox_k"] = dg_fox[0], dg_fox[1]
    dcum = jnp.pad((dcq[:, :, 0] + dck[:, 0, :]).T, ((0, 0), (0, 128 - N_HEADS)))
    dff, dbf = fox_cum_bwd(z, P["b_f"], dcum)
    ds["b_f"] = dbf[:, :N_HEADS]
    dqa, dka, dva = attn_bwd(s["qa"], s["ka"], s["kv_raw"], lambda h: 2 * h + 1, s["o_a"], do[0], s["lse_a"],
                             mode="mla", name="attn_mla_bwd", n_heads=N_HEADS)
    dq_raw, dgq = prep_bwd_q(s["q_raw"], dqa, P["g_mla_q"], name="prep_mla_q_bwd", n_heads=N_HEADS,
                             dh=MLA_PAD, n_real=MLA_QK, rope=rope)
    dkv_raw, dkr, dgk = prep_bwd_mla_k(s["kv_raw"], z, dka, dva, P["g_mla_k"], rope, name="prep_mla_k_bwd")
    ds["g_mla_q"], ds["g_mla_k"] = dgq[:, :MLA_QK], dgk[:, :MLA_QK]
    dcq_n = mm_nt(dq_raw, W["uq_p"], name="mm_uq_dx", out_dtype=F32)
    dw["uq_p"] = mm_tn(s["cq_n"], dq_raw, nb=1, name="mm_uq_dw", out_dtype=BF16)
    dckv_n = mm_nt(dkv_raw, W["ukv"], name="mm_ukv_dx", out_dtype=F32)
    dw["ukv"] = mm_tn(s["ckv_n"], dkv_raw, nb=4, name="mm_ukv_dw", out_dtype=BF16)
    d_cq, ds["g_cq"] = rms_bwd(z, P["g_cq"], dcq_n, name="rms_cq_bwd", col0=Z_CQ, width=Q_LORA, dx_dtype=BF16)
    d_ckv, ds["g_ckv"] = rms_bwd(z, P["g_ckv"], dckv_n, name="rms_ckv_bwd", col0=Z_CKV, width=KV_LORA,
                                 dx_dtype=BF16)
    dz = jnp.concatenate([d_cq, d_ckv, dkr.astype(BF16), dff.astype(BF16), d_fox, d_ch, *dgl], axis=1)
    dh = mm_nt(dz, W["in_p"], name="mm_in_dx", out_dtype=F32)
    dw["in_p"] = mm_tn(s["h"], dz, nb=1, name="mm_in_dw", out_dtype=BF16)
    g0, ds["g_mix"] = rms_bwd(x, P["g_mix"], dh, name="rms_d_bwd", res=g1)
    return g0, dw, ds


def local_step(x, mem, target, Ps, weights_of, grads_done):
    rope = rope_tables(x.shape[0])
    Ws, saved, xs = [], [], [x]
    for l, P in enumerate(Ps):
        Ws.append(weights_of(l, xs[-1]))
        y, s = layer_fwd(xs[-1], mem, Ws[l], P, rope)
        xs.append(y)
        saved.append(s)
    g, loss = loss_head(xs[-1], target)
    dss = []
    for l in reversed(range(len(Ps))):
        g, dw, ds = layer_bwd(g, xs[l], mem, Ws[l], Ps[l], rope, saved[l])
        grads_done(l, dw)
        dss.append(ds)
    return loss, g, dss[::-1]


HBM_SPEC = pl.BlockSpec(memory_space=pltpu.HBM)


def _place():
    return lax.axis_index("x"), lax.axis_index("y"), lax.axis_index("c")


def _other_chips(x, y):
    return [(1 - x, y), (x, 1 - y), (1 - x, 1 - y)]


def _remote(src, dst, send_sems, recv_sems, k, to):
    return pltpu.make_async_remote_copy(src_ref=src, dst_ref=dst, send_sem=send_sems.at[k],
                                        recv_sem=recv_sems.at[k], device_id=to, device_id_type=MESH)


def _comm_call(body, *, name, ins, out_shapes, n_sem):
    return pl.pallas_call(
        body, out_shape=tuple(out_shapes), in_specs=[HBM_SPEC] * len(ins),
        out_specs=tuple([HBM_SPEC] * len(out_shapes)),
        scratch_shapes=[pltpu.SemaphoreType.DMA((n_sem,)), pltpu.SemaphoreType.DMA((n_sem,))],
        name=name, interpret=False)(*ins)


SEM_SPEC = pl.BlockSpec(memory_space=pltpu.SEMAPHORE)
SPLIT_EFFECT = pltpu.SideEffectType.DATAFLOW_SIDE_EFFECTING


def _chip_exchange_copies(kind, srcs, lands, send_sems, recv_sems):
    x, y, c = _place()
    me = 2 * x + y
    cps = []
    for i, (src, land) in enumerate(zip(srcs, lands)):
        for j, (cx, cy) in enumerate(_other_chips(x, y)):
            if kind == "gather":
                h = src.shape[0] // 2
                s_ref, d_ref = src.at[pl.ds(c * h, h), :], land.at[me, pl.ds(c * h, h), :]
            else:
                s_ref, d_ref = src.at[2 * cx + cy], land.at[me]
            cps.append(_remote(s_ref, d_ref, send_sems, recv_sems, 3 * i + j, (cx, cy, c)))
    return cps


def chip_exchange_start(kind, srcs, land_shapes, *, name):
    n = len(srcs)

    def body(*refs):
        send_sems, recv_sems = refs[2 * n], refs[2 * n + 1]
        for cp in _chip_exchange_copies(kind, refs[:n], refs[n:2 * n], send_sems, recv_sems):
            cp.start()
        refs[-1][...] = jnp.zeros_like(refs[-1])

    hbm = lambda a: pltpu.with_memory_space_constraint(a, pltpu.HBM)
    ins = [hbm(s) for s in srcs] + [hbm(lax.empty(s.shape, s.dtype)) for s in land_shapes]
    sems = pltpu.SemaphoreType.DMA((3 * n,))
    out = pl.pallas_call(
        body, name=name, interpret=False,
        out_shape=(sems, sems, *[pltpu.HBM(a.shape, a.dtype) for a in ins], _sds((8, 128), F32)),
        in_specs=[HBM_SPEC] * (2 * n),
        out_specs=(SEM_SPEC, SEM_SPEC, *[HBM_SPEC] * (2 * n), pl.BlockSpec(memory_space=pltpu.VMEM)),
        input_output_aliases={i: 2 + i for i in range(2 * n)},
        compiler_params=pltpu.CompilerParams(has_side_effects=SPLIT_EFFECT))(*ins)
    return out[0], out[1], list(out[2:2 + n]), list(out[2 + n:2 + 2 * n]), out[-1]


def chip_exchange_wait(kind, started, after, *, name):
    send_sems, recv_sems, srcs, lands, _ = started
    n = len(srcs)

    def body(*refs):
        send_sems, recv_sems = refs[2 * n], refs[2 * n + 1]
        for cp in _chip_exchange_copies(kind, refs[:n], refs[n:2 * n], send_sems, recv_sems):
            cp.wait_send()
            cp.wait_recv()

    thru = srcs + lands
    out = pl.pallas_call(
        body, name=name, interpret=False,
        out_shape=tuple(pltpu.HBM(a.shape, a.dtype) for a in thru),
        in_specs=[HBM_SPEC] * (2 * n) + [SEM_SPEC, SEM_SPEC, pl.BlockSpec(memory_space=pl.ANY)],
        out_specs=tuple([HBM_SPEC] * (2 * n)),
        input_output_aliases={i: i for i in range(2 * n)},
        compiler_params=pltpu.CompilerParams(has_side_effects=SPLIT_EFFECT))(*thru, send_sems, recv_sems, after)
    return list(out[:n]), list(out[n:])


def gather_to_sibling(lands):
    n = len(lands)

    def body(*refs):
        outs = refs[n:2 * n]
        send_sems, recv_sems = refs[2 * n:]
        x, y, c = _place()
        sends = []
        for i in range(n):
            h = outs[i].shape[1] // 2
            for j, (cx, cy) in enumerate(_other_chips(x, y)):
                landed = outs[i].at[2 * cx + cy, pl.ds(c * h, h), :]
                cp = _remote(landed, landed, send_sems, recv_sems, 3 * i + j, (x, y, 1 - c))
                cp.start()
                sends.append(cp)
        for cp in sends:
            cp.wait()

    return pl.pallas_call(
        body, out_shape=tuple(_sds(a.shape, a.dtype) for a in lands), in_specs=[HBM_SPEC] * n,
        out_specs=tuple([HBM_SPEC] * n), input_output_aliases={i: i for i in range(n)},
        scratch_shapes=[pltpu.SemaphoreType.DMA((3 * n,)), pltpu.SemaphoreType.DMA((3 * n,))],
        name="gather_to_sibling", interpret=False)(*lands)


def rs_to_sibling(parts):
    n = len(parts)

    def body(*refs):
        srcs, outs = refs[:n], refs[n:2 * n]
        send_sems, recv_sems = refs[2 * n:]
        x, y, c = _place()
        sends = []
        for i in range(n):
            h = srcs[i].shape[1] // 2
            cp = _remote(srcs[i].at[:, pl.ds((1 - c) * h, h), :], outs[i], send_sems, recv_sems, i, (x, y, 1 - c))
            cp.start()
            sends.append(cp)
        for cp in sends:
            cp.wait()

    return _comm_call(body, name="rs_to_sibling", ins=parts, n_sem=n,
                      out_shapes=[_sds((4, p.shape[1] // 2, p.shape[2]), p.dtype) for p in parts])


def rs_share_halves(ts):
    n = len(ts)

    def body(*refs):
        srcs, outs = refs[:n], refs[n:2 * n]
        send_sems, recv_sems = refs[2 * n:]
        x, y, c = _place()
        sends = []
        for i in range(n):
            cp = _remote(srcs[i], outs[i].at[c], send_sems, recv_sems, i, (x, y, 1 - c))
            cp.start()
            sends.append(cp)
        for cp in sends:
            cp.wait()

    return _comm_call(body, name="rs_share_halves", ins=ts, n_sem=n,
                      out_shapes=[_sds((2,) + t.shape, t.dtype) for t in ts])


def add_pair(part, got, half_idx):
    _, a, b = part.shape
    h = a // 2
    tr = _pick(h, (512, 256, 128))

    def body(c_ref, p_ref, g_ref, o_ref):
        o_ref[...] = (p_ref[...].astype(F32) + g_ref[...].astype(F32)).astype(o_ref.dtype)

    spec = pl.BlockSpec((None, tr, b), lambda s_, i, c_ref: (s_, i, 0))
    grid_spec = pltpu.PrefetchScalarGridSpec(
        num_scalar_prefetch=1, grid=(4, h // tr),
        in_specs=[pl.BlockSpec((None, None, tr, b), lambda s_, i, c_ref: (s_, c_ref[0], i, 0)), spec],
        out_specs=spec)
    return pl.pallas_call(
        body, out_shape=_sds((4, h, b), part.dtype), grid_spec=grid_spec, name="rs_add_pair", interpret=False,
        compiler_params=pltpu.CompilerParams(vmem_limit_bytes=VMEM_LIMIT_BYTES))(
            half_idx, part.reshape(4, 2, h, b), got)


def sum_slots(r):
    _, h, b = r.shape
    tr = _pick(h, (512, 256, 128))

    def body(r0, r1, r2, r3, o_ref):
        o_ref[...] = ((r0[...].astype(F32) + r1[...].astype(F32)) + r2[...].astype(F32)) + r3[...].astype(F32)

    specs = [pl.BlockSpec((None, tr, b), functools.partial(lambda i, s_: (s_, i, 0), s_=s_)) for s_ in range(4)]
    return _pcall(body, name="rs_sum_slots", out_shape=_sds((h, b), F32), grid=(h // tr,),
                  in_specs=specs, out_specs=pl.BlockSpec((tr, b), lambda i: (i, 0)))(r, r, r, r)


def reduce_scatter_start(parts, core, *, name):
    half_idx = core.reshape(1).astype(jnp.int32)
    gots = rs_to_sibling(parts)
    chip_sums = [add_pair(p, g, half_idx) for p, g in zip(parts, gots)]
    return chip_exchange_start("scatter", chip_sums, chip_sums, name=name)


def reduce_scatter_finish(started, after, chip, core, *, name):
    chip_sums, slots = chip_exchange_wait("scatter", started, after, name=name)
    slots = [lax.dynamic_update_slice(s_, lax.dynamic_index_in_dim(q, chip, 0, keepdims=True), (chip, 0, 0))
             for s_, q in zip(slots, chip_sums)]
    halves = [sum_slots(s_) for s_ in slots]
    both = rs_share_halves(halves)
    both = [lax.dynamic_update_slice(o, t[None], (core, 0, 0)) for o, t in zip(both, halves)]
    return [o.reshape(2 * o.shape[1], o.shape[2]) for o in both]


def allreduce_small(v):
    Rs = v.shape[0]

    def body(v_ref, o_ref, buf, send_sems, recv_sems):
        x, y, c = _place()
        me = 4 * x + 2 * y + c
        buf[me] = v_ref[...]
        flips = [(fx, fy, fc) for fx in (0, 1) for fy in (0, 1) for fc in (0, 1)][1:]
        sends = []
        for k, (fx, fy, fc) in enumerate(flips):
            to = ((1 - x) if fx else x, (1 - y) if fy else y, (1 - c) if fc else c)
            cp = _remote(v_ref, buf.at[me], send_sems, recv_sems, k, to)
            cp.start()
            sends.append(cp)
        for cp in sends:
            cp.wait()
        acc = buf[0]
        for d in range(1, 8):
            acc = acc + buf[d]
        o_ref[...] = acc

    vm = pl.BlockSpec(memory_space=pltpu.VMEM)
    return pl.pallas_call(
        body, out_shape=_sds((Rs, 128), F32), in_specs=[vm], out_specs=vm,
        scratch_shapes=[pltpu.VMEM((8, Rs, 128), F32), pltpu.SemaphoreType.DMA((7,)),
                        pltpu.SemaphoreType.DMA((7,))],
        name="allreduce_small", interpret=False)(v)


INPUT_NAMES = (("x", "mem") + WEIGHT_ORDER + ("loss_target",) + tuple("m_" + n for n in WEIGHT_ORDER)
               + tuple("v_" + n for n in WEIGHT_ORDER))


def _pack_small(vals, n_layers, extra=None):
    flat = jnp.concatenate([vals[n].reshape(n_layers, -1).astype(F32) for n in SMALL_ORDER], axis=1).reshape(-1)
    if extra is not None:
        flat = jnp.concatenate([flat, extra.reshape(-1)])
    n = flat.shape[0]
    rows = -(-n // 1024) * 8
    return jnp.pad(flat, (0, rows * 128 - n)).reshape(rows, 128)


def _unpack_small(packed, like, n_layers):
    per_layer = sum(int(np.prod(like[n].shape[1:])) for n in SMALL_ORDER)
    body = packed.reshape(-1)[:n_layers * per_layer].reshape(n_layers, per_layer)
    out, off = {}, 0
    for n in SMALL_ORDER:
        k = int(np.prod(like[n].shape[1:]))
        out[n] = body[:, off:off + k].reshape(like[n].shape)
        off += k
    return out, packed.reshape(-1)[n_layers * per_layer]


def kernel(x, mem, g_mix, w_in, g_cq, w_uq, g_ckv, w_ukv, g_mla_q, g_mla_k, b_f, g_fox_q, g_fox_k, rel_bias, g_ch_q, g_ch_k, w_br, w_out, g_cross, g_mem, w_xq, w_xkv, g_x_q, g_x_k, w_xo, g_mlp, w_1, w_2, loss_target, m_g_mix, m_w_in, m_g_cq, m_w_uq, m_g_ckv, m_w_ukv, m_g_mla_q, m_g_mla_k, m_b_f, m_g_fox_q, m_g_fox_k, m_rel_bias, m_g_ch_q, m_g_ch_k, m_w_br, m_w_out, m_g_cross, m_g_mem, m_w_xq, m_w_xkv, m_g_x_q, m_g_x_k, m_w_xo, m_g_mlp, m_w_1, m_w_2, v_g_mix, v_w_in, v_g_cq, v_w_uq, v_g_ckv, v_w_ukv, v_g_mla_q, v_g_mla_k, v_b_f, v_g_fox_q, v_g_fox_k, v_rel_bias, v_g_ch_q, v_g_ch_k, v_w_br, v_w_out, v_g_cross, v_g_mem, v_w_xq, v_w_xkv, v_g_x_q, v_g_x_k, v_w_xo, v_g_mlp, v_w_1, v_w_2):
    d = dict(zip(INPUT_NAMES, (x, mem, g_mix, w_in, g_cq, w_uq, g_ckv, w_ukv, g_mla_q, g_mla_k, b_f, g_fox_q, g_fox_k, rel_bias, g_ch_q, g_ch_k, w_br, w_out, g_cross, g_mem, w_xq, w_xkv, g_x_q, g_x_k, w_xo, g_mlp, w_1, w_2, loss_target, m_g_mix, m_w_in, m_g_cq, m_w_uq, m_g_ckv, m_w_ukv, m_g_mla_q, m_g_mla_k, m_b_f, m_g_fox_q, m_g_fox_k, m_rel_bias, m_g_ch_q, m_g_ch_k, m_w_br, m_w_out, m_g_cross, m_g_mem, m_w_xq, m_w_xkv, m_g_x_q, m_g_x_k, m_w_xo, m_g_mlp, m_w_1, m_w_2, v_g_mix, v_w_in, v_g_cq, v_w_uq, v_g_ckv, v_w_ukv, v_g_mla_q, v_g_mla_k, v_b_f, v_g_fox_q, v_g_fox_k, v_rel_bias, v_g_ch_q, v_g_ch_k, v_w_br, v_w_out, v_g_cross, v_g_mem, v_w_xq, v_w_xkv, v_g_x_q, v_g_x_k, v_w_xo, v_g_mlp, v_w_1, v_w_2)))
    n_layers = g_mix.shape[0]
    assert x.shape[0] == 1 and x.shape[2] == D_MODEL and mem.shape[1:] == (MEM_LEN, D_MODEL)
    for n in PACK_ORDER:
        assert d[n].shape[1:] == SHARD_SHAPES[n], (n, d[n].shape)

    chip = 2 * lax.axis_index("x") + lax.axis_index("y")
    core = lax.axis_index("c")

    gathers = []
    for l in range(n_layers):
        mine = [_rows2d(d[n][l].astype(BF16), n) for n in PACK_ORDER]
        gathers.append(chip_exchange_start("gather", mine, [_sds((4,) + m_.shape, BF16) for m_ in mine],
                                           name=f"gather_start_{l}"))

    def weights_of(l, x_l):
        mine, lands = chip_exchange_wait("gather", gathers[l], x_l, name=f"gather_wait_{l}")
        full = [lax.dynamic_update_slice(t, m_[None], (chip, 0, 0)) for t, m_ in zip(gather_to_sibling(lands), mine)]
        return full_weights({n: f.reshape((4,) + SHARD_SHAPES[n]) for n, f in zip(PACK_ORDER, full)})

    scatters = {}

    def grads_done(l, dw):
        sg = shard_grads(dw)
        scatters[l] = reduce_scatter_start([_rows2d(sg[n], n) for n in PACK_ORDER], core, name=f"scatter_start_{l}")

    Ps = [layer_params(d, l) for l in range(n_layers)]
    loss, dx, dss = local_step(x[0], mem[0], loss_target[0], Ps, weights_of, grads_done)
    big = [None] * n_layers
    for l in reversed(range(n_layers)):
        big[l] = reduce_scatter_finish(scatters[l], dx, chip, core, name=f"scatter_wait_{l}")

    small_local = {n: jnp.stack([dss[l][n].reshape(d[n].shape[1:]) for l in range(n_layers)]) for n in SMALL_ORDER}
    small_sum, loss_sum = _unpack_small(allreduce_small(_pack_small(small_local, n_layers, extra=loss)),
                                        {n: d[n] for n in SMALL_ORDER}, n_layers)

    grads, delta, new_m, new_v = {}, {}, {}, {}
    for i, n in enumerate(PACK_ORDER):
        outs = adamw(_rows2d(d[n], n), [big[l][i] for l in range(n_layers)], _rows2d(d["m_" + n], n),
                     _rows2d(d["v_" + n], n), name="adamw_" + n)
        grads[n], delta[n], new_m[n], new_v[n] = (o.reshape(d[n].shape) for o in outs)
    like = {n: d[n] for n in SMALL_ORDER}
    sm = adamw(_pack_small(like, n_layers)[None], [_pack_small(small_sum, n_layers)],
               _pack_small({n: d["m_" + n] for n in SMALL_ORDER}, n_layers)[None],
               _pack_small({n: d["v_" + n] for n in SMALL_ORDER}, n_layers)[None], name="adamw_small")
    for res, src in zip((grads, delta, new_m, new_v), sm):
        res.update(_unpack_small(src[0], like, n_layers)[0])

    return (loss_sum, dx[None], *[grads[n] for n in WEIGHT_ORDER], *[delta[n] for n in WEIGHT_ORDER],
            *[new_m[n] for n in WEIGHT_ORDER], *[new_v[n] for n in WEIGHT_ORDER])
```

```python
import functools

import numpy as np
import jax
import jax.numpy as jnp
from jax import lax
from jax.experimental import pallas as pl
from jax.experimental.pallas import tpu as pltpu

F32 = jnp.float32
BF16 = jnp.bfloat16
MXU_DTYPE = jnp.bfloat16
MESH = pl.DeviceIdType.MESH

D_MODEL = 2048
MIX_W = 1024
N_HEADS = 8
DH = 128
MLA_NOPE = 128
MLA_ROPE = 64
MLA_QK = MLA_NOPE + MLA_ROPE
MLA_PAD = 256
Q_LORA = 512
KV_LORA = 256
CHUNK = 64
CHUNK_SHIFT = CHUNK.bit_length() - 1
LEFT_CHUNKS = 8
REL_CLIP = 128
N_REL = 2 * REL_CLIP + 1
X_HEADS = 4
MEM_LEN = 256
D_FF = 8192
ROPE_THETA = 10000.0
EPS = 1e-6
NEG = -1e30

Z_CQ, Z_CKV, Z_KR, Z_FF = 0, 512, 768, 896
ZS_W = 1024
Z_FOX = 1024
Z_CH = Z_FOX + 3 * MIX_W
Z_GATE = Z_CH + 3 * MIX_W
Z_TOT = Z_GATE + 3 * D_MODEL
W_IN_CUTS = (0, 512, 768, 832, 3904, 3912, 6984, 13128)
W_IN_SHARD = W_IN_CUTS[-1] // 4
W_IN_PIECES = ((0, 0, 832), (Z_FF, 3904, 8), (Z_FOX, 832, 3072), (Z_CH, 3912, 9216))

ADAM_LR, ADAM_B1, ADAM_B2, ADAM_EPS, ADAM_WD, ADAM_STEP = 0.001, 0.9, 0.999, 1e-08, 0.01, 10

VMEM_LIMIT_BYTES = 56 * 1024 * 1024
PACK_ORDER = ("w_uq", "w_ukv", "w_br", "w_out", "w_xq", "w_xkv", "w_xo", "w_1", "w_2", "w_in")
SMALL_ORDER = ("g_mix", "g_cq", "g_ckv", "g_mla_q", "g_mla_k", "b_f", "g_fox_q", "g_fox_k", "rel_bias",
               "g_ch_q", "g_ch_k", "g_cross", "g_mem", "g_x_q", "g_x_k", "g_mlp")
WEIGHT_ORDER = ("g_mix", "w_in", "g_cq", "w_uq", "g_ckv", "w_ukv", "g_mla_q", "g_mla_k", "b_f", "g_fox_q",
                "g_fox_k", "rel_bias", "g_ch_q", "g_ch_k", "w_br", "w_out", "g_cross", "g_mem", "w_xq",
                "w_xkv", "g_x_q", "g_x_k", "w_xo", "g_mlp", "w_1", "w_2")


def _pick(n, prefs):
    for p in prefs:
        if n % p == 0:
            return p
    raise ValueError(f"no block size among {prefs} divides {n}")


def _pcall(body, *, name, out_shape, in_specs, out_specs, grid=(), scratch=(), aliases=None):
    return pl.pallas_call(
        body, out_shape=out_shape, grid=grid, in_specs=in_specs, out_specs=out_specs,
        scratch_shapes=scratch, name=name, interpret=False,
        input_output_aliases=aliases or {},
        compiler_params=pltpu.CompilerParams(vmem_limit_bytes=VMEM_LIMIT_BYTES))


def _sds(shape, dtype):
    return jax.ShapeDtypeStruct(tuple(shape), dtype)


def _mx(v):
    return v.astype(MXU_DTYPE)


def mm_nn(a, b3, *, name, out_dtype, a_col0=0, res=None, relu2=False):
    M = a.shape[0]
    nb, K, Ns = b3.shape
    N = nb * Ns
    tm = _pick(M, (1024, 512, 256, 128))
    tk = _pick(K, (2048, 1024, 512, 256))
    tn = _pick(Ns, (512, 256, 128))
    assert a_col0 % tk == 0
    nk, nbs, ka0 = K // tk, Ns // tn, a_col0 // tk
    n_out = 2 if relu2 else 1

    def body(*refs):
        a_ref, b_ref = refs[0], refs[1]
        pos = 2
        res_ref = None
        if res is not None:
            res_ref = refs[pos]
            pos += 1
        outs = refs[pos:pos + n_out]
        acc_ref = refs[pos + n_out] if nk > 1 else None
        part = jnp.dot(_mx(a_ref[...]), _mx(b_ref[...]), preferred_element_type=F32)

        def finish(acc):
            if res_ref is not None:
                acc = acc + res_ref[...]
            outs[0][...] = acc.astype(outs[0].dtype)
            if relu2:
                r = jnp.maximum(acc, 0.0)
                outs[1][...] = (r * r).astype(outs[1].dtype)

        if nk == 1:
            finish(part)
        else:
            k = pl.program_id(2)

            @pl.when(k == 0)
            def _():
                acc_ref[...] = part

            @pl.when(k > 0)
            def _():
                acc_ref[...] += part

            @pl.when(k == nk - 1)
            def _():
                finish(acc_ref[...])

    in_specs = [pl.BlockSpec((tm, tk), lambda i, j, k: (i, ka0 + k)),
                pl.BlockSpec((None, tk, tn), lambda i, j, k: (j // nbs, k, j % nbs))]
    args = [a, b3]
    if res is not None:
        in_specs.append(pl.BlockSpec((tm, tn), lambda i, j, k: (i, j)))
        args.append(res)
    o_spec = pl.BlockSpec((tm, tn), lambda i, j, k: (i, j))
    if relu2:
        out_shape, out_specs = (_sds((M, N), out_dtype), _sds((M, N), out_dtype)), (o_spec, o_spec)
    else:
        out_shape, out_specs = _sds((M, N), out_dtype), o_spec
    scratch = (pltpu.VMEM((tm, tn), F32),) if nk > 1 else ()
    return _pcall(body, name=name, out_shape=out_shape, grid=(M // tm, N // tn, nk),
                  in_specs=in_specs, out_specs=out_specs, scratch=scratch)(*args)


def mm_nt(a, b3, *, name, out_dtype, a_col0=0, res=None, relu_mul=None):
    M = a.shape[0]
    nb, K, Ns = b3.shape
    tm = _pick(M, (1024, 512, 256, 128))
    tk = _pick(K, (1024, 512, 256))
    tn = _pick(Ns, (1024, 512, 256, 128))
    assert a_col0 % tn == 0
    nbs = Ns // tn
    nn, a0 = nb * nbs, a_col0 // tn

    def body(*refs):
        a_ref, b_ref = refs[0], refs[1]
        pos = 2
        mul_ref = res_ref = None
        if relu_mul is not None:
            mul_ref = refs[pos]
            pos += 1
        if res is not None:
            res_ref = refs[pos]
            pos += 1
        o_ref = refs[pos]
        acc_ref = refs[pos + 1] if nn > 1 else None
        part = lax.dot_general(_mx(a_ref[...]), _mx(b_ref[...]), (((1,), (1,)), ((), ())),
                               preferred_element_type=F32)

        def finish(acc):
            if mul_ref is not None:
                acc = acc * (2.0 * jnp.maximum(mul_ref[...].astype(F32), 0.0))
            if res_ref is not None:
                acc = acc + res_ref[...]
            o_ref[...] = acc.astype(o_ref.dtype)

        if nn == 1:
            finish(part)
        else:
            j = pl.program_id(2)

            @pl.when(j == 0)
            def _():
                acc_ref[...] = part

            @pl.when(j > 0)
            def _():
                acc_ref[...] += part

            @pl.when(j == nn - 1)
            def _():
                finish(acc_ref[...])

    in_specs = [pl.BlockSpec((tm, tn), lambda i, kk, j: (i, a0 + j)),
                pl.BlockSpec((None, tk, tn), lambda i, kk, j: (j // nbs, kk, j % nbs))]
    args = [a, b3]
    for extra in (relu_mul, res):
        if extra is not None:
            in_specs.append(pl.BlockSpec((tm, tk), lambda i, kk, j: (i, kk)))
            args.append(extra)
    scratch = (pltpu.VMEM((tm, tk), F32),) if nn > 1 else ()
    return _pcall(body, name=name, out_shape=_sds((M, K), out_dtype), grid=(M // tm, K // tk, nn),
                  in_specs=in_specs, out_specs=pl.BlockSpec((tm, tk), lambda i, kk, j: (i, kk)),
                  scratch=scratch)(*args)


def mm_tn(a, c, *, nb, name, out_dtype, K=None, N=None, a_col0=0, c_col0=0):
    M = a.shape[0]
    K = K or a.shape[1]
    N = N or c.shape[1]
    Ns = N // nb
    tm = _pick(M, (2048, 1024, 512, 256))
    tk = _pick(K, (1024, 512, 256))
    tn = _pick(Ns, (1024, 512, 256, 128))
    assert a_col0 % tk == 0 and c_col0 % tn == 0
    nm, nbs, a0, c0 = M // tm, Ns // tn, a_col0 // tk, c_col0 // tn

    def body(*refs):
        a_ref, c_ref, o_ref = refs[:3]
        acc_ref = refs[3] if nm > 1 else None
        part = lax.dot_general(_mx(a_ref[...]), _mx(c_ref[...]), (((0,), (0,)), ((), ())),
                               preferred_element_type=F32)
        if nm == 1:
            o_ref[...] = part.astype(o_ref.dtype)
        else:
            m = pl.program_id(2)

            @pl.when(m == 0)
            def _():
                acc_ref[...] = part

            @pl.when(m > 0)
            def _():
                acc_ref[...] += part

            @pl.when(m == nm - 1)
            def _():
                o_ref[...] = acc_ref[...].astype(o_ref.dtype)

    scratch = (pltpu.VMEM((tk, tn), F32),) if nm > 1 else ()
    return _pcall(
        body, name=name, out_shape=_sds((nb, K, Ns), out_dtype), grid=(N // tn, K // tk, nm),
        in_specs=[pl.BlockSpec((tm, tk), lambda j, kk, m: (m, a0 + kk)),
                  pl.BlockSpec((tm, tn), lambda j, kk, m: (m, c0 + j))],
        out_specs=pl.BlockSpec((None, tk, tn), lambda j, kk, m: (j // nbs, kk, j % nbs)),
        scratch=scratch)(a, c)


def rms_fwd(x, g, *, name, col0=0, width=None, out_dtype=BF16):
    R = x.shape[0]
    width = width or x.shape[1]
    assert col0 % width == 0
    cb = col0 // width
    tr = _pick(R, (512, 256, 128))

    def body(x_ref, g_ref, o_ref):
        xf = x_ref[...].astype(F32)
        r = lax.rsqrt(jnp.mean(xf * xf, axis=1, keepdims=True) + EPS)
        o_ref[...] = (xf * r * g_ref[...]).astype(o_ref.dtype)

    return _pcall(body, name=name, out_shape=_sds((R, width), out_dtype), grid=(R // tr,),
                  in_specs=[pl.BlockSpec((tr, width), lambda i: (i, cb)),
                            pl.BlockSpec((1, width), lambda i: (0, 0))],
                  out_specs=pl.BlockSpec((tr, width), lambda i: (i, 0)))(x, g)


def rms_bwd(x, g, dy, *, name, col0=0, width=None, res=None, dx_dtype=F32, need_dx=True):
    R = x.shape[0]
    width = width or x.shape[1]
    cb = col0 // width
    tr = _pick(R, (512, 256, 128))

    def body(*refs):
        x_ref, g_ref, dy_ref = refs[:3]
        pos = 3
        res_ref = None
        if res is not None:
            res_ref = refs[pos]
            pos += 1
        dx_ref = None
        if need_dx:
            dx_ref = refs[pos]
            pos += 1
        dg_ref = refs[pos]
        xf = x_ref[...].astype(F32)
        dyf = dy_ref[...].astype(F32)
        r = lax.rsqrt(jnp.mean(xf * xf, axis=1, keepdims=True) + EPS)
        xh = xf * r
        if need_dx:
            gy = dyf * g_ref[...]
            dx = r * (gy - xh * jnp.mean(gy * xh, axis=1, keepdims=True))
            if res_ref is not None:
                dx = dx + res_ref[...]
            dx_ref[...] = dx.astype(dx_ref.dtype)
        part = jnp.sum(dyf * xh, axis=0, keepdims=True)

        @pl.when(pl.program_id(0) == 0)
        def _():
            dg_ref[...] = part

        @pl.when(pl.program_id(0) > 0)
        def _():
            dg_ref[...] += part

    in_specs = [pl.BlockSpec((tr, width), lambda i: (i, cb)),
                pl.BlockSpec((1, width), lambda i: (0, 0)),
                pl.BlockSpec((tr, width), lambda i: (i, 0))]
    args = [x, g, dy]
    if res is not None:
        in_specs.append(pl.BlockSpec((tr, width), lambda i: (i, 0)))
        args.append(res)
    dg_shape, dg_spec = _sds((1, width), F32), pl.BlockSpec((1, width), lambda i: (0, 0))
    if need_dx:
        out_shape = (_sds((R, width), dx_dtype), dg_shape)
        out_specs = (pl.BlockSpec((tr, width), lambda i: (i, 0)), dg_spec)
    else:
        out_shape, out_specs = dg_shape, dg_spec
    out = _pcall(body, name=name, out_shape=out_shape, grid=(R // tr,), in_specs=in_specs,
                 out_specs=out_specs)(*args)
    return out if need_dx else (None, out)


HEAD_ROW_BLOCKS = (2048, 1024, 512, 256, 128)


def _rope_apply(y, c, sa, sb):
    return y * c + pltpu.roll(y, 96, 1) * sa + pltpu.roll(y, 32, 1) * sb


def _rope_transpose(dy, c, sa, sb):
    return dy * c + pltpu.roll(dy * sa, 32, 1) + pltpu.roll(dy * sb, 96, 1)


def rope_tables(seq):
    pos = jnp.arange(seq, dtype=F32)
    inv = ROPE_THETA ** (-jnp.arange(0, MLA_ROPE, 2, dtype=F32) / MLA_ROPE)
    ang = pos[:, None] * inv[None, :]
    cos, sin = jnp.cos(ang), jnp.sin(ang)
    z32, z64 = jnp.zeros_like(cos), jnp.zeros((seq, 64), F32)
    c = jnp.concatenate([cos, cos, z64], axis=1)
    sa = jnp.concatenate([-sin, z32, z64], axis=1)
    sb = jnp.concatenate([z32, sin, z64], axis=1)
    return c, sa, sb


def _head_vec(part_refs):
    xs = [p[...].astype(F32) for p in part_refs]
    return xs[0] if len(xs) == 1 else jnp.concatenate(xs, axis=1)


def prep_fwd(parts, g, *, name, n_heads, n_real, rope=None):
    rows = parts[0][0].shape[0]
    dh = sum(w for _, w, _ in parts)
    tr = _pick(rows, HEAD_ROW_BLOCKS)
    npart = len(parts)

    def body(*refs):
        part_refs, g_ref = refs[:npart], refs[npart]
        pos = npart + 1
        if rope is not None:
            c_ref, sa_ref, sb_ref = refs[pos:pos + 3]
            pos += 3
        o_ref = refs[pos]
        x = _head_vec(part_refs)
        r = lax.rsqrt(jnp.sum(x * x, axis=1, keepdims=True) * (1.0 / n_real) + EPS)
        y = x * r * g_ref[...]
        if rope is not None:
            yr = _rope_apply(y[:, dh - 128:], c_ref[...], sa_ref[...], sb_ref[...])
            y = jnp.concatenate([y[:, :dh - 128], yr], axis=1)
        o_ref[...] = y.astype(o_ref.dtype)

    in_specs, args = [], []
    for arr, w, fn in parts:
        in_specs.append(pl.BlockSpec((tr, w), functools.partial(lambda h, i, fn: (i, fn(h)), fn=fn)))
        args.append(arr)
    in_specs.append(pl.BlockSpec((1, dh), lambda h, i: (0, 0)))
    args.append(g)
    if rope is not None:
        for t in rope:
            in_specs.append(pl.BlockSpec((tr, 128), lambda h, i: (i, 0)))
            args.append(t)
    return _pcall(body, name=name, out_shape=_sds((n_heads, rows, dh), BF16), grid=(n_heads, rows // tr),
                  in_specs=in_specs, out_specs=pl.BlockSpec((None, tr, dh), lambda h, i: (h, i, 0)))(*args)


def _norm_bwd(x, g, dyn, n_real):
    r = lax.rsqrt(jnp.sum(x * x, axis=1, keepdims=True) * (1.0 / n_real) + EPS)
    xh = x * r
    gy = dyn * g
    dx = r * (gy - xh * (jnp.sum(gy * xh, axis=1, keepdims=True) * (1.0 / n_real)))
    return dx, jnp.sum(dyn * xh, axis=0, keepdims=True)


def prep_bwd_q(src, dy, g, *, name, n_heads, dh, n_real, rope=None, out_dtype=BF16):
    rows = src.shape[0]
    tr = _pick(rows, HEAD_ROW_BLOCKS)

    def body(*refs):
        x_ref, dy_ref, g_ref = refs[:3]
        pos = 3
        if rope is not None:
            c_ref, sa_ref, sb_ref = refs[pos:pos + 3]
            pos += 3
        dx_ref, dg_ref = refs[pos], refs[pos + 1]
        dyn = dy_ref[...].astype(F32)
        if rope is not None:
            dr = _rope_transpose(dyn[:, dh - 128:], c_ref[...], sa_ref[...], sb_ref[...])
            dyn = jnp.concatenate([dyn[:, :dh - 128], dr], axis=1)
        dx, dg = _norm_bwd(x_ref[...].astype(F32), g_ref[...], dyn, n_real)
        dx_ref[...] = dx.astype(dx_ref.dtype)
        first = jnp.logical_and(pl.program_id(0) == 0, pl.program_id(1) == 0)

        @pl.when(first)
        def _():
            dg_ref[...] = dg

        @pl.when(jnp.logical_not(first))
        def _():
            dg_ref[...] += dg

    in_specs = [pl.BlockSpec((tr, dh), lambda i, h: (i, h)),
                pl.BlockSpec((None, tr, dh), lambda i, h: (h, i, 0)),
                pl.BlockSpec((1, dh), lambda i, h: (0, 0))]
    args = [src, dy, g]
    if rope is not None:
        for t in rope:
            in_specs.append(pl.BlockSpec((tr, 128), lambda i, h: (i, 0)))
            args.append(t)
    return _pcall(body, name=name, out_shape=(_sds((rows, n_heads * dh), out_dtype), _sds((1, dh), F32)),
                  grid=(rows // tr, n_heads), in_specs=in_specs,
                  out_specs=(pl.BlockSpec((tr, dh), lambda i, h: (i, h)),
                             pl.BlockSpec((1, dh), lambda i, h: (0, 0))))(*args)


def prep_bwd_mla_k(kv_raw, zs, dkf, dv, g, rope, *, name):
    rows = kv_raw.shape[0]
    tr = _pick(rows, HEAD_ROW_BLOCKS)

    def body(kn_ref, kr_ref, dy_ref, dv_ref, g_ref, c_ref, sa_ref, sb_ref, dkv_ref, dkr_ref, dg_ref):
        h = pl.program_id(1)
        x = jnp.concatenate([kn_ref[...].astype(F32), kr_ref[...].astype(F32)], axis=1)
        dyn = dy_ref[...].astype(F32)
        dr = _rope_transpose(dyn[:, 128:], c_ref[...], sa_ref[...], sb_ref[...])
        dyn = jnp.concatenate([dyn[:, :128], dr], axis=1)
        dx, dg = _norm_bwd(x, g_ref[...], dyn, MLA_QK)
        dkv_ref[...] = jnp.concatenate([dx[:, :128], dv_ref[...].astype(F32)], axis=1).astype(dkv_ref.dtype)

        @pl.when(h == 0)
        def _():
            dkr_ref[...] = dx[:, 128:]

        @pl.when(h > 0)
        def _():
            dkr_ref[...] += dx[:, 128:]

        first = jnp.logical_and(pl.program_id(0) == 0, h == 0)

        @pl.when(first)
        def _():
            dg_ref[...] = dg

        @pl.when(jnp.logical_not(first))
        def _():
            dg_ref[...] += dg

    tab = pl.BlockSpec((tr, 128), lambda i, h: (i, 0))
    return _pcall(
        body, name=name,
        out_shape=(_sds((rows, N_HEADS * 256), BF16), _sds((rows, 128), F32), _sds((1, MLA_PAD), F32)),
        grid=(rows // tr, N_HEADS),
        in_specs=[pl.BlockSpec((tr, 128), lambda i, h: (i, 2 * h)),
                  pl.BlockSpec((tr, 128), lambda i, h: (i, Z_KR // 128)),
                  pl.BlockSpec((None, tr, MLA_PAD), lambda i, h: (h, i, 0)),
                  pl.BlockSpec((None, tr, 128), lambda i, h: (h, i, 0)),
                  pl.BlockSpec((1, MLA_PAD), lambda i, h: (0, 0)), tab, tab, tab],
        out_specs=(pl.BlockSpec((tr, 256), lambda i, h: (i, h)),
                   pl.BlockSpec((tr, 128), lambda i, h: (i, 0)),
                   pl.BlockSpec((1, MLA_PAD), lambda i, h: (0, 0))))(kv_raw, zs, dkf, dv, g, *rope)


def prep_bwd_groups(src, base_blk, dys, gs, *, name, n_heads, kinds, out_dtype=BF16):
    rows = src.shape[0]
    ng = len(kinds)
    J = ng * n_heads
    tr = _pick(rows, HEAD_ROW_BLOCKS)
    gstack = jnp.stack([gs[k] if kinds[k] == "norm" else jnp.ones((1, DH), F32) for k in range(ng)])

    def body(*refs):
        x_ref = refs[0]
        dy_refs = refs[1:1 + ng]
        g_ref, dx_ref, dg_ref = refs[1 + ng:4 + ng]
        j, i = pl.program_id(0), pl.program_id(1)
        grp = j // n_heads
        dy = dy_refs[0][...].astype(F32)
        for k in range(1, ng):
            dy = jnp.where(grp == k, dy_refs[k][...].astype(F32), dy)
        dx, dg = _norm_bwd(x_ref[...].astype(F32), g_ref[...], dy, DH)
        is_copy = functools.reduce(jnp.logical_or, [grp == k for k in range(ng) if kinds[k] == "copy"],
                                   jnp.bool_(False))
        dx_ref[...] = jnp.where(is_copy, dy, dx).astype(dx_ref.dtype)
        dg = jnp.where(is_copy, jnp.zeros_like(dg), dg)
        first = jnp.logical_and(j % n_heads == 0, i == 0)

        @pl.when(first)
        def _():
            dg_ref[...] = dg

        @pl.when(jnp.logical_not(first))
        def _():
            dg_ref[...] += dg

    in_specs = [pl.BlockSpec((tr, DH), lambda j, i: (i, base_blk + j))]
    for k in range(ng):
        in_specs.append(pl.BlockSpec(
            (None, tr, DH),
            functools.partial(lambda j, i, k: (jnp.clip(j - k * n_heads, 0, n_heads - 1),
                                               jnp.where(j // n_heads == k, i, 0), 0), k=k)))
    in_specs.append(pl.BlockSpec((None, 1, DH), lambda j, i: (j // n_heads, 0, 0)))
    return _pcall(body, name=name, out_shape=(_sds((rows, J * DH), out_dtype), _sds((ng, 1, DH), F32)),
                  grid=(J, rows // tr), in_specs=in_specs,
                  out_specs=(pl.BlockSpec((tr, DH), lambda j, i: (i, j)),
                             pl.BlockSpec((None, 1, DH), lambda j, i: (j // n_heads, 0, 0))))(src, *dys, gstack)


def _attn_cfg(mode, sq, sk):
    if mode == "chunk":
        tq = 128
        win = min((LEFT_CHUNKS + 2) * CHUNK, sk)
    else:
        tq = _pick(sq, (256, 128))
        win = sk
    scale = (MLA_QK if mode == "mla" else DH) ** -0.5
    return tq, win, scale


def _attn_key_rows(mode, i, tq, win, sk, run):
    if mode == "chunk":
        start = pl.multiple_of(jnp.clip((i - LEFT_CHUNKS // 2) * 128, 0, sk - win), 128)
        run(pl.ds(start, win), start)
    elif mode == "cross":
        run(slice(0, sk), 0)
    else:
        lax.switch(i, [functools.partial(run, slice(0, (b + 1) * tq), 0) for b in range(sk // tq)])


def _attn_scores(mode, i, tq, scale, q, kk, start, cq, ck, t_ref):
    nk = kk.shape[0]
    s = lax.dot_general(q, kk, (((1,), (1,)), ((), ())), preferred_element_type=F32) * scale
    if mode == "cross":
        return s
    t_pos = i * tq + lax.broadcasted_iota(jnp.int32, (tq, nk), 0)
    s_pos = start + lax.broadcasted_iota(jnp.int32, (tq, nk), 1)
    if mode == "fox":
        s = s + cq - ck
        allowed = s_pos <= t_pos
    else:
        qc, kc = lax.shift_right_logical(t_pos, CHUNK_SHIFT), lax.shift_right_logical(s_pos, CHUNK_SHIFT)
        allowed = kc <= qc
        if mode == "chunk":
            allowed = jnp.logical_and(allowed, kc >= qc - LEFT_CHUNKS)
            tiles = []
            for w in range(nk // 128):
                delta = i - (start // 128 + w)
                tiles.append(jnp.where(delta == 0, t_ref[0], jnp.where(delta == 1, t_ref[1], t_ref[2])))
            s = s + jnp.concatenate(tiles, axis=1)
    return jnp.where(allowed, s, NEG)


def attn_fwd(q, k, v_arr, v_blk, *, mode, name, n_heads, cq=None, ck=None, tiles=None):
    _, sq, dk = q.shape
    sk = k.shape[1]
    tq, win, scale = _attn_cfg(mode, sq, sk)

    def body(*refs):
        q_ref, k_ref, v_ref = refs[:3]
        pos = 3
        cq_ref = ck_ref = t_ref = None
        if mode == "fox":
            cq_ref, ck_ref = refs[pos:pos + 2]
            pos += 2
        if mode == "chunk":
            t_ref = refs[pos]
            pos += 1
        o_ref, lse_ref = refs[pos], refs[pos + 1]
        i = pl.program_id(1)

        def run(rows, start):
            cq, ck = (cq_ref[...], ck_ref[:, rows]) if mode == "fox" else (None, None)
            s = _attn_scores(mode, i, tq, scale, q_ref[...], k_ref[rows, :], start, cq, ck, t_ref)
            m = jnp.max(s, axis=1, keepdims=True)
            e = jnp.exp(s - m)
            l = jnp.sum(e, axis=1, keepdims=True)
            p = e * (1.0 / l)
            o_ref[...] = jnp.dot(_mx(p), _mx(v_ref[rows, :]), preferred_element_type=F32).astype(o_ref.dtype)
            lse_ref[...] = m + jnp.log(l)

        _attn_key_rows(mode, i, tq, win, sk, run)

    in_specs = [pl.BlockSpec((None, tq, dk), lambda h, i: (h, i, 0)),
                pl.BlockSpec((None, sk, dk), lambda h, i: (h, 0, 0)),
                pl.BlockSpec((sk, DH), lambda h, i: (0, v_blk(h)))]
    args = [q, k, v_arr]
    if mode == "fox":
        in_specs += [pl.BlockSpec((None, tq, 1), lambda h, i: (h, i, 0)),
                     pl.BlockSpec((None, 1, sk), lambda h, i: (h, 0, 0))]
        args += [cq, ck]
    if mode == "chunk":
        in_specs.append(pl.BlockSpec((3, None, 128, 128), lambda h, i: (0, h, 0, 0)))
        args.append(tiles)
    return _pcall(body, name=name,
                  out_shape=(_sds((sq, n_heads * DH), BF16), _sds((n_heads, sq, 1), F32)),
                  grid=(n_heads, sq // tq), in_specs=in_specs,
                  out_specs=(pl.BlockSpec((tq, DH), lambda h, i: (i, h)),
                             pl.BlockSpec((None, tq, 1), lambda h, i: (h, i, 0))))(*args)


def attn_bwd(q, k, v_arr, v_blk, o, do, lse, *, mode, name, n_heads, cq=None, ck=None, tiles=None):
    _, sq, dk = q.shape
    sk = k.shape[1]
    tq, win, scale = _attn_cfg(mode, sq, sk)
    n_extra = {"fox": 2, "chunk": 1}.get(mode, 0)

    def body(*refs):
        q_ref, k_ref, v_ref, o_ref, do_ref, lse_ref = refs[:6]
        pos = 6
        cq_ref = ck_ref = t_ref = None
        if mode == "fox":
            cq_ref, ck_ref = refs[pos:pos + 2]
            pos += 2
        if mode == "chunk":
            t_ref = refs[pos]
            pos += 1
        dq_ref, dk_ref, dv_ref = refs[pos:pos + 3]
        extra = refs[pos + 3:pos + 3 + n_extra]
        i = pl.program_id(1)

        @pl.when(i == 0)
        def _():
            dk_ref[...] = jnp.zeros_like(dk_ref)
            dv_ref[...] = jnp.zeros_like(dv_ref)
            if mode == "fox":
                extra[1][...] = jnp.zeros_like(extra[1])
            if mode == "chunk":
                extra[0][...] = jnp.zeros_like(extra[0])

        def run(rows, start):
            q = q_ref[...]
            do = do_ref[...]
            kk = k_ref[rows, :]
            cq, ck = (cq_ref[...], ck_ref[:, rows]) if mode == "fox" else (None, None)
            s = _attn_scores(mode, i, tq, scale, q, kk, start, cq, ck, t_ref)
            p = jnp.exp(s - lse_ref[...])
            drow = jnp.sum(do.astype(F32) * o_ref[...].astype(F32), axis=1, keepdims=True)
            dp = lax.dot_general(do, _mx(v_ref[rows, :]), (((1,), (1,)), ((), ())), preferred_element_type=F32)
            ds = p * (dp - drow)
            dsb = _mx(ds)
            dq_ref[...] = (jnp.dot(dsb, kk, preferred_element_type=F32) * scale).astype(dq_ref.dtype)
            dk_ref[rows, :] += lax.dot_general(dsb, q, (((0,), (0,)), ((), ())),
                                               preferred_element_type=F32) * scale
            dv_ref[rows, :] += lax.dot_general(_mx(p), do, (((0,), (0,)), ((), ())), preferred_element_type=F32)
            if mode == "chunk":
                dt_ref = extra[0]
                for w in range(win // 128):
                    delta = i - (start // 128 + w)
                    tile = ds[:, w * 128:(w + 1) * 128]
                    zero = jnp.zeros_like(tile)
                    dt_ref[0] += jnp.where(delta == 0, tile, zero)
                    dt_ref[1] += jnp.where(delta == 1, tile, zero)
                    dt_ref[2] += jnp.where(delta >= 2, tile, zero)
            if mode == "fox":
                extra[0][...] = jnp.sum(ds, axis=1, keepdims=True)
                extra[1][:, rows] -= jnp.sum(ds, axis=0, keepdims=True)

        _attn_key_rows(mode, i, tq, win, sk, run)

    in_specs = [pl.BlockSpec((None, tq, dk), lambda h, i: (h, i, 0)),
                pl.BlockSpec((None, sk, dk), lambda h, i: (h, 0, 0)),
                pl.BlockSpec((sk, DH), lambda h, i: (0, v_blk(h))),
                pl.BlockSpec((tq, DH), lambda h, i: (i, h)),
                pl.BlockSpec((tq, DH), lambda h, i: (i, h)),
                pl.BlockSpec((None, tq, 1), lambda h, i: (h, i, 0))]
    args = [q, k, v_arr, o, do, lse]
    out_shape = [_sds((n_heads, sq, dk), F32), _sds((n_heads, sk, dk), F32), _sds((n_heads, sk, DH), F32)]
    out_specs = [pl.BlockSpec((None, tq, dk), lambda h, i: (h, i, 0)),
                 pl.BlockSpec((None, sk, dk), lambda h, i: (h, 0, 0)),
                 pl.BlockSpec((None, sk, DH), lambda h, i: (h, 0, 0))]
    if mode == "fox":
        in_specs += [pl.BlockSpec((None, tq, 1), lambda h, i: (h, i, 0)),
                     pl.BlockSpec((None, 1, sk), lambda h, i: (h, 0, 0))]
        args += [cq, ck]
        out_shape += [_sds((n_heads, sq, 1), F32), _sds((n_heads, 1, sk), F32)]
        out_specs += [pl.BlockSpec((None, tq, 1), lambda h, i: (h, i, 0)),
                      pl.BlockSpec((None, 1, sk), lambda h, i: (h, 0, 0))]
    if mode == "chunk":
        in_specs.append(pl.BlockSpec((3, None, 128, 128), lambda h, i: (0, h, 0, 0)))
        args.append(tiles)
        out_shape.append(_sds((3, n_heads, 128, 128), F32))
        out_specs.append(pl.BlockSpec((3, None, 128, 128), lambda h, i: (0, h, 0, 0)))
    return _pcall(body, name=name, out_shape=tuple(out_shape), grid=(n_heads, sq // tq),
                  in_specs=in_specs, out_specs=tuple(out_specs))(*args)


REL_LANES = 384
REL_KBLK = 2048


def _rel_onehot(t, k):
    rho = k * REL_KBLK + lax.broadcasted_iota(jnp.int32, (REL_KBLK, REL_LANES), 0)
    lane = lax.broadcasted_iota(jnp.int32, (REL_KBLK, REL_LANES), 1)
    diff = lax.shift_right_logical(rho, 7) - jnp.bitwise_and(rho, 127)
    idx = jnp.where(t == 0, diff + REL_CLIP,
                    jnp.where(t == 1, jnp.minimum(diff + 128, REL_CLIP) + REL_CLIP, N_REL - 1))
    return jnp.where(idx == lane, 1.0, 0.0).astype(F32)


def relbias_tiles(rel_bias):
    nh = rel_bias.shape[0]
    rb = jnp.pad(rel_bias, ((0, 0), (0, REL_LANES - N_REL)))

    def body(rb_ref, o_ref):
        e = _rel_onehot(pl.program_id(0), pl.program_id(1))
        o_ref[...] = lax.dot_general(rb_ref[...], e, (((1,), (1,)), ((), ())), preferred_element_type=F32,
                                     precision=lax.Precision.HIGHEST)

    flat = _pcall(body, name="relbias_tiles", out_shape=_sds((3, nh, 128 * 128), F32),
                  grid=(3, 128 * 128 // REL_KBLK),
                  in_specs=[pl.BlockSpec((nh, REL_LANES), lambda t, k: (0, 0))],
                  out_specs=pl.BlockSpec((None, nh, REL_KBLK), lambda t, k: (t, 0, k)))(rb)
    return flat.reshape(3, nh, 128, 128)


def relbias_tiles_bwd(dtiles):
    nh = dtiles.shape[1]

    def body(dt_ref, o_ref):
        t, k = pl.program_id(0), pl.program_id(1)
        part = jnp.dot(dt_ref[...], _rel_onehot(t, k), preferred_element_type=F32,
                       precision=lax.Precision.HIGHEST)
        first = jnp.logical_and(t == 0, k == 0)

        @pl.when(first)
        def _():
            o_ref[...] = part

        @pl.when(jnp.logical_not(first))
        def _():
            o_ref[...] += part

    out = _pcall(body, name="relbias_tiles_bwd", out_shape=_sds((nh, REL_LANES), F32),
                 grid=(3, 128 * 128 // REL_KBLK),
                 in_specs=[pl.BlockSpec((None, nh, REL_KBLK), lambda t, k: (t, 0, k))],
                 out_specs=pl.BlockSpec((nh, REL_LANES), lambda t, k: (0, 0)))(dtiles.reshape(3, nh, 128 * 128))
    return out[:, :N_REL]


CUM_BLK = 256


def _tri(n, lower):
    r = lax.broadcasted_iota(jnp.int32, (n, n), 0)
    c = lax.broadcasted_iota(jnp.int32, (n, n), 1)
    return jnp.where(r >= c if lower else r <= c, 1.0, 0.0).astype(F32)


def fox_cum_fwd(zs, bf):
    S = zs.shape[0]
    tb = min(CUM_BLK, S)

    def body(f_ref, b_ref, cum_ref, cumt_ref, carry_ref):
        @pl.when(pl.program_id(0) == 0)
        def _():
            carry_ref[...] = jnp.zeros_like(carry_ref)

        x = f_ref[...] + b_ref[...]
        lane = lax.broadcasted_iota(jnp.int32, x.shape, 1)
        logf = jnp.where(lane < N_HEADS, jnp.minimum(x, 0.0) - jnp.log(1.0 + jnp.exp(-jnp.abs(x))), 0.0)
        cum = jnp.dot(_tri(tb, True), logf, preferred_element_type=F32,
                      precision=lax.Precision.HIGHEST) + carry_ref[...]
        carry_ref[...] = cum[tb - 1:tb, :]
        cum_ref[...] = cum
        cumt_ref[...] = cum.T

    return _pcall(body, name="fox_cum_fwd", out_shape=(_sds((S, 128), F32), _sds((128, S), F32)),
                  grid=(S // tb,),
                  in_specs=[pl.BlockSpec((tb, 128), lambda i: (i, Z_FF // 128)),
                            pl.BlockSpec((1, 128), lambda i: (0, 0))],
                  out_specs=(pl.BlockSpec((tb, 128), lambda i: (i, 0)),
                             pl.BlockSpec((128, tb), lambda i: (0, i))),
                  scratch=(pltpu.VMEM((1, 128), F32),))(zs, bf)


def fox_cum_bwd(zs, bf, dcum):
    S = zs.shape[0]
    tb = min(CUM_BLK, S)
    nblk = S // tb

    def body(f_ref, b_ref, d_ref, df_ref, db_ref, carry_ref):
        @pl.when(pl.program_id(0) == 0)
        def _():
            carry_ref[...] = jnp.zeros_like(carry_ref)
            db_ref[...] = jnp.zeros_like(db_ref)

        d = d_ref[...]
        dlogf = jnp.dot(_tri(tb, False), d, preferred_element_type=F32,
                        precision=lax.Precision.HIGHEST) + carry_ref[...]
        carry_ref[...] += jnp.sum(d, axis=0, keepdims=True)
        x = f_ref[...] + b_ref[...]
        lane = lax.broadcasted_iota(jnp.int32, x.shape, 1)
        dx = jnp.where(lane < N_HEADS, dlogf / (1.0 + jnp.exp(x)), 0.0)
        df_ref[...] = dx
        db_ref[...] += jnp.sum(dx, axis=0, keepdims=True)

    return _pcall(body, name="fox_cum_bwd", out_shape=(_sds((S, 128), F32), _sds((1, 128), F32)),
                  grid=(nblk,),
                  in_specs=[pl.BlockSpec((tb, 128), lambda i: (nblk - 1 - i, Z_FF // 128)),
                            pl.BlockSpec((1, 128), lambda i: (0, 0)),
                            pl.BlockSpec((tb, 128), lambda i: (nblk - 1 - i, 0))],
                  out_specs=(pl.BlockSpec((tb, 128), lambda i: (nblk - 1 - i, 0)),
                             pl.BlockSpec((1, 128), lambda i: (0, 0))),
                  scratch=(pltpu.VMEM((1, 128), F32),))(zs, bf, dcum)


def _sigmoid(x):
    return 1.0 / (1.0 + jnp.exp(-x))


def merge_fwd(z, projs):
    S = z.shape[0]
    tr, tc = _pick(S, (512, 256, 128)), 512
    nbc = D_MODEL // tc
    g0 = Z_GATE // tc

    def body(g0_ref, g1_ref, g2_ref, p0_ref, p1_ref, p2_ref, o_ref):
        acc = _sigmoid(g0_ref[...]) * p0_ref[...]
        acc += _sigmoid(g1_ref[...]) * p1_ref[...]
        acc += _sigmoid(g2_ref[...]) * p2_ref[...]
        o_ref[...] = acc.astype(o_ref.dtype)

    gspecs = [pl.BlockSpec((tr, tc), functools.partial(lambda i, j, n: (i, g0 + n * nbc + j), n=n))
              for n in range(3)]
    pspec = pl.BlockSpec((tr, tc), lambda i, j: (i, j))
    return _pcall(body, name="merge_fwd", out_shape=_sds((S, D_MODEL), BF16), grid=(S // tr, nbc),
                  in_specs=gspecs + [pspec] * 3, out_specs=pspec)(z, z, z, *projs)


def merge_bwd(z, projs, dmerged):
    S = z.shape[0]
    tr, tc = _pick(S, (512, 256, 128)), 512
    nbc = D_MODEL // tc
    g0 = Z_GATE // tc

    def body(g0_ref, g1_ref, g2_ref, p0_ref, p1_ref, p2_ref, dm_ref, dg0, dg1, dg2, dp0, dp1, dp2):
        dm = dm_ref[...]
        for g_ref, p_ref, dg_ref, dp_ref in ((g0_ref, p0_ref, dg0, dp0), (g1_ref, p1_ref, dg1, dp1),
                                             (g2_ref, p2_ref, dg2, dp2)):
            sg = _sigmoid(g_ref[...])
            dp_ref[...] = (dm * sg).astype(dp_ref.dtype)
            dg_ref[...] = (dm * p_ref[...] * sg * (1.0 - sg)).astype(dg_ref.dtype)

    gspecs = [pl.BlockSpec((tr, tc), functools.partial(lambda i, j, n: (i, g0 + n * nbc + j), n=n))
              for n in range(3)]
    pspec = pl.BlockSpec((tr, tc), lambda i, j: (i, j))
    out = _pcall(body, name="merge_bwd", out_shape=tuple(_sds((S, D_MODEL), BF16) for _ in range(6)),
                 grid=(S // tr, nbc), in_specs=gspecs + [pspec] * 4,
                 out_specs=tuple([pspec] * 6))(z, z, z, *projs, dmerged)
    return out[:3], out[3:]


def loss_head(y, target):
    S, D = y.shape
    tr = _pick(S, (256, 128))

    def body(y_ref, t_ref, dy_ref, l_ref):
        e = y_ref[...] - t_ref[...]
        dy_ref[...] = e * (1.0 / D)
        part = jnp.sum(jnp.sum(e * e, axis=1, keepdims=True), axis=0, keepdims=True) * (0.5 / D)

        @pl.when(pl.program_id(0) == 0)
        def _():
            l_ref[...] = part

        @pl.when(pl.program_id(0) > 0)
        def _():
            l_ref[...] += part

    spec = pl.BlockSpec((tr, D), lambda i: (i, 0))
    return _pcall(body, name="loss_head", out_shape=(_sds((S, D), F32), _sds((1, 1), F32)), grid=(S // tr,),
                  in_specs=[spec, spec], out_specs=(spec, pl.BlockSpec((1, 1), lambda i: (0, 0))))(y, target)


ADAMW_BLOCK_BYTES = 1024 * 1024


def adamw(w, gs, m, v, *, name):
    L, R, C = w.shape
    tr = _pick(R, (512, 256, 128, 64, 32, 16, 8))
    while tr * C * 4 > ADAMW_BLOCK_BYTES and tr % 16 == 0:
        tr //= 2
    c1 = 1.0 - ADAM_B1 ** ADAM_STEP
    c2 = 1.0 - ADAM_B2 ** ADAM_STEP

    def body(*refs):
        w_ref, m_ref, v_ref = refs[:3]
        g_refs = refs[3:3 + L]
        go_ref, d_ref, nm_ref, nv_ref = refs[3 + L:]
        l = pl.program_id(0)
        g_ = g_refs[0][...]
        for k in range(1, L):
            g_ = jnp.where(l == k, g_refs[k][...], g_)
        nm = ADAM_B1 * m_ref[...] + (1.0 - ADAM_B1) * g_
        nv = ADAM_B2 * v_ref[...] + (1.0 - ADAM_B2) * (g_ * g_)
        go_ref[...] = g_
        nm_ref[...] = nm
        nv_ref[...] = nv
        d_ref[...] = -ADAM_LR * ((nm / c1) / (jnp.sqrt(nv / c2) + ADAM_EPS) + ADAM_WD * w_ref[...])

    spec = pl.BlockSpec((None, tr, C), lambda l, i: (l, i, 0))
    gspecs = [pl.BlockSpec((tr, C), functools.partial(lambda l, i, k: (jnp.where(l == k, i, 0), 0), k=k))
              for k in range(L)]
    return _pcall(body, name=name, out_shape=tuple(_sds((L, R, C), F32) for _ in range(4)),
                  grid=(L, R // tr), in_specs=[spec] * 3 + gspecs, out_specs=(spec,) * 4)(w, m, v, *gs)


SHARD_SHAPES = {
    "w_uq": (Q_LORA, 384), "w_ukv": (KV_LORA, 512), "w_br": (3, MIX_W, 512), "w_out": (512, D_MODEL),
    "w_xq": (512, 512), "w_xkv": (512, 1024), "w_xo": (512, 512), "w_1": (D_MODEL, 2048),
    "w_2": (2048, D_MODEL), "w_in": (D_MODEL, 3282),
}
def _rows2d(a, name):
    shp = SHARD_SHAPES[name]
    return a.reshape(a.shape[:a.ndim - len(shp)] + (-1, shp[-1]))


def _cols_from_shards(g):
    return jnp.transpose(g, (1, 0, 2)).reshape(g.shape[1], 4 * g.shape[2])


def _cols_to_shards(w):
    return jnp.transpose(w.reshape(w.shape[0], 4, w.shape[1] // 4), (1, 0, 2))


def full_weights(g):
    zeros = lambda n: [jnp.zeros((D_MODEL, n), g["w_in"].dtype)] if n else []
    segs, at = [], 0
    for p0, o0, w in sorted(W_IN_PIECES):
        segs += zeros(p0 - at)
        while w > 0:
            s_, a = divmod(o0, W_IN_SHARD)
            take = min(w, W_IN_SHARD - a)
            segs.append(g["w_in"][s_][:, a:a + take])
            o0, p0, w = o0 + take, p0 + take, w - take
        at = p0
    w_in_p = jnp.concatenate(segs + zeros(Z_TOT - at), axis=1)
    w_uq =_cols_from_shards(g["w_uq"]).reshape(Q_LORA, N_HEADS, MLA_QK)
    w_uq_p = jnp.pad(w_uq, ((0, 0), (0, 0), (0, MLA_PAD - MLA_QK))).reshape(1, Q_LORA, N_HEADS * MLA_PAD)
    return dict(
        in_p=w_in_p[None], uq_p=w_uq_p, ukv=g["w_ukv"], br=[g["w_br"][:, n] for n in range(3)],
        out=g["w_out"].reshape(1, D_MODEL, D_MODEL), xq=g["w_xq"].reshape(1, D_MODEL, 512),
        xkv=g["w_xkv"].reshape(1, D_MODEL, 1024), xo=g["w_xo"], w1=g["w_1"],
        w2=g["w_2"].reshape(1, D_FF, D_MODEL))


def shard_grads(dw):
    dp = dw["in_p"][0]
    d_in = []
    for s_ in range(4):
        lo, hi, segs = s_ * W_IN_SHARD, (s_ + 1) * W_IN_SHARD, []
        for o0, p0, w in sorted((o0, p0, w) for p0, o0, w in W_IN_PIECES):
            a, b = max(lo, o0), min(hi, o0 + w)
            if a < b:
                segs.append(dp[:, p0 + a - o0:p0 + b - o0])
        d_in.append(jnp.concatenate(segs, axis=1))
    d_uq =dw["uq_p"].reshape(Q_LORA, N_HEADS, MLA_PAD)[:, :, :MLA_QK].reshape(Q_LORA, N_HEADS * MLA_QK)
    return {
        "w_in": jnp.stack(d_in), "w_uq": _cols_to_shards(d_uq), "w_ukv": dw["ukv"],
        "w_br": jnp.stack(dw["br"], axis=1), "w_out": dw["out"].reshape(4, 512, D_MODEL),
        "w_xq": dw["xq"].reshape(4, 512, 512), "w_xkv": dw["xkv"].reshape(4, 512, 1024), "w_xo": dw["xo"],
        "w_1": dw["w1"], "w_2": dw["w2"].reshape(4, 2048, D_MODEL)}


def layer_params(d, l):
    row = lambda v: v.reshape(1, -1).astype(F32)
    padto = lambda v, n: jnp.pad(row(v), ((0, 0), (0, n - v.shape[-1])))
    return dict(
        g_mix=row(d["g_mix"][l]), g_cq=row(d["g_cq"][l]), g_ckv=row(d["g_ckv"][l]),
        g_mla_q=padto(d["g_mla_q"][l], MLA_PAD), g_mla_k=padto(d["g_mla_k"][l], MLA_PAD),
        b_f=padto(d["b_f"][l], 128), g_fox_q=row(d["g_fox_q"][l]), g_fox_k=row(d["g_fox_k"][l]),
        rel_bias=d["rel_bias"][l].astype(F32), g_ch_q=row(d["g_ch_q"][l]), g_ch_k=row(d["g_ch_k"][l]),
        g_cross=row(d["g_cross"][l]), g_mem=row(d["g_mem"][l]), g_x_q=row(d["g_x_q"][l]),
        g_x_k=row(d["g_x_k"][l]), g_mlp=row(d["g_mlp"][l]))


FOX_B = Z_FOX // 128
CH_B = Z_CH // 128


def layer_fwd(x, mem, W, P, rope):
    S = x.shape[0]
    s = {}
    s["h"] = rms_fwd(x, P["g_mix"], name="rms_d")
    z = s["z"] = mm_nn(s["h"], W["in_p"], name="mm_in", out_dtype=F32)
    s["cq_n"] = rms_fwd(z, P["g_cq"], col0=Z_CQ, width=Q_LORA, name="rms_cq")
    s["ckv_n"] = rms_fwd(z, P["g_ckv"], col0=Z_CKV, width=KV_LORA, name="rms_ckv")
    s["q_raw"] = mm_nn(s["cq_n"], W["uq_p"], name="mm_uq", out_dtype=F32)
    s["kv_raw"] = mm_nn(s["ckv_n"], W["ukv"], name="mm_ukv", out_dtype=F32)
    s["qa"] = prep_fwd([(s["q_raw"], MLA_PAD, lambda h: h)], P["g_mla_q"], name="prep_mla_q",
                       n_heads=N_HEADS, n_real=MLA_QK, rope=rope)
    s["ka"] = prep_fwd([(s["kv_raw"], 128, lambda h: 2 * h), (z, 128, lambda h: Z_KR // 128)], P["g_mla_k"],
                       name="prep_mla_k", n_heads=N_HEADS, n_real=MLA_QK, rope=rope)
    s["o_a"], s["lse_a"] = attn_fwd(s["qa"], s["ka"], s["kv_raw"], lambda h: 2 * h + 1, mode="mla",
                                    name="attn_mla", n_heads=N_HEADS)
    s["qb"] = prep_fwd([(z, DH, lambda h: FOX_B + h)], P["g_fox_q"], name="prep_h", n_heads=N_HEADS, n_real=DH)
    s["kb"] = prep_fwd([(z, DH, lambda h: FOX_B + N_HEADS + h)], P["g_fox_k"], name="prep_h",
                       n_heads=N_HEADS, n_real=DH)
    _, cum_t = fox_cum_fwd(z, P["b_f"])
    s["cq"] = cum_t[:N_HEADS].reshape(N_HEADS, S, 1)
    s["ck"] = cum_t[:N_HEADS].reshape(N_HEADS, 1, S)
    s["o_b"], s["lse_b"] = attn_fwd(s["qb"], s["kb"], z, lambda h: FOX_B + 2 * N_HEADS + h, mode="fox",
                                    name="attn_fox", n_heads=N_HEADS, cq=s["cq"], ck=s["ck"])
    s["qc"] = prep_fwd([(z, DH, lambda h: CH_B + h)], P["g_ch_q"], name="prep_h", n_heads=N_HEADS, n_real=DH)
    s["kc"] = prep_fwd([(z, DH, lambda h: CH_B + N_HEADS + h)], P["g_ch_k"], name="prep_h",
                       n_heads=N_HEADS, n_real=DH)
    s["tiles"] = relbias_tiles(P["rel_bias"])
    s["o_c"], s["lse_c"] = attn_fwd(s["qc"], s["kc"], z, lambda h: CH_B + 2 * N_HEADS + h, mode="chunk",
                                    name="attn_chunk", n_heads=N_HEADS, tiles=s["tiles"])
    s["projs"] = [mm_nn(o, W["br"][n], name="mm_br", out_dtype=F32)
                  for n, o in enumerate((s["o_a"], s["o_b"], s["o_c"]))]
    s["merged"] = merge_fwd(z, s["projs"])
    x1 = s["x1"] = mm_nn(s["merged"], W["out"], name="mm_out", out_dtype=F32, res=x)
    s["hq"] = rms_fwd(x1, P["g_cross"], name="rms_d")
    s["xq_raw"] = mm_nn(s["hq"], W["xq"], name="mm_xq", out_dtype=F32)
    s["mem_n"] = rms_fwd(mem, P["g_mem"], name="rms_mem")
    s["mkv"] = mm_nn(s["mem_n"], W["xkv"], name="mm_xkv", out_dtype=F32)
    s["qx"] = prep_fwd([(s["xq_raw"], DH, lambda h: h)], P["g_x_q"], name="prep_xq", n_heads=X_HEADS, n_real=DH)
    s["kx"] = prep_fwd([(s["mkv"], DH, lambda h: h)], P["g_x_k"], name="prep_xk", n_heads=X_HEADS, n_real=DH)
    s["o_x"], s["lse_x"] = attn_fwd(s["qx"], s["kx"], s["mkv"], lambda h: X_HEADS + h, mode="cross",
                                    name="attn_cross", n_heads=X_HEADS)
    x2 = s["x2"] = mm_nn(s["o_x"], W["xo"], name="mm_xo", out_dtype=F32, res=x1)
    s["hm"] = rms_fwd(x2, P["g_mlp"], name="rms_d")
    s["a1"], s["act"] = mm_nn(s["hm"], W["w1"], name="mm_w1", out_dtype=BF16, relu2=True)
    x3 = mm_nn(s["act"], W["w2"], name="mm_w2", out_dtype=F32, res=x2)
    return x3, s


def layer_bwd(g, x, mem, W, P, rope, s):
    S = x.shape[0]
    z = s["z"]
    dw, ds = {}, {}
    da1 = mm_nt(g, W["w2"], name="mm_w2_dx", out_dtype=BF16, relu_mul=s["a1"])
    dw["w2"] = mm_tn(s["act"], g, nb=1, name="mm_w2_dw", out_dtype=BF16)
    dhm = mm_nt(da1, W["w1"], name="mm_w1_dx", out_dtype=F32)
    dw["w1"] = mm_tn(s["hm"], da1, nb=4, name="mm_w1_dw", out_dtype=BF16)
    g2, ds["g_mlp"] = rms_bwd(s["x2"], P["g_mlp"], dhm, name="rms_d_bwd", res=g)
    do_x = mm_nt(g2, W["xo"], name="mm_xo_dx", out_dtype=BF16)
    dw["xo"] = mm_tn(s["o_x"], g2, nb=4, name="mm_xo_dw", out_dtype=BF16)
    dqx, dkx, dvx = attn_bwd(s["qx"], s["kx"], s["mkv"], lambda h: X_HEADS + h, s["o_x"], do_x, s["lse_x"],
                             mode="cross", name="attn_cross_bwd", n_heads=X_HEADS)
    dxq_raw, ds["g_x_q"] = prep_bwd_q(s["xq_raw"], dqx, P["g_x_q"], name="prep_xq_bwd", n_heads=X_HEADS,
                                      dh=DH, n_real=DH)
    dmkv, dgk = prep_bwd_groups(s["mkv"], 0, [dkx, dvx], [P["g_x_k"], None], name="prep_xkv_bwd",
                                n_heads=X_HEADS, kinds=("norm", "copy"))
    ds["g_x_k"] = dgk[0]
    dhq = mm_nt(dxq_raw, W["xq"], name="mm_xq_dx", out_dtype=F32)
    dw["xq"] = mm_tn(s["hq"], dxq_raw, nb=1, name="mm_xq_dw", out_dtype=BF16)
    dmem_n = mm_nt(dmkv, W["xkv"], name="mm_xkv_dx", out_dtype=F32)
    dw["xkv"] = mm_tn(s["mem_n"], dmkv, nb=1, name="mm_xkv_dw", out_dtype=BF16)
    _, ds["g_mem"] = rms_bwd(mem, P["g_mem"], dmem_n, name="rms_mem_bwd", need_dx=False)
    g1, ds["g_cross"] = rms_bwd(s["x1"], P["g_cross"], dhq, name="rms_d_bwd", res=g2)
    dmerged = mm_nt(g1, W["out"], name="mm_out_dx", out_dtype=F32)
    dw["out"] = mm_tn(s["merged"], g1, nb=1, name="mm_out_dw", out_dtype=BF16)
    dgl, dproj = merge_bwd(z, s["projs"], dmerged)
    outs = (s["o_a"], s["o_b"], s["o_c"])
    do = [mm_nt(dproj[n], W["br"][n], name="mm_br_dx", out_dtype=BF16) for n in range(3)]
    dw["br"] = [mm_tn(outs[n], dproj[n], nb=4, name="mm_br_dw", out_dtype=BF16) for n in range(3)]
    dqc, dkc, dvc, dtiles = attn_bwd(s["qc"], s["kc"], z, lambda h: CH_B + 2 * N_HEADS + h, s["o_c"], do[2],
                                     s["lse_c"], mode="chunk", name="attn_chunk_bwd", n_heads=N_HEADS,
                                     tiles=s["tiles"])
    d_ch, dg_ch = prep_bwd_groups(z, CH_B, [dqc, dkc, dvc], [P["g_ch_q"], P["g_ch_k"], None],
                                  name="prep_h_bwd", n_heads=N_HEADS, kinds=("norm", "norm", "copy"))
    ds["g_ch_q"], ds["g_ch_k"] = dg_ch[0], dg_ch[1]
    ds["rel_bias"] = relbias_tiles_bwd(dtiles)
    dqb, dkb, dvb, dcq, dck = attn_bwd(s["qb"], s["kb"], z, lambda h: FOX_B + 2 * N_HEADS + h, s["o_b"], do[1],
                                       s["lse_b"], mode="fox", name="attn_fox_bwd", n_heads=N_HEADS,
                                       cq=s["cq"], ck=s["ck"])
    d_fox, dg_fox = prep_bwd_groups(z, FOX_B, [dqb, dkb, dvb], [P["g_fox_q"], P["g_fox_k"], None],
                                    name="prep_h_bwd", n_heads=N_HEADS, kinds=("norm", "norm", "copy"))
    ds["g_fox_q"], ds["g_fox_k"] = dg_fox[0], dg_fox[1]
    dcum = jnp.pad((dcq[:, :, 0] + dck[:, 0, :]).T, ((0, 0), (0, 128 - N_HEADS)))
    dff, dbf = fox_cum_bwd(z, P["b_f"], dcum)
    ds["b_f"] = dbf[:, :N_HEADS]
    dqa, dka, dva = attn_bwd(s["qa"], s["ka"], s["kv_raw"], lambda h: 2 * h + 1, s["o_a"], do[0], s["lse_a"],
                             mode="mla", name="attn_mla_bwd", n_heads=N_HEADS)
    dq_raw, dgq = prep_bwd_q(s["q_raw"], dqa, P["g_mla_q"], name="prep_mla_q_bwd", n_heads=N_HEADS,
                             dh=MLA_PAD, n_real=MLA_QK, rope=rope)
    dkv_raw, dkr, dgk = prep_bwd_mla_k(s["kv_raw"], z, dka, dva, P["g_mla_k"], rope, name="prep_mla_k_bwd")
    ds["g_mla_q"], ds["g_mla_k"] = dgq[:, :MLA_QK], dgk[:, :MLA_QK]
    dcq_n = mm_nt(dq_raw, W["uq_p"], name="mm_uq_dx", out_dtype=F32)
    dw["uq_p"] = mm_tn(s["cq_n"], dq_raw, nb=1, name="mm_uq_dw", out_dtype=BF16)
    dckv_n = mm_nt(dkv_raw, W["ukv"], name="mm_ukv_dx", out_dtype=F32)
    dw["ukv"] = mm_tn(s["ckv_n"], dkv_raw, nb=4, name="mm_ukv_dw", out_dtype=BF16)
    d_cq, ds["g_cq"] = rms_bwd(z, P["g_cq"], dcq_n, name="rms_cq_bwd", col0=Z_CQ, width=Q_LORA, dx_dtype=BF16)
    d_ckv, ds["g_ckv"] = rms_bwd(z, P["g_ckv"], dckv_n, name="rms_ckv_bwd", col0=Z_CKV, width=KV_LORA,
                                 dx_dtype=BF16)
    dz = jnp.concatenate([d_cq, d_ckv, dkr.astype(BF16), dff.astype(BF16), d_fox, d_ch, *dgl], axis=1)
    dh = mm_nt(dz, W["in_p"], name="mm_in_dx", out_dtype=F32)
    dw["in_p"] = mm_tn(s["h"], dz, nb=1, name="mm_in_dw", out_dtype=BF16)
    g0, ds["g_mix"] = rms_bwd(x, P["g_mix"], dh, name="rms_d_bwd", res=g1)
    return g0, dw, ds


def local_step(x, mem, target, Ps, weights_of, grads_done):
    rope = rope_tables(x.shape[0])
    Ws, saved, xs = [], [], [x]
    for l, P in enumerate(Ps):
        Ws.append(weights_of(l, xs[-1]))
        y, s = layer_fwd(xs[-1], mem, Ws[l], P, rope)
        xs.append(y)
        saved.append(s)
    g, loss = loss_head(xs[-1], target)
    dss, token = [], None
    for l in reversed(range(len(Ps))):
        P = Ps[l] if token is None else dict(Ps[l], g_mlp=Ps[l]["g_mlp"] + token[:1, :1])
        g, dw, ds = layer_bwd(g, xs[l], mem, Ws[l], P, rope, saved[l])
        token = grads_done(l, dw)
        dss.append(ds)
    return loss, g, dss[::-1]


HBM_SPEC = pl.BlockSpec(memory_space=pltpu.HBM)


def _place():
    return lax.axis_index("x"), lax.axis_index("y"), lax.axis_index("c")


def _other_chips(x, y):
    return [(1 - x, y), (x, 1 - y), (1 - x, 1 - y)]


def _remote(src, dst, send_sems, recv_sems, k, to):
    return pltpu.make_async_remote_copy(src_ref=src, dst_ref=dst, send_sem=send_sems.at[k],
                                        recv_sem=recv_sems.at[k], device_id=to, device_id_type=MESH)


def _comm_call(body, *, name, ins, out_shapes, n_sem):
    return pl.pallas_call(
        body, out_shape=tuple(out_shapes), in_specs=[HBM_SPEC] * len(ins),
        out_specs=tuple([HBM_SPEC] * len(out_shapes)),
        scratch_shapes=[pltpu.SemaphoreType.DMA((n_sem,)), pltpu.SemaphoreType.DMA((n_sem,))],
        name=name, interpret=False)(*ins)


SEM_SPEC = pl.BlockSpec(memory_space=pltpu.SEMAPHORE)
SPLIT_EFFECT = pltpu.SideEffectType.DATAFLOW_SIDE_EFFECTING


def _chip_exchange_copies(kind, srcs, lands, send_sems, recv_sems):
    x, y, c = _place()
    me = 2 * x + y
    cps = []
    for i, (src, land) in enumerate(zip(srcs, lands)):
        for j, (cx, cy) in enumerate(_other_chips(x, y)):
            if kind == "gather":
                h = src.shape[0] // 2
                s_ref, d_ref = src.at[pl.ds(c * h, h), :], land.at[me, pl.ds(c * h, h), :]
            else:
                s_ref, d_ref = src.at[2 * cx + cy], land.at[me]
            cps.append(_remote(s_ref, d_ref, send_sems, recv_sems, 3 * i + j, (cx, cy, c)))
    return cps


def chip_exchange_start(kind, srcs, land_shapes, *, name):
    n = len(srcs)

    def body(*refs):
        send_sems, recv_sems = refs[2 * n], refs[2 * n + 1]
        for cp in _chip_exchange_copies(kind, refs[:n], refs[n:2 * n], send_sems, recv_sems):
            cp.start()
        refs[-1][...] = jnp.zeros_like(refs[-1])

    hbm = lambda a: pltpu.with_memory_space_constraint(a, pltpu.HBM)
    ins = [hbm(s) for s in srcs] + [hbm(lax.empty(s.shape, s.dtype)) for s in land_shapes]
    sems = pltpu.SemaphoreType.DMA((3 * n,))
    out = pl.pallas_call(
        body, name=name, interpret=False,
        out_shape=(sems, sems, *[pltpu.HBM(a.shape, a.dtype) for a in ins], _sds((8, 128), F32)),
        in_specs=[HBM_SPEC] * (2 * n),
        out_specs=(SEM_SPEC, SEM_SPEC, *[HBM_SPEC] * (2 * n), pl.BlockSpec(memory_space=pltpu.VMEM)),
        input_output_aliases={i: 2 + i for i in range(2 * n)},
        compiler_params=pltpu.CompilerParams(has_side_effects=SPLIT_EFFECT))(*ins)
    return out[0], out[1], list(out[2:2 + n]), list(out[2 + n:2 + 2 * n]), out[-1]


def chip_exchange_wait(kind, started, after, *, name):
    send_sems, recv_sems, srcs, lands, _ = started
    n = len(srcs)

    def body(*refs):
        send_sems, recv_sems = refs[2 * n], refs[2 * n + 1]
        for cp in _chip_exchange_copies(kind, refs[:n], refs[n:2 * n], send_sems, recv_sems):
            cp.wait_send()
            cp.wait_recv()

    thru = srcs + lands
    out = pl.pallas_call(
        body, name=name, interpret=False,
        out_shape=tuple(pltpu.HBM(a.shape, a.dtype) for a in thru),
        in_specs=[HBM_SPEC] * (2 * n) + [SEM_SPEC, SEM_SPEC, pl.BlockSpec(memory_space=pl.ANY)],
        out_specs=tuple([HBM_SPEC] * (2 * n)),
        input_output_aliases={i: i for i in range(2 * n)},
        compiler_params=pltpu.CompilerParams(has_side_effects=SPLIT_EFFECT))(*thru, send_sems, recv_sems, after)
    return list(out[:n]), list(out[n:])


def gather_to_sibling(lands):
    n = len(lands)

    def body(*refs):
        outs = refs[n:2 * n]
        send_sems, recv_sems = refs[2 * n:]
        x, y, c = _place()
        sends = []
        for i in range(n):
            h = outs[i].shape[1] // 2
            for j, (cx, cy) in enumerate(_other_chips(x, y)):
                landed = outs[i].at[2 * cx + cy, pl.ds(c * h, h), :]
                cp = _remote(landed, landed, send_sems, recv_sems, 3 * i + j, (x, y, 1 - c))
                cp.start()
                sends.append(cp)
        for cp in sends:
            cp.wait()

    return pl.pallas_call(
        body, out_shape=tuple(_sds(a.shape, a.dtype) for a in lands), in_specs=[HBM_SPEC] * n,
        out_specs=tuple([HBM_SPEC] * n), input_output_aliases={i: i for i in range(n)},
        scratch_shapes=[pltpu.SemaphoreType.DMA((3 * n,)), pltpu.SemaphoreType.DMA((3 * n,))],
        name="gather_to_sibling", interpret=False)(*lands)


def rs_to_sibling(parts):
    n = len(parts)

    def body(*refs):
        srcs, outs = refs[:n], refs[n:2 * n]
        send_sems, recv_sems = refs[2 * n:]
        x, y, c = _place()
        sends = []
        for i in range(n):
            h = srcs[i].shape[1] // 2
            cp = _remote(srcs[i].at[:, pl.ds((1 - c) * h, h), :], outs[i], send_sems, recv_sems, i, (x, y, 1 - c))
            cp.start()
            sends.append(cp)
        for cp in sends:
            cp.wait()

    return _comm_call(body, name="rs_to_sibling", ins=parts, n_sem=n,
                      out_shapes=[_sds((4, p.shape[1] // 2, p.shape[2]), p.dtype) for p in parts])


def rs_share_halves(ts):
    n = len(ts)

    def body(*refs):
        srcs, outs = refs[:n], refs[n:2 * n]
        send_sems, recv_sems = refs[2 * n:]
        x, y, c = _place()
        sends = []
        for i in range(n):
            cp = _remote(srcs[i], outs[i].at[c], send_sems, recv_sems, i, (x, y, 1 - c))
            cp.start()
            sends.append(cp)
        for cp in sends:
            cp.wait()

    return _comm_call(body, name="rs_share_halves", ins=ts, n_sem=n,
                      out_shapes=[_sds((2,) + t.shape, t.dtype) for t in ts])


def add_pair(part, got, half_idx):
    _, a, b = part.shape
    h = a // 2
    tr = _pick(h, (512, 256, 128))

    def body(c_ref, p_ref, g_ref, o_ref):
        o_ref[...] = (p_ref[...].astype(F32) + g_ref[...].astype(F32)).astype(o_ref.dtype)

    spec = pl.BlockSpec((None, tr, b), lambda s_, i, c_ref: (s_, i, 0))
    grid_spec = pltpu.PrefetchScalarGridSpec(
        num_scalar_prefetch=1, grid=(4, h // tr),
        in_specs=[pl.BlockSpec((None, None, tr, b), lambda s_, i, c_ref: (s_, c_ref[0], i, 0)), spec],
        out_specs=spec)
    return pl.pallas_call(
        body, out_shape=_sds((4, h, b), part.dtype), grid_spec=grid_spec, name="rs_add_pair", interpret=False,
        compiler_params=pltpu.CompilerParams(vmem_limit_bytes=VMEM_LIMIT_BYTES))(
            half_idx, part.reshape(4, 2, h, b), got)


def sum_slots(r):
    _, h, b = r.shape
    tr = _pick(h, (512, 256, 128))

    def body(r0, r1, r2, r3, o_ref):
        o_ref[...] = ((r0[...].astype(F32) + r1[...].astype(F32)) + r2[...].astype(F32)) + r3[...].astype(F32)

    specs = [pl.BlockSpec((None, tr, b), functools.partial(lambda i, s_: (s_, i, 0), s_=s_)) for s_ in range(4)]
    return _pcall(body, name="rs_sum_slots", out_shape=_sds((h, b), F32), grid=(h // tr,),
                  in_specs=specs, out_specs=pl.BlockSpec((tr, b), lambda i: (i, 0)))(r, r, r, r)


def reduce_scatter_start(parts, core, *, name):
    half_idx = core.reshape(1).astype(jnp.int32)
    gots = rs_to_sibling(parts)
    chip_sums = [add_pair(p, g, half_idx) for p, g in zip(parts, gots)]
    return chip_exchange_start("scatter", chip_sums, chip_sums, name=name)


def reduce_scatter_finish(started, after, chip, core, *, name):
    chip_sums, slots = chip_exchange_wait("scatter", started, after, name=name)
    slots = [lax.dynamic_update_slice(s_, lax.dynamic_index_in_dim(q, chip, 0, keepdims=True), (chip, 0, 0))
             for s_, q in zip(slots, chip_sums)]
    halves = [sum_slots(s_) for s_ in slots]
    both = rs_share_halves(halves)
    both = [lax.dynamic_update_slice(o, t[None], (core, 0, 0)) for o, t in zip(both, halves)]
    return [o.reshape(2 * o.shape[1], o.shape[2]) for o in both]


def allreduce_small(v):
    Rs = v.shape[0]

    def body(v_ref, o_ref, buf, send_sems, recv_sems):
        x, y, c = _place()
        me = 4 * x + 2 * y + c
        buf[me] = v_ref[...]
        flips = [(fx, fy, fc) for fx in (0, 1) for fy in (0, 1) for fc in (0, 1)][1:]
        sends = []
        for k, (fx, fy, fc) in enumerate(flips):
            to = ((1 - x) if fx else x, (1 - y) if fy else y, (1 - c) if fc else c)
            cp = _remote(v_ref, buf.at[me], send_sems, recv_sems, k, to)
            cp.start()
            sends.append(cp)
        for cp in sends:
            cp.wait()
        acc = buf[0]
        for d in range(1, 8):
            acc = acc + buf[d]
        o_ref[...] = acc

    vm = pl.BlockSpec(memory_space=pltpu.VMEM)
    return pl.pallas_call(
        body, out_shape=_sds((Rs, 128), F32), in_specs=[vm], out_specs=vm,
        scratch_shapes=[pltpu.VMEM((8, Rs, 128), F32), pltpu.SemaphoreType.DMA((7,)),
                        pltpu.SemaphoreType.DMA((7,))],
        name="allreduce_small", interpret=False)(v)


INPUT_NAMES = (("x", "mem") + WEIGHT_ORDER + ("loss_target",) + tuple("m_" + n for n in WEIGHT_ORDER)
               + tuple("v_" + n for n in WEIGHT_ORDER))


def _pack_small(vals, n_layers, extra=None):
    flat = jnp.concatenate([vals[n].reshape(n_layers, -1).astype(F32) for n in SMALL_ORDER], axis=1).reshape(-1)
    if extra is not None:
        flat = jnp.concatenate([flat, extra.reshape(-1)])
    n = flat.shape[0]
    rows = -(-n // 1024) * 8
    return jnp.pad(flat, (0, rows * 128 - n)).reshape(rows, 128)


def _unpack_small(packed, like, n_layers):
    per_layer = sum(int(np.prod(like[n].shape[1:])) for n in SMALL_ORDER)
    body = packed.reshape(-1)[:n_layers * per_layer].reshape(n_layers, per_layer)
    out, off = {}, 0
    for n in SMALL_ORDER:
        k = int(np.prod(like[n].shape[1:]))
        out[n] = body[:, off:off + k].reshape(like[n].shape)
        off += k
    return out, packed.reshape(-1)[n_layers * per_layer]


def kernel(x, mem, g_mix, w_in, g_cq, w_uq, g_ckv, w_ukv, g_mla_q, g_mla_k, b_f, g_fox_q, g_fox_k, rel_bias, g_ch_q, g_ch_k, w_br, w_out, g_cross, g_mem, w_xq, w_xkv, g_x_q, g_x_k, w_xo, g_mlp, w_1, w_2, loss_target, m_g_mix, m_w_in, m_g_cq, m_w_uq, m_g_ckv, m_w_ukv, m_g_mla_q, m_g_mla_k, m_b_f, m_g_fox_q, m_g_fox_k, m_rel_bias, m_g_ch_q, m_g_ch_k, m_w_br, m_w_out, m_g_cross, m_g_mem, m_w_xq, m_w_xkv, m_g_x_q, m_g_x_k, m_w_xo, m_g_mlp, m_w_1, m_w_2, v_g_mix, v_w_in, v_g_cq, v_w_uq, v_g_ckv, v_w_ukv, v_g_mla_q, v_g_mla_k, v_b_f, v_g_fox_q, v_g_fox_k, v_rel_bias, v_g_ch_q, v_g_ch_k, v_w_br, v_w_out, v_g_cross, v_g_mem, v_w_xq, v_w_xkv, v_g_x_q, v_g_x_k, v_w_xo, v_g_mlp, v_w_1, v_w_2):
    d = dict(zip(INPUT_NAMES, (x, mem, g_mix, w_in, g_cq, w_uq, g_ckv, w_ukv, g_mla_q, g_mla_k, b_f, g_fox_q, g_fox_k, rel_bias, g_ch_q, g_ch_k, w_br, w_out, g_cross, g_mem, w_xq, w_xkv, g_x_q, g_x_k, w_xo, g_mlp, w_1, w_2, loss_target, m_g_mix, m_w_in, m_g_cq, m_w_uq, m_g_ckv, m_w_ukv, m_g_mla_q, m_g_mla_k, m_b_f, m_g_fox_q, m_g_fox_k, m_rel_bias, m_g_ch_q, m_g_ch_k, m_w_br, m_w_out, m_g_cross, m_g_mem, m_w_xq, m_w_xkv, m_g_x_q, m_g_x_k, m_w_xo, m_g_mlp, m_w_1, m_w_2, v_g_mix, v_w_in, v_g_cq, v_w_uq, v_g_ckv, v_w_ukv, v_g_mla_q, v_g_mla_k, v_b_f, v_g_fox_q, v_g_fox_k, v_rel_bias, v_g_ch_q, v_g_ch_k, v_w_br, v_w_out, v_g_cross, v_g_mem, v_w_xq, v_w_xkv, v_g_x_q, v_g_x_k, v_w_xo, v_g_mlp, v_w_1, v_w_2)))
    n_layers = g_mix.shape[0]
    assert x.shape[0] == 1 and x.shape[2] == D_MODEL and mem.shape[1:] == (MEM_LEN, D_MODEL)
    for n in PACK_ORDER:
        assert d[n].shape[1:] == SHARD_SHAPES[n], (n, d[n].shape)

    chip = 2 * lax.axis_index("x") + lax.axis_index("y")
    core = lax.axis_index("c")

    gathers = []
    for l in range(n_layers):
        mine = [_rows2d(d[n][l].astype(BF16), n) for n in PACK_ORDER]
        gathers.append(chip_exchange_start("gather", mine, [_sds((4,) + m_.shape, BF16) for m_ in mine],
                                           name=f"gather_start_{l}"))

    def weights_of(l, x_l):
        mine, lands = chip_exchange_wait("gather", gathers[l], x_l, name=f"gather_wait_{l}")
        full = [lax.dynamic_update_slice(t, m_[None], (chip, 0, 0)) for t, m_ in zip(gather_to_sibling(lands), mine)]
        return full_weights({n: f.reshape((4,) + SHARD_SHAPES[n]) for n, f in zip(PACK_ORDER, full)})

    scatters = {}

    def grads_done(l, dw):
        sg = shard_grads(dw)
        scatters[l] = reduce_scatter_start([_rows2d(sg[n], n) for n in PACK_ORDER], core, name=f"scatter_start_{l}")
        return scatters[l][4]

    Ps = [layer_params(d, l) for l in range(n_layers)]
    loss, dx, dss = local_step(x[0], mem[0], loss_target[0], Ps, weights_of, grads_done)
    big = [None] * n_layers
    for l in reversed(range(n_layers)):
        big[l] = reduce_scatter_finish(scatters[l], dx, chip, core, name=f"scatter_wait_{l}")

    small_local = {n: jnp.stack([dss[l][n].reshape(d[n].shape[1:]) for l in range(n_layers)]) for n in SMALL_ORDER}
    small_sum, loss_sum = _unpack_small(allreduce_small(_pack_small(small_local, n_layers, extra=loss)),
                                        {n: d[n] for n in SMALL_ORDER}, n_layers)

    grads, delta, new_m, new_v = {}, {}, {}, {}
    for i, n in enumerate(PACK_ORDER):
        outs = adamw(_rows2d(d[n], n), [big[l][i] for l in range(n_layers)], _rows2d(d["m_" + n], n),
                     _rows2d(d["v_" + n], n), name="adamw_" + n)
        grads[n], delta[n], new_m[n], new_v[n] = (o.reshape(d[n].shape) for o in outs)
    like = {n: d[n] for n in SMALL_ORDER}
    sm = adamw(_pack_small(like, n_layers)[None], [_pack_small(small_sum, n_layers)],
               _pack_small({n: d["m_" + n] for n in SMALL_ORDER}, n_layers)[None],
               _pack_small({n: d["v_" + n] for n in SMALL_ORDER}, n_layers)[None], name="adamw_small")
    for res, src in zip((grads, delta, new_m, new_v), sm):
        res.update(_unpack_small(src[0], like, n_layers)[0])

    return (loss_sum, dx[None], *[grads[n] for n in WEIGHT_ORDER], *[delta[n] for n in WEIGHT_ORDER],
            *[new_m[n] for n in WEIGHT_ORDER], *[new_v[n] for n in WEIGHT_ORDER])
```

```python
import functools

import numpy as np
import jax
import jax.numpy as jnp
from jax import lax
from jax.experimental import pallas as pl
from jax.experimental.pallas import tpu as pltpu

F32 = jnp.float32
BF16 = jnp.bfloat16
MXU_DTYPE = jnp.bfloat16
MESH = pl.DeviceIdType.MESH

D_MODEL = 2048
MIX_W = 1024
N_HEADS = 8
DH = 128
MLA_NOPE = 128
MLA_ROPE = 64
MLA_QK = MLA_NOPE + MLA_ROPE
MLA_PAD = 256
Q_LORA = 512
KV_LORA = 256
CHUNK = 64
CHUNK_SHIFT = CHUNK.bit_length() - 1
LEFT_CHUNKS = 8
REL_CLIP = 128
N_REL = 2 * REL_CLIP + 1
X_HEADS = 4
MEM_LEN = 256
D_FF = 8192
ROPE_THETA = 10000.0
EPS = 1e-6
NEG = -1e30

Z_CQ, Z_CKV, Z_KR, Z_FF = 0, 512, 768, 896
ZS_W = 1024
Z_FOX = 1024
Z_CH = Z_FOX + 3 * MIX_W
Z_GATE = Z_CH + 3 * MIX_W
Z_TOT = Z_GATE + 3 * D_MODEL
W_IN_CUTS = (0, 512, 768, 832, 3904, 3912, 6984, 13128)
W_IN_SHARD = W_IN_CUTS[-1] // 4
W_IN_PIECES = ((0, 0, 832), (Z_FF, 3904, 8), (Z_FOX, 832, 3072), (Z_CH, 3912, 9216))

ADAM_LR, ADAM_B1, ADAM_B2, ADAM_EPS, ADAM_WD, ADAM_STEP = 0.001, 0.9, 0.999, 1e-08, 0.01, 10

VMEM_LIMIT_BYTES = 56 * 1024 * 1024
PACK_ORDER = ("w_uq", "w_ukv", "w_br", "w_out", "w_xq", "w_xkv", "w_xo", "w_1", "w_2", "w_in")
SMALL_ORDER = ("g_mix", "g_cq", "g_ckv", "g_mla_q", "g_mla_k", "b_f", "g_fox_q", "g_fox_k", "rel_bias",
               "g_ch_q", "g_ch_k", "g_cross", "g_mem", "g_x_q", "g_x_k", "g_mlp")
WEIGHT_ORDER = ("g_mix", "w_in", "g_cq", "w_uq", "g_ckv", "w_ukv", "g_mla_q", "g_mla_k", "b_f", "g_fox_q",
                "g_fox_k", "rel_bias", "g_ch_q", "g_ch_k", "w_br", "w_out", "g_cross", "g_mem", "w_xq",
                "w_xkv", "g_x_q", "g_x_k", "w_xo", "g_mlp", "w_1", "w_2")


def _pick(n, prefs):
    for p in prefs:
        if n % p == 0:
            return p
    raise ValueError(f"no block size among {prefs} divides {n}")


def _pcall(body, *, name, out_shape, in_specs, out_specs, grid=(), scratch=(), aliases=None):
    return pl.pallas_call(
        body, out_shape=out_shape, grid=grid, in_specs=in_specs, out_specs=out_specs,
        scratch_shapes=scratch, name=name, interpret=False,
        input_output_aliases=aliases or {},
        compiler_params=pltpu.CompilerParams(vmem_limit_bytes=VMEM_LIMIT_BYTES))


def _sds(shape, dtype):
    return jax.ShapeDtypeStruct(tuple(shape), dtype)


def _mx(v):
    return v.astype(MXU_DTYPE)


def mm_nn(a, b3, *, name, out_dtype, a_col0=0, res=None, relu2=False):
    M = a.shape[0]
    nb, K, Ns = b3.shape
    N = nb * Ns
    tm = _pick(M, (1024, 512, 256, 128))
    tk = _pick(K, (2048, 1024, 512, 256))
    tn = _pick(Ns, (512, 256, 128))
    assert a_col0 % tk == 0
    nk, nbs, ka0 = K // tk, Ns // tn, a_col0 // tk
    n_out = 2 if relu2 else 1

    def body(*refs):
        a_ref, b_ref = refs[0], refs[1]
        pos = 2
        res_ref = None
        if res is not None:
            res_ref = refs[pos]
            pos += 1
        outs = refs[pos:pos + n_out]
        acc_ref = refs[pos + n_out] if nk > 1 else None
        part = jnp.dot(_mx(a_ref[...]), _mx(b_ref[...]), preferred_element_type=F32)

        def finish(acc):
            if res_ref is not None:
                acc = acc + res_ref[...]
            outs[0][...] = acc.astype(outs[0].dtype)
            if relu2:
                r = jnp.maximum(acc, 0.0)
                outs[1][...] = (r * r).astype(outs[1].dtype)

        if nk == 1:
            finish(part)
        else:
            k = pl.program_id(2)

            @pl.when(k == 0)
            def _():
                acc_ref[...] = part

            @pl.when(k > 0)
            def _():
                acc_ref[...] += part

            @pl.when(k == nk - 1)
            def _():
                finish(acc_ref[...])

    in_specs = [pl.BlockSpec((tm, tk), lambda i, j, k: (i, ka0 + k)),
                pl.BlockSpec((None, tk, tn), lambda i, j, k: (j // nbs, k, j % nbs))]
    args = [a, b3]
    if res is not None:
        in_specs.append(pl.BlockSpec((tm, tn), lambda i, j, k: (i, j)))
        args.append(res)
    o_spec = pl.BlockSpec((tm, tn), lambda i, j, k: (i, j))
    if relu2:
        out_shape, out_specs = (_sds((M, N), out_dtype), _sds((M, N), out_dtype)), (o_spec, o_spec)
    else:
        out_shape, out_specs = _sds((M, N), out_dtype), o_spec
    scratch = (pltpu.VMEM((tm, tn), F32),) if nk > 1 else ()
    return _pcall(body, name=name, out_shape=out_shape, grid=(M // tm, N // tn, nk),
                  in_specs=in_specs, out_specs=out_specs, scratch=scratch)(*args)


def mm_nt(a, b3, *, name, out_dtype, a_col0=0, res=None, relu_mul=None):
    M = a.shape[0]
    nb, K, Ns = b3.shape
    tm = _pick(M, (1024, 512, 256, 128))
    tk = _pick(K, (1024, 512, 256))
    tn = _pick(Ns, (1024, 512, 256, 128))
    assert a_col0 % tn == 0
    nbs = Ns // tn
    nn, a0 = nb * nbs, a_col0 // tn

    def body(*refs):
        a_ref, b_ref = refs[0], refs[1]
        pos = 2
        mul_ref = res_ref = None
        if relu_mul is not None:
            mul_ref = refs[pos]
            pos += 1
        if res is not None:
            res_ref = refs[pos]
            pos += 1
        o_ref = refs[pos]
        acc_ref = refs[pos + 1] if nn > 1 else None
        part = lax.dot_general(_mx(a_ref[...]), _mx(b_ref[...]), (((1,), (1,)), ((), ())),
                               preferred_element_type=F32)

        def finish(acc):
            if mul_ref is not None:
                acc = acc * (2.0 * jnp.maximum(mul_ref[...].astype(F32), 0.0))
            if res_ref is not None:
                acc = acc + res_ref[...]
            o_ref[...] = acc.astype(o_ref.dtype)

        if nn == 1:
            finish(part)
        else:
            j = pl.program_id(2)

            @pl.when(j == 0)
            def _():
                acc_ref[...] = part

            @pl.when(j > 0)
            def _():
                acc_ref[...] += part

            @pl.when(j == nn - 1)
            def _():
                finish(acc_ref[...])

    in_specs = [pl.BlockSpec((tm, tn), lambda i, kk, j: (i, a0 + j)),
                pl.BlockSpec((None, tk, tn), lambda i, kk, j: (j // nbs, kk, j % nbs))]
    args = [a, b3]
    for extra in (relu_mul, res):
        if extra is not None:
            in_specs.append(pl.BlockSpec((tm, tk), lambda i, kk, j: (i, kk)))
            args.append(extra)
    scratch = (pltpu.VMEM((tm, tk), F32),) if nn > 1 else ()
    return _pcall(body, name=name, out_shape=_sds((M, K), out_dtype), grid=(M // tm, K // tk, nn),
                  in_specs=in_specs, out_specs=pl.BlockSpec((tm, tk), lambda i, kk, j: (i, kk)),
                  scratch=scratch)(*args)


def mm_tn(a, c, *, nb, name, out_dtype, K=None, N=None, a_col0=0, c_col0=0):
    M = a.shape[0]
    K = K or a.shape[1]
    N = N or c.shape[1]
    Ns = N // nb
    tm = _pick(M, (2048, 1024, 512, 256))
    tk = _pick(K, (1024, 512, 256))
    tn = _pick(Ns, (1024, 512, 256, 128))
    assert a_col0 % tk == 0 and c_col0 % tn == 0
    nm, nbs, a0, c0 = M // tm, Ns // tn, a_col0 // tk, c_col0 // tn

    def body(*refs):
        a_ref, c_ref, o_ref = refs[:3]
        acc_ref = refs[3] if nm > 1 else None
        part = lax.dot_general(_mx(a_ref[...]), _mx(c_ref[...]), (((0,), (0,)), ((), ())),
                               preferred_element_type=F32)
        if nm == 1:
            o_ref[...] = part.astype(o_ref.dtype)
        else:
            m = pl.program_id(2)

            @pl.when(m == 0)
            def _():
                acc_ref[...] = part

            @pl.when(m > 0)
            def _():
                acc_ref[...] += part

            @pl.when(m == nm - 1)
            def _():
                o_ref[...] = acc_ref[...].astype(o_ref.dtype)

    scratch = (pltpu.VMEM((tk, tn), F32),) if nm > 1 else ()
    return _pcall(
        body, name=name, out_shape=_sds((nb, K, Ns), out_dtype), grid=(N // tn, K // tk, nm),
        in_specs=[pl.BlockSpec((tm, tk), lambda j, kk, m: (m, a0 + kk)),
                  pl.BlockSpec((tm, tn), lambda j, kk, m: (m, c0 + j))],
        out_specs=pl.BlockSpec((None, tk, tn), lambda j, kk, m: (j // nbs, kk, j % nbs)),
        scratch=scratch)(a, c)


def rms_fwd(x, g, *, name, col0=0, width=None, out_dtype=BF16):
    R = x.shape[0]
    width = width or x.shape[1]
    assert col0 % width == 0
    cb = col0 // width
    tr = _pick(R, (512, 256, 128))

    def body(x_ref, g_ref, o_ref):
        xf = x_ref[...].astype(F32)
        r = lax.rsqrt(jnp.mean(xf * xf, axis=1, keepdims=True) + EPS)
        o_ref[...] = (xf * r * g_ref[...]).astype(o_ref.dtype)

    return _pcall(body, name=name, out_shape=_sds((R, width), out_dtype), grid=(R // tr,),
                  in_specs=[pl.BlockSpec((tr, width), lambda i: (i, cb)),
                            pl.BlockSpec((1, width), lambda i: (0, 0))],
                  out_specs=pl.BlockSpec((tr, width), lambda i: (i, 0)))(x, g)


def rms_bwd(x, g, dy, *, name, col0=0, width=None, res=None, dx_dtype=F32, need_dx=True):
    R = x.shape[0]
    width = width or x.shape[1]
    cb = col0 // width
    tr = _pick(R, (512, 256, 128))

    def body(*refs):
        x_ref, g_ref, dy_ref = refs[:3]
        pos = 3
        res_ref = None
        if res is not None:
            res_ref = refs[pos]
            pos += 1
        dx_ref = None
        if need_dx:
            dx_ref = refs[pos]
            pos += 1
        dg_ref = refs[pos]
        xf = x_ref[...].astype(F32)
        dyf = dy_ref[...].astype(F32)
        r = lax.rsqrt(jnp.mean(xf * xf, axis=1, keepdims=True) + EPS)
        xh = xf * r
        if need_dx:
            gy = dyf * g_ref[...]
            dx = r * (gy - xh * jnp.mean(gy * xh, axis=1, keepdims=True))
            if res_ref is not None:
                dx = dx + res_ref[...]
            dx_ref[...] = dx.astype(dx_ref.dtype)
        part = jnp.sum(dyf * xh, axis=0, keepdims=True)

        @pl.when(pl.program_id(0) == 0)
        def _():
            dg_ref[...] = part

        @pl.when(pl.program_id(0) > 0)
        def _():
            dg_ref[...] += part

    in_specs = [pl.BlockSpec((tr, width), lambda i: (i, cb)),
                pl.BlockSpec((1, width), lambda i: (0, 0)),
                pl.BlockSpec((tr, width), lambda i: (i, 0))]
    args = [x, g, dy]
    if res is not None:
        in_specs.append(pl.BlockSpec((tr, width), lambda i: (i, 0)))
        args.append(res)
    dg_shape, dg_spec = _sds((1, width), F32), pl.BlockSpec((1, width), lambda i: (0, 0))
    if need_dx:
        out_shape = (_sds((R, width), dx_dtype), dg_shape)
        out_specs = (pl.BlockSpec((tr, width), lambda i: (i, 0)), dg_spec)
    else:
        out_shape, out_specs = dg_shape, dg_spec
    out = _pcall(body, name=name, out_shape=out_shape, grid=(R // tr,), in_specs=in_specs,
                 out_specs=out_specs)(*args)
    return out if need_dx else (None, out)


HEAD_ROW_BLOCKS = (2048, 1024, 512, 256, 128)


def _rope_apply(y, c, sa, sb):
    return y * c + pltpu.roll(y, 96, 1) * sa + pltpu.roll(y, 32, 1) * sb


def _rope_transpose(dy, c, sa, sb):
    return dy * c + pltpu.roll(dy * sa, 32, 1) + pltpu.roll(dy * sb, 96, 1)


def rope_tables(seq):
    pos = jnp.arange(seq, dtype=F32)
    inv = ROPE_THETA ** (-jnp.arange(0, MLA_ROPE, 2, dtype=F32) / MLA_ROPE)
    ang = pos[:, None] * inv[None, :]
    cos, sin = jnp.cos(ang), jnp.sin(ang)
    z32, z64 = jnp.zeros_like(cos), jnp.zeros((seq, 64), F32)
    c = jnp.concatenate([cos, cos, z64], axis=1)
    sa = jnp.concatenate([-sin, z32, z64], axis=1)
    sb = jnp.concatenate([z32, sin, z64], axis=1)
    return c, sa, sb


def _head_vec(part_refs):
    xs = [p[...].astype(F32) for p in part_refs]
    return xs[0] if len(xs) == 1 else jnp.concatenate(xs, axis=1)


def prep_fwd(parts, g, *, name, n_heads, n_real, rope=None):
    rows = parts[0][0].shape[0]
    dh = sum(w for _, w, _ in parts)
    tr = _pick(rows, HEAD_ROW_BLOCKS)
    npart = len(parts)

    def body(*refs):
        part_refs, g_ref = refs[:npart], refs[npart]
        pos = npart + 1
        if rope is not None:
            c_ref, sa_ref, sb_ref = refs[pos:pos + 3]
            pos += 3
        o_ref = refs[pos]
        x = _head_vec(part_refs)
        r = lax.rsqrt(jnp.sum(x * x, axis=1, keepdims=True) * (1.0 / n_real) + EPS)
        y = x * r * g_ref[...]
        if rope is not None:
            yr = _rope_apply(y[:, dh - 128:], c_ref[...], sa_ref[...], sb_ref[...])
            y = jnp.concatenate([y[:, :dh - 128], yr], axis=1)
        o_ref[...] = y.astype(o_ref.dtype)

    in_specs, args = [], []
    for arr, w, fn in parts:
        in_specs.append(pl.BlockSpec((tr, w), functools.partial(lambda h, i, fn: (i, fn(h)), fn=fn)))
        args.append(arr)
    in_specs.append(pl.BlockSpec((1, dh), lambda h, i: (0, 0)))
    args.append(g)
    if rope is not None:
        for t in rope:
            in_specs.append(pl.BlockSpec((tr, 128), lambda h, i: (i, 0)))
            args.append(t)
    return _pcall(body, name=name, out_shape=_sds((n_heads, rows, dh), BF16), grid=(n_heads, rows // tr),
                  in_specs=in_specs, out_specs=pl.BlockSpec((None, tr, dh), lambda h, i: (h, i, 0)))(*args)


def _norm_bwd(x, g, dyn, n_real):
    r = lax.rsqrt(jnp.sum(x * x, axis=1, keepdims=True) * (1.0 / n_real) + EPS)
    xh = x * r
    gy = dyn * g
    dx = r * (gy - xh * (jnp.sum(gy * xh, axis=1, keepdims=True) * (1.0 / n_real)))
    return dx, jnp.sum(dyn * xh, axis=0, keepdims=True)


def prep_bwd_q(src, dy, g, *, name, n_heads, dh, n_real, rope=None, out_dtype=BF16):
    rows = src.shape[0]
    tr = _pick(rows, HEAD_ROW_BLOCKS)

    def body(*refs):
        x_ref, dy_ref, g_ref = refs[:3]
        pos = 3
        if rope is not None:
            c_ref, sa_ref, sb_ref = refs[pos:pos + 3]
            pos += 3
        dx_ref, dg_ref = refs[pos], refs[pos + 1]
        dyn = dy_ref[...].astype(F32)
        if rope is not None:
            dr = _rope_transpose(dyn[:, dh - 128:], c_ref[...], sa_ref[...], sb_ref[...])
            dyn = jnp.concatenate([dyn[:, :dh - 128], dr], axis=1)
        dx, dg = _norm_bwd(x_ref[...].astype(F32), g_ref[...], dyn, n_real)
        dx_ref[...] = dx.astype(dx_ref.dtype)
        first = jnp.logical_and(pl.program_id(0) == 0, pl.program_id(1) == 0)

        @pl.when(first)
        def _():
            dg_ref[...] = dg

        @pl.when(jnp.logical_not(first))
        def _():
            dg_ref[...] += dg

    in_specs = [pl.BlockSpec((tr, dh), lambda i, h: (i, h)),
                pl.BlockSpec((None, tr, dh), lambda i, h: (h, i, 0)),
                pl.BlockSpec((1, dh), lambda i, h: (0, 0))]
    args = [src, dy, g]
    if rope is not None:
        for t in rope:
            in_specs.append(pl.BlockSpec((tr, 128), lambda i, h: (i, 0)))
            args.append(t)
    return _pcall(body, name=name, out_shape=(_sds((rows, n_heads * dh), out_dtype), _sds((1, dh), F32)),
                  grid=(rows // tr, n_heads), in_specs=in_specs,
                  out_specs=(pl.BlockSpec((tr, dh), lambda i, h: (i, h)),
                             pl.BlockSpec((1, dh), lambda i, h: (0, 0))))(*args)


def prep_bwd_mla_k(kv_raw, zs, dkf, dv, g, rope, *, name):
    rows = kv_raw.shape[0]
    tr = _pick(rows, HEAD_ROW_BLOCKS)

    def body(kn_ref, kr_ref, dy_ref, dv_ref, g_ref, c_ref, sa_ref, sb_ref, dkv_ref, dkr_ref, dg_ref):
        h = pl.program_id(1)
        x = jnp.concatenate([kn_ref[...].astype(F32), kr_ref[...].astype(F32)], axis=1)
        dyn = dy_ref[...].astype(F32)
        dr = _rope_transpose(dyn[:, 128:], c_ref[...], sa_ref[...], sb_ref[...])
        dyn = jnp.concatenate([dyn[:, :128], dr], axis=1)
        dx, dg = _norm_bwd(x, g_ref[...], dyn, MLA_QK)
        dkv_ref[...] = jnp.concatenate([dx[:, :128], dv_ref[...].astype(F32)], axis=1).astype(dkv_ref.dtype)

        @pl.when(h == 0)
        def _():
            dkr_ref[...] = dx[:, 128:]

        @pl.when(h > 0)
        def _():
            dkr_ref[...] += dx[:, 128:]

        first = jnp.logical_and(pl.program_id(0) == 0, h == 0)

        @pl.when(first)
        def _():
            dg_ref[...] = dg

        @pl.when(jnp.logical_not(first))
        def _():
            dg_ref[...] += dg

    tab = pl.BlockSpec((tr, 128), lambda i, h: (i, 0))
    return _pcall(
        body, name=name,
        out_shape=(_sds((rows, N_HEADS * 256), BF16), _sds((rows, 128), F32), _sds((1, MLA_PAD), F32)),
        grid=(rows // tr, N_HEADS),
        in_specs=[pl.BlockSpec((tr, 128), lambda i, h: (i, 2 * h)),
                  pl.BlockSpec((tr, 128), lambda i, h: (i, Z_KR // 128)),
                  pl.BlockSpec((None, tr, MLA_PAD), lambda i, h: (h, i, 0)),
                  pl.BlockSpec((None, tr, 128), lambda i, h: (h, i, 0)),
                  pl.BlockSpec((1, MLA_PAD), lambda i, h: (0, 0)), tab, tab, tab],
        out_specs=(pl.BlockSpec((tr, 256), lambda i, h: (i, h)),
                   pl.BlockSpec((tr, 128), lambda i, h: (i, 0)),
                   pl.BlockSpec((1, MLA_PAD), lambda i, h: (0, 0))))(kv_raw, zs, dkf, dv, g, *rope)


def prep_bwd_groups(src, base_blk, dys, gs, *, name, n_heads, kinds, out_dtype=BF16):
    rows = src.shape[0]
    ng = len(kinds)
    J = ng * n_heads
    tr = _pick(rows, HEAD_ROW_BLOCKS)
    gstack = jnp.stack([gs[k] if kinds[k] == "norm" else jnp.ones((1, DH), F32) for k in range(ng)])

    def body(*refs):
        x_ref = refs[0]
        dy_refs = refs[1:1 + ng]
        g_ref, dx_ref, dg_ref = refs[1 + ng:4 + ng]
        j, i = pl.program_id(0), pl.program_id(1)
        grp = j // n_heads
        dy = dy_refs[0][...].astype(F32)
        for k in range(1, ng):
            dy = jnp.where(grp == k, dy_refs[k][...].astype(F32), dy)
        dx, dg = _norm_bwd(x_ref[...].astype(F32), g_ref[...], dy, DH)
        is_copy = functools.reduce(jnp.logical_or, [grp == k for k in range(ng) if kinds[k] == "copy"],
                                   jnp.bool_(False))
        dx_ref[...] = jnp.where(is_copy, dy, dx).astype(dx_ref.dtype)
        dg = jnp.where(is_copy, jnp.zeros_like(dg), dg)
        first = jnp.logical_and(j % n_heads == 0, i == 0)

        @pl.when(first)
        def _():
            dg_ref[...] = dg

        @pl.when(jnp.logical_not(first))
        def _():
            dg_ref[...] += dg

    in_specs = [pl.BlockSpec((tr, DH), lambda j, i: (i, base_blk + j))]
    for k in range(ng):
        in_specs.append(pl.BlockSpec(
            (None, tr, DH),
            functools.partial(lambda j, i, k: (jnp.clip(j - k * n_heads, 0, n_heads - 1),
                                               jnp.where(j // n_heads == k, i, 0), 0), k=k)))
    in_specs.append(pl.BlockSpec((None, 1, DH), lambda j, i: (j // n_heads, 0, 0)))
    return _pcall(body, name=name, out_shape=(_sds((rows, J * DH), out_dtype), _sds((ng, 1, DH), F32)),
                  grid=(J, rows // tr), in_specs=in_specs,
                  out_specs=(pl.BlockSpec((tr, DH), lambda j, i: (i, j)),
                             pl.BlockSpec((None, 1, DH), lambda j, i: (j // n_heads, 0, 0))))(src, *dys, gstack)


def _attn_cfg(mode, sq, sk):
    if mode == "chunk":
        tq = 128
        win = min((LEFT_CHUNKS + 2) * CHUNK, sk)
    else:
        tq = _pick(sq, (256, 128))
        win = sk
    scale = (MLA_QK if mode == "mla" else DH) ** -0.5
    return tq, win, scale


def _attn_key_rows(mode, i, tq, win, sk, run):
    if mode == "chunk":
        start = pl.multiple_of(jnp.clip((i - LEFT_CHUNKS // 2) * 128, 0, sk - win), 128)
        run(pl.ds(start, win), start)
    elif mode == "cross":
        run(slice(0, sk), 0)
    else:
        lax.switch(i, [functools.partial(run, slice(0, (b + 1) * tq), 0) for b in range(sk // tq)])


def _attn_scores(mode, i, tq, scale, q, kk, start, cq, ck, t_ref):
    nk = kk.shape[0]
    s = lax.dot_general(q, kk, (((1,), (1,)), ((), ())), preferred_element_type=F32) * scale
    if mode == "cross":
        return s
    t_pos = i * tq + lax.broadcasted_iota(jnp.int32, (tq, nk), 0)
    s_pos = start + lax.broadcasted_iota(jnp.int32, (tq, nk), 1)
    if mode == "fox":
        s = s + cq - ck
        allowed = s_pos <= t_pos
    else:
        qc, kc = lax.shift_right_logical(t_pos, CHUNK_SHIFT), lax.shift_right_logical(s_pos, CHUNK_SHIFT)
        allowed = kc <= qc
        if mode == "chunk":
            allowed = jnp.logical_and(allowed, kc >= qc - LEFT_CHUNKS)
            tiles = []
            for w in range(nk // 128):
                delta = i - (start // 128 + w)
                tiles.append(jnp.where(delta == 0, t_ref[0], jnp.where(delta == 1, t_ref[1], t_ref[2])))
            s = s + jnp.concatenate(tiles, axis=1)
    return jnp.where(allowed, s, NEG)


def attn_fwd(q, k, v_arr, v_blk, *, mode, name, n_heads, cq=None, ck=None, tiles=None):
    _, sq, dk = q.shape
    sk = k.shape[1]
    tq, win, scale = _attn_cfg(mode, sq, sk)

    def body(*refs):
        q_ref, k_ref, v_ref = refs[:3]
        pos = 3
        cq_ref = ck_ref = t_ref = None
        if mode == "fox":
            cq_ref, ck_ref = refs[pos:pos + 2]
            pos += 2
        if mode == "chunk":
            t_ref = refs[pos]
            pos += 1
        o_ref, lse_ref = refs[pos], refs[pos + 1]
        i = pl.program_id(1)

        def run(rows, start):
            cq, ck = (cq_ref[...], ck_ref[:, rows]) if mode == "fox" else (None, None)
            s = _attn_scores(mode, i, tq, scale, q_ref[...], k_ref[rows, :], start, cq, ck, t_ref)
            m = jnp.max(s, axis=1, keepdims=True)
            e = jnp.exp(s - m)
            l = jnp.sum(e, axis=1, keepdims=True)
            p = e * (1.0 / l)
            o_ref[...] = jnp.dot(_mx(p), _mx(v_ref[rows, :]), preferred_element_type=F32).astype(o_ref.dtype)
            lse_ref[...] = m + jnp.log(l)

        _attn_key_rows(mode, i, tq, win, sk, run)

    in_specs = [pl.BlockSpec((None, tq, dk), lambda h, i: (h, i, 0)),
                pl.BlockSpec((None, sk, dk), lambda h, i: (h, 0, 0)),
                pl.BlockSpec((sk, DH), lambda h, i: (0, v_blk(h)))]
    args = [q, k, v_arr]
    if mode == "fox":
        in_specs += [pl.BlockSpec((None, tq, 1), lambda h, i: (h, i, 0)),
                     pl.BlockSpec((None, 1, sk), lambda h, i: (h, 0, 0))]
        args += [cq, ck]
    if mode == "chunk":
        in_specs.append(pl.BlockSpec((3, None, 128, 128), lambda h, i: (0, h, 0, 0)))
        args.append(tiles)
    return _pcall(body, name=name,
                  out_shape=(_sds((sq, n_heads * DH), BF16), _sds((n_heads, sq, 1), F32)),
                  grid=(n_heads, sq // tq), in_specs=in_specs,
                  out_specs=(pl.BlockSpec((tq, DH), lambda h, i: (i, h)),
                             pl.BlockSpec((None, tq, 1), lambda h, i: (h, i, 0))))(*args)


def attn_bwd(q, k, v_arr, v_blk, o, do, lse, *, mode, name, n_heads, cq=None, ck=None, tiles=None):
    _, sq, dk = q.shape
    sk = k.shape[1]
    tq, win, scale = _attn_cfg(mode, sq, sk)
    n_extra = {"fox": 2, "chunk": 1}.get(mode, 0)

    def body(*refs):
        q_ref, k_ref, v_ref, o_ref, do_ref, lse_ref = refs[:6]
        pos = 6
        cq_ref = ck_ref = t_ref = None
        if mode == "fox":
            cq_ref, ck_ref = refs[pos:pos + 2]
            pos += 2
        if mode == "chunk":
            t_ref = refs[pos]
            pos += 1
        dq_ref, dk_ref, dv_ref = refs[pos:pos + 3]
        extra = refs[pos + 3:pos + 3 + n_extra]
        i = pl.program_id(1)

        @pl.when(i == 0)
        def _():
            dk_ref[...] = jnp.zeros_like(dk_ref)
            dv_ref[...] = jnp.zeros_like(dv_ref)
            if mode == "fox":
                extra[1][...] = jnp.zeros_like(extra[1])
            if mode == "chunk":
                extra[0][...] = jnp.zeros_like(extra[0])

        def run(rows, start):
            q = q_ref[...]
            do = do_ref[...]
            kk = k_ref[rows, :]
            cq, ck = (cq_ref[...], ck_ref[:, rows]) if mode == "fox" else (None, None)
            s = _attn_scores(mode, i, tq, scale, q, kk, start, cq, ck, t_ref)
            p = jnp.exp(s - lse_ref[...])
            drow = jnp.sum(do.astype(F32) * o_ref[...].astype(F32), axis=1, keepdims=True)
            dp = lax.dot_general(do, _mx(v_ref[rows, :]), (((1,), (1,)), ((), ())), preferred_element_type=F32)
            ds = p * (dp - drow)
            dsb = _mx(ds)
            dq_ref[...] = (jnp.dot(dsb, kk, preferred_element_type=F32) * scale).astype(dq_ref.dtype)
            dk_ref[rows, :] += lax.dot_general(dsb, q, (((0,), (0,)), ((), ())),
                                               preferred_element_type=F32) * scale
            dv_ref[rows, :] += lax.dot_general(_mx(p), do, (((0,), (0,)), ((), ())), preferred_element_type=F32)
            if mode == "chunk":
                dt_ref = extra[0]
                for w in range(win // 128):
                    delta = i - (start // 128 + w)
                    tile = ds[:, w * 128:(w + 1) * 128]
                    zero = jnp.zeros_like(tile)
                    dt_ref[0] += jnp.where(delta == 0, tile, zero)
                    dt_ref[1] += jnp.where(delta == 1, tile, zero)
                    dt_ref[2] += jnp.where(delta >= 2, tile, zero)
            if mode == "fox":
                extra[0][...] = jnp.sum(ds, axis=1, keepdims=True)
                extra[1][:, rows] -= jnp.sum(ds, axis=0, keepdims=True)

        _attn_key_rows(mode, i, tq, win, sk, run)

    in_specs = [pl.BlockSpec((None, tq, dk), lambda h, i: (h, i, 0)),
                pl.BlockSpec((None, sk, dk), lambda h, i: (h, 0, 0)),
                pl.BlockSpec((sk, DH), lambda h, i: (0, v_blk(h))),
                pl.BlockSpec((tq, DH), lambda h, i: (i, h)),
                pl.BlockSpec((tq, DH), lambda h, i: (i, h)),
                pl.BlockSpec((None, tq, 1), lambda h, i: (h, i, 0))]
    args = [q, k, v_arr, o, do, lse]
    out_shape = [_sds((n_heads, sq, dk), F32), _sds((n_heads, sk, dk), F32), _sds((n_heads, sk, DH), F32)]
    out_specs = [pl.BlockSpec((None, tq, dk), lambda h, i: (h, i, 0)),
                 pl.BlockSpec((None, sk, dk), lambda h, i: (h, 0, 0)),
                 pl.BlockSpec((None, sk, DH), lambda h, i: (h, 0, 0))]
    if mode == "fox":
        in_specs += [pl.BlockSpec((None, tq, 1), lambda h, i: (h, i, 0)),
                     pl.BlockSpec((None, 1, sk), lambda h, i: (h, 0, 0))]
        args += [cq, ck]
        out_shape += [_sds((n_heads, sq, 1), F32), _sds((n_heads, 1, sk), F32)]
        out_specs += [pl.BlockSpec((None, tq, 1), lambda h, i: (h, i, 0)),
                      pl.BlockSpec((None, 1, sk), lambda h, i: (h, 0, 0))]
    if mode == "chunk":
        in_specs.append(pl.BlockSpec((3, None, 128, 128), lambda h, i: (0, h, 0, 0)))
        args.append(tiles)
        out_shape.append(_sds((3, n_heads, 128, 128), F32))
        out_specs.append(pl.BlockSpec((3, None, 128, 128), lambda h, i: (0, h, 0, 0)))
    return _pcall(body, name=name, out_shape=tuple(out_shape), grid=(n_heads, sq // tq),
                  in_specs=in_specs, out_specs=tuple(out_specs))(*args)


REL_LANES = 384
REL_KBLK = 2048


def _rel_onehot(t, k):
    rho = k * REL_KBLK + lax.broadcasted_iota(jnp.int32, (REL_KBLK, REL_LANES), 0)
    lane = lax.broadcasted_iota(jnp.int32, (REL_KBLK, REL_LANES), 1)
    diff = lax.shift_right_logical(rho, 7) - jnp.bitwise_and(rho, 127)
    idx = jnp.where(t == 0, diff + REL_CLIP,
                    jnp.where(t == 1, jnp.minimum(diff + 128, REL_CLIP) + REL_CLIP, N_REL - 1))
    return jnp.where(idx == lane, 1.0, 0.0).astype(F32)


def relbias_onehot():
    def body(o_ref):
        o_ref[...] = _rel_onehot(pl.program_id(0), pl.program_id(1)).astype(o_ref.dtype)

    return _pcall(body, name="relbias_onehot", out_shape=_sds((3, 128 * 128, REL_LANES), BF16),
                  grid=(3, 128 * 128 // REL_KBLK), in_specs=[],
                  out_specs=pl.BlockSpec((None, REL_KBLK, REL_LANES), lambda t, k: (t, k, 0)))()


ONEHOT_SPEC = pl.BlockSpec((None, REL_KBLK, REL_LANES), lambda t, k: (t, k, 0))


def relbias_tiles(rel_bias, onehot):
    nh = rel_bias.shape[0]
    rb = jnp.pad(rel_bias, ((0, 0), (0, REL_LANES - N_REL)))

    def body(rb_ref, e_ref, o_ref):
        o_ref[...] = lax.dot_general(rb_ref[...], e_ref[...].astype(F32), (((1,), (1,)), ((), ())),
                                     preferred_element_type=F32, precision=lax.Precision.HIGHEST)

    flat = _pcall(body, name="relbias_tiles", out_shape=_sds((3, nh, 128 * 128), F32),
                  grid=(3, 128 * 128 // REL_KBLK),
                  in_specs=[pl.BlockSpec((nh, REL_LANES), lambda t, k: (0, 0)), ONEHOT_SPEC],
                  out_specs=pl.BlockSpec((None, nh, REL_KBLK), lambda t, k: (t, 0, k)))(rb, onehot)
    return flat.reshape(3, nh, 128, 128)


def relbias_tiles_bwd(dtiles, onehot):
    nh = dtiles.shape[1]

    def body(dt_ref, e_ref, o_ref):
        t, k = pl.program_id(0), pl.program_id(1)
        part = jnp.dot(dt_ref[...], e_ref[...].astype(F32), preferred_element_type=F32,
                       precision=lax.Precision.HIGHEST)
        first = jnp.logical_and(t == 0, k == 0)

        @pl.when(first)
        def _():
            o_ref[...] = part

        @pl.when(jnp.logical_not(first))
        def _():
            o_ref[...] += part

    out = _pcall(body, name="relbias_tiles_bwd", out_shape=_sds((nh, REL_LANES), F32),
                 grid=(3, 128 * 128 // REL_KBLK),
                 in_specs=[pl.BlockSpec((None, nh, REL_KBLK), lambda t, k: (t, 0, k)), ONEHOT_SPEC],
                 out_specs=pl.BlockSpec((nh, REL_LANES), lambda t, k: (0, 0)))(
                     dtiles.reshape(3, nh, 128 * 128), onehot)
    return out[:, :N_REL]


CUM_BLK = 256


def _tri(n, lower):
    r = lax.broadcasted_iota(jnp.int32, (n, n), 0)
    c = lax.broadcasted_iota(jnp.int32, (n, n), 1)
    return jnp.where(r >= c if lower else r <= c, 1.0, 0.0).astype(F32)


def fox_cum_fwd(zs, bf):
    S = zs.shape[0]
    tb = min(CUM_BLK, S)

    def body(f_ref, b_ref, cum_ref, cumt_ref, carry_ref):
        @pl.when(pl.program_id(0) == 0)
        def _():
            carry_ref[...] = jnp.zeros_like(carry_ref)

        x = f_ref[...] + b_ref[...]
        lane = lax.broadcasted_iota(jnp.int32, x.shape, 1)
        logf = jnp.where(lane < N_HEADS, jnp.minimum(x, 0.0) - jnp.log(1.0 + jnp.exp(-jnp.abs(x))), 0.0)
        cum = jnp.dot(_tri(tb, True), logf, preferred_element_type=F32,
                      precision=lax.Precision.HIGHEST) + carry_ref[...]
        carry_ref[...] = cum[tb - 1:tb, :]
        cum_ref[...] = cum
        cumt_ref[...] = cum.T

    return _pcall(body, name="fox_cum_fwd", out_shape=(_sds((S, 128), F32), _sds((128, S), F32)),
                  grid=(S // tb,),
                  in_specs=[pl.BlockSpec((tb, 128), lambda i: (i, Z_FF // 128)),
                            pl.BlockSpec((1, 128), lambda i: (0, 0))],
                  out_specs=(pl.BlockSpec((tb, 128), lambda i: (i, 0)),
                             pl.BlockSpec((128, tb), lambda i: (0, i))),
                  scratch=(pltpu.VMEM((1, 128), F32),))(zs, bf)


def fox_cum_bwd(zs, bf, dcum):
    S = zs.shape[0]
    tb = min(CUM_BLK, S)
    nblk = S // tb

    def body(f_ref, b_ref, d_ref, df_ref, db_ref, carry_ref):
        @pl.when(pl.program_id(0) == 0)
        def _():
            carry_ref[...] = jnp.zeros_like(carry_ref)
            db_ref[...] = jnp.zeros_like(db_ref)

        d = d_ref[...]
        dlogf = jnp.dot(_tri(tb, False), d, preferred_element_type=F32,
                        precision=lax.Precision.HIGHEST) + carry_ref[...]
        carry_ref[...] += jnp.sum(d, axis=0, keepdims=True)
        x = f_ref[...] + b_ref[...]
        lane = lax.broadcasted_iota(jnp.int32, x.shape, 1)
        dx = jnp.where(lane < N_HEADS, dlogf / (1.0 + jnp.exp(x)), 0.0)
        df_ref[...] = dx
        db_ref[...] += jnp.sum(dx, axis=0, keepdims=True)

    return _pcall(body, name="fox_cum_bwd", out_shape=(_sds((S, 128), F32), _sds((1, 128), F32)),
                  grid=(nblk,),
                  in_specs=[pl.BlockSpec((tb, 128), lambda i: (nblk - 1 - i, Z_FF // 128)),
                            pl.BlockSpec((1, 128), lambda i: (0, 0)),
                            pl.BlockSpec((tb, 128), lambda i: (nblk - 1 - i, 0))],
                  out_specs=(pl.BlockSpec((tb, 128), lambda i: (nblk - 1 - i, 0)),
                             pl.BlockSpec((1, 128), lambda i: (0, 0))),
                  scratch=(pltpu.VMEM((1, 128), F32),))(zs, bf, dcum)


def _sigmoid(x):
    return 1.0 / (1.0 + jnp.exp(-x))


def merge_fwd(z, projs):
    S = z.shape[0]
    tr, tc = _pick(S, (512, 256, 128)), 512
    nbc = D_MODEL // tc
    g0 = Z_GATE // tc

    def body(g0_ref, g1_ref, g2_ref, p0_ref, p1_ref, p2_ref, o_ref):
        acc = _sigmoid(g0_ref[...]) * p0_ref[...]
        acc += _sigmoid(g1_ref[...]) * p1_ref[...]
        acc += _sigmoid(g2_ref[...]) * p2_ref[...]
        o_ref[...] = acc.astype(o_ref.dtype)

    gspecs = [pl.BlockSpec((tr, tc), functools.partial(lambda i, j, n: (i, g0 + n * nbc + j), n=n))
              for n in range(3)]
    pspec = pl.BlockSpec((tr, tc), lambda i, j: (i, j))
    return _pcall(body, name="merge_fwd", out_shape=_sds((S, D_MODEL), BF16), grid=(S // tr, nbc),
                  in_specs=gspecs + [pspec] * 3, out_specs=pspec)(z, z, z, *projs)


def merge_bwd(z, projs, dmerged):
    S = z.shape[0]
    tr, tc = _pick(S, (512, 256, 128)), 512
    nbc = D_MODEL // tc
    g0 = Z_GATE // tc

    def body(g0_ref, g1_ref, g2_ref, p0_ref, p1_ref, p2_ref, dm_ref, dg0, dg1, dg2, dp0, dp1, dp2):
        dm = dm_ref[...]
        for g_ref, p_ref, dg_ref, dp_ref in ((g0_ref, p0_ref, dg0, dp0), (g1_ref, p1_ref, dg1, dp1),
                                             (g2_ref, p2_ref, dg2, dp2)):
            sg = _sigmoid(g_ref[...])
            dp_ref[...] = (dm * sg).astype(dp_ref.dtype)
            dg_ref[...] = (dm * p_ref[...] * sg * (1.0 - sg)).astype(dg_ref.dtype)

    gspecs = [pl.BlockSpec((tr, tc), functools.partial(lambda i, j, n: (i, g0 + n * nbc + j), n=n))
              for n in range(3)]
    pspec = pl.BlockSpec((tr, tc), lambda i, j: (i, j))
    out = _pcall(body, name="merge_bwd", out_shape=tuple(_sds((S, D_MODEL), BF16) for _ in range(6)),
                 grid=(S // tr, nbc), in_specs=gspecs + [pspec] * 4,
                 out_specs=tuple([pspec] * 6))(z, z, z, *projs, dmerged)
    return out[:3], out[3:]


def loss_head(y, target):
    S, D = y.shape
    tr = _pick(S, (256, 128))

    def body(y_ref, t_ref, dy_ref, l_ref):
        e = y_ref[...] - t_ref[...]
        dy_ref[...] = e * (1.0 / D)
        part = jnp.sum(jnp.sum(e * e, axis=1, keepdims=True), axis=0, keepdims=True) * (0.5 / D)

        @pl.when(pl.program_id(0) == 0)
        def _():
            l_ref[...] = part

        @pl.when(pl.program_id(0) > 0)
        def _():
            l_ref[...] += part

    spec = pl.BlockSpec((tr, D), lambda i: (i, 0))
    return _pcall(body, name="loss_head", out_shape=(_sds((S, D), F32), _sds((1, 1), F32)), grid=(S // tr,),
                  in_specs=[spec, spec], out_specs=(spec, pl.BlockSpec((1, 1), lambda i: (0, 0))))(y, target)


ADAMW_BLOCK_BYTES = 1024 * 1024


def adamw(w, gs, m, v, *, name):
    L, R, C = w.shape
    tr = _pick(R, (512, 256, 128, 64, 32, 16, 8))
    while tr * C * 4 > ADAMW_BLOCK_BYTES and tr % 16 == 0:
        tr //= 2
    c1 = 1.0 - ADAM_B1 ** ADAM_STEP
    c2 = 1.0 - ADAM_B2 ** ADAM_STEP

    def body(*refs):
        w_ref, m_ref, v_ref = refs[:3]
        g_refs = refs[3:3 + L]
        go_ref, d_ref, nm_ref, nv_ref = refs[3 + L:]
        l = pl.program_id(0)
        g_ = g_refs[0][...]
        for k in range(1, L):
            g_ = jnp.where(l == k, g_refs[k][...], g_)
        nm = ADAM_B1 * m_ref[...] + (1.0 - ADAM_B1) * g_
        nv = ADAM_B2 * v_ref[...] + (1.0 - ADAM_B2) * (g_ * g_)
        go_ref[...] = g_
        nm_ref[...] = nm
        nv_ref[...] = nv
        d_ref[...] = -ADAM_LR * ((nm / c1) / (jnp.sqrt(nv / c2) + ADAM_EPS) + ADAM_WD * w_ref[...])

    spec = pl.BlockSpec((None, tr, C), lambda l, i: (l, i, 0))
    gspecs = [pl.BlockSpec((tr, C), functools.partial(lambda l, i, k: (jnp.where(l == k, i, 0), 0), k=k))
              for k in range(L)]
    return _pcall(body, name=name, out_shape=tuple(_sds((L, R, C), F32) for _ in range(4)),
                  grid=(L, R // tr), in_specs=[spec] * 3 + gspecs, out_specs=(spec,) * 4)(w, m, v, *gs)


SHARD_SHAPES = {
    "w_uq": (Q_LORA, 384), "w_ukv": (KV_LORA, 512), "w_br": (3, MIX_W, 512), "w_out": (512, D_MODEL),
    "w_xq": (512, 512), "w_xkv": (512, 1024), "w_xo": (512, 512), "w_1": (D_MODEL, 2048),
    "w_2": (2048, D_MODEL), "w_in": (D_MODEL, 3282),
}
def _rows2d(a, name):
    shp = SHARD_SHAPES[name]
    return a.reshape(a.shape[:a.ndim - len(shp)] + (-1, shp[-1]))


def _cols_from_shards(g):
    return jnp.transpose(g, (1, 0, 2)).reshape(g.shape[1], 4 * g.shape[2])


def _cols_to_shards(w):
    return jnp.transpose(w.reshape(w.shape[0], 4, w.shape[1] // 4), (1, 0, 2))


def full_weights(g):
    zeros = lambda n: [jnp.zeros((D_MODEL, n), g["w_in"].dtype)] if n else []
    segs, at = [], 0
    for p0, o0, w in sorted(W_IN_PIECES):
        segs += zeros(p0 - at)
        while w > 0:
            s_, a = divmod(o0, W_IN_SHARD)
            take = min(w, W_IN_SHARD - a)
            segs.append(g["w_in"][s_][:, a:a + take])
            o0, p0, w = o0 + take, p0 + take, w - take
        at = p0
    w_in_p = jnp.concatenate(segs + zeros(Z_TOT - at), axis=1)
    w_uq =_cols_from_shards(g["w_uq"]).reshape(Q_LORA, N_HEADS, MLA_QK)
    w_uq_p = jnp.pad(w_uq, ((0, 0), (0, 0), (0, MLA_PAD - MLA_QK))).reshape(1, Q_LORA, N_HEADS * MLA_PAD)
    return dict(
        in_p=w_in_p[None], uq_p=w_uq_p, ukv=g["w_ukv"], br=[g["w_br"][:, n] for n in range(3)],
        out=g["w_out"].reshape(1, D_MODEL, D_MODEL), xq=g["w_xq"].reshape(1, D_MODEL, 512),
        xkv=g["w_xkv"].reshape(1, D_MODEL, 1024), xo=g["w_xo"], w1=g["w_1"],
        w2=g["w_2"].reshape(1, D_FF, D_MODEL))


def shard_grads(dw):
    dp = dw["in_p"][0]
    d_in = []
    for s_ in range(4):
        lo, hi, segs = s_ * W_IN_SHARD, (s_ + 1) * W_IN_SHARD, []
        for o0, p0, w in sorted((o0, p0, w) for p0, o0, w in W_IN_PIECES):
            a, b = max(lo, o0), min(hi, o0 + w)
            if a < b:
                segs.append(dp[:, p0 + a - o0:p0 + b - o0])
        d_in.append(jnp.concatenate(segs, axis=1))
    d_uq =dw["uq_p"].reshape(Q_LORA, N_HEADS, MLA_PAD)[:, :, :MLA_QK].reshape(Q_LORA, N_HEADS * MLA_QK)
    return {
        "w_in": jnp.stack(d_in), "w_uq": _cols_to_shards(d_uq), "w_ukv": dw["ukv"],
        "w_br": jnp.stack(dw["br"], axis=1), "w_out": dw["out"].reshape(4, 512, D_MODEL),
        "w_xq": dw["xq"].reshape(4, 512, 512), "w_xkv": dw["xkv"].reshape(4, 512, 1024), "w_xo": dw["xo"],
        "w_1": dw["w1"], "w_2": dw["w2"].reshape(4, 2048, D_MODEL)}


def layer_params(d, l):
    row = lambda v: v.reshape(1, -1).astype(F32)
    padto = lambda v, n: jnp.pad(row(v), ((0, 0), (0, n - v.shape[-1])))
    return dict(
        g_mix=row(d["g_mix"][l]), g_cq=row(d["g_cq"][l]), g_ckv=row(d["g_ckv"][l]),
        g_mla_q=padto(d["g_mla_q"][l], MLA_PAD), g_mla_k=padto(d["g_mla_k"][l], MLA_PAD),
        b_f=padto(d["b_f"][l], 128), g_fox_q=row(d["g_fox_q"][l]), g_fox_k=row(d["g_fox_k"][l]),
        rel_bias=d["rel_bias"][l].astype(F32), g_ch_q=row(d["g_ch_q"][l]), g_ch_k=row(d["g_ch_k"][l]),
        g_cross=row(d["g_cross"][l]), g_mem=row(d["g_mem"][l]), g_x_q=row(d["g_x_q"][l]),
        g_x_k=row(d["g_x_k"][l]), g_mlp=row(d["g_mlp"][l]))


FOX_B = Z_FOX // 128
CH_B = Z_CH // 128


def layer_fwd(x, mem, W, P, rope):
    S = x.shape[0]
    s = {}
    s["h"] = rms_fwd(x, P["g_mix"], name="rms_d")
    z = s["z"] = mm_nn(s["h"], W["in_p"], name="mm_in", out_dtype=F32)
    s["cq_n"] = rms_fwd(z, P["g_cq"], col0=Z_CQ, width=Q_LORA, name="rms_cq")
    s["ckv_n"] = rms_fwd(z, P["g_ckv"], col0=Z_CKV, width=KV_LORA, name="rms_ckv")
    s["q_raw"] = mm_nn(s["cq_n"], W["uq_p"], name="mm_uq", out_dtype=F32)
    s["kv_raw"] = mm_nn(s["ckv_n"], W["ukv"], name="mm_ukv", out_dtype=F32)
    s["qa"] = prep_fwd([(s["q_raw"], MLA_PAD, lambda h: h)], P["g_mla_q"], name="prep_mla_q",
                       n_heads=N_HEADS, n_real=MLA_QK, rope=rope)
    s["ka"] = prep_fwd([(s["kv_raw"], 128, lambda h: 2 * h), (z, 128, lambda h: Z_KR // 128)], P["g_mla_k"],
                       name="prep_mla_k", n_heads=N_HEADS, n_real=MLA_QK, rope=rope)
    s["o_a"], s["lse_a"] = attn_fwd(s["qa"], s["ka"], s["kv_raw"], lambda h: 2 * h + 1, mode="mla",
                                    name="attn_mla", n_heads=N_HEADS)
    s["qb"] = prep_fwd([(z, DH, lambda h: FOX_B + h)], P["g_fox_q"], name="prep_h", n_heads=N_HEADS, n_real=DH)
    s["kb"] = prep_fwd([(z, DH, lambda h: FOX_B + N_HEADS + h)], P["g_fox_k"], name="prep_h",
                       n_heads=N_HEADS, n_real=DH)
    _, cum_t = fox_cum_fwd(z, P["b_f"])
    s["cq"] = cum_t[:N_HEADS].reshape(N_HEADS, S, 1)
    s["ck"] = cum_t[:N_HEADS].reshape(N_HEADS, 1, S)
    s["o_b"], s["lse_b"] = attn_fwd(s["qb"], s["kb"], z, lambda h: FOX_B + 2 * N_HEADS + h, mode="fox",
                                    name="attn_fox", n_heads=N_HEADS, cq=s["cq"], ck=s["ck"])
    s["qc"] = prep_fwd([(z, DH, lambda h: CH_B + h)], P["g_ch_q"], name="prep_h", n_heads=N_HEADS, n_real=DH)
    s["kc"] = prep_fwd([(z, DH, lambda h: CH_B + N_HEADS + h)], P["g_ch_k"], name="prep_h",
                       n_heads=N_HEADS, n_real=DH)
    s["tiles"] = relbias_tiles(P["rel_bias"], P["rel_onehot"])
    s["o_c"], s["lse_c"] = attn_fwd(s["qc"], s["kc"], z, lambda h: CH_B + 2 * N_HEADS + h, mode="chunk",
                                    name="attn_chunk", n_heads=N_HEADS, tiles=s["tiles"])
    s["projs"] = [mm_nn(o, W["br"][n], name="mm_br", out_dtype=F32)
                  for n, o in enumerate((s["o_a"], s["o_b"], s["o_c"]))]
    s["merged"] = merge_fwd(z, s["projs"])
    x1 = s["x1"] = mm_nn(s["merged"], W["out"], name="mm_out", out_dtype=F32, res=x)
    s["hq"] = rms_fwd(x1, P["g_cross"], name="rms_d")
    s["xq_raw"] = mm_nn(s["hq"], W["xq"], name="mm_xq", out_dtype=F32)
    s["mem_n"] = rms_fwd(mem, P["g_mem"], name="rms_mem")
    s["mkv"] = mm_nn(s["mem_n"], W["xkv"], name="mm_xkv", out_dtype=F32)
    s["qx"] = prep_fwd([(s["xq_raw"], DH, lambda h: h)], P["g_x_q"], name="prep_xq", n_heads=X_HEADS, n_real=DH)
    s["kx"] = prep_fwd([(s["mkv"], DH, lambda h: h)], P["g_x_k"], name="prep_xk", n_heads=X_HEADS, n_real=DH)
    s["o_x"], s["lse_x"] = attn_fwd(s["qx"], s["kx"], s["mkv"], lambda h: X_HEADS + h, mode="cross",
                                    name="attn_cross", n_heads=X_HEADS)
    x2 = s["x2"] = mm_nn(s["o_x"], W["xo"], name="mm_xo", out_dtype=F32, res=x1)
    s["hm"] = rms_fwd(x2, P["g_mlp"], name="rms_d")
    s["a1"], s["act"] = mm_nn(s["hm"], W["w1"], name="mm_w1", out_dtype=BF16, relu2=True)
    x3 = mm_nn(s["act"], W["w2"], name="mm_w2", out_dtype=F32, res=x2)
    return x3, s


def layer_bwd(g, x, mem, W, P, rope, s):
    S = x.shape[0]
    z = s["z"]
    dw, ds = {}, {}
    da1 = mm_nt(g, W["w2"], name="mm_w2_dx", out_dtype=BF16, relu_mul=s["a1"])
    dw["w2"] = mm_tn(s["act"], g, nb=1, name="mm_w2_dw", out_dtype=BF16)
    dhm = mm_nt(da1, W["w1"], name="mm_w1_dx", out_dtype=F32)
    dw["w1"] = mm_tn(s["hm"], da1, nb=4, name="mm_w1_dw", out_dtype=BF16)
    g2, ds["g_mlp"] = rms_bwd(s["x2"], P["g_mlp"], dhm, name="rms_d_bwd", res=g)
    do_x = mm_nt(g2, W["xo"], name="mm_xo_dx", out_dtype=BF16)
    dw["xo"] = mm_tn(s["o_x"], g2, nb=4, name="mm_xo_dw", out_dtype=BF16)
    dqx, dkx, dvx = attn_bwd(s["qx"], s["kx"], s["mkv"], lambda h: X_HEADS + h, s["o_x"], do_x, s["lse_x"],
                             mode="cross", name="attn_cross_bwd", n_heads=X_HEADS)
    dxq_raw, ds["g_x_q"] = prep_bwd_q(s["xq_raw"], dqx, P["g_x_q"], name="prep_xq_bwd", n_heads=X_HEADS,
                                      dh=DH, n_real=DH)
    dmkv, dgk = prep_bwd_groups(s["mkv"], 0, [dkx, dvx], [P["g_x_k"], None], name="prep_xkv_bwd",
                                n_heads=X_HEADS, kinds=("norm", "copy"))
    ds["g_x_k"] = dgk[0]
    dhq = mm_nt(dxq_raw, W["xq"], name="mm_xq_dx", out_dtype=F32)
    dw["xq"] = mm_tn(s["hq"], dxq_raw, nb=1, name="mm_xq_dw", out_dtype=BF16)
    dmem_n = mm_nt(dmkv, W["xkv"], name="mm_xkv_dx", out_dtype=F32)
    dw["xkv"] = mm_tn(s["mem_n"], dmkv, nb=1, name="mm_xkv_dw", out_dtype=BF16)
    _, ds["g_mem"] = rms_bwd(mem, P["g_mem"], dmem_n, name="rms_mem_bwd", need_dx=False)
    g1, ds["g_cross"] = rms_bwd(s["x1"], P["g_cross"], dhq, name="rms_d_bwd", res=g2)
    dmerged = mm_nt(g1, W["out"], name="mm_out_dx", out_dtype=F32)
    dw["out"] = mm_tn(s["merged"], g1, nb=1, name="mm_out_dw", out_dtype=BF16)
    dgl, dproj = merge_bwd(z, s["projs"], dmerged)
    outs = (s["o_a"], s["o_b"], s["o_c"])
    do = [mm_nt(dproj[n], W["br"][n], name="mm_br_dx", out_dtype=BF16) for n in range(3)]
    dw["br"] = [mm_tn(outs[n], dproj[n], nb=4, name="mm_br_dw", out_dtype=BF16) for n in range(3)]
    dqc, dkc, dvc, dtiles = attn_bwd(s["qc"], s["kc"], z, lambda h: CH_B + 2 * N_HEADS + h, s["o_c"], do[2],
                                     s["lse_c"], mode="chunk", name="attn_chunk_bwd", n_heads=N_HEADS,
                                     tiles=s["tiles"])
    d_ch, dg_ch = prep_bwd_groups(z, CH_B, [dqc, dkc, dvc], [P["g_ch_q"], P["g_ch_k"], None],
                                  name="prep_h_bwd", n_heads=N_HEADS, kinds=("norm", "norm", "copy"))
    ds["g_ch_q"], ds["g_ch_k"] = dg_ch[0], dg_ch[1]
    ds["rel_bias"] = relbias_tiles_bwd(dtiles, P["rel_onehot"])
    dqb, dkb, dvb, dcq, dck = attn_bwd(s["qb"], s["kb"], z, lambda h: FOX_B + 2 * N_HEADS + h, s["o_b"], do[1],
                                       s["lse_b"], mode="fox", name="attn_fox_bwd", n_heads=N_HEADS,
                                       cq=s["cq"], ck=s["ck"])
    d_fox, dg_fox = prep_bwd_groups(z, FOX_B, [dqb, dkb, dvb], [P["g_fox_q"], P["g_fox_k"], None],
                                    name="prep_h_bwd", n_heads=N_HEADS, kinds=("norm", "norm", "copy"))
    ds["g_fox_q"], ds["g_fox_k"] = dg_fox[0], dg_fox[1]
    dcum = jnp.pad((dcq[:, :, 0] + dck[:, 0, :]).T, ((0, 0), (0, 128 - N_HEADS)))
    dff, dbf = fox_cum_bwd(z, P["b_f"], dcum)
    ds["b_f"] = dbf[:, :N_HEADS]
    dqa, dka, dva = attn_bwd(s["qa"], s["ka"], s["kv_raw"], lambda h: 2 * h + 1, s["o_a"], do[0], s["lse_a"],
                             mode="mla", name="attn_mla_bwd", n_heads=N_HEADS)
    dq_raw, dgq = prep_bwd_q(s["q_raw"], dqa, P["g_mla_q"], name="prep_mla_q_bwd", n_heads=N_HEADS,
                             dh=MLA_PAD, n_real=MLA_QK, rope=rope)
    dkv_raw, dkr, dgk = prep_bwd_mla_k(s["kv_raw"], z, dka, dva, P["g_mla_k"], rope, name="prep_mla_k_bwd")
    ds["g_mla_q"], ds["g_mla_k"] = dgq[:, :MLA_QK], dgk[:, :MLA_QK]
    dcq_n = mm_nt(dq_raw, W["uq_p"], name="mm_uq_dx", out_dtype=F32)
    dw["uq_p"] = mm_tn(s["cq_n"], dq_raw, nb=1, name="mm_uq_dw", out_dtype=BF16)
    dckv_n = mm_nt(dkv_raw, W["ukv"], name="mm_ukv_dx", out_dtype=F32)
    dw["ukv"] = mm_tn(s["ckv_n"], dkv_raw, nb=4, name="mm_ukv_dw", out_dtype=BF16)
    d_cq, ds["g_cq"] = rms_bwd(z, P["g_cq"], dcq_n, name="rms_cq_bwd", col0=Z_CQ, width=Q_LORA, dx_dtype=BF16)
    d_ckv, ds["g_ckv"] = rms_bwd(z, P["g_ckv"], dckv_n, name="rms_ckv_bwd", col0=Z_CKV, width=KV_LORA,
                                 dx_dtype=BF16)
    dz = jnp.concatenate([d_cq, d_ckv, dkr.astype(BF16), dff.astype(BF16), d_fox, d_ch, *dgl], axis=1)
    dh = mm_nt(dz, W["in_p"], name="mm_in_dx", out_dtype=F32)
    dw["in_p"] = mm_tn(s["h"], dz, nb=1, name="mm_in_dw", out_dtype=BF16)
    g0, ds["g_mix"] = rms_bwd(x, P["g_mix"], dh, name="rms_d_bwd", res=g1)
    return g0, dw, ds


def local_step(x, mem, target, Ps, weights_of, grads_done):
    rope = rope_tables(x.shape[0])
    onehot = relbias_onehot()
    Ps = [dict(P, rel_onehot=onehot) for P in Ps]
    Ws, saved, xs = [], [], [x]
    for l, P in enumerate(Ps):
        Ws.append(weights_of(l, xs[-1]))
        y, s = layer_fwd(xs[-1], mem, Ws[l], P, rope)
        xs.append(y)
        saved.append(s)
    g, loss = loss_head(xs[-1], target)
    dss, token = [], None
    for l in reversed(range(len(Ps))):
        P = Ps[l] if token is None else dict(Ps[l], g_mlp=Ps[l]["g_mlp"] + token[:1, :1])
        g, dw, ds = layer_bwd(g, xs[l], mem, Ws[l], P, rope, saved[l])
        token = grads_done(l, dw)
        dss.append(ds)
    return loss, g, dss[::-1]


HBM_SPEC = pl.BlockSpec(memory_space=pltpu.HBM)


def _place():
    return lax.axis_index("x"), lax.axis_index("y"), lax.axis_index("c")


def _other_chips(x, y):
    return [(1 - x, y), (x, 1 - y), (1 - x, 1 - y)]


def _remote(src, dst, send_sems, recv_sems, k, to):
    return pltpu.make_async_remote_copy(src_ref=src, dst_ref=dst, send_sem=send_sems.at[k],
                                        recv_sem=recv_sems.at[k], device_id=to, device_id_type=MESH)


def _comm_call(body, *, name, ins, out_shapes, n_sem):
    return pl.pallas_call(
        body, out_shape=tuple(out_shapes), in_specs=[HBM_SPEC] * len(ins),
        out_specs=tuple([HBM_SPEC] * len(out_shapes)),
        scratch_shapes=[pltpu.SemaphoreType.DMA((n_sem,)), pltpu.SemaphoreType.DMA((n_sem,))],
        name=name, interpret=False)(*ins)


SEM_SPEC = pl.BlockSpec(memory_space=pltpu.SEMAPHORE)
SPLIT_EFFECT = pltpu.SideEffectType.DATAFLOW_SIDE_EFFECTING


def _chip_exchange_copies(kind, srcs, lands, send_sems, recv_sems):
    x, y, c = _place()
    me = 2 * x + y
    cps = []
    for i, (src, land) in enumerate(zip(srcs, lands)):
        for j, (cx, cy) in enumerate(_other_chips(x, y)):
            if kind == "gather":
                h = src.shape[0] // 2
                s_ref, d_ref = src.at[pl.ds(c * h, h), :], land.at[me, pl.ds(c * h, h), :]
            else:
                s_ref, d_ref = src.at[2 * cx + cy], land.at[me]
            cps.append(_remote(s_ref, d_ref, send_sems, recv_sems, 3 * i + j, (cx, cy, c)))
    return cps


def chip_exchange_start(kind, srcs, land_shapes, *, name):
    n = len(srcs)

    def body(*refs):
        send_sems, recv_sems = refs[2 * n], refs[2 * n + 1]
        for cp in _chip_exchange_copies(kind, refs[:n], refs[n:2 * n], send_sems, recv_sems):
            cp.start()
        refs[-1][...] = jnp.zeros_like(refs[-1])

    hbm = lambda a: pltpu.with_memory_space_constraint(a, pltpu.HBM)
    ins = [hbm(s) for s in srcs] + [hbm(lax.empty(s.shape, s.dtype)) for s in land_shapes]
    sems = pltpu.SemaphoreType.DMA((3 * n,))
    out = pl.pallas_call(
        body, name=name, interpret=False,
        out_shape=(sems, sems, *[pltpu.HBM(a.shape, a.dtype) for a in ins], _sds((8, 128), F32)),
        in_specs=[HBM_SPEC] * (2 * n),
        out_specs=(SEM_SPEC, SEM_SPEC, *[HBM_SPEC] * (2 * n), pl.BlockSpec(memory_space=pltpu.VMEM)),
        input_output_aliases={i: 2 + i for i in range(2 * n)},
        compiler_params=pltpu.CompilerParams(has_side_effects=SPLIT_EFFECT))(*ins)
    return out[0], out[1], list(out[2:2 + n]), list(out[2 + n:2 + 2 * n]), out[-1]


def chip_exchange_wait(kind, started, after, *, name):
    send_sems, recv_sems, srcs, lands, _ = started
    n = len(srcs)

    def body(*refs):
        send_sems, recv_sems = refs[2 * n], refs[2 * n + 1]
        for cp in _chip_exchange_copies(kind, refs[:n], refs[n:2 * n], send_sems, recv_sems):
            cp.wait_send()
            cp.wait_recv()

    thru = srcs + lands
    out = pl.pallas_call(
        body, name=name, interpret=False,
        out_shape=tuple(pltpu.HBM(a.shape, a.dtype) for a in thru),
        in_specs=[HBM_SPEC] * (2 * n) + [SEM_SPEC, SEM_SPEC, pl.BlockSpec(memory_space=pl.ANY)],
        out_specs=tuple([HBM_SPEC] * (2 * n)),
        input_output_aliases={i: i for i in range(2 * n)},
        compiler_params=pltpu.CompilerParams(has_side_effects=SPLIT_EFFECT))(*thru, send_sems, recv_sems, after)
    return list(out[:n]), list(out[n:])


def gather_to_sibling(lands):
    n = len(lands)

    def body(*refs):
        outs = refs[n:2 * n]
        send_sems, recv_sems = refs[2 * n:]
        x, y, c = _place()
        sends = []
        for i in range(n):
            h = outs[i].shape[1] // 2
            for j, (cx, cy) in enumerate(_other_chips(x, y)):
                landed = outs[i].at[2 * cx + cy, pl.ds(c * h, h), :]
                cp = _remote(landed, landed, send_sems, recv_sems, 3 * i + j, (x, y, 1 - c))
                cp.start()
                sends.append(cp)
        for cp in sends:
            cp.wait()

    return pl.pallas_call(
        body, out_shape=tuple(_sds(a.shape, a.dtype) for a in lands), in_specs=[HBM_SPEC] * n,
        out_specs=tuple([HBM_SPEC] * n), input_output_aliases={i: i for i in range(n)},
        scratch_shapes=[pltpu.SemaphoreType.DMA((3 * n,)), pltpu.SemaphoreType.DMA((3 * n,))],
        name="gather_to_sibling", interpret=False)(*lands)


def rs_to_sibling(parts):
    n = len(parts)

    def body(*refs):
        srcs, outs = refs[:n], refs[n:2 * n]
        send_sems, recv_sems = refs[2 * n:]
        x, y, c = _place()
        sends = []
        for i in range(n):
            h = srcs[i].shape[1] // 2
            cp = _remote(srcs[i].at[:, pl.ds((1 - c) * h, h), :], outs[i], send_sems, recv_sems, i, (x, y, 1 - c))
            cp.start()
            sends.append(cp)
        for cp in sends:
            cp.wait()

    return _comm_call(body, name="rs_to_sibling", ins=parts, n_sem=n,
                      out_shapes=[_sds((4, p.shape[1] // 2, p.shape[2]), p.dtype) for p in parts])


def rs_share_halves(ts):
    n = len(ts)

    def body(*refs):
        srcs, outs = refs[:n], refs[n:2 * n]
        send_sems, recv_sems = refs[2 * n:]
        x, y, c = _place()
        sends = []
        for i in range(n):
            cp = _remote(srcs[i], outs[i].at[c], send_sems, recv_sems, i, (x, y, 1 - c))
            cp.start()
            sends.append(cp)
        for cp in sends:
            cp.wait()

    return _comm_call(body, name="rs_share_halves", ins=ts, n_sem=n,
                      out_shapes=[_sds((2,) + t.shape, t.dtype) for t in ts])


def add_pair(part, got, half_idx):
    _, a, b = part.shape
    h = a // 2
    tr = _pick(h, (512, 256, 128))

    def body(c_ref, p_ref, g_ref, o_ref):
        o_ref[...] = (p_ref[...].astype(F32) + g_ref[...].astype(F32)).astype(o_ref.dtype)

    spec = pl.BlockSpec((None, tr, b), lambda s_, i, c_ref: (s_, i, 0))
    grid_spec = pltpu.PrefetchScalarGridSpec(
        num_scalar_prefetch=1, grid=(4, h // tr),
        in_specs=[pl.BlockSpec((None, None, tr, b), lambda s_, i, c_ref: (s_, c_ref[0], i, 0)), spec],
        out_specs=spec)
    return pl.pallas_call(
        body, out_shape=_sds((4, h, b), part.dtype), grid_spec=grid_spec, name="rs_add_pair", interpret=False,
        compiler_params=pltpu.CompilerParams(vmem_limit_bytes=VMEM_LIMIT_BYTES))(
            half_idx, part.reshape(4, 2, h, b), got)


def sum_slots(r):
    _, h, b = r.shape
    tr = _pick(h, (512, 256, 128))

    def body(r0, r1, r2, r3, o_ref):
        o_ref[...] = ((r0[...].astype(F32) + r1[...].astype(F32)) + r2[...].astype(F32)) + r3[...].astype(F32)

    specs = [pl.BlockSpec((None, tr, b), functools.partial(lambda i, s_: (s_, i, 0), s_=s_)) for s_ in range(4)]
    return _pcall(body, name="rs_sum_slots", out_shape=_sds((h, b), F32), grid=(h // tr,),
                  in_specs=specs, out_specs=pl.BlockSpec((tr, b), lambda i: (i, 0)))(r, r, r, r)


def reduce_scatter_start(parts, core, *, name):
    half_idx = core.reshape(1).astype(jnp.int32)
    gots = rs_to_sibling(parts)
    chip_sums = [add_pair(p, g, half_idx) for p, g in zip(parts, gots)]
    return chip_exchange_start("scatter", chip_sums, chip_sums, name=name)


def reduce_scatter_finish(started, after, chip, core, *, name):
    chip_sums, slots = chip_exchange_wait("scatter", started, after, name=name)
    slots = [lax.dynamic_update_slice(s_, lax.dynamic_index_in_dim(q, chip, 0, keepdims=True), (chip, 0, 0))
             for s_, q in zip(slots, chip_sums)]
    halves = [sum_slots(s_) for s_ in slots]
    both = rs_share_halves(halves)
    both = [lax.dynamic_update_slice(o, t[None], (core, 0, 0)) for o, t in zip(both, halves)]
    return [o.reshape(2 * o.shape[1], o.shape[2]) for o in both]


def allreduce_small(v):
    Rs = v.shape[0]

    def body(v_ref, o_ref, buf, send_sems, recv_sems):
        x, y, c = _place()
        me = 4 * x + 2 * y + c
        buf[me] = v_ref[...]
        flips = [(fx, fy, fc) for fx in (0, 1) for fy in (0, 1) for fc in (0, 1)][1:]
        sends = []
        for k, (fx, fy, fc) in enumerate(flips):
            to = ((1 - x) if fx else x, (1 - y) if fy else y, (1 - c) if fc else c)
            cp = _remote(v_ref, buf.at[me], send_sems, recv_sems, k, to)
            cp.start()
            sends.append(cp)
        for cp in sends:
            cp.wait()
        acc = buf[0]
        for d in range(1, 8):
            acc = acc + buf[d]
        o_ref[...] = acc

    vm = pl.BlockSpec(memory_space=pltpu.VMEM)
    return pl.pallas_call(
        body, out_shape=_sds((Rs, 128), F32), in_specs=[vm], out_specs=vm,
        scratch_shapes=[pltpu.VMEM((8, Rs, 128), F32), pltpu.SemaphoreType.DMA((7,)),
                        pltpu.SemaphoreType.DMA((7,))],
        name="allreduce_small", interpret=False)(v)


INPUT_NAMES = (("x", "mem") + WEIGHT_ORDER + ("loss_target",) + tuple("m_" + n for n in WEIGHT_ORDER)
               + tuple("v_" + n for n in WEIGHT_ORDER))


def _pack_small(vals, n_layers, extra=None):
    flat = jnp.concatenate([vals[n].reshape(n_layers, -1).astype(F32) for n in SMALL_ORDER], axis=1).reshape(-1)
    if extra is not None:
        flat = jnp.concatenate([flat, extra.reshape(-1)])
    n = flat.shape[0]
    rows = -(-n // 1024) * 8
    return jnp.pad(flat, (0, rows * 128 - n)).reshape(rows, 128)


def _unpack_small(packed, like, n_layers):
    per_layer = sum(int(np.prod(like[n].shape[1:])) for n in SMALL_ORDER)
    body = packed.reshape(-1)[:n_layers * per_layer].reshape(n_layers, per_layer)
    out, off = {}, 0
    for n in SMALL_ORDER:
        k = int(np.prod(like[n].shape[1:]))
        out[n] = body[:, off:off + k].reshape(like[n].shape)
        off += k
    return out, packed.reshape(-1)[n_layers * per_layer]


def kernel(x, mem, g_mix, w_in, g_cq, w_uq, g_ckv, w_ukv, g_mla_q, g_mla_k, b_f, g_fox_q, g_fox_k, rel_bias, g_ch_q, g_ch_k, w_br, w_out, g_cross, g_mem, w_xq, w_xkv, g_x_q, g_x_k, w_xo, g_mlp, w_1, w_2, loss_target, m_g_mix, m_w_in, m_g_cq, m_w_uq, m_g_ckv, m_w_ukv, m_g_mla_q, m_g_mla_k, m_b_f, m_g_fox_q, m_g_fox_k, m_rel_bias, m_g_ch_q, m_g_ch_k, m_w_br, m_w_out, m_g_cross, m_g_mem, m_w_xq, m_w_xkv, m_g_x_q, m_g_x_k, m_w_xo, m_g_mlp, m_w_1, m_w_2, v_g_mix, v_w_in, v_g_cq, v_w_uq, v_g_ckv, v_w_ukv, v_g_mla_q, v_g_mla_k, v_b_f, v_g_fox_q, v_g_fox_k, v_rel_bias, v_g_ch_q, v_g_ch_k, v_w_br, v_w_out, v_g_cross, v_g_mem, v_w_xq, v_w_xkv, v_g_x_q, v_g_x_k, v_w_xo, v_g_mlp, v_w_1, v_w_2):
    d = dict(zip(INPUT_NAMES, (x, mem, g_mix, w_in, g_cq, w_uq, g_ckv, w_ukv, g_mla_q, g_mla_k, b_f, g_fox_q, g_fox_k, rel_bias, g_ch_q, g_ch_k, w_br, w_out, g_cross, g_mem, w_xq, w_xkv, g_x_q, g_x_k, w_xo, g_mlp, w_1, w_2, loss_target, m_g_mix, m_w_in, m_g_cq, m_w_uq, m_g_ckv, m_w_ukv, m_g_mla_q, m_g_mla_k, m_b_f, m_g_fox_q, m_g_fox_k, m_rel_bias, m_g_ch_q, m_g_ch_k, m_w_br, m_w_out, m_g_cross, m_g_mem, m_w_xq, m_w_xkv, m_g_x_q, m_g_x_k, m_w_xo, m_g_mlp, m_w_1, m_w_2, v_g_mix, v_w_in, v_g_cq, v_w_uq, v_g_ckv, v_w_ukv, v_g_mla_q, v_g_mla_k, v_b_f, v_g_fox_q, v_g_fox_k, v_rel_bias, v_g_ch_q, v_g_ch_k, v_w_br, v_w_out, v_g_cross, v_g_mem, v_w_xq, v_w_xkv, v_g_x_q, v_g_x_k, v_w_xo, v_g_mlp, v_w_1, v_w_2)))
    n_layers = g_mix.shape[0]
    assert x.shape[0] == 1 and x.shape[2] == D_MODEL and mem.shape[1:] == (MEM_LEN, D_MODEL)
    for n in PACK_ORDER:
        assert d[n].shape[1:] == SHARD_SHAPES[n], (n, d[n].shape)

    chip = 2 * lax.axis_index("x") + lax.axis_index("y")
    core = lax.axis_index("c")

    gathers = []
    for l in range(n_layers):
        mine = [_rows2d(d[n][l].astype(BF16), n) for n in PACK_ORDER]
        gathers.append(chip_exchange_start("gather", mine, [_sds((4,) + m_.shape, BF16) for m_ in mine],
                                           name=f"gather_start_{l}"))

    def weights_of(l, x_l):
        mine, lands = chip_exchange_wait("gather", gathers[l], x_l, name=f"gather_wait_{l}")
        full = [lax.dynamic_update_slice(t, m_[None], (chip, 0, 0)) for t, m_ in zip(gather_to_sibling(lands), mine)]
        return full_weights({n: f.reshape((4,) + SHARD_SHAPES[n]) for n, f in zip(PACK_ORDER, full)})

    scatters = {}

    def grads_done(l, dw):
        sg = shard_grads(dw)
        scatters[l] = reduce_scatter_start([_rows2d(sg[n], n) for n in PACK_ORDER], core, name=f"scatter_start_{l}")
        return scatters[l][4]

    Ps = [layer_params(d, l) for l in range(n_layers)]
    started = functools.reduce(jnp.add, [g[4] for g in gathers])
    Ps[0] = dict(Ps[0], g_mix=Ps[0]["g_mix"] + started[:1, :1])
    loss, dx, dss = local_step(x[0], mem[0], loss_target[0], Ps, weights_of, grads_done)
    big = [None] * n_layers
    for l in reversed(range(n_layers)):
        big[l] = reduce_scatter_finish(scatters[l], dx, chip, core, name=f"scatter_wait_{l}")

    small_local = {n: jnp.stack([dss[l][n].reshape(d[n].shape[1:]) for l in range(n_layers)]) for n in SMALL_ORDER}
    small_sum, loss_sum = _unpack_small(allreduce_small(_pack_small(small_local, n_layers, extra=loss)),
                                        {n: d[n] for n in SMALL_ORDER}, n_layers)

    grads, delta, new_m, new_v = {}, {}, {}, {}
    for i, n in enumerate(PACK_ORDER):
        outs = adamw(_rows2d(d[n], n), [big[l][i] for l in range(n_layers)], _rows2d(d["m_" + n], n),
                     _rows2d(d["v_" + n], n), name="adamw_" + n)
        grads[n], delta[n], new_m[n], new_v[n] = (o.reshape(d[n].shape) for o in outs)
    like = {n: d[n] for n in SMALL_ORDER}
    sm = adamw(_pack_small(like, n_layers)[None], [_pack_small(small_sum, n_layers)],
               _pack_small({n: d["m_" + n] for n in SMALL_ORDER}, n_layers)[None],
               _pack_small({n: d["v_" + n] for n in SMALL_ORDER}, n_layers)[None], name="adamw_small")
    for res, src in zip((grads, delta, new_m, new_v), sm):
        res.update(_unpack_small(src[0], like, n_layers)[0])

    return (loss_sum, dx[None], *[grads[n] for n in WEIGHT_ORDER], *[delta[n] for n in WEIGHT_ORDER],
            *[new_m[n] for n in WEIGHT_ORDER], *[new_v[n] for n in WEIGHT_ORDER])
```

```python
import functools

import numpy as np
import jax
import jax.numpy as jnp
from jax import lax
from jax.experimental import pallas as pl
from jax.experimental.pallas import tpu as pltpu

F32 = jnp.float32
BF16 = jnp.bfloat16
MXU_DTYPE = jnp.bfloat16
MESH = pl.DeviceIdType.MESH

D_MODEL = 2048
MIX_W = 1024
N_HEADS = 8
DH = 128
MLA_NOPE = 128
MLA_ROPE = 64
MLA_QK = MLA_NOPE + MLA_ROPE
MLA_PAD = 256
Q_LORA = 512
KV_LORA = 256
CHUNK = 64
CHUNK_SHIFT = CHUNK.bit_length() - 1
LEFT_CHUNKS = 8
REL_CLIP = 128
N_REL = 2 * REL_CLIP + 1
X_HEADS = 4
MEM_LEN = 256
D_FF = 8192
ROPE_THETA = 10000.0
EPS = 1e-6
NEG = -1e30

Z_CQ, Z_CKV, Z_KR, Z_FF = 0, 512, 768, 896
ZS_W = 1024
Z_FOX = 1024
Z_CH = Z_FOX + 3 * MIX_W
Z_GATE = Z_CH + 3 * MIX_W
Z_TOT = Z_GATE + 3 * D_MODEL
W_IN_CUTS = (0, 512, 768, 832, 3904, 3912, 6984, 13128)
W_IN_SHARD = W_IN_CUTS[-1] // 4
W_IN_PIECES = ((0, 0, 832), (Z_FF, 3904, 8), (Z_FOX, 832, 3072), (Z_CH, 3912, 9216))

ADAM_LR, ADAM_B1, ADAM_B2, ADAM_EPS, ADAM_WD, ADAM_STEP = 0.001, 0.9, 0.999, 1e-08, 0.01, 10

VMEM_LIMIT_BYTES = 56 * 1024 * 1024
PACK_ORDER = ("w_uq", "w_ukv", "w_br", "w_out", "w_xq", "w_xkv", "w_xo", "w_1", "w_2", "w_in")
SMALL_ORDER = ("g_mix", "g_cq", "g_ckv", "g_mla_q", "g_mla_k", "b_f", "g_fox_q", "g_fox_k", "rel_bias",
               "g_ch_q", "g_ch_k", "g_cross", "g_mem", "g_x_q", "g_x_k", "g_mlp")
WEIGHT_ORDER = ("g_mix", "w_in", "g_cq", "w_uq", "g_ckv", "w_ukv", "g_mla_q", "g_mla_k", "b_f", "g_fox_q",
                "g_fox_k", "rel_bias", "g_ch_q", "g_ch_k", "w_br", "w_out", "g_cross", "g_mem", "w_xq",
                "w_xkv", "g_x_q", "g_x_k", "w_xo", "g_mlp", "w_1", "w_2")


def _pick(n, prefs):
    for p in prefs:
        if n % p == 0:
            return p
    raise ValueError(f"no block size among {prefs} divides {n}")


def _pcall(body, *, name, out_shape, in_specs, out_specs, grid=(), scratch=(), aliases=None):
    return pl.pallas_call(
        body, out_shape=out_shape, grid=grid, in_specs=in_specs, out_specs=out_specs,
        scratch_shapes=scratch, name=name, interpret=False,
        input_output_aliases=aliases or {},
        compiler_params=pltpu.CompilerParams(vmem_limit_bytes=VMEM_LIMIT_BYTES))


def _sds(shape, dtype):
    return jax.ShapeDtypeStruct(tuple(shape), dtype)


def _mx(v):
    return v.astype(MXU_DTYPE)


def mm_nn(a, b3, *, name, out_dtype, a_col0=0, res=None, relu2=False):
    M = a.shape[0]
    nb, K, Ns = b3.shape
    N = nb * Ns
    tm = _pick(M, (1024, 512, 256, 128))
    tk = _pick(K, (2048, 1024, 512, 256))
    tn = _pick(Ns, (512, 256, 128))
    assert a_col0 % tk == 0
    nk, nbs, ka0 = K // tk, Ns // tn, a_col0 // tk
    n_out = 2 if relu2 else 1

    def body(*refs):
        a_ref, b_ref = refs[0], refs[1]
        pos = 2
        res_ref = None
        if res is not None:
            res_ref = refs[pos]
            pos += 1
        outs = refs[pos:pos + n_out]
        acc_ref = refs[pos + n_out] if nk > 1 else None
        part = jnp.dot(_mx(a_ref[...]), _mx(b_ref[...]), preferred_element_type=F32)

        def finish(acc):
            if res_ref is not None:
                acc = acc + res_ref[...]
            outs[0][...] = acc.astype(outs[0].dtype)
            if relu2:
                r = jnp.maximum(acc, 0.0)
                outs[1][...] = (r * r).astype(outs[1].dtype)

        if nk == 1:
            finish(part)
        else:
            k = pl.program_id(2)

            @pl.when(k == 0)
            def _():
                acc_ref[...] = part

            @pl.when(k > 0)
            def _():
                acc_ref[...] += part

            @pl.when(k == nk - 1)
            def _():
                finish(acc_ref[...])

    in_specs = [pl.BlockSpec((tm, tk), lambda i, j, k: (i, ka0 + k)),
                pl.BlockSpec((None, tk, tn), lambda i, j, k: (j // nbs, k, j % nbs))]
    args = [a, b3]
    if res is not None:
        in_specs.append(pl.BlockSpec((tm, tn), lambda i, j, k: (i, j)))
        args.append(res)
    o_spec = pl.BlockSpec((tm, tn), lambda i, j, k: (i, j))
    if relu2:
        out_shape, out_specs = (_sds((M, N), out_dtype), _sds((M, N), out_dtype)), (o_spec, o_spec)
    else:
        out_shape, out_specs = _sds((M, N), out_dtype), o_spec
    scratch = (pltpu.VMEM((tm, tn), F32),) if nk > 1 else ()
    return _pcall(body, name=name, out_shape=out_shape, grid=(M // tm, N // tn, nk),
                  in_specs=in_specs, out_specs=out_specs, scratch=scratch)(*args)


def mm_nt(a, b3, *, name, out_dtype, a_col0=0, res=None, relu_mul=None):
    M = a.shape[0]
    nb, K, Ns = b3.shape
    tm = _pick(M, (1024, 512, 256, 128))
    tk = _pick(K, (1024, 512, 256))
    tn = _pick(Ns, (1024, 512, 256, 128))
    assert a_col0 % tn == 0
    nbs = Ns // tn
    nn, a0 = nb * nbs, a_col0 // tn

    def body(*refs):
        a_ref, b_ref = refs[0], refs[1]
        pos = 2
        mul_ref = res_ref = None
        if relu_mul is not None:
            mul_ref = refs[pos]
            pos += 1
        if res is not None:
            res_ref = refs[pos]
            pos += 1
        o_ref = refs[pos]
        acc_ref = refs[pos + 1] if nn > 1 else None
        part = lax.dot_general(_mx(a_ref[...]), _mx(b_ref[...]), (((1,), (1,)), ((), ())),
                               preferred_element_type=F32)

        def finish(acc):
            if mul_ref is not None:
                acc = acc * (2.0 * jnp.maximum(mul_ref[...].astype(F32), 0.0))
            if res_ref is not None:
                acc = acc + res_ref[...]
            o_ref[...] = acc.astype(o_ref.dtype)

        if nn == 1:
            finish(part)
        else:
            j = pl.program_id(2)

            @pl.when(j == 0)
            def _():
                acc_ref[...] = part

            @pl.when(j > 0)
            def _():
                acc_ref[...] += part

            @pl.when(j == nn - 1)
            def _():
                finish(acc_ref[...])

    in_specs = [pl.BlockSpec((tm, tn), lambda i, kk, j: (i, a0 + j)),
                pl.BlockSpec((None, tk, tn), lambda i, kk, j: (j // nbs, kk, j % nbs))]
    args = [a, b3]
    for extra in (relu_mul, res):
        if extra is not None:
            in_specs.append(pl.BlockSpec((tm, tk), lambda i, kk, j: (i, kk)))
            args.append(extra)
    scratch = (pltpu.VMEM((tm, tk), F32),) if nn > 1 else ()
    return _pcall(body, name=name, out_shape=_sds((M, K), out_dtype), grid=(M // tm, K // tk, nn),
                  in_specs=in_specs, out_specs=pl.BlockSpec((tm, tk), lambda i, kk, j: (i, kk)),
                  scratch=scratch)(*args)


def mm_tn(a, c, *, nb, name, out_dtype, K=None, N=None, a_col0=0, c_col0=0):
    M = a.shape[0]
    K = K or a.shape[1]
    N = N or c.shape[1]
    Ns = N // nb
    tm = _pick(M, (2048, 1024, 512, 256))
    tk = _pick(K, (1024, 512, 256))
    tn = _pick(Ns, (1024, 512, 256, 128))
    assert a_col0 % tk == 0 and c_col0 % tn == 0
    nm, nbs, a0, c0 = M // tm, Ns // tn, a_col0 // tk, c_col0 // tn

    def body(*refs):
        a_ref, c_ref, o_ref = refs[:3]
        acc_ref = refs[3] if nm > 1 else None
        part = lax.dot_general(_mx(a_ref[...]), _mx(c_ref[...]), (((0,), (0,)), ((), ())),
                               preferred_element_type=F32)
        if nm == 1:
            o_ref[...] = part.astype(o_ref.dtype)
        else:
            m = pl.program_id(2)

            @pl.when(m == 0)
            def _():
                acc_ref[...] = part

            @pl.when(m > 0)
            def _():
                acc_ref[...] += part

            @pl.when(m == nm - 1)
            def _():
                o_ref[...] = acc_ref[...].astype(o_ref.dtype)

    scratch = (pltpu.VMEM((tk, tn), F32),) if nm > 1 else ()
    return _pcall(
        body, name=name, out_shape=_sds((nb, K, Ns), out_dtype), grid=(N // tn, K // tk, nm),
        in_specs=[pl.BlockSpec((tm, tk), lambda j, kk, m: (m, a0 + kk)),
                  pl.BlockSpec((tm, tn), lambda j, kk, m: (m, c0 + j))],
        out_specs=pl.BlockSpec((None, tk, tn), lambda j, kk, m: (j // nbs, kk, j % nbs)),
        scratch=scratch)(a, c)


def rms_fwd(x, g, *, name, col0=0, width=None, out_dtype=BF16):
    R = x.shape[0]
    width = width or x.shape[1]
    assert col0 % width == 0
    cb = col0 // width
    tr = _pick(R, (512, 256, 128))

    def body(x_ref, g_ref, o_ref):
        xf = x_ref[...].astype(F32)
        r = lax.rsqrt(jnp.mean(xf * xf, axis=1, keepdims=True) + EPS)
        o_ref[...] = (xf * r * g_ref[...]).astype(o_ref.dtype)

    return _pcall(body, name=name, out_shape=_sds((R, width), out_dtype), grid=(R // tr,),
                  in_specs=[pl.BlockSpec((tr, width), lambda i: (i, cb)),
                            pl.BlockSpec((1, width), lambda i: (0, 0))],
                  out_specs=pl.BlockSpec((tr, width), lambda i: (i, 0)))(x, g)


def rms_bwd(x, g, dy, *, name, col0=0, width=None, res=None, dx_dtype=F32, need_dx=True):
    R = x.shape[0]
    width = width or x.shape[1]
    cb = col0 // width
    tr = _pick(R, (512, 256, 128))

    def body(*refs):
        x_ref, g_ref, dy_ref = refs[:3]
        pos = 3
        res_ref = None
        if res is not None:
            res_ref = refs[pos]
            pos += 1
        dx_ref = None
        if need_dx:
            dx_ref = refs[pos]
            pos += 1
        dg_ref = refs[pos]
        xf = x_ref[...].astype(F32)
        dyf = dy_ref[...].astype(F32)
        r = lax.rsqrt(jnp.mean(xf * xf, axis=1, keepdims=True) + EPS)
        xh = xf * r
        if need_dx:
            gy = dyf * g_ref[...]
            dx = r * (gy - xh * jnp.mean(gy * xh, axis=1, keepdims=True))
            if res_ref is not None:
                dx = dx + res_ref[...]
            dx_ref[...] = dx.astype(dx_ref.dtype)
        part = jnp.sum(dyf * xh, axis=0, keepdims=True)

        @pl.when(pl.program_id(0) == 0)
        def _():
            dg_ref[...] = part

        @pl.when(pl.program_id(0) > 0)
        def _():
            dg_ref[...] += part

    in_specs = [pl.BlockSpec((tr, width), lambda i: (i, cb)),
                pl.BlockSpec((1, width), lambda i: (0, 0)),
                pl.BlockSpec((tr, width), lambda i: (i, 0))]
    args = [x, g, dy]
    if res is not None:
        in_specs.append(pl.BlockSpec((tr, width), lambda i: (i, 0)))
        args.append(res)
    dg_shape, dg_spec = _sds((1, width), F32), pl.BlockSpec((1, width), lambda i: (0, 0))
    if need_dx:
        out_shape = (_sds((R, width), dx_dtype), dg_shape)
        out_specs = (pl.BlockSpec((tr, width), lambda i: (i, 0)), dg_spec)
    else:
        out_shape, out_specs = dg_shape, dg_spec
    out = _pcall(body, name=name, out_shape=out_shape, grid=(R // tr,), in_specs=in_specs,
                 out_specs=out_specs)(*args)
    return out if need_dx else (None, out)


HEAD_ROW_BLOCKS = (2048, 1024, 512, 256, 128)


def _rope_apply(y, c, sa, sb):
    return y * c + pltpu.roll(y, 96, 1) * sa + pltpu.roll(y, 32, 1) * sb


def _rope_transpose(dy, c, sa, sb):
    return dy * c + pltpu.roll(dy * sa, 32, 1) + pltpu.roll(dy * sb, 96, 1)


def rope_tables(seq):
    pos = jnp.arange(seq, dtype=F32)
    inv = ROPE_THETA ** (-jnp.arange(0, MLA_ROPE, 2, dtype=F32) / MLA_ROPE)
    ang = pos[:, None] * inv[None, :]
    cos, sin = jnp.cos(ang), jnp.sin(ang)
    z32, z64 = jnp.zeros_like(cos), jnp.zeros((seq, 64), F32)
    c = jnp.concatenate([cos, cos, z64], axis=1)
    sa = jnp.concatenate([-sin, z32, z64], axis=1)
    sb = jnp.concatenate([z32, sin, z64], axis=1)
    return c, sa, sb


def _head_vec(part_refs):
    xs = [p[...].astype(F32) for p in part_refs]
    return xs[0] if len(xs) == 1 else jnp.concatenate(xs, axis=1)


def prep_fwd(parts, g, *, name, n_heads, n_real, rope=None):
    rows = parts[0][0].shape[0]
    dh = sum(w for _, w, _ in parts)
    tr = _pick(rows, HEAD_ROW_BLOCKS)
    npart = len(parts)

    def body(*refs):
        part_refs, g_ref = refs[:npart], refs[npart]
        pos = npart + 1
        if rope is not None:
            c_ref, sa_ref, sb_ref = refs[pos:pos + 3]
            pos += 3
        o_ref = refs[pos]
        x = _head_vec(part_refs)
        r = lax.rsqrt(jnp.sum(x * x, axis=1, keepdims=True) * (1.0 / n_real) + EPS)
        y = x * r * g_ref[...]
        if rope is not None:
            yr = _rope_apply(y[:, dh - 128:], c_ref[...], sa_ref[...], sb_ref[...])
            y = jnp.concatenate([y[:, :dh - 128], yr], axis=1)
        o_ref[...] = y.astype(o_ref.dtype)

    in_specs, args = [], []
    for arr, w, fn in parts:
        in_specs.append(pl.BlockSpec((tr, w), functools.partial(lambda h, i, fn: (i, fn(h)), fn=fn)))
        args.append(arr)
    in_specs.append(pl.BlockSpec((1, dh), lambda h, i: (0, 0)))
    args.append(g)
    if rope is not None:
        for t in rope:
            in_specs.append(pl.BlockSpec((tr, 128), lambda h, i: (i, 0)))
            args.append(t)
    return _pcall(body, name=name, out_shape=_sds((n_heads, rows, dh), BF16), grid=(n_heads, rows // tr),
                  in_specs=in_specs, out_specs=pl.BlockSpec((None, tr, dh), lambda h, i: (h, i, 0)))(*args)


def _norm_bwd(x, g, dyn, n_real):
    r = lax.rsqrt(jnp.sum(x * x, axis=1, keepdims=True) * (1.0 / n_real) + EPS)
    xh = x * r
    gy = dyn * g
    dx = r * (gy - xh * (jnp.sum(gy * xh, axis=1, keepdims=True) * (1.0 / n_real)))
    return dx, jnp.sum(dyn * xh, axis=0, keepdims=True)


def prep_bwd_q(src, dy, g, *, name, n_heads, dh, n_real, rope=None, out_dtype=BF16):
    rows = src.shape[0]
    tr = _pick(rows, HEAD_ROW_BLOCKS)

    def body(*refs):
        x_ref, dy_ref, g_ref = refs[:3]
        pos = 3
        if rope is not None:
            c_ref, sa_ref, sb_ref = refs[pos:pos + 3]
            pos += 3
        dx_ref, dg_ref = refs[pos], refs[pos + 1]
        dyn = dy_ref[...].astype(F32)
        if rope is not None:
            dr = _rope_transpose(dyn[:, dh - 128:], c_ref[...], sa_ref[...], sb_ref[...])
            dyn = jnp.concatenate([dyn[:, :dh - 128], dr], axis=1)
        dx, dg = _norm_bwd(x_ref[...].astype(F32), g_ref[...], dyn, n_real)
        dx_ref[...] = dx.astype(dx_ref.dtype)
        first = jnp.logical_and(pl.program_id(0) == 0, pl.program_id(1) == 0)

        @pl.when(first)
        def _():
            dg_ref[...] = dg

        @pl.when(jnp.logical_not(first))
        def _():
            dg_ref[...] += dg

    in_specs = [pl.BlockSpec((tr, dh), lambda i, h: (i, h)),
                pl.BlockSpec((None, tr, dh), lambda i, h: (h, i, 0)),
                pl.BlockSpec((1, dh), lambda i, h: (0, 0))]
    args = [src, dy, g]
    if rope is not None:
        for t in rope:
            in_specs.append(pl.BlockSpec((tr, 128), lambda i, h: (i, 0)))
            args.append(t)
    return _pcall(body, name=name, out_shape=(_sds((rows, n_heads * dh), out_dtype), _sds((1, dh), F32)),
                  grid=(rows // tr, n_heads), in_specs=in_specs,
                  out_specs=(pl.BlockSpec((tr, dh), lambda i, h: (i, h)),
                             pl.BlockSpec((1, dh), lambda i, h: (0, 0))))(*args)


def prep_bwd_mla_k(kv_raw, zs, dkf, dv, g, rope, *, name):
    rows = kv_raw.shape[0]
    tr = _pick(rows, HEAD_ROW_BLOCKS)

    def body(kn_ref, kr_ref, dy_ref, dv_ref, g_ref, c_ref, sa_ref, sb_ref, dkv_ref, dkr_ref, dg_ref):
        h = pl.program_id(1)
        x = jnp.concatenate([kn_ref[...].astype(F32), kr_ref[...].astype(F32)], axis=1)
        dyn = dy_ref[...].astype(F32)
        dr = _rope_transpose(dyn[:, 128:], c_ref[...], sa_ref[...], sb_ref[...])
        dyn = jnp.concatenate([dyn[:, :128], dr], axis=1)
        dx, dg = _norm_bwd(x, g_ref[...], dyn, MLA_QK)
        dkv_ref[...] = jnp.concatenate([dx[:, :128], dv_ref[...].astype(F32)], axis=1).astype(dkv_ref.dtype)

        @pl.when(h == 0)
        def _():
            dkr_ref[...] = dx[:, 128:]

        @pl.when(h > 0)
        def _():
            dkr_ref[...] += dx[:, 128:]

        first = jnp.logical_and(pl.program_id(0) == 0, h == 0)

        @pl.when(first)
        def _():
            dg_ref[...] = dg

        @pl.when(jnp.logical_not(first))
        def _():
            dg_ref[...] += dg

    tab = pl.BlockSpec((tr, 128), lambda i, h: (i, 0))
    return _pcall(
        body, name=name,
        out_shape=(_sds((rows, N_HEADS * 256), BF16), _sds((rows, 128), F32), _sds((1, MLA_PAD), F32)),
        grid=(rows // tr, N_HEADS),
        in_specs=[pl.BlockSpec((tr, 128), lambda i, h: (i, 2 * h)),
                  pl.BlockSpec((tr, 128), lambda i, h: (i, Z_KR // 128)),
                  pl.BlockSpec((None, tr, MLA_PAD), lambda i, h: (h, i, 0)),
                  pl.BlockSpec((None, tr, 128), lambda i, h: (h, i, 0)),
                  pl.BlockSpec((1, MLA_PAD), lambda i, h: (0, 0)), tab, tab, tab],
        out_specs=(pl.BlockSpec((tr, 256), lambda i, h: (i, h)),
                   pl.BlockSpec((tr, 128), lambda i, h: (i, 0)),
                   pl.BlockSpec((1, MLA_PAD), lambda i, h: (0, 0))))(kv_raw, zs, dkf, dv, g, *rope)


def prep_bwd_groups(src, base_blk, dys, gs, *, name, n_heads, kinds, out_dtype=BF16):
    rows = src.shape[0]
    ng = len(kinds)
    J = ng * n_heads
    tr = _pick(rows, HEAD_ROW_BLOCKS)
    gstack = jnp.stack([gs[k] if kinds[k] == "norm" else jnp.ones((1, DH), F32) for k in range(ng)])

    def body(*refs):
        x_ref = refs[0]
        dy_refs = refs[1:1 + ng]
        g_ref, dx_ref, dg_ref = refs[1 + ng:4 + ng]
        j, i = pl.program_id(0), pl.program_id(1)
        grp = j // n_heads
        dy = dy_refs[0][...].astype(F32)
        for k in range(1, ng):
            dy = jnp.where(grp == k, dy_refs[k][...].astype(F32), dy)
        dx, dg = _norm_bwd(x_ref[...].astype(F32), g_ref[...], dy, DH)
        is_copy = functools.reduce(jnp.logical_or, [grp == k for k in range(ng) if kinds[k] == "copy"],
                                   jnp.bool_(False))
        dx_ref[...] = jnp.where(is_copy, dy, dx).astype(dx_ref.dtype)
        dg = jnp.where(is_copy, jnp.zeros_like(dg), dg)
        first = jnp.logical_and(j % n_heads == 0, i == 0)

        @pl.when(first)
        def _():
            dg_ref[...] = dg

        @pl.when(jnp.logical_not(first))
        def _():
            dg_ref[...] += dg

    in_specs = [pl.BlockSpec((tr, DH), lambda j, i: (i, base_blk + j))]
    for k in range(ng):
        in_specs.append(pl.BlockSpec(
            (None, tr, DH),
            functools.partial(lambda j, i, k: (jnp.clip(j - k * n_heads, 0, n_heads - 1),
                                               jnp.where(j // n_heads == k, i, 0), 0), k=k)))
    in_specs.append(pl.BlockSpec((None, 1, DH), lambda j, i: (j // n_heads, 0, 0)))
    return _pcall(body, name=name, out_shape=(_sds((rows, J * DH), out_dtype), _sds((ng, 1, DH), F32)),
                  grid=(J, rows // tr), in_specs=in_specs,
                  out_specs=(pl.BlockSpec((tr, DH), lambda j, i: (i, j)),
                             pl.BlockSpec((None, 1, DH), lambda j, i: (j // n_heads, 0, 0))))(src, *dys, gstack)


def _attn_cfg(mode, sq, sk):
    if mode == "chunk":
        tq = 128
        win = min((LEFT_CHUNKS + 2) * CHUNK, sk)
    else:
        tq = _pick(sq, (256, 128))
        win = sk
    scale = (MLA_QK if mode == "mla" else DH) ** -0.5
    return tq, win, scale


def _attn_key_rows(mode, i, tq, win, sk, run):
    if mode == "chunk":
        start = pl.multiple_of(jnp.clip((i - LEFT_CHUNKS // 2) * 128, 0, sk - win), 128)
        run(pl.ds(start, win), start)
    elif mode == "cross":
        run(slice(0, sk), 0)
    else:
        lax.switch(i, [functools.partial(run, slice(0, (b + 1) * tq), 0) for b in range(sk // tq)])


def _attn_scores(mode, i, tq, scale, q, kk, start, cq, ck, t_ref):
    nk = kk.shape[0]
    s = lax.dot_general(q, kk, (((1,), (1,)), ((), ())), preferred_element_type=F32) * scale
    if mode == "cross":
        return s
    t_pos = i * tq + lax.broadcasted_iota(jnp.int32, (tq, nk), 0)
    s_pos = start + lax.broadcasted_iota(jnp.int32, (tq, nk), 1)
    if mode == "fox":
        s = s + cq - ck
        allowed = s_pos <= t_pos
    else:
        qc, kc = lax.shift_right_logical(t_pos, CHUNK_SHIFT), lax.shift_right_logical(s_pos, CHUNK_SHIFT)
        allowed = kc <= qc
        if mode == "chunk":
            allowed = jnp.logical_and(allowed, kc >= qc - LEFT_CHUNKS)
            tiles = []
            for w in range(nk // 128):
                delta = i - (start // 128 + w)
                tiles.append(jnp.where(delta == 0, t_ref[0], jnp.where(delta == 1, t_ref[1], t_ref[2])))
            s = s + jnp.concatenate(tiles, axis=1)
    return jnp.where(allowed, s, NEG)


def attn_fwd(q, k, v_arr, v_blk, *, mode, name, n_heads, cq=None, ck=None, tiles=None):
    _, sq, dk = q.shape
    sk = k.shape[1]
    tq, win, scale = _attn_cfg(mode, sq, sk)

    def body(*refs):
        q_ref, k_ref, v_ref = refs[:3]
        pos = 3
        cq_ref = ck_ref = t_ref = None
        if mode == "fox":
            cq_ref, ck_ref = refs[pos:pos + 2]
            pos += 2
        if mode == "chunk":
            t_ref = refs[pos]
            pos += 1
        o_ref, lse_ref = refs[pos], refs[pos + 1]
        i = pl.program_id(1)

        def run(rows, start):
            cq, ck = (cq_ref[...], ck_ref[:, rows]) if mode == "fox" else (None, None)
            s = _attn_scores(mode, i, tq, scale, q_ref[...], k_ref[rows, :], start, cq, ck, t_ref)
            m = jnp.max(s, axis=1, keepdims=True)
            e = jnp.exp(s - m)
            l = jnp.sum(e, axis=1, keepdims=True)
            p = e * (1.0 / l)
            o_ref[...] = jnp.dot(_mx(p), _mx(v_ref[rows, :]), preferred_element_type=F32).astype(o_ref.dtype)
            lse_ref[...] = m + jnp.log(l)

        _attn_key_rows(mode, i, tq, win, sk, run)

    in_specs = [pl.BlockSpec((None, tq, dk), lambda h, i: (h, i, 0)),
                pl.BlockSpec((None, sk, dk), lambda h, i: (h, 0, 0)),
                pl.BlockSpec((sk, DH), lambda h, i: (0, v_blk(h)))]
    args = [q, k, v_arr]
    if mode == "fox":
        in_specs += [pl.BlockSpec((None, tq, 1), lambda h, i: (h, i, 0)),
                     pl.BlockSpec((None, 1, sk), lambda h, i: (h, 0, 0))]
        args += [cq, ck]
    if mode == "chunk":
        in_specs.append(pl.BlockSpec((3, None, 128, 128), lambda h, i: (0, h, 0, 0)))
        args.append(tiles)
    return _pcall(body, name=name,
                  out_shape=(_sds((sq, n_heads * DH), BF16), _sds((n_heads, sq, 1), F32)),
                  grid=(n_heads, sq // tq), in_specs=in_specs,
                  out_specs=(pl.BlockSpec((tq, DH), lambda h, i: (i, h)),
                             pl.BlockSpec((None, tq, 1), lambda h, i: (h, i, 0))))(*args)


def attn_bwd(q, k, v_arr, v_blk, o, do, lse, *, mode, name, n_heads, cq=None, ck=None, tiles=None):
    _, sq, dk = q.shape
    sk = k.shape[1]
    tq, win, scale = _attn_cfg(mode, sq, sk)
    n_extra = {"fox": 2, "chunk": 1}.get(mode, 0)

    def body(*refs):
        q_ref, k_ref, v_ref, o_ref, do_ref, lse_ref = refs[:6]
        pos = 6
        cq_ref = ck_ref = t_ref = None
        if mode == "fox":
            cq_ref, ck_ref = refs[pos:pos + 2]
            pos += 2
        if mode == "chunk":
            t_ref = refs[pos]
            pos += 1
        dq_ref, dk_ref, dv_ref = refs[pos:pos + 3]
        extra = refs[pos + 3:pos + 3 + n_extra]
        i = pl.program_id(1)

        @pl.when(i == 0)
        def _():
            dk_ref[...] = jnp.zeros_like(dk_ref)
            dv_ref[...] = jnp.zeros_like(dv_ref)
            if mode == "fox":
                extra[1][...] = jnp.zeros_like(extra[1])
            if mode == "chunk":
                extra[0][...] = jnp.zeros_like(extra[0])

        def run(rows, start):
            q = q_ref[...]
            do = do_ref[...]
            kk = k_ref[rows, :]
            cq, ck = (cq_ref[...], ck_ref[:, rows]) if mode == "fox" else (None, None)
            s = _attn_scores(mode, i, tq, scale, q, kk, start, cq, ck, t_ref)
            p = jnp.exp(s - lse_ref[...])
            drow = jnp.sum(do.astype(F32) * o_ref[...].astype(F32), axis=1, keepdims=True)
            dp = lax.dot_general(do, _mx(v_ref[rows, :]), (((1,), (1,)), ((), ())), preferred_element_type=F32)
            ds = p * (dp - drow)
            dsb = _mx(ds)
            dq_ref[...] = (jnp.dot(dsb, kk, preferred_element_type=F32) * scale).astype(dq_ref.dtype)
            dk_ref[rows, :] += lax.dot_general(dsb, q, (((0,), (0,)), ((), ())),
                                               preferred_element_type=F32) * scale
            dv_ref[rows, :] += lax.dot_general(_mx(p), do, (((0,), (0,)), ((), ())), preferred_element_type=F32)
            if mode == "chunk":
                dt_ref = extra[0]
                for w in range(win // 128):
                    delta = i - (start // 128 + w)
                    tile = ds[:, w * 128:(w + 1) * 128]
                    zero = jnp.zeros_like(tile)
                    dt_ref[0] += jnp.where(delta == 0, tile, zero)
                    dt_ref[1] += jnp.where(delta == 1, tile, zero)
                    dt_ref[2] += jnp.where(delta >= 2, tile, zero)
            if mode == "fox":
                extra[0][...] = jnp.sum(ds, axis=1, keepdims=True)
                extra[1][:, rows] -= jnp.sum(ds, axis=0, keepdims=True)

        _attn_key_rows(mode, i, tq, win, sk, run)

    in_specs = [pl.BlockSpec((None, tq, dk), lambda h, i: (h, i, 0)),
                pl.BlockSpec((None, sk, dk), lambda h, i: (h, 0, 0)),
                pl.BlockSpec((sk, DH), lambda h, i: (0, v_blk(h))),
                pl.BlockSpec((tq, DH), lambda h, i: (i, h)),
                pl.BlockSpec((tq, DH), lambda h, i: (i, h)),
                pl.BlockSpec((None, tq, 1), lambda h, i: (h, i, 0))]
    args = [q, k, v_arr, o, do, lse]
    out_shape = [_sds((n_heads, sq, dk), F32), _sds((n_heads, sk, dk), F32), _sds((n_heads, sk, DH), F32)]
    out_specs = [pl.BlockSpec((None, tq, dk), lambda h, i: (h, i, 0)),
                 pl.BlockSpec((None, sk, dk), lambda h, i: (h, 0, 0)),
                 pl.BlockSpec((None, sk, DH), lambda h, i: (h, 0, 0))]
    if mode == "fox":
        in_specs += [pl.BlockSpec((None, tq, 1), lambda h, i: (h, i, 0)),
                     pl.BlockSpec((None, 1, sk), lambda h, i: (h, 0, 0))]
        args += [cq, ck]
        out_shape += [_sds((n_heads, sq, 1), F32), _sds((n_heads, 1, sk), F32)]
        out_specs += [pl.BlockSpec((None, tq, 1), lambda h, i: (h, i, 0)),
                      pl.BlockSpec((None, 1, sk), lambda h, i: (h, 0, 0))]
    if mode == "chunk":
        in_specs.append(pl.BlockSpec((3, None, 128, 128), lambda h, i: (0, h, 0, 0)))
        args.append(tiles)
        out_shape.append(_sds((3, n_heads, 128, 128), F32))
        out_specs.append(pl.BlockSpec((3, None, 128, 128), lambda h, i: (0, h, 0, 0)))
    return _pcall(body, name=name, out_shape=tuple(out_shape), grid=(n_heads, sq // tq),
                  in_specs=in_specs, out_specs=tuple(out_specs))(*args)


REL_LANES = 384
REL_KBLK = 2048


def _rel_onehot(t, k):
    rho = k * REL_KBLK + lax.broadcasted_iota(jnp.int32, (REL_KBLK, REL_LANES), 0)
    lane = lax.broadcasted_iota(jnp.int32, (REL_KBLK, REL_LANES), 1)
    diff = lax.shift_right_logical(rho, 7) - jnp.bitwise_and(rho, 127)
    idx = jnp.where(t == 0, diff + REL_CLIP,
                    jnp.where(t == 1, jnp.minimum(diff + 128, REL_CLIP) + REL_CLIP, N_REL - 1))
    return jnp.where(idx == lane, 1.0, 0.0).astype(F32)


def relbias_onehot():
    def body(o_ref):
        o_ref[...] = _rel_onehot(pl.program_id(0), pl.program_id(1)).astype(o_ref.dtype)

    return _pcall(body, name="relbias_onehot", out_shape=_sds((3, 128 * 128, REL_LANES), BF16),
                  grid=(3, 128 * 128 // REL_KBLK), in_specs=[],
                  out_specs=pl.BlockSpec((None, REL_KBLK, REL_LANES), lambda t, k: (t, k, 0)))()


ONEHOT_SPEC = pl.BlockSpec((None, REL_KBLK, REL_LANES), lambda t, k: (t, k, 0))


def relbias_tiles(rel_bias, onehot):
    nh = rel_bias.shape[0]
    rb = jnp.pad(rel_bias, ((0, 0), (0, REL_LANES - N_REL)))

    def body(rb_ref, e_ref, o_ref):
        o_ref[...] = lax.dot_general(rb_ref[...], e_ref[...].astype(F32), (((1,), (1,)), ((), ())),
                                     preferred_element_type=F32, precision=lax.Precision.HIGHEST)

    flat = _pcall(body, name="relbias_tiles", out_shape=_sds((3, nh, 128 * 128), F32),
                  grid=(3, 128 * 128 // REL_KBLK),
                  in_specs=[pl.BlockSpec((nh, REL_LANES), lambda t, k: (0, 0)), ONEHOT_SPEC],
                  out_specs=pl.BlockSpec((None, nh, REL_KBLK), lambda t, k: (t, 0, k)))(rb, onehot)
    return flat.reshape(3, nh, 128, 128)


def relbias_tiles_bwd(dtiles, onehot):
    nh = dtiles.shape[1]

    def body(dt_ref, e_ref, o_ref):
        t, k = pl.program_id(0), pl.program_id(1)
        part = jnp.dot(dt_ref[...], e_ref[...].astype(F32), preferred_element_type=F32,
                       precision=lax.Precision.HIGHEST)
        first = jnp.logical_and(t == 0, k == 0)

        @pl.when(first)
        def _():
            o_ref[...] = part

        @pl.when(jnp.logical_not(first))
        def _():
            o_ref[...] += part

    out = _pcall(body, name="relbias_tiles_bwd", out_shape=_sds((nh, REL_LANES), F32),
                 grid=(3, 128 * 128 // REL_KBLK),
                 in_specs=[pl.BlockSpec((None, nh, REL_KBLK), lambda t, k: (t, 0, k)), ONEHOT_SPEC],
                 out_specs=pl.BlockSpec((nh, REL_LANES), lambda t, k: (0, 0)))(
                     dtiles.reshape(3, nh, 128 * 128), onehot)
    return out[:, :N_REL]


CUM_BLK = 256


def _tri(n, lower):
    r = lax.broadcasted_iota(jnp.int32, (n, n), 0)
    c = lax.broadcasted_iota(jnp.int32, (n, n), 1)
    return jnp.where(r >= c if lower else r <= c, 1.0, 0.0).astype(F32)


def fox_cum_fwd(zs, bf):
    S = zs.shape[0]
    tb = min(CUM_BLK, S)

    def body(f_ref, b_ref, cum_ref, cumt_ref, carry_ref):
        @pl.when(pl.program_id(0) == 0)
        def _():
            carry_ref[...] = jnp.zeros_like(carry_ref)

        x = f_ref[...] + b_ref[...]
        lane = lax.broadcasted_iota(jnp.int32, x.shape, 1)
        logf = jnp.where(lane < N_HEADS, jnp.minimum(x, 0.0) - jnp.log(1.0 + jnp.exp(-jnp.abs(x))), 0.0)
        cum = jnp.dot(_tri(tb, True), logf, preferred_element_type=F32,
                      precision=lax.Precision.HIGHEST) + carry_ref[...]
        carry_ref[...] = cum[tb - 1:tb, :]
        cum_ref[...] = cum
        cumt_ref[...] = cum.T

    return _pcall(body, name="fox_cum_fwd", out_shape=(_sds((S, 128), F32), _sds((128, S), F32)),
                  grid=(S // tb,),
                  in_specs=[pl.BlockSpec((tb, 128), lambda i: (i, Z_FF // 128)),
                            pl.BlockSpec((1, 128), lambda i: (0, 0))],
                  out_specs=(pl.BlockSpec((tb, 128), lambda i: (i, 0)),
                             pl.BlockSpec((128, tb), lambda i: (0, i))),
                  scratch=(pltpu.VMEM((1, 128), F32),))(zs, bf)


def fox_cum_bwd(zs, bf, dcum):
    S = zs.shape[0]
    tb = min(CUM_BLK, S)
    nblk = S // tb

    def body(f_ref, b_ref, d_ref, df_ref, db_ref, carry_ref):
        @pl.when(pl.program_id(0) == 0)
        def _():
            carry_ref[...] = jnp.zeros_like(carry_ref)
            db_ref[...] = jnp.zeros_like(db_ref)

        d = d_ref[...]
        dlogf = jnp.dot(_tri(tb, False), d, preferred_element_type=F32,
                        precision=lax.Precision.HIGHEST) + carry_ref[...]
        carry_ref[...] += jnp.sum(d, axis=0, keepdims=True)
        x = f_ref[...] + b_ref[...]
        lane = lax.broadcasted_iota(jnp.int32, x.shape, 1)
        dx = jnp.where(lane < N_HEADS, dlogf / (1.0 + jnp.exp(x)), 0.0)
        df_ref[...] = dx
        db_ref[...] += jnp.sum(dx, axis=0, keepdims=True)

    return _pcall(body, name="fox_cum_bwd", out_shape=(_sds((S, 128), F32), _sds((1, 128), F32)),
                  grid=(nblk,),
                  in_specs=[pl.BlockSpec((tb, 128), lambda i: (nblk - 1 - i, Z_FF // 128)),
                            pl.BlockSpec((1, 128), lambda i: (0, 0)),
                            pl.BlockSpec((tb, 128), lambda i: (nblk - 1 - i, 0))],
                  out_specs=(pl.BlockSpec((tb, 128), lambda i: (nblk - 1 - i, 0)),
                             pl.BlockSpec((1, 128), lambda i: (0, 0))),
                  scratch=(pltpu.VMEM((1, 128), F32),))(zs, bf, dcum)


def _sigmoid(x):
    return 1.0 / (1.0 + jnp.exp(-x))


def merge_fwd(z, projs):
    S = z.shape[0]
    tr, tc = _pick(S, (512, 256, 128)), 512
    nbc = D_MODEL // tc
    g0 = Z_GATE // tc

    def body(g0_ref, g1_ref, g2_ref, p0_ref, p1_ref, p2_ref, o_ref):
        acc = _sigmoid(g0_ref[...]) * p0_ref[...]
        acc += _sigmoid(g1_ref[...]) * p1_ref[...]
        acc += _sigmoid(g2_ref[...]) * p2_ref[...]
        o_ref[...] = acc.astype(o_ref.dtype)

    gspecs = [pl.BlockSpec((tr, tc), functools.partial(lambda i, j, n: (i, g0 + n * nbc + j), n=n))
              for n in range(3)]
    pspec = pl.BlockSpec((tr, tc), lambda i, j: (i, j))
    return _pcall(body, name="merge_fwd", out_shape=_sds((S, D_MODEL), BF16), grid=(S // tr, nbc),
                  in_specs=gspecs + [pspec] * 3, out_specs=pspec)(z, z, z, *projs)


def merge_bwd(z, projs, dmerged):
    S = z.shape[0]
    tr, tc = _pick(S, (512, 256, 128)), 512
    nbc = D_MODEL // tc
    g0 = Z_GATE // tc

    def body(g0_ref, g1_ref, g2_ref, p0_ref, p1_ref, p2_ref, dm_ref, dg0, dg1, dg2, dp0, dp1, dp2):
        dm = dm_ref[...]
        for g_ref, p_ref, dg_ref, dp_ref in ((g0_ref, p0_ref, dg0, dp0), (g1_ref, p1_ref, dg1, dp1),
                                             (g2_ref, p2_ref, dg2, dp2)):
            sg = _sigmoid(g_ref[...])
            dp_ref[...] = (dm * sg).astype(dp_ref.dtype)
            dg_ref[...] = (dm * p_ref[...] * sg * (1.0 - sg)).astype(dg_ref.dtype)

    gspecs = [pl.BlockSpec((tr, tc), functools.partial(lambda i, j, n: (i, g0 + n * nbc + j), n=n))
              for n in range(3)]
    pspec = pl.BlockSpec((tr, tc), lambda i, j: (i, j))
    out = _pcall(body, name="merge_bwd", out_shape=tuple(_sds((S, D_MODEL), BF16) for _ in range(6)),
                 grid=(S // tr, nbc), in_specs=gspecs + [pspec] * 4,
                 out_specs=tuple([pspec] * 6))(z, z, z, *projs, dmerged)
    return out[:3], out[3:]


def loss_head(y, target):
    S, D = y.shape
    tr = _pick(S, (256, 128))

    def body(y_ref, t_ref, dy_ref, l_ref):
        e = y_ref[...] - t_ref[...]
        dy_ref[...] = e * (1.0 / D)
        part = jnp.sum(jnp.sum(e * e, axis=1, keepdims=True), axis=0, keepdims=True) * (0.5 / D)

        @pl.when(pl.program_id(0) == 0)
        def _():
            l_ref[...] = part

        @pl.when(pl.program_id(0) > 0)
        def _():
            l_ref[...] += part

    spec = pl.BlockSpec((tr, D), lambda i: (i, 0))
    return _pcall(body, name="loss_head", out_shape=(_sds((S, D), F32), _sds((1, 1), F32)), grid=(S // tr,),
                  in_specs=[spec, spec], out_specs=(spec, pl.BlockSpec((1, 1), lambda i: (0, 0))))(y, target)


ADAMW_BLOCK_BYTES = 1024 * 1024


def adamw(w, gs, m, v, *, name):
    L, R, C = w.shape
    tr = _pick(R, (512, 256, 128, 64, 32, 16, 8))
    while tr * C * 4 > ADAMW_BLOCK_BYTES and tr % 16 == 0:
        tr //= 2
    c1 = 1.0 - ADAM_B1 ** ADAM_STEP
    c2 = 1.0 - ADAM_B2 ** ADAM_STEP

    def body(*refs):
        w_ref, m_ref, v_ref = refs[:3]
        g_refs = refs[3:3 + L]
        go_ref, d_ref, nm_ref, nv_ref = refs[3 + L:]
        l = pl.program_id(0)
        g_ = g_refs[0][...]
        for k in range(1, L):
            g_ = jnp.where(l == k, g_refs[k][...], g_)
        nm = ADAM_B1 * m_ref[...] + (1.0 - ADAM_B1) * g_
        nv = ADAM_B2 * v_ref[...] + (1.0 - ADAM_B2) * (g_ * g_)
        go_ref[...] = g_
        nm_ref[...] = nm
        nv_ref[...] = nv
        d_ref[...] = -ADAM_LR * ((nm / c1) / (jnp.sqrt(nv / c2) + ADAM_EPS) + ADAM_WD * w_ref[...])

    spec = pl.BlockSpec((None, tr, C), lambda l, i: (l, i, 0))
    gspecs = [pl.BlockSpec((tr, C), functools.partial(lambda l, i, k: (jnp.where(l == k, i, 0), 0), k=k))
              for k in range(L)]
    return _pcall(body, name=name, out_shape=tuple(_sds((L, R, C), F32) for _ in range(4)),
                  grid=(L, R // tr), in_specs=[spec] * 3 + gspecs, out_specs=(spec,) * 4)(w, m, v, *gs)


SHARD_SHAPES = {
    "w_uq": (Q_LORA, 384), "w_ukv": (KV_LORA, 512), "w_br": (3, MIX_W, 512), "w_out": (512, D_MODEL),
    "w_xq": (512, 512), "w_xkv": (512, 1024), "w_xo": (512, 512), "w_1": (D_MODEL, 2048),
    "w_2": (2048, D_MODEL), "w_in": (D_MODEL, 3282),
}
def _rows2d(a, name):
    shp = SHARD_SHAPES[name]
    return a.reshape(a.shape[:a.ndim - len(shp)] + (-1, shp[-1]))


def _cols_from_shards(g):
    return jnp.transpose(g, (1, 0, 2)).reshape(g.shape[1], 4 * g.shape[2])


def _cols_to_shards(w):
    return jnp.transpose(w.reshape(w.shape[0], 4, w.shape[1] // 4), (1, 0, 2))


EARLY_WEIGHTS = ("w_in", "w_uq", "w_ukv")
LATE_WEIGHTS = ("w_br", "w_out", "w_xq", "w_xkv", "w_xo", "w_1", "w_2")
assert sorted(EARLY_WEIGHTS + LATE_WEIGHTS) == sorted(PACK_ORDER)


def _w_in_relaid(g_in):
    zeros = lambda n: [jnp.zeros((D_MODEL, n), g_in.dtype)] if n else []
    segs, at = [], 0
    for p0, o0, w in sorted(W_IN_PIECES):
        segs += zeros(p0 - at)
        while w > 0:
            s_, a = divmod(o0, W_IN_SHARD)
            take = min(w, W_IN_SHARD - a)
            segs.append(g_in[s_][:, a:a + take])
            o0, p0, w = o0 + take, p0 + take, w - take
        at = p0
    return jnp.concatenate(segs + zeros(Z_TOT - at), axis=1)


def full_weights(g):
    forms = {
        "w_in": lambda a: {"in_p": _w_in_relaid(a)[None]},
        "w_uq": lambda a: {"uq_p": jnp.pad(_cols_from_shards(a).reshape(Q_LORA, N_HEADS, MLA_QK),
                                           ((0, 0), (0, 0), (0, MLA_PAD - MLA_QK))
                                           ).reshape(1, Q_LORA, N_HEADS * MLA_PAD)},
        "w_ukv": lambda a: {"ukv": a},
        "w_br": lambda a: {"br": [a[:, n] for n in range(3)]},
        "w_out": lambda a: {"out": a.reshape(1, D_MODEL, D_MODEL)},
        "w_xq": lambda a: {"xq": a.reshape(1, D_MODEL, 512)},
        "w_xkv": lambda a: {"xkv": a.reshape(1, D_MODEL, 1024)},
        "w_xo": lambda a: {"xo": a},
        "w_1": lambda a: {"w1": a},
        "w_2": lambda a: {"w2": a.reshape(1, D_FF, D_MODEL)},
    }
    out = {}
    for n, a in g.items():
        out.update(forms[n](a))
    return out


def _w_in_grad_shards(dp):
    d_in = []
    for s_ in range(4):
        lo, hi, segs = s_ * W_IN_SHARD, (s_ + 1) * W_IN_SHARD, []
        for o0, p0, w in sorted((o0, p0, w) for p0, o0, w in W_IN_PIECES):
            a, b = max(lo, o0), min(hi, o0 + w)
            if a < b:
                segs.append(dp[:, p0 + a - o0:p0 + b - o0])
        d_in.append(jnp.concatenate(segs, axis=1))
    return jnp.stack(d_in)


def shard_grads(dw):
    forms = {
        "in_p": lambda a: {"w_in": _w_in_grad_shards(a[0])},
        "uq_p": lambda a: {"w_uq": _cols_to_shards(
            a.reshape(Q_LORA, N_HEADS, MLA_PAD)[:, :, :MLA_QK].reshape(Q_LORA, N_HEADS * MLA_QK))},
        "ukv": lambda a: {"w_ukv": a},
        "br": lambda a: {"w_br": jnp.stack(a, axis=1)},
        "out": lambda a: {"w_out": a.reshape(4, 512, D_MODEL)},
        "xq": lambda a: {"w_xq": a.reshape(4, 512, 512)},
        "xkv": lambda a: {"w_xkv": a.reshape(4, 512, 1024)},
        "xo": lambda a: {"w_xo": a},
        "w1": lambda a: {"w_1": a},
        "w2": lambda a: {"w_2": a.reshape(4, 2048, D_MODEL)},
    }
    out = {}
    for k, a in dw.items():
        out.update(forms[k](a))
    return out


def layer_params(d, l):
    row = lambda v: v.reshape(1, -1).astype(F32)
    padto = lambda v, n: jnp.pad(row(v), ((0, 0), (0, n - v.shape[-1])))
    return dict(
        g_mix=row(d["g_mix"][l]), g_cq=row(d["g_cq"][l]), g_ckv=row(d["g_ckv"][l]),
        g_mla_q=padto(d["g_mla_q"][l], MLA_PAD), g_mla_k=padto(d["g_mla_k"][l], MLA_PAD),
        b_f=padto(d["b_f"][l], 128), g_fox_q=row(d["g_fox_q"][l]), g_fox_k=row(d["g_fox_k"][l]),
        rel_bias=d["rel_bias"][l].astype(F32), g_ch_q=row(d["g_ch_q"][l]), g_ch_k=row(d["g_ch_k"][l]),
        g_cross=row(d["g_cross"][l]), g_mem=row(d["g_mem"][l]), g_x_q=row(d["g_x_q"][l]),
        g_x_k=row(d["g_x_k"][l]), g_mlp=row(d["g_mlp"][l]))


FOX_B = Z_FOX // 128
CH_B = Z_CH // 128


def _after(small, token):
    return small if token is None else small + token[:1, :1].reshape((1,) * small.ndim)


def layer_fwd(x, mem, W, P, rope, late_weights):
    S = x.shape[0]
    s = {}
    s["h"] = rms_fwd(x, P["g_mix"], name="rms_d")
    z = s["z"] = mm_nn(s["h"], W["in_p"], name="mm_in", out_dtype=F32)
    s["cq_n"] = rms_fwd(z, P["g_cq"], col0=Z_CQ, width=Q_LORA, name="rms_cq")
    s["ckv_n"] = rms_fwd(z, P["g_ckv"], col0=Z_CKV, width=KV_LORA, name="rms_ckv")
    s["q_raw"] = mm_nn(s["cq_n"], W["uq_p"], name="mm_uq", out_dtype=F32)
    s["kv_raw"] = mm_nn(s["ckv_n"], W["ukv"], name="mm_ukv", out_dtype=F32)
    s["qa"] = prep_fwd([(s["q_raw"], MLA_PAD, lambda h: h)], P["g_mla_q"], name="prep_mla_q",
                       n_heads=N_HEADS, n_real=MLA_QK, rope=rope)
    s["ka"] = prep_fwd([(s["kv_raw"], 128, lambda h: 2 * h), (z, 128, lambda h: Z_KR // 128)], P["g_mla_k"],
                       name="prep_mla_k", n_heads=N_HEADS, n_real=MLA_QK, rope=rope)
    s["o_a"], s["lse_a"] = attn_fwd(s["qa"], s["ka"], s["kv_raw"], lambda h: 2 * h + 1, mode="mla",
                                    name="attn_mla", n_heads=N_HEADS)
    s["qb"] = prep_fwd([(z, DH, lambda h: FOX_B + h)], P["g_fox_q"], name="prep_h", n_heads=N_HEADS, n_real=DH)
    s["kb"] = prep_fwd([(z, DH, lambda h: FOX_B + N_HEADS + h)], P["g_fox_k"], name="prep_h",
                       n_heads=N_HEADS, n_real=DH)
    _, cum_t = fox_cum_fwd(z, P["b_f"])
    s["cq"] = cum_t[:N_HEADS].reshape(N_HEADS, S, 1)
    s["ck"] = cum_t[:N_HEADS].reshape(N_HEADS, 1, S)
    s["o_b"], s["lse_b"] = attn_fwd(s["qb"], s["kb"], z, lambda h: FOX_B + 2 * N_HEADS + h, mode="fox",
                                    name="attn_fox", n_heads=N_HEADS, cq=s["cq"], ck=s["ck"])
    s["qc"] = prep_fwd([(z, DH, lambda h: CH_B + h)], P["g_ch_q"], name="prep_h", n_heads=N_HEADS, n_real=DH)
    s["kc"] = prep_fwd([(z, DH, lambda h: CH_B + N_HEADS + h)], P["g_ch_k"], name="prep_h",
                       n_heads=N_HEADS, n_real=DH)
    s["tiles"] = relbias_tiles(P["rel_bias"], P["rel_onehot"])
    s["o_c"], s["lse_c"] = attn_fwd(s["qc"], s["kc"], z, lambda h: CH_B + 2 * N_HEADS + h, mode="chunk",
                                    name="attn_chunk", n_heads=N_HEADS, tiles=s["tiles"])
    late, token = late_weights(s["o_c"])
    W = dict(W, **late)
    P = dict(P, g_cross=_after(P["g_cross"], token))
    s["projs"] = [mm_nn(o, W["br"][n], name="mm_br", out_dtype=F32)
                  for n, o in enumerate((s["o_a"], s["o_b"], s["o_c"]))]
    s["merged"] = merge_fwd(z, s["projs"])
    x1 = s["x1"] = mm_nn(s["merged"], W["out"], name="mm_out", out_dtype=F32, res=x)
    s["hq"] = rms_fwd(x1, P["g_cross"], name="rms_d")
    s["xq_raw"] = mm_nn(s["hq"], W["xq"], name="mm_xq", out_dtype=F32)
    s["mem_n"] = rms_fwd(mem, P["g_mem"], name="rms_mem")
    s["mkv"] = mm_nn(s["mem_n"], W["xkv"], name="mm_xkv", out_dtype=F32)
    s["qx"] = prep_fwd([(s["xq_raw"], DH, lambda h: h)], P["g_x_q"], name="prep_xq", n_heads=X_HEADS, n_real=DH)
    s["kx"] = prep_fwd([(s["mkv"], DH, lambda h: h)], P["g_x_k"], name="prep_xk", n_heads=X_HEADS, n_real=DH)
    s["o_x"], s["lse_x"] = attn_fwd(s["qx"], s["kx"], s["mkv"], lambda h: X_HEADS + h, mode="cross",
                                    name="attn_cross", n_heads=X_HEADS)
    x2 = s["x2"] = mm_nn(s["o_x"], W["xo"], name="mm_xo", out_dtype=F32, res=x1)
    s["hm"] = rms_fwd(x2, P["g_mlp"], name="rms_d")
    s["a1"], s["act"] = mm_nn(s["hm"], W["w1"], name="mm_w1", out_dtype=BF16, relu2=True)
    x3 = mm_nn(s["act"], W["w2"], name="mm_w2", out_dtype=F32, res=x2)
    return x3, s, W


LATE_GRADS = ("w2", "w1", "xo", "xq", "xkv", "out", "br")


def layer_bwd(g, x, mem, W, P, rope, s, late_grads_done):
    S = x.shape[0]
    z = s["z"]
    dw, ds = {}, {}
    da1 = mm_nt(g, W["w2"], name="mm_w2_dx", out_dtype=BF16, relu_mul=s["a1"])
    dw["w2"] = mm_tn(s["act"], g, nb=1, name="mm_w2_dw", out_dtype=BF16)
    dhm = mm_nt(da1, W["w1"], name="mm_w1_dx", out_dtype=F32)
    dw["w1"] = mm_tn(s["hm"], da1, nb=4, name="mm_w1_dw", out_dtype=BF16)
    g2, ds["g_mlp"] = rms_bwd(s["x2"], P["g_mlp"], dhm, name="rms_d_bwd", res=g)
    do_x = mm_nt(g2, W["xo"], name="mm_xo_dx", out_dtype=BF16)
    dw["xo"] = mm_tn(s["o_x"], g2, nb=4, name="mm_xo_dw", out_dtype=BF16)
    dqx, dkx, dvx = attn_bwd(s["qx"], s["kx"], s["mkv"], lambda h: X_HEADS + h, s["o_x"], do_x, s["lse_x"],
                             mode="cross", name="attn_cross_bwd", n_heads=X_HEADS)
    dxq_raw, ds["g_x_q"] = prep_bwd_q(s["xq_raw"], dqx, P["g_x_q"], name="prep_xq_bwd", n_heads=X_HEADS,
                                      dh=DH, n_real=DH)
    dmkv, dgk = prep_bwd_groups(s["mkv"], 0, [dkx, dvx], [P["g_x_k"], None], name="prep_xkv_bwd",
                                n_heads=X_HEADS, kinds=("norm", "copy"))
    ds["g_x_k"] = dgk[0]
    dhq = mm_nt(dxq_raw, W["xq"], name="mm_xq_dx", out_dtype=F32)
    dw["xq"] = mm_tn(s["hq"], dxq_raw, nb=1, name="mm_xq_dw", out_dtype=BF16)
    dmem_n = mm_nt(dmkv, W["xkv"], name="mm_xkv_dx", out_dtype=F32)
    dw["xkv"] = mm_tn(s["mem_n"], dmkv, nb=1, name="mm_xkv_dw", out_dtype=BF16)
    _, ds["g_mem"] = rms_bwd(mem, P["g_mem"], dmem_n, name="rms_mem_bwd", need_dx=False)
    g1, ds["g_cross"] = rms_bwd(s["x1"], P["g_cross"], dhq, name="rms_d_bwd", res=g2)
    dmerged = mm_nt(g1, W["out"], name="mm_out_dx", out_dtype=F32)
    dw["out"] = mm_tn(s["merged"], g1, nb=1, name="mm_out_dw", out_dtype=BF16)
    dgl, dproj = merge_bwd(z, s["projs"], dmerged)
    outs = (s["o_a"], s["o_b"], s["o_c"])
    do = [mm_nt(dproj[n], W["br"][n], name="mm_br_dx", out_dtype=BF16) for n in range(3)]
    dw["br"] = [mm_tn(outs[n], dproj[n], nb=4, name="mm_br_dw", out_dtype=BF16) for n in range(3)]
    token = late_grads_done({k: dw.pop(k) for k in LATE_GRADS})
    dqc, dkc, dvc, dtiles = attn_bwd(s["qc"], s["kc"], z, lambda h: CH_B + 2 * N_HEADS + h, s["o_c"], do[2],
                                     s["lse_c"], mode="chunk", name="attn_chunk_bwd", n_heads=N_HEADS,
                                     tiles=_after(s["tiles"], token))
    d_ch, dg_ch = prep_bwd_groups(z, CH_B, [dqc, dkc, dvc], [P["g_ch_q"], P["g_ch_k"], None],
                                  name="prep_h_bwd", n_heads=N_HEADS, kinds=("norm", "norm", "copy"))
    ds["g_ch_q"], ds["g_ch_k"] = dg_ch[0], dg_ch[1]
    ds["rel_bias"] = relbias_tiles_bwd(dtiles, P["rel_onehot"])
    dqb, dkb, dvb, dcq, dck = attn_bwd(s["qb"], s["kb"], z, lambda h: FOX_B + 2 * N_HEADS + h, s["o_b"], do[1],
                                       s["lse_b"], mode="fox", name="attn_fox_bwd", n_heads=N_HEADS,
                                       cq=s["cq"], ck=s["ck"])
    d_fox, dg_fox = prep_bwd_groups(z, FOX_B, [dqb, dkb, dvb], [P["g_fox_q"], P["g_fox_k"], None],
                                    name="prep_h_bwd", n_heads=N_HEADS, kinds=("norm", "norm", "copy"))
    ds["g_fox_q"], ds["g_fox_k"] = dg_fox[0], dg_fox[1]
    dcum = jnp.pad((dcq[:, :, 0] + dck[:, 0, :]).T, ((0, 0), (0, 128 - N_HEADS)))
    dff, dbf = fox_cum_bwd(z, P["b_f"], dcum)
    ds["b_f"] = dbf[:, :N_HEADS]
    dqa, dka, dva = attn_bwd(s["qa"], s["ka"], s["kv_raw"], lambda h: 2 * h + 1, s["o_a"], do[0], s["lse_a"],
                             mode="mla", name="attn_mla_bwd", n_heads=N_HEADS)
    dq_raw, dgq = prep_bwd_q(s["q_raw"], dqa, P["g_mla_q"], name="prep_mla_q_bwd", n_heads=N_HEADS,
                             dh=MLA_PAD, n_real=MLA_QK, rope=rope)
    dkv_raw, dkr, dgk = prep_bwd_mla_k(s["kv_raw"], z, dka, dva, P["g_mla_k"], rope, name="prep_mla_k_bwd")
    ds["g_mla_q"], ds["g_mla_k"] = dgq[:, :MLA_QK], dgk[:, :MLA_QK]
    dcq_n = mm_nt(dq_raw, W["uq_p"], name="mm_uq_dx", out_dtype=F32)
    dw["uq_p"] = mm_tn(s["cq_n"], dq_raw, nb=1, name="mm_uq_dw", out_dtype=BF16)
    dckv_n = mm_nt(dkv_raw, W["ukv"], name="mm_ukv_dx", out_dtype=F32)
    dw["ukv"] = mm_tn(s["ckv_n"], dkv_raw, nb=4, name="mm_ukv_dw", out_dtype=BF16)
    d_cq, ds["g_cq"] = rms_bwd(z, P["g_cq"], dcq_n, name="rms_cq_bwd", col0=Z_CQ, width=Q_LORA, dx_dtype=BF16)
    d_ckv, ds["g_ckv"] = rms_bwd(z, P["g_ckv"], dckv_n, name="rms_ckv_bwd", col0=Z_CKV, width=KV_LORA,
                                 dx_dtype=BF16)
    dz = jnp.concatenate([d_cq, d_ckv, dkr.astype(BF16), dff.astype(BF16), d_fox, d_ch, *dgl], axis=1)
    dh = mm_nt(dz, W["in_p"], name="mm_in_dx", out_dtype=F32)
    dw["in_p"] = mm_tn(s["h"], dz, nb=1, name="mm_in_dw", out_dtype=BF16)
    g0, ds["g_mix"] = rms_bwd(x, P["g_mix"], dh, name="rms_d_bwd", res=g1)
    return g0, dw, ds


def local_step(x, mem, target, Ps, weights_of, grads_done):
    rope = rope_tables(x.shape[0])
    onehot = relbias_onehot()
    Ps = [dict(P, rel_onehot=onehot) for P in Ps]
    Ws, saved, xs = [], [], [x]
    for l, P in enumerate(Ps):
        W_early, late_weights, token = weights_of(l, xs[-1])
        y, s, W = layer_fwd(xs[-1], mem, W_early, dict(P, g_mix=_after(P["g_mix"], token)), rope, late_weights)
        xs.append(y)
        saved.append(s)
        Ws.append(W)
    g, loss = loss_head(xs[-1], target)
    dss, token = [], None
    for l in reversed(range(len(Ps))):
        g, dw, ds = layer_bwd(g, xs[l], mem, Ws[l], dict(Ps[l], g_mlp=_after(Ps[l]["g_mlp"], token)), rope,
                              saved[l], functools.partial(grads_done, l, "late"))
        token = grads_done(l, "early", dw)
        dss.append(ds)
    return loss, g, dss[::-1]


HBM_SPEC = pl.BlockSpec(memory_space=pltpu.HBM)


def _place():
    return lax.axis_index("x"), lax.axis_index("y"), lax.axis_index("c")


def _other_chips(x, y):
    return [(1 - x, y), (x, 1 - y), (1 - x, 1 - y)]


def _remote(src, dst, send_sems, recv_sems, k, to):
    return pltpu.make_async_remote_copy(src_ref=src, dst_ref=dst, send_sem=send_sems.at[k],
                                        recv_sem=recv_sems.at[k], device_id=to, device_id_type=MESH)


def _comm_call(body, *, name, ins, out_shapes, n_sem):
    return pl.pallas_call(
        body, out_shape=tuple(out_shapes), in_specs=[HBM_SPEC] * len(ins),
        out_specs=tuple([HBM_SPEC] * len(out_shapes)),
        scratch_shapes=[pltpu.SemaphoreType.DMA((n_sem,)), pltpu.SemaphoreType.DMA((n_sem,))],
        name=name, interpret=False)(*ins)


SEM_SPEC = pl.BlockSpec(memory_space=pltpu.SEMAPHORE)
SPLIT_EFFECT = pltpu.SideEffectType.DATAFLOW_SIDE_EFFECTING


def _chip_exchange_copies(kind, srcs, lands, send_sems, recv_sems):
    x, y, c = _place()
    me = 2 * x + y
    cps = []
    for i, (src, land) in enumerate(zip(srcs, lands)):
        for j, (cx, cy) in enumerate(_other_chips(x, y)):
            if kind == "gather":
                h = src.shape[0] // 2
                s_ref, d_ref = src.at[pl.ds(c * h, h), :], land.at[me, pl.ds(c * h, h), :]
            else:
                s_ref, d_ref = src.at[2 * cx + cy], land.at[me]
            cps.append(_remote(s_ref, d_ref, send_sems, recv_sems, 3 * i + j, (cx, cy, c)))
    return cps


def chip_exchange_start(kind, srcs, land_shapes, *, name, after=None):
    n = len(srcs)
    n_in = 2 * n + (after is not None)

    def body(*refs):
        send_sems, recv_sems = refs[n_in], refs[n_in + 1]
        for cp in _chip_exchange_copies(kind, refs[:n], refs[n:2 * n], send_sems, recv_sems):
            cp.start()
        refs[-1][...] = jnp.zeros_like(refs[-1])

    hbm = lambda a: pltpu.with_memory_space_constraint(a, pltpu.HBM)
    ins = [hbm(s) for s in srcs] + [hbm(lax.empty(s.shape, s.dtype)) for s in land_shapes]
    sems = pltpu.SemaphoreType.DMA((3 * n,))
    out = pl.pallas_call(
        body, name=name, interpret=False,
        out_shape=(sems, sems, *[pltpu.HBM(a.shape, a.dtype) for a in ins], _sds((8, 128), F32)),
        in_specs=[HBM_SPEC] * (2 * n) + [pl.BlockSpec(memory_space=pl.ANY)] * (after is not None),
        out_specs=(SEM_SPEC, SEM_SPEC, *[HBM_SPEC] * (2 * n), pl.BlockSpec(memory_space=pltpu.VMEM)),
        input_output_aliases={i: 2 + i for i in range(2 * n)},
        compiler_params=pltpu.CompilerParams(has_side_effects=SPLIT_EFFECT))(
            *ins, *([after] if after is not None else []))
    return out[0], out[1], list(out[2:2 + n]), list(out[2 + n:2 + 2 * n]), out[-1]


def chip_exchange_wait(kind, started, after, *, name):
    send_sems, recv_sems, srcs, lands, _ = started
    n = len(srcs)

    def body(*refs):
        send_sems, recv_sems = refs[2 * n], refs[2 * n + 1]
        for cp in _chip_exchange_copies(kind, refs[:n], refs[n:2 * n], send_sems, recv_sems):
            cp.wait_send()
            cp.wait_recv()

    thru = srcs + lands
    out = pl.pallas_call(
        body, name=name, interpret=False,
        out_shape=tuple(pltpu.HBM(a.shape, a.dtype) for a in thru),
        in_specs=[HBM_SPEC] * (2 * n) + [SEM_SPEC, SEM_SPEC, pl.BlockSpec(memory_space=pl.ANY)],
        out_specs=tuple([HBM_SPEC] * (2 * n)),
        input_output_aliases={i: i for i in range(2 * n)},
        compiler_params=pltpu.CompilerParams(has_side_effects=SPLIT_EFFECT))(*thru, send_sems, recv_sems, after)
    return list(out[:n]), list(out[n:])


def gather_to_sibling(lands):
    n = len(lands)

    def body(*refs):
        outs = refs[n:2 * n]
        send_sems, recv_sems = refs[2 * n:]
        x, y, c = _place()
        sends = []
        for i in range(n):
            h = outs[i].shape[1] // 2
            for j, (cx, cy) in enumerate(_other_chips(x, y)):
                landed = outs[i].at[2 * cx + cy, pl.ds(c * h, h), :]
                cp = _remote(landed, landed, send_sems, recv_sems, 3 * i + j, (x, y, 1 - c))
                cp.start()
                sends.append(cp)
        for cp in sends:
            cp.wait()

    return pl.pallas_call(
        body, out_shape=tuple(_sds(a.shape, a.dtype) for a in lands), in_specs=[HBM_SPEC] * n,
        out_specs=tuple([HBM_SPEC] * n), input_output_aliases={i: i for i in range(n)},
        scratch_shapes=[pltpu.SemaphoreType.DMA((3 * n,)), pltpu.SemaphoreType.DMA((3 * n,))],
        name="gather_to_sibling", interpret=False)(*lands)


def rs_to_sibling(parts):
    n = len(parts)

    def body(*refs):
        srcs, outs = refs[:n], refs[n:2 * n]
        send_sems, recv_sems = refs[2 * n:]
        x, y, c = _place()
        sends = []
        for i in range(n):
            h = srcs[i].shape[1] // 2
            cp = _remote(srcs[i].at[:, pl.ds((1 - c) * h, h), :], outs[i], send_sems, recv_sems, i, (x, y, 1 - c))
            cp.start()
            sends.append(cp)
        for cp in sends:
            cp.wait()

    return _comm_call(body, name="rs_to_sibling", ins=parts, n_sem=n,
                      out_shapes=[_sds((4, p.shape[1] // 2, p.shape[2]), p.dtype) for p in parts])


def rs_share_halves(ts):
    n = len(ts)

    def body(*refs):
        srcs, outs = refs[:n], refs[n:2 * n]
        send_sems, recv_sems = refs[2 * n:]
        x, y, c = _place()
        sends = []
        for i in range(n):
            cp = _remote(srcs[i], outs[i].at[c], send_sems, recv_sems, i, (x, y, 1 - c))
            cp.start()
            sends.append(cp)
        for cp in sends:
            cp.wait()

    return _comm_call(body, name="rs_share_halves", ins=ts, n_sem=n,
                      out_shapes=[_sds((2,) + t.shape, t.dtype) for t in ts])


def add_pair(part, got, half_idx):
    _, a, b = part.shape
    h = a // 2
    tr = _pick(h, (512, 256, 128))

    def body(c_ref, p_ref, g_ref, o_ref):
        o_ref[...] = (p_ref[...].astype(F32) + g_ref[...].astype(F32)).astype(o_ref.dtype)

    spec = pl.BlockSpec((None, tr, b), lambda s_, i, c_ref: (s_, i, 0))
    grid_spec = pltpu.PrefetchScalarGridSpec(
        num_scalar_prefetch=1, grid=(4, h // tr),
        in_specs=[pl.BlockSpec((None, None, tr, b), lambda s_, i, c_ref: (s_, c_ref[0], i, 0)), spec],
        out_specs=spec)
    return pl.pallas_call(
        body, out_shape=_sds((4, h, b), part.dtype), grid_spec=grid_spec, name="rs_add_pair", interpret=False,
        compiler_params=pltpu.CompilerParams(vmem_limit_bytes=VMEM_LIMIT_BYTES))(
            half_idx, part.reshape(4, 2, h, b), got)


def sum_slots(r):
    _, h, b = r.shape
    tr = _pick(h, (512, 256, 128))

    def body(r0, r1, r2, r3, o_ref):
        o_ref[...] = ((r0[...].astype(F32) + r1[...].astype(F32)) + r2[...].astype(F32)) + r3[...].astype(F32)

    specs = [pl.BlockSpec((None, tr, b), functools.partial(lambda i, s_: (s_, i, 0), s_=s_)) for s_ in range(4)]
    return _pcall(body, name="rs_sum_slots", out_shape=_sds((h, b), F32), grid=(h // tr,),
                  in_specs=specs, out_specs=pl.BlockSpec((tr, b), lambda i: (i, 0)))(r, r, r, r)


def reduce_scatter_start(parts, core, *, name):
    half_idx = core.reshape(1).astype(jnp.int32)
    gots = rs_to_sibling(parts)
    chip_sums = [add_pair(p, g, half_idx) for p, g in zip(parts, gots)]
    return chip_exchange_start("scatter", chip_sums, chip_sums, name=name)


def reduce_scatter_finish(started, after, chip, core, *, name):
    chip_sums, slots = chip_exchange_wait("scatter", started, after, name=name)
    slots = [lax.dynamic_update_slice(s_, lax.dynamic_index_in_dim(q, chip, 0, keepdims=True), (chip, 0, 0))
             for s_, q in zip(slots, chip_sums)]
    halves = [sum_slots(s_) for s_ in slots]
    both = rs_share_halves(halves)
    both = [lax.dynamic_update_slice(o, t[None], (core, 0, 0)) for o, t in zip(both, halves)]
    return [o.reshape(2 * o.shape[1], o.shape[2]) for o in both]


def allreduce_small(v):
    Rs = v.shape[0]

    def body(v_ref, o_ref, buf, send_sems, recv_sems):
        x, y, c = _place()
        me = 4 * x + 2 * y + c
        buf[me] = v_ref[...]
        flips = [(fx, fy, fc) for fx in (0, 1) for fy in (0, 1) for fc in (0, 1)][1:]
        sends = []
        for k, (fx, fy, fc) in enumerate(flips):
            to = ((1 - x) if fx else x, (1 - y) if fy else y, (1 - c) if fc else c)
            cp = _remote(v_ref, buf.at[me], send_sems, recv_sems, k, to)
            cp.start()
            sends.append(cp)
        for cp in sends:
            cp.wait()
        acc = buf[0]
        for d in range(1, 8):
            acc = acc + buf[d]
        o_ref[...] = acc

    vm = pl.BlockSpec(memory_space=pltpu.VMEM)
    return pl.pallas_call(
        body, out_shape=_sds((Rs, 128), F32), in_specs=[vm], out_specs=vm,
        scratch_shapes=[pltpu.VMEM((8, Rs, 128), F32), pltpu.SemaphoreType.DMA((7,)),
                        pltpu.SemaphoreType.DMA((7,))],
        name="allreduce_small", interpret=False)(v)


INPUT_NAMES = (("x", "mem") + WEIGHT_ORDER + ("loss_target",) + tuple("m_" + n for n in WEIGHT_ORDER)
               + tuple("v_" + n for n in WEIGHT_ORDER))


def _pack_small(vals, n_layers, extra=None):
    flat = jnp.concatenate([vals[n].reshape(n_layers, -1).astype(F32) for n in SMALL_ORDER], axis=1).reshape(-1)
    if extra is not None:
        flat = jnp.concatenate([flat, extra.reshape(-1)])
    n = flat.shape[0]
    rows = -(-n // 1024) * 8
    return jnp.pad(flat, (0, rows * 128 - n)).reshape(rows, 128)


def _unpack_small(packed, like, n_layers):
    per_layer = sum(int(np.prod(like[n].shape[1:])) for n in SMALL_ORDER)
    body = packed.reshape(-1)[:n_layers * per_layer].reshape(n_layers, per_layer)
    out, off = {}, 0
    for n in SMALL_ORDER:
        k = int(np.prod(like[n].shape[1:]))
        out[n] = body[:, off:off + k].reshape(like[n].shape)
        off += k
    return out, packed.reshape(-1)[n_layers * per_layer]


def kernel(x, mem, g_mix, w_in, g_cq, w_uq, g_ckv, w_ukv, g_mla_q, g_mla_k, b_f, g_fox_q, g_fox_k, rel_bias, g_ch_q, g_ch_k, w_br, w_out, g_cross, g_mem, w_xq, w_xkv, g_x_q, g_x_k, w_xo, g_mlp, w_1, w_2, loss_target, m_g_mix, m_w_in, m_g_cq, m_w_uq, m_g_ckv, m_w_ukv, m_g_mla_q, m_g_mla_k, m_b_f, m_g_fox_q, m_g_fox_k, m_rel_bias, m_g_ch_q, m_g_ch_k, m_w_br, m_w_out, m_g_cross, m_g_mem, m_w_xq, m_w_xkv, m_g_x_q, m_g_x_k, m_w_xo, m_g_mlp, m_w_1, m_w_2, v_g_mix, v_w_in, v_g_cq, v_w_uq, v_g_ckv, v_w_ukv, v_g_mla_q, v_g_mla_k, v_b_f, v_g_fox_q, v_g_fox_k, v_rel_bias, v_g_ch_q, v_g_ch_k, v_w_br, v_w_out, v_g_cross, v_g_mem, v_w_xq, v_w_xkv, v_g_x_q, v_g_x_k, v_w_xo, v_g_mlp, v_w_1, v_w_2):
    d = dict(zip(INPUT_NAMES, (x, mem, g_mix, w_in, g_cq, w_uq, g_ckv, w_ukv, g_mla_q, g_mla_k, b_f, g_fox_q, g_fox_k, rel_bias, g_ch_q, g_ch_k, w_br, w_out, g_cross, g_mem, w_xq, w_xkv, g_x_q, g_x_k, w_xo, g_mlp, w_1, w_2, loss_target, m_g_mix, m_w_in, m_g_cq, m_w_uq, m_g_ckv, m_w_ukv, m_g_mla_q, m_g_mla_k, m_b_f, m_g_fox_q, m_g_fox_k, m_rel_bias, m_g_ch_q, m_g_ch_k, m_w_br, m_w_out, m_g_cross, m_g_mem, m_w_xq, m_w_xkv, m_g_x_q, m_g_x_k, m_w_xo, m_g_mlp, m_w_1, m_w_2, v_g_mix, v_w_in, v_g_cq, v_w_uq, v_g_ckv, v_w_ukv, v_g_mla_q, v_g_mla_k, v_b_f, v_g_fox_q, v_g_fox_k, v_rel_bias, v_g_ch_q, v_g_ch_k, v_w_br, v_w_out, v_g_cross, v_g_mem, v_w_xq, v_w_xkv, v_g_x_q, v_g_x_k, v_w_xo, v_g_mlp, v_w_1, v_w_2)))
    n_layers = g_mix.shape[0]
    assert x.shape[0] == 1 and x.shape[2] == D_MODEL and mem.shape[1:] == (MEM_LEN, D_MODEL)
    for n in PACK_ORDER:
        assert d[n].shape[1:] == SHARD_SHAPES[n], (n, d[n].shape)

    chip = 2 * lax.axis_index("x") + lax.axis_index("y")
    core = lax.axis_index("c")

    def start_gather(l, names, after):
        mine = [_rows2d(d[n][l].astype(BF16), n) for n in names]
        return chip_exchange_start("gather", mine, [_sds((4,) + m_.shape, BF16) for m_ in mine],
                                   name=f"gather_start_{l}_{names[0]}", after=after)

    def finish_gather(l, names, started, after):
        mine, lands = chip_exchange_wait("gather", started, after, name=f"gather_wait_{l}_{names[0]}")
        full = [lax.dynamic_update_slice(t, m_[None], (chip, 0, 0)) for t, m_ in zip(gather_to_sibling(lands), mine)]
        return mine, full_weights({n: f.reshape((4,) + SHARD_SHAPES[n]) for n, f in zip(names, full)})

    in_flight = {"early": start_gather(0, EARLY_WEIGHTS, x)}

    def weights_of(l, x_l):
        mine, early = finish_gather(l, EARLY_WEIGHTS, in_flight.pop("early"), x_l)
        in_flight["late"] = start_gather(l, LATE_WEIGHTS, mine[0])

        def late_weights(after):
            mine, late = finish_gather(l, LATE_WEIGHTS, in_flight.pop("late"), after)
            if l + 1 == n_layers:
                return late, None
            in_flight["early"] = start_gather(l + 1, EARLY_WEIGHTS, mine[0])
            return late, in_flight["early"][4]

        return early, late_weights, in_flight["late"][4]

    scatters = []

    def grads_done(l, group, dw):
        names = LATE_WEIGHTS if group == "late" else EARLY_WEIGHTS
        sg = shard_grads(dw)
        started = reduce_scatter_start([_rows2d(sg[n], n) for n in names], core, name=f"scatter_start_{l}_{names[0]}")
        scatters.append((l, names, started))
        return started[4]

    Ps = [layer_params(d, l) for l in range(n_layers)]
    loss, dx, dss = local_step(x[0], mem[0], loss_target[0], Ps, weights_of, grads_done)
    big = [{} for _ in range(n_layers)]
    for l, names, started in scatters:
        done = reduce_scatter_finish(started, dx, chip, core, name=f"scatter_wait_{l}_{names[0]}")
        big[l].update(zip(names, done))

    small_local = {n: jnp.stack([dss[l][n].reshape(d[n].shape[1:]) for l in range(n_layers)]) for n in SMALL_ORDER}
    small_sum, loss_sum = _unpack_small(allreduce_small(_pack_small(small_local, n_layers, extra=loss)),
                                        {n: d[n] for n in SMALL_ORDER}, n_layers)

    grads, delta, new_m, new_v = {}, {}, {}, {}
    for n in PACK_ORDER:
        outs = adamw(_rows2d(d[n], n), [big[l][n] for l in range(n_layers)], _rows2d(d["m_" + n], n),
                     _rows2d(d["v_" + n], n), name="adamw_" + n)
        grads[n], delta[n], new_m[n], new_v[n] = (o.reshape(d[n].shape) for o in outs)
    like = {n: d[n] for n in SMALL_ORDER}
    sm = adamw(_pack_small(like, n_layers)[None], [_pack_small(small_sum, n_layers)],
               _pack_small({n: d["m_" + n] for n in SMALL_ORDER}, n_layers)[None],
               _pack_small({n: d["v_" + n] for n in SMALL_ORDER}, n_layers)[None], name="adamw_small")
    for res, src in zip((grads, delta, new_m, new_v), sm):
        res.update(_unpack_small(src[0], like, n_layers)[0])

    return (loss_sum, dx[None], *[grads[n] for n in WEIGHT_ORDER], *[delta[n] for n in WEIGHT_ORDER],
            *[new_m[n] for n in WEIGHT_ORDER], *[new_v[n] for n in WEIGHT_ORDER])
```

```python
import functools

import numpy as np
import jax
import jax.numpy as jnp
from jax import lax
from jax.experimental import pallas as pl
from jax.experimental.pallas import tpu as pltpu

F32 = jnp.float32
BF16 = jnp.bfloat16
MXU_DTYPE = jnp.bfloat16
MESH = pl.DeviceIdType.MESH

D_MODEL = 2048
MIX_W = 1024
N_HEADS = 8
DH = 128
MLA_NOPE = 128
MLA_ROPE = 64
MLA_QK = MLA_NOPE + MLA_ROPE
MLA_PAD = 256
Q_LORA = 512
KV_LORA = 256
CHUNK = 64
CHUNK_SHIFT = CHUNK.bit_length() - 1
LEFT_CHUNKS = 8
REL_CLIP = 128
N_REL = 2 * REL_CLIP + 1
X_HEADS = 4
MEM_LEN = 256
D_FF = 8192
ROPE_THETA = 10000.0
EPS = 1e-6
NEG = -1e30

Z_CQ, Z_CKV, Z_KR, Z_FF = 0, 512, 768, 896
ZS_W = 1024
Z_FOX = 1024
Z_CH = Z_FOX + 3 * MIX_W
Z_GATE = Z_CH + 3 * MIX_W
Z_TOT = Z_GATE + 3 * D_MODEL
W_IN_CUTS = (0, 512, 768, 832, 3904, 3912, 6984, 13128)
W_IN_SHARD = W_IN_CUTS[-1] // 4
W_IN_PIECES = ((0, 0, 832), (Z_FF, 3904, 8), (Z_FOX, 832, 3072), (Z_CH, 3912, 9216))

ADAM_LR, ADAM_B1, ADAM_B2, ADAM_EPS, ADAM_WD, ADAM_STEP = 0.001, 0.9, 0.999, 1e-08, 0.01, 10

VMEM_LIMIT_BYTES = 56 * 1024 * 1024
PACK_ORDER = ("w_uq", "w_ukv", "w_br", "w_out", "w_xq", "w_xkv", "w_xo", "w_1", "w_2", "w_in")
SMALL_ORDER = ("g_mix", "g_cq", "g_ckv", "g_mla_q", "g_mla_k", "b_f", "g_fox_q", "g_fox_k", "rel_bias",
               "g_ch_q", "g_ch_k", "g_cross", "g_mem", "g_x_q", "g_x_k", "g_mlp")
WEIGHT_ORDER = ("g_mix", "w_in", "g_cq", "w_uq", "g_ckv", "w_ukv", "g_mla_q", "g_mla_k", "b_f", "g_fox_q",
                "g_fox_k", "rel_bias", "g_ch_q", "g_ch_k", "w_br", "w_out", "g_cross", "g_mem", "w_xq",
                "w_xkv", "g_x_q", "g_x_k", "w_xo", "g_mlp", "w_1", "w_2")


def _pick(n, prefs):
    for p in prefs:
        if n % p == 0:
            return p
    raise ValueError(f"no block size among {prefs} divides {n}")


def _pcall(body, *, name, out_shape, in_specs, out_specs, grid=(), scratch=(), aliases=None):
    return pl.pallas_call(
        body, out_shape=out_shape, grid=grid, in_specs=in_specs, out_specs=out_specs,
        scratch_shapes=scratch, name=name, interpret=False,
        input_output_aliases=aliases or {},
        compiler_params=pltpu.CompilerParams(vmem_limit_bytes=VMEM_LIMIT_BYTES))


def _sds(shape, dtype):
    return jax.ShapeDtypeStruct(tuple(shape), dtype)


def _mx(v):
    return v.astype(MXU_DTYPE)


def mm_nn(a, b3, *, name, out_dtype, a_col0=0, res=None, relu2=False):
    M = a.shape[0]
    nb, K, Ns = b3.shape
    N = nb * Ns
    tm = _pick(M, (1024, 512, 256, 128))
    tk = _pick(K, (2048, 1024, 512, 256))
    tn = _pick(Ns, (512, 256, 128))
    assert a_col0 % tk == 0
    nk, nbs, ka0 = K // tk, Ns // tn, a_col0 // tk
    n_out = 2 if relu2 else 1

    def body(*refs):
        a_ref, b_ref = refs[0], refs[1]
        pos = 2
        res_ref = None
        if res is not None:
            res_ref = refs[pos]
            pos += 1
        outs = refs[pos:pos + n_out]
        acc_ref = refs[pos + n_out] if nk > 1 else None
        part = jnp.dot(_mx(a_ref[...]), _mx(b_ref[...]), preferred_element_type=F32)

        def finish(acc):
            if res_ref is not None:
                acc = acc + res_ref[...]
            outs[0][...] = acc.astype(outs[0].dtype)
            if relu2:
                r = jnp.maximum(acc, 0.0)
                outs[1][...] = (r * r).astype(outs[1].dtype)

        if nk == 1:
            finish(part)
        else:
            k = pl.program_id(2)

            @pl.when(k == 0)
            def _():
                acc_ref[...] = part

            @pl.when(k > 0)
            def _():
                acc_ref[...] += part

            @pl.when(k == nk - 1)
            def _():
                finish(acc_ref[...])

    in_specs = [pl.BlockSpec((tm, tk), lambda i, j, k: (i, ka0 + k)),
                pl.BlockSpec((None, tk, tn), lambda i, j, k: (j // nbs, k, j % nbs))]
    args = [a, b3]
    if res is not None:
        in_specs.append(pl.BlockSpec((tm, tn), lambda i, j, k: (i, j)))
        args.append(res)
    o_spec = pl.BlockSpec((tm, tn), lambda i, j, k: (i, j))
    if relu2:
        out_shape, out_specs = (_sds((M, N), out_dtype), _sds((M, N), out_dtype)), (o_spec, o_spec)
    else:
        out_shape, out_specs = _sds((M, N), out_dtype), o_spec
    scratch = (pltpu.VMEM((tm, tn), F32),) if nk > 1 else ()
    return _pcall(body, name=name, out_shape=out_shape, grid=(M // tm, N // tn, nk),
                  in_specs=in_specs, out_specs=out_specs, scratch=scratch)(*args)


def mm_nt(a, b3, *, name, out_dtype, a_col0=0, res=None, relu_mul=None, after=None):
    M = a.shape[0]
    nb, K, Ns = b3.shape
    tm = _pick(M, (1024, 512, 256, 128))
    tk = _pick(K, (1024, 512, 256))
    tn = _pick(Ns, (2048, 1024, 512, 256, 128) if nb == 1 and Ns <= 2048 else (1024, 512, 256, 128))
    assert a_col0 % tn == 0
    nbs = Ns // tn
    nn, a0 = nb * nbs, a_col0 // tn

    def body(*refs):
        a_ref, b_ref = refs[0], refs[1]
        pos = 2
        mul_ref = res_ref = None
        if relu_mul is not None:
            mul_ref = refs[pos]
            pos += 1
        if res is not None:
            res_ref = refs[pos]
            pos += 1
        pos += after is not None
        o_ref = refs[pos]
        acc_ref = refs[pos + 1] if nn > 1 else None
        part = lax.dot_general(_mx(a_ref[...]), _mx(b_ref[...]), (((1,), (1,)), ((), ())),
                               preferred_element_type=F32)

        def finish(acc):
            if mul_ref is not None:
                acc = acc * (2.0 * jnp.maximum(mul_ref[...].astype(F32), 0.0))
            if res_ref is not None:
                acc = acc + res_ref[...]
            o_ref[...] = acc.astype(o_ref.dtype)

        if nn == 1:
            finish(part)
        else:
            j = pl.program_id(2)

            @pl.when(j == 0)
            def _():
                acc_ref[...] = part

            @pl.when(j > 0)
            def _():
                acc_ref[...] += part

            @pl.when(j == nn - 1)
            def _():
                finish(acc_ref[...])

    in_specs = [pl.BlockSpec((tm, tn), lambda i, kk, j: (i, a0 + j)),
                pl.BlockSpec((None, tk, tn), lambda i, kk, j: (j // nbs, kk, j % nbs))]
    args = [a, b3]
    for extra in (relu_mul, res):
        if extra is not None:
            in_specs.append(pl.BlockSpec((tm, tk), lambda i, kk, j: (i, kk)))
            args.append(extra)
    if after is not None:
        in_specs.append(pl.BlockSpec(memory_space=pl.ANY))
        args.append(after)
    scratch = (pltpu.VMEM((tm, tk), F32),) if nn > 1 else ()
    return _pcall(body, name=name, out_shape=_sds((M, K), out_dtype), grid=(M // tm, K // tk, nn),
                  in_specs=in_specs, out_specs=pl.BlockSpec((tm, tk), lambda i, kk, j: (i, kk)),
                  scratch=scratch)(*args)


def mm_tn(a, c, *, nb, name, out_dtype, K=None, N=None, a_col0=0, c_col0=0):
    M = a.shape[0]
    K = K or a.shape[1]
    N = N or c.shape[1]
    Ns = N // nb
    tm = _pick(M, (2048, 1024, 512, 256))
    tk = _pick(K, (1024, 512, 256))
    tn = _pick(Ns, (1024, 512, 256, 128))
    assert a_col0 % tk == 0 and c_col0 % tn == 0
    nm, nbs, a0, c0 = M // tm, Ns // tn, a_col0 // tk, c_col0 // tn

    def body(*refs):
        a_ref, c_ref, o_ref = refs[:3]
        acc_ref = refs[3] if nm > 1 else None
        part = lax.dot_general(_mx(a_ref[...]), _mx(c_ref[...]), (((0,), (0,)), ((), ())),
                               preferred_element_type=F32)
        if nm == 1:
            o_ref[...] = part.astype(o_ref.dtype)
        else:
            m = pl.program_id(2)

            @pl.when(m == 0)
            def _():
                acc_ref[...] = part

            @pl.when(m > 0)
            def _():
                acc_ref[...] += part

            @pl.when(m == nm - 1)
            def _():
                o_ref[...] = acc_ref[...].astype(o_ref.dtype)

    scratch = (pltpu.VMEM((tk, tn), F32),) if nm > 1 else ()
    return _pcall(
        body, name=name, out_shape=_sds((nb, K, Ns), out_dtype), grid=(N // tn, K // tk, nm),
        in_specs=[pl.BlockSpec((tm, tk), lambda j, kk, m: (m, a0 + kk)),
                  pl.BlockSpec((tm, tn), lambda j, kk, m: (m, c0 + j))],
        out_specs=pl.BlockSpec((None, tk, tn), lambda j, kk, m: (j // nbs, kk, j % nbs)),
        scratch=scratch)(a, c)


def rms_fwd(x, g, *, name, col0=0, width=None, out_dtype=BF16):
    R = x.shape[0]
    width = width or x.shape[1]
    assert col0 % width == 0
    cb = col0 // width
    tr = _pick(R, (512, 256, 128))

    def body(x_ref, g_ref, o_ref):
        xf = x_ref[...].astype(F32)
        r = lax.rsqrt(jnp.mean(xf * xf, axis=1, keepdims=True) + EPS)
        o_ref[...] = (xf * r * g_ref[...]).astype(o_ref.dtype)

    return _pcall(body, name=name, out_shape=_sds((R, width), out_dtype), grid=(R // tr,),
                  in_specs=[pl.BlockSpec((tr, width), lambda i: (i, cb)),
                            pl.BlockSpec((1, width), lambda i: (0, 0))],
                  out_specs=pl.BlockSpec((tr, width), lambda i: (i, 0)))(x, g)


def rms_bwd(x, g, dy, *, name, col0=0, width=None, res=None, dx_dtype=F32, need_dx=True):
    R = x.shape[0]
    width = width or x.shape[1]
    cb = col0 // width
    tr = _pick(R, (512, 256, 128))

    def body(*refs):
        x_ref, g_ref, dy_ref = refs[:3]
        pos = 3
        res_ref = None
        if res is not None:
            res_ref = refs[pos]
            pos += 1
        dx_ref = None
        if need_dx:
            dx_ref = refs[pos]
            pos += 1
        dg_ref = refs[pos]
        xf = x_ref[...].astype(F32)
        dyf = dy_ref[...].astype(F32)
        r = lax.rsqrt(jnp.mean(xf * xf, axis=1, keepdims=True) + EPS)
        xh = xf * r
        if need_dx:
            gy = dyf * g_ref[...]
            dx = r * (gy - xh * jnp.mean(gy * xh, axis=1, keepdims=True))
            if res_ref is not None:
                dx = dx + res_ref[...]
            dx_ref[...] = dx.astype(dx_ref.dtype)
        part = jnp.sum(dyf * xh, axis=0, keepdims=True)

        @pl.when(pl.program_id(0) == 0)
        def _():
            dg_ref[...] = part

        @pl.when(pl.program_id(0) > 0)
        def _():
            dg_ref[...] += part

    in_specs = [pl.BlockSpec((tr, width), lambda i: (i, cb)),
                pl.BlockSpec((1, width), lambda i: (0, 0)),
                pl.BlockSpec((tr, width), lambda i: (i, 0))]
    args = [x, g, dy]
    if res is not None:
        in_specs.append(pl.BlockSpec((tr, width), lambda i: (i, 0)))
        args.append(res)
    dg_shape, dg_spec = _sds((1, width), F32), pl.BlockSpec((1, width), lambda i: (0, 0))
    if need_dx:
        out_shape = (_sds((R, width), dx_dtype), dg_shape)
        out_specs = (pl.BlockSpec((tr, width), lambda i: (i, 0)), dg_spec)
    else:
        out_shape, out_specs = dg_shape, dg_spec
    out = _pcall(body, name=name, out_shape=out_shape, grid=(R // tr,), in_specs=in_specs,
                 out_specs=out_specs)(*args)
    return out if need_dx else (None, out)


HEAD_ROW_BLOCKS = (2048, 1024, 512, 256, 128)


def _rope_apply(y, c, sa, sb):
    return y * c + pltpu.roll(y, 96, 1) * sa + pltpu.roll(y, 32, 1) * sb


def _rope_transpose(dy, c, sa, sb):
    return dy * c + pltpu.roll(dy * sa, 32, 1) + pltpu.roll(dy * sb, 96, 1)


def rope_tables(seq):
    pos = jnp.arange(seq, dtype=F32)
    inv = ROPE_THETA ** (-jnp.arange(0, MLA_ROPE, 2, dtype=F32) / MLA_ROPE)
    ang = pos[:, None] * inv[None, :]
    cos, sin = jnp.cos(ang), jnp.sin(ang)
    z32, z64 = jnp.zeros_like(cos), jnp.zeros((seq, 64), F32)
    c = jnp.concatenate([cos, cos, z64], axis=1)
    sa = jnp.concatenate([-sin, z32, z64], axis=1)
    sb = jnp.concatenate([z32, sin, z64], axis=1)
    return c, sa, sb


def _head_vec(part_refs):
    xs = [p[...].astype(F32) for p in part_refs]
    return xs[0] if len(xs) == 1 else jnp.concatenate(xs, axis=1)


def prep_fwd(parts, g, *, name, n_heads, n_real, rope=None):
    rows = parts[0][0].shape[0]
    dh = sum(w for _, w, _ in parts)
    tr = _pick(rows, HEAD_ROW_BLOCKS)
    npart = len(parts)

    def body(*refs):
        part_refs, g_ref = refs[:npart], refs[npart]
        pos = npart + 1
        if rope is not None:
            c_ref, sa_ref, sb_ref = refs[pos:pos + 3]
            pos += 3
        o_ref = refs[pos]
        x = _head_vec(part_refs)
        r = lax.rsqrt(jnp.sum(x * x, axis=1, keepdims=True) * (1.0 / n_real) + EPS)
        y = x * r * g_ref[...]
        if rope is not None:
            yr = _rope_apply(y[:, dh - 128:], c_ref[...], sa_ref[...], sb_ref[...])
            y = jnp.concatenate([y[:, :dh - 128], yr], axis=1)
        o_ref[...] = y.astype(o_ref.dtype)

    in_specs, args = [], []
    for arr, w, fn in parts:
        in_specs.append(pl.BlockSpec((tr, w), functools.partial(lambda h, i, fn: (i, fn(h)), fn=fn)))
        args.append(arr)
    in_specs.append(pl.BlockSpec((1, dh), lambda h, i: (0, 0)))
    args.append(g)
    if rope is not None:
        for t in rope:
            in_specs.append(pl.BlockSpec((tr, 128), lambda h, i: (i, 0)))
            args.append(t)
    return _pcall(body, name=name, out_shape=_sds((n_heads, rows, dh), BF16), grid=(n_heads, rows // tr),
                  in_specs=in_specs, out_specs=pl.BlockSpec((None, tr, dh), lambda h, i: (h, i, 0)))(*args)


def _norm_bwd(x, g, dyn, n_real):
    r = lax.rsqrt(jnp.sum(x * x, axis=1, keepdims=True) * (1.0 / n_real) + EPS)
    xh = x * r
    gy = dyn * g
    dx = r * (gy - xh * (jnp.sum(gy * xh, axis=1, keepdims=True) * (1.0 / n_real)))
    return dx, jnp.sum(dyn * xh, axis=0, keepdims=True)


def prep_bwd_q(src, dy, g, *, name, n_heads, dh, n_real, rope=None, out_dtype=BF16):
    rows = src.shape[0]
    tr = _pick(rows, HEAD_ROW_BLOCKS)

    def body(*refs):
        x_ref, dy_ref, g_ref = refs[:3]
        pos = 3
        if rope is not None:
            c_ref, sa_ref, sb_ref = refs[pos:pos + 3]
            pos += 3
        dx_ref, dg_ref = refs[pos], refs[pos + 1]
        dyn = dy_ref[...].astype(F32)
        if rope is not None:
            dr = _rope_transpose(dyn[:, dh - 128:], c_ref[...], sa_ref[...], sb_ref[...])
            dyn = jnp.concatenate([dyn[:, :dh - 128], dr], axis=1)
        dx, dg = _norm_bwd(x_ref[...].astype(F32), g_ref[...], dyn, n_real)
        dx_ref[...] = dx.astype(dx_ref.dtype)
        first = jnp.logical_and(pl.program_id(0) == 0, pl.program_id(1) == 0)

        @pl.when(first)
        def _():
            dg_ref[...] = dg

        @pl.when(jnp.logical_not(first))
        def _():
            dg_ref[...] += dg

    in_specs = [pl.BlockSpec((tr, dh), lambda i, h: (i, h)),
                pl.BlockSpec((None, tr, dh), lambda i, h: (h, i, 0)),
                pl.BlockSpec((1, dh), lambda i, h: (0, 0))]
    args = [src, dy, g]
    if rope is not None:
        for t in rope:
            in_specs.append(pl.BlockSpec((tr, 128), lambda i, h: (i, 0)))
            args.append(t)
    return _pcall(body, name=name, out_shape=(_sds((rows, n_heads * dh), out_dtype), _sds((1, dh), F32)),
                  grid=(rows // tr, n_heads), in_specs=in_specs,
                  out_specs=(pl.BlockSpec((tr, dh), lambda i, h: (i, h)),
                             pl.BlockSpec((1, dh), lambda i, h: (0, 0))))(*args)


def prep_bwd_mla_k(kv_raw, zs, dkf, dv, g, rope, *, name):
    rows = kv_raw.shape[0]
    tr = _pick(rows, HEAD_ROW_BLOCKS)

    def body(kn_ref, kr_ref, dy_ref, dv_ref, g_ref, c_ref, sa_ref, sb_ref, dkv_ref, dkr_ref, dg_ref):
        h = pl.program_id(1)
        x = jnp.concatenate([kn_ref[...].astype(F32), kr_ref[...].astype(F32)], axis=1)
        dyn = dy_ref[...].astype(F32)
        dr = _rope_transpose(dyn[:, 128:], c_ref[...], sa_ref[...], sb_ref[...])
        dyn = jnp.concatenate([dyn[:, :128], dr], axis=1)
        dx, dg = _norm_bwd(x, g_ref[...], dyn, MLA_QK)
        dkv_ref[...] = jnp.concatenate([dx[:, :128], dv_ref[...].astype(F32)], axis=1).astype(dkv_ref.dtype)

        @pl.when(h == 0)
        def _():
            dkr_ref[...] = dx[:, 128:]

        @pl.when(h > 0)
        def _():
            dkr_ref[...] += dx[:, 128:]

        first = jnp.logical_and(pl.program_id(0) == 0, h == 0)

        @pl.when(first)
        def _():
            dg_ref[...] = dg

        @pl.when(jnp.logical_not(first))
        def _():
            dg_ref[...] += dg

    tab = pl.BlockSpec((tr, 128), lambda i, h: (i, 0))
    return _pcall(
        body, name=name,
        out_shape=(_sds((rows, N_HEADS * 256), BF16), _sds((rows, 128), F32), _sds((1, MLA_PAD), F32)),
        grid=(rows // tr, N_HEADS),
        in_specs=[pl.BlockSpec((tr, 128), lambda i, h: (i, 2 * h)),
                  pl.BlockSpec((tr, 128), lambda i, h: (i, Z_KR // 128)),
                  pl.BlockSpec((None, tr, MLA_PAD), lambda i, h: (h, i, 0)),
                  pl.BlockSpec((None, tr, 128), lambda i, h: (h, i, 0)),
                  pl.BlockSpec((1, MLA_PAD), lambda i, h: (0, 0)), tab, tab, tab],
        out_specs=(pl.BlockSpec((tr, 256), lambda i, h: (i, h)),
                   pl.BlockSpec((tr, 128), lambda i, h: (i, 0)),
                   pl.BlockSpec((1, MLA_PAD), lambda i, h: (0, 0))))(kv_raw, zs, dkf, dv, g, *rope)


def prep_bwd_groups(src, base_blk, dys, gs, *, name, n_heads, kinds, out_dtype=BF16):
    rows = src.shape[0]
    ng = len(kinds)
    J = ng * n_heads
    tr = _pick(rows, HEAD_ROW_BLOCKS)
    gstack = jnp.stack([gs[k] if kinds[k] == "norm" else jnp.ones((1, DH), F32) for k in range(ng)])

    def body(*refs):
        x_ref = refs[0]
        dy_refs = refs[1:1 + ng]
        g_ref, dx_ref, dg_ref = refs[1 + ng:4 + ng]
        j, i = pl.program_id(0), pl.program_id(1)
        grp = j // n_heads
        dy = dy_refs[0][...].astype(F32)
        for k in range(1, ng):
            dy = jnp.where(grp == k, dy_refs[k][...].astype(F32), dy)
        dx, dg = _norm_bwd(x_ref[...].astype(F32), g_ref[...], dy, DH)
        is_copy = functools.reduce(jnp.logical_or, [grp == k for k in range(ng) if kinds[k] == "copy"],
                                   jnp.bool_(False))
        dx_ref[...] = jnp.where(is_copy, dy, dx).astype(dx_ref.dtype)
        dg = jnp.where(is_copy, jnp.zeros_like(dg), dg)
        first = jnp.logical_and(j % n_heads == 0, i == 0)

        @pl.when(first)
        def _():
            dg_ref[...] = dg

        @pl.when(jnp.logical_not(first))
        def _():
            dg_ref[...] += dg

    in_specs = [pl.BlockSpec((tr, DH), lambda j, i: (i, base_blk + j))]
    for k in range(ng):
        in_specs.append(pl.BlockSpec(
            (None, tr, DH),
            functools.partial(lambda j, i, k: (jnp.clip(j - k * n_heads, 0, n_heads - 1),
                                               jnp.where(j // n_heads == k, i, 0), 0), k=k)))
    in_specs.append(pl.BlockSpec((None, 1, DH), lambda j, i: (j // n_heads, 0, 0)))
    return _pcall(body, name=name, out_shape=(_sds((rows, J * DH), out_dtype), _sds((ng, 1, DH), F32)),
                  grid=(J, rows // tr), in_specs=in_specs,
                  out_specs=(pl.BlockSpec((tr, DH), lambda j, i: (i, j)),
                             pl.BlockSpec((None, 1, DH), lambda j, i: (j // n_heads, 0, 0))))(src, *dys, gstack)


def _attn_cfg(mode, sq, sk):
    if mode == "chunk":
        tq = 128
        win = min((LEFT_CHUNKS + 2) * CHUNK, sk)
    else:
        tq = _pick(sq, (256, 128))
        win = sk
    scale = (MLA_QK if mode == "mla" else DH) ** -0.5
    return tq, win, scale


def _attn_key_rows(mode, i, tq, win, sk, run):
    if mode == "chunk":
        start = pl.multiple_of(jnp.clip((i - LEFT_CHUNKS // 2) * 128, 0, sk - win), 128)
        run(pl.ds(start, win), start)
    elif mode == "cross":
        run(slice(0, sk), 0)
    else:
        lax.switch(i, [functools.partial(run, slice(0, (b + 1) * tq), 0) for b in range(sk // tq)])


def _attn_scores(mode, i, tq, scale, q, kk, start, cq, ck, t_ref):
    nk = kk.shape[0]
    s = lax.dot_general(q, kk, (((1,), (1,)), ((), ())), preferred_element_type=F32) * scale
    if mode == "cross":
        return s
    t_pos = i * tq + lax.broadcasted_iota(jnp.int32, (tq, nk), 0)
    s_pos = start + lax.broadcasted_iota(jnp.int32, (tq, nk), 1)
    if mode == "fox":
        s = s + cq - ck
        allowed = s_pos <= t_pos
    else:
        qc, kc = lax.shift_right_logical(t_pos, CHUNK_SHIFT), lax.shift_right_logical(s_pos, CHUNK_SHIFT)
        allowed = kc <= qc
        if mode == "chunk":
            allowed = jnp.logical_and(allowed, kc >= qc - LEFT_CHUNKS)
            tiles = []
            for w in range(nk // 128):
                delta = i - (start // 128 + w)
                tiles.append(jnp.where(delta == 0, t_ref[0], jnp.where(delta == 1, t_ref[1], t_ref[2])))
            s = s + jnp.concatenate(tiles, axis=1)
    return jnp.where(allowed, s, NEG)


def attn_fwd(q, k, v_arr, v_blk, *, mode, name, n_heads, cq=None, ck=None, tiles=None):
    _, sq, dk = q.shape
    sk = k.shape[1]
    tq, win, scale = _attn_cfg(mode, sq, sk)

    def body(*refs):
        q_ref, k_ref, v_ref = refs[:3]
        pos = 3
        cq_ref = ck_ref = t_ref = None
        if mode == "fox":
            cq_ref, ck_ref = refs[pos:pos + 2]
            pos += 2
        if mode == "chunk":
            t_ref = refs[pos]
            pos += 1
        o_ref, lse_ref = refs[pos], refs[pos + 1]
        i = pl.program_id(1)

        def run(rows, start):
            cq, ck = (cq_ref[...], ck_ref[:, rows]) if mode == "fox" else (None, None)
            s = _attn_scores(mode, i, tq, scale, q_ref[...], k_ref[rows, :], start, cq, ck, t_ref)
            m = jnp.max(s, axis=1, keepdims=True)
            e = jnp.exp(s - m)
            l = jnp.sum(e, axis=1, keepdims=True)
            p = e * (1.0 / l)
            o_ref[...] = jnp.dot(_mx(p), _mx(v_ref[rows, :]), preferred_element_type=F32).astype(o_ref.dtype)
            lse_ref[...] = m + jnp.log(l)

        _attn_key_rows(mode, i, tq, win, sk, run)

    in_specs = [pl.BlockSpec((None, tq, dk), lambda h, i: (h, i, 0)),
                pl.BlockSpec((None, sk, dk), lambda h, i: (h, 0, 0)),
                pl.BlockSpec((sk, DH), lambda h, i: (0, v_blk(h)))]
    args = [q, k, v_arr]
    if mode == "fox":
        in_specs += [pl.BlockSpec((None, tq, 1), lambda h, i: (h, i, 0)),
                     pl.BlockSpec((None, 1, sk), lambda h, i: (h, 0, 0))]
        args += [cq, ck]
    if mode == "chunk":
        in_specs.append(pl.BlockSpec((3, None, 128, 128), lambda h, i: (0, h, 0, 0)))
        args.append(tiles)
    return _pcall(body, name=name,
                  out_shape=(_sds((sq, n_heads * DH), BF16), _sds((n_heads, sq, 1), F32)),
                  grid=(n_heads, sq // tq), in_specs=in_specs,
                  out_specs=(pl.BlockSpec((tq, DH), lambda h, i: (i, h)),
                             pl.BlockSpec((None, tq, 1), lambda h, i: (h, i, 0))))(*args)


def attn_bwd(q, k, v_arr, v_blk, o, do, lse, *, mode, name, n_heads, cq=None, ck=None, tiles=None):
    _, sq, dk = q.shape
    sk = k.shape[1]
    tq, win, scale = _attn_cfg(mode, sq, sk)
    n_extra = {"fox": 2, "chunk": 1}.get(mode, 0)

    def body(*refs):
        q_ref, k_ref, v_ref, o_ref, do_ref, lse_ref = refs[:6]
        pos = 6
        cq_ref = ck_ref = t_ref = None
        if mode == "fox":
            cq_ref, ck_ref = refs[pos:pos + 2]
            pos += 2
        if mode == "chunk":
            t_ref = refs[pos]
            pos += 1
        dq_ref, dk_ref, dv_ref = refs[pos:pos + 3]
        extra = refs[pos + 3:pos + 3 + n_extra]
        i = pl.program_id(1)

        @pl.when(i == 0)
        def _():
            dk_ref[...] = jnp.zeros_like(dk_ref)
            dv_ref[...] = jnp.zeros_like(dv_ref)
            if mode == "fox":
                extra[1][...] = jnp.zeros_like(extra[1])
            if mode == "chunk":
                extra[0][...] = jnp.zeros_like(extra[0])

        def run(rows, start):
            q = q_ref[...]
            do = do_ref[...]
            kk = k_ref[rows, :]
            cq, ck = (cq_ref[...], ck_ref[:, rows]) if mode == "fox" else (None, None)
            s = _attn_scores(mode, i, tq, scale, q, kk, start, cq, ck, t_ref)
            p = jnp.exp(s - lse_ref[...])
            drow = jnp.sum(do.astype(F32) * o_ref[...].astype(F32), axis=1, keepdims=True)
            dp = lax.dot_general(do, _mx(v_ref[rows, :]), (((1,), (1,)), ((), ())), preferred_element_type=F32)
            ds = p * (dp - drow)
            dsb = _mx(ds)
            dq_ref[...] = (jnp.dot(dsb, kk, preferred_element_type=F32) * scale).astype(dq_ref.dtype)
            dk_ref[rows, :] += lax.dot_general(dsb, q, (((0,), (0,)), ((), ())),
                                               preferred_element_type=F32) * scale
            dv_ref[rows, :] += lax.dot_general(_mx(p), do, (((0,), (0,)), ((), ())), preferred_element_type=F32)
            if mode == "chunk":
                dt_ref = extra[0]
                for w in range(win // 128):
                    delta = i - (start // 128 + w)
                    tile = ds[:, w * 128:(w + 1) * 128]
                    zero = jnp.zeros_like(tile)
                    dt_ref[0] += jnp.where(delta == 0, tile, zero)
                    dt_ref[1] += jnp.where(delta == 1, tile, zero)
                    dt_ref[2] += jnp.where(delta >= 2, tile, zero)
            if mode == "fox":
                extra[0][...] = jnp.sum(ds, axis=1, keepdims=True)
                extra[1][:, rows] -= jnp.sum(ds, axis=0, keepdims=True)

        _attn_key_rows(mode, i, tq, win, sk, run)

    in_specs = [pl.BlockSpec((None, tq, dk), lambda h, i: (h, i, 0)),
                pl.BlockSpec((None, sk, dk), lambda h, i: (h, 0, 0)),
                pl.BlockSpec((sk, DH), lambda h, i: (0, v_blk(h))),
                pl.BlockSpec((tq, DH), lambda h, i: (i, h)),
                pl.BlockSpec((tq, DH), lambda h, i: (i, h)),
                pl.BlockSpec((None, tq, 1), lambda h, i: (h, i, 0))]
    args = [q, k, v_arr, o, do, lse]
    out_shape = [_sds((n_heads, sq, dk), F32), _sds((n_heads, sk, dk), F32), _sds((n_heads, sk, DH), F32)]
    out_specs = [pl.BlockSpec((None, tq, dk), lambda h, i: (h, i, 0)),
                 pl.BlockSpec((None, sk, dk), lambda h, i: (h, 0, 0)),
                 pl.BlockSpec((None, sk, DH), lambda h, i: (h, 0, 0))]
    if mode == "fox":
        in_specs += [pl.BlockSpec((None, tq, 1), lambda h, i: (h, i, 0)),
                     pl.BlockSpec((None, 1, sk), lambda h, i: (h, 0, 0))]
        args += [cq, ck]
        out_shape += [_sds((n_heads, sq, 1), F32), _sds((n_heads, 1, sk), F32)]
        out_specs += [pl.BlockSpec((None, tq, 1), lambda h, i: (h, i, 0)),
                      pl.BlockSpec((None, 1, sk), lambda h, i: (h, 0, 0))]
    if mode == "chunk":
        in_specs.append(pl.BlockSpec((3, None, 128, 128), lambda h, i: (0, h, 0, 0)))
        args.append(tiles)
        out_shape.append(_sds((3, n_heads, 128, 128), F32))
        out_specs.append(pl.BlockSpec((3, None, 128, 128), lambda h, i: (0, h, 0, 0)))
    return _pcall(body, name=name, out_shape=tuple(out_shape), grid=(n_heads, sq // tq),
                  in_specs=in_specs, out_specs=tuple(out_specs))(*args)


REL_LANES = 384
REL_KBLK = 2048


def _rel_onehot(t, k):
    rho = k * REL_KBLK + lax.broadcasted_iota(jnp.int32, (REL_KBLK, REL_LANES), 0)
    lane = lax.broadcasted_iota(jnp.int32, (REL_KBLK, REL_LANES), 1)
    diff = lax.shift_right_logical(rho, 7) - jnp.bitwise_and(rho, 127)
    idx = jnp.where(t == 0, diff + REL_CLIP,
                    jnp.where(t == 1, jnp.minimum(diff + 128, REL_CLIP) + REL_CLIP, N_REL - 1))
    return jnp.where(idx == lane, 1.0, 0.0).astype(F32)


def _split3(x):
    hi = x.astype(BF16)
    rest = x - hi.astype(F32)
    mid = rest.astype(BF16)
    return hi, mid, (rest - mid.astype(F32)).astype(BF16)


def relbias_onehot():
    def body(o_ref):
        o_ref[...] = _rel_onehot(pl.program_id(0), pl.program_id(1)).astype(o_ref.dtype)

    return _pcall(body, name="relbias_onehot", out_shape=_sds((3, 128 * 128, REL_LANES), BF16),
                  grid=(3, 128 * 128 // REL_KBLK), in_specs=[],
                  out_specs=pl.BlockSpec((None, REL_KBLK, REL_LANES), lambda t, k: (t, k, 0)))()


ONEHOT_SPEC = pl.BlockSpec((None, REL_KBLK, REL_LANES), lambda t, k: (t, k, 0))


def relbias_tiles(rel_bias, onehot):
    nh = rel_bias.shape[0]
    rb = jnp.pad(rel_bias, ((0, 0), (0, REL_LANES - N_REL)))

    def body(rb_ref, e_ref, o_ref):
        e = e_ref[...]
        hi, mid, lo = [lax.dot_general(t, e, (((1,), (1,)), ((), ())), preferred_element_type=F32)
                       for t in _split3(rb_ref[...])]
        o_ref[...] = (hi + mid) + lo

    flat = _pcall(body, name="relbias_tiles", out_shape=_sds((3, nh, 128 * 128), F32),
                  grid=(3, 128 * 128 // REL_KBLK),
                  in_specs=[pl.BlockSpec((nh, REL_LANES), lambda t, k: (0, 0)), ONEHOT_SPEC],
                  out_specs=pl.BlockSpec((None, nh, REL_KBLK), lambda t, k: (t, 0, k)))(rb, onehot)
    return flat.reshape(3, nh, 128, 128)


def relbias_tiles_bwd(dtiles, onehot):
    nh = dtiles.shape[1]

    def body(dt_ref, e_ref, o_ref):
        t, k = pl.program_id(0), pl.program_id(1)
        e = e_ref[...]
        hi, mid, lo = [jnp.dot(t_, e, preferred_element_type=F32) for t_ in _split3(dt_ref[...])]
        part = (hi + mid) + lo
        first = jnp.logical_and(t == 0, k == 0)

        @pl.when(first)
        def _():
            o_ref[...] = part

        @pl.when(jnp.logical_not(first))
        def _():
            o_ref[...] += part

    out = _pcall(body, name="relbias_tiles_bwd", out_shape=_sds((nh, REL_LANES), F32),
                 grid=(3, 128 * 128 // REL_KBLK),
                 in_specs=[pl.BlockSpec((None, nh, REL_KBLK), lambda t, k: (t, 0, k)), ONEHOT_SPEC],
                 out_specs=pl.BlockSpec((nh, REL_LANES), lambda t, k: (0, 0)))(
                     dtiles.reshape(3, nh, 128 * 128), onehot)
    return out[:, :N_REL]


CUM_BLK = 256


def _tri(n, lower):
    r = lax.broadcasted_iota(jnp.int32, (n, n), 0)
    c = lax.broadcasted_iota(jnp.int32, (n, n), 1)
    return jnp.where(r >= c if lower else r <= c, 1.0, 0.0).astype(F32)


def fox_cum_fwd(zs, bf):
    S = zs.shape[0]
    tb = min(CUM_BLK, S)

    def body(f_ref, b_ref, cum_ref, cumt_ref, carry_ref):
        @pl.when(pl.program_id(0) == 0)
        def _():
            carry_ref[...] = jnp.zeros_like(carry_ref)

        x = f_ref[...] + b_ref[...]
        lane = lax.broadcasted_iota(jnp.int32, x.shape, 1)
        logf = jnp.where(lane < N_HEADS, jnp.minimum(x, 0.0) - jnp.log(1.0 + jnp.exp(-jnp.abs(x))), 0.0)
        cum = jnp.dot(_tri(tb, True), logf, preferred_element_type=F32,
                      precision=lax.Precision.HIGHEST) + carry_ref[...]
        carry_ref[...] = cum[tb - 1:tb, :]
        cum_ref[...] = cum
        cumt_ref[...] = cum.T

    return _pcall(body, name="fox_cum_fwd", out_shape=(_sds((S, 128), F32), _sds((128, S), F32)),
                  grid=(S // tb,),
                  in_specs=[pl.BlockSpec((tb, 128), lambda i: (i, Z_FF // 128)),
                            pl.BlockSpec((1, 128), lambda i: (0, 0))],
                  out_specs=(pl.BlockSpec((tb, 128), lambda i: (i, 0)),
                             pl.BlockSpec((128, tb), lambda i: (0, i))),
                  scratch=(pltpu.VMEM((1, 128), F32),))(zs, bf)


def fox_cum_bwd(zs, bf, dcum):
    S = zs.shape[0]
    tb = min(CUM_BLK, S)
    nblk = S // tb

    def body(f_ref, b_ref, d_ref, df_ref, db_ref, carry_ref):
        @pl.when(pl.program_id(0) == 0)
        def _():
            carry_ref[...] = jnp.zeros_like(carry_ref)
            db_ref[...] = jnp.zeros_like(db_ref)

        d = d_ref[...]
        dlogf = jnp.dot(_tri(tb, False), d, preferred_element_type=F32,
                        precision=lax.Precision.HIGHEST) + carry_ref[...]
        carry_ref[...] += jnp.sum(d, axis=0, keepdims=True)
        x = f_ref[...] + b_ref[...]
        lane = lax.broadcasted_iota(jnp.int32, x.shape, 1)
        dx = jnp.where(lane < N_HEADS, dlogf / (1.0 + jnp.exp(x)), 0.0)
        df_ref[...] = dx
        db_ref[...] += jnp.sum(dx, axis=0, keepdims=True)

    return _pcall(body, name="fox_cum_bwd", out_shape=(_sds((S, 128), F32), _sds((1, 128), F32)),
                  grid=(nblk,),
                  in_specs=[pl.BlockSpec((tb, 128), lambda i: (nblk - 1 - i, Z_FF // 128)),
                            pl.BlockSpec((1, 128), lambda i: (0, 0)),
                            pl.BlockSpec((tb, 128), lambda i: (nblk - 1 - i, 0))],
                  out_specs=(pl.BlockSpec((tb, 128), lambda i: (nblk - 1 - i, 0)),
                             pl.BlockSpec((1, 128), lambda i: (0, 0))),
                  scratch=(pltpu.VMEM((1, 128), F32),))(zs, bf, dcum)


def _sigmoid(x):
    return 1.0 / (1.0 + jnp.exp(-x))


def merge_fwd(z, projs):
    S = z.shape[0]
    tr, tc = _pick(S, (512, 256, 128)), 512
    nbc = D_MODEL // tc
    g0 = Z_GATE // tc

    def body(g0_ref, g1_ref, g2_ref, p0_ref, p1_ref, p2_ref, o_ref):
        acc = _sigmoid(g0_ref[...]) * p0_ref[...]
        acc += _sigmoid(g1_ref[...]) * p1_ref[...]
        acc += _sigmoid(g2_ref[...]) * p2_ref[...]
        o_ref[...] = acc.astype(o_ref.dtype)

    gspecs = [pl.BlockSpec((tr, tc), functools.partial(lambda i, j, n: (i, g0 + n * nbc + j), n=n))
              for n in range(3)]
    pspec = pl.BlockSpec((tr, tc), lambda i, j: (i, j))
    return _pcall(body, name="merge_fwd", out_shape=_sds((S, D_MODEL), BF16), grid=(S // tr, nbc),
                  in_specs=gspecs + [pspec] * 3, out_specs=pspec)(z, z, z, *projs)


def merge_bwd(z, projs, dmerged):
    S = z.shape[0]
    tr, tc = _pick(S, (512, 256, 128)), 512
    nbc = D_MODEL // tc
    g0 = Z_GATE // tc

    def body(g0_ref, g1_ref, g2_ref, p0_ref, p1_ref, p2_ref, dm_ref, dg0, dg1, dg2, dp0, dp1, dp2):
        dm = dm_ref[...]
        for g_ref, p_ref, dg_ref, dp_ref in ((g0_ref, p0_ref, dg0, dp0), (g1_ref, p1_ref, dg1, dp1),
                                             (g2_ref, p2_ref, dg2, dp2)):
            sg = _sigmoid(g_ref[...])
            dp_ref[...] = (dm * sg).astype(dp_ref.dtype)
            dg_ref[...] = (dm * p_ref[...] * sg * (1.0 - sg)).astype(dg_ref.dtype)

    gspecs = [pl.BlockSpec((tr, tc), functools.partial(lambda i, j, n: (i, g0 + n * nbc + j), n=n))
              for n in range(3)]
    pspec = pl.BlockSpec((tr, tc), lambda i, j: (i, j))
    out = _pcall(body, name="merge_bwd", out_shape=tuple(_sds((S, D_MODEL), BF16) for _ in range(6)),
                 grid=(S // tr, nbc), in_specs=gspecs + [pspec] * 4,
                 out_specs=tuple([pspec] * 6))(z, z, z, *projs, dmerged)
    return out[:3], out[3:]


def loss_head(y, target):
    S, D = y.shape
    tr = _pick(S, (256, 128))

    def body(y_ref, t_ref, dy_ref, l_ref):
        e = y_ref[...] - t_ref[...]
        dy_ref[...] = e * (1.0 / D)
        part = jnp.sum(jnp.sum(e * e, axis=1, keepdims=True), axis=0, keepdims=True) * (0.5 / D)

        @pl.when(pl.program_id(0) == 0)
        def _():
            l_ref[...] = part

        @pl.when(pl.program_id(0) > 0)
        def _():
            l_ref[...] += part

    spec = pl.BlockSpec((tr, D), lambda i: (i, 0))
    return _pcall(body, name="loss_head", out_shape=(_sds((S, D), F32), _sds((1, 1), F32)), grid=(S // tr,),
                  in_specs=[spec, spec], out_specs=(spec, pl.BlockSpec((1, 1), lambda i: (0, 0))))(y, target)


def _adamw_update(w, g, m, v):
    nm = ADAM_B1 * m + (1.0 - ADAM_B1) * g
    nv = ADAM_B2 * v + (1.0 - ADAM_B2) * (g * g)
    delta = -ADAM_LR * ((nm / (1.0 - ADAM_B1 ** ADAM_STEP)) / (jnp.sqrt(nv / (1.0 - ADAM_B2 ** ADAM_STEP)) + ADAM_EPS)
                        + ADAM_WD * w)
    return delta, nm, nv


ADAMW_BLOCK_BYTES = 1024 * 1024


def adamw(w, gs, m, v, *, name):
    L, R, C = w.shape
    tr = _pick(R, (512, 256, 128, 64, 32, 16, 8))
    while tr * C * 4 > ADAMW_BLOCK_BYTES and tr % 16 == 0:
        tr //= 2

    def body(*refs):
        w_ref, m_ref, v_ref = refs[:3]
        g_refs = refs[3:3 + L]
        go_ref, d_ref, nm_ref, nv_ref = refs[3 + L:]
        l = pl.program_id(0)
        g_ = g_refs[0][...]
        for k in range(1, L):
            g_ = jnp.where(l == k, g_refs[k][...], g_)
        go_ref[...] = g_
        d_ref[...], nm_ref[...], nv_ref[...] = _adamw_update(w_ref[...], g_, m_ref[...], v_ref[...])

    spec = pl.BlockSpec((None, tr, C), lambda l, i: (l, i, 0))
    gspecs = [pl.BlockSpec((tr, C), functools.partial(lambda l, i, k: (jnp.where(l == k, i, 0), 0), k=k))
              for k in range(L)]
    return _pcall(body, name=name, out_shape=tuple(_sds((L, R, C), F32) for _ in range(4)),
                  grid=(L, R // tr), in_specs=[spec] * 3 + gspecs, out_specs=(spec,) * 4)(w, m, v, *gs)


SHARD_SHAPES = {
    "w_uq": (Q_LORA, 384), "w_ukv": (KV_LORA, 512), "w_br": (3, MIX_W, 512), "w_out": (512, D_MODEL),
    "w_xq": (512, 512), "w_xkv": (512, 1024), "w_xo": (512, 512), "w_1": (D_MODEL, 2048),
    "w_2": (2048, D_MODEL), "w_in": (D_MODEL, W_IN_SHARD),
}


def _rows2d(a, name):
    shp = SHARD_SHAPES[name]
    return a.reshape(a.shape[:a.ndim - len(shp)] + (-1, shp[-1]))


def _cols_from_shards(g):
    return jnp.transpose(g, (1, 0, 2)).reshape(g.shape[1], 4 * g.shape[2])


def _cols_to_shards(w):
    return jnp.transpose(w.reshape(w.shape[0], 4, w.shape[1] // 4), (1, 0, 2))


EARLY_WEIGHTS = ("w_in", "w_uq", "w_ukv")
LATE_WEIGHTS = ("w_br", "w_out", "w_xq", "w_xkv", "w_xo", "w_1", "w_2")
assert sorted(EARLY_WEIGHTS + LATE_WEIGHTS) == sorted(PACK_ORDER)


def _w_in_relaid(g_in):
    zeros = lambda n: [jnp.zeros((D_MODEL, n), g_in.dtype)] if n else []
    segs, at = [], 0
    for p0, o0, w in sorted(W_IN_PIECES):
        segs += zeros(p0 - at)
        while w > 0:
            s_, a = divmod(o0, W_IN_SHARD)
            take = min(w, W_IN_SHARD - a)
            segs.append(g_in[s_][:, a:a + take])
            o0, p0, w = o0 + take, p0 + take, w - take
        at = p0
    return jnp.concatenate(segs + zeros(Z_TOT - at), axis=1)


def full_weights(g):
    forms = {
        "w_in": lambda a: {"in_p": _w_in_relaid(a)[None]},
        "w_uq": lambda a: {"uq_p": jnp.pad(_cols_from_shards(a).reshape(Q_LORA, N_HEADS, MLA_QK),
                                           ((0, 0), (0, 0), (0, MLA_PAD - MLA_QK))
                                           ).reshape(1, Q_LORA, N_HEADS * MLA_PAD)},
        "w_ukv": lambda a: {"ukv": a},
        "w_br": lambda a: {"br": [a[:, n] for n in range(3)]},
        "w_out": lambda a: {"out": a.reshape(1, D_MODEL, D_MODEL)},
        "w_xq": lambda a: {"xq": a.reshape(1, D_MODEL, 512)},
        "w_xkv": lambda a: {"xkv": a.reshape(1, D_MODEL, 1024)},
        "w_xo": lambda a: {"xo": a},
        "w_1": lambda a: {"w1": a},
        "w_2": lambda a: {"w2": a.reshape(1, D_FF, D_MODEL)},
    }
    out = {}
    for n, a in g.items():
        out.update(forms[n](a))
    return out


def _w_in_grad_shards(dp):
    d_in = []
    for s_ in range(4):
        lo, hi, segs = s_ * W_IN_SHARD, (s_ + 1) * W_IN_SHARD, []
        for o0, p0, w in sorted((o0, p0, w) for p0, o0, w in W_IN_PIECES):
            a, b = max(lo, o0), min(hi, o0 + w)
            if a < b:
                segs.append(dp[:, p0 + a - o0:p0 + b - o0])
        d_in.append(jnp.concatenate(segs, axis=1))
    return jnp.stack(d_in)


def shard_grads(dw):
    forms = {
        "in_p": lambda a: {"w_in": _w_in_grad_shards(a[0])},
        "uq_p": lambda a: {"w_uq": _cols_to_shards(
            a.reshape(Q_LORA, N_HEADS, MLA_PAD)[:, :, :MLA_QK].reshape(Q_LORA, N_HEADS * MLA_QK))},
        "ukv": lambda a: {"w_ukv": a},
        "br": lambda a: {"w_br": jnp.stack(a, axis=1)},
        "out": lambda a: {"w_out": a.reshape(4, 512, D_MODEL)},
        "xq": lambda a: {"w_xq": a.reshape(4, 512, 512)},
        "xkv": lambda a: {"w_xkv": a.reshape(4, 512, 1024)},
        "xo": lambda a: {"w_xo": a},
        "w1": lambda a: {"w_1": a},
        "w2": lambda a: {"w_2": a.reshape(4, 2048, D_MODEL)},
    }
    out = {}
    for k, a in dw.items():
        out.update(forms[k](a))
    return out


def layer_params(d, l):
    row = lambda v: v.reshape(1, -1).astype(F32)
    padto = lambda v, n: jnp.pad(row(v), ((0, 0), (0, n - v.shape[-1])))
    return dict(
        g_mix=row(d["g_mix"][l]), g_cq=row(d["g_cq"][l]), g_ckv=row(d["g_ckv"][l]),
        g_mla_q=padto(d["g_mla_q"][l], MLA_PAD), g_mla_k=padto(d["g_mla_k"][l], MLA_PAD),
        b_f=padto(d["b_f"][l], 128), g_fox_q=row(d["g_fox_q"][l]), g_fox_k=row(d["g_fox_k"][l]),
        rel_bias=d["rel_bias"][l].astype(F32), g_ch_q=row(d["g_ch_q"][l]), g_ch_k=row(d["g_ch_k"][l]),
        g_cross=row(d["g_cross"][l]), g_mem=row(d["g_mem"][l]), g_x_q=row(d["g_x_q"][l]),
        g_x_k=row(d["g_x_k"][l]), g_mlp=row(d["g_mlp"][l]))


FOX_B = Z_FOX // 128
CH_B = Z_CH // 128


def _after(small, token):
    return small if token is None else small + token[:1, :1].reshape((1,) * small.ndim)


def layer_fwd(x, mem, W, P, rope, late_weights):
    S = x.shape[0]
    s = {}
    s["h"] = rms_fwd(x, P["g_mix"], name="rms_d")
    z = s["z"] = mm_nn(s["h"], W["in_p"], name="mm_in", out_dtype=F32)
    s["cq_n"] = rms_fwd(z, P["g_cq"], col0=Z_CQ, width=Q_LORA, name="rms_cq")
    s["ckv_n"] = rms_fwd(z, P["g_ckv"], col0=Z_CKV, width=KV_LORA, name="rms_ckv")
    s["q_raw"] = mm_nn(s["cq_n"], W["uq_p"], name="mm_uq", out_dtype=F32)
    s["kv_raw"] = mm_nn(s["ckv_n"], W["ukv"], name="mm_ukv", out_dtype=F32)
    s["qa"] = prep_fwd([(s["q_raw"], MLA_PAD, lambda h: h)], P["g_mla_q"], name="prep_mla_q",
                       n_heads=N_HEADS, n_real=MLA_QK, rope=rope)
    s["ka"] = prep_fwd([(s["kv_raw"], 128, lambda h: 2 * h), (z, 128, lambda h: Z_KR // 128)], P["g_mla_k"],
                       name="prep_mla_k", n_heads=N_HEADS, n_real=MLA_QK, rope=rope)
    s["o_a"], s["lse_a"] = attn_fwd(s["qa"], s["ka"], s["kv_raw"], lambda h: 2 * h + 1, mode="mla",
                                    name="attn_mla", n_heads=N_HEADS)
    s["qb"] = prep_fwd([(z, DH, lambda h: FOX_B + h)], P["g_fox_q"], name="prep_h", n_heads=N_HEADS, n_real=DH)
    s["kb"] = prep_fwd([(z, DH, lambda h: FOX_B + N_HEADS + h)], P["g_fox_k"], name="prep_h",
                       n_heads=N_HEADS, n_real=DH)
    _, cum_t = fox_cum_fwd(z, P["b_f"])
    s["cq"] = cum_t[:N_HEADS].reshape(N_HEADS, S, 1)
    s["ck"] = cum_t[:N_HEADS].reshape(N_HEADS, 1, S)
    s["o_b"], s["lse_b"] = attn_fwd(s["qb"], s["kb"], z, lambda h: FOX_B + 2 * N_HEADS + h, mode="fox",
                                    name="attn_fox", n_heads=N_HEADS, cq=s["cq"], ck=s["ck"])
    s["qc"] = prep_fwd([(z, DH, lambda h: CH_B + h)], P["g_ch_q"], name="prep_h", n_heads=N_HEADS, n_real=DH)
    s["kc"] = prep_fwd([(z, DH, lambda h: CH_B + N_HEADS + h)], P["g_ch_k"], name="prep_h",
                       n_heads=N_HEADS, n_real=DH)
    s["tiles"] = relbias_tiles(P["rel_bias"], P["rel_onehot"])
    s["o_c"], s["lse_c"] = attn_fwd(s["qc"], s["kc"], z, lambda h: CH_B + 2 * N_HEADS + h, mode="chunk",
                                    name="attn_chunk", n_heads=N_HEADS, tiles=s["tiles"])
    late, token = late_weights(s["o_c"])
    W = dict(W, **late)
    P = dict(P, g_cross=_after(P["g_cross"], token))
    s["projs"] = [mm_nn(o, W["br"][n], name="mm_br", out_dtype=F32)
                  for n, o in enumerate((s["o_a"], s["o_b"], s["o_c"]))]
    s["merged"] = merge_fwd(z, s["projs"])
    x1 = s["x1"] = mm_nn(s["merged"], W["out"], name="mm_out", out_dtype=F32, res=x)
    s["hq"] = rms_fwd(x1, P["g_cross"], name="rms_d")
    s["xq_raw"] = mm_nn(s["hq"], W["xq"], name="mm_xq", out_dtype=F32)
    s["mem_n"] = rms_fwd(mem, P["g_mem"], name="rms_mem")
    s["mkv"] = mm_nn(s["mem_n"], W["xkv"], name="mm_xkv", out_dtype=F32)
    s["qx"] = prep_fwd([(s["xq_raw"], DH, lambda h: h)], P["g_x_q"], name="prep_xq", n_heads=X_HEADS, n_real=DH)
    s["kx"] = prep_fwd([(s["mkv"], DH, lambda h: h)], P["g_x_k"], name="prep_xk", n_heads=X_HEADS, n_real=DH)
    s["o_x"], s["lse_x"] = attn_fwd(s["qx"], s["kx"], s["mkv"], lambda h: X_HEADS + h, mode="cross",
                                    name="attn_cross", n_heads=X_HEADS)
    x2 = s["x2"] = mm_nn(s["o_x"], W["xo"], name="mm_xo", out_dtype=F32, res=x1)
    s["hm"] = rms_fwd(x2, P["g_mlp"], name="rms_d")
    s["a1"], s["act"] = mm_nn(s["hm"], W["w1"], name="mm_w1", out_dtype=BF16, relu2=True)
    x3 = mm_nn(s["act"], W["w2"], name="mm_w2", out_dtype=F32, res=x2)
    return x3, s, W


LATE_GRADS = ("w2", "w1", "xo", "xq", "xkv", "out", "br")


def layer_bwd(g, x, mem, W, P, rope, s, late_grads_done, early_grads_done):
    S = x.shape[0]
    z = s["z"]
    dw, ds = {}, {}
    da1 = mm_nt(g, W["w2"], name="mm_w2_dx", out_dtype=BF16, relu_mul=s["a1"])
    dw["w2"] = mm_tn(s["act"], g, nb=1, name="mm_w2_dw", out_dtype=BF16)
    dhm = mm_nt(da1, W["w1"], name="mm_w1_dx", out_dtype=F32)
    dw["w1"] = mm_tn(s["hm"], da1, nb=4, name="mm_w1_dw", out_dtype=BF16)
    g2, ds["g_mlp"] = rms_bwd(s["x2"], P["g_mlp"], dhm, name="rms_d_bwd", res=g)
    do_x = mm_nt(g2, W["xo"], name="mm_xo_dx", out_dtype=BF16)
    dw["xo"] = mm_tn(s["o_x"], g2, nb=4, name="mm_xo_dw", out_dtype=BF16)
    dqx, dkx, dvx = attn_bwd(s["qx"], s["kx"], s["mkv"], lambda h: X_HEADS + h, s["o_x"], do_x, s["lse_x"],
                             mode="cross", name="attn_cross_bwd", n_heads=X_HEADS)
    dxq_raw, ds["g_x_q"] = prep_bwd_q(s["xq_raw"], dqx, P["g_x_q"], name="prep_xq_bwd", n_heads=X_HEADS,
                                      dh=DH, n_real=DH)
    dmkv, dgk = prep_bwd_groups(s["mkv"], 0, [dkx, dvx], [P["g_x_k"], None], name="prep_xkv_bwd",
                                n_heads=X_HEADS, kinds=("norm", "copy"))
    ds["g_x_k"] = dgk[0]
    dhq = mm_nt(dxq_raw, W["xq"], name="mm_xq_dx", out_dtype=F32)
    dw["xq"] = mm_tn(s["hq"], dxq_raw, nb=1, name="mm_xq_dw", out_dtype=BF16)
    dmem_n = mm_nt(dmkv, W["xkv"], name="mm_xkv_dx", out_dtype=F32)
    dw["xkv"] = mm_tn(s["mem_n"], dmkv, nb=1, name="mm_xkv_dw", out_dtype=BF16)
    _, ds["g_mem"] = rms_bwd(mem, P["g_mem"], dmem_n, name="rms_mem_bwd", need_dx=False)
    g1, ds["g_cross"] = rms_bwd(s["x1"], P["g_cross"], dhq, name="rms_d_bwd", res=g2)
    dmerged = mm_nt(g1, W["out"], name="mm_out_dx", out_dtype=F32)
    dw["out"] = mm_tn(s["merged"], g1, nb=1, name="mm_out_dw", out_dtype=BF16)
    dgl, dproj = merge_bwd(z, s["projs"], dmerged)
    outs = (s["o_a"], s["o_b"], s["o_c"])
    do = [mm_nt(dproj[n], W["br"][n], name="mm_br_dx", out_dtype=BF16) for n in range(3)]
    dw["br"] = [mm_tn(outs[n], dproj[n], nb=4, name="mm_br_dw", out_dtype=BF16) for n in range(3)]
    token = late_grads_done({k: dw.pop(k) for k in LATE_GRADS})
    dqc, dkc, dvc, dtiles = attn_bwd(s["qc"], s["kc"], z, lambda h: CH_B + 2 * N_HEADS + h, s["o_c"], do[2],
                                     s["lse_c"], mode="chunk", name="attn_chunk_bwd", n_heads=N_HEADS,
                                     tiles=_after(s["tiles"], token))
    d_ch, dg_ch = prep_bwd_groups(z, CH_B, [dqc, dkc, dvc], [P["g_ch_q"], P["g_ch_k"], None],
                                  name="prep_h_bwd", n_heads=N_HEADS, kinds=("norm", "norm", "copy"))
    ds["g_ch_q"], ds["g_ch_k"] = dg_ch[0], dg_ch[1]
    ds["rel_bias"] = relbias_tiles_bwd(dtiles, P["rel_onehot"])
    dqb, dkb, dvb, dcq, dck = attn_bwd(s["qb"], s["kb"], z, lambda h: FOX_B + 2 * N_HEADS + h, s["o_b"], do[1],
                                       s["lse_b"], mode="fox", name="attn_fox_bwd", n_heads=N_HEADS,
                                       cq=s["cq"], ck=s["ck"])
    d_fox, dg_fox = prep_bwd_groups(z, FOX_B, [dqb, dkb, dvb], [P["g_fox_q"], P["g_fox_k"], None],
                                    name="prep_h_bwd", n_heads=N_HEADS, kinds=("norm", "norm", "copy"))
    ds["g_fox_q"], ds["g_fox_k"] = dg_fox[0], dg_fox[1]
    dcum = jnp.pad((dcq[:, :, 0] + dck[:, 0, :]).T, ((0, 0), (0, 128 - N_HEADS)))
    dff, dbf = fox_cum_bwd(z, P["b_f"], dcum)
    ds["b_f"] = dbf[:, :N_HEADS]
    dqa, dka, dva = attn_bwd(s["qa"], s["ka"], s["kv_raw"], lambda h: 2 * h + 1, s["o_a"], do[0], s["lse_a"],
                             mode="mla", name="attn_mla_bwd", n_heads=N_HEADS)
    dq_raw, dgq = prep_bwd_q(s["q_raw"], dqa, P["g_mla_q"], name="prep_mla_q_bwd", n_heads=N_HEADS,
                             dh=MLA_PAD, n_real=MLA_QK, rope=rope)
    dkv_raw, dkr, dgk = prep_bwd_mla_k(s["kv_raw"], z, dka, dva, P["g_mla_k"], rope, name="prep_mla_k_bwd")
    ds["g_mla_q"], ds["g_mla_k"] = dgq[:, :MLA_QK], dgk[:, :MLA_QK]
    dcq_n = mm_nt(dq_raw, W["uq_p"], name="mm_uq_dx", out_dtype=F32)
    dw["uq_p"] = mm_tn(s["cq_n"], dq_raw, nb=1, name="mm_uq_dw", out_dtype=BF16)
    dckv_n = mm_nt(dkv_raw, W["ukv"], name="mm_ukv_dx", out_dtype=F32)
    dw["ukv"] = mm_tn(s["ckv_n"], dkv_raw, nb=4, name="mm_ukv_dw", out_dtype=BF16)
    d_cq, ds["g_cq"] = rms_bwd(z, P["g_cq"], dcq_n, name="rms_cq_bwd", col0=Z_CQ, width=Q_LORA, dx_dtype=BF16)
    d_ckv, ds["g_ckv"] = rms_bwd(z, P["g_ckv"], dckv_n, name="rms_ckv_bwd", col0=Z_CKV, width=KV_LORA,
                                 dx_dtype=BF16)
    dz = jnp.concatenate([d_cq, d_ckv, dkr.astype(BF16), dff.astype(BF16), d_fox, d_ch, *dgl], axis=1)
    dw["in_p"] = mm_tn(s["h"], dz, nb=1, name="mm_in_dw", out_dtype=BF16)
    token = early_grads_done(dw)
    dh = mm_nt(dz, W["in_p"], name="mm_in_dx", out_dtype=F32, after=token)
    g0, ds["g_mix"] = rms_bwd(x, P["g_mix"], dh, name="rms_d_bwd", res=g1)
    return g0, ds, token


def local_step(x, mem, target, Ps, weights_of, grads_done):
    rope = rope_tables(x.shape[0])
    onehot = relbias_onehot()
    Ps = [dict(P, rel_onehot=onehot) for P in Ps]
    Ws, saved, xs = [], [], [x]
    for l, P in enumerate(Ps):
        W_early, late_weights, token = weights_of(l, xs[-1])
        y, s, W = layer_fwd(xs[-1], mem, W_early, dict(P, g_mix=_after(P["g_mix"], token)), rope, late_weights)
        xs.append(y)
        saved.append(s)
        Ws.append(W)
    g, loss = loss_head(xs[-1], target)
    dss, token = [], None
    for l in reversed(range(len(Ps))):
        g, ds, token = layer_bwd(g, xs[l], mem, Ws[l], dict(Ps[l], g_mlp=_after(Ps[l]["g_mlp"], token)), rope,
                                 saved[l], functools.partial(grads_done, l, "late"),
                                 functools.partial(grads_done, l, "early"))
        dss.append(ds)
    return loss, g, dss[::-1]


HBM_SPEC = pl.BlockSpec(memory_space=pltpu.HBM)


def _place():
    return lax.axis_index("x"), lax.axis_index("y"), lax.axis_index("c")


def _other_chips(x, y):
    return [(1 - x, y), (x, 1 - y), (1 - x, 1 - y)]


def _remote(src, dst, send_sems, recv_sems, k, to):
    return pltpu.make_async_remote_copy(src_ref=src, dst_ref=dst, send_sem=send_sems.at[k],
                                        recv_sem=recv_sems.at[k], device_id=to, device_id_type=MESH)


def _comm_call(body, *, name, ins, out_shapes, n_sem):
    return pl.pallas_call(
        body, out_shape=tuple(out_shapes), in_specs=[HBM_SPEC] * len(ins),
        out_specs=tuple([HBM_SPEC] * len(out_shapes)),
        scratch_shapes=[pltpu.SemaphoreType.DMA((n_sem,)), pltpu.SemaphoreType.DMA((n_sem,))],
        name=name, interpret=False)(*ins)


SEM_SPEC = pl.BlockSpec(memory_space=pltpu.SEMAPHORE)
SPLIT_EFFECT = pltpu.SideEffectType.DATAFLOW_SIDE_EFFECTING


def _chip_exchange_copies(kind, srcs, lands, send_sems, recv_sems):
    x, y, c = _place()
    me = 2 * x + y
    cps = []
    for i, (src, land) in enumerate(zip(srcs, lands)):
        for j, (cx, cy) in enumerate(_other_chips(x, y)):
            if kind == "gather":
                h = src.shape[0] // 2
                s_ref, d_ref = src.at[pl.ds(c * h, h), :], land.at[me, pl.ds(c * h, h), :]
            else:
                s_ref, d_ref = src.at[2 * cx + cy], land.at[me]
            cps.append(_remote(s_ref, d_ref, send_sems, recv_sems, 3 * i + j, (cx, cy, c)))
    return cps


def chip_exchange_start(kind, srcs, land_shapes, *, name, after=None):
    n = len(srcs)
    n_in = 2 * n + (after is not None)

    def body(*refs):
        send_sems, recv_sems = refs[n_in], refs[n_in + 1]
        for cp in _chip_exchange_copies(kind, refs[:n], refs[n:2 * n], send_sems, recv_sems):
            cp.start()
        refs[-1][...] = jnp.zeros_like(refs[-1])

    hbm = lambda a: pltpu.with_memory_space_constraint(a, pltpu.HBM)
    ins = [hbm(s) for s in srcs] + [hbm(lax.empty(s.shape, s.dtype)) for s in land_shapes]
    sems = pltpu.SemaphoreType.DMA((3 * n,))
    out = pl.pallas_call(
        body, name=name, interpret=False,
        out_shape=(sems, sems, *[pltpu.HBM(a.shape, a.dtype) for a in ins], _sds((8, 128), F32)),
        in_specs=[HBM_SPEC] * (2 * n) + [pl.BlockSpec(memory_space=pl.ANY)] * (after is not None),
        out_specs=(SEM_SPEC, SEM_SPEC, *[HBM_SPEC] * (2 * n), pl.BlockSpec(memory_space=pltpu.VMEM)),
        input_output_aliases={i: 2 + i for i in range(2 * n)},
        compiler_params=pltpu.CompilerParams(has_side_effects=SPLIT_EFFECT))(
            *ins, *([after] if after is not None else []))
    return out[0], out[1], list(out[2:2 + n]), list(out[2 + n:2 + 2 * n]), out[-1]


def chip_exchange_wait(kind, started, after, *, name):
    send_sems, recv_sems, srcs, lands, _ = started
    n = len(srcs)

    def body(*refs):
        send_sems, recv_sems = refs[2 * n], refs[2 * n + 1]
        for cp in _chip_exchange_copies(kind, refs[:n], refs[n:2 * n], send_sems, recv_sems):
            cp.wait_send()
            cp.wait_recv()

    thru = srcs + lands
    out = pl.pallas_call(
        body, name=name, interpret=False,
        out_shape=tuple(pltpu.HBM(a.shape, a.dtype) for a in thru),
        in_specs=[HBM_SPEC] * (2 * n) + [SEM_SPEC, SEM_SPEC, pl.BlockSpec(memory_space=pl.ANY)],
        out_specs=tuple([HBM_SPEC] * (2 * n)),
        input_output_aliases={i: i for i in range(2 * n)},
        compiler_params=pltpu.CompilerParams(has_side_effects=SPLIT_EFFECT))(*thru, send_sems, recv_sems, after)
    return list(out[:n]), list(out[n:])


def gather_to_sibling(lands):
    n = len(lands)

    def body(*refs):
        outs = refs[n:2 * n]
        send_sems, recv_sems = refs[2 * n:]
        x, y, c = _place()
        sends = []
        for i in range(n):
            h = outs[i].shape[1] // 2
            for j, (cx, cy) in enumerate(_other_chips(x, y)):
                landed = outs[i].at[2 * cx + cy, pl.ds(c * h, h), :]
                cp = _remote(landed, landed, send_sems, recv_sems, 3 * i + j, (x, y, 1 - c))
                cp.start()
                sends.append(cp)
        for cp in sends:
            cp.wait()

    return pl.pallas_call(
        body, out_shape=tuple(_sds(a.shape, a.dtype) for a in lands), in_specs=[HBM_SPEC] * n,
        out_specs=tuple([HBM_SPEC] * n), input_output_aliases={i: i for i in range(n)},
        scratch_shapes=[pltpu.SemaphoreType.DMA((3 * n,)), pltpu.SemaphoreType.DMA((3 * n,))],
        name="gather_to_sibling", interpret=False)(*lands)


def rs_to_sibling(parts):
    n = len(parts)

    def body(*refs):
        srcs, outs = refs[:n], refs[n:2 * n]
        send_sems, recv_sems = refs[2 * n:]
        x, y, c = _place()
        sends = []
        for i in range(n):
            h = srcs[i].shape[1] // 2
            cp = _remote(srcs[i].at[:, pl.ds((1 - c) * h, h), :], outs[i], send_sems, recv_sems, i, (x, y, 1 - c))
            cp.start()
            sends.append(cp)
        for cp in sends:
            cp.wait()

    return _comm_call(body, name="rs_to_sibling", ins=parts, n_sem=n,
                      out_shapes=[_sds((4, p.shape[1] // 2, p.shape[2]), p.dtype) for p in parts])


def rs_share_halves(ts):
    n = len(ts)

    def body(*refs):
        srcs, outs = refs[:n], refs[n:2 * n]
        send_sems, recv_sems = refs[2 * n:]
        x, y, c = _place()
        sends = []
        for i in range(n):
            cp = _remote(srcs[i], outs[i].at[c], send_sems, recv_sems, i, (x, y, 1 - c))
            cp.start()
            sends.append(cp)
        for cp in sends:
            cp.wait()

    return _comm_call(body, name="rs_share_halves", ins=ts, n_sem=n,
                      out_shapes=[_sds((2,) + t.shape, t.dtype) for t in ts])


def add_pair(part, got, half_idx):
    _, a, b = part.shape
    h = a // 2
    tr = _pick(h, (512, 256, 128))

    def body(c_ref, p_ref, g_ref, o_ref):
        o_ref[...] = (p_ref[...].astype(F32) + g_ref[...].astype(F32)).astype(o_ref.dtype)

    spec = pl.BlockSpec((None, tr, b), lambda s_, i, c_ref: (s_, i, 0))
    grid_spec = pltpu.PrefetchScalarGridSpec(
        num_scalar_prefetch=1, grid=(4, h // tr),
        in_specs=[pl.BlockSpec((None, None, tr, b), lambda s_, i, c_ref: (s_, c_ref[0], i, 0)), spec],
        out_specs=spec)
    return pl.pallas_call(
        body, out_shape=_sds((4, h, b), part.dtype), grid_spec=grid_spec, name="rs_add_pair", interpret=False,
        compiler_params=pltpu.CompilerParams(vmem_limit_bytes=VMEM_LIMIT_BYTES))(
            half_idx, part.reshape(4, 2, h, b), got)


def sum_slots(r):
    _, h, b = r.shape
    tr = _pick(h, (512, 256, 128))

    def body(r0, r1, r2, r3, o_ref):
        o_ref[...] = ((r0[...].astype(F32) + r1[...].astype(F32)) + r2[...].astype(F32)) + r3[...].astype(F32)

    specs = [pl.BlockSpec((None, tr, b), functools.partial(lambda i, s_: (s_, i, 0), s_=s_)) for s_ in range(4)]
    return _pcall(body, name="rs_sum_slots", out_shape=_sds((h, b), F32), grid=(h // tr,),
                  in_specs=specs, out_specs=pl.BlockSpec((tr, b), lambda i: (i, 0)))(r, r, r, r)


def reduce_scatter_start(parts, core, *, name):
    half_idx = core.reshape(1).astype(jnp.int32)
    gots = rs_to_sibling(parts)
    chip_sums = [add_pair(p, g, half_idx) for p, g in zip(parts, gots)]
    return chip_exchange_start("scatter", chip_sums, chip_sums, name=name)


def reduce_scatter_finish(started, after, chip, core, *, name):
    chip_sums, slots = chip_exchange_wait("scatter", started, after, name=name)
    slots = [lax.dynamic_update_slice(s_, lax.dynamic_index_in_dim(q, chip, 0, keepdims=True), (chip, 0, 0))
             for s_, q in zip(slots, chip_sums)]
    halves = [sum_slots(s_) for s_ in slots]
    both = rs_share_halves(halves)
    both = [lax.dynamic_update_slice(o, t[None], (core, 0, 0)) for o, t in zip(both, halves)]
    return [o.reshape(2 * o.shape[1], o.shape[2]) for o in both]


def allreduce_small(v):
    Rs = v.shape[0]

    def body(v_ref, o_ref, buf, send_sems, recv_sems):
        x, y, c = _place()
        me = 4 * x + 2 * y + c
        buf[me] = v_ref[...]
        flips = [(fx, fy, fc) for fx in (0, 1) for fy in (0, 1) for fc in (0, 1)][1:]
        sends = []
        for k, (fx, fy, fc) in enumerate(flips):
            to = ((1 - x) if fx else x, (1 - y) if fy else y, (1 - c) if fc else c)
            cp = _remote(v_ref, buf.at[me], send_sems, recv_sems, k, to)
            cp.start()
            sends.append(cp)
        for cp in sends:
            cp.wait()
        acc = buf[0]
        for d in range(1, 8):
            acc = acc + buf[d]
        o_ref[...] = acc

    vm = pl.BlockSpec(memory_space=pltpu.VMEM)
    return pl.pallas_call(
        body, out_shape=_sds((Rs, 128), F32), in_specs=[vm], out_specs=vm,
        scratch_shapes=[pltpu.VMEM((8, Rs, 128), F32), pltpu.SemaphoreType.DMA((7,)),
                        pltpu.SemaphoreType.DMA((7,))],
        name="allreduce_small", interpret=False)(v)


INPUT_NAMES = (("x", "mem") + WEIGHT_ORDER + ("loss_target",) + tuple("m_" + n for n in WEIGHT_ORDER)
               + tuple("v_" + n for n in WEIGHT_ORDER))


def _pack_small(vals, n_layers, extra=None):
    flat = jnp.concatenate([vals[n].reshape(n_layers, -1).astype(F32) for n in SMALL_ORDER], axis=1).reshape(-1)
    if extra is not None:
        flat = jnp.concatenate([flat, extra.reshape(-1)])
    n = flat.shape[0]
    rows = -(-n // 1024) * 8
    return jnp.pad(flat, (0, rows * 128 - n)).reshape(rows, 128)


def _unpack_small(packed, like, n_layers):
    per_layer = sum(int(np.prod(like[n].shape[1:])) for n in SMALL_ORDER)
    body = packed.reshape(-1)[:n_layers * per_layer].reshape(n_layers, per_layer)
    out, off = {}, 0
    for n in SMALL_ORDER:
        k = int(np.prod(like[n].shape[1:]))
        out[n] = body[:, off:off + k].reshape(like[n].shape)
        off += k
    return out, packed.reshape(-1)[n_layers * per_layer]


def kernel(x, mem, g_mix, w_in, g_cq, w_uq, g_ckv, w_ukv, g_mla_q, g_mla_k, b_f, g_fox_q, g_fox_k, rel_bias, g_ch_q, g_ch_k, w_br, w_out, g_cross, g_mem, w_xq, w_xkv, g_x_q, g_x_k, w_xo, g_mlp, w_1, w_2, loss_target, m_g_mix, m_w_in, m_g_cq, m_w_uq, m_g_ckv, m_w_ukv, m_g_mla_q, m_g_mla_k, m_b_f, m_g_fox_q, m_g_fox_k, m_rel_bias, m_g_ch_q, m_g_ch_k, m_w_br, m_w_out, m_g_cross, m_g_mem, m_w_xq, m_w_xkv, m_g_x_q, m_g_x_k, m_w_xo, m_g_mlp, m_w_1, m_w_2, v_g_mix, v_w_in, v_g_cq, v_w_uq, v_g_ckv, v_w_ukv, v_g_mla_q, v_g_mla_k, v_b_f, v_g_fox_q, v_g_fox_k, v_rel_bias, v_g_ch_q, v_g_ch_k, v_w_br, v_w_out, v_g_cross, v_g_mem, v_w_xq, v_w_xkv, v_g_x_q, v_g_x_k, v_w_xo, v_g_mlp, v_w_1, v_w_2):
    d = dict(zip(INPUT_NAMES, (x, mem, g_mix, w_in, g_cq, w_uq, g_ckv, w_ukv, g_mla_q, g_mla_k, b_f, g_fox_q, g_fox_k, rel_bias, g_ch_q, g_ch_k, w_br, w_out, g_cross, g_mem, w_xq, w_xkv, g_x_q, g_x_k, w_xo, g_mlp, w_1, w_2, loss_target, m_g_mix, m_w_in, m_g_cq, m_w_uq, m_g_ckv, m_w_ukv, m_g_mla_q, m_g_mla_k, m_b_f, m_g_fox_q, m_g_fox_k, m_rel_bias, m_g_ch_q, m_g_ch_k, m_w_br, m_w_out, m_g_cross, m_g_mem, m_w_xq, m_w_xkv, m_g_x_q, m_g_x_k, m_w_xo, m_g_mlp, m_w_1, m_w_2, v_g_mix, v_w_in, v_g_cq, v_w_uq, v_g_ckv, v_w_ukv, v_g_mla_q, v_g_mla_k, v_b_f, v_g_fox_q, v_g_fox_k, v_rel_bias, v_g_ch_q, v_g_ch_k, v_w_br, v_w_out, v_g_cross, v_g_mem, v_w_xq, v_w_xkv, v_g_x_q, v_g_x_k, v_w_xo, v_g_mlp, v_w_1, v_w_2)))
    n_layers = g_mix.shape[0]
    assert x.shape[0] == 1 and x.shape[2] == D_MODEL and mem.shape[1:] == (MEM_LEN, D_MODEL)
    for n in PACK_ORDER:
        assert d[n].shape[1:] == SHARD_SHAPES[n], (n, d[n].shape)

    chip = 2 * lax.axis_index("x") + lax.axis_index("y")
    core = lax.axis_index("c")

    def start_gather(l, names, after):
        mine = [_rows2d(d[n][l].astype(BF16), n) for n in names]
        return chip_exchange_start("gather", mine, [_sds((4,) + m_.shape, BF16) for m_ in mine],
                                   name=f"gather_start_{l}_{names[0]}", after=after)

    def finish_gather(l, names, started, after):
        mine, lands = chip_exchange_wait("gather", started, after, name=f"gather_wait_{l}_{names[0]}")
        full = [lax.dynamic_update_slice(t, m_[None], (chip, 0, 0)) for t, m_ in zip(gather_to_sibling(lands), mine)]
        return mine, full_weights({n: f.reshape((4,) + SHARD_SHAPES[n]) for n, f in zip(names, full)})

    in_flight = {"early": start_gather(0, EARLY_WEIGHTS, x)}

    def weights_of(l, x_l):
        mine, early = finish_gather(l, EARLY_WEIGHTS, in_flight.pop("early"), x_l)
        in_flight["late"] = start_gather(l, LATE_WEIGHTS, mine[0])

        def late_weights(after):
            mine, late = finish_gather(l, LATE_WEIGHTS, in_flight.pop("late"), after)
            if l + 1 == n_layers:
                return late, None
            in_flight["early"] = start_gather(l + 1, EARLY_WEIGHTS, mine[0])
            return late, in_flight["early"][4]

        return early, late_weights, in_flight["late"][4]

    scatters = []

    def grads_done(l, group, dw):
        names = LATE_WEIGHTS if group == "late" else EARLY_WEIGHTS
        sg = shard_grads(dw)
        started = reduce_scatter_start([_rows2d(sg[n], n) for n in names], core, name=f"scatter_start_{l}_{names[0]}")
        scatters.append((l, names, started))
        return started[4]

    Ps = [layer_params(d, l) for l in range(n_layers)]
    loss, dx, dss = local_step(x[0], mem[0], loss_target[0], Ps, weights_of, grads_done)
    big = [{} for _ in range(n_layers)]
    for l, names, started in scatters:
        done = reduce_scatter_finish(started, dx, chip, core, name=f"scatter_wait_{l}_{names[0]}")
        big[l].update(zip(names, done))

    small_local = {n: jnp.stack([dss[l][n].reshape(d[n].shape[1:]) for l in range(n_layers)]) for n in SMALL_ORDER}
    small_sum, loss_sum = _unpack_small(allreduce_small(_pack_small(small_local, n_layers, extra=loss)),
                                        {n: d[n] for n in SMALL_ORDER}, n_layers)

    grads, delta, new_m, new_v = {}, {}, {}, {}
    for n in PACK_ORDER:
        outs = adamw(_rows2d(d[n], n), [big[l][n] for l in range(n_layers)], _rows2d(d["m_" + n], n),
                     _rows2d(d["v_" + n], n), name="adamw_" + n)
        grads[n], delta[n], new_m[n], new_v[n] = (o.reshape(d[n].shape) for o in outs)
    like = {n: d[n] for n in SMALL_ORDER}
    sm = adamw(_pack_small(like, n_layers)[None], [_pack_small(small_sum, n_layers)],
               _pack_small({n: d["m_" + n] for n in SMALL_ORDER}, n_layers)[None],
               _pack_small({n: d["v_" + n] for n in SMALL_ORDER}, n_layers)[None], name="adamw_small")
    for res, src in zip((grads, delta, new_m, new_v), sm):
        res.update(_unpack_small(src[0], like, n_layers)[0])

    return (loss_sum, dx[None], *[grads[n] for n in WEIGHT_ORDER], *[delta[n] for n in WEIGHT_ORDER],
            *[new_m[n] for n in WEIGHT_ORDER], *[new_v[n] for n in WEIGHT_ORDER])
```

```python
import functools

import numpy as np
import jax
import jax.numpy as jnp
from jax import lax
from jax.experimental import pallas as pl
from jax.experimental.pallas import tpu as pltpu

F32 = jnp.float32
BF16 = jnp.bfloat16
MXU_DTYPE = jnp.bfloat16
MESH = pl.DeviceIdType.MESH

D_MODEL = 2048
MIX_W = 1024
N_HEADS = 8
DH = 128
MLA_NOPE = 128
MLA_ROPE = 64
MLA_QK = MLA_NOPE + MLA_ROPE
MLA_PAD = 256
Q_LORA = 512
KV_LORA = 256
CHUNK = 64
CHUNK_SHIFT = CHUNK.bit_length() - 1
LEFT_CHUNKS = 8
REL_CLIP = 128
N_REL = 2 * REL_CLIP + 1
X_HEADS = 4
MEM_LEN = 256
D_FF = 8192
ROPE_THETA = 10000.0
EPS = 1e-6
NEG = -1e30

Z_CQ, Z_CKV, Z_KR, Z_FF = 0, 512, 768, 896
ZS_W = 1024
Z_FOX = 1024
Z_CH = Z_FOX + 3 * MIX_W
Z_GATE = Z_CH + 3 * MIX_W
Z_TOT = Z_GATE + 3 * D_MODEL
W_IN_CUTS = (0, 512, 768, 832, 3904, 3912, 6984, 13128)
W_IN_SHARD = W_IN_CUTS[-1] // 4
W_IN_PIECES = ((0, 0, 832), (Z_FF, 3904, 8), (Z_FOX, 832, 3072), (Z_CH, 3912, 9216))

ADAM_LR, ADAM_B1, ADAM_B2, ADAM_EPS, ADAM_WD, ADAM_STEP = 0.001, 0.9, 0.999, 1e-08, 0.01, 10

VMEM_LIMIT_BYTES = 56 * 1024 * 1024
PACK_ORDER = ("w_uq", "w_ukv", "w_br", "w_out", "w_xq", "w_xkv", "w_xo", "w_1", "w_2", "w_in")
SMALL_ORDER = ("g_mix", "g_cq", "g_ckv", "g_mla_q", "g_mla_k", "b_f", "g_fox_q", "g_fox_k", "rel_bias",
               "g_ch_q", "g_ch_k", "g_cross", "g_mem", "g_x_q", "g_x_k", "g_mlp")
WEIGHT_ORDER = ("g_mix", "w_in", "g_cq", "w_uq", "g_ckv", "w_ukv", "g_mla_q", "g_mla_k", "b_f", "g_fox_q",
                "g_fox_k", "rel_bias", "g_ch_q", "g_ch_k", "w_br", "w_out", "g_cross", "g_mem", "w_xq",
                "w_xkv", "g_x_q", "g_x_k", "w_xo", "g_mlp", "w_1", "w_2")


def _pick(n, prefs):
    for p in prefs:
        if n % p == 0:
            return p
    raise ValueError(f"no block size among {prefs} divides {n}")


def _pcall(body, *, name, out_shape, in_specs, out_specs, grid=(), scratch=(), aliases=None):
    return pl.pallas_call(
        body, out_shape=out_shape, grid=grid, in_specs=in_specs, out_specs=out_specs,
        scratch_shapes=scratch, name=name, interpret=False,
        input_output_aliases=aliases or {},
        compiler_params=pltpu.CompilerParams(vmem_limit_bytes=VMEM_LIMIT_BYTES))


def _sds(shape, dtype):
    return jax.ShapeDtypeStruct(tuple(shape), dtype)


def _mx(v):
    return v.astype(MXU_DTYPE)


def mm_nn(a, b3, *, name, out_dtype, a_col0=0, res=None, relu2=False):
    M = a.shape[0]
    nb, K, Ns = b3.shape
    N = nb * Ns
    tm = _pick(M, (1024, 512, 256, 128))
    tk = _pick(K, (2048, 1024, 512, 256))
    tn = _pick(Ns, (512, 256, 128))
    assert a_col0 % tk == 0
    nk, nbs, ka0 = K // tk, Ns // tn, a_col0 // tk
    n_out = 2 if relu2 else 1

    def body(*refs):
        a_ref, b_ref = refs[0], refs[1]
        pos = 2
        res_ref = None
        if res is not None:
            res_ref = refs[pos]
            pos += 1
        outs = refs[pos:pos + n_out]
        acc_ref = refs[pos + n_out] if nk > 1 else None
        part = jnp.dot(_mx(a_ref[...]), _mx(b_ref[...]), preferred_element_type=F32)

        def finish(acc):
            if res_ref is not None:
                acc = acc + res_ref[...]
            outs[0][...] = acc.astype(outs[0].dtype)
            if relu2:
                r = jnp.maximum(acc, 0.0)
                outs[1][...] = (r * r).astype(outs[1].dtype)

        if nk == 1:
            finish(part)
        else:
            k = pl.program_id(2)

            @pl.when(k == 0)
            def _():
                acc_ref[...] = part

            @pl.when(k > 0)
            def _():
                acc_ref[...] += part

            @pl.when(k == nk - 1)
            def _():
                finish(acc_ref[...])

    in_specs = [pl.BlockSpec((tm, tk), lambda i, j, k: (i, ka0 + k)),
                pl.BlockSpec((None, tk, tn), lambda i, j, k: (j // nbs, k, j % nbs))]
    args = [a, b3]
    if res is not None:
        in_specs.append(pl.BlockSpec((tm, tn), lambda i, j, k: (i, j)))
        args.append(res)
    o_spec = pl.BlockSpec((tm, tn), lambda i, j, k: (i, j))
    if relu2:
        out_shape, out_specs = (_sds((M, N), out_dtype), _sds((M, N), out_dtype)), (o_spec, o_spec)
    else:
        out_shape, out_specs = _sds((M, N), out_dtype), o_spec
    scratch = (pltpu.VMEM((tm, tn), F32),) if nk > 1 else ()
    return _pcall(body, name=name, out_shape=out_shape, grid=(M // tm, N // tn, nk),
                  in_specs=in_specs, out_specs=out_specs, scratch=scratch)(*args)


def mm_nt(a, b3, *, name, out_dtype, a_col0=0, res=None, relu_mul=None, after=None):
    M = a.shape[0]
    nb, K, Ns = b3.shape
    tm = _pick(M, (1024, 512, 256, 128))
    tk = _pick(K, (1024, 512, 256))
    tn = _pick(Ns, (2048, 1024, 512, 256, 128) if nb == 1 and Ns <= 2048 else (1024, 512, 256, 128))
    assert a_col0 % tn == 0
    nbs = Ns // tn
    nn, a0 = nb * nbs, a_col0 // tn

    def body(*refs):
        a_ref, b_ref = refs[0], refs[1]
        pos = 2
        mul_ref = res_ref = None
        if relu_mul is not None:
            mul_ref = refs[pos]
            pos += 1
        if res is not None:
            res_ref = refs[pos]
            pos += 1
        pos += after is not None
        o_ref = refs[pos]
        acc_ref = refs[pos + 1] if nn > 1 else None
        part = lax.dot_general(_mx(a_ref[...]), _mx(b_ref[...]), (((1,), (1,)), ((), ())),
                               preferred_element_type=F32)

        def finish(acc):
            if mul_ref is not None:
                acc = acc * (2.0 * jnp.maximum(mul_ref[...].astype(F32), 0.0))
            if res_ref is not None:
                acc = acc + res_ref[...]
            o_ref[...] = acc.astype(o_ref.dtype)

        if nn == 1:
            finish(part)
        else:
            j = pl.program_id(2)

            @pl.when(j == 0)
            def _():
                acc_ref[...] = part

            @pl.when(j > 0)
            def _():
                acc_ref[...] += part

            @pl.when(j == nn - 1)
            def _():
                finish(acc_ref[...])

    in_specs = [pl.BlockSpec((tm, tn), lambda i, kk, j: (i, a0 + j)),
                pl.BlockSpec((None, tk, tn), lambda i, kk, j: (j // nbs, kk, j % nbs))]
    args = [a, b3]
    for extra in (relu_mul, res):
        if extra is not None:
            in_specs.append(pl.BlockSpec((tm, tk), lambda i, kk, j: (i, kk)))
            args.append(extra)
    if after is not None:
        in_specs.append(pl.BlockSpec(memory_space=pl.ANY))
        args.append(after)
    scratch = (pltpu.VMEM((tm, tk), F32),) if nn > 1 else ()
    return _pcall(body, name=name, out_shape=_sds((M, K), out_dtype), grid=(M // tm, K // tk, nn),
                  in_specs=in_specs, out_specs=pl.BlockSpec((tm, tk), lambda i, kk, j: (i, kk)),
                  scratch=scratch)(*args)


def mm_tn(a, c, *, nb, name, out_dtype, K=None, N=None, a_col0=0, c_col0=0):
    M = a.shape[0]
    K = K or a.shape[1]
    N = N or c.shape[1]
    Ns = N // nb
    tm = _pick(M, (2048, 1024, 512, 256))
    tk = _pick(K, (1024, 512, 256))
    tn = _pick(Ns, (1024, 512, 256, 128))
    assert a_col0 % tk == 0 and c_col0 % tn == 0
    nm, nbs, a0, c0 = M // tm, Ns // tn, a_col0 // tk, c_col0 // tn

    def body(*refs):
        a_ref, c_ref, o_ref = refs[:3]
        acc_ref = refs[3] if nm > 1 else None
        part = lax.dot_general(_mx(a_ref[...]), _mx(c_ref[...]), (((0,), (0,)), ((), ())),
                               preferred_element_type=F32)
        if nm == 1:
            o_ref[...] = part.astype(o_ref.dtype)
        else:
            m = pl.program_id(2)

            @pl.when(m == 0)
            def _():
                acc_ref[...] = part

            @pl.when(m > 0)
            def _():
                acc_ref[...] += part

            @pl.when(m == nm - 1)
            def _():
                o_ref[...] = acc_ref[...].astype(o_ref.dtype)

    scratch = (pltpu.VMEM((tk, tn), F32),) if nm > 1 else ()
    return _pcall(
        body, name=name, out_shape=_sds((nb, K, Ns), out_dtype), grid=(N // tn, K // tk, nm),
        in_specs=[pl.BlockSpec((tm, tk), lambda j, kk, m: (m, a0 + kk)),
                  pl.BlockSpec((tm, tn), lambda j, kk, m: (m, c0 + j))],
        out_specs=pl.BlockSpec((None, tk, tn), lambda j, kk, m: (j // nbs, kk, j % nbs)),
        scratch=scratch)(a, c)


def rms_fwd(x, g, *, name, col0=0, width=None, out_dtype=BF16):
    R = x.shape[0]
    width = width or x.shape[1]
    assert col0 % width == 0
    cb = col0 // width
    tr = _pick(R, (512, 256, 128))

    def body(x_ref, g_ref, o_ref):
        xf = x_ref[...].astype(F32)
        r = lax.rsqrt(jnp.mean(xf * xf, axis=1, keepdims=True) + EPS)
        o_ref[...] = (xf * r * g_ref[...]).astype(o_ref.dtype)

    return _pcall(body, name=name, out_shape=_sds((R, width), out_dtype), grid=(R // tr,),
                  in_specs=[pl.BlockSpec((tr, width), lambda i: (i, cb)),
                            pl.BlockSpec((1, width), lambda i: (0, 0))],
                  out_specs=pl.BlockSpec((tr, width), lambda i: (i, 0)))(x, g)


def rms_bwd(x, g, dy, *, name, col0=0, width=None, res=None, dx_dtype=F32, need_dx=True):
    R = x.shape[0]
    width = width or x.shape[1]
    cb = col0 // width
    tr = _pick(R, (512, 256, 128))

    def body(*refs):
        x_ref, g_ref, dy_ref = refs[:3]
        pos = 3
        res_ref = None
        if res is not None:
            res_ref = refs[pos]
            pos += 1
        dx_ref = None
        if need_dx:
            dx_ref = refs[pos]
            pos += 1
        dg_ref = refs[pos]
        xf = x_ref[...].astype(F32)
        dyf = dy_ref[...].astype(F32)
        r = lax.rsqrt(jnp.mean(xf * xf, axis=1, keepdims=True) + EPS)
        xh = xf * r
        if need_dx:
            gy = dyf * g_ref[...]
            dx = r * (gy - xh * jnp.mean(gy * xh, axis=1, keepdims=True))
            if res_ref is not None:
                dx = dx + res_ref[...]
            dx_ref[...] = dx.astype(dx_ref.dtype)
        part = jnp.sum(dyf * xh, axis=0, keepdims=True)

        @pl.when(pl.program_id(0) == 0)
        def _():
            dg_ref[...] = part

        @pl.when(pl.program_id(0) > 0)
        def _():
            dg_ref[...] += part

    in_specs = [pl.BlockSpec((tr, width), lambda i: (i, cb)),
                pl.BlockSpec((1, width), lambda i: (0, 0)),
                pl.BlockSpec((tr, width), lambda i: (i, 0))]
    args = [x, g, dy]
    if res is not None:
        in_specs.append(pl.BlockSpec((tr, width), lambda i: (i, 0)))
        args.append(res)
    dg_shape, dg_spec = _sds((1, width), F32), pl.BlockSpec((1, width), lambda i: (0, 0))
    if need_dx:
        out_shape = (_sds((R, width), dx_dtype), dg_shape)
        out_specs = (pl.BlockSpec((tr, width), lambda i: (i, 0)), dg_spec)
    else:
        out_shape, out_specs = dg_shape, dg_spec
    out = _pcall(body, name=name, out_shape=out_shape, grid=(R // tr,), in_specs=in_specs,
                 out_specs=out_specs)(*args)
    return out if need_dx else (None, out)


HEAD_ROW_BLOCKS = (2048, 1024, 512, 256, 128)


def _rope_apply(y, c, sa, sb):
    return y * c + pltpu.roll(y, 96, 1) * sa + pltpu.roll(y, 32, 1) * sb


def _rope_transpose(dy, c, sa, sb):
    return dy * c + pltpu.roll(dy * sa, 32, 1) + pltpu.roll(dy * sb, 96, 1)


def rope_tables(seq):
    pos = jnp.arange(seq, dtype=F32)
    inv = ROPE_THETA ** (-jnp.arange(0, MLA_ROPE, 2, dtype=F32) / MLA_ROPE)
    ang = pos[:, None] * inv[None, :]
    cos, sin = jnp.cos(ang), jnp.sin(ang)
    z32, z64 = jnp.zeros_like(cos), jnp.zeros((seq, 64), F32)
    c = jnp.concatenate([cos, cos, z64], axis=1)
    sa = jnp.concatenate([-sin, z32, z64], axis=1)
    sb = jnp.concatenate([z32, sin, z64], axis=1)
    return c, sa, sb


def _head_vec(part_refs):
    xs = [p[...].astype(F32) for p in part_refs]
    return xs[0] if len(xs) == 1 else jnp.concatenate(xs, axis=1)


def prep_fwd(parts, g, *, name, n_heads, n_real, rope=None):
    rows = parts[0][0].shape[0]
    dh = sum(w for _, w, _ in parts)
    tr = _pick(rows, HEAD_ROW_BLOCKS)
    npart = len(parts)

    def body(*refs):
        part_refs, g_ref = refs[:npart], refs[npart]
        pos = npart + 1
        if rope is not None:
            c_ref, sa_ref, sb_ref = refs[pos:pos + 3]
            pos += 3
        o_ref = refs[pos]
        x = _head_vec(part_refs)
        r = lax.rsqrt(jnp.sum(x * x, axis=1, keepdims=True) * (1.0 / n_real) + EPS)
        y = x * r * g_ref[...]
        if rope is not None:
            yr = _rope_apply(y[:, dh - 128:], c_ref[...], sa_ref[...], sb_ref[...])
            y = jnp.concatenate([y[:, :dh - 128], yr], axis=1)
        o_ref[...] = y.astype(o_ref.dtype)

    in_specs, args = [], []
    for arr, w, fn in parts:
        in_specs.append(pl.BlockSpec((tr, w), functools.partial(lambda h, i, fn: (i, fn(h)), fn=fn)))
        args.append(arr)
    in_specs.append(pl.BlockSpec((1, dh), lambda h, i: (0, 0)))
    args.append(g)
    if rope is not None:
        for t in rope:
            in_specs.append(pl.BlockSpec((tr, 128), lambda h, i: (i, 0)))
            args.append(t)
    return _pcall(body, name=name, out_shape=_sds((n_heads, rows, dh), BF16), grid=(n_heads, rows // tr),
                  in_specs=in_specs, out_specs=pl.BlockSpec((None, tr, dh), lambda h, i: (h, i, 0)))(*args)


def _norm_bwd(x, g, dyn, n_real):
    r = lax.rsqrt(jnp.sum(x * x, axis=1, keepdims=True) * (1.0 / n_real) + EPS)
    xh = x * r
    gy = dyn * g
    dx = r * (gy - xh * (jnp.sum(gy * xh, axis=1, keepdims=True) * (1.0 / n_real)))
    return dx, jnp.sum(dyn * xh, axis=0, keepdims=True)


def prep_bwd_q(src, dy, g, *, name, n_heads, dh, n_real, rope=None, out_dtype=BF16):
    rows = src.shape[0]
    tr = _pick(rows, HEAD_ROW_BLOCKS)

    def body(*refs):
        x_ref, dy_ref, g_ref = refs[:3]
        pos = 3
        if rope is not None:
            c_ref, sa_ref, sb_ref = refs[pos:pos + 3]
            pos += 3
        dx_ref, dg_ref = refs[pos], refs[pos + 1]
        dyn = dy_ref[...].astype(F32)
        if rope is not None:
            dr = _rope_transpose(dyn[:, dh - 128:], c_ref[...], sa_ref[...], sb_ref[...])
            dyn = jnp.concatenate([dyn[:, :dh - 128], dr], axis=1)
        dx, dg = _norm_bwd(x_ref[...].astype(F32), g_ref[...], dyn, n_real)
        dx_ref[...] = dx.astype(dx_ref.dtype)
        first = jnp.logical_and(pl.program_id(0) == 0, pl.program_id(1) == 0)

        @pl.when(first)
        def _():
            dg_ref[...] = dg

        @pl.when(jnp.logical_not(first))
        def _():
            dg_ref[...] += dg

    in_specs = [pl.BlockSpec((tr, dh), lambda i, h: (i, h)),
                pl.BlockSpec((None, tr, dh), lambda i, h: (h, i, 0)),
                pl.BlockSpec((1, dh), lambda i, h: (0, 0))]
    args = [src, dy, g]
    if rope is not None:
        for t in rope:
            in_specs.append(pl.BlockSpec((tr, 128), lambda i, h: (i, 0)))
            args.append(t)
    return _pcall(body, name=name, out_shape=(_sds((rows, n_heads * dh), out_dtype), _sds((1, dh), F32)),
                  grid=(rows // tr, n_heads), in_specs=in_specs,
                  out_specs=(pl.BlockSpec((tr, dh), lambda i, h: (i, h)),
                             pl.BlockSpec((1, dh), lambda i, h: (0, 0))))(*args)


def prep_bwd_mla_k(kv_raw, zs, dkf, dv, g, rope, *, name):
    rows = kv_raw.shape[0]
    tr = _pick(rows, HEAD_ROW_BLOCKS)

    def body(kn_ref, kr_ref, dy_ref, dv_ref, g_ref, c_ref, sa_ref, sb_ref, dkv_ref, dkr_ref, dg_ref):
        h = pl.program_id(1)
        x = jnp.concatenate([kn_ref[...].astype(F32), kr_ref[...].astype(F32)], axis=1)
        dyn = dy_ref[...].astype(F32)
        dr = _rope_transpose(dyn[:, 128:], c_ref[...], sa_ref[...], sb_ref[...])
        dyn = jnp.concatenate([dyn[:, :128], dr], axis=1)
        dx, dg = _norm_bwd(x, g_ref[...], dyn, MLA_QK)
        dkv_ref[...] = jnp.concatenate([dx[:, :128], dv_ref[...].astype(F32)], axis=1).astype(dkv_ref.dtype)

        @pl.when(h == 0)
        def _():
            dkr_ref[...] = dx[:, 128:]

        @pl.when(h > 0)
        def _():
            dkr_ref[...] += dx[:, 128:]

        first = jnp.logical_and(pl.program_id(0) == 0, h == 0)

        @pl.when(first)
        def _():
            dg_ref[...] = dg

        @pl.when(jnp.logical_not(first))
        def _():
            dg_ref[...] += dg

    tab = pl.BlockSpec((tr, 128), lambda i, h: (i, 0))
    return _pcall(
        body, name=name,
        out_shape=(_sds((rows, N_HEADS * 256), BF16), _sds((rows, 128), F32), _sds((1, MLA_PAD), F32)),
        grid=(rows // tr, N_HEADS),
        in_specs=[pl.BlockSpec((tr, 128), lambda i, h: (i, 2 * h)),
                  pl.BlockSpec((tr, 128), lambda i, h: (i, Z_KR // 128)),
                  pl.BlockSpec((None, tr, MLA_PAD), lambda i, h: (h, i, 0)),
                  pl.BlockSpec((None, tr, 128), lambda i, h: (h, i, 0)),
                  pl.BlockSpec((1, MLA_PAD), lambda i, h: (0, 0)), tab, tab, tab],
        out_specs=(pl.BlockSpec((tr, 256), lambda i, h: (i, h)),
                   pl.BlockSpec((tr, 128), lambda i, h: (i, 0)),
                   pl.BlockSpec((1, MLA_PAD), lambda i, h: (0, 0))))(kv_raw, zs, dkf, dv, g, *rope)


def prep_bwd_groups(src, base_blk, dys, gs, *, name, n_heads, kinds, out_dtype=BF16):
    rows = src.shape[0]
    ng = len(kinds)
    J = ng * n_heads
    tr = _pick(rows, HEAD_ROW_BLOCKS)
    gstack = jnp.stack([gs[k] if kinds[k] == "norm" else jnp.ones((1, DH), F32) for k in range(ng)])

    def body(*refs):
        x_ref = refs[0]
        dy_refs = refs[1:1 + ng]
        g_ref, dx_ref, dg_ref = refs[1 + ng:4 + ng]
        j, i = pl.program_id(0), pl.program_id(1)
        grp = j // n_heads
        dy = dy_refs[0][...].astype(F32)
        for k in range(1, ng):
            dy = jnp.where(grp == k, dy_refs[k][...].astype(F32), dy)
        dx, dg = _norm_bwd(x_ref[...].astype(F32), g_ref[...], dy, DH)
        is_copy = functools.reduce(jnp.logical_or, [grp == k for k in range(ng) if kinds[k] == "copy"],
                                   jnp.bool_(False))
        dx_ref[...] = jnp.where(is_copy, dy, dx).astype(dx_ref.dtype)
        dg = jnp.where(is_copy, jnp.zeros_like(dg), dg)
        first = jnp.logical_and(j % n_heads == 0, i == 0)

        @pl.when(first)
        def _():
            dg_ref[...] = dg

        @pl.when(jnp.logical_not(first))
        def _():
            dg_ref[...] += dg

    in_specs = [pl.BlockSpec((tr, DH), lambda j, i: (i, base_blk + j))]
    for k in range(ng):
        in_specs.append(pl.BlockSpec(
            (None, tr, DH),
            functools.partial(lambda j, i, k: (jnp.clip(j - k * n_heads, 0, n_heads - 1),
                                               jnp.where(j // n_heads == k, i, 0), 0), k=k)))
    in_specs.append(pl.BlockSpec((None, 1, DH), lambda j, i: (j // n_heads, 0, 0)))
    return _pcall(body, name=name, out_shape=(_sds((rows, J * DH), out_dtype), _sds((ng, 1, DH), F32)),
                  grid=(J, rows // tr), in_specs=in_specs,
                  out_specs=(pl.BlockSpec((tr, DH), lambda j, i: (i, j)),
                             pl.BlockSpec((None, 1, DH), lambda j, i: (j // n_heads, 0, 0))))(src, *dys, gstack)


def _attn_cfg(mode, sq, sk):
    if mode == "chunk":
        tq = 128
        win = min((LEFT_CHUNKS + 2) * CHUNK, sk)
    else:
        tq = _pick(sq, (256, 128))
        win = sk
    scale = (MLA_QK if mode == "mla" else DH) ** -0.5
    return tq, win, scale


def _attn_key_rows(mode, i, tq, win, sk, run):
    if mode == "chunk":
        start = pl.multiple_of(jnp.clip((i - LEFT_CHUNKS // 2) * 128, 0, sk - win), 128)
        run(pl.ds(start, win), start)
    elif mode == "cross":
        run(slice(0, sk), 0)
    else:
        lax.switch(i, [functools.partial(run, slice(0, (b + 1) * tq), 0) for b in range(sk // tq)])


def _attn_scores(mode, i, tq, scale, q, kk, start, cq, ck, t_ref):
    nk = kk.shape[0]
    s = lax.dot_general(q, kk, (((1,), (1,)), ((), ())), preferred_element_type=F32) * scale
    if mode == "cross":
        return s
    t_pos = i * tq + lax.broadcasted_iota(jnp.int32, (tq, nk), 0)
    s_pos = start + lax.broadcasted_iota(jnp.int32, (tq, nk), 1)
    if mode == "fox":
        s = s + cq - ck
        allowed = s_pos <= t_pos
    else:
        qc, kc = lax.shift_right_logical(t_pos, CHUNK_SHIFT), lax.shift_right_logical(s_pos, CHUNK_SHIFT)
        allowed = kc <= qc
        if mode == "chunk":
            allowed = jnp.logical_and(allowed, kc >= qc - LEFT_CHUNKS)
            tiles = []
            for w in range(nk // 128):
                delta = i - (start // 128 + w)
                tiles.append(jnp.where(delta == 0, t_ref[0], jnp.where(delta == 1, t_ref[1], t_ref[2])))
            s = s + jnp.concatenate(tiles, axis=1)
    return jnp.where(allowed, s, NEG)


def attn_fwd(q, k, v_arr, v_blk, *, mode, name, n_heads, cq=None, ck=None, tiles=None):
    _, sq, dk = q.shape
    sk = k.shape[1]
    tq, win, scale = _attn_cfg(mode, sq, sk)

    def body(*refs):
        q_ref, k_ref, v_ref = refs[:3]
        pos = 3
        cq_ref = ck_ref = t_ref = None
        if mode == "fox":
            cq_ref, ck_ref = refs[pos:pos + 2]
            pos += 2
        if mode == "chunk":
            t_ref = refs[pos]
            pos += 1
        o_ref, lse_ref = refs[pos], refs[pos + 1]
        i = pl.program_id(1)

        def run(rows, start):
            cq, ck = (cq_ref[...], ck_ref[:, rows]) if mode == "fox" else (None, None)
            s = _attn_scores(mode, i, tq, scale, q_ref[...], k_ref[rows, :], start, cq, ck, t_ref)
            m = jnp.max(s, axis=1, keepdims=True)
            e = jnp.exp(s - m)
            l = jnp.sum(e, axis=1, keepdims=True)
            p = e * (1.0 / l)
            o_ref[...] = jnp.dot(_mx(p), _mx(v_ref[rows, :]), preferred_element_type=F32).astype(o_ref.dtype)
            lse_ref[...] = m + jnp.log(l)

        _attn_key_rows(mode, i, tq, win, sk, run)

    in_specs = [pl.BlockSpec((None, tq, dk), lambda h, i: (h, i, 0)),
                pl.BlockSpec((None, sk, dk), lambda h, i: (h, 0, 0)),
                pl.BlockSpec((sk, DH), lambda h, i: (0, v_blk(h)))]
    args = [q, k, v_arr]
    if mode == "fox":
        in_specs += [pl.BlockSpec((None, tq, 1), lambda h, i: (h, i, 0)),
                     pl.BlockSpec((None, 1, sk), lambda h, i: (h, 0, 0))]
        args += [cq, ck]
    if mode == "chunk":
        in_specs.append(pl.BlockSpec((3, None, 128, 128), lambda h, i: (0, h, 0, 0)))
        args.append(tiles)
    return _pcall(body, name=name,
                  out_shape=(_sds((sq, n_heads * DH), BF16), _sds((n_heads, sq, 1), F32)),
                  grid=(n_heads, sq // tq), in_specs=in_specs,
                  out_specs=(pl.BlockSpec((tq, DH), lambda h, i: (i, h)),
                             pl.BlockSpec((None, tq, 1), lambda h, i: (h, i, 0))))(*args)


def attn_bwd(q, k, v_arr, v_blk, o, do, lse, *, mode, name, n_heads, cq=None, ck=None, tiles=None):
    _, sq, dk = q.shape
    sk = k.shape[1]
    tq, win, scale = _attn_cfg(mode, sq, sk)
    n_extra = {"fox": 2, "chunk": 1}.get(mode, 0)

    def body(*refs):
        q_ref, k_ref, v_ref, o_ref, do_ref, lse_ref = refs[:6]
        pos = 6
        cq_ref = ck_ref = t_ref = None
        if mode == "fox":
            cq_ref, ck_ref = refs[pos:pos + 2]
            pos += 2
        if mode == "chunk":
            t_ref = refs[pos]
            pos += 1
        dq_ref, dk_ref, dv_ref = refs[pos:pos + 3]
        extra = refs[pos + 3:pos + 3 + n_extra]
        i = pl.program_id(1)

        @pl.when(i == 0)
        def _():
            dk_ref[...] = jnp.zeros_like(dk_ref)
            dv_ref[...] = jnp.zeros_like(dv_ref)
            if mode == "fox":
                extra[1][...] = jnp.zeros_like(extra[1])
            if mode == "chunk":
                extra[0][...] = jnp.zeros_like(extra[0])

        def run(rows, start):
            q = q_ref[...]
            do = do_ref[...]
            kk = k_ref[rows, :]
            cq, ck = (cq_ref[...], ck_ref[:, rows]) if mode == "fox" else (None, None)
            s = _attn_scores(mode, i, tq, scale, q, kk, start, cq, ck, t_ref)
            p = jnp.exp(s - lse_ref[...])
            drow = jnp.sum(do.astype(F32) * o_ref[...].astype(F32), axis=1, keepdims=True)
            dp = lax.dot_general(do, _mx(v_ref[rows, :]), (((1,), (1,)), ((), ())), preferred_element_type=F32)
            ds = p * (dp - drow)
            dsb = _mx(ds)
            dq_ref[...] = (jnp.dot(dsb, kk, preferred_element_type=F32) * scale).astype(dq_ref.dtype)
            dk_ref[rows, :] += lax.dot_general(dsb, q, (((0,), (0,)), ((), ())),
                                               preferred_element_type=F32) * scale
            dv_ref[rows, :] += lax.dot_general(_mx(p), do, (((0,), (0,)), ((), ())), preferred_element_type=F32)
            if mode == "chunk":
                dt_ref = extra[0]
                for w in range(win // 128):
                    delta = i - (start // 128 + w)
                    tile = ds[:, w * 128:(w + 1) * 128]
                    zero = jnp.zeros_like(tile)
                    dt_ref[0] += jnp.where(delta == 0, tile, zero)
                    dt_ref[1] += jnp.where(delta == 1, tile, zero)
                    dt_ref[2] += jnp.where(delta >= 2, tile, zero)
            if mode == "fox":
                extra[0][...] = jnp.sum(ds, axis=1, keepdims=True)
                extra[1][:, rows] -= jnp.sum(ds, axis=0, keepdims=True)

        _attn_key_rows(mode, i, tq, win, sk, run)

    in_specs = [pl.BlockSpec((None, tq, dk), lambda h, i: (h, i, 0)),
                pl.BlockSpec((None, sk, dk), lambda h, i: (h, 0, 0)),
                pl.BlockSpec((sk, DH), lambda h, i: (0, v_blk(h))),
                pl.BlockSpec((tq, DH), lambda h, i: (i, h)),
                pl.BlockSpec((tq, DH), lambda h, i: (i, h)),
                pl.BlockSpec((None, tq, 1), lambda h, i: (h, i, 0))]
    args = [q, k, v_arr, o, do, lse]
    out_shape = [_sds((n_heads, sq, dk), F32), _sds((n_heads, sk, dk), F32), _sds((n_heads, sk, DH), F32)]
    out_specs = [pl.BlockSpec((None, tq, dk), lambda h, i: (h, i, 0)),
                 pl.BlockSpec((None, sk, dk), lambda h, i: (h, 0, 0)),
                 pl.BlockSpec((None, sk, DH), lambda h, i: (h, 0, 0))]
    if mode == "fox":
        in_specs += [pl.BlockSpec((None, tq, 1), lambda h, i: (h, i, 0)),
                     pl.BlockSpec((None, 1, sk), lambda h, i: (h, 0, 0))]
        args += [cq, ck]
        out_shape += [_sds((n_heads, sq, 1), F32), _sds((n_heads, 1, sk), F32)]
        out_specs += [pl.BlockSpec((None, tq, 1), lambda h, i: (h, i, 0)),
                      pl.BlockSpec((None, 1, sk), lambda h, i: (h, 0, 0))]
    if mode == "chunk":
        in_specs.append(pl.BlockSpec((3, None, 128, 128), lambda h, i: (0, h, 0, 0)))
        args.append(tiles)
        out_shape.append(_sds((3, n_heads, 128, 128), F32))
        out_specs.append(pl.BlockSpec((3, None, 128, 128), lambda h, i: (0, h, 0, 0)))
    return _pcall(body, name=name, out_shape=tuple(out_shape), grid=(n_heads, sq // tq),
                  in_specs=in_specs, out_specs=tuple(out_specs))(*args)


REL_LANES = 384
REL_KBLK = 2048


def _rel_onehot(t, k):
    rho = k * REL_KBLK + lax.broadcasted_iota(jnp.int32, (REL_KBLK, REL_LANES), 0)
    lane = lax.broadcasted_iota(jnp.int32, (REL_KBLK, REL_LANES), 1)
    diff = lax.shift_right_logical(rho, 7) - jnp.bitwise_and(rho, 127)
    idx = jnp.where(t == 0, diff + REL_CLIP,
                    jnp.where(t == 1, jnp.minimum(diff + 128, REL_CLIP) + REL_CLIP, N_REL - 1))
    return jnp.where(idx == lane, 1.0, 0.0).astype(F32)


def _split3(x):
    hi = x.astype(BF16)
    rest = x - hi.astype(F32)
    mid = rest.astype(BF16)
    return hi, mid, (rest - mid.astype(F32)).astype(BF16)


def relbias_onehot():
    def body(o_ref):
        o_ref[...] = _rel_onehot(pl.program_id(0), pl.program_id(1)).astype(o_ref.dtype)

    return _pcall(body, name="relbias_onehot", out_shape=_sds((3, 128 * 128, REL_LANES), BF16),
                  grid=(3, 128 * 128 // REL_KBLK), in_specs=[],
                  out_specs=pl.BlockSpec((None, REL_KBLK, REL_LANES), lambda t, k: (t, k, 0)))()


ONEHOT_SPEC = pl.BlockSpec((None, REL_KBLK, REL_LANES), lambda t, k: (t, k, 0))


def relbias_tiles(rel_bias, onehot):
    nh = rel_bias.shape[0]
    rb = jnp.pad(rel_bias, ((0, 0), (0, REL_LANES - N_REL)))

    def body(rb_ref, e_ref, o_ref):
        e = e_ref[...]
        hi, mid, lo = [lax.dot_general(t, e, (((1,), (1,)), ((), ())), preferred_element_type=F32)
                       for t in _split3(rb_ref[...])]
        o_ref[...] = (hi + mid) + lo

    flat = _pcall(body, name="relbias_tiles", out_shape=_sds((3, nh, 128 * 128), F32),
                  grid=(3, 128 * 128 // REL_KBLK),
                  in_specs=[pl.BlockSpec((nh, REL_LANES), lambda t, k: (0, 0)), ONEHOT_SPEC],
                  out_specs=pl.BlockSpec((None, nh, REL_KBLK), lambda t, k: (t, 0, k)))(rb, onehot)
    return flat.reshape(3, nh, 128, 128)


def relbias_tiles_bwd(dtiles, onehot):
    nh = dtiles.shape[1]

    def body(dt_ref, e_ref, o_ref):
        t, k = pl.program_id(0), pl.program_id(1)
        e = e_ref[...]
        hi, mid, lo = [jnp.dot(t_, e, preferred_element_type=F32) for t_ in _split3(dt_ref[...])]
        part = (hi + mid) + lo
        first = jnp.logical_and(t == 0, k == 0)

        @pl.when(first)
        def _():
            o_ref[...] = part

        @pl.when(jnp.logical_not(first))
        def _():
            o_ref[...] += part

    out = _pcall(body, name="relbias_tiles_bwd", out_shape=_sds((nh, REL_LANES), F32),
                 grid=(3, 128 * 128 // REL_KBLK),
                 in_specs=[pl.BlockSpec((None, nh, REL_KBLK), lambda t, k: (t, 0, k)), ONEHOT_SPEC],
                 out_specs=pl.BlockSpec((nh, REL_LANES), lambda t, k: (0, 0)))(
                     dtiles.reshape(3, nh, 128 * 128), onehot)
    return out[:, :N_REL]


CUM_BLK = 256


def _tri(n, lower):
    r = lax.broadcasted_iota(jnp.int32, (n, n), 0)
    c = lax.broadcasted_iota(jnp.int32, (n, n), 1)
    return jnp.where(r >= c if lower else r <= c, 1.0, 0.0).astype(F32)


def fox_cum_fwd(zs, bf):
    S = zs.shape[0]
    tb = min(CUM_BLK, S)

    def body(f_ref, b_ref, cum_ref, cumt_ref, carry_ref):
        @pl.when(pl.program_id(0) == 0)
        def _():
            carry_ref[...] = jnp.zeros_like(carry_ref)

        x = f_ref[...] + b_ref[...]
        lane = lax.broadcasted_iota(jnp.int32, x.shape, 1)
        logf = jnp.where(lane < N_HEADS, jnp.minimum(x, 0.0) - jnp.log(1.0 + jnp.exp(-jnp.abs(x))), 0.0)
        cum = jnp.dot(_tri(tb, True), logf, preferred_element_type=F32,
                      precision=lax.Precision.HIGHEST) + carry_ref[...]
        carry_ref[...] = cum[tb - 1:tb, :]
        cum_ref[...] = cum
        cumt_ref[...] = cum.T

    return _pcall(body, name="fox_cum_fwd", out_shape=(_sds((S, 128), F32), _sds((128, S), F32)),
                  grid=(S // tb,),
                  in_specs=[pl.BlockSpec((tb, 128), lambda i: (i, Z_FF // 128)),
                            pl.BlockSpec((1, 128), lambda i: (0, 0))],
                  out_specs=(pl.BlockSpec((tb, 128), lambda i: (i, 0)),
                             pl.BlockSpec((128, tb), lambda i: (0, i))),
                  scratch=(pltpu.VMEM((1, 128), F32),))(zs, bf)


def fox_cum_bwd(zs, bf, dcum):
    S = zs.shape[0]
    tb = min(CUM_BLK, S)
    nblk = S // tb

    def body(f_ref, b_ref, d_ref, df_ref, db_ref, carry_ref):
        @pl.when(pl.program_id(0) == 0)
        def _():
            carry_ref[...] = jnp.zeros_like(carry_ref)
            db_ref[...] = jnp.zeros_like(db_ref)

        d = d_ref[...]
        dlogf = jnp.dot(_tri(tb, False), d, preferred_element_type=F32,
                        precision=lax.Precision.HIGHEST) + carry_ref[...]
        carry_ref[...] += jnp.sum(d, axis=0, keepdims=True)
        x = f_ref[...] + b_ref[...]
        lane = lax.broadcasted_iota(jnp.int32, x.shape, 1)
        dx = jnp.where(lane < N_HEADS, dlogf / (1.0 + jnp.exp(x)), 0.0)
        df_ref[...] = dx
        db_ref[...] += jnp.sum(dx, axis=0, keepdims=True)

    return _pcall(body, name="fox_cum_bwd", out_shape=(_sds((S, 128), F32), _sds((1, 128), F32)),
                  grid=(nblk,),
                  in_specs=[pl.BlockSpec((tb, 128), lambda i: (nblk - 1 - i, Z_FF // 128)),
                            pl.BlockSpec((1, 128), lambda i: (0, 0)),
                            pl.BlockSpec((tb, 128), lambda i: (nblk - 1 - i, 0))],
                  out_specs=(pl.BlockSpec((tb, 128), lambda i: (nblk - 1 - i, 0)),
                             pl.BlockSpec((1, 128), lambda i: (0, 0))),
                  scratch=(pltpu.VMEM((1, 128), F32),))(zs, bf, dcum)


def _sigmoid(x):
    return 1.0 / (1.0 + jnp.exp(-x))


def merge_fwd(z, projs):
    S = z.shape[0]
    tr, tc = _pick(S, (512, 256, 128)), 512
    nbc = D_MODEL // tc
    g0 = Z_GATE // tc

    def body(g0_ref, g1_ref, g2_ref, p0_ref, p1_ref, p2_ref, o_ref):
        acc = _sigmoid(g0_ref[...]) * p0_ref[...]
        acc += _sigmoid(g1_ref[...]) * p1_ref[...]
        acc += _sigmoid(g2_ref[...]) * p2_ref[...]
        o_ref[...] = acc.astype(o_ref.dtype)

    gspecs = [pl.BlockSpec((tr, tc), functools.partial(lambda i, j, n: (i, g0 + n * nbc + j), n=n))
              for n in range(3)]
    pspec = pl.BlockSpec((tr, tc), lambda i, j: (i, j))
    return _pcall(body, name="merge_fwd", out_shape=_sds((S, D_MODEL), BF16), grid=(S // tr, nbc),
                  in_specs=gspecs + [pspec] * 3, out_specs=pspec)(z, z, z, *projs)


def merge_bwd(z, projs, dmerged):
    S = z.shape[0]
    tr, tc = _pick(S, (512, 256, 128)), 512
    nbc = D_MODEL // tc
    g0 = Z_GATE // tc

    def body(g0_ref, g1_ref, g2_ref, p0_ref, p1_ref, p2_ref, dm_ref, dg0, dg1, dg2, dp0, dp1, dp2):
        dm = dm_ref[...]
        for g_ref, p_ref, dg_ref, dp_ref in ((g0_ref, p0_ref, dg0, dp0), (g1_ref, p1_ref, dg1, dp1),
                                             (g2_ref, p2_ref, dg2, dp2)):
            sg = _sigmoid(g_ref[...])
            dp_ref[...] = (dm * sg).astype(dp_ref.dtype)
            dg_ref[...] = (dm * p_ref[...] * sg * (1.0 - sg)).astype(dg_ref.dtype)

    gspecs = [pl.BlockSpec((tr, tc), functools.partial(lambda i, j, n: (i, g0 + n * nbc + j), n=n))
              for n in range(3)]
    pspec = pl.BlockSpec((tr, tc), lambda i, j: (i, j))
    out = _pcall(body, name="merge_bwd", out_shape=tuple(_sds((S, D_MODEL), BF16) for _ in range(6)),
                 grid=(S // tr, nbc), in_specs=gspecs + [pspec] * 4,
                 out_specs=tuple([pspec] * 6))(z, z, z, *projs, dmerged)
    return out[:3], out[3:]


def loss_head(y, target):
    S, D = y.shape
    tr = _pick(S, (256, 128))

    def body(y_ref, t_ref, dy_ref, l_ref):
        e = y_ref[...] - t_ref[...]
        dy_ref[...] = e * (1.0 / D)
        part = jnp.sum(jnp.sum(e * e, axis=1, keepdims=True), axis=0, keepdims=True) * (0.5 / D)

        @pl.when(pl.program_id(0) == 0)
        def _():
            l_ref[...] = part

        @pl.when(pl.program_id(0) > 0)
        def _():
            l_ref[...] += part

    spec = pl.BlockSpec((tr, D), lambda i: (i, 0))
    return _pcall(body, name="loss_head", out_shape=(_sds((S, D), F32), _sds((1, 1), F32)), grid=(S // tr,),
                  in_specs=[spec, spec], out_specs=(spec, pl.BlockSpec((1, 1), lambda i: (0, 0))))(y, target)


def _adamw_update(w, g, m, v):
    nm = ADAM_B1 * m + (1.0 - ADAM_B1) * g
    nv = ADAM_B2 * v + (1.0 - ADAM_B2) * (g * g)
    delta = -ADAM_LR * ((nm / (1.0 - ADAM_B1 ** ADAM_STEP)) / (jnp.sqrt(nv / (1.0 - ADAM_B2 ** ADAM_STEP)) + ADAM_EPS)
                        + ADAM_WD * w)
    return delta, nm, nv


ADAMW_BLOCK_BYTES = 1024 * 1024


def adamw(w, gs, m, v, *, name):
    L, R, C = w.shape
    tr = _pick(R, (512, 256, 128, 64, 32, 16, 8))
    while tr * C * 4 > ADAMW_BLOCK_BYTES and tr % 16 == 0:
        tr //= 2

    def body(*refs):
        w_ref, m_ref, v_ref = refs[:3]
        g_refs = refs[3:3 + L]
        go_ref, d_ref, nm_ref, nv_ref = refs[3 + L:]
        l = pl.program_id(0)
        g_ = g_refs[0][...]
        for k in range(1, L):
            g_ = jnp.where(l == k, g_refs[k][...], g_)
        go_ref[...] = g_
        d_ref[...], nm_ref[...], nv_ref[...] = _adamw_update(w_ref[...], g_, m_ref[...], v_ref[...])

    spec = pl.BlockSpec((None, tr, C), lambda l, i: (l, i, 0))
    gspecs = [pl.BlockSpec((tr, C), functools.partial(lambda l, i, k: (jnp.where(l == k, i, 0), 0), k=k))
              for k in range(L)]
    return _pcall(body, name=name, out_shape=tuple(_sds((L, R, C), F32) for _ in range(4)),
                  grid=(L, R // tr), in_specs=[spec] * 3 + gspecs, out_specs=(spec,) * 4)(w, m, v, *gs)


SHARD_SHAPES = {
    "w_uq": (Q_LORA, 384), "w_ukv": (KV_LORA, 512), "w_br": (3, MIX_W, 512), "w_out": (512, D_MODEL),
    "w_xq": (512, 512), "w_xkv": (512, 1024), "w_xo": (512, 512), "w_1": (D_MODEL, 2048),
    "w_2": (2048, D_MODEL), "w_in": (D_MODEL, W_IN_SHARD),
}


def _rows2d(a, name):
    shp = SHARD_SHAPES[name]
    return a.reshape(a.shape[:a.ndim - len(shp)] + (-1, shp[-1]))


def _cols_from_shards(g):
    return jnp.transpose(g, (1, 0, 2)).reshape(g.shape[1], 4 * g.shape[2])


def _cols_to_shards(w):
    return jnp.transpose(w.reshape(w.shape[0], 4, w.shape[1] // 4), (1, 0, 2))


EARLY_WEIGHTS = ("w_in", "w_uq", "w_ukv")
MID_WEIGHTS = ("w_br", "w_out", "w_xq", "w_xkv", "w_xo")
LAST_WEIGHTS = ("w_1", "w_2")
LATE_WEIGHTS = MID_WEIGHTS + LAST_WEIGHTS
assert sorted(EARLY_WEIGHTS + LATE_WEIGHTS) == sorted(PACK_ORDER)


def _w_in_relaid(g_in):
    zeros = lambda n: [jnp.zeros((D_MODEL, n), g_in.dtype)] if n else []
    segs, at = [], 0
    for p0, o0, w in sorted(W_IN_PIECES):
        segs += zeros(p0 - at)
        while w > 0:
            s_, a = divmod(o0, W_IN_SHARD)
            take = min(w, W_IN_SHARD - a)
            segs.append(g_in[s_][:, a:a + take])
            o0, p0, w = o0 + take, p0 + take, w - take
        at = p0
    return jnp.concatenate(segs + zeros(Z_TOT - at), axis=1)


def full_weights(g):
    forms = {
        "w_in": lambda a: {"in_p": _w_in_relaid(a)[None]},
        "w_uq": lambda a: {"uq_p": jnp.pad(_cols_from_shards(a).reshape(Q_LORA, N_HEADS, MLA_QK),
                                           ((0, 0), (0, 0), (0, MLA_PAD - MLA_QK))
                                           ).reshape(1, Q_LORA, N_HEADS * MLA_PAD)},
        "w_ukv": lambda a: {"ukv": a},
        "w_br": lambda a: {"br": [a[:, n] for n in range(3)]},
        "w_out": lambda a: {"out": a.reshape(1, D_MODEL, D_MODEL)},
        "w_xq": lambda a: {"xq": a.reshape(1, D_MODEL, 512)},
        "w_xkv": lambda a: {"xkv": a.reshape(1, D_MODEL, 1024)},
        "w_xo": lambda a: {"xo": a},
        "w_1": lambda a: {"w1": a},
        "w_2": lambda a: {"w2": a.reshape(1, D_FF, D_MODEL)},
    }
    out = {}
    for n, a in g.items():
        out.update(forms[n](a))
    return out


def _w_in_grad_shards(dp):
    d_in = []
    for s_ in range(4):
        lo, hi, segs = s_ * W_IN_SHARD, (s_ + 1) * W_IN_SHARD, []
        for o0, p0, w in sorted((o0, p0, w) for p0, o0, w in W_IN_PIECES):
            a, b = max(lo, o0), min(hi, o0 + w)
            if a < b:
                segs.append(dp[:, p0 + a - o0:p0 + b - o0])
        d_in.append(jnp.concatenate(segs, axis=1))
    return jnp.stack(d_in)


def shard_grads(dw):
    forms = {
        "in_p": lambda a: {"w_in": _w_in_grad_shards(a[0])},
        "uq_p": lambda a: {"w_uq": _cols_to_shards(
            a.reshape(Q_LORA, N_HEADS, MLA_PAD)[:, :, :MLA_QK].reshape(Q_LORA, N_HEADS * MLA_QK))},
        "ukv": lambda a: {"w_ukv": a},
        "br": lambda a: {"w_br": jnp.stack(a, axis=1)},
        "out": lambda a: {"w_out": a.reshape(4, 512, D_MODEL)},
        "xq": lambda a: {"w_xq": a.reshape(4, 512, 512)},
        "xkv": lambda a: {"w_xkv": a.reshape(4, 512, 1024)},
        "xo": lambda a: {"w_xo": a},
        "w1": lambda a: {"w_1": a},
        "w2": lambda a: {"w_2": a.reshape(4, 2048, D_MODEL)},
    }
    out = {}
    for k, a in dw.items():
        out.update(forms[k](a))
    return out


def layer_params(d, l):
    row = lambda v: v.reshape(1, -1).astype(F32)
    padto = lambda v, n: jnp.pad(row(v), ((0, 0), (0, n - v.shape[-1])))
    return dict(
        g_mix=row(d["g_mix"][l]), g_cq=row(d["g_cq"][l]), g_ckv=row(d["g_ckv"][l]),
        g_mla_q=padto(d["g_mla_q"][l], MLA_PAD), g_mla_k=padto(d["g_mla_k"][l], MLA_PAD),
        b_f=padto(d["b_f"][l], 128), g_fox_q=row(d["g_fox_q"][l]), g_fox_k=row(d["g_fox_k"][l]),
        rel_bias=d["rel_bias"][l].astype(F32), g_ch_q=row(d["g_ch_q"][l]), g_ch_k=row(d["g_ch_k"][l]),
        g_cross=row(d["g_cross"][l]), g_mem=row(d["g_mem"][l]), g_x_q=row(d["g_x_q"][l]),
        g_x_k=row(d["g_x_k"][l]), g_mlp=row(d["g_mlp"][l]))


FOX_B = Z_FOX // 128
CH_B = Z_CH // 128


def _after(small, token):
    return small if token is None else small + token[:1, :1].reshape((1,) * small.ndim)


def layer_fwd(x, mem, W, P, rope, mid_weights, last_weights):
    S = x.shape[0]
    s = {}
    s["h"] = rms_fwd(x, P["g_mix"], name="rms_d")
    z = s["z"] = mm_nn(s["h"], W["in_p"], name="mm_in", out_dtype=F32)
    s["cq_n"] = rms_fwd(z, P["g_cq"], col0=Z_CQ, width=Q_LORA, name="rms_cq")
    s["ckv_n"] = rms_fwd(z, P["g_ckv"], col0=Z_CKV, width=KV_LORA, name="rms_ckv")
    s["q_raw"] = mm_nn(s["cq_n"], W["uq_p"], name="mm_uq", out_dtype=F32)
    s["kv_raw"] = mm_nn(s["ckv_n"], W["ukv"], name="mm_ukv", out_dtype=F32)
    s["qa"] = prep_fwd([(s["q_raw"], MLA_PAD, lambda h: h)], P["g_mla_q"], name="prep_mla_q",
                       n_heads=N_HEADS, n_real=MLA_QK, rope=rope)
    s["ka"] = prep_fwd([(s["kv_raw"], 128, lambda h: 2 * h), (z, 128, lambda h: Z_KR // 128)], P["g_mla_k"],
                       name="prep_mla_k", n_heads=N_HEADS, n_real=MLA_QK, rope=rope)
    s["o_a"], s["lse_a"] = attn_fwd(s["qa"], s["ka"], s["kv_raw"], lambda h: 2 * h + 1, mode="mla",
                                    name="attn_mla", n_heads=N_HEADS)
    s["qb"] = prep_fwd([(z, DH, lambda h: FOX_B + h)], P["g_fox_q"], name="prep_h", n_heads=N_HEADS, n_real=DH)
    s["kb"] = prep_fwd([(z, DH, lambda h: FOX_B + N_HEADS + h)], P["g_fox_k"], name="prep_h",
                       n_heads=N_HEADS, n_real=DH)
    _, cum_t = fox_cum_fwd(z, P["b_f"])
    s["cq"] = cum_t[:N_HEADS].reshape(N_HEADS, S, 1)
    s["ck"] = cum_t[:N_HEADS].reshape(N_HEADS, 1, S)
    s["o_b"], s["lse_b"] = attn_fwd(s["qb"], s["kb"], z, lambda h: FOX_B + 2 * N_HEADS + h, mode="fox",
                                    name="attn_fox", n_heads=N_HEADS, cq=s["cq"], ck=s["ck"])
    s["qc"] = prep_fwd([(z, DH, lambda h: CH_B + h)], P["g_ch_q"], name="prep_h", n_heads=N_HEADS, n_real=DH)
    s["kc"] = prep_fwd([(z, DH, lambda h: CH_B + N_HEADS + h)], P["g_ch_k"], name="prep_h",
                       n_heads=N_HEADS, n_real=DH)
    s["tiles"] = relbias_tiles(P["rel_bias"], P["rel_onehot"])
    s["o_c"], s["lse_c"] = attn_fwd(s["qc"], s["kc"], z, lambda h: CH_B + 2 * N_HEADS + h, mode="chunk",
                                    name="attn_chunk", n_heads=N_HEADS, tiles=s["tiles"])
    mid, token = mid_weights(s["o_c"])
    W = dict(W, **mid)
    P = dict(P, g_cross=_after(P["g_cross"], token))
    s["projs"] = [mm_nn(o, W["br"][n], name="mm_br", out_dtype=F32)
                  for n, o in enumerate((s["o_a"], s["o_b"], s["o_c"]))]
    s["merged"] = merge_fwd(z, s["projs"])
    x1 = s["x1"] = mm_nn(s["merged"], W["out"], name="mm_out", out_dtype=F32, res=x)
    s["hq"] = rms_fwd(x1, P["g_cross"], name="rms_d")
    s["xq_raw"] = mm_nn(s["hq"], W["xq"], name="mm_xq", out_dtype=F32)
    s["mem_n"] = rms_fwd(mem, P["g_mem"], name="rms_mem")
    s["mkv"] = mm_nn(s["mem_n"], W["xkv"], name="mm_xkv", out_dtype=F32)
    s["qx"] = prep_fwd([(s["xq_raw"], DH, lambda h: h)], P["g_x_q"], name="prep_xq", n_heads=X_HEADS, n_real=DH)
    s["kx"] = prep_fwd([(s["mkv"], DH, lambda h: h)], P["g_x_k"], name="prep_xk", n_heads=X_HEADS, n_real=DH)
    s["o_x"], s["lse_x"] = attn_fwd(s["qx"], s["kx"], s["mkv"], lambda h: X_HEADS + h, mode="cross",
                                    name="attn_cross", n_heads=X_HEADS)
    x2 = s["x2"] = mm_nn(s["o_x"], W["xo"], name="mm_xo", out_dtype=F32, res=x1)
    s["hm"] = rms_fwd(x2, P["g_mlp"], name="rms_d")
    last, _ = last_weights(s["hm"])
    W = dict(W, **last)
    s["a1"], s["act"] = mm_nn(s["hm"], W["w1"], name="mm_w1", out_dtype=BF16, relu2=True)
    x3 = mm_nn(s["act"], W["w2"], name="mm_w2", out_dtype=F32, res=x2)
    return x3, s, W


LATE_GRADS = ("w2", "w1", "xo", "xq", "xkv", "out", "br")


def layer_bwd(g, x, mem, W, P, rope, s, late_grads_done, early_grads_done):
    S = x.shape[0]
    z = s["z"]
    dw, ds = {}, {}
    da1 = mm_nt(g, W["w2"], name="mm_w2_dx", out_dtype=BF16, relu_mul=s["a1"])
    dw["w2"] = mm_tn(s["act"], g, nb=1, name="mm_w2_dw", out_dtype=BF16)
    dhm = mm_nt(da1, W["w1"], name="mm_w1_dx", out_dtype=F32)
    dw["w1"] = mm_tn(s["hm"], da1, nb=4, name="mm_w1_dw", out_dtype=BF16)
    g2, ds["g_mlp"] = rms_bwd(s["x2"], P["g_mlp"], dhm, name="rms_d_bwd", res=g)
    do_x = mm_nt(g2, W["xo"], name="mm_xo_dx", out_dtype=BF16)
    dw["xo"] = mm_tn(s["o_x"], g2, nb=4, name="mm_xo_dw", out_dtype=BF16)
    dqx, dkx, dvx = attn_bwd(s["qx"], s["kx"], s["mkv"], lambda h: X_HEADS + h, s["o_x"], do_x, s["lse_x"],
                             mode="cross", name="attn_cross_bwd", n_heads=X_HEADS)
    dxq_raw, ds["g_x_q"] = prep_bwd_q(s["xq_raw"], dqx, P["g_x_q"], name="prep_xq_bwd", n_heads=X_HEADS,
                                      dh=DH, n_real=DH)
    dmkv, dgk = prep_bwd_groups(s["mkv"], 0, [dkx, dvx], [P["g_x_k"], None], name="prep_xkv_bwd",
                                n_heads=X_HEADS, kinds=("norm", "copy"))
    ds["g_x_k"] = dgk[0]
    dhq = mm_nt(dxq_raw, W["xq"], name="mm_xq_dx", out_dtype=F32)
    dw["xq"] = mm_tn(s["hq"], dxq_raw, nb=1, name="mm_xq_dw", out_dtype=BF16)
    dmem_n = mm_nt(dmkv, W["xkv"], name="mm_xkv_dx", out_dtype=F32)
    dw["xkv"] = mm_tn(s["mem_n"], dmkv, nb=1, name="mm_xkv_dw", out_dtype=BF16)
    _, ds["g_mem"] = rms_bwd(mem, P["g_mem"], dmem_n, name="rms_mem_bwd", need_dx=False)
    g1, ds["g_cross"] = rms_bwd(s["x1"], P["g_cross"], dhq, name="rms_d_bwd", res=g2)
    dmerged = mm_nt(g1, W["out"], name="mm_out_dx", out_dtype=F32)
    dw["out"] = mm_tn(s["merged"], g1, nb=1, name="mm_out_dw", out_dtype=BF16)
    dgl, dproj = merge_bwd(z, s["projs"], dmerged)
    outs = (s["o_a"], s["o_b"], s["o_c"])
    do = [mm_nt(dproj[n], W["br"][n], name="mm_br_dx", out_dtype=BF16) for n in range(3)]
    dw["br"] = [mm_tn(outs[n], dproj[n], nb=4, name="mm_br_dw", out_dtype=BF16) for n in range(3)]
    token = late_grads_done({k: dw.pop(k) for k in LATE_GRADS})
    dqc, dkc, dvc, dtiles = attn_bwd(s["qc"], s["kc"], z, lambda h: CH_B + 2 * N_HEADS + h, s["o_c"], do[2],
                                     s["lse_c"], mode="chunk", name="attn_chunk_bwd", n_heads=N_HEADS,
                                     tiles=_after(s["tiles"], token))
    d_ch, dg_ch = prep_bwd_groups(z, CH_B, [dqc, dkc, dvc], [P["g_ch_q"], P["g_ch_k"], None],
                                  name="prep_h_bwd", n_heads=N_HEADS, kinds=("norm", "norm", "copy"))
    ds["g_ch_q"], ds["g_ch_k"] = dg_ch[0], dg_ch[1]
    ds["rel_bias"] = relbias_tiles_bwd(dtiles, P["rel_onehot"])
    dqb, dkb, dvb, dcq, dck = attn_bwd(s["qb"], s["kb"], z, lambda h: FOX_B + 2 * N_HEADS + h, s["o_b"], do[1],
                                       s["lse_b"], mode="fox", name="attn_fox_bwd", n_heads=N_HEADS,
                                       cq=s["cq"], ck=s["ck"])
    d_fox, dg_fox = prep_bwd_groups(z, FOX_B, [dqb, dkb, dvb], [P["g_fox_q"], P["g_fox_k"], None],
                                    name="prep_h_bwd", n_heads=N_HEADS, kinds=("norm", "norm", "copy"))
    ds["g_fox_q"], ds["g_fox_k"] = dg_fox[0], dg_fox[1]
    dcum = jnp.pad((dcq[:, :, 0] + dck[:, 0, :]).T, ((0, 0), (0, 128 - N_HEADS)))
    dff, dbf = fox_cum_bwd(z, P["b_f"], dcum)
    ds["b_f"] = dbf[:, :N_HEADS]
    dqa, dka, dva = attn_bwd(s["qa"], s["ka"], s["kv_raw"], lambda h: 2 * h + 1, s["o_a"], do[0], s["lse_a"],
                             mode="mla", name="attn_mla_bwd", n_heads=N_HEADS)
    dq_raw, dgq = prep_bwd_q(s["q_raw"], dqa, P["g_mla_q"], name="prep_mla_q_bwd", n_heads=N_HEADS,
                             dh=MLA_PAD, n_real=MLA_QK, rope=rope)
    dkv_raw, dkr, dgk = prep_bwd_mla_k(s["kv_raw"], z, dka, dva, P["g_mla_k"], rope, name="prep_mla_k_bwd")
    ds["g_mla_q"], ds["g_mla_k"] = dgq[:, :MLA_QK], dgk[:, :MLA_QK]
    dcq_n = mm_nt(dq_raw, W["uq_p"], name="mm_uq_dx", out_dtype=F32)
    dw["uq_p"] = mm_tn(s["cq_n"], dq_raw, nb=1, name="mm_uq_dw", out_dtype=BF16)
    dckv_n = mm_nt(dkv_raw, W["ukv"], name="mm_ukv_dx", out_dtype=F32)
    dw["ukv"] = mm_tn(s["ckv_n"], dkv_raw, nb=4, name="mm_ukv_dw", out_dtype=BF16)
    d_cq, ds["g_cq"] = rms_bwd(z, P["g_cq"], dcq_n, name="rms_cq_bwd", col0=Z_CQ, width=Q_LORA, dx_dtype=BF16)
    d_ckv, ds["g_ckv"] = rms_bwd(z, P["g_ckv"], dckv_n, name="rms_ckv_bwd", col0=Z_CKV, width=KV_LORA,
                                 dx_dtype=BF16)
    dz = jnp.concatenate([d_cq, d_ckv, dkr.astype(BF16), dff.astype(BF16), d_fox, d_ch, *dgl], axis=1)
    dw["in_p"] = mm_tn(s["h"], dz, nb=1, name="mm_in_dw", out_dtype=BF16)
    token = early_grads_done(dw)
    dh = mm_nt(dz, W["in_p"], name="mm_in_dx", out_dtype=F32, after=token)
    g0, ds["g_mix"] = rms_bwd(x, P["g_mix"], dh, name="rms_d_bwd", res=g1)
    return g0, ds, token


def local_step(x, mem, target, Ps, weights_of, grads_done):
    rope = rope_tables(x.shape[0])
    onehot = relbias_onehot()
    Ps = [dict(P, rel_onehot=onehot) for P in Ps]
    Ws, saved, xs = [], [], [x]
    for l, P in enumerate(Ps):
        W_early, mid_weights, last_weights, token = weights_of(l, xs[-1])
        y, s, W = layer_fwd(xs[-1], mem, W_early, dict(P, g_mix=_after(P["g_mix"], token)), rope,
                            mid_weights, last_weights)
        xs.append(y)
        saved.append(s)
        Ws.append(W)
    g, loss = loss_head(xs[-1], target)
    dss, token = [], None
    for l in reversed(range(len(Ps))):
        g, ds, token = layer_bwd(g, xs[l], mem, Ws[l], dict(Ps[l], g_mlp=_after(Ps[l]["g_mlp"], token)), rope,
                                 saved[l], functools.partial(grads_done, l, "late"),
                                 functools.partial(grads_done, l, "early"))
        dss.append(ds)
    return loss, g, dss[::-1]


HBM_SPEC = pl.BlockSpec(memory_space=pltpu.HBM)


def _place():
    return lax.axis_index("x"), lax.axis_index("y"), lax.axis_index("c")


def _other_chips(x, y):
    return [(1 - x, y), (x, 1 - y), (1 - x, 1 - y)]


def _remote(src, dst, send_sems, recv_sems, k, to):
    return pltpu.make_async_remote_copy(src_ref=src, dst_ref=dst, send_sem=send_sems.at[k],
                                        recv_sem=recv_sems.at[k], device_id=to, device_id_type=MESH)


def _comm_call(body, *, name, ins, out_shapes, n_sem):
    return pl.pallas_call(
        body, out_shape=tuple(out_shapes), in_specs=[HBM_SPEC] * len(ins),
        out_specs=tuple([HBM_SPEC] * len(out_shapes)),
        scratch_shapes=[pltpu.SemaphoreType.DMA((n_sem,)), pltpu.SemaphoreType.DMA((n_sem,))],
        name=name, interpret=False)(*ins)


SEM_SPEC = pl.BlockSpec(memory_space=pltpu.SEMAPHORE)
SPLIT_EFFECT = pltpu.SideEffectType.DATAFLOW_SIDE_EFFECTING


def _chip_exchange_copies(kind, srcs, lands, send_sems, recv_sems):
    x, y, c = _place()
    me = 2 * x + y
    cps = []
    for i, (src, land) in enumerate(zip(srcs, lands)):
        for j, (cx, cy) in enumerate(_other_chips(x, y)):
            if kind == "gather":
                h = src.shape[0] // 2
                s_ref, d_ref = src.at[pl.ds(c * h, h), :], land.at[me, pl.ds(c * h, h), :]
            else:
                s_ref, d_ref = src.at[2 * cx + cy], land.at[me]
            cps.append(_remote(s_ref, d_ref, send_sems, recv_sems, 3 * i + j, (cx, cy, c)))
    return cps


def chip_exchange_start(kind, srcs, land_shapes, *, name, after=None):
    n = len(srcs)
    n_in = 2 * n + (after is not None)

    def body(*refs):
        send_sems, recv_sems = refs[n_in], refs[n_in + 1]
        for cp in _chip_exchange_copies(kind, refs[:n], refs[n:2 * n], send_sems, recv_sems):
            cp.start()
        refs[-1][...] = jnp.zeros_like(refs[-1])

    hbm = lambda a: pltpu.with_memory_space_constraint(a, pltpu.HBM)
    ins = [hbm(s) for s in srcs] + [hbm(lax.empty(s.shape, s.dtype)) for s in land_shapes]
    sems = pltpu.SemaphoreType.DMA((3 * n,))
    out = pl.pallas_call(
        body, name=name, interpret=False,
        out_shape=(sems, sems, *[pltpu.HBM(a.shape, a.dtype) for a in ins], _sds((8, 128), F32)),
        in_specs=[HBM_SPEC] * (2 * n) + [pl.BlockSpec(memory_space=pl.ANY)] * (after is not None),
        out_specs=(SEM_SPEC, SEM_SPEC, *[HBM_SPEC] * (2 * n), pl.BlockSpec(memory_space=pltpu.VMEM)),
        input_output_aliases={i: 2 + i for i in range(2 * n)},
        compiler_params=pltpu.CompilerParams(has_side_effects=SPLIT_EFFECT))(
            *ins, *([after] if after is not None else []))
    return out[0], out[1], list(out[2:2 + n]), list(out[2 + n:2 + 2 * n]), out[-1]


def chip_exchange_wait(kind, started, after, *, name):
    send_sems, recv_sems, srcs, lands, _ = started
    n = len(srcs)

    def body(*refs):
        send_sems, recv_sems = refs[2 * n], refs[2 * n + 1]
        for cp in _chip_exchange_copies(kind, refs[:n], refs[n:2 * n], send_sems, recv_sems):
            cp.wait_send()
            cp.wait_recv()

    thru = srcs + lands
    out = pl.pallas_call(
        body, name=name, interpret=False,
        out_shape=tuple(pltpu.HBM(a.shape, a.dtype) for a in thru),
        in_specs=[HBM_SPEC] * (2 * n) + [SEM_SPEC, SEM_SPEC, pl.BlockSpec(memory_space=pl.ANY)],
        out_specs=tuple([HBM_SPEC] * (2 * n)),
        input_output_aliases={i: i for i in range(2 * n)},
        compiler_params=pltpu.CompilerParams(has_side_effects=SPLIT_EFFECT))(*thru, send_sems, recv_sems, after)
    return list(out[:n]), list(out[n:])


def gather_to_sibling(lands):
    n = len(lands)

    def body(*refs):
        outs = refs[n:2 * n]
        send_sems, recv_sems = refs[2 * n:]
        x, y, c = _place()
        sends = []
        for i in range(n):
            h = outs[i].shape[1] // 2
            for j, (cx, cy) in enumerate(_other_chips(x, y)):
                landed = outs[i].at[2 * cx + cy, pl.ds(c * h, h), :]
                cp = _remote(landed, landed, send_sems, recv_sems, 3 * i + j, (x, y, 1 - c))
                cp.start()
                sends.append(cp)
        for cp in sends:
            cp.wait()

    return pl.pallas_call(
        body, out_shape=tuple(_sds(a.shape, a.dtype) for a in lands), in_specs=[HBM_SPEC] * n,
        out_specs=tuple([HBM_SPEC] * n), input_output_aliases={i: i for i in range(n)},
        scratch_shapes=[pltpu.SemaphoreType.DMA((3 * n,)), pltpu.SemaphoreType.DMA((3 * n,))],
        name="gather_to_sibling", interpret=False)(*lands)


def rs_to_sibling(parts):
    n = len(parts)

    def body(*refs):
        srcs, outs = refs[:n], refs[n:2 * n]
        send_sems, recv_sems = refs[2 * n:]
        x, y, c = _place()
        sends = []
        for i in range(n):
            h = srcs[i].shape[1] // 2
            cp = _remote(srcs[i].at[:, pl.ds((1 - c) * h, h), :], outs[i], send_sems, recv_sems, i, (x, y, 1 - c))
            cp.start()
            sends.append(cp)
        for cp in sends:
            cp.wait()

    return _comm_call(body, name="rs_to_sibling", ins=parts, n_sem=n,
                      out_shapes=[_sds((4, p.shape[1] // 2, p.shape[2]), p.dtype) for p in parts])


def rs_share_halves(ts):
    n = len(ts)

    def body(*refs):
        srcs, outs = refs[:n], refs[n:2 * n]
        send_sems, recv_sems = refs[2 * n:]
        x, y, c = _place()
        sends = []
        for i in range(n):
            cp = _remote(srcs[i], outs[i].at[c], send_sems, recv_sems, i, (x, y, 1 - c))
            cp.start()
            sends.append(cp)
        for cp in sends:
            cp.wait()

    return _comm_call(body, name="rs_share_halves", ins=ts, n_sem=n,
                      out_shapes=[_sds((2,) + t.shape, t.dtype) for t in ts])


def add_pair(part, got, half_idx):
    _, a, b = part.shape
    h = a // 2
    tr = _pick(h, (512, 256, 128))

    def body(c_ref, p_ref, g_ref, o_ref):
        o_ref[...] = (p_ref[...].astype(F32) + g_ref[...].astype(F32)).astype(o_ref.dtype)

    spec = pl.BlockSpec((None, tr, b), lambda s_, i, c_ref: (s_, i, 0))
    grid_spec = pltpu.PrefetchScalarGridSpec(
        num_scalar_prefetch=1, grid=(4, h // tr),
        in_specs=[pl.BlockSpec((None, None, tr, b), lambda s_, i, c_ref: (s_, c_ref[0], i, 0)), spec],
        out_specs=spec)
    return pl.pallas_call(
        body, out_shape=_sds((4, h, b), part.dtype), grid_spec=grid_spec, name="rs_add_pair", interpret=False,
        compiler_params=pltpu.CompilerParams(vmem_limit_bytes=VMEM_LIMIT_BYTES))(
            half_idx, part.reshape(4, 2, h, b), got)


def sum_slots(r):
    _, h, b = r.shape
    tr = _pick(h, (512, 256, 128))

    def body(r0, r1, r2, r3, o_ref):
        o_ref[...] = ((r0[...].astype(F32) + r1[...].astype(F32)) + r2[...].astype(F32)) + r3[...].astype(F32)

    specs = [pl.BlockSpec((None, tr, b), functools.partial(lambda i, s_: (s_, i, 0), s_=s_)) for s_ in range(4)]
    return _pcall(body, name="rs_sum_slots", out_shape=_sds((h, b), F32), grid=(h // tr,),
                  in_specs=specs, out_specs=pl.BlockSpec((tr, b), lambda i: (i, 0)))(r, r, r, r)


def reduce_scatter_start(parts, core, *, name):
    half_idx = core.reshape(1).astype(jnp.int32)
    gots = rs_to_sibling(parts)
    chip_sums = [add_pair(p, g, half_idx) for p, g in zip(parts, gots)]
    return chip_exchange_start("scatter", chip_sums, chip_sums, name=name)


def reduce_scatter_finish(started, after, chip, core, *, name):
    chip_sums, slots = chip_exchange_wait("scatter", started, after, name=name)
    slots = [lax.dynamic_update_slice(s_, lax.dynamic_index_in_dim(q, chip, 0, keepdims=True), (chip, 0, 0))
             for s_, q in zip(slots, chip_sums)]
    halves = [sum_slots(s_) for s_ in slots]
    both = rs_share_halves(halves)
    both = [lax.dynamic_update_slice(o, t[None], (core, 0, 0)) for o, t in zip(both, halves)]
    return [o.reshape(2 * o.shape[1], o.shape[2]) for o in both]


def allreduce_small(v):
    Rs = v.shape[0]

    def body(v_ref, o_ref, buf, send_sems, recv_sems):
        x, y, c = _place()
        me = 4 * x + 2 * y + c
        buf[me] = v_ref[...]
        flips = [(fx, fy, fc) for fx in (0, 1) for fy in (0, 1) for fc in (0, 1)][1:]
        sends = []
        for k, (fx, fy, fc) in enumerate(flips):
            to = ((1 - x) if fx else x, (1 - y) if fy else y, (1 - c) if fc else c)
            cp = _remote(v_ref, buf.at[me], send_sems, recv_sems, k, to)
            cp.start()
            sends.append(cp)
        for cp in sends:
            cp.wait()
        acc = buf[0]
        for d in range(1, 8):
            acc = acc + buf[d]
        o_ref[...] = acc

    vm = pl.BlockSpec(memory_space=pltpu.VMEM)
    return pl.pallas_call(
        body, out_shape=_sds((Rs, 128), F32), in_specs=[vm], out_specs=vm,
        scratch_shapes=[pltpu.VMEM((8, Rs, 128), F32), pltpu.SemaphoreType.DMA((7,)),
                        pltpu.SemaphoreType.DMA((7,))],
        name="allreduce_small", interpret=False)(v)


INPUT_NAMES = (("x", "mem") + WEIGHT_ORDER + ("loss_target",) + tuple("m_" + n for n in WEIGHT_ORDER)
               + tuple("v_" + n for n in WEIGHT_ORDER))


def _pack_small(vals, n_layers, extra=None):
    flat = jnp.concatenate([vals[n].reshape(n_layers, -1).astype(F32) for n in SMALL_ORDER], axis=1).reshape(-1)
    if extra is not None:
        flat = jnp.concatenate([flat, extra.reshape(-1)])
    n = flat.shape[0]
    rows = -(-n // 1024) * 8
    return jnp.pad(flat, (0, rows * 128 - n)).reshape(rows, 128)


def _unpack_small(packed, like, n_layers):
    per_layer = sum(int(np.prod(like[n].shape[1:])) for n in SMALL_ORDER)
    body = packed.reshape(-1)[:n_layers * per_layer].reshape(n_layers, per_layer)
    out, off = {}, 0
    for n in SMALL_ORDER:
        k = int(np.prod(like[n].shape[1:]))
        out[n] = body[:, off:off + k].reshape(like[n].shape)
        off += k
    return out, packed.reshape(-1)[n_layers * per_layer]


def kernel(x, mem, g_mix, w_in, g_cq, w_uq, g_ckv, w_ukv, g_mla_q, g_mla_k, b_f, g_fox_q, g_fox_k, rel_bias, g_ch_q, g_ch_k, w_br, w_out, g_cross, g_mem, w_xq, w_xkv, g_x_q, g_x_k, w_xo, g_mlp, w_1, w_2, loss_target, m_g_mix, m_w_in, m_g_cq, m_w_uq, m_g_ckv, m_w_ukv, m_g_mla_q, m_g_mla_k, m_b_f, m_g_fox_q, m_g_fox_k, m_rel_bias, m_g_ch_q, m_g_ch_k, m_w_br, m_w_out, m_g_cross, m_g_mem, m_w_xq, m_w_xkv, m_g_x_q, m_g_x_k, m_w_xo, m_g_mlp, m_w_1, m_w_2, v_g_mix, v_w_in, v_g_cq, v_w_uq, v_g_ckv, v_w_ukv, v_g_mla_q, v_g_mla_k, v_b_f, v_g_fox_q, v_g_fox_k, v_rel_bias, v_g_ch_q, v_g_ch_k, v_w_br, v_w_out, v_g_cross, v_g_mem, v_w_xq, v_w_xkv, v_g_x_q, v_g_x_k, v_w_xo, v_g_mlp, v_w_1, v_w_2):
    d = dict(zip(INPUT_NAMES, (x, mem, g_mix, w_in, g_cq, w_uq, g_ckv, w_ukv, g_mla_q, g_mla_k, b_f, g_fox_q, g_fox_k, rel_bias, g_ch_q, g_ch_k, w_br, w_out, g_cross, g_mem, w_xq, w_xkv, g_x_q, g_x_k, w_xo, g_mlp, w_1, w_2, loss_target, m_g_mix, m_w_in, m_g_cq, m_w_uq, m_g_ckv, m_w_ukv, m_g_mla_q, m_g_mla_k, m_b_f, m_g_fox_q, m_g_fox_k, m_rel_bias, m_g_ch_q, m_g_ch_k, m_w_br, m_w_out, m_g_cross, m_g_mem, m_w_xq, m_w_xkv, m_g_x_q, m_g_x_k, m_w_xo, m_g_mlp, m_w_1, m_w_2, v_g_mix, v_w_in, v_g_cq, v_w_uq, v_g_ckv, v_w_ukv, v_g_mla_q, v_g_mla_k, v_b_f, v_g_fox_q, v_g_fox_k, v_rel_bias, v_g_ch_q, v_g_ch_k, v_w_br, v_w_out, v_g_cross, v_g_mem, v_w_xq, v_w_xkv, v_g_x_q, v_g_x_k, v_w_xo, v_g_mlp, v_w_1, v_w_2)))
    n_layers = g_mix.shape[0]
    assert x.shape[0] == 1 and x.shape[2] == D_MODEL and mem.shape[1:] == (MEM_LEN, D_MODEL)
    for n in PACK_ORDER:
        assert d[n].shape[1:] == SHARD_SHAPES[n], (n, d[n].shape)

    chip = 2 * lax.axis_index("x") + lax.axis_index("y")
    core = lax.axis_index("c")

    def start_gather(l, names, after):
        mine = [_rows2d(d[n][l].astype(BF16), n) for n in names]
        return chip_exchange_start("gather", mine, [_sds((4,) + m_.shape, BF16) for m_ in mine],
                                   name=f"gather_start_{l}_{names[0]}", after=after)

    def finish_gather(l, names, started, after):
        mine, lands = chip_exchange_wait("gather", started, after, name=f"gather_wait_{l}_{names[0]}")
        full = [lax.dynamic_update_slice(t, m_[None], (chip, 0, 0)) for t, m_ in zip(gather_to_sibling(lands), mine)]
        return mine, full_weights({n: f.reshape((4,) + SHARD_SHAPES[n]) for n, f in zip(names, full)})

    in_flight = {"early": start_gather(0, EARLY_WEIGHTS, x)}

    def weights_of(l, x_l):
        mine, early = finish_gather(l, EARLY_WEIGHTS, in_flight.pop("early"), x_l)
        in_flight["mid"] = start_gather(l, MID_WEIGHTS, mine[0])
        in_flight["last"] = start_gather(l, LAST_WEIGHTS, in_flight["mid"][4])

        def mid_weights(after):
            mine, mid = finish_gather(l, MID_WEIGHTS, in_flight.pop("mid"), after)
            if l + 1 == n_layers:
                return mid, None
            in_flight["early"] = start_gather(l + 1, EARLY_WEIGHTS, mine[0])
            return mid, in_flight["early"][4]

        def last_weights(after):
            return finish_gather(l, LAST_WEIGHTS, in_flight.pop("last"), after)[1], None

        return early, mid_weights, last_weights, in_flight["mid"][4] + in_flight["last"][4]

    scatters = []

    def grads_done(l, group, dw):
        names = LATE_WEIGHTS if group == "late" else EARLY_WEIGHTS
        sg = shard_grads(dw)
        started = reduce_scatter_start([_rows2d(sg[n], n) for n in names], core, name=f"scatter_start_{l}_{names[0]}")
        scatters.append((l, names, started))
        return started[4]

    Ps = [layer_params(d, l) for l in range(n_layers)]
    loss, dx, dss = local_step(x[0], mem[0], loss_target[0], Ps, weights_of, grads_done)
    big = [{} for _ in range(n_layers)]
    for l, names, started in scatters:
        done = reduce_scatter_finish(started, dx, chip, core, name=f"scatter_wait_{l}_{names[0]}")
        big[l].update(zip(names, done))

    small_local = {n: jnp.stack([dss[l][n].reshape(d[n].shape[1:]) for l in range(n_layers)]) for n in SMALL_ORDER}
    small_sum, loss_sum = _unpack_small(allreduce_small(_pack_small(small_local, n_layers, extra=loss)),
                                        {n: d[n] for n in SMALL_ORDER}, n_layers)

    grads, delta, new_m, new_v = {}, {}, {}, {}
    for n in PACK_ORDER:
        outs = adamw(_rows2d(d[n], n), [big[l][n] for l in range(n_layers)], _rows2d(d["m_" + n], n),
                     _rows2d(d["v_" + n], n), name="adamw_" + n)
        grads[n], delta[n], new_m[n], new_v[n] = (o.reshape(d[n].shape) for o in outs)
    like = {n: d[n] for n in SMALL_ORDER}
    sm = adamw(_pack_small(like, n_layers)[None], [_pack_small(small_sum, n_layers)],
               _pack_small({n: d["m_" + n] for n in SMALL_ORDER}, n_layers)[None],
               _pack_small({n: d["v_" + n] for n in SMALL_ORDER}, n_layers)[None], name="adamw_small")
    for res, src in zip((grads, delta, new_m, new_v), sm):
        res.update(_unpack_small(src[0], like, n_layers)[0])

    return (loss_sum, dx[None], *[grads[n] for n in WEIGHT_ORDER], *[delta[n] for n in WEIGHT_ORDER],
            *[new_m[n] for n in WEIGHT_ORDER], *[new_v[n] for n in WEIGHT_ORDER])
```

```python
import functools

import numpy as np
import jax
import jax.numpy as jnp
from jax import lax
from jax.experimental import pallas as pl
from jax.experimental.pallas import tpu as pltpu

F32 = jnp.float32
BF16 = jnp.bfloat16
MXU_DTYPE = jnp.bfloat16
MESH = pl.DeviceIdType.MESH

D_MODEL = 2048
MIX_W = 1024
N_HEADS = 8
DH = 128
MLA_NOPE = 128
MLA_ROPE = 64
MLA_QK = MLA_NOPE + MLA_ROPE
MLA_PAD = 256
Q_LORA = 512
KV_LORA = 256
CHUNK = 64
CHUNK_SHIFT = CHUNK.bit_length() - 1
LEFT_CHUNKS = 8
REL_CLIP = 128
N_REL = 2 * REL_CLIP + 1
X_HEADS = 4
MEM_LEN = 256
D_FF = 8192
ROPE_THETA = 10000.0
EPS = 1e-6
NEG = -1e30

Z_CQ, Z_CKV, Z_KR, Z_FF = 0, 512, 768, 896
ZS_W = 1024
Z_FOX = 1024
Z_CH = Z_FOX + 3 * MIX_W
Z_GATE = Z_CH + 3 * MIX_W
Z_TOT = Z_GATE + 3 * D_MODEL
W_IN_CUTS = (0, 512, 768, 832, 3904, 3912, 6984, 13128)
W_IN_SHARD = W_IN_CUTS[-1] // 4
W_IN_PIECES = ((0, 0, 832), (Z_FF, 3904, 8), (Z_FOX, 832, 3072), (Z_CH, 3912, 9216))

ADAM_LR, ADAM_B1, ADAM_B2, ADAM_EPS, ADAM_WD, ADAM_STEP = 0.001, 0.9, 0.999, 1e-08, 0.01, 10

VMEM_LIMIT_BYTES = 56 * 1024 * 1024
PACK_ORDER = ("w_uq", "w_ukv", "w_br", "w_out", "w_xq", "w_xkv", "w_xo", "w_1", "w_2", "w_in")
SMALL_ORDER = ("g_mix", "g_cq", "g_ckv", "g_mla_q", "g_mla_k", "b_f", "g_fox_q", "g_fox_k", "rel_bias",
               "g_ch_q", "g_ch_k", "g_cross", "g_mem", "g_x_q", "g_x_k", "g_mlp")
WEIGHT_ORDER = ("g_mix", "w_in", "g_cq", "w_uq", "g_ckv", "w_ukv", "g_mla_q", "g_mla_k", "b_f", "g_fox_q",
                "g_fox_k", "rel_bias", "g_ch_q", "g_ch_k", "w_br", "w_out", "g_cross", "g_mem", "w_xq",
                "w_xkv", "g_x_q", "g_x_k", "w_xo", "g_mlp", "w_1", "w_2")


def _pick(n, prefs):
    for p in prefs:
        if n % p == 0:
            return p
    raise ValueError(f"no block size among {prefs} divides {n}")


def _pcall(body, *, name, out_shape, in_specs, out_specs, grid=(), scratch=(), aliases=None):
    return pl.pallas_call(
        body, out_shape=out_shape, grid=grid, in_specs=in_specs, out_specs=out_specs,
        scratch_shapes=scratch, name=name, interpret=False,
        input_output_aliases=aliases or {},
        compiler_params=pltpu.CompilerParams(vmem_limit_bytes=VMEM_LIMIT_BYTES))


def _sds(shape, dtype):
    return jax.ShapeDtypeStruct(tuple(shape), dtype)


def _mx(v):
    return v.astype(MXU_DTYPE)


def mm_nn(a, b3, *, name, out_dtype, a_col0=0, res=None, relu2=False):
    M = a.shape[0]
    nb, K, Ns = b3.shape
    N = nb * Ns
    tm = _pick(M, (1024, 512, 256, 128))
    tk = _pick(K, (2048, 1024, 512, 256))
    tn = _pick(Ns, (512, 256, 128))
    assert a_col0 % tk == 0
    nk, nbs, ka0 = K // tk, Ns // tn, a_col0 // tk
    n_out = 2 if relu2 else 1

    def body(*refs):
        a_ref, b_ref = refs[0], refs[1]
        pos = 2
        res_ref = None
        if res is not None:
            res_ref = refs[pos]
            pos += 1
        outs = refs[pos:pos + n_out]
        acc_ref = refs[pos + n_out] if nk > 1 else None
        part = jnp.dot(_mx(a_ref[...]), _mx(b_ref[...]), preferred_element_type=F32)

        def finish(acc):
            if res_ref is not None:
                acc = acc + res_ref[...]
            outs[0][...] = acc.astype(outs[0].dtype)
            if relu2:
                r = jnp.maximum(acc, 0.0)
                outs[1][...] = (r * r).astype(outs[1].dtype)

        if nk == 1:
            finish(part)
        else:
            k = pl.program_id(2)

            @pl.when(k == 0)
            def _():
                acc_ref[...] = part

            @pl.when(k > 0)
            def _():
                acc_ref[...] += part

            @pl.when(k == nk - 1)
            def _():
                finish(acc_ref[...])

    in_specs = [pl.BlockSpec((tm, tk), lambda i, j, k: (i, ka0 + k)),
                pl.BlockSpec((None, tk, tn), lambda i, j, k: (j // nbs, k, j % nbs))]
    args = [a, b3]
    if res is not None:
        in_specs.append(pl.BlockSpec((tm, tn), lambda i, j, k: (i, j)))
        args.append(res)
    o_spec = pl.BlockSpec((tm, tn), lambda i, j, k: (i, j))
    if relu2:
        out_shape, out_specs = (_sds((M, N), out_dtype), _sds((M, N), out_dtype)), (o_spec, o_spec)
    else:
        out_shape, out_specs = _sds((M, N), out_dtype), o_spec
    scratch = (pltpu.VMEM((tm, tn), F32),) if nk > 1 else ()
    return _pcall(body, name=name, out_shape=out_shape, grid=(M // tm, N // tn, nk),
                  in_specs=in_specs, out_specs=out_specs, scratch=scratch)(*args)


def mm_nt(a, b3, *, name, out_dtype, a_col0=0, res=None, relu_mul=None, after=None):
    M = a.shape[0]
    nb, K, Ns = b3.shape
    tm = _pick(M, (1024, 512, 256, 128))
    tk = _pick(K, (1024, 512, 256))
    tn = _pick(Ns, (2048, 1024, 512, 256, 128) if nb == 1 and Ns <= 2048 else (1024, 512, 256, 128))
    assert a_col0 % tn == 0
    nbs = Ns // tn
    nn, a0 = nb * nbs, a_col0 // tn

    def body(*refs):
        a_ref, b_ref = refs[0], refs[1]
        pos = 2
        mul_ref = res_ref = None
        if relu_mul is not None:
            mul_ref = refs[pos]
            pos += 1
        if res is not None:
            res_ref = refs[pos]
            pos += 1
        pos += after is not None
        o_ref = refs[pos]
        acc_ref = refs[pos + 1] if nn > 1 else None
        part = lax.dot_general(_mx(a_ref[...]), _mx(b_ref[...]), (((1,), (1,)), ((), ())),
                               preferred_element_type=F32)

        def finish(acc):
            if mul_ref is not None:
                acc = acc * (2.0 * jnp.maximum(mul_ref[...].astype(F32), 0.0))
            if res_ref is not None:
                acc = acc + res_ref[...]
            o_ref[...] = acc.astype(o_ref.dtype)

        if nn == 1:
            finish(part)
        else:
            j = pl.program_id(2)

            @pl.when(j == 0)
            def _():
                acc_ref[...] = part

            @pl.when(j > 0)
            def _():
                acc_ref[...] += part

            @pl.when(j == nn - 1)
            def _():
                finish(acc_ref[...])

    in_specs = [pl.BlockSpec((tm, tn), lambda i, kk, j: (i, a0 + j)),
                pl.BlockSpec((None, tk, tn), lambda i, kk, j: (j // nbs, kk, j % nbs))]
    args = [a, b3]
    for extra in (relu_mul, res):
        if extra is not None:
            in_specs.append(pl.BlockSpec((tm, tk), lambda i, kk, j: (i, kk)))
            args.append(extra)
    if after is not None:
        in_specs.append(pl.BlockSpec(memory_space=pl.ANY))
        args.append(after)
    scratch = (pltpu.VMEM((tm, tk), F32),) if nn > 1 else ()
    return _pcall(body, name=name, out_shape=_sds((M, K), out_dtype), grid=(M // tm, K // tk, nn),
                  in_specs=in_specs, out_specs=pl.BlockSpec((tm, tk), lambda i, kk, j: (i, kk)),
                  scratch=scratch)(*args)


def mm_tn(a, c, *, nb, name, out_dtype, K=None, N=None, a_col0=0, c_col0=0):
    M = a.shape[0]
    K = K or a.shape[1]
    N = N or c.shape[1]
    Ns = N // nb
    tm = _pick(M, (2048, 1024, 512, 256))
    tk = _pick(K, (1024, 512, 256))
    tn = _pick(Ns, (1024, 512, 256, 128))
    assert a_col0 % tk == 0 and c_col0 % tn == 0
    nm, nbs, a0, c0 = M // tm, Ns // tn, a_col0 // tk, c_col0 // tn

    def body(*refs):
        a_ref, c_ref, o_ref = refs[:3]
        acc_ref = refs[3] if nm > 1 else None
        part = lax.dot_general(_mx(a_ref[...]), _mx(c_ref[...]), (((0,), (0,)), ((), ())),
                               preferred_element_type=F32)
        if nm == 1:
            o_ref[...] = part.astype(o_ref.dtype)
        else:
            m = pl.program_id(2)

            @pl.when(m == 0)
            def _():
                acc_ref[...] = part

            @pl.when(m > 0)
            def _():
                acc_ref[...] += part

            @pl.when(m == nm - 1)
            def _():
                o_ref[...] = acc_ref[...].astype(o_ref.dtype)

    scratch = (pltpu.VMEM((tk, tn), F32),) if nm > 1 else ()
    return _pcall(
        body, name=name, out_shape=_sds((nb, K, Ns), out_dtype), grid=(N // tn, K // tk, nm),
        in_specs=[pl.BlockSpec((tm, tk), lambda j, kk, m: (m, a0 + kk)),
                  pl.BlockSpec((tm, tn), lambda j, kk, m: (m, c0 + j))],
        out_specs=pl.BlockSpec((None, tk, tn), lambda j, kk, m: (j // nbs, kk, j % nbs)),
        scratch=scratch)(a, c)


def rms_fwd(x, g, *, name, col0=0, width=None, out_dtype=BF16):
    R = x.shape[0]
    width = width or x.shape[1]
    assert col0 % width == 0
    cb = col0 // width
    tr = _pick(R, (512, 256, 128))

    def body(x_ref, g_ref, o_ref):
        xf = x_ref[...].astype(F32)
        r = lax.rsqrt(jnp.mean(xf * xf, axis=1, keepdims=True) + EPS)
        o_ref[...] = (xf * r * g_ref[...]).astype(o_ref.dtype)

    return _pcall(body, name=name, out_shape=_sds((R, width), out_dtype), grid=(R // tr,),
                  in_specs=[pl.BlockSpec((tr, width), lambda i: (i, cb)),
                            pl.BlockSpec((1, width), lambda i: (0, 0))],
                  out_specs=pl.BlockSpec((tr, width), lambda i: (i, 0)))(x, g)


def rms_bwd(x, g, dy, *, name, col0=0, width=None, res=None, dx_dtype=F32, need_dx=True):
    R = x.shape[0]
    width = width or x.shape[1]
    cb = col0 // width
    tr = _pick(R, (512, 256, 128))

    def body(*refs):
        x_ref, g_ref, dy_ref = refs[:3]
        pos = 3
        res_ref = None
        if res is not None:
            res_ref = refs[pos]
            pos += 1
        dx_ref = None
        if need_dx:
            dx_ref = refs[pos]
            pos += 1
        dg_ref = refs[pos]
        xf = x_ref[...].astype(F32)
        dyf = dy_ref[...].astype(F32)
        r = lax.rsqrt(jnp.mean(xf * xf, axis=1, keepdims=True) + EPS)
        xh = xf * r
        if need_dx:
            gy = dyf * g_ref[...]
            dx = r * (gy - xh * jnp.mean(gy * xh, axis=1, keepdims=True))
            if res_ref is not None:
                dx = dx + res_ref[...]
            dx_ref[...] = dx.astype(dx_ref.dtype)
        part = jnp.sum(dyf * xh, axis=0, keepdims=True)

        @pl.when(pl.program_id(0) == 0)
        def _():
            dg_ref[...] = part

        @pl.when(pl.program_id(0) > 0)
        def _():
            dg_ref[...] += part

    in_specs = [pl.BlockSpec((tr, width), lambda i: (i, cb)),
                pl.BlockSpec((1, width), lambda i: (0, 0)),
                pl.BlockSpec((tr, width), lambda i: (i, 0))]
    args = [x, g, dy]
    if res is not None:
        in_specs.append(pl.BlockSpec((tr, width), lambda i: (i, 0)))
        args.append(res)
    dg_shape, dg_spec = _sds((1, width), F32), pl.BlockSpec((1, width), lambda i: (0, 0))
    if need_dx:
        out_shape = (_sds((R, width), dx_dtype), dg_shape)
        out_specs = (pl.BlockSpec((tr, width), lambda i: (i, 0)), dg_spec)
    else:
        out_shape, out_specs = dg_shape, dg_spec
    out = _pcall(body, name=name, out_shape=out_shape, grid=(R // tr,), in_specs=in_specs,
                 out_specs=out_specs)(*args)
    return out if need_dx else (None, out)


HEAD_ROW_BLOCKS = (2048, 1024, 512, 256, 128)


def _rope_apply(y, c, sa, sb):
    return y * c + pltpu.roll(y, 96, 1) * sa + pltpu.roll(y, 32, 1) * sb


def _rope_transpose(dy, c, sa, sb):
    return dy * c + pltpu.roll(dy * sa, 32, 1) + pltpu.roll(dy * sb, 96, 1)


def rope_tables(seq):
    pos = jnp.arange(seq, dtype=F32)
    inv = ROPE_THETA ** (-jnp.arange(0, MLA_ROPE, 2, dtype=F32) / MLA_ROPE)
    ang = pos[:, None] * inv[None, :]
    cos, sin = jnp.cos(ang), jnp.sin(ang)
    z32, z64 = jnp.zeros_like(cos), jnp.zeros((seq, 64), F32)
    c = jnp.concatenate([cos, cos, z64], axis=1)
    sa = jnp.concatenate([-sin, z32, z64], axis=1)
    sb = jnp.concatenate([z32, sin, z64], axis=1)
    return c, sa, sb


def _head_vec(part_refs):
    xs = [p[...].astype(F32) for p in part_refs]
    return xs[0] if len(xs) == 1 else jnp.concatenate(xs, axis=1)


def prep_fwd(parts, g, *, name, n_heads, n_real, rope=None):
    rows = parts[0][0].shape[0]
    dh = sum(w for _, w, _ in parts)
    tr = _pick(rows, HEAD_ROW_BLOCKS)
    npart = len(parts)

    def body(*refs):
        part_refs, g_ref = refs[:npart], refs[npart]
        pos = npart + 1
        if rope is not None:
            c_ref, sa_ref, sb_ref = refs[pos:pos + 3]
            pos += 3
        o_ref = refs[pos]
        x = _head_vec(part_refs)
        r = lax.rsqrt(jnp.sum(x * x, axis=1, keepdims=True) * (1.0 / n_real) + EPS)
        y = x * r * g_ref[...]
        if rope is not None:
            yr = _rope_apply(y[:, dh - 128:], c_ref[...], sa_ref[...], sb_ref[...])
            y = jnp.concatenate([y[:, :dh - 128], yr], axis=1)
        o_ref[...] = y.astype(o_ref.dtype)

    in_specs, args = [], []
    for arr, w, fn in parts:
        in_specs.append(pl.BlockSpec((tr, w), functools.partial(lambda h, i, fn: (i, fn(h)), fn=fn)))
        args.append(arr)
    in_specs.append(pl.BlockSpec((1, dh), lambda h, i: (0, 0)))
    args.append(g)
    if rope is not None:
        for t in rope:
            in_specs.append(pl.BlockSpec((tr, 128), lambda h, i: (i, 0)))
            args.append(t)
    return _pcall(body, name=name, out_shape=_sds((n_heads, rows, dh), BF16), grid=(n_heads, rows // tr),
                  in_specs=in_specs, out_specs=pl.BlockSpec((None, tr, dh), lambda h, i: (h, i, 0)))(*args)


def _norm_bwd(x, g, dyn, n_real):
    r = lax.rsqrt(jnp.sum(x * x, axis=1, keepdims=True) * (1.0 / n_real) + EPS)
    xh = x * r
    gy = dyn * g
    dx = r * (gy - xh * (jnp.sum(gy * xh, axis=1, keepdims=True) * (1.0 / n_real)))
    return dx, jnp.sum(dyn * xh, axis=0, keepdims=True)


def prep_bwd_q(src, dy, g, *, name, n_heads, dh, n_real, rope=None, out_dtype=BF16):
    rows = src.shape[0]
    tr = _pick(rows, HEAD_ROW_BLOCKS)

    def body(*refs):
        x_ref, dy_ref, g_ref = refs[:3]
        pos = 3
        if rope is not None:
            c_ref, sa_ref, sb_ref = refs[pos:pos + 3]
            pos += 3
        dx_ref, dg_ref = refs[pos], refs[pos + 1]
        dyn = dy_ref[...].astype(F32)
        if rope is not None:
            dr = _rope_transpose(dyn[:, dh - 128:], c_ref[...], sa_ref[...], sb_ref[...])
            dyn = jnp.concatenate([dyn[:, :dh - 128], dr], axis=1)
        dx, dg = _norm_bwd(x_ref[...].astype(F32), g_ref[...], dyn, n_real)
        dx_ref[...] = dx.astype(dx_ref.dtype)
        first = jnp.logical_and(pl.program_id(0) == 0, pl.program_id(1) == 0)

        @pl.when(first)
        def _():
            dg_ref[...] = dg

        @pl.when(jnp.logical_not(first))
        def _():
            dg_ref[...] += dg

    in_specs = [pl.BlockSpec((tr, dh), lambda i, h: (i, h)),
                pl.BlockSpec((None, tr, dh), lambda i, h: (h, i, 0)),
                pl.BlockSpec((1, dh), lambda i, h: (0, 0))]
    args = [src, dy, g]
    if rope is not None:
        for t in rope:
            in_specs.append(pl.BlockSpec((tr, 128), lambda i, h: (i, 0)))
            args.append(t)
    return _pcall(body, name=name, out_shape=(_sds((rows, n_heads * dh), out_dtype), _sds((1, dh), F32)),
                  grid=(rows // tr, n_heads), in_specs=in_specs,
                  out_specs=(pl.BlockSpec((tr, dh), lambda i, h: (i, h)),
                             pl.BlockSpec((1, dh), lambda i, h: (0, 0))))(*args)


def prep_bwd_mla_k(kv_raw, zs, dkf, dv, g, rope, *, name):
    rows = kv_raw.shape[0]
    tr = _pick(rows, HEAD_ROW_BLOCKS)

    def body(kn_ref, kr_ref, dy_ref, dv_ref, g_ref, c_ref, sa_ref, sb_ref, dkv_ref, dkr_ref, dg_ref):
        h = pl.program_id(1)
        x = jnp.concatenate([kn_ref[...].astype(F32), kr_ref[...].astype(F32)], axis=1)
        dyn = dy_ref[...].astype(F32)
        dr = _rope_transpose(dyn[:, 128:], c_ref[...], sa_ref[...], sb_ref[...])
        dyn = jnp.concatenate([dyn[:, :128], dr], axis=1)
        dx, dg = _norm_bwd(x, g_ref[...], dyn, MLA_QK)
        dkv_ref[...] = jnp.concatenate([dx[:, :128], dv_ref[...].astype(F32)], axis=1).astype(dkv_ref.dtype)

        @pl.when(h == 0)
        def _():
            dkr_ref[...] = dx[:, 128:]

        @pl.when(h > 0)
        def _():
            dkr_ref[...] += dx[:, 128:]

        first = jnp.logical_and(pl.program_id(0) == 0, h == 0)

        @pl.when(first)
        def _():
            dg_ref[...] = dg

        @pl.when(jnp.logical_not(first))
        def _():
            dg_ref[...] += dg

    tab = pl.BlockSpec((tr, 128), lambda i, h: (i, 0))
    return _pcall(
        body, name=name,
        out_shape=(_sds((rows, N_HEADS * 256), BF16), _sds((rows, 128), F32), _sds((1, MLA_PAD), F32)),
        grid=(rows // tr, N_HEADS),
        in_specs=[pl.BlockSpec((tr, 128), lambda i, h: (i, 2 * h)),
                  pl.BlockSpec((tr, 128), lambda i, h: (i, Z_KR // 128)),
                  pl.BlockSpec((None, tr, MLA_PAD), lambda i, h: (h, i, 0)),
                  pl.BlockSpec((None, tr, 128), lambda i, h: (h, i, 0)),
                  pl.BlockSpec((1, MLA_PAD), lambda i, h: (0, 0)), tab, tab, tab],
        out_specs=(pl.BlockSpec((tr, 256), lambda i, h: (i, h)),
                   pl.BlockSpec((tr, 128), lambda i, h: (i, 0)),
                   pl.BlockSpec((1, MLA_PAD), lambda i, h: (0, 0))))(kv_raw, zs, dkf, dv, g, *rope)


def prep_bwd_groups(src, base_blk, dys, gs, *, name, n_heads, kinds, out_dtype=BF16):
    rows = src.shape[0]
    ng = len(kinds)
    J = ng * n_heads
    tr = _pick(rows, HEAD_ROW_BLOCKS)
    gstack = jnp.stack([gs[k] if kinds[k] == "norm" else jnp.ones((1, DH), F32) for k in range(ng)])

    def body(*refs):
        x_ref = refs[0]
        dy_refs = refs[1:1 + ng]
        g_ref, dx_ref, dg_ref = refs[1 + ng:4 + ng]
        j, i = pl.program_id(0), pl.program_id(1)
        grp = j // n_heads
        dy = dy_refs[0][...].astype(F32)
        for k in range(1, ng):
            dy = jnp.where(grp == k, dy_refs[k][...].astype(F32), dy)
        dx, dg = _norm_bwd(x_ref[...].astype(F32), g_ref[...], dy, DH)
        is_copy = functools.reduce(jnp.logical_or, [grp == k for k in range(ng) if kinds[k] == "copy"],
                                   jnp.bool_(False))
        dx_ref[...] = jnp.where(is_copy, dy, dx).astype(dx_ref.dtype)
        dg = jnp.where(is_copy, jnp.zeros_like(dg), dg)
        first = jnp.logical_and(j % n_heads == 0, i == 0)

        @pl.when(first)
        def _():
            dg_ref[...] = dg

        @pl.when(jnp.logical_not(first))
        def _():
            dg_ref[...] += dg

    in_specs = [pl.BlockSpec((tr, DH), lambda j, i: (i, base_blk + j))]
    for k in range(ng):
        in_specs.append(pl.BlockSpec(
            (None, tr, DH),
            functools.partial(lambda j, i, k: (jnp.clip(j - k * n_heads, 0, n_heads - 1),
                                               jnp.where(j // n_heads == k, i, 0), 0), k=k)))
    in_specs.append(pl.BlockSpec((None, 1, DH), lambda j, i: (j // n_heads, 0, 0)))
    return _pcall(body, name=name, out_shape=(_sds((rows, J * DH), out_dtype), _sds((ng, 1, DH), F32)),
                  grid=(J, rows // tr), in_specs=in_specs,
                  out_specs=(pl.BlockSpec((tr, DH), lambda j, i: (i, j)),
                             pl.BlockSpec((None, 1, DH), lambda j, i: (j // n_heads, 0, 0))))(src, *dys, gstack)


def _attn_cfg(mode, sq, sk):
    if mode == "chunk":
        tq = 128
        win = min((LEFT_CHUNKS + 2) * CHUNK, sk)
    else:
        tq = _pick(sq, (256, 128))
        win = sk
    scale = (MLA_QK if mode == "mla" else DH) ** -0.5
    return tq, win, scale


def _attn_key_rows(mode, i, tq, win, sk, run):
    if mode == "chunk":
        start = pl.multiple_of(jnp.clip((i - LEFT_CHUNKS // 2) * 128, 0, sk - win), 128)
        run(pl.ds(start, win), start)
    elif mode == "cross":
        run(slice(0, sk), 0)
    else:
        lax.switch(i, [functools.partial(run, slice(0, (b + 1) * tq), 0) for b in range(sk // tq)])


def _attn_scores(mode, i, tq, scale, q, kk, start, cq, ck, t_ref):
    nk = kk.shape[0]
    s = lax.dot_general(q, kk, (((1,), (1,)), ((), ())), preferred_element_type=F32) * scale
    if mode == "cross":
        return s
    t_pos = i * tq + lax.broadcasted_iota(jnp.int32, (tq, nk), 0)
    s_pos = start + lax.broadcasted_iota(jnp.int32, (tq, nk), 1)
    if mode == "fox":
        s = s + cq - ck
        allowed = s_pos <= t_pos
    else:
        qc, kc = lax.shift_right_logical(t_pos, CHUNK_SHIFT), lax.shift_right_logical(s_pos, CHUNK_SHIFT)
        allowed = kc <= qc
        if mode == "chunk":
            allowed = jnp.logical_and(allowed, kc >= qc - LEFT_CHUNKS)
            tiles = []
            for w in range(nk // 128):
                delta = i - (start // 128 + w)
                tiles.append(jnp.where(delta == 0, t_ref[0], jnp.where(delta == 1, t_ref[1], t_ref[2])))
            s = s + jnp.concatenate(tiles, axis=1)
    return jnp.where(allowed, s, NEG)


def attn_fwd(q, k, v_arr, v_blk, *, mode, name, n_heads, cq=None, ck=None, tiles=None):
    _, sq, dk = q.shape
    sk = k.shape[1]
    tq, win, scale = _attn_cfg(mode, sq, sk)

    def body(*refs):
        q_ref, k_ref, v_ref = refs[:3]
        pos = 3
        cq_ref = ck_ref = t_ref = None
        if mode == "fox":
            cq_ref, ck_ref = refs[pos:pos + 2]
            pos += 2
        if mode == "chunk":
            t_ref = refs[pos]
            pos += 1
        o_ref, lse_ref = refs[pos], refs[pos + 1]
        i = pl.program_id(1)

        def run(rows, start):
            cq, ck = (cq_ref[...], ck_ref[:, rows]) if mode == "fox" else (None, None)
            s = _attn_scores(mode, i, tq, scale, q_ref[...], k_ref[rows, :], start, cq, ck, t_ref)
            m = jnp.max(s, axis=1, keepdims=True)
            e = jnp.exp(s - m)
            l = jnp.sum(e, axis=1, keepdims=True)
            p = e * (1.0 / l)
            o_ref[...] = jnp.dot(_mx(p), _mx(v_ref[rows, :]), preferred_element_type=F32).astype(o_ref.dtype)
            lse_ref[...] = m + jnp.log(l)

        _attn_key_rows(mode, i, tq, win, sk, run)

    in_specs = [pl.BlockSpec((None, tq, dk), lambda h, i: (h, i, 0)),
                pl.BlockSpec((None, sk, dk), lambda h, i: (h, 0, 0)),
                pl.BlockSpec((sk, DH), lambda h, i: (0, v_blk(h)))]
    args = [q, k, v_arr]
    if mode == "fox":
        in_specs += [pl.BlockSpec((None, tq, 1), lambda h, i: (h, i, 0)),
                     pl.BlockSpec((None, 1, sk), lambda h, i: (h, 0, 0))]
        args += [cq, ck]
    if mode == "chunk":
        in_specs.append(pl.BlockSpec((3, None, 128, 128), lambda h, i: (0, h, 0, 0)))
        args.append(tiles)
    return _pcall(body, name=name,
                  out_shape=(_sds((sq, n_heads * DH), BF16), _sds((n_heads, sq, 1), F32)),
                  grid=(n_heads, sq // tq), in_specs=in_specs,
                  out_specs=(pl.BlockSpec((tq, DH), lambda h, i: (i, h)),
                             pl.BlockSpec((None, tq, 1), lambda h, i: (h, i, 0))))(*args)


def attn_bwd(q, k, v_arr, v_blk, o, do, lse, *, mode, name, n_heads, cq=None, ck=None, tiles=None):
    _, sq, dk = q.shape
    sk = k.shape[1]
    tq, win, scale = _attn_cfg(mode, sq, sk)
    n_extra = {"fox": 2, "chunk": 1}.get(mode, 0)

    def body(*refs):
        q_ref, k_ref, v_ref, o_ref, do_ref, lse_ref = refs[:6]
        pos = 6
        cq_ref = ck_ref = t_ref = None
        if mode == "fox":
            cq_ref, ck_ref = refs[pos:pos + 2]
            pos += 2
        if mode == "chunk":
            t_ref = refs[pos]
            pos += 1
        dq_ref, dk_ref, dv_ref = refs[pos:pos + 3]
        extra = refs[pos + 3:pos + 3 + n_extra]
        i = pl.program_id(1)

        @pl.when(i == 0)
        def _():
            dk_ref[...] = jnp.zeros_like(dk_ref)
            dv_ref[...] = jnp.zeros_like(dv_ref)
            if mode == "fox":
                extra[1][...] = jnp.zeros_like(extra[1])
            if mode == "chunk":
                extra[0][...] = jnp.zeros_like(extra[0])

        def run(rows, start):
            q = q_ref[...]
            do = do_ref[...]
            kk = k_ref[rows, :]
            cq, ck = (cq_ref[...], ck_ref[:, rows]) if mode == "fox" else (None, None)
            s = _attn_scores(mode, i, tq, scale, q, kk, start, cq, ck, t_ref)
            p = jnp.exp(s - lse_ref[...])
            drow = jnp.sum(do.astype(F32) * o_ref[...].astype(F32), axis=1, keepdims=True)
            dp = lax.dot_general(do, _mx(v_ref[rows, :]), (((1,), (1,)), ((), ())), preferred_element_type=F32)
            ds = p * (dp - drow)
            dsb = _mx(ds)
            dq_ref[...] = (jnp.dot(dsb, kk, preferred_element_type=F32) * scale).astype(dq_ref.dtype)
            dk_ref[rows, :] += lax.dot_general(dsb, q, (((0,), (0,)), ((), ())),
                                               preferred_element_type=F32) * scale
            dv_ref[rows, :] += lax.dot_general(_mx(p), do, (((0,), (0,)), ((), ())), preferred_element_type=F32)
            if mode == "chunk":
                dt_ref = extra[0]
                for w in range(win // 128):
                    delta = i - (start // 128 + w)
                    tile = ds[:, w * 128:(w + 1) * 128]
                    zero = jnp.zeros_like(tile)
                    dt_ref[0] += jnp.where(delta == 0, tile, zero)
                    dt_ref[1] += jnp.where(delta == 1, tile, zero)
                    dt_ref[2] += jnp.where(delta >= 2, tile, zero)
            if mode == "fox":
                extra[0][...] = jnp.sum(ds, axis=1, keepdims=True)
                extra[1][:, rows] -= jnp.sum(ds, axis=0, keepdims=True)

        _attn_key_rows(mode, i, tq, win, sk, run)

    in_specs = [pl.BlockSpec((None, tq, dk), lambda h, i: (h, i, 0)),
                pl.BlockSpec((None, sk, dk), lambda h, i: (h, 0, 0)),
                pl.BlockSpec((sk, DH), lambda h, i: (0, v_blk(h))),
                pl.BlockSpec((tq, DH), lambda h, i: (i, h)),
                pl.BlockSpec((tq, DH), lambda h, i: (i, h)),
                pl.BlockSpec((None, tq, 1), lambda h, i: (h, i, 0))]
    args = [q, k, v_arr, o, do, lse]
    out_shape = [_sds((n_heads, sq, dk), F32), _sds((n_heads, sk, dk), F32), _sds((n_heads, sk, DH), F32)]
    out_specs = [pl.BlockSpec((None, tq, dk), lambda h, i: (h, i, 0)),
                 pl.BlockSpec((None, sk, dk), lambda h, i: (h, 0, 0)),
                 pl.BlockSpec((None, sk, DH), lambda h, i: (h, 0, 0))]
    if mode == "fox":
        in_specs += [pl.BlockSpec((None, tq, 1), lambda h, i: (h, i, 0)),
                     pl.BlockSpec((None, 1, sk), lambda h, i: (h, 0, 0))]
        args += [cq, ck]
        out_shape += [_sds((n_heads, sq, 1), F32), _sds((n_heads, 1, sk), F32)]
        out_specs += [pl.BlockSpec((None, tq, 1), lambda h, i: (h, i, 0)),
                      pl.BlockSpec((None, 1, sk), lambda h, i: (h, 0, 0))]
    if mode == "chunk":
        in_specs.append(pl.BlockSpec((3, None, 128, 128), lambda h, i: (0, h, 0, 0)))
        args.append(tiles)
        out_shape.append(_sds((3, n_heads, 128, 128), F32))
        out_specs.append(pl.BlockSpec((3, None, 128, 128), lambda h, i: (0, h, 0, 0)))
    return _pcall(body, name=name, out_shape=tuple(out_shape), grid=(n_heads, sq // tq),
                  in_specs=in_specs, out_specs=tuple(out_specs))(*args)


REL_LANES = 384
REL_KBLK = 2048


def _rel_onehot(t, k):
    rho = k * REL_KBLK + lax.broadcasted_iota(jnp.int32, (REL_KBLK, REL_LANES), 0)
    lane = lax.broadcasted_iota(jnp.int32, (REL_KBLK, REL_LANES), 1)
    diff = lax.shift_right_logical(rho, 7) - jnp.bitwise_and(rho, 127)
    idx = jnp.where(t == 0, diff + REL_CLIP,
                    jnp.where(t == 1, jnp.minimum(diff + 128, REL_CLIP) + REL_CLIP, N_REL - 1))
    return jnp.where(idx == lane, 1.0, 0.0).astype(F32)


def _split3(x):
    hi = x.astype(BF16)
    rest = x - hi.astype(F32)
    mid = rest.astype(BF16)
    return hi, mid, (rest - mid.astype(F32)).astype(BF16)


def relbias_onehot():
    def body(o_ref):
        o_ref[...] = _rel_onehot(pl.program_id(0), pl.program_id(1)).astype(o_ref.dtype)

    return _pcall(body, name="relbias_onehot", out_shape=_sds((3, 128 * 128, REL_LANES), BF16),
                  grid=(3, 128 * 128 // REL_KBLK), in_specs=[],
                  out_specs=pl.BlockSpec((None, REL_KBLK, REL_LANES), lambda t, k: (t, k, 0)))()


ONEHOT_SPEC = pl.BlockSpec((None, REL_KBLK, REL_LANES), lambda t, k: (t, k, 0))


def relbias_tiles(rel_bias, onehot):
    nh = rel_bias.shape[0]
    rb = jnp.pad(rel_bias, ((0, 0), (0, REL_LANES - N_REL)))

    def body(rb_ref, e_ref, o_ref):
        e = e_ref[...]
        hi, mid, lo = [lax.dot_general(t, e, (((1,), (1,)), ((), ())), preferred_element_type=F32)
                       for t in _split3(rb_ref[...])]
        o_ref[...] = (hi + mid) + lo

    flat = _pcall(body, name="relbias_tiles", out_shape=_sds((3, nh, 128 * 128), F32),
                  grid=(3, 128 * 128 // REL_KBLK),
                  in_specs=[pl.BlockSpec((nh, REL_LANES), lambda t, k: (0, 0)), ONEHOT_SPEC],
                  out_specs=pl.BlockSpec((None, nh, REL_KBLK), lambda t, k: (t, 0, k)))(rb, onehot)
    return flat.reshape(3, nh, 128, 128)


def relbias_tiles_bwd(dtiles, onehot):
    nh = dtiles.shape[1]

    def body(dt_ref, e_ref, o_ref):
        t, k = pl.program_id(0), pl.program_id(1)
        e = e_ref[...]
        hi, mid, lo = [jnp.dot(t_, e, preferred_element_type=F32) for t_ in _split3(dt_ref[...])]
        part = (hi + mid) + lo
        first = jnp.logical_and(t == 0, k == 0)

        @pl.when(first)
        def _():
            o_ref[...] = part

        @pl.when(jnp.logical_not(first))
        def _():
            o_ref[...] += part

    out = _pcall(body, name="relbias_tiles_bwd", out_shape=_sds((nh, REL_LANES), F32),
                 grid=(3, 128 * 128 // REL_KBLK),
                 in_specs=[pl.BlockSpec((None, nh, REL_KBLK), lambda t, k: (t, 0, k)), ONEHOT_SPEC],
                 out_specs=pl.BlockSpec((nh, REL_LANES), lambda t, k: (0, 0)))(
                     dtiles.reshape(3, nh, 128 * 128), onehot)
    return out[:, :N_REL]


CUM_BLK = 256


def _tri(n, lower):
    r = lax.broadcasted_iota(jnp.int32, (n, n), 0)
    c = lax.broadcasted_iota(jnp.int32, (n, n), 1)
    return jnp.where(r >= c if lower else r <= c, 1.0, 0.0).astype(F32)


def fox_cum_fwd(zs, bf):
    S = zs.shape[0]
    tb = min(CUM_BLK, S)

    def body(f_ref, b_ref, cum_ref, cumt_ref, carry_ref):
        @pl.when(pl.program_id(0) == 0)
        def _():
            carry_ref[...] = jnp.zeros_like(carry_ref)

        x = f_ref[...] + b_ref[...]
        lane = lax.broadcasted_iota(jnp.int32, x.shape, 1)
        logf = jnp.where(lane < N_HEADS, jnp.minimum(x, 0.0) - jnp.log(1.0 + jnp.exp(-jnp.abs(x))), 0.0)
        cum = jnp.dot(_tri(tb, True), logf, preferred_element_type=F32,
                      precision=lax.Precision.HIGHEST) + carry_ref[...]
        carry_ref[...] = cum[tb - 1:tb, :]
        cum_ref[...] = cum
        cumt_ref[...] = cum.T

    return _pcall(body, name="fox_cum_fwd", out_shape=(_sds((S, 128), F32), _sds((128, S), F32)),
                  grid=(S // tb,),
                  in_specs=[pl.BlockSpec((tb, 128), lambda i: (i, Z_FF // 128)),
                            pl.BlockSpec((1, 128), lambda i: (0, 0))],
                  out_specs=(pl.BlockSpec((tb, 128), lambda i: (i, 0)),
                             pl.BlockSpec((128, tb), lambda i: (0, i))),
                  scratch=(pltpu.VMEM((1, 128), F32),))(zs, bf)


def fox_cum_bwd(zs, bf, dcum):
    S = zs.shape[0]
    tb = min(CUM_BLK, S)
    nblk = S // tb

    def body(f_ref, b_ref, d_ref, df_ref, db_ref, carry_ref):
        @pl.when(pl.program_id(0) == 0)
        def _():
            carry_ref[...] = jnp.zeros_like(carry_ref)
            db_ref[...] = jnp.zeros_like(db_ref)

        d = d_ref[...]
        dlogf = jnp.dot(_tri(tb, False), d, preferred_element_type=F32,
                        precision=lax.Precision.HIGHEST) + carry_ref[...]
        carry_ref[...] += jnp.sum(d, axis=0, keepdims=True)
        x = f_ref[...] + b_ref[...]
        lane = lax.broadcasted_iota(jnp.int32, x.shape, 1)
        dx = jnp.where(lane < N_HEADS, dlogf / (1.0 + jnp.exp(x)), 0.0)
        df_ref[...] = dx
        db_ref[...] += jnp.sum(dx, axis=0, keepdims=True)

    return _pcall(body, name="fox_cum_bwd", out_shape=(_sds((S, 128), F32), _sds((1, 128), F32)),
                  grid=(nblk,),
                  in_specs=[pl.BlockSpec((tb, 128), lambda i: (nblk - 1 - i, Z_FF // 128)),
                            pl.BlockSpec((1, 128), lambda i: (0, 0)),
                            pl.BlockSpec((tb, 128), lambda i: (nblk - 1 - i, 0))],
                  out_specs=(pl.BlockSpec((tb, 128), lambda i: (nblk - 1 - i, 0)),
                             pl.BlockSpec((1, 128), lambda i: (0, 0))),
                  scratch=(pltpu.VMEM((1, 128), F32),))(zs, bf, dcum)


def _sigmoid(x):
    return 1.0 / (1.0 + jnp.exp(-x))


def merge_fwd(z, projs):
    S = z.shape[0]
    tr, tc = _pick(S, (512, 256, 128)), 512
    nbc = D_MODEL // tc
    g0 = Z_GATE // tc

    def body(g0_ref, g1_ref, g2_ref, p0_ref, p1_ref, p2_ref, o_ref):
        acc = _sigmoid(g0_ref[...]) * p0_ref[...]
        acc += _sigmoid(g1_ref[...]) * p1_ref[...]
        acc += _sigmoid(g2_ref[...]) * p2_ref[...]
        o_ref[...] = acc.astype(o_ref.dtype)

    gspecs = [pl.BlockSpec((tr, tc), functools.partial(lambda i, j, n: (i, g0 + n * nbc + j), n=n))
              for n in range(3)]
    pspec = pl.BlockSpec((tr, tc), lambda i, j: (i, j))
    return _pcall(body, name="merge_fwd", out_shape=_sds((S, D_MODEL), BF16), grid=(S // tr, nbc),
                  in_specs=gspecs + [pspec] * 3, out_specs=pspec)(z, z, z, *projs)


def merge_bwd(z, projs, dmerged):
    S = z.shape[0]
    tr, tc = _pick(S, (512, 256, 128)), 512
    nbc = D_MODEL // tc
    g0 = Z_GATE // tc

    def body(g0_ref, g1_ref, g2_ref, p0_ref, p1_ref, p2_ref, dm_ref, dg0, dg1, dg2, dp0, dp1, dp2):
        dm = dm_ref[...]
        for g_ref, p_ref, dg_ref, dp_ref in ((g0_ref, p0_ref, dg0, dp0), (g1_ref, p1_ref, dg1, dp1),
                                             (g2_ref, p2_ref, dg2, dp2)):
            sg = _sigmoid(g_ref[...])
            dp_ref[...] = (dm * sg).astype(dp_ref.dtype)
            dg_ref[...] = (dm * p_ref[...] * sg * (1.0 - sg)).astype(dg_ref.dtype)

    gspecs = [pl.BlockSpec((tr, tc), functools.partial(lambda i, j, n: (i, g0 + n * nbc + j), n=n))
              for n in range(3)]
    pspec = pl.BlockSpec((tr, tc), lambda i, j: (i, j))
    out = _pcall(body, name="merge_bwd", out_shape=tuple(_sds((S, D_MODEL), BF16) for _ in range(6)),
                 grid=(S // tr, nbc), in_specs=gspecs + [pspec] * 4,
                 out_specs=tuple([pspec] * 6))(z, z, z, *projs, dmerged)
    return out[:3], out[3:]


def loss_head(y, target):
    S, D = y.shape
    tr = _pick(S, (256, 128))

    def body(y_ref, t_ref, dy_ref, l_ref):
        e = y_ref[...] - t_ref[...]
        dy_ref[...] = e * (1.0 / D)
        part = jnp.sum(jnp.sum(e * e, axis=1, keepdims=True), axis=0, keepdims=True) * (0.5 / D)

        @pl.when(pl.program_id(0) == 0)
        def _():
            l_ref[...] = part

        @pl.when(pl.program_id(0) > 0)
        def _():
            l_ref[...] += part

    spec = pl.BlockSpec((tr, D), lambda i: (i, 0))
    return _pcall(body, name="loss_head", out_shape=(_sds((S, D), F32), _sds((1, 1), F32)), grid=(S // tr,),
                  in_specs=[spec, spec], out_specs=(spec, pl.BlockSpec((1, 1), lambda i: (0, 0))))(y, target)


def _adamw_update(w, g, m, v):
    nm = ADAM_B1 * m + (1.0 - ADAM_B1) * g
    nv = ADAM_B2 * v + (1.0 - ADAM_B2) * (g * g)
    delta = -ADAM_LR * ((nm / (1.0 - ADAM_B1 ** ADAM_STEP)) / (jnp.sqrt(nv / (1.0 - ADAM_B2 ** ADAM_STEP)) + ADAM_EPS)
                        + ADAM_WD * w)
    return delta, nm, nv


ADAMW_BLOCK_BYTES = 1024 * 1024


def adamw(w, gs, m, v, *, name):
    L, R, C = w.shape
    tr = _pick(R, (512, 256, 128, 64, 32, 16, 8))
    while tr * C * 4 > ADAMW_BLOCK_BYTES and tr % 16 == 0:
        tr //= 2

    def body(*refs):
        w_ref, m_ref, v_ref = refs[:3]
        g_refs = refs[3:3 + L]
        go_ref, d_ref, nm_ref, nv_ref = refs[3 + L:]
        l = pl.program_id(0)
        g_ = g_refs[0][...]
        for k in range(1, L):
            g_ = jnp.where(l == k, g_refs[k][...], g_)
        go_ref[...] = g_
        d_ref[...], nm_ref[...], nv_ref[...] = _adamw_update(w_ref[...], g_, m_ref[...], v_ref[...])

    spec = pl.BlockSpec((None, tr, C), lambda l, i: (l, i, 0))
    gspecs = [pl.BlockSpec((tr, C), functools.partial(lambda l, i, k: (jnp.where(l == k, i, 0), 0), k=k))
              for k in range(L)]
    return _pcall(body, name=name, out_shape=tuple(_sds((L, R, C), F32) for _ in range(4)),
                  grid=(L, R // tr), in_specs=[spec] * 3 + gspecs, out_specs=(spec,) * 4)(w, m, v, *gs)


SHARD_SHAPES = {
    "w_uq": (Q_LORA, 384), "w_ukv": (KV_LORA, 512), "w_br": (3, MIX_W, 512), "w_out": (512, D_MODEL),
    "w_xq": (512, 512), "w_xkv": (512, 1024), "w_xo": (512, 512), "w_1": (D_MODEL, 2048),
    "w_2": (2048, D_MODEL), "w_in": (D_MODEL, W_IN_SHARD),
}


def _rows2d(a, name):
    shp = SHARD_SHAPES[name]
    return a.reshape(a.shape[:a.ndim - len(shp)] + (-1, shp[-1]))


def _cols_from_shards(g):
    return jnp.transpose(g, (1, 0, 2)).reshape(g.shape[1], 4 * g.shape[2])


def _cols_to_shards(w):
    return jnp.transpose(w.reshape(w.shape[0], 4, w.shape[1] // 4), (1, 0, 2))


EARLY_WEIGHTS = ("w_in", "w_uq", "w_ukv")
MID_WEIGHTS = ("w_br", "w_out", "w_xq", "w_xkv", "w_xo")
LAST_WEIGHTS = ("w_1", "w_2")
LATE_WEIGHTS = MID_WEIGHTS + LAST_WEIGHTS
assert sorted(EARLY_WEIGHTS + LATE_WEIGHTS) == sorted(PACK_ORDER)


def _w_in_relaid(g_in):
    zeros = lambda n: [jnp.zeros((D_MODEL, n), g_in.dtype)] if n else []
    segs, at = [], 0
    for p0, o0, w in sorted(W_IN_PIECES):
        segs += zeros(p0 - at)
        while w > 0:
            s_, a = divmod(o0, W_IN_SHARD)
            take = min(w, W_IN_SHARD - a)
            segs.append(g_in[s_][:, a:a + take])
            o0, p0, w = o0 + take, p0 + take, w - take
        at = p0
    return jnp.concatenate(segs + zeros(Z_TOT - at), axis=1)


def full_weights(g):
    forms = {
        "w_in": lambda a: {"in_p": _w_in_relaid(a)[None]},
        "w_uq": lambda a: {"uq_p": jnp.pad(_cols_from_shards(a).reshape(Q_LORA, N_HEADS, MLA_QK),
                                           ((0, 0), (0, 0), (0, MLA_PAD - MLA_QK))
                                           ).reshape(1, Q_LORA, N_HEADS * MLA_PAD)},
        "w_ukv": lambda a: {"ukv": a},
        "w_br": lambda a: {"br": [a[:, n] for n in range(3)]},
        "w_out": lambda a: {"out": a.reshape(1, D_MODEL, D_MODEL)},
        "w_xq": lambda a: {"xq": a.reshape(1, D_MODEL, 512)},
        "w_xkv": lambda a: {"xkv": a.reshape(1, D_MODEL, 1024)},
        "w_xo": lambda a: {"xo": a},
        "w_1": lambda a: {"w1": a},
        "w_2": lambda a: {"w2": a.reshape(1, D_FF, D_MODEL)},
    }
    out = {}
    for n, a in g.items():
        out.update(forms[n](a))
    return out


def _w_in_grad_shards(dp):
    d_in = []
    for s_ in range(4):
        lo, hi, segs = s_ * W_IN_SHARD, (s_ + 1) * W_IN_SHARD, []
        for o0, p0, w in sorted((o0, p0, w) for p0, o0, w in W_IN_PIECES):
            a, b = max(lo, o0), min(hi, o0 + w)
            if a < b:
                segs.append(dp[:, p0 + a - o0:p0 + b - o0])
        d_in.append(jnp.concatenate(segs, axis=1))
    return jnp.stack(d_in)


def shard_grads(dw):
    forms = {
        "in_p": lambda a: {"w_in": _w_in_grad_shards(a[0])},
        "uq_p": lambda a: {"w_uq": _cols_to_shards(
            a.reshape(Q_LORA, N_HEADS, MLA_PAD)[:, :, :MLA_QK].reshape(Q_LORA, N_HEADS * MLA_QK))},
        "ukv": lambda a: {"w_ukv": a},
        "br": lambda a: {"w_br": jnp.stack(a, axis=1)},
        "out": lambda a: {"w_out": a.reshape(4, 512, D_MODEL)},
        "xq": lambda a: {"w_xq": a.reshape(4, 512, 512)},
        "xkv": lambda a: {"w_xkv": a.reshape(4, 512, 1024)},
        "xo": lambda a: {"w_xo": a},
        "w1": lambda a: {"w_1": a},
        "w2": lambda a: {"w_2": a.reshape(4, 2048, D_MODEL)},
    }
    out = {}
    for k, a in dw.items():
        out.update(forms[k](a))
    return out


def layer_params(d, l):
    row = lambda v: v.reshape(1, -1).astype(F32)
    padto = lambda v, n: jnp.pad(row(v), ((0, 0), (0, n - v.shape[-1])))
    return dict(
        g_mix=row(d["g_mix"][l]), g_cq=row(d["g_cq"][l]), g_ckv=row(d["g_ckv"][l]),
        g_mla_q=padto(d["g_mla_q"][l], MLA_PAD), g_mla_k=padto(d["g_mla_k"][l], MLA_PAD),
        b_f=padto(d["b_f"][l], 128), g_fox_q=row(d["g_fox_q"][l]), g_fox_k=row(d["g_fox_k"][l]),
        rel_bias=d["rel_bias"][l].astype(F32), g_ch_q=row(d["g_ch_q"][l]), g_ch_k=row(d["g_ch_k"][l]),
        g_cross=row(d["g_cross"][l]), g_mem=row(d["g_mem"][l]), g_x_q=row(d["g_x_q"][l]),
        g_x_k=row(d["g_x_k"][l]), g_mlp=row(d["g_mlp"][l]))


FOX_B = Z_FOX // 128
CH_B = Z_CH // 128


def _after(small, token):
    return small if token is None else small + token[:1, :1].reshape((1,) * small.ndim)


def layer_fwd(x, mem, W, P, rope, mid_weights, last_weights):
    S = x.shape[0]
    s = {}
    s["h"] = rms_fwd(x, P["g_mix"], name="rms_d")
    z = s["z"] = mm_nn(s["h"], W["in_p"], name="mm_in", out_dtype=F32)
    s["cq_n"] = rms_fwd(z, P["g_cq"], col0=Z_CQ, width=Q_LORA, name="rms_cq")
    s["ckv_n"] = rms_fwd(z, P["g_ckv"], col0=Z_CKV, width=KV_LORA, name="rms_ckv")
    s["q_raw"] = mm_nn(s["cq_n"], W["uq_p"], name="mm_uq", out_dtype=F32)
    s["kv_raw"] = mm_nn(s["ckv_n"], W["ukv"], name="mm_ukv", out_dtype=F32)
    s["qa"] = prep_fwd([(s["q_raw"], MLA_PAD, lambda h: h)], P["g_mla_q"], name="prep_mla_q",
                       n_heads=N_HEADS, n_real=MLA_QK, rope=rope)
    s["ka"] = prep_fwd([(s["kv_raw"], 128, lambda h: 2 * h), (z, 128, lambda h: Z_KR // 128)], P["g_mla_k"],
                       name="prep_mla_k", n_heads=N_HEADS, n_real=MLA_QK, rope=rope)
    s["o_a"], s["lse_a"] = attn_fwd(s["qa"], s["ka"], s["kv_raw"], lambda h: 2 * h + 1, mode="mla",
                                    name="attn_mla", n_heads=N_HEADS)
    s["qb"] = prep_fwd([(z, DH, lambda h: FOX_B + h)], P["g_fox_q"], name="prep_h", n_heads=N_HEADS, n_real=DH)
    s["kb"] = prep_fwd([(z, DH, lambda h: FOX_B + N_HEADS + h)], P["g_fox_k"], name="prep_h",
                       n_heads=N_HEADS, n_real=DH)
    _, cum_t = fox_cum_fwd(z, P["b_f"])
    s["cq"] = cum_t[:N_HEADS].reshape(N_HEADS, S, 1)
    s["ck"] = cum_t[:N_HEADS].reshape(N_HEADS, 1, S)
    s["o_b"], s["lse_b"] = attn_fwd(s["qb"], s["kb"], z, lambda h: FOX_B + 2 * N_HEADS + h, mode="fox",
                                    name="attn_fox", n_heads=N_HEADS, cq=s["cq"], ck=s["ck"])
    s["qc"] = prep_fwd([(z, DH, lambda h: CH_B + h)], P["g_ch_q"], name="prep_h", n_heads=N_HEADS, n_real=DH)
    s["kc"] = prep_fwd([(z, DH, lambda h: CH_B + N_HEADS + h)], P["g_ch_k"], name="prep_h",
                       n_heads=N_HEADS, n_real=DH)
    s["tiles"] = relbias_tiles(P["rel_bias"], P["rel_onehot"])
    s["o_c"], s["lse_c"] = attn_fwd(s["qc"], s["kc"], z, lambda h: CH_B + 2 * N_HEADS + h, mode="chunk",
                                    name="attn_chunk", n_heads=N_HEADS, tiles=s["tiles"])
    mid, token = mid_weights(s["o_c"])
    W = dict(W, **mid)
    P = dict(P, g_cross=_after(P["g_cross"], token))
    s["projs"] = [mm_nn(o, W["br"][n], name="mm_br", out_dtype=F32)
                  for n, o in enumerate((s["o_a"], s["o_b"], s["o_c"]))]
    s["merged"] = merge_fwd(z, s["projs"])
    x1 = s["x1"] = mm_nn(s["merged"], W["out"], name="mm_out", out_dtype=F32, res=x)
    s["hq"] = rms_fwd(x1, P["g_cross"], name="rms_d")
    s["xq_raw"] = mm_nn(s["hq"], W["xq"], name="mm_xq", out_dtype=F32)
    s["mem_n"] = rms_fwd(mem, P["g_mem"], name="rms_mem")
    s["mkv"] = mm_nn(s["mem_n"], W["xkv"], name="mm_xkv", out_dtype=F32)
    s["qx"] = prep_fwd([(s["xq_raw"], DH, lambda h: h)], P["g_x_q"], name="prep_xq", n_heads=X_HEADS, n_real=DH)
    s["kx"] = prep_fwd([(s["mkv"], DH, lambda h: h)], P["g_x_k"], name="prep_xk", n_heads=X_HEADS, n_real=DH)
    s["o_x"], s["lse_x"] = attn_fwd(s["qx"], s["kx"], s["mkv"], lambda h: X_HEADS + h, mode="cross",
                                    name="attn_cross", n_heads=X_HEADS)
    x2 = s["x2"] = mm_nn(s["o_x"], W["xo"], name="mm_xo", out_dtype=F32, res=x1)
    s["hm"] = rms_fwd(x2, P["g_mlp"], name="rms_d")
    last, _ = last_weights(s["hm"])
    W = dict(W, **last)
    s["a1"], s["act"] = mm_nn(s["hm"], W["w1"], name="mm_w1", out_dtype=BF16, relu2=True)
    x3 = mm_nn(s["act"], W["w2"], name="mm_w2", out_dtype=F32, res=x2)
    return x3, s, W


LATE_GRADS = ("w2", "w1", "xo", "xq", "xkv", "out", "br")


def layer_bwd(g, x, mem, W, P, rope, s, grads_done):
    S = x.shape[0]
    z = s["z"]
    dw, ds = {}, {}
    da1 = mm_nt(g, W["w2"], name="mm_w2_dx", out_dtype=BF16, relu_mul=s["a1"])
    dw["w2"] = mm_tn(s["act"], g, nb=1, name="mm_w2_dw", out_dtype=BF16)
    dhm = mm_nt(da1, W["w1"], name="mm_w1_dx", out_dtype=F32)
    dw["w1"] = mm_tn(s["hm"], da1, nb=4, name="mm_w1_dw", out_dtype=BF16)
    g2, ds["g_mlp"] = rms_bwd(s["x2"], P["g_mlp"], dhm, name="rms_d_bwd", res=g)
    do_x = mm_nt(g2, W["xo"], name="mm_xo_dx", out_dtype=BF16)
    dw["xo"] = mm_tn(s["o_x"], g2, nb=4, name="mm_xo_dw", out_dtype=BF16)
    dqx, dkx, dvx = attn_bwd(s["qx"], s["kx"], s["mkv"], lambda h: X_HEADS + h, s["o_x"], do_x, s["lse_x"],
                             mode="cross", name="attn_cross_bwd", n_heads=X_HEADS)
    dxq_raw, ds["g_x_q"] = prep_bwd_q(s["xq_raw"], dqx, P["g_x_q"], name="prep_xq_bwd", n_heads=X_HEADS,
                                      dh=DH, n_real=DH)
    dmkv, dgk = prep_bwd_groups(s["mkv"], 0, [dkx, dvx], [P["g_x_k"], None], name="prep_xkv_bwd",
                                n_heads=X_HEADS, kinds=("norm", "copy"))
    ds["g_x_k"] = dgk[0]
    dhq = mm_nt(dxq_raw, W["xq"], name="mm_xq_dx", out_dtype=F32)
    dw["xq"] = mm_tn(s["hq"], dxq_raw, nb=1, name="mm_xq_dw", out_dtype=BF16)
    dmem_n = mm_nt(dmkv, W["xkv"], name="mm_xkv_dx", out_dtype=F32)
    dw["xkv"] = mm_tn(s["mem_n"], dmkv, nb=1, name="mm_xkv_dw", out_dtype=BF16)
    _, ds["g_mem"] = rms_bwd(mem, P["g_mem"], dmem_n, name="rms_mem_bwd", need_dx=False)
    g1, ds["g_cross"] = rms_bwd(s["x1"], P["g_cross"], dhq, name="rms_d_bwd", res=g2)
    dmerged = mm_nt(g1, W["out"], name="mm_out_dx", out_dtype=F32)
    dw["out"] = mm_tn(s["merged"], g1, nb=1, name="mm_out_dw", out_dtype=BF16)
    dgl, dproj = merge_bwd(z, s["projs"], dmerged)
    outs = (s["o_a"], s["o_b"], s["o_c"])
    do = [mm_nt(dproj[n], W["br"][n], name="mm_br_dx", out_dtype=BF16) for n in range(3)]
    dw["br"] = [mm_tn(outs[n], dproj[n], nb=4, name="mm_br_dw", out_dtype=BF16) for n in range(3)]
    token = grads_done("late", {k: dw.pop(k) for k in LATE_GRADS})
    dqc, dkc, dvc, dtiles = attn_bwd(s["qc"], s["kc"], z, lambda h: CH_B + 2 * N_HEADS + h, s["o_c"], do[2],
                                     s["lse_c"], mode="chunk", name="attn_chunk_bwd", n_heads=N_HEADS,
                                     tiles=_after(s["tiles"], token))
    d_ch, dg_ch = prep_bwd_groups(z, CH_B, [dqc, dkc, dvc], [P["g_ch_q"], P["g_ch_k"], None],
                                  name="prep_h_bwd", n_heads=N_HEADS, kinds=("norm", "norm", "copy"))
    ds["g_ch_q"], ds["g_ch_k"] = dg_ch[0], dg_ch[1]
    ds["rel_bias"] = relbias_tiles_bwd(dtiles, P["rel_onehot"])
    token = grads_done("late_followup", d_ch)
    dqb, dkb, dvb, dcq, dck = attn_bwd(s["qb"], s["kb"], z, lambda h: FOX_B + 2 * N_HEADS + h, s["o_b"], do[1],
                                       s["lse_b"], mode="fox", name="attn_fox_bwd", n_heads=N_HEADS,
                                       cq=s["cq"], ck=_after(s["ck"], token))
    d_fox, dg_fox = prep_bwd_groups(z, FOX_B, [dqb, dkb, dvb], [P["g_fox_q"], P["g_fox_k"], None],
                                    name="prep_h_bwd", n_heads=N_HEADS, kinds=("norm", "norm", "copy"))
    ds["g_fox_q"], ds["g_fox_k"] = dg_fox[0], dg_fox[1]
    dcum = jnp.pad((dcq[:, :, 0] + dck[:, 0, :]).T, ((0, 0), (0, 128 - N_HEADS)))
    dff, dbf = fox_cum_bwd(z, P["b_f"], dcum)
    ds["b_f"] = dbf[:, :N_HEADS]
    dqa, dka, dva = attn_bwd(s["qa"], s["ka"], s["kv_raw"], lambda h: 2 * h + 1, s["o_a"], do[0], s["lse_a"],
                             mode="mla", name="attn_mla_bwd", n_heads=N_HEADS)
    dq_raw, dgq = prep_bwd_q(s["q_raw"], dqa, P["g_mla_q"], name="prep_mla_q_bwd", n_heads=N_HEADS,
                             dh=MLA_PAD, n_real=MLA_QK, rope=rope)
    dkv_raw, dkr, dgk = prep_bwd_mla_k(s["kv_raw"], z, dka, dva, P["g_mla_k"], rope, name="prep_mla_k_bwd")
    ds["g_mla_q"], ds["g_mla_k"] = dgq[:, :MLA_QK], dgk[:, :MLA_QK]
    dcq_n = mm_nt(dq_raw, W["uq_p"], name="mm_uq_dx", out_dtype=F32)
    dw["uq_p"] = mm_tn(s["cq_n"], dq_raw, nb=1, name="mm_uq_dw", out_dtype=BF16)
    dckv_n = mm_nt(dkv_raw, W["ukv"], name="mm_ukv_dx", out_dtype=F32)
    dw["ukv"] = mm_tn(s["ckv_n"], dkv_raw, nb=4, name="mm_ukv_dw", out_dtype=BF16)
    d_cq, ds["g_cq"] = rms_bwd(z, P["g_cq"], dcq_n, name="rms_cq_bwd", col0=Z_CQ, width=Q_LORA, dx_dtype=BF16)
    d_ckv, ds["g_ckv"] = rms_bwd(z, P["g_ckv"], dckv_n, name="rms_ckv_bwd", col0=Z_CKV, width=KV_LORA,
                                 dx_dtype=BF16)
    dz = jnp.concatenate([d_cq, d_ckv, dkr.astype(BF16), dff.astype(BF16), d_fox, d_ch, *dgl], axis=1)
    dw["in_p"] = mm_tn(s["h"], dz, nb=1, name="mm_in_dw", out_dtype=BF16)
    token = grads_done("early", dw)
    dh = mm_nt(dz, W["in_p"], name="mm_in_dx", out_dtype=F32, after=token)
    g0, ds["g_mix"] = rms_bwd(x, P["g_mix"], dh, name="rms_d_bwd", res=g1)
    return g0, ds


def local_step(x, mem, target, Ps, weights_of, grads_done):
    rope = rope_tables(x.shape[0])
    onehot = relbias_onehot()
    Ps = [dict(P, rel_onehot=onehot) for P in Ps]
    Ws, saved, xs = [], [], [x]
    for l, P in enumerate(Ps):
        W_early, mid_weights, last_weights, token = weights_of(l, xs[-1])
        y, s, W = layer_fwd(xs[-1], mem, W_early, dict(P, g_mix=_after(P["g_mix"], token)), rope,
                            mid_weights, last_weights)
        xs.append(y)
        saved.append(s)
        Ws.append(W)
    g, loss = loss_head(xs[-1], target)
    dss, token = [], None
    for l in reversed(range(len(Ps))):
        g, ds = layer_bwd(g, xs[l], mem, Ws[l], dict(Ps[l], g_mlp=_after(Ps[l]["g_mlp"], token)), rope,
                          saved[l], functools.partial(grads_done, l))
        token = grads_done(l, "early_followup", g)
        dss.append(ds)
    return loss, g, dss[::-1]


HBM_SPEC = pl.BlockSpec(memory_space=pltpu.HBM)


def _place():
    return lax.axis_index("x"), lax.axis_index("y"), lax.axis_index("c")


def _other_chips(x, y):
    return [(1 - x, y), (x, 1 - y), (1 - x, 1 - y)]


def _remote(src, dst, send_sems, recv_sems, k, to):
    return pltpu.make_async_remote_copy(src_ref=src, dst_ref=dst, send_sem=send_sems.at[k],
                                        recv_sem=recv_sems.at[k], device_id=to, device_id_type=MESH)


SEM_SPEC = pl.BlockSpec(memory_space=pltpu.SEMAPHORE)
SPLIT_EFFECT = pltpu.SideEffectType.DATAFLOW_SIDE_EFFECTING


COPIES_PER_ARRAY = {"gather": 3, "scatter": 3, "halves": 1, "share": 1}


def _chip_exchange_copies(kind, srcs, lands, send_sems, recv_sems):
    x, y, c = _place()
    me = 2 * x + y
    cps = []
    for i, (src, land) in enumerate(zip(srcs, lands)):
        if kind == "halves":
            h = src.shape[1] // 2
            cps.append(_remote(src.at[:, pl.ds((1 - c) * h, h), :], land, send_sems, recv_sems, i, (x, y, 1 - c)))
            continue
        if kind == "share":
            cps.append(_remote(src, land.at[c], send_sems, recv_sems, i, (x, y, 1 - c)))
            continue
        for j, (cx, cy) in enumerate(_other_chips(x, y)):
            if kind == "gather":
                h = src.shape[0] // 2
                s_ref, d_ref = src.at[pl.ds(c * h, h), :], land.at[me, pl.ds(c * h, h), :]
            else:
                s_ref, d_ref = src.at[2 * cx + cy], land.at[me]
            cps.append(_remote(s_ref, d_ref, send_sems, recv_sems, 3 * i + j, (cx, cy, c)))
    return cps


def chip_exchange_start(kind, srcs, land_shapes, *, name, after=None):
    n = len(srcs)
    n_in = 2 * n + (after is not None)

    def body(*refs):
        send_sems, recv_sems = refs[n_in], refs[n_in + 1]
        for cp in _chip_exchange_copies(kind, refs[:n], refs[n:2 * n], send_sems, recv_sems):
            cp.start()
        refs[-1][...] = jnp.zeros_like(refs[-1])

    hbm = lambda a: pltpu.with_memory_space_constraint(a, pltpu.HBM)
    ins = [hbm(s) for s in srcs] + [hbm(lax.empty(s.shape, s.dtype)) for s in land_shapes]
    sems = pltpu.SemaphoreType.DMA((COPIES_PER_ARRAY[kind] * n,))
    out = pl.pallas_call(
        body, name=name, interpret=False,
        out_shape=(sems, sems, *[pltpu.HBM(a.shape, a.dtype) for a in ins], _sds((8, 128), F32)),
        in_specs=[HBM_SPEC] * (2 * n) + [pl.BlockSpec(memory_space=pl.ANY)] * (after is not None),
        out_specs=(SEM_SPEC, SEM_SPEC, *[HBM_SPEC] * (2 * n), pl.BlockSpec(memory_space=pltpu.VMEM)),
        input_output_aliases={i: 2 + i for i in range(2 * n)},
        compiler_params=pltpu.CompilerParams(has_side_effects=SPLIT_EFFECT))(
            *ins, *([after] if after is not None else []))
    return out[0], out[1], list(out[2:2 + n]), list(out[2 + n:2 + 2 * n]), out[-1]


def chip_exchange_wait(kind, started, after, *, name):
    send_sems, recv_sems, srcs, lands, _ = started
    n = len(srcs)

    def body(*refs):
        send_sems, recv_sems = refs[2 * n], refs[2 * n + 1]
        for cp in _chip_exchange_copies(kind, refs[:n], refs[n:2 * n], send_sems, recv_sems):
            cp.wait_send()
            cp.wait_recv()

    thru = srcs + lands
    out = pl.pallas_call(
        body, name=name, interpret=False,
        out_shape=tuple(pltpu.HBM(a.shape, a.dtype) for a in thru),
        in_specs=[HBM_SPEC] * (2 * n) + [SEM_SPEC, SEM_SPEC, pl.BlockSpec(memory_space=pl.ANY)],
        out_specs=tuple([HBM_SPEC] * (2 * n)),
        input_output_aliases={i: i for i in range(2 * n)},
        compiler_params=pltpu.CompilerParams(has_side_effects=SPLIT_EFFECT))(*thru, send_sems, recv_sems, after)
    return list(out[:n]), list(out[n:])


def gather_to_sibling(lands):
    n = len(lands)

    def body(*refs):
        outs = refs[n:2 * n]
        send_sems, recv_sems = refs[2 * n:]
        x, y, c = _place()
        sends = []
        for i in range(n):
            h = outs[i].shape[1] // 2
            for j, (cx, cy) in enumerate(_other_chips(x, y)):
                landed = outs[i].at[2 * cx + cy, pl.ds(c * h, h), :]
                cp = _remote(landed, landed, send_sems, recv_sems, 3 * i + j, (x, y, 1 - c))
                cp.start()
                sends.append(cp)
        for cp in sends:
            cp.wait()

    return pl.pallas_call(
        body, out_shape=tuple(_sds(a.shape, a.dtype) for a in lands), in_specs=[HBM_SPEC] * n,
        out_specs=tuple([HBM_SPEC] * n), input_output_aliases={i: i for i in range(n)},
        scratch_shapes=[pltpu.SemaphoreType.DMA((3 * n,)), pltpu.SemaphoreType.DMA((3 * n,))],
        name="gather_to_sibling", interpret=False)(*lands)


def add_pair(part, got, half_idx):
    _, a, b = part.shape
    h = a // 2
    tr = _pick(h, (512, 256, 128))

    def body(c_ref, p_ref, g_ref, o_ref):
        o_ref[...] = (p_ref[...].astype(F32) + g_ref[...].astype(F32)).astype(o_ref.dtype)

    spec = pl.BlockSpec((None, tr, b), lambda s_, i, c_ref: (s_, i, 0))
    grid_spec = pltpu.PrefetchScalarGridSpec(
        num_scalar_prefetch=1, grid=(4, h // tr),
        in_specs=[pl.BlockSpec((None, None, tr, b), lambda s_, i, c_ref: (s_, c_ref[0], i, 0)), spec],
        out_specs=spec)
    return pl.pallas_call(
        body, out_shape=_sds((4, h, b), part.dtype), grid_spec=grid_spec, name="rs_add_pair", interpret=False,
        compiler_params=pltpu.CompilerParams(vmem_limit_bytes=VMEM_LIMIT_BYTES))(
            half_idx, part.reshape(4, 2, h, b), got)


def sum_slots(r):
    _, h, b = r.shape
    tr = _pick(h, (512, 256, 128))

    def body(r0, r1, r2, r3, o_ref):
        o_ref[...] = ((r0[...].astype(F32) + r1[...].astype(F32)) + r2[...].astype(F32)) + r3[...].astype(F32)

    specs = [pl.BlockSpec((None, tr, b), functools.partial(lambda i, s_: (s_, i, 0), s_=s_)) for s_ in range(4)]
    return _pcall(body, name="rs_sum_slots", out_shape=_sds((h, b), F32), grid=(h // tr,),
                  in_specs=specs, out_specs=pl.BlockSpec((tr, b), lambda i: (i, 0)))(r, r, r, r)


def reduce_halves_start(parts, *, tag):
    return chip_exchange_start("halves", parts, [_sds((4, p.shape[1] // 2, p.shape[2]), p.dtype) for p in parts],
                               name="halves_start_" + tag)


def reduce_scatter_start(halves_started, after, core, *, tag):
    parts, gots = chip_exchange_wait("halves", halves_started, after, name="halves_wait_" + tag)
    half_idx = core.reshape(1).astype(jnp.int32)
    chip_sums = [add_pair(p, g, half_idx) for p, g in zip(parts, gots)]
    return chip_exchange_start("scatter", chip_sums, chip_sums, name="scatter_start_" + tag)


def reduce_share_start(scatter_started, after, chip, *, tag):
    chip_sums, slots = chip_exchange_wait("scatter", scatter_started, after, name="scatter_wait_" + tag)
    slots = [lax.dynamic_update_slice(s_, lax.dynamic_index_in_dim(q, chip, 0, keepdims=True), (chip, 0, 0))
             for s_, q in zip(slots, chip_sums)]
    halves = [sum_slots(s_) for s_ in slots]
    return chip_exchange_start("share", halves, [_sds((2,) + t.shape, t.dtype) for t in halves],
                               name="share_start_" + tag)


def reduce_finish(share_started, after, core, *, tag):
    halves, both = chip_exchange_wait("share", share_started, after, name="share_wait_" + tag)
    both = [lax.dynamic_update_slice(o, t[None], (core, 0, 0)) for o, t in zip(both, halves)]
    return [o.reshape(2 * o.shape[1], o.shape[2]) for o in both]


def allreduce_small(v):
    Rs = v.shape[0]

    def body(v_ref, o_ref, buf, send_sems, recv_sems):
        x, y, c = _place()
        me = 4 * x + 2 * y + c
        buf[me] = v_ref[...]
        flips = [(fx, fy, fc) for fx in (0, 1) for fy in (0, 1) for fc in (0, 1)][1:]
        sends = []
        for k, (fx, fy, fc) in enumerate(flips):
            to = ((1 - x) if fx else x, (1 - y) if fy else y, (1 - c) if fc else c)
            cp = _remote(v_ref, buf.at[me], send_sems, recv_sems, k, to)
            cp.start()
            sends.append(cp)
        for cp in sends:
            cp.wait()
        acc = buf[0]
        for d in range(1, 8):
            acc = acc + buf[d]
        o_ref[...] = acc

    vm = pl.BlockSpec(memory_space=pltpu.VMEM)
    return pl.pallas_call(
        body, out_shape=_sds((Rs, 128), F32), in_specs=[vm], out_specs=vm,
        scratch_shapes=[pltpu.VMEM((8, Rs, 128), F32), pltpu.SemaphoreType.DMA((7,)),
                        pltpu.SemaphoreType.DMA((7,))],
        name="allreduce_small", interpret=False)(v)


INPUT_NAMES = (("x", "mem") + WEIGHT_ORDER + ("loss_target",) + tuple("m_" + n for n in WEIGHT_ORDER)
               + tuple("v_" + n for n in WEIGHT_ORDER))


def _pack_small(vals, n_layers, extra=None):
    flat = jnp.concatenate([vals[n].reshape(n_layers, -1).astype(F32) for n in SMALL_ORDER], axis=1).reshape(-1)
    if extra is not None:
        flat = jnp.concatenate([flat, extra.reshape(-1)])
    n = flat.shape[0]
    rows = -(-n // 1024) * 8
    return jnp.pad(flat, (0, rows * 128 - n)).reshape(rows, 128)


def _unpack_small(packed, like, n_layers):
    per_layer = sum(int(np.prod(like[n].shape[1:])) for n in SMALL_ORDER)
    body = packed.reshape(-1)[:n_layers * per_layer].reshape(n_layers, per_layer)
    out, off = {}, 0
    for n in SMALL_ORDER:
        k = int(np.prod(like[n].shape[1:]))
        out[n] = body[:, off:off + k].reshape(like[n].shape)
        off += k
    return out, packed.reshape(-1)[n_layers * per_layer]


def kernel(x, mem, g_mix, w_in, g_cq, w_uq, g_ckv, w_ukv, g_mla_q, g_mla_k, b_f, g_fox_q, g_fox_k, rel_bias, g_ch_q, g_ch_k, w_br, w_out, g_cross, g_mem, w_xq, w_xkv, g_x_q, g_x_k, w_xo, g_mlp, w_1, w_2, loss_target, m_g_mix, m_w_in, m_g_cq, m_w_uq, m_g_ckv, m_w_ukv, m_g_mla_q, m_g_mla_k, m_b_f, m_g_fox_q, m_g_fox_k, m_rel_bias, m_g_ch_q, m_g_ch_k, m_w_br, m_w_out, m_g_cross, m_g_mem, m_w_xq, m_w_xkv, m_g_x_q, m_g_x_k, m_w_xo, m_g_mlp, m_w_1, m_w_2, v_g_mix, v_w_in, v_g_cq, v_w_uq, v_g_ckv, v_w_ukv, v_g_mla_q, v_g_mla_k, v_b_f, v_g_fox_q, v_g_fox_k, v_rel_bias, v_g_ch_q, v_g_ch_k, v_w_br, v_w_out, v_g_cross, v_g_mem, v_w_xq, v_w_xkv, v_g_x_q, v_g_x_k, v_w_xo, v_g_mlp, v_w_1, v_w_2):
    d = dict(zip(INPUT_NAMES, (x, mem, g_mix, w_in, g_cq, w_uq, g_ckv, w_ukv, g_mla_q, g_mla_k, b_f, g_fox_q, g_fox_k, rel_bias, g_ch_q, g_ch_k, w_br, w_out, g_cross, g_mem, w_xq, w_xkv, g_x_q, g_x_k, w_xo, g_mlp, w_1, w_2, loss_target, m_g_mix, m_w_in, m_g_cq, m_w_uq, m_g_ckv, m_w_ukv, m_g_mla_q, m_g_mla_k, m_b_f, m_g_fox_q, m_g_fox_k, m_rel_bias, m_g_ch_q, m_g_ch_k, m_w_br, m_w_out, m_g_cross, m_g_mem, m_w_xq, m_w_xkv, m_g_x_q, m_g_x_k, m_w_xo, m_g_mlp, m_w_1, m_w_2, v_g_mix, v_w_in, v_g_cq, v_w_uq, v_g_ckv, v_w_ukv, v_g_mla_q, v_g_mla_k, v_b_f, v_g_fox_q, v_g_fox_k, v_rel_bias, v_g_ch_q, v_g_ch_k, v_w_br, v_w_out, v_g_cross, v_g_mem, v_w_xq, v_w_xkv, v_g_x_q, v_g_x_k, v_w_xo, v_g_mlp, v_w_1, v_w_2)))
    n_layers = g_mix.shape[0]
    assert x.shape[0] == 1 and x.shape[2] == D_MODEL and mem.shape[1:] == (MEM_LEN, D_MODEL)
    for n in PACK_ORDER:
        assert d[n].shape[1:] == SHARD_SHAPES[n], (n, d[n].shape)

    chip = 2 * lax.axis_index("x") + lax.axis_index("y")
    core = lax.axis_index("c")

    def start_gather(l, names, after):
        mine = [_rows2d(d[n][l].astype(BF16), n) for n in names]
        return chip_exchange_start("gather", mine, [_sds((4,) + m_.shape, BF16) for m_ in mine],
                                   name=f"gather_start_{l}_{names[0]}", after=after)

    def finish_gather(l, names, started, after):
        mine, lands = chip_exchange_wait("gather", started, after, name=f"gather_wait_{l}_{names[0]}")
        full = [lax.dynamic_update_slice(t, m_[None], (chip, 0, 0)) for t, m_ in zip(gather_to_sibling(lands), mine)]
        return mine, full_weights({n: f.reshape((4,) + SHARD_SHAPES[n]) for n, f in zip(names, full)})

    in_flight = {"early": start_gather(0, EARLY_WEIGHTS, x)}

    def weights_of(l, x_l):
        mine, early = finish_gather(l, EARLY_WEIGHTS, in_flight.pop("early"), x_l)
        in_flight["mid"] = start_gather(l, MID_WEIGHTS, mine[0])
        in_flight["last"] = start_gather(l, LAST_WEIGHTS, in_flight["mid"][4])

        def mid_weights(after):
            mine, mid = finish_gather(l, MID_WEIGHTS, in_flight.pop("mid"), after)
            if l + 1 == n_layers:
                return mid, None
            in_flight["early"] = start_gather(l + 1, EARLY_WEIGHTS, mine[0])
            return mid, in_flight["early"][4]

        def last_weights(after):
            return finish_gather(l, LAST_WEIGHTS, in_flight.pop("last"), after)[1], None

        return early, mid_weights, last_weights, in_flight["mid"][4] + in_flight["last"][4]

    halves, scatters = {}, []

    def grads_done(l, stage, arg):
        names = LATE_WEIGHTS if stage.startswith("late") else EARLY_WEIGHTS
        tag = f"{l}_{names[0]}"
        if not stage.endswith("followup"):
            sg = shard_grads(arg)
            halves[tag] = reduce_halves_start([_rows2d(sg[n], n) for n in names], tag=tag)
            return halves[tag][4]
        started = reduce_scatter_start(halves.pop(tag), arg, core, tag=tag)
        scatters.append((l, names, tag, started))
        return started[4]

    Ps = [layer_params(d, l) for l in range(n_layers)]
    loss, dx, dss = local_step(x[0], mem[0], loss_target[0], Ps, weights_of, grads_done)
    shares = [(l, names, tag, reduce_share_start(started, dx, chip, tag=tag))
              for l, names, tag, started in scatters]
    big = [{} for _ in range(n_layers)]
    for l, names, tag, started in shares:
        big[l].update(zip(names, reduce_finish(started, shares[-1][3][4], core, tag=tag)))

    small_local = {n: jnp.stack([dss[l][n].reshape(d[n].shape[1:]) for l in range(n_layers)]) for n in SMALL_ORDER}
    small_sum, loss_sum = _unpack_small(allreduce_small(_pack_small(small_local, n_layers, extra=loss)),
                                        {n: d[n] for n in SMALL_ORDER}, n_layers)

    grads, delta, new_m, new_v = {}, {}, {}, {}
    for n in PACK_ORDER:
        outs = adamw(_rows2d(d[n], n), [big[l][n] for l in range(n_layers)], _rows2d(d["m_" + n], n),
                     _rows2d(d["v_" + n], n), name="adamw_" + n)
        grads[n], delta[n], new_m[n], new_v[n] = (o.reshape(d[n].shape) for o in outs)
    like = {n: d[n] for n in SMALL_ORDER}
    sm = adamw(_pack_small(like, n_layers)[None], [_pack_small(small_sum, n_layers)],
               _pack_small({n: d["m_" + n] for n in SMALL_ORDER}, n_layers)[None],
               _pack_small({n: d["v_" + n] for n in SMALL_ORDER}, n_layers)[None], name="adamw_small")
    for res, src in zip((grads, delta, new_m, new_v), sm):
        res.update(_unpack_small(src[0], like, n_layers)[0])

    return (loss_sum, dx[None], *[grads[n] for n in WEIGHT_ORDER], *[delta[n] for n in WEIGHT_ORDER],
            *[new_m[n] for n in WEIGHT_ORDER], *[new_v[n] for n in WEIGHT_ORDER])
```

```python
import functools

import numpy as np
import jax
import jax.numpy as jnp
from jax import lax
from jax.experimental import pallas as pl
from jax.experimental.pallas import tpu as pltpu

F32 = jnp.float32
BF16 = jnp.bfloat16
MXU_DTYPE = jnp.bfloat16
MESH = pl.DeviceIdType.MESH

D_MODEL = 2048
MIX_W = 1024
N_HEADS = 8
DH = 128
MLA_NOPE = 128
MLA_ROPE = 64
MLA_QK = MLA_NOPE + MLA_ROPE
MLA_PAD = 256
Q_LORA = 512
KV_LORA = 256
CHUNK = 64
CHUNK_SHIFT = CHUNK.bit_length() - 1
LEFT_CHUNKS = 8
REL_CLIP = 128
N_REL = 2 * REL_CLIP + 1
X_HEADS = 4
MEM_LEN = 256
D_FF = 8192
ROPE_THETA = 10000.0
EPS = 1e-6
NEG = -1e30

Z_CQ, Z_CKV, Z_KR, Z_FF = 0, 512, 768, 896
ZS_W = 1024
Z_FOX = 1024
Z_CH = Z_FOX + 3 * MIX_W
Z_GATE = Z_CH + 3 * MIX_W
Z_TOT = Z_GATE + 3 * D_MODEL
W_IN_CUTS = (0, 512, 768, 832, 3904, 3912, 6984, 13128)
W_IN_SHARD = W_IN_CUTS[-1] // 4
W_IN_PIECES = ((0, 0, 832), (Z_FF, 3904, 8), (Z_FOX, 832, 3072), (Z_CH, 3912, 9216))

ADAM_LR, ADAM_B1, ADAM_B2, ADAM_EPS, ADAM_WD, ADAM_STEP = 0.001, 0.9, 0.999, 1e-08, 0.01, 10

VMEM_LIMIT_BYTES = 56 * 1024 * 1024
PACK_ORDER = ("w_uq", "w_ukv", "w_br", "w_out", "w_xq", "w_xkv", "w_xo", "w_1", "w_2", "w_in")
SMALL_ORDER = ("g_mix", "g_cq", "g_ckv", "g_mla_q", "g_mla_k", "b_f", "g_fox_q", "g_fox_k", "rel_bias",
               "g_ch_q", "g_ch_k", "g_cross", "g_mem", "g_x_q", "g_x_k", "g_mlp")
WEIGHT_ORDER = ("g_mix", "w_in", "g_cq", "w_uq", "g_ckv", "w_ukv", "g_mla_q", "g_mla_k", "b_f", "g_fox_q",
                "g_fox_k", "rel_bias", "g_ch_q", "g_ch_k", "w_br", "w_out", "g_cross", "g_mem", "w_xq",
                "w_xkv", "g_x_q", "g_x_k", "w_xo", "g_mlp", "w_1", "w_2")


def _pick(n, prefs):
    for p in prefs:
        if n % p == 0:
            return p
    raise ValueError(f"no block size among {prefs} divides {n}")


def _pcall(body, *, name, out_shape, in_specs, out_specs, grid=(), scratch=(), aliases=None):
    return pl.pallas_call(
        body, out_shape=out_shape, grid=grid, in_specs=in_specs, out_specs=out_specs,
        scratch_shapes=scratch, name=name, interpret=False,
        input_output_aliases=aliases or {},
        compiler_params=pltpu.CompilerParams(vmem_limit_bytes=VMEM_LIMIT_BYTES))


def _sds(shape, dtype):
    return jax.ShapeDtypeStruct(tuple(shape), dtype)


def _mx(v):
    return v.astype(MXU_DTYPE)


def mm_nn(a, b3, *, name, out_dtype, a_col0=0, res=None, relu2=False):
    M = a.shape[0]
    nb, K, Ns = b3.shape
    N = nb * Ns
    tm = _pick(M, (1024, 512, 256, 128))
    tk = _pick(K, (2048, 1024, 512, 256))
    tn = _pick(Ns, (512, 256, 128))
    assert a_col0 % tk == 0
    nk, nbs, ka0 = K // tk, Ns // tn, a_col0 // tk
    n_out = 2 if relu2 else 1

    def body(*refs):
        a_ref, b_ref = refs[0], refs[1]
        pos = 2
        res_ref = None
        if res is not None:
            res_ref = refs[pos]
            pos += 1
        outs = refs[pos:pos + n_out]
        acc_ref = refs[pos + n_out] if nk > 1 else None
        part = jnp.dot(_mx(a_ref[...]), _mx(b_ref[...]), preferred_element_type=F32)

        def finish(acc):
            if res_ref is not None:
                acc = acc + res_ref[...]
            outs[0][...] = acc.astype(outs[0].dtype)
            if relu2:
                r = jnp.maximum(acc, 0.0)
                outs[1][...] = (r * r).astype(outs[1].dtype)

        if nk == 1:
            finish(part)
        else:
            k = pl.program_id(2)

            @pl.when(k == 0)
            def _():
                acc_ref[...] = part

            @pl.when(k > 0)
            def _():
                acc_ref[...] += part

            @pl.when(k == nk - 1)
            def _():
                finish(acc_ref[...])

    in_specs = [pl.BlockSpec((tm, tk), lambda i, j, k: (i, ka0 + k)),
                pl.BlockSpec((None, tk, tn), lambda i, j, k: (j // nbs, k, j % nbs))]
    args = [a, b3]
    if res is not None:
        in_specs.append(pl.BlockSpec((tm, tn), lambda i, j, k: (i, j)))
        args.append(res)
    o_spec = pl.BlockSpec((tm, tn), lambda i, j, k: (i, j))
    if relu2:
        out_shape, out_specs = (_sds((M, N), out_dtype), _sds((M, N), out_dtype)), (o_spec, o_spec)
    else:
        out_shape, out_specs = _sds((M, N), out_dtype), o_spec
    scratch = (pltpu.VMEM((tm, tn), F32),) if nk > 1 else ()
    return _pcall(body, name=name, out_shape=out_shape, grid=(M // tm, N // tn, nk),
                  in_specs=in_specs, out_specs=out_specs, scratch=scratch)(*args)


def mm_nt(a, b3, *, name, out_dtype, a_col0=0, res=None, relu_mul=None, after=None):
    M = a.shape[0]
    nb, K, Ns = b3.shape
    tm = _pick(M, (1024, 512, 256, 128))
    tk = _pick(K, (1024, 512, 256))
    tn = _pick(Ns, (2048, 1024, 512, 256, 128) if nb == 1 and Ns <= 2048 else (1024, 512, 256, 128))
    assert a_col0 % tn == 0
    nbs = Ns // tn
    nn, a0 = nb * nbs, a_col0 // tn

    def body(*refs):
        a_ref, b_ref = refs[0], refs[1]
        pos = 2
        mul_ref = res_ref = None
        if relu_mul is not None:
            mul_ref = refs[pos]
            pos += 1
        if res is not None:
            res_ref = refs[pos]
            pos += 1
        pos += after is not None
        o_ref = refs[pos]
        acc_ref = refs[pos + 1] if nn > 1 else None
        part = lax.dot_general(_mx(a_ref[...]), _mx(b_ref[...]), (((1,), (1,)), ((), ())),
                               preferred_element_type=F32)

        def finish(acc):
            if mul_ref is not None:
                acc = acc * (2.0 * jnp.maximum(mul_ref[...].astype(F32), 0.0))
            if res_ref is not None:
                acc = acc + res_ref[...]
            o_ref[...] = acc.astype(o_ref.dtype)

        if nn == 1:
            finish(part)
        else:
            j = pl.program_id(2)

            @pl.when(j == 0)
            def _():
                acc_ref[...] = part

            @pl.when(j > 0)
            def _():
                acc_ref[...] += part

            @pl.when(j == nn - 1)
            def _():
                finish(acc_ref[...])

    in_specs = [pl.BlockSpec((tm, tn), lambda i, kk, j: (i, a0 + j)),
                pl.BlockSpec((None, tk, tn), lambda i, kk, j: (j // nbs, kk, j % nbs))]
    args = [a, b3]
    for extra in (relu_mul, res):
        if extra is not None:
            in_specs.append(pl.BlockSpec((tm, tk), lambda i, kk, j: (i, kk)))
            args.append(extra)
    if after is not None:
        in_specs.append(pl.BlockSpec(memory_space=pl.ANY))
        args.append(after)
    scratch = (pltpu.VMEM((tm, tk), F32),) if nn > 1 else ()
    return _pcall(body, name=name, out_shape=_sds((M, K), out_dtype), grid=(M // tm, K // tk, nn),
                  in_specs=in_specs, out_specs=pl.BlockSpec((tm, tk), lambda i, kk, j: (i, kk)),
                  scratch=scratch)(*args)


def mm_tn(a, c, *, nb, name, out_dtype, K=None, N=None, a_col0=0, c_col0=0):
    M = a.shape[0]
    K = K or a.shape[1]
    N = N or c.shape[1]
    Ns = N // nb
    tm = _pick(M, (2048, 1024, 512, 256))
    tk = _pick(K, (1024, 512, 256))
    tn = _pick(Ns, (1024, 512, 256, 128))
    assert a_col0 % tk == 0 and c_col0 % tn == 0
    nm, nbs, a0, c0 = M // tm, Ns // tn, a_col0 // tk, c_col0 // tn

    def body(*refs):
        a_ref, c_ref, o_ref = refs[:3]
        acc_ref = refs[3] if nm > 1 else None
        part = lax.dot_general(_mx(a_ref[...]), _mx(c_ref[...]), (((0,), (0,)), ((), ())),
                               preferred_element_type=F32)
        if nm == 1:
            o_ref[...] = part.astype(o_ref.dtype)
        else:
            m = pl.program_id(2)

            @pl.when(m == 0)
            def _():
                acc_ref[...] = part

            @pl.when(m > 0)
            def _():
                acc_ref[...] += part

            @pl.when(m == nm - 1)
            def _():
                o_ref[...] = acc_ref[...].astype(o_ref.dtype)

    scratch = (pltpu.VMEM((tk, tn), F32),) if nm > 1 else ()
    return _pcall(
        body, name=name, out_shape=_sds((nb, K, Ns), out_dtype), grid=(N // tn, K // tk, nm),
        in_specs=[pl.BlockSpec((tm, tk), lambda j, kk, m: (m, a0 + kk)),
                  pl.BlockSpec((tm, tn), lambda j, kk, m: (m, c0 + j))],
        out_specs=pl.BlockSpec((None, tk, tn), lambda j, kk, m: (j // nbs, kk, j % nbs)),
        scratch=scratch)(a, c)


def rms_fwd(x, g, *, name, col0=0, width=None, out_dtype=BF16):
    R = x.shape[0]
    width = width or x.shape[1]
    assert col0 % width == 0
    cb = col0 // width
    tr = _pick(R, (512, 256, 128))

    def body(x_ref, g_ref, o_ref):
        xf = x_ref[...].astype(F32)
        r = lax.rsqrt(jnp.mean(xf * xf, axis=1, keepdims=True) + EPS)
        o_ref[...] = (xf * r * g_ref[...]).astype(o_ref.dtype)

    return _pcall(body, name=name, out_shape=_sds((R, width), out_dtype), grid=(R // tr,),
                  in_specs=[pl.BlockSpec((tr, width), lambda i: (i, cb)),
                            pl.BlockSpec((1, width), lambda i: (0, 0))],
                  out_specs=pl.BlockSpec((tr, width), lambda i: (i, 0)))(x, g)


def rms_bwd(x, g, dy, *, name, col0=0, width=None, res=None, dx_dtype=F32, need_dx=True):
    R = x.shape[0]
    width = width or x.shape[1]
    cb = col0 // width
    tr = _pick(R, (512, 256, 128))

    def body(*refs):
        x_ref, g_ref, dy_ref = refs[:3]
        pos = 3
        res_ref = None
        if res is not None:
            res_ref = refs[pos]
            pos += 1
        dx_ref = None
        if need_dx:
            dx_ref = refs[pos]
            pos += 1
        dg_ref = refs[pos]
        xf = x_ref[...].astype(F32)
        dyf = dy_ref[...].astype(F32)
        r = lax.rsqrt(jnp.mean(xf * xf, axis=1, keepdims=True) + EPS)
        xh = xf * r
        if need_dx:
            gy = dyf * g_ref[...]
            dx = r * (gy - xh * jnp.mean(gy * xh, axis=1, keepdims=True))
            if res_ref is not None:
                dx = dx + res_ref[...]
            dx_ref[...] = dx.astype(dx_ref.dtype)
        part = jnp.sum(dyf * xh, axis=0, keepdims=True)

        @pl.when(pl.program_id(0) == 0)
        def _():
            dg_ref[...] = part

        @pl.when(pl.program_id(0) > 0)
        def _():
            dg_ref[...] += part

    in_specs = [pl.BlockSpec((tr, width), lambda i: (i, cb)),
                pl.BlockSpec((1, width), lambda i: (0, 0)),
                pl.BlockSpec((tr, width), lambda i: (i, 0))]
    args = [x, g, dy]
    if res is not None:
        in_specs.append(pl.BlockSpec((tr, width), lambda i: (i, 0)))
        args.append(res)
    dg_shape, dg_spec = _sds((1, width), F32), pl.BlockSpec((1, width), lambda i: (0, 0))
    if need_dx:
        out_shape = (_sds((R, width), dx_dtype), dg_shape)
        out_specs = (pl.BlockSpec((tr, width), lambda i: (i, 0)), dg_spec)
    else:
        out_shape, out_specs = dg_shape, dg_spec
    out = _pcall(body, name=name, out_shape=out_shape, grid=(R // tr,), in_specs=in_specs,
                 out_specs=out_specs)(*args)
    return out if need_dx else (None, out)


HEAD_ROW_BLOCKS = (2048, 1024, 512, 256, 128)


def _rope_apply(y, c, sa, sb):
    return y * c + pltpu.roll(y, 96, 1) * sa + pltpu.roll(y, 32, 1) * sb


def _rope_transpose(dy, c, sa, sb):
    return dy * c + pltpu.roll(dy * sa, 32, 1) + pltpu.roll(dy * sb, 96, 1)


def rope_tables(seq):
    pos = jnp.arange(seq, dtype=F32)
    inv = ROPE_THETA ** (-jnp.arange(0, MLA_ROPE, 2, dtype=F32) / MLA_ROPE)
    ang = pos[:, None] * inv[None, :]
    cos, sin = jnp.cos(ang), jnp.sin(ang)
    z32, z64 = jnp.zeros_like(cos), jnp.zeros((seq, 64), F32)
    c = jnp.concatenate([cos, cos, z64], axis=1)
    sa = jnp.concatenate([-sin, z32, z64], axis=1)
    sb = jnp.concatenate([z32, sin, z64], axis=1)
    return c, sa, sb


def _head_vec(part_refs):
    xs = [p[...].astype(F32) for p in part_refs]
    return xs[0] if len(xs) == 1 else jnp.concatenate(xs, axis=1)


def prep_fwd(parts, g, *, name, n_heads, n_real, rope=None):
    rows = parts[0][0].shape[0]
    dh = sum(w for _, w, _ in parts)
    tr = _pick(rows, HEAD_ROW_BLOCKS)
    npart = len(parts)

    def body(*refs):
        part_refs, g_ref = refs[:npart], refs[npart]
        pos = npart + 1
        if rope is not None:
            c_ref, sa_ref, sb_ref = refs[pos:pos + 3]
            pos += 3
        o_ref = refs[pos]
        x = _head_vec(part_refs)
        r = lax.rsqrt(jnp.sum(x * x, axis=1, keepdims=True) * (1.0 / n_real) + EPS)
        y = x * r * g_ref[...]
        if rope is not None:
            yr = _rope_apply(y[:, dh - 128:], c_ref[...], sa_ref[...], sb_ref[...])
            y = jnp.concatenate([y[:, :dh - 128], yr], axis=1)
        o_ref[...] = y.astype(o_ref.dtype)

    in_specs, args = [], []
    for arr, w, fn in parts:
        in_specs.append(pl.BlockSpec((tr, w), functools.partial(lambda h, i, fn: (i, fn(h)), fn=fn)))
        args.append(arr)
    in_specs.append(pl.BlockSpec((1, dh), lambda h, i: (0, 0)))
    args.append(g)
    if rope is not None:
        for t in rope:
            in_specs.append(pl.BlockSpec((tr, 128), lambda h, i: (i, 0)))
            args.append(t)
    return _pcall(body, name=name, out_shape=_sds((n_heads, rows, dh), BF16), grid=(n_heads, rows // tr),
                  in_specs=in_specs, out_specs=pl.BlockSpec((None, tr, dh), lambda h, i: (h, i, 0)))(*args)


def _norm_bwd(x, g, dyn, n_real):
    r = lax.rsqrt(jnp.sum(x * x, axis=1, keepdims=True) * (1.0 / n_real) + EPS)
    xh = x * r
    gy = dyn * g
    dx = r * (gy - xh * (jnp.sum(gy * xh, axis=1, keepdims=True) * (1.0 / n_real)))
    return dx, jnp.sum(dyn * xh, axis=0, keepdims=True)


def prep_bwd_q(src, dy, g, *, name, n_heads, dh, n_real, rope=None, out_dtype=BF16):
    rows = src.shape[0]
    tr = _pick(rows, HEAD_ROW_BLOCKS)

    def body(*refs):
        x_ref, dy_ref, g_ref = refs[:3]
        pos = 3
        if rope is not None:
            c_ref, sa_ref, sb_ref = refs[pos:pos + 3]
            pos += 3
        dx_ref, dg_ref = refs[pos], refs[pos + 1]
        dyn = dy_ref[...].astype(F32)
        if rope is not None:
            dr = _rope_transpose(dyn[:, dh - 128:], c_ref[...], sa_ref[...], sb_ref[...])
            dyn = jnp.concatenate([dyn[:, :dh - 128], dr], axis=1)
        dx, dg = _norm_bwd(x_ref[...].astype(F32), g_ref[...], dyn, n_real)
        dx_ref[...] = dx.astype(dx_ref.dtype)
        first = jnp.logical_and(pl.program_id(0) == 0, pl.program_id(1) == 0)

        @pl.when(first)
        def _():
            dg_ref[...] = dg

        @pl.when(jnp.logical_not(first))
        def _():
            dg_ref[...] += dg

    in_specs = [pl.BlockSpec((tr, dh), lambda i, h: (i, h)),
                pl.BlockSpec((None, tr, dh), lambda i, h: (h, i, 0)),
                pl.BlockSpec((1, dh), lambda i, h: (0, 0))]
    args = [src, dy, g]
    if rope is not None:
        for t in rope:
            in_specs.append(pl.BlockSpec((tr, 128), lambda i, h: (i, 0)))
            args.append(t)
    return _pcall(body, name=name, out_shape=(_sds((rows, n_heads * dh), out_dtype), _sds((1, dh), F32)),
                  grid=(rows // tr, n_heads), in_specs=in_specs,
                  out_specs=(pl.BlockSpec((tr, dh), lambda i, h: (i, h)),
                             pl.BlockSpec((1, dh), lambda i, h: (0, 0))))(*args)


def prep_bwd_mla_k(kv_raw, zs, dkf, dv, g, rope, *, name):
    rows = kv_raw.shape[0]
    tr = _pick(rows, HEAD_ROW_BLOCKS)

    def body(kn_ref, kr_ref, dy_ref, dv_ref, g_ref, c_ref, sa_ref, sb_ref, dkv_ref, dkr_ref, dg_ref):
        h = pl.program_id(1)
        x = jnp.concatenate([kn_ref[...].astype(F32), kr_ref[...].astype(F32)], axis=1)
        dyn = dy_ref[...].astype(F32)
        dr = _rope_transpose(dyn[:, 128:], c_ref[...], sa_ref[...], sb_ref[...])
        dyn = jnp.concatenate([dyn[:, :128], dr], axis=1)
        dx, dg = _norm_bwd(x, g_ref[...], dyn, MLA_QK)
        dkv_ref[...] = jnp.concatenate([dx[:, :128], dv_ref[...].astype(F32)], axis=1).astype(dkv_ref.dtype)

        @pl.when(h == 0)
        def _():
            dkr_ref[...] = dx[:, 128:]

        @pl.when(h > 0)
        def _():
            dkr_ref[...] += dx[:, 128:]

        first = jnp.logical_and(pl.program_id(0) == 0, h == 0)

        @pl.when(first)
        def _():
            dg_ref[...] = dg

        @pl.when(jnp.logical_not(first))
        def _():
            dg_ref[...] += dg

    tab = pl.BlockSpec((tr, 128), lambda i, h: (i, 0))
    return _pcall(
        body, name=name,
        out_shape=(_sds((rows, N_HEADS * 256), BF16), _sds((rows, 128), F32), _sds((1, MLA_PAD), F32)),
        grid=(rows // tr, N_HEADS),
        in_specs=[pl.BlockSpec((tr, 128), lambda i, h: (i, 2 * h)),
                  pl.BlockSpec((tr, 128), lambda i, h: (i, Z_KR // 128)),
                  pl.BlockSpec((None, tr, MLA_PAD), lambda i, h: (h, i, 0)),
                  pl.BlockSpec((None, tr, 128), lambda i, h: (h, i, 0)),
                  pl.BlockSpec((1, MLA_PAD), lambda i, h: (0, 0)), tab, tab, tab],
        out_specs=(pl.BlockSpec((tr, 256), lambda i, h: (i, h)),
                   pl.BlockSpec((tr, 128), lambda i, h: (i, 0)),
                   pl.BlockSpec((1, MLA_PAD), lambda i, h: (0, 0))))(kv_raw, zs, dkf, dv, g, *rope)


def prep_bwd_groups(src, base_blk, dys, gs, *, name, n_heads, kinds, out_dtype=BF16):
    rows = src.shape[0]
    ng = len(kinds)
    J = ng * n_heads
    tr = _pick(rows, HEAD_ROW_BLOCKS)
    gstack = jnp.stack([gs[k] if kinds[k] == "norm" else jnp.ones((1, DH), F32) for k in range(ng)])

    def body(*refs):
        x_ref = refs[0]
        dy_refs = refs[1:1 + ng]
        g_ref, dx_ref, dg_ref = refs[1 + ng:4 + ng]
        j, i = pl.program_id(0), pl.program_id(1)
        grp = j // n_heads
        dy = dy_refs[0][...].astype(F32)
        for k in range(1, ng):
            dy = jnp.where(grp == k, dy_refs[k][...].astype(F32), dy)
        dx, dg = _norm_bwd(x_ref[...].astype(F32), g_ref[...], dy, DH)
        is_copy = functools.reduce(jnp.logical_or, [grp == k for k in range(ng) if kinds[k] == "copy"],
                                   jnp.bool_(False))
        dx_ref[...] = jnp.where(is_copy, dy, dx).astype(dx_ref.dtype)
        dg = jnp.where(is_copy, jnp.zeros_like(dg), dg)
        first = jnp.logical_and(j % n_heads == 0, i == 0)

        @pl.when(first)
        def _():
            dg_ref[...] = dg

        @pl.when(jnp.logical_not(first))
        def _():
            dg_ref[...] += dg

    in_specs = [pl.BlockSpec((tr, DH), lambda j, i: (i, base_blk + j))]
    for k in range(ng):
        in_specs.append(pl.BlockSpec(
            (None, tr, DH),
            functools.partial(lambda j, i, k: (jnp.clip(j - k * n_heads, 0, n_heads - 1),
                                               jnp.where(j // n_heads == k, i, 0), 0), k=k)))
    in_specs.append(pl.BlockSpec((None, 1, DH), lambda j, i: (j // n_heads, 0, 0)))
    return _pcall(body, name=name, out_shape=(_sds((rows, J * DH), out_dtype), _sds((ng, 1, DH), F32)),
                  grid=(J, rows // tr), in_specs=in_specs,
                  out_specs=(pl.BlockSpec((tr, DH), lambda j, i: (i, j)),
                             pl.BlockSpec((None, 1, DH), lambda j, i: (j // n_heads, 0, 0))))(src, *dys, gstack)


def _attn_cfg(mode, sq, sk):
    if mode == "chunk":
        tq = 128
        win = min((LEFT_CHUNKS + 2) * CHUNK, sk)
    else:
        tq = _pick(sq, (256, 128))
        win = sk
    scale = (MLA_QK if mode == "mla" else DH) ** -0.5
    return tq, win, scale


def _attn_key_rows(mode, i, tq, win, sk, run):
    if mode == "chunk":
        start = pl.multiple_of(jnp.clip((i - LEFT_CHUNKS // 2) * 128, 0, sk - win), 128)
        run(pl.ds(start, win), start)
    elif mode == "cross":
        run(slice(0, sk), 0)
    else:
        lax.switch(i, [functools.partial(run, slice(0, (b + 1) * tq), 0) for b in range(sk // tq)])


def _attn_scores(mode, i, tq, scale, q, kk, start, cq, ck, t_ref):
    nk = kk.shape[0]
    s = lax.dot_general(q, kk, (((1,), (1,)), ((), ())), preferred_element_type=F32) * scale
    if mode == "cross":
        return s
    t_pos = i * tq + lax.broadcasted_iota(jnp.int32, (tq, nk), 0)
    s_pos = start + lax.broadcasted_iota(jnp.int32, (tq, nk), 1)
    if mode == "fox":
        s = s + cq - ck
        allowed = s_pos <= t_pos
    else:
        qc, kc = lax.shift_right_logical(t_pos, CHUNK_SHIFT), lax.shift_right_logical(s_pos, CHUNK_SHIFT)
        allowed = kc <= qc
        if mode == "chunk":
            allowed = jnp.logical_and(allowed, kc >= qc - LEFT_CHUNKS)
            tiles = []
            for w in range(nk // 128):
                delta = i - (start // 128 + w)
                tiles.append(jnp.where(delta == 0, t_ref[0], jnp.where(delta == 1, t_ref[1], t_ref[2])))
            s = s + jnp.concatenate(tiles, axis=1)
    return jnp.where(allowed, s, NEG)


def attn_fwd(q, k, v_arr, v_blk, *, mode, name, n_heads, cq=None, ck=None, tiles=None):
    _, sq, dk = q.shape
    sk = k.shape[1]
    tq, win, scale = _attn_cfg(mode, sq, sk)

    def body(*refs):
        q_ref, k_ref, v_ref = refs[:3]
        pos = 3
        cq_ref = ck_ref = t_ref = None
        if mode == "fox":
            cq_ref, ck_ref = refs[pos:pos + 2]
            pos += 2
        if mode == "chunk":
            t_ref = refs[pos]
            pos += 1
        o_ref, lse_ref = refs[pos], refs[pos + 1]
        i = pl.program_id(1)

        def run(rows, start):
            cq, ck = (cq_ref[...], ck_ref[:, rows]) if mode == "fox" else (None, None)
            s = _attn_scores(mode, i, tq, scale, q_ref[...], k_ref[rows, :], start, cq, ck, t_ref)
            m = jnp.max(s, axis=1, keepdims=True)
            e = jnp.exp(s - m)
            l = jnp.sum(e, axis=1, keepdims=True)
            p = e * (1.0 / l)
            o_ref[...] = jnp.dot(_mx(p), _mx(v_ref[rows, :]), preferred_element_type=F32).astype(o_ref.dtype)
            lse_ref[...] = m + jnp.log(l)

        _attn_key_rows(mode, i, tq, win, sk, run)

    in_specs = [pl.BlockSpec((None, tq, dk), lambda h, i: (h, i, 0)),
                pl.BlockSpec((None, sk, dk), lambda h, i: (h, 0, 0)),
                pl.BlockSpec((sk, DH), lambda h, i: (0, v_blk(h)))]
    args = [q, k, v_arr]
    if mode == "fox":
        in_specs += [pl.BlockSpec((None, tq, 1), lambda h, i: (h, i, 0)),
                     pl.BlockSpec((None, 1, sk), lambda h, i: (h, 0, 0))]
        args += [cq, ck]
    if mode == "chunk":
        in_specs.append(pl.BlockSpec((3, None, 128, 128), lambda h, i: (0, h, 0, 0)))
        args.append(tiles)
    return _pcall(body, name=name,
                  out_shape=(_sds((sq, n_heads * DH), BF16), _sds((n_heads, sq, 1), F32)),
                  grid=(n_heads, sq // tq), in_specs=in_specs,
                  out_specs=(pl.BlockSpec((tq, DH), lambda h, i: (i, h)),
                             pl.BlockSpec((None, tq, 1), lambda h, i: (h, i, 0))))(*args)


def attn_bwd(q, k, v_arr, v_blk, o, do, lse, *, mode, name, n_heads, cq=None, ck=None, tiles=None):
    _, sq, dk = q.shape
    sk = k.shape[1]
    tq, win, scale = _attn_cfg(mode, sq, sk)
    n_extra = {"fox": 2, "chunk": 1}.get(mode, 0)

    def body(*refs):
        q_ref, k_ref, v_ref, o_ref, do_ref, lse_ref = refs[:6]
        pos = 6
        cq_ref = ck_ref = t_ref = None
        if mode == "fox":
            cq_ref, ck_ref = refs[pos:pos + 2]
            pos += 2
        if mode == "chunk":
            t_ref = refs[pos]
            pos += 1
        dq_ref, dk_ref, dv_ref = refs[pos:pos + 3]
        extra = refs[pos + 3:pos + 3 + n_extra]
        i = pl.program_id(1)

        @pl.when(i == 0)
        def _():
            dk_ref[...] = jnp.zeros_like(dk_ref)
            dv_ref[...] = jnp.zeros_like(dv_ref)
            if mode == "fox":
                extra[1][...] = jnp.zeros_like(extra[1])
            if mode == "chunk":
                extra[0][...] = jnp.zeros_like(extra[0])

        def run(rows, start):
            q = q_ref[...]
            do = do_ref[...]
            kk = k_ref[rows, :]
            cq, ck = (cq_ref[...], ck_ref[:, rows]) if mode == "fox" else (None, None)
            s = _attn_scores(mode, i, tq, scale, q, kk, start, cq, ck, t_ref)
            p = jnp.exp(s - lse_ref[...])
            drow = jnp.sum(do.astype(F32) * o_ref[...].astype(F32), axis=1, keepdims=True)
            dp = lax.dot_general(do, _mx(v_ref[rows, :]), (((1,), (1,)), ((), ())), preferred_element_type=F32)
            ds = p * (dp - drow)
            dsb = _mx(ds)
            dq_ref[...] = (jnp.dot(dsb, kk, preferred_element_type=F32) * scale).astype(dq_ref.dtype)
            dk_ref[rows, :] += lax.dot_general(dsb, q, (((0,), (0,)), ((), ())),
                                               preferred_element_type=F32) * scale
            dv_ref[rows, :] += lax.dot_general(_mx(p), do, (((0,), (0,)), ((), ())), preferred_element_type=F32)
            if mode == "chunk":
                dt_ref = extra[0]
                for w in range(win // 128):
                    delta = i - (start // 128 + w)
                    tile = ds[:, w * 128:(w + 1) * 128]
                    zero = jnp.zeros_like(tile)
                    dt_ref[0] += jnp.where(delta == 0, tile, zero)
                    dt_ref[1] += jnp.where(delta == 1, tile, zero)
                    dt_ref[2] += jnp.where(delta >= 2, tile, zero)
            if mode == "fox":
                extra[0][...] = jnp.sum(ds, axis=1, keepdims=True)
                extra[1][:, rows] -= jnp.sum(ds, axis=0, keepdims=True)

        _attn_key_rows(mode, i, tq, win, sk, run)

    in_specs = [pl.BlockSpec((None, tq, dk), lambda h, i: (h, i, 0)),
                pl.BlockSpec((None, sk, dk), lambda h, i: (h, 0, 0)),
                pl.BlockSpec((sk, DH), lambda h, i: (0, v_blk(h))),
                pl.BlockSpec((tq, DH), lambda h, i: (i, h)),
                pl.BlockSpec((tq, DH), lambda h, i: (i, h)),
                pl.BlockSpec((None, tq, 1), lambda h, i: (h, i, 0))]
    args = [q, k, v_arr, o, do, lse]
    out_shape = [_sds((n_heads, sq, dk), F32), _sds((n_heads, sk, dk), F32), _sds((n_heads, sk, DH), F32)]
    out_specs = [pl.BlockSpec((None, tq, dk), lambda h, i: (h, i, 0)),
                 pl.BlockSpec((None, sk, dk), lambda h, i: (h, 0, 0)),
                 pl.BlockSpec((None, sk, DH), lambda h, i: (h, 0, 0))]
    if mode == "fox":
        in_specs += [pl.BlockSpec((None, tq, 1), lambda h, i: (h, i, 0)),
                     pl.BlockSpec((None, 1, sk), lambda h, i: (h, 0, 0))]
        args += [cq, ck]
        out_shape += [_sds((n_heads, sq, 1), F32), _sds((n_heads, 1, sk), F32)]
        out_specs += [pl.BlockSpec((None, tq, 1), lambda h, i: (h, i, 0)),
                      pl.BlockSpec((None, 1, sk), lambda h, i: (h, 0, 0))]
    if mode == "chunk":
        in_specs.append(pl.BlockSpec((3, None, 128, 128), lambda h, i: (0, h, 0, 0)))
        args.append(tiles)
        out_shape.append(_sds((3, n_heads, 128, 128), F32))
        out_specs.append(pl.BlockSpec((3, None, 128, 128), lambda h, i: (0, h, 0, 0)))
    return _pcall(body, name=name, out_shape=tuple(out_shape), grid=(n_heads, sq // tq),
                  in_specs=in_specs, out_specs=tuple(out_specs))(*args)


REL_LANES = 384
REL_KBLK = 2048


def _rel_onehot(t, k):
    rho = k * REL_KBLK + lax.broadcasted_iota(jnp.int32, (REL_KBLK, REL_LANES), 0)
    lane = lax.broadcasted_iota(jnp.int32, (REL_KBLK, REL_LANES), 1)
    diff = lax.shift_right_logical(rho, 7) - jnp.bitwise_and(rho, 127)
    idx = jnp.where(t == 0, diff + REL_CLIP,
                    jnp.where(t == 1, jnp.minimum(diff + 128, REL_CLIP) + REL_CLIP, N_REL - 1))
    return jnp.where(idx == lane, 1.0, 0.0).astype(F32)


def _split3(x):
    hi = x.astype(BF16)
    rest = x - hi.astype(F32)
    mid = rest.astype(BF16)
    return hi, mid, (rest - mid.astype(F32)).astype(BF16)


def relbias_onehot():
    def body(o_ref):
        o_ref[...] = _rel_onehot(pl.program_id(0), pl.program_id(1)).astype(o_ref.dtype)

    return _pcall(body, name="relbias_onehot", out_shape=_sds((3, 128 * 128, REL_LANES), BF16),
                  grid=(3, 128 * 128 // REL_KBLK), in_specs=[],
                  out_specs=pl.BlockSpec((None, REL_KBLK, REL_LANES), lambda t, k: (t, k, 0)))()


ONEHOT_SPEC = pl.BlockSpec((None, REL_KBLK, REL_LANES), lambda t, k: (t, k, 0))


def relbias_tiles(rel_bias, onehot):
    nh = rel_bias.shape[0]
    rb = jnp.pad(rel_bias, ((0, 0), (0, REL_LANES - N_REL)))

    def body(rb_ref, e_ref, o_ref):
        e = e_ref[...]
        hi, mid, lo = [lax.dot_general(t, e, (((1,), (1,)), ((), ())), preferred_element_type=F32)
                       for t in _split3(rb_ref[...])]
        o_ref[...] = (hi + mid) + lo

    flat = _pcall(body, name="relbias_tiles", out_shape=_sds((3, nh, 128 * 128), F32),
                  grid=(3, 128 * 128 // REL_KBLK),
                  in_specs=[pl.BlockSpec((nh, REL_LANES), lambda t, k: (0, 0)), ONEHOT_SPEC],
                  out_specs=pl.BlockSpec((None, nh, REL_KBLK), lambda t, k: (t, 0, k)))(rb, onehot)
    return flat.reshape(3, nh, 128, 128)


def relbias_tiles_bwd(dtiles, onehot):
    nh = dtiles.shape[1]

    def body(dt_ref, e_ref, o_ref):
        t, k = pl.program_id(0), pl.program_id(1)
        e = e_ref[...]
        hi, mid, lo = [jnp.dot(t_, e, preferred_element_type=F32) for t_ in _split3(dt_ref[...])]
        part = (hi + mid) + lo
        first = jnp.logical_and(t == 0, k == 0)

        @pl.when(first)
        def _():
            o_ref[...] = part

        @pl.when(jnp.logical_not(first))
        def _():
            o_ref[...] += part

    out = _pcall(body, name="relbias_tiles_bwd", out_shape=_sds((nh, REL_LANES), F32),
                 grid=(3, 128 * 128 // REL_KBLK),
                 in_specs=[pl.BlockSpec((None, nh, REL_KBLK), lambda t, k: (t, 0, k)), ONEHOT_SPEC],
                 out_specs=pl.BlockSpec((nh, REL_LANES), lambda t, k: (0, 0)))(
                     dtiles.reshape(3, nh, 128 * 128), onehot)
    return out[:, :N_REL]


CUM_BLK = 256


def _tri(n, lower):
    r = lax.broadcasted_iota(jnp.int32, (n, n), 0)
    c = lax.broadcasted_iota(jnp.int32, (n, n), 1)
    return jnp.where(r >= c if lower else r <= c, 1.0, 0.0).astype(F32)


def fox_cum_fwd(zs, bf):
    S = zs.shape[0]
    tb = min(CUM_BLK, S)

    def body(f_ref, b_ref, cum_ref, cumt_ref, carry_ref):
        @pl.when(pl.program_id(0) == 0)
        def _():
            carry_ref[...] = jnp.zeros_like(carry_ref)

        x = f_ref[...] + b_ref[...]
        lane = lax.broadcasted_iota(jnp.int32, x.shape, 1)
        logf = jnp.where(lane < N_HEADS, jnp.minimum(x, 0.0) - jnp.log(1.0 + jnp.exp(-jnp.abs(x))), 0.0)
        cum = jnp.dot(_tri(tb, True), logf, preferred_element_type=F32,
                      precision=lax.Precision.HIGHEST) + carry_ref[...]
        carry_ref[...] = cum[tb - 1:tb, :]
        cum_ref[...] = cum
        cumt_ref[...] = cum.T

    return _pcall(body, name="fox_cum_fwd", out_shape=(_sds((S, 128), F32), _sds((128, S), F32)),
                  grid=(S // tb,),
                  in_specs=[pl.BlockSpec((tb, 128), lambda i: (i, Z_FF // 128)),
                            pl.BlockSpec((1, 128), lambda i: (0, 0))],
                  out_specs=(pl.BlockSpec((tb, 128), lambda i: (i, 0)),
                             pl.BlockSpec((128, tb), lambda i: (0, i))),
                  scratch=(pltpu.VMEM((1, 128), F32),))(zs, bf)


def fox_cum_bwd(zs, bf, dcum):
    S = zs.shape[0]
    tb = min(CUM_BLK, S)
    nblk = S // tb

    def body(f_ref, b_ref, d_ref, df_ref, db_ref, carry_ref):
        @pl.when(pl.program_id(0) == 0)
        def _():
            carry_ref[...] = jnp.zeros_like(carry_ref)
            db_ref[...] = jnp.zeros_like(db_ref)

        d = d_ref[...]
        dlogf = jnp.dot(_tri(tb, False), d, preferred_element_type=F32,
                        precision=lax.Precision.HIGHEST) + carry_ref[...]
        carry_ref[...] += jnp.sum(d, axis=0, keepdims=True)
        x = f_ref[...] + b_ref[...]
        lane = lax.broadcasted_iota(jnp.int32, x.shape, 1)
        dx = jnp.where(lane < N_HEADS, dlogf / (1.0 + jnp.exp(x)), 0.0)
        df_ref[...] = dx
        db_ref[...] += jnp.sum(dx, axis=0, keepdims=True)

    return _pcall(body, name="fox_cum_bwd", out_shape=(_sds((S, 128), F32), _sds((1, 128), F32)),
                  grid=(nblk,),
                  in_specs=[pl.BlockSpec((tb, 128), lambda i: (nblk - 1 - i, Z_FF // 128)),
                            pl.BlockSpec((1, 128), lambda i: (0, 0)),
                            pl.BlockSpec((tb, 128), lambda i: (nblk - 1 - i, 0))],
                  out_specs=(pl.BlockSpec((tb, 128), lambda i: (nblk - 1 - i, 0)),
                             pl.BlockSpec((1, 128), lambda i: (0, 0))),
                  scratch=(pltpu.VMEM((1, 128), F32),))(zs, bf, dcum)


def _sigmoid(x):
    return 1.0 / (1.0 + jnp.exp(-x))


def merge_fwd(z, projs):
    S = z.shape[0]
    tr, tc = _pick(S, (512, 256, 128)), 512
    nbc = D_MODEL // tc
    g0 = Z_GATE // tc

    def body(g0_ref, g1_ref, g2_ref, p0_ref, p1_ref, p2_ref, o_ref):
        acc = _sigmoid(g0_ref[...]) * p0_ref[...]
        acc += _sigmoid(g1_ref[...]) * p1_ref[...]
        acc += _sigmoid(g2_ref[...]) * p2_ref[...]
        o_ref[...] = acc.astype(o_ref.dtype)

    gspecs = [pl.BlockSpec((tr, tc), functools.partial(lambda i, j, n: (i, g0 + n * nbc + j), n=n))
              for n in range(3)]
    pspec = pl.BlockSpec((tr, tc), lambda i, j: (i, j))
    return _pcall(body, name="merge_fwd", out_shape=_sds((S, D_MODEL), BF16), grid=(S // tr, nbc),
                  in_specs=gspecs + [pspec] * 3, out_specs=pspec)(z, z, z, *projs)


def merge_bwd(z, projs, dmerged):
    S = z.shape[0]
    tr, tc = _pick(S, (512, 256, 128)), 512
    nbc = D_MODEL // tc
    g0 = Z_GATE // tc

    def body(g0_ref, g1_ref, g2_ref, p0_ref, p1_ref, p2_ref, dm_ref, dg0, dg1, dg2, dp0, dp1, dp2):
        dm = dm_ref[...]
        for g_ref, p_ref, dg_ref, dp_ref in ((g0_ref, p0_ref, dg0, dp0), (g1_ref, p1_ref, dg1, dp1),
                                             (g2_ref, p2_ref, dg2, dp2)):
            sg = _sigmoid(g_ref[...])
            dp_ref[...] = (dm * sg).astype(dp_ref.dtype)
            dg_ref[...] = (dm * p_ref[...] * sg * (1.0 - sg)).astype(dg_ref.dtype)

    gspecs = [pl.BlockSpec((tr, tc), functools.partial(lambda i, j, n: (i, g0 + n * nbc + j), n=n))
              for n in range(3)]
    pspec = pl.BlockSpec((tr, tc), lambda i, j: (i, j))
    out = _pcall(body, name="merge_bwd", out_shape=tuple(_sds((S, D_MODEL), BF16) for _ in range(6)),
                 grid=(S // tr, nbc), in_specs=gspecs + [pspec] * 4,
                 out_specs=tuple([pspec] * 6))(z, z, z, *projs, dmerged)
    return out[:3], out[3:]


def loss_head(y, target):
    S, D = y.shape
    tr = _pick(S, (256, 128))

    def body(y_ref, t_ref, dy_ref, l_ref):
        e = y_ref[...] - t_ref[...]
        dy_ref[...] = e * (1.0 / D)
        part = jnp.sum(jnp.sum(e * e, axis=1, keepdims=True), axis=0, keepdims=True) * (0.5 / D)

        @pl.when(pl.program_id(0) == 0)
        def _():
            l_ref[...] = part

        @pl.when(pl.program_id(0) > 0)
        def _():
            l_ref[...] += part

    spec = pl.BlockSpec((tr, D), lambda i: (i, 0))
    return _pcall(body, name="loss_head", out_shape=(_sds((S, D), F32), _sds((1, 1), F32)), grid=(S // tr,),
                  in_specs=[spec, spec], out_specs=(spec, pl.BlockSpec((1, 1), lambda i: (0, 0))))(y, target)


def _adamw_update(w, g, m, v):
    nm = ADAM_B1 * m + (1.0 - ADAM_B1) * g
    nv = ADAM_B2 * v + (1.0 - ADAM_B2) * (g * g)
    delta = -ADAM_LR * ((nm / (1.0 - ADAM_B1 ** ADAM_STEP)) / (jnp.sqrt(nv / (1.0 - ADAM_B2 ** ADAM_STEP)) + ADAM_EPS)
                        + ADAM_WD * w)
    return delta, nm, nv


ADAMW_BLOCK_BYTES = 1024 * 1024


def adamw(w, gs, m, v, *, name):
    L, R, C = w.shape
    tr = _pick(R, (512, 256, 128, 64, 32, 16, 8))
    while tr * C * 4 > ADAMW_BLOCK_BYTES and tr % 16 == 0:
        tr //= 2

    def body(*refs):
        w_ref, m_ref, v_ref = refs[:3]
        g_refs = refs[3:3 + L]
        go_ref, d_ref, nm_ref, nv_ref = refs[3 + L:]
        l = pl.program_id(0)
        g_ = g_refs[0][...]
        for k in range(1, L):
            g_ = jnp.where(l == k, g_refs[k][...], g_)
        go_ref[...] = g_
        d_ref[...], nm_ref[...], nv_ref[...] = _adamw_update(w_ref[...], g_, m_ref[...], v_ref[...])

    spec = pl.BlockSpec((None, tr, C), lambda l, i: (l, i, 0))
    gspecs = [pl.BlockSpec((tr, C), functools.partial(lambda l, i, k: (jnp.where(l == k, i, 0), 0), k=k))
              for k in range(L)]
    return _pcall(body, name=name, out_shape=tuple(_sds((L, R, C), F32) for _ in range(4)),
                  grid=(L, R // tr), in_specs=[spec] * 3 + gspecs, out_specs=(spec,) * 4)(w, m, v, *gs)


SHARD_SHAPES = {
    "w_uq": (Q_LORA, 384), "w_ukv": (KV_LORA, 512), "w_br": (3, MIX_W, 512), "w_out": (512, D_MODEL),
    "w_xq": (512, 512), "w_xkv": (512, 1024), "w_xo": (512, 512), "w_1": (D_MODEL, 2048),
    "w_2": (2048, D_MODEL), "w_in": (D_MODEL, W_IN_SHARD),
}


def _rows2d(a, name):
    shp = SHARD_SHAPES[name]
    return a.reshape(a.shape[:a.ndim - len(shp)] + (-1, shp[-1]))


def _cols_from_shards(g):
    return jnp.transpose(g, (1, 0, 2)).reshape(g.shape[1], 4 * g.shape[2])


def _cols_to_shards(w):
    return jnp.transpose(w.reshape(w.shape[0], 4, w.shape[1] // 4), (1, 0, 2))


EARLY_WEIGHTS = ("w_in", "w_uq", "w_ukv")
MID_WEIGHTS = ("w_br", "w_out", "w_xq", "w_xkv", "w_xo")
LAST_WEIGHTS = ("w_1", "w_2")
LATE_WEIGHTS = MID_WEIGHTS + LAST_WEIGHTS
assert sorted(EARLY_WEIGHTS + LATE_WEIGHTS) == sorted(PACK_ORDER)


def _w_in_relaid(g_in):
    zeros = lambda n: [jnp.zeros((D_MODEL, n), g_in.dtype)] if n else []
    segs, at = [], 0
    for p0, o0, w in sorted(W_IN_PIECES):
        segs += zeros(p0 - at)
        while w > 0:
            s_, a = divmod(o0, W_IN_SHARD)
            take = min(w, W_IN_SHARD - a)
            segs.append(g_in[s_][:, a:a + take])
            o0, p0, w = o0 + take, p0 + take, w - take
        at = p0
    return jnp.concatenate(segs + zeros(Z_TOT - at), axis=1)


def full_weights(g):
    forms = {
        "w_in": lambda a: {"in_p": _w_in_relaid(a)[None]},
        "w_uq": lambda a: {"uq_p": jnp.pad(_cols_from_shards(a).reshape(Q_LORA, N_HEADS, MLA_QK),
                                           ((0, 0), (0, 0), (0, MLA_PAD - MLA_QK))
                                           ).reshape(1, Q_LORA, N_HEADS * MLA_PAD)},
        "w_ukv": lambda a: {"ukv": a},
        "w_br": lambda a: {"br": [a[:, n] for n in range(3)]},
        "w_out": lambda a: {"out": a.reshape(1, D_MODEL, D_MODEL)},
        "w_xq": lambda a: {"xq": a.reshape(1, D_MODEL, 512)},
        "w_xkv": lambda a: {"xkv": a.reshape(1, D_MODEL, 1024)},
        "w_xo": lambda a: {"xo": a},
        "w_1": lambda a: {"w1": a},
        "w_2": lambda a: {"w2": a.reshape(1, D_FF, D_MODEL)},
    }
    out = {}
    for n, a in g.items():
        out.update(forms[n](a))
    return out


def _w_in_grad_shards(dp):
    d_in = []
    for s_ in range(4):
        lo, hi, segs = s_ * W_IN_SHARD, (s_ + 1) * W_IN_SHARD, []
        for o0, p0, w in sorted((o0, p0, w) for p0, o0, w in W_IN_PIECES):
            a, b = max(lo, o0), min(hi, o0 + w)
            if a < b:
                segs.append(dp[:, p0 + a - o0:p0 + b - o0])
        d_in.append(jnp.concatenate(segs, axis=1))
    return jnp.stack(d_in)


def shard_grads(dw):
    forms = {
        "in_p": lambda a: {"w_in": _w_in_grad_shards(a[0])},
        "uq_p": lambda a: {"w_uq": _cols_to_shards(
            a.reshape(Q_LORA, N_HEADS, MLA_PAD)[:, :, :MLA_QK].reshape(Q_LORA, N_HEADS * MLA_QK))},
        "ukv": lambda a: {"w_ukv": a},
        "br": lambda a: {"w_br": jnp.stack(a, axis=1)},
        "out": lambda a: {"w_out": a.reshape(4, 512, D_MODEL)},
        "xq": lambda a: {"w_xq": a.reshape(4, 512, 512)},
        "xkv": lambda a: {"w_xkv": a.reshape(4, 512, 1024)},
        "xo": lambda a: {"w_xo": a},
        "w1": lambda a: {"w_1": a},
        "w2": lambda a: {"w_2": a.reshape(4, 2048, D_MODEL)},
    }
    out = {}
    for k, a in dw.items():
        out.update(forms[k](a))
    return out


def layer_params(d, l):
    row = lambda v: v.reshape(1, -1).astype(F32)
    padto = lambda v, n: jnp.pad(row(v), ((0, 0), (0, n - v.shape[-1])))
    return dict(
        g_mix=row(d["g_mix"][l]), g_cq=row(d["g_cq"][l]), g_ckv=row(d["g_ckv"][l]),
        g_mla_q=padto(d["g_mla_q"][l], MLA_PAD), g_mla_k=padto(d["g_mla_k"][l], MLA_PAD),
        b_f=padto(d["b_f"][l], 128), g_fox_q=row(d["g_fox_q"][l]), g_fox_k=row(d["g_fox_k"][l]),
        rel_bias=d["rel_bias"][l].astype(F32), g_ch_q=row(d["g_ch_q"][l]), g_ch_k=row(d["g_ch_k"][l]),
        g_cross=row(d["g_cross"][l]), g_mem=row(d["g_mem"][l]), g_x_q=row(d["g_x_q"][l]),
        g_x_k=row(d["g_x_k"][l]), g_mlp=row(d["g_mlp"][l]))


FOX_B = Z_FOX // 128
CH_B = Z_CH // 128


def _after(small, token):
    return small if token is None else small + token[:1, :1].reshape((1,) * small.ndim)


def layer_fwd(x, mem, W, P, rope, mid_weights, last_weights):
    S = x.shape[0]
    s = {}
    s["h"] = rms_fwd(x, P["g_mix"], name="rms_d")
    z = s["z"] = mm_nn(s["h"], W["in_p"], name="mm_in", out_dtype=F32)
    s["cq_n"] = rms_fwd(z, P["g_cq"], col0=Z_CQ, width=Q_LORA, name="rms_cq")
    s["ckv_n"] = rms_fwd(z, P["g_ckv"], col0=Z_CKV, width=KV_LORA, name="rms_ckv")
    s["q_raw"] = mm_nn(s["cq_n"], W["uq_p"], name="mm_uq", out_dtype=F32)
    s["kv_raw"] = mm_nn(s["ckv_n"], W["ukv"], name="mm_ukv", out_dtype=F32)
    s["qa"] = prep_fwd([(s["q_raw"], MLA_PAD, lambda h: h)], P["g_mla_q"], name="prep_mla_q",
                       n_heads=N_HEADS, n_real=MLA_QK, rope=rope)
    s["ka"] = prep_fwd([(s["kv_raw"], 128, lambda h: 2 * h), (z, 128, lambda h: Z_KR // 128)], P["g_mla_k"],
                       name="prep_mla_k", n_heads=N_HEADS, n_real=MLA_QK, rope=rope)
    s["o_a"], s["lse_a"] = attn_fwd(s["qa"], s["ka"], s["kv_raw"], lambda h: 2 * h + 1, mode="mla",
                                    name="attn_mla", n_heads=N_HEADS)
    s["qb"] = prep_fwd([(z, DH, lambda h: FOX_B + h)], P["g_fox_q"], name="prep_h", n_heads=N_HEADS, n_real=DH)
    s["kb"] = prep_fwd([(z, DH, lambda h: FOX_B + N_HEADS + h)], P["g_fox_k"], name="prep_h",
                       n_heads=N_HEADS, n_real=DH)
    _, cum_t = fox_cum_fwd(z, P["b_f"])
    s["cq"] = cum_t[:N_HEADS].reshape(N_HEADS, S, 1)
    s["ck"] = cum_t[:N_HEADS].reshape(N_HEADS, 1, S)
    s["o_b"], s["lse_b"] = attn_fwd(s["qb"], s["kb"], z, lambda h: FOX_B + 2 * N_HEADS + h, mode="fox",
                                    name="attn_fox", n_heads=N_HEADS, cq=s["cq"], ck=s["ck"])
    s["qc"] = prep_fwd([(z, DH, lambda h: CH_B + h)], P["g_ch_q"], name="prep_h", n_heads=N_HEADS, n_real=DH)
    s["kc"] = prep_fwd([(z, DH, lambda h: CH_B + N_HEADS + h)], P["g_ch_k"], name="prep_h",
                       n_heads=N_HEADS, n_real=DH)
    s["tiles"] = relbias_tiles(P["rel_bias"], P["rel_onehot"])
    s["o_c"], s["lse_c"] = attn_fwd(s["qc"], s["kc"], z, lambda h: CH_B + 2 * N_HEADS + h, mode="chunk",
                                    name="attn_chunk", n_heads=N_HEADS, tiles=s["tiles"])
    mid, token = mid_weights(s["o_c"])
    W = dict(W, **mid)
    P = dict(P, g_cross=_after(P["g_cross"], token))
    s["projs"] = [mm_nn(o, W["br"][n], name="mm_br", out_dtype=F32)
                  for n, o in enumerate((s["o_a"], s["o_b"], s["o_c"]))]
    s["merged"] = merge_fwd(z, s["projs"])
    x1 = s["x1"] = mm_nn(s["merged"], W["out"], name="mm_out", out_dtype=F32, res=x)
    s["hq"] = rms_fwd(x1, P["g_cross"], name="rms_d")
    s["xq_raw"] = mm_nn(s["hq"], W["xq"], name="mm_xq", out_dtype=F32)
    s["mem_n"] = rms_fwd(mem, P["g_mem"], name="rms_mem")
    s["mkv"] = mm_nn(s["mem_n"], W["xkv"], name="mm_xkv", out_dtype=F32)
    s["qx"] = prep_fwd([(s["xq_raw"], DH, lambda h: h)], P["g_x_q"], name="prep_xq", n_heads=X_HEADS, n_real=DH)
    s["kx"] = prep_fwd([(s["mkv"], DH, lambda h: h)], P["g_x_k"], name="prep_xk", n_heads=X_HEADS, n_real=DH)
    s["o_x"], s["lse_x"] = attn_fwd(s["qx"], s["kx"], s["mkv"], lambda h: X_HEADS + h, mode="cross",
                                    name="attn_cross", n_heads=X_HEADS)
    x2 = s["x2"] = mm_nn(s["o_x"], W["xo"], name="mm_xo", out_dtype=F32, res=x1)
    s["hm"] = rms_fwd(x2, P["g_mlp"], name="rms_d")
    last, _ = last_weights(s["hm"])
    W = dict(W, **last)
    s["a1"], s["act"] = mm_nn(s["hm"], W["w1"], name="mm_w1", out_dtype=BF16, relu2=True)
    x3 = mm_nn(s["act"], W["w2"], name="mm_w2", out_dtype=F32, res=x2)
    return x3, s, W


LATE_GRADS = ("w2", "w1", "xo", "xq", "xkv", "out", "br")


def layer_bwd(g, x, mem, W, P, rope, s, grads_done):
    S = x.shape[0]
    z = s["z"]
    dw, ds = {}, {}
    da1 = mm_nt(g, W["w2"], name="mm_w2_dx", out_dtype=BF16, relu_mul=s["a1"])
    dw["w2"] = mm_tn(s["act"], g, nb=1, name="mm_w2_dw", out_dtype=BF16)
    dhm = mm_nt(da1, W["w1"], name="mm_w1_dx", out_dtype=F32)
    dw["w1"] = mm_tn(s["hm"], da1, nb=4, name="mm_w1_dw", out_dtype=BF16)
    g2, ds["g_mlp"] = rms_bwd(s["x2"], P["g_mlp"], dhm, name="rms_d_bwd", res=g)
    do_x = mm_nt(g2, W["xo"], name="mm_xo_dx", out_dtype=BF16)
    dw["xo"] = mm_tn(s["o_x"], g2, nb=4, name="mm_xo_dw", out_dtype=BF16)
    dqx, dkx, dvx = attn_bwd(s["qx"], s["kx"], s["mkv"], lambda h: X_HEADS + h, s["o_x"], do_x, s["lse_x"],
                             mode="cross", name="attn_cross_bwd", n_heads=X_HEADS)
    dxq_raw, ds["g_x_q"] = prep_bwd_q(s["xq_raw"], dqx, P["g_x_q"], name="prep_xq_bwd", n_heads=X_HEADS,
                                      dh=DH, n_real=DH)
    dmkv, dgk = prep_bwd_groups(s["mkv"], 0, [dkx, dvx], [P["g_x_k"], None], name="prep_xkv_bwd",
                                n_heads=X_HEADS, kinds=("norm", "copy"))
    ds["g_x_k"] = dgk[0]
    dhq = mm_nt(dxq_raw, W["xq"], name="mm_xq_dx", out_dtype=F32)
    dw["xq"] = mm_tn(s["hq"], dxq_raw, nb=1, name="mm_xq_dw", out_dtype=BF16)
    dmem_n = mm_nt(dmkv, W["xkv"], name="mm_xkv_dx", out_dtype=F32)
    dw["xkv"] = mm_tn(s["mem_n"], dmkv, nb=1, name="mm_xkv_dw", out_dtype=BF16)
    _, ds["g_mem"] = rms_bwd(mem, P["g_mem"], dmem_n, name="rms_mem_bwd", need_dx=False)
    g1, ds["g_cross"] = rms_bwd(s["x1"], P["g_cross"], dhq, name="rms_d_bwd", res=g2)
    dmerged = mm_nt(g1, W["out"], name="mm_out_dx", out_dtype=F32)
    dw["out"] = mm_tn(s["merged"], g1, nb=1, name="mm_out_dw", out_dtype=BF16)
    dgl, dproj = merge_bwd(z, s["projs"], dmerged)
    outs = (s["o_a"], s["o_b"], s["o_c"])
    do = [mm_nt(dproj[n], W["br"][n], name="mm_br_dx", out_dtype=BF16) for n in range(3)]
    dw["br"] = [mm_tn(outs[n], dproj[n], nb=4, name="mm_br_dw", out_dtype=BF16) for n in range(3)]
    token = grads_done("late", {k: dw.pop(k) for k in LATE_GRADS})
    dqc, dkc, dvc, dtiles = attn_bwd(s["qc"], s["kc"], z, lambda h: CH_B + 2 * N_HEADS + h, s["o_c"], do[2],
                                     s["lse_c"], mode="chunk", name="attn_chunk_bwd", n_heads=N_HEADS,
                                     tiles=_after(s["tiles"], token))
    d_ch, dg_ch = prep_bwd_groups(z, CH_B, [dqc, dkc, dvc], [P["g_ch_q"], P["g_ch_k"], None],
                                  name="prep_h_bwd", n_heads=N_HEADS, kinds=("norm", "norm", "copy"))
    ds["g_ch_q"], ds["g_ch_k"] = dg_ch[0], dg_ch[1]
    ds["rel_bias"] = relbias_tiles_bwd(dtiles, P["rel_onehot"])
    token = grads_done("late_followup", d_ch)
    dqb, dkb, dvb, dcq, dck = attn_bwd(s["qb"], s["kb"], z, lambda h: FOX_B + 2 * N_HEADS + h, s["o_b"], do[1],
                                       s["lse_b"], mode="fox", name="attn_fox_bwd", n_heads=N_HEADS,
                                       cq=s["cq"], ck=_after(s["ck"], token))
    d_fox, dg_fox = prep_bwd_groups(z, FOX_B, [dqb, dkb, dvb], [P["g_fox_q"], P["g_fox_k"], None],
                                    name="prep_h_bwd", n_heads=N_HEADS, kinds=("norm", "norm", "copy"))
    ds["g_fox_q"], ds["g_fox_k"] = dg_fox[0], dg_fox[1]
    dcum = jnp.pad((dcq[:, :, 0] + dck[:, 0, :]).T, ((0, 0), (0, 128 - N_HEADS)))
    dff, dbf = fox_cum_bwd(z, P["b_f"], dcum)
    ds["b_f"] = dbf[:, :N_HEADS]
    dqa, dka, dva = attn_bwd(s["qa"], s["ka"], s["kv_raw"], lambda h: 2 * h + 1, s["o_a"], do[0], s["lse_a"],
                             mode="mla", name="attn_mla_bwd", n_heads=N_HEADS)
    dq_raw, dgq = prep_bwd_q(s["q_raw"], dqa, P["g_mla_q"], name="prep_mla_q_bwd", n_heads=N_HEADS,
                             dh=MLA_PAD, n_real=MLA_QK, rope=rope)
    dkv_raw, dkr, dgk = prep_bwd_mla_k(s["kv_raw"], z, dka, dva, P["g_mla_k"], rope, name="prep_mla_k_bwd")
    ds["g_mla_q"], ds["g_mla_k"] = dgq[:, :MLA_QK], dgk[:, :MLA_QK]
    dcq_n = mm_nt(dq_raw, W["uq_p"], name="mm_uq_dx", out_dtype=F32)
    dw["uq_p"] = mm_tn(s["cq_n"], dq_raw, nb=1, name="mm_uq_dw", out_dtype=BF16)
    dckv_n = mm_nt(dkv_raw, W["ukv"], name="mm_ukv_dx", out_dtype=F32)
    dw["ukv"] = mm_tn(s["ckv_n"], dkv_raw, nb=4, name="mm_ukv_dw", out_dtype=BF16)
    d_cq, ds["g_cq"] = rms_bwd(z, P["g_cq"], dcq_n, name="rms_cq_bwd", col0=Z_CQ, width=Q_LORA, dx_dtype=BF16)
    d_ckv, ds["g_ckv"] = rms_bwd(z, P["g_ckv"], dckv_n, name="rms_ckv_bwd", col0=Z_CKV, width=KV_LORA,
                                 dx_dtype=BF16)
    dz = jnp.concatenate([d_cq, d_ckv, dkr.astype(BF16), dff.astype(BF16), d_fox, d_ch, *dgl], axis=1)
    dw["in_p"] = mm_tn(s["h"], dz, nb=1, name="mm_in_dw", out_dtype=BF16)
    token = grads_done("early", dw)
    dh = mm_nt(dz, W["in_p"], name="mm_in_dx", out_dtype=F32, after=token)
    g0, ds["g_mix"] = rms_bwd(x, P["g_mix"], dh, name="rms_d_bwd", res=g1)
    return g0, ds


def local_step(x, mem, target, Ps, weights_of, grads_done):
    rope = rope_tables(x.shape[0])
    onehot = relbias_onehot()
    Ps = [dict(P, rel_onehot=onehot) for P in Ps]
    Ws, saved, xs = [], [], [x]
    for l, P in enumerate(Ps):
        W_early, mid_weights, last_weights, token = weights_of(l, xs[-1])
        y, s, W = layer_fwd(xs[-1], mem, W_early, dict(P, g_mix=_after(P["g_mix"], token)), rope,
                            mid_weights, last_weights)
        xs.append(y)
        saved.append(s)
        Ws.append(W)
    g, loss = loss_head(xs[-1], target)
    dss, token = [], None
    for l in reversed(range(len(Ps))):
        g, ds = layer_bwd(g, xs[l], mem, Ws[l], dict(Ps[l], g_mlp=_after(Ps[l]["g_mlp"], token)), rope,
                          saved[l], functools.partial(grads_done, l))
        token = grads_done(l, "early_followup", g)
        dss.append(ds)
    return loss, g, dss[::-1]


HBM_SPEC = pl.BlockSpec(memory_space=pltpu.HBM)


def _place():
    return lax.axis_index("x"), lax.axis_index("y"), lax.axis_index("c")


def _other_chips(x, y):
    return [(1 - x, y), (x, 1 - y), (1 - x, 1 - y)]


def _remote(src, dst, send_sems, recv_sems, k, to):
    return pltpu.make_async_remote_copy(src_ref=src, dst_ref=dst, send_sem=send_sems.at[k],
                                        recv_sem=recv_sems.at[k], device_id=to, device_id_type=MESH)


SEM_SPEC = pl.BlockSpec(memory_space=pltpu.SEMAPHORE)
SPLIT_EFFECT = pltpu.SideEffectType.DATAFLOW_SIDE_EFFECTING


COPIES_PER_ARRAY = {"gather": 3, "scatter": 3, "halves": 1, "share": 1}


def _chip_exchange_copies(kind, srcs, lands, send_sems, recv_sems):
    x, y, c = _place()
    me = 2 * x + y
    cps = []
    for i, (src, land) in enumerate(zip(srcs, lands)):
        if kind == "halves":
            h = src.shape[1] // 2
            cps.append(_remote(src.at[:, pl.ds((1 - c) * h, h), :], land, send_sems, recv_sems, i, (x, y, 1 - c)))
            continue
        if kind == "share":
            cps.append(_remote(src, land.at[c], send_sems, recv_sems, i, (x, y, 1 - c)))
            continue
        for j, (cx, cy) in enumerate(_other_chips(x, y)):
            if kind == "gather":
                h = src.shape[0] // 2
                s_ref, d_ref = src.at[pl.ds(c * h, h), :], land.at[me, pl.ds(c * h, h), :]
            else:
                s_ref, d_ref = src.at[2 * cx + cy], land.at[me]
            cps.append(_remote(s_ref, d_ref, send_sems, recv_sems, 3 * i + j, (cx, cy, c)))
    return cps


def chip_exchange_start(kind, srcs, land_shapes, *, name, after=None):
    n = len(srcs)
    n_in = 2 * n + (after is not None)

    def body(*refs):
        send_sems, recv_sems = refs[n_in], refs[n_in + 1]
        for cp in _chip_exchange_copies(kind, refs[:n], refs[n:2 * n], send_sems, recv_sems):
            cp.start()
        refs[-1][...] = jnp.zeros_like(refs[-1])

    hbm = lambda a: pltpu.with_memory_space_constraint(a, pltpu.HBM)
    ins = [hbm(s) for s in srcs] + [hbm(lax.empty(s.shape, s.dtype)) for s in land_shapes]
    sems = pltpu.SemaphoreType.DMA((COPIES_PER_ARRAY[kind] * n,))
    out = pl.pallas_call(
        body, name=name, interpret=False,
        out_shape=(sems, sems, *[pltpu.HBM(a.shape, a.dtype) for a in ins], _sds((8, 128), F32)),
        in_specs=[HBM_SPEC] * (2 * n) + [pl.BlockSpec(memory_space=pl.ANY)] * (after is not None),
        out_specs=(SEM_SPEC, SEM_SPEC, *[HBM_SPEC] * (2 * n), pl.BlockSpec(memory_space=pltpu.VMEM)),
        input_output_aliases={i: 2 + i for i in range(2 * n)},
        compiler_params=pltpu.CompilerParams(has_side_effects=SPLIT_EFFECT))(
            *ins, *([after] if after is not None else []))
    return out[0], out[1], list(out[2:2 + n]), list(out[2 + n:2 + 2 * n]), out[-1]


def chip_exchange_wait(kind, started, after, *, name):
    send_sems, recv_sems, srcs, lands, _ = started
    n = len(srcs)

    def body(*refs):
        send_sems, recv_sems = refs[2 * n], refs[2 * n + 1]
        for cp in _chip_exchange_copies(kind, refs[:n], refs[n:2 * n], send_sems, recv_sems):
            cp.wait_send()
            cp.wait_recv()

    thru = srcs + lands
    out = pl.pallas_call(
        body, name=name, interpret=False,
        out_shape=tuple(pltpu.HBM(a.shape, a.dtype) for a in thru),
        in_specs=[HBM_SPEC] * (2 * n) + [SEM_SPEC, SEM_SPEC, pl.BlockSpec(memory_space=pl.ANY)],
        out_specs=tuple([HBM_SPEC] * (2 * n)),
        input_output_aliases={i: i for i in range(2 * n)},
        compiler_params=pltpu.CompilerParams(has_side_effects=SPLIT_EFFECT))(*thru, send_sems, recv_sems, after)
    return list(out[:n]), list(out[n:])


def gather_to_sibling(lands):
    n = len(lands)

    def body(*refs):
        outs = refs[n:2 * n]
        send_sems, recv_sems = refs[2 * n:]
        x, y, c = _place()
        sends = []
        for i in range(n):
            h = outs[i].shape[1] // 2
            for j, (cx, cy) in enumerate(_other_chips(x, y)):
                landed = outs[i].at[2 * cx + cy, pl.ds(c * h, h), :]
                cp = _remote(landed, landed, send_sems, recv_sems, 3 * i + j, (x, y, 1 - c))
                cp.start()
                sends.append(cp)
        for cp in sends:
            cp.wait()

    return pl.pallas_call(
        body, out_shape=tuple(_sds(a.shape, a.dtype) for a in lands), in_specs=[HBM_SPEC] * n,
        out_specs=tuple([HBM_SPEC] * n), input_output_aliases={i: i for i in range(n)},
        scratch_shapes=[pltpu.SemaphoreType.DMA((3 * n,)), pltpu.SemaphoreType.DMA((3 * n,))],
        name="gather_to_sibling", interpret=False)(*lands)


def add_pair(part, got, half_idx):
    _, a, b = part.shape
    h = a // 2
    tr = _pick(h, (512, 256, 128))

    def body(c_ref, p_ref, g_ref, o_ref):
        o_ref[...] = (p_ref[...].astype(F32) + g_ref[...].astype(F32)).astype(o_ref.dtype)

    spec = pl.BlockSpec((None, tr, b), lambda s_, i, c_ref: (s_, i, 0))
    grid_spec = pltpu.PrefetchScalarGridSpec(
        num_scalar_prefetch=1, grid=(4, h // tr),
        in_specs=[pl.BlockSpec((None, None, tr, b), lambda s_, i, c_ref: (s_, c_ref[0], i, 0)), spec],
        out_specs=spec)
    return pl.pallas_call(
        body, out_shape=_sds((4, h, b), part.dtype), grid_spec=grid_spec, name="rs_add_pair", interpret=False,
        compiler_params=pltpu.CompilerParams(vmem_limit_bytes=VMEM_LIMIT_BYTES))(
            half_idx, part.reshape(4, 2, h, b), got)


def sum_slots(r):
    _, h, b = r.shape
    tr = _pick(h, (512, 256, 128))

    def body(r0, r1, r2, r3, o_ref):
        o_ref[...] = ((r0[...].astype(F32) + r1[...].astype(F32)) + r2[...].astype(F32)) + r3[...].astype(F32)

    specs = [pl.BlockSpec((None, tr, b), functools.partial(lambda i, s_: (s_, i, 0), s_=s_)) for s_ in range(4)]
    return _pcall(body, name="rs_sum_slots", out_shape=_sds((h, b), F32), grid=(h // tr,),
                  in_specs=specs, out_specs=pl.BlockSpec((tr, b), lambda i: (i, 0)))(r, r, r, r)


def reduce_halves_start(parts, *, tag):
    return chip_exchange_start("halves", parts, [_sds((4, p.shape[1] // 2, p.shape[2]), p.dtype) for p in parts],
                               name="halves_start_" + tag)


def reduce_scatter_start(halves_started, after, core, *, tag):
    parts, gots = chip_exchange_wait("halves", halves_started, after, name="halves_wait_" + tag)
    half_idx = core.reshape(1).astype(jnp.int32)
    chip_sums = [add_pair(p, g, half_idx) for p, g in zip(parts, gots)]
    return chip_exchange_start("scatter", chip_sums, chip_sums, name="scatter_start_" + tag)


def reduce_share_start(scatter_started, after, chip, *, tag):
    chip_sums, slots = chip_exchange_wait("scatter", scatter_started, after, name="scatter_wait_" + tag)
    slots = [lax.dynamic_update_slice(s_, lax.dynamic_index_in_dim(q, chip, 0, keepdims=True), (chip, 0, 0))
             for s_, q in zip(slots, chip_sums)]
    halves = [sum_slots(s_) for s_ in slots]
    return chip_exchange_start("share", halves, [_sds((2,) + t.shape, t.dtype) for t in halves],
                               name="share_start_" + tag)


def reduce_finish(share_started, after, core, *, tag):
    halves, both = chip_exchange_wait("share", share_started, after, name="share_wait_" + tag)
    both = [lax.dynamic_update_slice(o, t[None], (core, 0, 0)) for o, t in zip(both, halves)]
    return [o.reshape(2 * o.shape[1], o.shape[2]) for o in both]


def allreduce_small(v):
    Rs = v.shape[0]

    def body(v_ref, o_ref, buf, send_sems, recv_sems):
        x, y, c = _place()
        me = 4 * x + 2 * y + c
        buf[me] = v_ref[...]
        flips = [(fx, fy, fc) for fx in (0, 1) for fy in (0, 1) for fc in (0, 1)][1:]
        sends = []
        for k, (fx, fy, fc) in enumerate(flips):
            to = ((1 - x) if fx else x, (1 - y) if fy else y, (1 - c) if fc else c)
            cp = _remote(v_ref, buf.at[me], send_sems, recv_sems, k, to)
            cp.start()
            sends.append(cp)
        for cp in sends:
            cp.wait()
        acc = buf[0]
        for d in range(1, 8):
            acc = acc + buf[d]
        o_ref[...] = acc

    vm = pl.BlockSpec(memory_space=pltpu.VMEM)
    return pl.pallas_call(
        body, out_shape=_sds((Rs, 128), F32), in_specs=[vm], out_specs=vm,
        scratch_shapes=[pltpu.VMEM((8, Rs, 128), F32), pltpu.SemaphoreType.DMA((7,)),
                        pltpu.SemaphoreType.DMA((7,))],
        name="allreduce_small", interpret=False)(v)


INPUT_NAMES = (("x", "mem") + WEIGHT_ORDER + ("loss_target",) + tuple("m_" + n for n in WEIGHT_ORDER)
               + tuple("v_" + n for n in WEIGHT_ORDER))


def _pack_small(vals, n_layers, extra=None):
    flat = jnp.concatenate([vals[n].reshape(n_layers, -1).astype(F32) for n in SMALL_ORDER], axis=1).reshape(-1)
    if extra is not None:
        flat = jnp.concatenate([flat, extra.reshape(-1)])
    n = flat.shape[0]
    rows = -(-n // 1024) * 8
    return jnp.pad(flat, (0, rows * 128 - n)).reshape(rows, 128)


def _unpack_small(packed, like, n_layers):
    per_layer = sum(int(np.prod(like[n].shape[1:])) for n in SMALL_ORDER)
    body = packed.reshape(-1)[:n_layers * per_layer].reshape(n_layers, per_layer)
    out, off = {}, 0
    for n in SMALL_ORDER:
        k = int(np.prod(like[n].shape[1:]))
        out[n] = body[:, off:off + k].reshape(like[n].shape)
        off += k
    return out, packed.reshape(-1)[n_layers * per_layer]


def kernel(x, mem, g_mix, w_in, g_cq, w_uq, g_ckv, w_ukv, g_mla_q, g_mla_k, b_f, g_fox_q, g_fox_k, rel_bias, g_ch_q, g_ch_k, w_br, w_out, g_cross, g_mem, w_xq, w_xkv, g_x_q, g_x_k, w_xo, g_mlp, w_1, w_2, loss_target, m_g_mix, m_w_in, m_g_cq, m_w_uq, m_g_ckv, m_w_ukv, m_g_mla_q, m_g_mla_k, m_b_f, m_g_fox_q, m_g_fox_k, m_rel_bias, m_g_ch_q, m_g_ch_k, m_w_br, m_w_out, m_g_cross, m_g_mem, m_w_xq, m_w_xkv, m_g_x_q, m_g_x_k, m_w_xo, m_g_mlp, m_w_1, m_w_2, v_g_mix, v_w_in, v_g_cq, v_w_uq, v_g_ckv, v_w_ukv, v_g_mla_q, v_g_mla_k, v_b_f, v_g_fox_q, v_g_fox_k, v_rel_bias, v_g_ch_q, v_g_ch_k, v_w_br, v_w_out, v_g_cross, v_g_mem, v_w_xq, v_w_xkv, v_g_x_q, v_g_x_k, v_w_xo, v_g_mlp, v_w_1, v_w_2):
    d = dict(zip(INPUT_NAMES, (x, mem, g_mix, w_in, g_cq, w_uq, g_ckv, w_ukv, g_mla_q, g_mla_k, b_f, g_fox_q, g_fox_k, rel_bias, g_ch_q, g_ch_k, w_br, w_out, g_cross, g_mem, w_xq, w_xkv, g_x_q, g_x_k, w_xo, g_mlp, w_1, w_2, loss_target, m_g_mix, m_w_in, m_g_cq, m_w_uq, m_g_ckv, m_w_ukv, m_g_mla_q, m_g_mla_k, m_b_f, m_g_fox_q, m_g_fox_k, m_rel_bias, m_g_ch_q, m_g_ch_k, m_w_br, m_w_out, m_g_cross, m_g_mem, m_w_xq, m_w_xkv, m_g_x_q, m_g_x_k, m_w_xo, m_g_mlp, m_w_1, m_w_2, v_g_mix, v_w_in, v_g_cq, v_w_uq, v_g_ckv, v_w_ukv, v_g_mla_q, v_g_mla_k, v_b_f, v_g_fox_q, v_g_fox_k, v_rel_bias, v_g_ch_q, v_g_ch_k, v_w_br, v_w_out, v_g_cross, v_g_mem, v_w_xq, v_w_xkv, v_g_x_q, v_g_x_k, v_w_xo, v_g_mlp, v_w_1, v_w_2)))
    n_layers = g_mix.shape[0]
    assert x.shape[0] == 1 and x.shape[2] == D_MODEL and mem.shape[1:] == (MEM_LEN, D_MODEL)
    for n in PACK_ORDER:
        assert d[n].shape[1:] == SHARD_SHAPES[n], (n, d[n].shape)

    chip = 2 * lax.axis_index("x") + lax.axis_index("y")
    core = lax.axis_index("c")

    def start_gather(l, names, after):
        mine = [_rows2d(d[n][l].astype(BF16), n) for n in names]
        return chip_exchange_start("gather", mine, [_sds((4,) + m_.shape, BF16) for m_ in mine],
                                   name=f"gather_start_{l}_{names[0]}", after=after)

    def finish_gather(l, names, started, after):
        mine, lands = chip_exchange_wait("gather", started, after, name=f"gather_wait_{l}_{names[0]}")
        full = [lax.dynamic_update_slice(t, m_[None], (chip, 0, 0)) for t, m_ in zip(gather_to_sibling(lands), mine)]
        return mine, full_weights({n: f.reshape((4,) + SHARD_SHAPES[n]) for n, f in zip(names, full)})

    in_flight = {"early": start_gather(0, EARLY_WEIGHTS, x)}

    def weights_of(l, x_l):
        mine, early = finish_gather(l, EARLY_WEIGHTS, in_flight.pop("early"), x_l)
        in_flight["mid"] = start_gather(l, MID_WEIGHTS, mine[0])
        in_flight["last"] = start_gather(l, LAST_WEIGHTS, in_flight["mid"][4])

        def mid_weights(after):
            mine, mid = finish_gather(l, MID_WEIGHTS, in_flight.pop("mid"), after)
            if l + 1 == n_layers:
                return mid, None
            in_flight["early"] = start_gather(l + 1, EARLY_WEIGHTS, mine[0])
            return mid, in_flight["early"][4]

        def last_weights(after):
            return finish_gather(l, LAST_WEIGHTS, in_flight.pop("last"), after)[1], None

        return early, mid_weights, last_weights, in_flight["mid"][4] + in_flight["last"][4]

    halves, scatters = {}, []

    def grads_done(l, stage, arg):
        names = LATE_WEIGHTS if stage.startswith("late") else EARLY_WEIGHTS
        tag = f"{l}_{names[0]}"
        if not stage.endswith("followup"):
            sg = shard_grads(arg)
            halves[tag] = reduce_halves_start([_rows2d(sg[n], n) for n in names], tag=tag)
            if (l, stage) != (0, "early"):
                return halves[tag][4]
            arg = halves[tag][4]
        elif tag not in halves:
            return None
        started = reduce_scatter_start(halves.pop(tag), arg, core, tag=tag)
        scatters.append((l, names, tag, started))
        return started[4]

    Ps = [layer_params(d, l) for l in range(n_layers)]
    loss, dx, dss = local_step(x[0], mem[0], loss_target[0], Ps, weights_of, grads_done)
    shares = [(l, names, tag, reduce_share_start(started, dx, chip, tag=tag))
              for l, names, tag, started in scatters]
    big = [{} for _ in range(n_layers)]
    for l, names, tag, started in shares:
        big[l].update(zip(names, reduce_finish(started, shares[-1][3][4], core, tag=tag)))

    small_local = {n: jnp.stack([dss[l][n].reshape(d[n].shape[1:]) for l in range(n_layers)]) for n in SMALL_ORDER}
    small_sum, loss_sum = _unpack_small(allreduce_small(_pack_small(small_local, n_layers, extra=loss)),
                                        {n: d[n] for n in SMALL_ORDER}, n_layers)

    grads, delta, new_m, new_v = {}, {}, {}, {}
    for n in PACK_ORDER:
        outs = adamw(_rows2d(d[n], n), [big[l][n] for l in range(n_layers)], _rows2d(d["m_" + n], n),
                     _rows2d(d["v_" + n], n), name="adamw_" + n)
        grads[n], delta[n], new_m[n], new_v[n] = (o.reshape(d[n].shape) for o in outs)
    like = {n: d[n] for n in SMALL_ORDER}
    sm = adamw(_pack_small(like, n_layers)[None], [_pack_small(small_sum, n_layers)],
               _pack_small({n: d["m_" + n] for n in SMALL_ORDER}, n_layers)[None],
               _pack_small({n: d["v_" + n] for n in SMALL_ORDER}, n_layers)[None], name="adamw_small")
    for res, src in zip((grads, delta, new_m, new_v), sm):
        res.update(_unpack_small(src[0], like, n_layers)[0])

    return (loss_sum, dx[None], *[grads[n] for n in WEIGHT_ORDER], *[delta[n] for n in WEIGHT_ORDER],
            *[new_m[n] for n in WEIGHT_ORDER], *[new_v[n] for n in WEIGHT_ORDER])
```

```python
import functools

import numpy as np
import jax
import jax.numpy as jnp
from jax import lax
from jax.experimental import pallas as pl
from jax.experimental.pallas import tpu as pltpu

F32 = jnp.float32
BF16 = jnp.bfloat16
MXU_DTYPE = jnp.bfloat16
MESH = pl.DeviceIdType.MESH

D_MODEL = 2048
MIX_W = 1024
N_HEADS = 8
DH = 128
MLA_NOPE = 128
MLA_ROPE = 64
MLA_QK = MLA_NOPE + MLA_ROPE
MLA_PAD = 256
Q_LORA = 512
KV_LORA = 256
CHUNK = 64
CHUNK_SHIFT = CHUNK.bit_length() - 1
LEFT_CHUNKS = 8
REL_CLIP = 128
N_REL = 2 * REL_CLIP + 1
X_HEADS = 4
MEM_LEN = 256
D_FF = 8192
ROPE_THETA = 10000.0
EPS = 1e-6
NEG = -1e30

Z_CQ, Z_CKV, Z_KR, Z_FF = 0, 512, 768, 896
ZS_W = 1024
Z_FOX = 1024
Z_CH = Z_FOX + 3 * MIX_W
Z_GATE = Z_CH + 3 * MIX_W
Z_TOT = Z_GATE + 3 * D_MODEL
W_IN_CUTS = (0, 512, 768, 832, 3904, 3912, 6984, 13128)
W_IN_SHARD = W_IN_CUTS[-1] // 4
W_IN_PIECES = ((0, 0, 832), (Z_FF, 3904, 8), (Z_FOX, 832, 3072), (Z_CH, 3912, 9216))

ADAM_LR, ADAM_B1, ADAM_B2, ADAM_EPS, ADAM_WD, ADAM_STEP = 0.001, 0.9, 0.999, 1e-08, 0.01, 10

VMEM_LIMIT_BYTES = 56 * 1024 * 1024
PACK_ORDER = ("w_uq", "w_ukv", "w_br", "w_out", "w_xq", "w_xkv", "w_xo", "w_1", "w_2", "w_in")
SMALL_ORDER = ("g_mix", "g_cq", "g_ckv", "g_mla_q", "g_mla_k", "b_f", "g_fox_q", "g_fox_k", "rel_bias",
               "g_ch_q", "g_ch_k", "g_cross", "g_mem", "g_x_q", "g_x_k", "g_mlp")
WEIGHT_ORDER = ("g_mix", "w_in", "g_cq", "w_uq", "g_ckv", "w_ukv", "g_mla_q", "g_mla_k", "b_f", "g_fox_q",
                "g_fox_k", "rel_bias", "g_ch_q", "g_ch_k", "w_br", "w_out", "g_cross", "g_mem", "w_xq",
                "w_xkv", "g_x_q", "g_x_k", "w_xo", "g_mlp", "w_1", "w_2")


def _pick(n, prefs):
    for p in prefs:
        if n % p == 0:
            return p
    raise ValueError(f"no block size among {prefs} divides {n}")


def _pcall(body, *, name, out_shape, in_specs, out_specs, grid=(), scratch=(), aliases=None):
    return pl.pallas_call(
        body, out_shape=out_shape, grid=grid, in_specs=in_specs, out_specs=out_specs,
        scratch_shapes=scratch, name=name, interpret=False,
        input_output_aliases=aliases or {},
        compiler_params=pltpu.CompilerParams(vmem_limit_bytes=VMEM_LIMIT_BYTES))


def _sds(shape, dtype):
    return jax.ShapeDtypeStruct(tuple(shape), dtype)


def _mx(v):
    return v.astype(MXU_DTYPE)


def mm_nn(a, b3, *, name, out_dtype, a_col0=0, res=None, relu2=False):
    M = a.shape[0]
    nb, K, Ns = b3.shape
    N = nb * Ns
    tm = _pick(M, (1024, 512, 256, 128))
    tk = _pick(K, (2048, 1024, 512, 256))
    tn = _pick(Ns, (512, 256, 128))
    assert a_col0 % tk == 0
    nk, nbs, ka0 = K // tk, Ns // tn, a_col0 // tk
    n_out = 2 if relu2 else 1

    def body(*refs):
        a_ref, b_ref = refs[0], refs[1]
        pos = 2
        res_ref = None
        if res is not None:
            res_ref = refs[pos]
            pos += 1
        outs = refs[pos:pos + n_out]
        acc_ref = refs[pos + n_out] if nk > 1 else None
        part = jnp.dot(_mx(a_ref[...]), _mx(b_ref[...]), preferred_element_type=F32)

        def finish(acc):
            if res_ref is not None:
                acc = acc + res_ref[...]
            outs[0][...] = acc.astype(outs[0].dtype)
            if relu2:
                r = jnp.maximum(acc, 0.0)
                outs[1][...] = (r * r).astype(outs[1].dtype)

        if nk == 1:
            finish(part)
        else:
            k = pl.program_id(2)

            @pl.when(k == 0)
            def _():
                acc_ref[...] = part

            @pl.when(k > 0)
            def _():
                acc_ref[...] += part

            @pl.when(k == nk - 1)
            def _():
                finish(acc_ref[...])

    in_specs = [pl.BlockSpec((tm, tk), lambda i, j, k: (i, ka0 + k)),
                pl.BlockSpec((None, tk, tn), lambda i, j, k: (j // nbs, k, j % nbs))]
    args = [a, b3]
    if res is not None:
        in_specs.append(pl.BlockSpec((tm, tn), lambda i, j, k: (i, j)))
        args.append(res)
    o_spec = pl.BlockSpec((tm, tn), lambda i, j, k: (i, j))
    if relu2:
        out_shape, out_specs = (_sds((M, N), out_dtype), _sds((M, N), out_dtype)), (o_spec, o_spec)
    else:
        out_shape, out_specs = _sds((M, N), out_dtype), o_spec
    scratch = (pltpu.VMEM((tm, tn), F32),) if nk > 1 else ()
    return _pcall(body, name=name, out_shape=out_shape, grid=(M // tm, N // tn, nk),
                  in_specs=in_specs, out_specs=out_specs, scratch=scratch)(*args)


def mm_nt(a, b3, *, name, out_dtype, a_col0=0, res=None, relu_mul=None, after=None):
    M = a.shape[0]
    nb, K, Ns = b3.shape
    tm = _pick(M, (1024, 512, 256, 128))
    tk = _pick(K, (1024, 512, 256))
    tn = _pick(Ns, (2048, 1024, 512, 256, 128) if nb == 1 and Ns <= 2048 else (1024, 512, 256, 128))
    assert a_col0 % tn == 0
    nbs = Ns // tn
    nn, a0 = nb * nbs, a_col0 // tn

    def body(*refs):
        a_ref, b_ref = refs[0], refs[1]
        pos = 2
        mul_ref = res_ref = None
        if relu_mul is not None:
            mul_ref = refs[pos]
            pos += 1
        if res is not None:
            res_ref = refs[pos]
            pos += 1
        pos += after is not None
        o_ref = refs[pos]
        acc_ref = refs[pos + 1] if nn > 1 else None
        part = lax.dot_general(_mx(a_ref[...]), _mx(b_ref[...]), (((1,), (1,)), ((), ())),
                               preferred_element_type=F32)

        def finish(acc):
            if mul_ref is not None:
                acc = acc * (2.0 * jnp.maximum(mul_ref[...].astype(F32), 0.0))
            if res_ref is not None:
                acc = acc + res_ref[...]
            o_ref[...] = acc.astype(o_ref.dtype)

        if nn == 1:
            finish(part)
        else:
            j = pl.program_id(2)

            @pl.when(j == 0)
            def _():
                acc_ref[...] = part

            @pl.when(j > 0)
            def _():
                acc_ref[...] += part

            @pl.when(j == nn - 1)
            def _():
                finish(acc_ref[...])

    in_specs = [pl.BlockSpec((tm, tn), lambda i, kk, j: (i, a0 + j)),
                pl.BlockSpec((None, tk, tn), lambda i, kk, j: (j // nbs, kk, j % nbs))]
    args = [a, b3]
    for extra in (relu_mul, res):
        if extra is not None:
            in_specs.append(pl.BlockSpec((tm, tk), lambda i, kk, j: (i, kk)))
            args.append(extra)
    if after is not None:
        in_specs.append(pl.BlockSpec(memory_space=pl.ANY))
        args.append(after)
    scratch = (pltpu.VMEM((tm, tk), F32),) if nn > 1 else ()
    return _pcall(body, name=name, out_shape=_sds((M, K), out_dtype), grid=(M // tm, K // tk, nn),
                  in_specs=in_specs, out_specs=pl.BlockSpec((tm, tk), lambda i, kk, j: (i, kk)),
                  scratch=scratch)(*args)


def mm_tn(a, c, *, nb, name, out_dtype, K=None, N=None, a_col0=0, c_col0=0):
    M = a.shape[0]
    K = K or a.shape[1]
    N = N or c.shape[1]
    Ns = N // nb
    tm = _pick(M, (2048, 1024, 512, 256))
    tk = _pick(K, (1024, 512, 256))
    tn = _pick(Ns, (1024, 512, 256, 128))
    assert a_col0 % tk == 0 and c_col0 % tn == 0
    nm, nbs, a0, c0 = M // tm, Ns // tn, a_col0 // tk, c_col0 // tn

    def body(*refs):
        a_ref, c_ref, o_ref = refs[:3]
        acc_ref = refs[3] if nm > 1 else None
        part = lax.dot_general(_mx(a_ref[...]), _mx(c_ref[...]), (((0,), (0,)), ((), ())),
                               preferred_element_type=F32)
        if nm == 1:
            o_ref[...] = part.astype(o_ref.dtype)
        else:
            m = pl.program_id(2)

            @pl.when(m == 0)
            def _():
                acc_ref[...] = part

            @pl.when(m > 0)
            def _():
                acc_ref[...] += part

            @pl.when(m == nm - 1)
            def _():
                o_ref[...] = acc_ref[...].astype(o_ref.dtype)

    scratch = (pltpu.VMEM((tk, tn), F32),) if nm > 1 else ()
    return _pcall(
        body, name=name, out_shape=_sds((nb, K, Ns), out_dtype), grid=(N // tn, K // tk, nm),
        in_specs=[pl.BlockSpec((tm, tk), lambda j, kk, m: (m, a0 + kk)),
                  pl.BlockSpec((tm, tn), lambda j, kk, m: (m, c0 + j))],
        out_specs=pl.BlockSpec((None, tk, tn), lambda j, kk, m: (j // nbs, kk, j % nbs)),
        scratch=scratch)(a, c)


def rms_fwd(x, g, *, name, col0=0, width=None, out_dtype=BF16):
    R = x.shape[0]
    width = width or x.shape[1]
    assert col0 % width == 0
    cb = col0 // width
    tr = _pick(R, (512, 256, 128))

    def body(x_ref, g_ref, o_ref):
        xf = x_ref[...].astype(F32)
        r = lax.rsqrt(jnp.mean(xf * xf, axis=1, keepdims=True) + EPS)
        o_ref[...] = (xf * r * g_ref[...]).astype(o_ref.dtype)

    return _pcall(body, name=name, out_shape=_sds((R, width), out_dtype), grid=(R // tr,),
                  in_specs=[pl.BlockSpec((tr, width), lambda i: (i, cb)),
                            pl.BlockSpec((1, width), lambda i: (0, 0))],
                  out_specs=pl.BlockSpec((tr, width), lambda i: (i, 0)))(x, g)


def rms_bwd(x, g, dy, *, name, col0=0, width=None, res=None, dx_dtype=F32, need_dx=True):
    R = x.shape[0]
    width = width or x.shape[1]
    cb = col0 // width
    tr = _pick(R, (512, 256, 128))

    def body(*refs):
        x_ref, g_ref, dy_ref = refs[:3]
        pos = 3
        res_ref = None
        if res is not None:
            res_ref = refs[pos]
            pos += 1
        dx_ref = None
        if need_dx:
            dx_ref = refs[pos]
            pos += 1
        dg_ref = refs[pos]
        xf = x_ref[...].astype(F32)
        dyf = dy_ref[...].astype(F32)
        r = lax.rsqrt(jnp.mean(xf * xf, axis=1, keepdims=True) + EPS)
        xh = xf * r
        if need_dx:
            gy = dyf * g_ref[...]
            dx = r * (gy - xh * jnp.mean(gy * xh, axis=1, keepdims=True))
            if res_ref is not None:
                dx = dx + res_ref[...]
            dx_ref[...] = dx.astype(dx_ref.dtype)
        part = jnp.sum(dyf * xh, axis=0, keepdims=True)

        @pl.when(pl.program_id(0) == 0)
        def _():
            dg_ref[...] = part

        @pl.when(pl.program_id(0) > 0)
        def _():
            dg_ref[...] += part

    in_specs = [pl.BlockSpec((tr, width), lambda i: (i, cb)),
                pl.BlockSpec((1, width), lambda i: (0, 0)),
                pl.BlockSpec((tr, width), lambda i: (i, 0))]
    args = [x, g, dy]
    if res is not None:
        in_specs.append(pl.BlockSpec((tr, width), lambda i: (i, 0)))
        args.append(res)
    dg_shape, dg_spec = _sds((1, width), F32), pl.BlockSpec((1, width), lambda i: (0, 0))
    if need_dx:
        out_shape = (_sds((R, width), dx_dtype), dg_shape)
        out_specs = (pl.BlockSpec((tr, width), lambda i: (i, 0)), dg_spec)
    else:
        out_shape, out_specs = dg_shape, dg_spec
    out = _pcall(body, name=name, out_shape=out_shape, grid=(R // tr,), in_specs=in_specs,
                 out_specs=out_specs)(*args)
    return out if need_dx else (None, out)


HEAD_ROW_BLOCKS = (2048, 1024, 512, 256, 128)


def _rope_apply(y, c, sa, sb):
    return y * c + pltpu.roll(y, 96, 1) * sa + pltpu.roll(y, 32, 1) * sb


def _rope_transpose(dy, c, sa, sb):
    return dy * c + pltpu.roll(dy * sa, 32, 1) + pltpu.roll(dy * sb, 96, 1)


def rope_tables(seq):
    pos = jnp.arange(seq, dtype=F32)
    inv = ROPE_THETA ** (-jnp.arange(0, MLA_ROPE, 2, dtype=F32) / MLA_ROPE)
    ang = pos[:, None] * inv[None, :]
    cos, sin = jnp.cos(ang), jnp.sin(ang)
    z32, z64 = jnp.zeros_like(cos), jnp.zeros((seq, 64), F32)
    c = jnp.concatenate([cos, cos, z64], axis=1)
    sa = jnp.concatenate([-sin, z32, z64], axis=1)
    sb = jnp.concatenate([z32, sin, z64], axis=1)
    return c, sa, sb


def _head_vec(part_refs):
    xs = [p[...].astype(F32) for p in part_refs]
    return xs[0] if len(xs) == 1 else jnp.concatenate(xs, axis=1)


def prep_fwd(parts, g, *, name, n_heads, n_real, rope=None):
    rows = parts[0][0].shape[0]
    dh = sum(w for _, w, _ in parts)
    tr = _pick(rows, HEAD_ROW_BLOCKS)
    npart = len(parts)

    def body(*refs):
        part_refs, g_ref = refs[:npart], refs[npart]
        pos = npart + 1
        if rope is not None:
            c_ref, sa_ref, sb_ref = refs[pos:pos + 3]
            pos += 3
        o_ref = refs[pos]
        x = _head_vec(part_refs)
        r = lax.rsqrt(jnp.sum(x * x, axis=1, keepdims=True) * (1.0 / n_real) + EPS)
        y = x * r * g_ref[...]
        if rope is not None:
            yr = _rope_apply(y[:, dh - 128:], c_ref[...], sa_ref[...], sb_ref[...])
            y = jnp.concatenate([y[:, :dh - 128], yr], axis=1)
        o_ref[...] = y.astype(o_ref.dtype)

    in_specs, args = [], []
    for arr, w, fn in parts:
        in_specs.append(pl.BlockSpec((tr, w), functools.partial(lambda h, i, fn: (i, fn(h)), fn=fn)))
        args.append(arr)
    in_specs.append(pl.BlockSpec((1, dh), lambda h, i: (0, 0)))
    args.append(g)
    if rope is not None:
        for t in rope:
            in_specs.append(pl.BlockSpec((tr, 128), lambda h, i: (i, 0)))
            args.append(t)
    return _pcall(body, name=name, out_shape=_sds((n_heads, rows, dh), BF16), grid=(n_heads, rows // tr),
                  in_specs=in_specs, out_specs=pl.BlockSpec((None, tr, dh), lambda h, i: (h, i, 0)))(*args)


def _norm_bwd(x, g, dyn, n_real):
    r = lax.rsqrt(jnp.sum(x * x, axis=1, keepdims=True) * (1.0 / n_real) + EPS)
    xh = x * r
    gy = dyn * g
    dx = r * (gy - xh * (jnp.sum(gy * xh, axis=1, keepdims=True) * (1.0 / n_real)))
    return dx, jnp.sum(dyn * xh, axis=0, keepdims=True)


def prep_bwd_q(src, dy, g, *, name, n_heads, dh, n_real, rope=None, out_dtype=BF16):
    rows = src.shape[0]
    tr = _pick(rows, HEAD_ROW_BLOCKS)

    def body(*refs):
        x_ref, dy_ref, g_ref = refs[:3]
        pos = 3
        if rope is not None:
            c_ref, sa_ref, sb_ref = refs[pos:pos + 3]
            pos += 3
        dx_ref, dg_ref = refs[pos], refs[pos + 1]
        dyn = dy_ref[...].astype(F32)
        if rope is not None:
            dr = _rope_transpose(dyn[:, dh - 128:], c_ref[...], sa_ref[...], sb_ref[...])
            dyn = jnp.concatenate([dyn[:, :dh - 128], dr], axis=1)
        dx, dg = _norm_bwd(x_ref[...].astype(F32), g_ref[...], dyn, n_real)
        dx_ref[...] = dx.astype(dx_ref.dtype)
        first = jnp.logical_and(pl.program_id(0) == 0, pl.program_id(1) == 0)

        @pl.when(first)
        def _():
            dg_ref[...] = dg

        @pl.when(jnp.logical_not(first))
        def _():
            dg_ref[...] += dg

    in_specs = [pl.BlockSpec((tr, dh), lambda i, h: (i, h)),
                pl.BlockSpec((None, tr, dh), lambda i, h: (h, i, 0)),
                pl.BlockSpec((1, dh), lambda i, h: (0, 0))]
    args = [src, dy, g]
    if rope is not None:
        for t in rope:
            in_specs.append(pl.BlockSpec((tr, 128), lambda i, h: (i, 0)))
            args.append(t)
    return _pcall(body, name=name, out_shape=(_sds((rows, n_heads * dh), out_dtype), _sds((1, dh), F32)),
                  grid=(rows // tr, n_heads), in_specs=in_specs,
                  out_specs=(pl.BlockSpec((tr, dh), lambda i, h: (i, h)),
                             pl.BlockSpec((1, dh), lambda i, h: (0, 0))))(*args)


def prep_bwd_mla_k(kv_raw, zs, dkf, dv, g, rope, *, name):
    rows = kv_raw.shape[0]
    tr = _pick(rows, HEAD_ROW_BLOCKS)

    def body(kn_ref, kr_ref, dy_ref, dv_ref, g_ref, c_ref, sa_ref, sb_ref, dkv_ref, dkr_ref, dg_ref):
        h = pl.program_id(1)
        x = jnp.concatenate([kn_ref[...].astype(F32), kr_ref[...].astype(F32)], axis=1)
        dyn = dy_ref[...].astype(F32)
        dr = _rope_transpose(dyn[:, 128:], c_ref[...], sa_ref[...], sb_ref[...])
        dyn = jnp.concatenate([dyn[:, :128], dr], axis=1)
        dx, dg = _norm_bwd(x, g_ref[...], dyn, MLA_QK)
        dkv_ref[...] = jnp.concatenate([dx[:, :128], dv_ref[...].astype(F32)], axis=1).astype(dkv_ref.dtype)

        @pl.when(h == 0)
        def _():
            dkr_ref[...] = dx[:, 128:]

        @pl.when(h > 0)
        def _():
            dkr_ref[...] += dx[:, 128:]

        first = jnp.logical_and(pl.program_id(0) == 0, h == 0)

        @pl.when(first)
        def _():
            dg_ref[...] = dg

        @pl.when(jnp.logical_not(first))
        def _():
            dg_ref[...] += dg

    tab = pl.BlockSpec((tr, 128), lambda i, h: (i, 0))
    return _pcall(
        body, name=name,
        out_shape=(_sds((rows, N_HEADS * 256), BF16), _sds((rows, 128), F32), _sds((1, MLA_PAD), F32)),
        grid=(rows // tr, N_HEADS),
        in_specs=[pl.BlockSpec((tr, 128), lambda i, h: (i, 2 * h)),
                  pl.BlockSpec((tr, 128), lambda i, h: (i, Z_KR // 128)),
                  pl.BlockSpec((None, tr, MLA_PAD), lambda i, h: (h, i, 0)),
                  pl.BlockSpec((None, tr, 128), lambda i, h: (h, i, 0)),
                  pl.BlockSpec((1, MLA_PAD), lambda i, h: (0, 0)), tab, tab, tab],
        out_specs=(pl.BlockSpec((tr, 256), lambda i, h: (i, h)),
                   pl.BlockSpec((tr, 128), lambda i, h: (i, 0)),
                   pl.BlockSpec((1, MLA_PAD), lambda i, h: (0, 0))))(kv_raw, zs, dkf, dv, g, *rope)


def prep_bwd_groups(src, base_blk, dys, gs, *, name, n_heads, kinds, out_dtype=BF16):
    rows = src.shape[0]
    ng = len(kinds)
    J = ng * n_heads
    tr = _pick(rows, HEAD_ROW_BLOCKS)
    gstack = jnp.stack([gs[k] if kinds[k] == "norm" else jnp.ones((1, DH), F32) for k in range(ng)])

    def body(*refs):
        x_ref = refs[0]
        dy_refs = refs[1:1 + ng]
        g_ref, dx_ref, dg_ref = refs[1 + ng:4 + ng]
        j, i = pl.program_id(0), pl.program_id(1)
        grp = j // n_heads
        dy = dy_refs[0][...].astype(F32)
        for k in range(1, ng):
            dy = jnp.where(grp == k, dy_refs[k][...].astype(F32), dy)
        dx, dg = _norm_bwd(x_ref[...].astype(F32), g_ref[...], dy, DH)
        is_copy = functools.reduce(jnp.logical_or, [grp == k for k in range(ng) if kinds[k] == "copy"],
                                   jnp.bool_(False))
        dx_ref[...] = jnp.where(is_copy, dy, dx).astype(dx_ref.dtype)
        dg = jnp.where(is_copy, jnp.zeros_like(dg), dg)
        first = jnp.logical_and(j % n_heads == 0, i == 0)

        @pl.when(first)
        def _():
            dg_ref[...] = dg

        @pl.when(jnp.logical_not(first))
        def _():
            dg_ref[...] += dg

    in_specs = [pl.BlockSpec((tr, DH), lambda j, i: (i, base_blk + j))]
    for k in range(ng):
        in_specs.append(pl.BlockSpec(
            (None, tr, DH),
            functools.partial(lambda j, i, k: (jnp.clip(j - k * n_heads, 0, n_heads - 1),
                                               jnp.where(j // n_heads == k, i, 0), 0), k=k)))
    in_specs.append(pl.BlockSpec((None, 1, DH), lambda j, i: (j // n_heads, 0, 0)))
    return _pcall(body, name=name, out_shape=(_sds((rows, J * DH), out_dtype), _sds((ng, 1, DH), F32)),
                  grid=(J, rows // tr), in_specs=in_specs,
                  out_specs=(pl.BlockSpec((tr, DH), lambda j, i: (i, j)),
                             pl.BlockSpec((None, 1, DH), lambda j, i: (j // n_heads, 0, 0))))(src, *dys, gstack)


def _attn_cfg(mode, sq, sk):
    if mode == "chunk":
        tq = 128
        win = min((LEFT_CHUNKS + 2) * CHUNK, sk)
    else:
        tq = _pick(sq, (256, 128))
        win = sk
    scale = (MLA_QK if mode == "mla" else DH) ** -0.5
    return tq, win, scale


def _attn_key_rows(mode, i, tq, win, sk, run):
    if mode == "chunk":
        start = pl.multiple_of(jnp.clip((i - LEFT_CHUNKS // 2) * 128, 0, sk - win), 128)
        run(pl.ds(start, win), start)
    elif mode == "cross":
        run(slice(0, sk), 0)
    else:
        lax.switch(i, [functools.partial(run, slice(0, (b + 1) * tq), 0) for b in range(sk // tq)])


def _attn_scores(mode, i, tq, scale, q, kk, start, cq, ck, t_ref):
    nk = kk.shape[0]
    s = lax.dot_general(q, kk, (((1,), (1,)), ((), ())), preferred_element_type=F32) * scale
    if mode == "cross":
        return s
    t_pos = i * tq + lax.broadcasted_iota(jnp.int32, (tq, nk), 0)
    s_pos = start + lax.broadcasted_iota(jnp.int32, (tq, nk), 1)
    if mode == "fox":
        s = s + cq - ck
        allowed = s_pos <= t_pos
    else:
        qc, kc = lax.shift_right_logical(t_pos, CHUNK_SHIFT), lax.shift_right_logical(s_pos, CHUNK_SHIFT)
        allowed = kc <= qc
        if mode == "chunk":
            allowed = jnp.logical_and(allowed, kc >= qc - LEFT_CHUNKS)
            tiles = []
            for w in range(nk // 128):
                delta = i - (start // 128 + w)
                tiles.append(jnp.where(delta == 0, t_ref[0], jnp.where(delta == 1, t_ref[1], t_ref[2])))
            s = s + jnp.concatenate(tiles, axis=1)
    return jnp.where(allowed, s, NEG)


def attn_fwd(q, k, v_arr, v_blk, *, mode, name, n_heads, cq=None, ck=None, tiles=None):
    _, sq, dk = q.shape
    sk = k.shape[1]
    tq, win, scale = _attn_cfg(mode, sq, sk)

    def body(*refs):
        q_ref, k_ref, v_ref = refs[:3]
        pos = 3
        cq_ref = ck_ref = t_ref = None
        if mode == "fox":
            cq_ref, ck_ref = refs[pos:pos + 2]
            pos += 2
        if mode == "chunk":
            t_ref = refs[pos]
            pos += 1
        o_ref, lse_ref = refs[pos], refs[pos + 1]
        i = pl.program_id(1)

        def run(rows, start):
            cq, ck = (cq_ref[...], ck_ref[:, rows]) if mode == "fox" else (None, None)
            s = _attn_scores(mode, i, tq, scale, q_ref[...], k_ref[rows, :], start, cq, ck, t_ref)
            m = jnp.max(s, axis=1, keepdims=True)
            e = jnp.exp(s - m)
            l = jnp.sum(e, axis=1, keepdims=True)
            p = e * (1.0 / l)
            o_ref[...] = jnp.dot(_mx(p), _mx(v_ref[rows, :]), preferred_element_type=F32).astype(o_ref.dtype)
            lse_ref[...] = m + jnp.log(l)

        _attn_key_rows(mode, i, tq, win, sk, run)

    in_specs = [pl.BlockSpec((None, tq, dk), lambda h, i: (h, i, 0)),
                pl.BlockSpec((None, sk, dk), lambda h, i: (h, 0, 0)),
                pl.BlockSpec((sk, DH), lambda h, i: (0, v_blk(h)))]
    args = [q, k, v_arr]
    if mode == "fox":
        in_specs += [pl.BlockSpec((None, tq, 1), lambda h, i: (h, i, 0)),
                     pl.BlockSpec((None, 1, sk), lambda h, i: (h, 0, 0))]
        args += [cq, ck]
    if mode == "chunk":
        in_specs.append(pl.BlockSpec((3, None, 128, 128), lambda h, i: (0, h, 0, 0)))
        args.append(tiles)
    return _pcall(body, name=name,
                  out_shape=(_sds((sq, n_heads * DH), BF16), _sds((n_heads, sq, 1), F32)),
                  grid=(n_heads, sq // tq), in_specs=in_specs,
                  out_specs=(pl.BlockSpec((tq, DH), lambda h, i: (i, h)),
                             pl.BlockSpec((None, tq, 1), lambda h, i: (h, i, 0))))(*args)


def attn_bwd(q, k, v_arr, v_blk, o, do, lse, *, mode, name, n_heads, cq=None, ck=None, tiles=None):
    _, sq, dk = q.shape
    sk = k.shape[1]
    tq, win, scale = _attn_cfg(mode, sq, sk)
    n_extra = {"fox": 2, "chunk": 1}.get(mode, 0)

    def body(*refs):
        q_ref, k_ref, v_ref, o_ref, do_ref, lse_ref = refs[:6]
        pos = 6
        cq_ref = ck_ref = t_ref = None
        if mode == "fox":
            cq_ref, ck_ref = refs[pos:pos + 2]
            pos += 2
        if mode == "chunk":
            t_ref = refs[pos]
            pos += 1
        dq_ref, dk_ref, dv_ref = refs[pos:pos + 3]
        extra = refs[pos + 3:pos + 3 + n_extra]
        i = pl.program_id(1)

        @pl.when(i == 0)
        def _():
            dk_ref[...] = jnp.zeros_like(dk_ref)
            dv_ref[...] = jnp.zeros_like(dv_ref)
            if mode == "fox":
                extra[1][...] = jnp.zeros_like(extra[1])
            if mode == "chunk":
                extra[0][...] = jnp.zeros_like(extra[0])

        def run(rows, start):
            q = q_ref[...]
            do = do_ref[...]
            kk = k_ref[rows, :]
            cq, ck = (cq_ref[...], ck_ref[:, rows]) if mode == "fox" else (None, None)
            s = _attn_scores(mode, i, tq, scale, q, kk, start, cq, ck, t_ref)
            p = jnp.exp(s - lse_ref[...])
            drow = jnp.sum(do.astype(F32) * o_ref[...].astype(F32), axis=1, keepdims=True)
            dp = lax.dot_general(do, _mx(v_ref[rows, :]), (((1,), (1,)), ((), ())), preferred_element_type=F32)
            ds = p * (dp - drow)
            dsb = _mx(ds)
            dq_ref[...] = (jnp.dot(dsb, kk, preferred_element_type=F32) * scale).astype(dq_ref.dtype)
            dk_ref[rows, :] += lax.dot_general(dsb, q, (((0,), (0,)), ((), ())),
                                               preferred_element_type=F32) * scale
            dv_ref[rows, :] += lax.dot_general(_mx(p), do, (((0,), (0,)), ((), ())), preferred_element_type=F32)
            if mode == "chunk":
                dt_ref = extra[0]
                for w in range(win // 128):
                    delta = i - (start // 128 + w)
                    tile = ds[:, w * 128:(w + 1) * 128]
                    zero = jnp.zeros_like(tile)
                    dt_ref[0] += jnp.where(delta == 0, tile, zero)
                    dt_ref[1] += jnp.where(delta == 1, tile, zero)
                    dt_ref[2] += jnp.where(delta >= 2, tile, zero)
            if mode == "fox":
                extra[0][...] = jnp.sum(ds, axis=1, keepdims=True)
                extra[1][:, rows] -= jnp.sum(ds, axis=0, keepdims=True)

        _attn_key_rows(mode, i, tq, win, sk, run)

    in_specs = [pl.BlockSpec((None, tq, dk), lambda h, i: (h, i, 0)),
                pl.BlockSpec((None, sk, dk), lambda h, i: (h, 0, 0)),
                pl.BlockSpec((sk, DH), lambda h, i: (0, v_blk(h))),
                pl.BlockSpec((tq, DH), lambda h, i: (i, h)),
                pl.BlockSpec((tq, DH), lambda h, i: (i, h)),
                pl.BlockSpec((None, tq, 1), lambda h, i: (h, i, 0))]
    args = [q, k, v_arr, o, do, lse]
    out_shape = [_sds((n_heads, sq, dk), F32), _sds((n_heads, sk, dk), F32), _sds((n_heads, sk, DH), F32)]
    out_specs = [pl.BlockSpec((None, tq, dk), lambda h, i: (h, i, 0)),
                 pl.BlockSpec((None, sk, dk), lambda h, i: (h, 0, 0)),
                 pl.BlockSpec((None, sk, DH), lambda h, i: (h, 0, 0))]
    if mode == "fox":
        in_specs += [pl.BlockSpec((None, tq, 1), lambda h, i: (h, i, 0)),
                     pl.BlockSpec((None, 1, sk), lambda h, i: (h, 0, 0))]
        args += [cq, ck]
        out_shape += [_sds((n_heads, sq, 1), F32), _sds((n_heads, 1, sk), F32)]
        out_specs += [pl.BlockSpec((None, tq, 1), lambda h, i: (h, i, 0)),
                      pl.BlockSpec((None, 1, sk), lambda h, i: (h, 0, 0))]
    if mode == "chunk":
        in_specs.append(pl.BlockSpec((3, None, 128, 128), lambda h, i: (0, h, 0, 0)))
        args.append(tiles)
        out_shape.append(_sds((3, n_heads, 128, 128), F32))
        out_specs.append(pl.BlockSpec((3, None, 128, 128), lambda h, i: (0, h, 0, 0)))
    return _pcall(body, name=name, out_shape=tuple(out_shape), grid=(n_heads, sq // tq),
                  in_specs=in_specs, out_specs=tuple(out_specs))(*args)


REL_LANES = 384
REL_KBLK = 2048


def _rel_onehot(t, k):
    rho = k * REL_KBLK + lax.broadcasted_iota(jnp.int32, (REL_KBLK, REL_LANES), 0)
    lane = lax.broadcasted_iota(jnp.int32, (REL_KBLK, REL_LANES), 1)
    diff = lax.shift_right_logical(rho, 7) - jnp.bitwise_and(rho, 127)
    idx = jnp.where(t == 0, diff + REL_CLIP,
                    jnp.where(t == 1, jnp.minimum(diff + 128, REL_CLIP) + REL_CLIP, N_REL - 1))
    return jnp.where(idx == lane, 1.0, 0.0).astype(F32)


def _split3(x):
    hi = x.astype(BF16)
    rest = x - hi.astype(F32)
    mid = rest.astype(BF16)
    return hi, mid, (rest - mid.astype(F32)).astype(BF16)


def relbias_onehot():
    def body(o_ref):
        o_ref[...] = _rel_onehot(pl.program_id(0), pl.program_id(1)).astype(o_ref.dtype)

    return _pcall(body, name="relbias_onehot", out_shape=_sds((3, 128 * 128, REL_LANES), BF16),
                  grid=(3, 128 * 128 // REL_KBLK), in_specs=[],
                  out_specs=pl.BlockSpec((None, REL_KBLK, REL_LANES), lambda t, k: (t, k, 0)))()


ONEHOT_SPEC = pl.BlockSpec((None, REL_KBLK, REL_LANES), lambda t, k: (t, k, 0))


def relbias_tiles(rel_bias, onehot):
    nh = rel_bias.shape[0]
    rb = jnp.pad(rel_bias, ((0, 0), (0, REL_LANES - N_REL)))

    def body(rb_ref, e_ref, o_ref):
        e = e_ref[...]
        hi, mid, lo = [lax.dot_general(t, e, (((1,), (1,)), ((), ())), preferred_element_type=F32)
                       for t in _split3(rb_ref[...])]
        o_ref[...] = (hi + mid) + lo

    flat = _pcall(body, name="relbias_tiles", out_shape=_sds((3, nh, 128 * 128), F32),
                  grid=(3, 128 * 128 // REL_KBLK),
                  in_specs=[pl.BlockSpec((nh, REL_LANES), lambda t, k: (0, 0)), ONEHOT_SPEC],
                  out_specs=pl.BlockSpec((None, nh, REL_KBLK), lambda t, k: (t, 0, k)))(rb, onehot)
    return flat.reshape(3, nh, 128, 128)


def relbias_tiles_bwd(dtiles, onehot):
    nh = dtiles.shape[1]

    def body(dt_ref, e_ref, o_ref):
        t, k = pl.program_id(0), pl.program_id(1)
        e = e_ref[...]
        hi, mid, lo = [jnp.dot(t_, e, preferred_element_type=F32) for t_ in _split3(dt_ref[...])]
        part = (hi + mid) + lo
        first = jnp.logical_and(t == 0, k == 0)

        @pl.when(first)
        def _():
            o_ref[...] = part

        @pl.when(jnp.logical_not(first))
        def _():
            o_ref[...] += part

    out = _pcall(body, name="relbias_tiles_bwd", out_shape=_sds((nh, REL_LANES), F32),
                 grid=(3, 128 * 128 // REL_KBLK),
                 in_specs=[pl.BlockSpec((None, nh, REL_KBLK), lambda t, k: (t, 0, k)), ONEHOT_SPEC],
                 out_specs=pl.BlockSpec((nh, REL_LANES), lambda t, k: (0, 0)))(
                     dtiles.reshape(3, nh, 128 * 128), onehot)
    return out[:, :N_REL]


CUM_BLK = 256


def _tri(n, lower):
    r = lax.broadcasted_iota(jnp.int32, (n, n), 0)
    c = lax.broadcasted_iota(jnp.int32, (n, n), 1)
    return jnp.where(r >= c if lower else r <= c, 1.0, 0.0).astype(F32)


def fox_cum_fwd(zs, bf):
    S = zs.shape[0]
    tb = min(CUM_BLK, S)

    def body(f_ref, b_ref, cum_ref, cumt_ref, carry_ref):
        @pl.when(pl.program_id(0) == 0)
        def _():
            carry_ref[...] = jnp.zeros_like(carry_ref)

        x = f_ref[...] + b_ref[...]
        lane = lax.broadcasted_iota(jnp.int32, x.shape, 1)
        logf = jnp.where(lane < N_HEADS, jnp.minimum(x, 0.0) - jnp.log(1.0 + jnp.exp(-jnp.abs(x))), 0.0)
        cum = jnp.dot(_tri(tb, True), logf, preferred_element_type=F32,
                      precision=lax.Precision.HIGHEST) + carry_ref[...]
        carry_ref[...] = cum[tb - 1:tb, :]
        cum_ref[...] = cum
        cumt_ref[...] = cum.T

    return _pcall(body, name="fox_cum_fwd", out_shape=(_sds((S, 128), F32), _sds((128, S), F32)),
                  grid=(S // tb,),
                  in_specs=[pl.BlockSpec((tb, 128), lambda i: (i, Z_FF // 128)),
                            pl.BlockSpec((1, 128), lambda i: (0, 0))],
                  out_specs=(pl.BlockSpec((tb, 128), lambda i: (i, 0)),
                             pl.BlockSpec((128, tb), lambda i: (0, i))),
                  scratch=(pltpu.VMEM((1, 128), F32),))(zs, bf)


def fox_cum_bwd(zs, bf, dcum):
    S = zs.shape[0]
    tb = min(CUM_BLK, S)
    nblk = S // tb

    def body(f_ref, b_ref, d_ref, df_ref, db_ref, carry_ref):
        @pl.when(pl.program_id(0) == 0)
        def _():
            carry_ref[...] = jnp.zeros_like(carry_ref)
            db_ref[...] = jnp.zeros_like(db_ref)

        d = d_ref[...]
        dlogf = jnp.dot(_tri(tb, False), d, preferred_element_type=F32,
                        precision=lax.Precision.HIGHEST) + carry_ref[...]
        carry_ref[...] += jnp.sum(d, axis=0, keepdims=True)
        x = f_ref[...] + b_ref[...]
        lane = lax.broadcasted_iota(jnp.int32, x.shape, 1)
        dx = jnp.where(lane < N_HEADS, dlogf / (1.0 + jnp.exp(x)), 0.0)
        df_ref[...] = dx
        db_ref[...] += jnp.sum(dx, axis=0, keepdims=True)

    return _pcall(body, name="fox_cum_bwd", out_shape=(_sds((S, 128), F32), _sds((1, 128), F32)),
                  grid=(nblk,),
                  in_specs=[pl.BlockSpec((tb, 128), lambda i: (nblk - 1 - i, Z_FF // 128)),
                            pl.BlockSpec((1, 128), lambda i: (0, 0)),
                            pl.BlockSpec((tb, 128), lambda i: (nblk - 1 - i, 0))],
                  out_specs=(pl.BlockSpec((tb, 128), lambda i: (nblk - 1 - i, 0)),
                             pl.BlockSpec((1, 128), lambda i: (0, 0))),
                  scratch=(pltpu.VMEM((1, 128), F32),))(zs, bf, dcum)


def _sigmoid(x):
    return 1.0 / (1.0 + jnp.exp(-x))


def merge_fwd(z, projs):
    S = z.shape[0]
    tr, tc = _pick(S, (512, 256, 128)), 512
    nbc = D_MODEL // tc
    g0 = Z_GATE // tc

    def body(g0_ref, g1_ref, g2_ref, p0_ref, p1_ref, p2_ref, o_ref):
        acc = _sigmoid(g0_ref[...]) * p0_ref[...]
        acc += _sigmoid(g1_ref[...]) * p1_ref[...]
        acc += _sigmoid(g2_ref[...]) * p2_ref[...]
        o_ref[...] = acc.astype(o_ref.dtype)

    gspecs = [pl.BlockSpec((tr, tc), functools.partial(lambda i, j, n: (i, g0 + n * nbc + j), n=n))
              for n in range(3)]
    pspec = pl.BlockSpec((tr, tc), lambda i, j: (i, j))
    return _pcall(body, name="merge_fwd", out_shape=_sds((S, D_MODEL), BF16), grid=(S // tr, nbc),
                  in_specs=gspecs + [pspec] * 3, out_specs=pspec)(z, z, z, *projs)


def merge_bwd(z, projs, dmerged):
    S = z.shape[0]
    tr, tc = _pick(S, (512, 256, 128)), 512
    nbc = D_MODEL // tc
    g0 = Z_GATE // tc

    def body(g0_ref, g1_ref, g2_ref, p0_ref, p1_ref, p2_ref, dm_ref, dg0, dg1, dg2, dp0, dp1, dp2):
        dm = dm_ref[...]
        for g_ref, p_ref, dg_ref, dp_ref in ((g0_ref, p0_ref, dg0, dp0), (g1_ref, p1_ref, dg1, dp1),
                                             (g2_ref, p2_ref, dg2, dp2)):
            sg = _sigmoid(g_ref[...])
            dp_ref[...] = (dm * sg).astype(dp_ref.dtype)
            dg_ref[...] = (dm * p_ref[...] * sg * (1.0 - sg)).astype(dg_ref.dtype)

    gspecs = [pl.BlockSpec((tr, tc), functools.partial(lambda i, j, n: (i, g0 + n * nbc + j), n=n))
              for n in range(3)]
    pspec = pl.BlockSpec((tr, tc), lambda i, j: (i, j))
    out = _pcall(body, name="merge_bwd", out_shape=tuple(_sds((S, D_MODEL), BF16) for _ in range(6)),
                 grid=(S // tr, nbc), in_specs=gspecs + [pspec] * 4,
                 out_specs=tuple([pspec] * 6))(z, z, z, *projs, dmerged)
    return out[:3], out[3:]


def loss_head(y, target):
    S, D = y.shape
    tr = _pick(S, (256, 128))

    def body(y_ref, t_ref, dy_ref, l_ref):
        e = y_ref[...] - t_ref[...]
        dy_ref[...] = e * (1.0 / D)
        part = jnp.sum(jnp.sum(e * e, axis=1, keepdims=True), axis=0, keepdims=True) * (0.5 / D)

        @pl.when(pl.program_id(0) == 0)
        def _():
            l_ref[...] = part

        @pl.when(pl.program_id(0) > 0)
        def _():
            l_ref[...] += part

    spec = pl.BlockSpec((tr, D), lambda i: (i, 0))
    return _pcall(body, name="loss_head", out_shape=(_sds((S, D), F32), _sds((1, 1), F32)), grid=(S // tr,),
                  in_specs=[spec, spec], out_specs=(spec, pl.BlockSpec((1, 1), lambda i: (0, 0))))(y, target)


def _adamw_update(w, g, m, v):
    nm = ADAM_B1 * m + (1.0 - ADAM_B1) * g
    nv = ADAM_B2 * v + (1.0 - ADAM_B2) * (g * g)
    delta = -ADAM_LR * ((nm / (1.0 - ADAM_B1 ** ADAM_STEP)) / (jnp.sqrt(nv / (1.0 - ADAM_B2 ** ADAM_STEP)) + ADAM_EPS)
                        + ADAM_WD * w)
    return delta, nm, nv


ADAMW_BLOCK_BYTES = 1024 * 1024


def adamw_halves(w, pairs, m, v, core, *, name):
    L, R, C = w.shape
    h = R // 2
    tr = _pick(h, (512, 256, 128, 64, 32, 16, 8))
    while tr * C * 4 > ADAMW_BLOCK_BYTES and tr % 16 == 0:
        tr //= 2
    nbh = h // tr

    def body(*refs):
        c_ref, w_ref, m_ref, v_ref = refs[:4]
        g_refs = refs[4:4 + 2 * L]
        go_ref, d_ref, nm_ref, nv_ref = refs[4 + 2 * L:]
        l, mine = pl.program_id(0), pl.program_id(1) // nbh == c_ref[0]
        g_ = jnp.where(mine, g_refs[0][...], g_refs[1][...])
        for k in range(1, L):
            g_ = jnp.where(l == k, jnp.where(mine, g_refs[2 * k][...], g_refs[2 * k + 1][...]), g_)
        go_ref[...] = g_
        d_ref[...], nm_ref[...], nv_ref[...] = _adamw_update(w_ref[...], g_, m_ref[...], v_ref[...])

    def g_index(l, i, c_ref, k, own):
        on = jnp.logical_and(l == k, (i // nbh == c_ref[0]) == own)
        return jnp.where(on, i % nbh, 0), 0

    spec = pl.BlockSpec((None, tr, C), lambda l, i, c_ref: (l, i, 0))
    gspecs = [pl.BlockSpec((tr, C), functools.partial(g_index, k=k, own=own)) for k in range(L) for own in (True, False)]
    grid_spec = pltpu.PrefetchScalarGridSpec(num_scalar_prefetch=1, grid=(L, R // tr),
                                             in_specs=[spec] * 3 + gspecs, out_specs=(spec,) * 4)
    return pl.pallas_call(
        body, out_shape=tuple(_sds((L, R, C), F32) for _ in range(4)), grid_spec=grid_spec, name=name,
        interpret=False, compiler_params=pltpu.CompilerParams(vmem_limit_bytes=VMEM_LIMIT_BYTES))(
            core, w, m, v, *[a for pair in pairs for a in pair])


def adamw(w, gs, m, v, *, name):
    L, R, C = w.shape
    tr = _pick(R, (512, 256, 128, 64, 32, 16, 8))
    while tr * C * 4 > ADAMW_BLOCK_BYTES and tr % 16 == 0:
        tr //= 2

    def body(*refs):
        w_ref, m_ref, v_ref = refs[:3]
        g_refs = refs[3:3 + L]
        go_ref, d_ref, nm_ref, nv_ref = refs[3 + L:]
        l = pl.program_id(0)
        g_ = g_refs[0][...]
        for k in range(1, L):
            g_ = jnp.where(l == k, g_refs[k][...], g_)
        go_ref[...] = g_
        d_ref[...], nm_ref[...], nv_ref[...] = _adamw_update(w_ref[...], g_, m_ref[...], v_ref[...])

    spec = pl.BlockSpec((None, tr, C), lambda l, i: (l, i, 0))
    gspecs = [pl.BlockSpec((tr, C), functools.partial(lambda l, i, k: (jnp.where(l == k, i, 0), 0), k=k))
              for k in range(L)]
    return _pcall(body, name=name, out_shape=tuple(_sds((L, R, C), F32) for _ in range(4)),
                  grid=(L, R // tr), in_specs=[spec] * 3 + gspecs, out_specs=(spec,) * 4)(w, m, v, *gs)


SHARD_SHAPES = {
    "w_uq": (Q_LORA, 384), "w_ukv": (KV_LORA, 512), "w_br": (3, MIX_W, 512), "w_out": (512, D_MODEL),
    "w_xq": (512, 512), "w_xkv": (512, 1024), "w_xo": (512, 512), "w_1": (D_MODEL, 2048),
    "w_2": (2048, D_MODEL), "w_in": (D_MODEL, W_IN_SHARD),
}


def _rows2d(a, name):
    shp = SHARD_SHAPES[name]
    return a.reshape(a.shape[:a.ndim - len(shp)] + (-1, shp[-1]))


def _cols_from_shards(g):
    return jnp.transpose(g, (1, 0, 2)).reshape(g.shape[1], 4 * g.shape[2])


def _cols_to_shards(w):
    return jnp.transpose(w.reshape(w.shape[0], 4, w.shape[1] // 4), (1, 0, 2))


EARLY_WEIGHTS = ("w_in", "w_uq", "w_ukv")
MID_WEIGHTS = ("w_br", "w_out", "w_xq", "w_xkv", "w_xo")
LAST_WEIGHTS = ("w_1", "w_2")
LATE_WEIGHTS = MID_WEIGHTS + LAST_WEIGHTS
assert sorted(EARLY_WEIGHTS + LATE_WEIGHTS) == sorted(PACK_ORDER)


def _w_in_relaid(g_in):
    zeros = lambda n: [jnp.zeros((D_MODEL, n), g_in.dtype)] if n else []
    segs, at = [], 0
    for p0, o0, w in sorted(W_IN_PIECES):
        segs += zeros(p0 - at)
        while w > 0:
            s_, a = divmod(o0, W_IN_SHARD)
            take = min(w, W_IN_SHARD - a)
            segs.append(g_in[s_][:, a:a + take])
            o0, p0, w = o0 + take, p0 + take, w - take
        at = p0
    return jnp.concatenate(segs + zeros(Z_TOT - at), axis=1)


def full_weights(g):
    forms = {
        "w_in": lambda a: {"in_p": _w_in_relaid(a)[None]},
        "w_uq": lambda a: {"uq_p": jnp.pad(_cols_from_shards(a).reshape(Q_LORA, N_HEADS, MLA_QK),
                                           ((0, 0), (0, 0), (0, MLA_PAD - MLA_QK))
                                           ).reshape(1, Q_LORA, N_HEADS * MLA_PAD)},
        "w_ukv": lambda a: {"ukv": a},
        "w_br": lambda a: {"br": [a[:, n] for n in range(3)]},
        "w_out": lambda a: {"out": a.reshape(1, D_MODEL, D_MODEL)},
        "w_xq": lambda a: {"xq": a.reshape(1, D_MODEL, 512)},
        "w_xkv": lambda a: {"xkv": a.reshape(1, D_MODEL, 1024)},
        "w_xo": lambda a: {"xo": a},
        "w_1": lambda a: {"w1": a},
        "w_2": lambda a: {"w2": a.reshape(1, D_FF, D_MODEL)},
    }
    out = {}
    for n, a in g.items():
        out.update(forms[n](a))
    return out


def _w_in_grad_shards(dp):
    d_in = []
    for s_ in range(4):
        lo, hi, segs = s_ * W_IN_SHARD, (s_ + 1) * W_IN_SHARD, []
        for o0, p0, w in sorted((o0, p0, w) for p0, o0, w in W_IN_PIECES):
            a, b = max(lo, o0), min(hi, o0 + w)
            if a < b:
                segs.append(dp[:, p0 + a - o0:p0 + b - o0])
        d_in.append(jnp.concatenate(segs, axis=1))
    return jnp.stack(d_in)


def shard_grads(dw):
    forms = {
        "in_p": lambda a: {"w_in": _w_in_grad_shards(a[0])},
        "uq_p": lambda a: {"w_uq": _cols_to_shards(
            a.reshape(Q_LORA, N_HEADS, MLA_PAD)[:, :, :MLA_QK].reshape(Q_LORA, N_HEADS * MLA_QK))},
        "ukv": lambda a: {"w_ukv": a},
        "br": lambda a: {"w_br": jnp.stack(a, axis=1)},
        "out": lambda a: {"w_out": a.reshape(4, 512, D_MODEL)},
        "xq": lambda a: {"w_xq": a.reshape(4, 512, 512)},
        "xkv": lambda a: {"w_xkv": a.reshape(4, 512, 1024)},
        "xo": lambda a: {"w_xo": a},
        "w1": lambda a: {"w_1": a},
        "w2": lambda a: {"w_2": a.reshape(4, 2048, D_MODEL)},
    }
    out = {}
    for k, a in dw.items():
        out.update(forms[k](a))
    return out


def layer_params(d, l):
    row = lambda v: v.reshape(1, -1).astype(F32)
    padto = lambda v, n: jnp.pad(row(v), ((0, 0), (0, n - v.shape[-1])))
    return dict(
        g_mix=row(d["g_mix"][l]), g_cq=row(d["g_cq"][l]), g_ckv=row(d["g_ckv"][l]),
        g_mla_q=padto(d["g_mla_q"][l], MLA_PAD), g_mla_k=padto(d["g_mla_k"][l], MLA_PAD),
        b_f=padto(d["b_f"][l], 128), g_fox_q=row(d["g_fox_q"][l]), g_fox_k=row(d["g_fox_k"][l]),
        rel_bias=d["rel_bias"][l].astype(F32), g_ch_q=row(d["g_ch_q"][l]), g_ch_k=row(d["g_ch_k"][l]),
        g_cross=row(d["g_cross"][l]), g_mem=row(d["g_mem"][l]), g_x_q=row(d["g_x_q"][l]),
        g_x_k=row(d["g_x_k"][l]), g_mlp=row(d["g_mlp"][l]))


FOX_B = Z_FOX // 128
CH_B = Z_CH // 128


def _after(small, token):
    return small if token is None else small + token[:1, :1].reshape((1,) * small.ndim)


def layer_fwd(x, mem, W, P, rope, mid_weights, last_weights):
    S = x.shape[0]
    s = {}
    s["h"] = rms_fwd(x, P["g_mix"], name="rms_d")
    z = s["z"] = mm_nn(s["h"], W["in_p"], name="mm_in", out_dtype=F32)
    s["cq_n"] = rms_fwd(z, P["g_cq"], col0=Z_CQ, width=Q_LORA, name="rms_cq")
    s["ckv_n"] = rms_fwd(z, P["g_ckv"], col0=Z_CKV, width=KV_LORA, name="rms_ckv")
    s["q_raw"] = mm_nn(s["cq_n"], W["uq_p"], name="mm_uq", out_dtype=F32)
    s["kv_raw"] = mm_nn(s["ckv_n"], W["ukv"], name="mm_ukv", out_dtype=F32)
    s["qa"] = prep_fwd([(s["q_raw"], MLA_PAD, lambda h: h)], P["g_mla_q"], name="prep_mla_q",
                       n_heads=N_HEADS, n_real=MLA_QK, rope=rope)
    s["ka"] = prep_fwd([(s["kv_raw"], 128, lambda h: 2 * h), (z, 128, lambda h: Z_KR // 128)], P["g_mla_k"],
                       name="prep_mla_k", n_heads=N_HEADS, n_real=MLA_QK, rope=rope)
    s["o_a"], s["lse_a"] = attn_fwd(s["qa"], s["ka"], s["kv_raw"], lambda h: 2 * h + 1, mode="mla",
                                    name="attn_mla", n_heads=N_HEADS)
    s["qb"] = prep_fwd([(z, DH, lambda h: FOX_B + h)], P["g_fox_q"], name="prep_h", n_heads=N_HEADS, n_real=DH)
    s["kb"] = prep_fwd([(z, DH, lambda h: FOX_B + N_HEADS + h)], P["g_fox_k"], name="prep_h",
                       n_heads=N_HEADS, n_real=DH)
    _, cum_t = fox_cum_fwd(z, P["b_f"])
    s["cq"] = cum_t[:N_HEADS].reshape(N_HEADS, S, 1)
    s["ck"] = cum_t[:N_HEADS].reshape(N_HEADS, 1, S)
    s["o_b"], s["lse_b"] = attn_fwd(s["qb"], s["kb"], z, lambda h: FOX_B + 2 * N_HEADS + h, mode="fox",
                                    name="attn_fox", n_heads=N_HEADS, cq=s["cq"], ck=s["ck"])
    s["qc"] = prep_fwd([(z, DH, lambda h: CH_B + h)], P["g_ch_q"], name="prep_h", n_heads=N_HEADS, n_real=DH)
    s["kc"] = prep_fwd([(z, DH, lambda h: CH_B + N_HEADS + h)], P["g_ch_k"], name="prep_h",
                       n_heads=N_HEADS, n_real=DH)
    s["tiles"] = relbias_tiles(P["rel_bias"], P["rel_onehot"])
    s["o_c"], s["lse_c"] = attn_fwd(s["qc"], s["kc"], z, lambda h: CH_B + 2 * N_HEADS + h, mode="chunk",
                                    name="attn_chunk", n_heads=N_HEADS, tiles=s["tiles"])
    mid, token = mid_weights(s["o_c"])
    W = dict(W, **mid)
    P = dict(P, g_cross=_after(P["g_cross"], token))
    s["projs"] = [mm_nn(o, W["br"][n], name="mm_br", out_dtype=F32)
                  for n, o in enumerate((s["o_a"], s["o_b"], s["o_c"]))]
    s["merged"] = merge_fwd(z, s["projs"])
    x1 = s["x1"] = mm_nn(s["merged"], W["out"], name="mm_out", out_dtype=F32, res=x)
    s["hq"] = rms_fwd(x1, P["g_cross"], name="rms_d")
    s["xq_raw"] = mm_nn(s["hq"], W["xq"], name="mm_xq", out_dtype=F32)
    s["mem_n"] = rms_fwd(mem, P["g_mem"], name="rms_mem")
    s["mkv"] = mm_nn(s["mem_n"], W["xkv"], name="mm_xkv", out_dtype=F32)
    s["qx"] = prep_fwd([(s["xq_raw"], DH, lambda h: h)], P["g_x_q"], name="prep_xq", n_heads=X_HEADS, n_real=DH)
    s["kx"] = prep_fwd([(s["mkv"], DH, lambda h: h)], P["g_x_k"], name="prep_xk", n_heads=X_HEADS, n_real=DH)
    s["o_x"], s["lse_x"] = attn_fwd(s["qx"], s["kx"], s["mkv"], lambda h: X_HEADS + h, mode="cross",
                                    name="attn_cross", n_heads=X_HEADS)
    x2 = s["x2"] = mm_nn(s["o_x"], W["xo"], name="mm_xo", out_dtype=F32, res=x1)
    s["hm"] = rms_fwd(x2, P["g_mlp"], name="rms_d")
    last, _ = last_weights(s["hm"])
    W = dict(W, **last)
    s["a1"], s["act"] = mm_nn(s["hm"], W["w1"], name="mm_w1", out_dtype=BF16, relu2=True)
    x3 = mm_nn(s["act"], W["w2"], name="mm_w2", out_dtype=F32, res=x2)
    return x3, s, W


LATE_GRADS = ("w2", "w1", "xo", "xq", "xkv", "out", "br")


def layer_bwd(g, x, mem, W, P, rope, s, grads_done):
    S = x.shape[0]
    z = s["z"]
    dw, ds = {}, {}
    da1 = mm_nt(g, W["w2"], name="mm_w2_dx", out_dtype=BF16, relu_mul=s["a1"])
    dw["w2"] = mm_tn(s["act"], g, nb=1, name="mm_w2_dw", out_dtype=BF16)
    dhm = mm_nt(da1, W["w1"], name="mm_w1_dx", out_dtype=F32)
    dw["w1"] = mm_tn(s["hm"], da1, nb=4, name="mm_w1_dw", out_dtype=BF16)
    g2, ds["g_mlp"] = rms_bwd(s["x2"], P["g_mlp"], dhm, name="rms_d_bwd", res=g)
    do_x = mm_nt(g2, W["xo"], name="mm_xo_dx", out_dtype=BF16)
    dw["xo"] = mm_tn(s["o_x"], g2, nb=4, name="mm_xo_dw", out_dtype=BF16)
    dqx, dkx, dvx = attn_bwd(s["qx"], s["kx"], s["mkv"], lambda h: X_HEADS + h, s["o_x"], do_x, s["lse_x"],
                             mode="cross", name="attn_cross_bwd", n_heads=X_HEADS)
    dxq_raw, ds["g_x_q"] = prep_bwd_q(s["xq_raw"], dqx, P["g_x_q"], name="prep_xq_bwd", n_heads=X_HEADS,
                                      dh=DH, n_real=DH)
    dmkv, dgk = prep_bwd_groups(s["mkv"], 0, [dkx, dvx], [P["g_x_k"], None], name="prep_xkv_bwd",
                                n_heads=X_HEADS, kinds=("norm", "copy"))
    ds["g_x_k"] = dgk[0]
    dhq = mm_nt(dxq_raw, W["xq"], name="mm_xq_dx", out_dtype=F32)
    dw["xq"] = mm_tn(s["hq"], dxq_raw, nb=1, name="mm_xq_dw", out_dtype=BF16)
    dmem_n = mm_nt(dmkv, W["xkv"], name="mm_xkv_dx", out_dtype=F32)
    dw["xkv"] = mm_tn(s["mem_n"], dmkv, nb=1, name="mm_xkv_dw", out_dtype=BF16)
    _, ds["g_mem"] = rms_bwd(mem, P["g_mem"], dmem_n, name="rms_mem_bwd", need_dx=False)
    g1, ds["g_cross"] = rms_bwd(s["x1"], P["g_cross"], dhq, name="rms_d_bwd", res=g2)
    dmerged = mm_nt(g1, W["out"], name="mm_out_dx", out_dtype=F32)
    dw["out"] = mm_tn(s["merged"], g1, nb=1, name="mm_out_dw", out_dtype=BF16)
    dgl, dproj = merge_bwd(z, s["projs"], dmerged)
    outs = (s["o_a"], s["o_b"], s["o_c"])
    do = [mm_nt(dproj[n], W["br"][n], name="mm_br_dx", out_dtype=BF16) for n in range(3)]
    dw["br"] = [mm_tn(outs[n], dproj[n], nb=4, name="mm_br_dw", out_dtype=BF16) for n in range(3)]
    token = grads_done("late", {k: dw.pop(k) for k in LATE_GRADS})
    dqc, dkc, dvc, dtiles = attn_bwd(s["qc"], s["kc"], z, lambda h: CH_B + 2 * N_HEADS + h, s["o_c"], do[2],
                                     s["lse_c"], mode="chunk", name="attn_chunk_bwd", n_heads=N_HEADS,
                                     tiles=_after(s["tiles"], token))
    d_ch, dg_ch = prep_bwd_groups(z, CH_B, [dqc, dkc, dvc], [P["g_ch_q"], P["g_ch_k"], None],
                                  name="prep_h_bwd", n_heads=N_HEADS, kinds=("norm", "norm", "copy"))
    ds["g_ch_q"], ds["g_ch_k"] = dg_ch[0], dg_ch[1]
    ds["rel_bias"] = relbias_tiles_bwd(dtiles, P["rel_onehot"])
    token = grads_done("late_followup", d_ch)
    dqb, dkb, dvb, dcq, dck = attn_bwd(s["qb"], s["kb"], z, lambda h: FOX_B + 2 * N_HEADS + h, s["o_b"], do[1],
                                       s["lse_b"], mode="fox", name="attn_fox_bwd", n_heads=N_HEADS,
                                       cq=s["cq"], ck=_after(s["ck"], token))
    d_fox, dg_fox = prep_bwd_groups(z, FOX_B, [dqb, dkb, dvb], [P["g_fox_q"], P["g_fox_k"], None],
                                    name="prep_h_bwd", n_heads=N_HEADS, kinds=("norm", "norm", "copy"))
    ds["g_fox_q"], ds["g_fox_k"] = dg_fox[0], dg_fox[1]
    dcum = jnp.pad((dcq[:, :, 0] + dck[:, 0, :]).T, ((0, 0), (0, 128 - N_HEADS)))
    dff, dbf = fox_cum_bwd(z, P["b_f"], dcum)
    ds["b_f"] = dbf[:, :N_HEADS]
    dqa, dka, dva = attn_bwd(s["qa"], s["ka"], s["kv_raw"], lambda h: 2 * h + 1, s["o_a"], do[0], s["lse_a"],
                             mode="mla", name="attn_mla_bwd", n_heads=N_HEADS)
    dq_raw, dgq = prep_bwd_q(s["q_raw"], dqa, P["g_mla_q"], name="prep_mla_q_bwd", n_heads=N_HEADS,
                             dh=MLA_PAD, n_real=MLA_QK, rope=rope)
    dkv_raw, dkr, dgk = prep_bwd_mla_k(s["kv_raw"], z, dka, dva, P["g_mla_k"], rope, name="prep_mla_k_bwd")
    ds["g_mla_q"], ds["g_mla_k"] = dgq[:, :MLA_QK], dgk[:, :MLA_QK]
    dcq_n = mm_nt(dq_raw, W["uq_p"], name="mm_uq_dx", out_dtype=F32)
    dw["uq_p"] = mm_tn(s["cq_n"], dq_raw, nb=1, name="mm_uq_dw", out_dtype=BF16)
    dckv_n = mm_nt(dkv_raw, W["ukv"], name="mm_ukv_dx", out_dtype=F32)
    dw["ukv"] = mm_tn(s["ckv_n"], dkv_raw, nb=4, name="mm_ukv_dw", out_dtype=BF16)
    d_cq, ds["g_cq"] = rms_bwd(z, P["g_cq"], dcq_n, name="rms_cq_bwd", col0=Z_CQ, width=Q_LORA, dx_dtype=BF16)
    d_ckv, ds["g_ckv"] = rms_bwd(z, P["g_ckv"], dckv_n, name="rms_ckv_bwd", col0=Z_CKV, width=KV_LORA,
                                 dx_dtype=BF16)
    dz = jnp.concatenate([d_cq, d_ckv, dkr.astype(BF16), dff.astype(BF16), d_fox, d_ch, *dgl], axis=1)
    dw["in_p"] = mm_tn(s["h"], dz, nb=1, name="mm_in_dw", out_dtype=BF16)
    token = grads_done("early", dw)
    dh = mm_nt(dz, W["in_p"], name="mm_in_dx", out_dtype=F32, after=token)
    g0, ds["g_mix"] = rms_bwd(x, P["g_mix"], dh, name="rms_d_bwd", res=g1)
    return g0, ds


def local_step(x, mem, target, Ps, weights_of, grads_done):
    rope = rope_tables(x.shape[0])
    onehot = relbias_onehot()
    Ps = [dict(P, rel_onehot=onehot) for P in Ps]
    Ws, saved, xs = [], [], [x]
    for l, P in enumerate(Ps):
        W_early, mid_weights, last_weights, token = weights_of(l, xs[-1])
        y, s, W = layer_fwd(xs[-1], mem, W_early, dict(P, g_mix=_after(P["g_mix"], token)), rope,
                            mid_weights, last_weights)
        xs.append(y)
        saved.append(s)
        Ws.append(W)
    g, loss = loss_head(xs[-1], target)
    dss, token = [], None
    for l in reversed(range(len(Ps))):
        g, ds = layer_bwd(g, xs[l], mem, Ws[l], dict(Ps[l], g_mlp=_after(Ps[l]["g_mlp"], token)), rope,
                          saved[l], functools.partial(grads_done, l))
        token = grads_done(l, "early_followup", g)
        dss.append(ds)
    return loss, g, dss[::-1]


HBM_SPEC = pl.BlockSpec(memory_space=pltpu.HBM)


def _place():
    return lax.axis_index("x"), lax.axis_index("y"), lax.axis_index("c")


def _other_chips(x, y):
    return [(1 - x, y), (x, 1 - y), (1 - x, 1 - y)]


def _remote(src, dst, send_sems, recv_sems, k, to):
    return pltpu.make_async_remote_copy(src_ref=src, dst_ref=dst, send_sem=send_sems.at[k],
                                        recv_sem=recv_sems.at[k], device_id=to, device_id_type=MESH)


SEM_SPEC = pl.BlockSpec(memory_space=pltpu.SEMAPHORE)
SPLIT_EFFECT = pltpu.SideEffectType.DATAFLOW_SIDE_EFFECTING


COPIES_PER_ARRAY = {"gather": 3, "scatter": 3, "halves": 1, "share": 1}


def _chip_exchange_copies(kind, srcs, lands, send_sems, recv_sems):
    x, y, c = _place()
    me = 2 * x + y
    cps = []
    for i, (src, land) in enumerate(zip(srcs, lands)):
        if kind == "halves":
            h = src.shape[1] // 2
            cps.append(_remote(src.at[:, pl.ds((1 - c) * h, h), :], land, send_sems, recv_sems, i, (x, y, 1 - c)))
            continue
        if kind == "share":
            cps.append(_remote(src, land, send_sems, recv_sems, i, (x, y, 1 - c)))
            continue
        for j, (cx, cy) in enumerate(_other_chips(x, y)):
            if kind == "gather":
                h = src.shape[0] // 2
                s_ref, d_ref = src.at[pl.ds(c * h, h), :], land.at[me, pl.ds(c * h, h), :]
            else:
                s_ref, d_ref = src.at[2 * cx + cy], land.at[me]
            cps.append(_remote(s_ref, d_ref, send_sems, recv_sems, 3 * i + j, (cx, cy, c)))
    return cps


def chip_exchange_start(kind, srcs, land_shapes, *, name, after=None):
    n = len(srcs)
    n_in = 2 * n + (after is not None)

    def body(*refs):
        send_sems, recv_sems = refs[n_in], refs[n_in + 1]
        for cp in _chip_exchange_copies(kind, refs[:n], refs[n:2 * n], send_sems, recv_sems):
            cp.start()
        refs[-1][...] = jnp.zeros_like(refs[-1])

    hbm = lambda a: pltpu.with_memory_space_constraint(a, pltpu.HBM)
    ins = [hbm(s) for s in srcs] + [hbm(lax.empty(s.shape, s.dtype)) for s in land_shapes]
    sems = pltpu.SemaphoreType.DMA((COPIES_PER_ARRAY[kind] * n,))
    out = pl.pallas_call(
        body, name=name, interpret=False,
        out_shape=(sems, sems, *[pltpu.HBM(a.shape, a.dtype) for a in ins], _sds((8, 128), F32)),
        in_specs=[HBM_SPEC] * (2 * n) + [pl.BlockSpec(memory_space=pl.ANY)] * (after is not None),
        out_specs=(SEM_SPEC, SEM_SPEC, *[HBM_SPEC] * (2 * n), pl.BlockSpec(memory_space=pltpu.VMEM)),
        input_output_aliases={i: 2 + i for i in range(2 * n)},
        compiler_params=pltpu.CompilerParams(has_side_effects=SPLIT_EFFECT))(
            *ins, *([after] if after is not None else []))
    return out[0], out[1], list(out[2:2 + n]), list(out[2 + n:2 + 2 * n]), out[-1]


def chip_exchange_wait(kind, started, after, *, name):
    send_sems, recv_sems, srcs, lands, _ = started
    n = len(srcs)

    def body(*refs):
        send_sems, recv_sems = refs[2 * n], refs[2 * n + 1]
        for cp in _chip_exchange_copies(kind, refs[:n], refs[n:2 * n], send_sems, recv_sems):
            cp.wait_send()
            cp.wait_recv()

    thru = srcs + lands
    out = pl.pallas_call(
        body, name=name, interpret=False,
        out_shape=tuple(pltpu.HBM(a.shape, a.dtype) for a in thru),
        in_specs=[HBM_SPEC] * (2 * n) + [SEM_SPEC, SEM_SPEC, pl.BlockSpec(memory_space=pl.ANY)],
        out_specs=tuple([HBM_SPEC] * (2 * n)),
        input_output_aliases={i: i for i in range(2 * n)},
        compiler_params=pltpu.CompilerParams(has_side_effects=SPLIT_EFFECT))(*thru, send_sems, recv_sems, after)
    return list(out[:n]), list(out[n:])


def gather_to_sibling(lands):
    n = len(lands)

    def body(*refs):
        outs = refs[n:2 * n]
        send_sems, recv_sems = refs[2 * n:]
        x, y, c = _place()
        sends = []
        for i in range(n):
            h = outs[i].shape[1] // 2
            for j, (cx, cy) in enumerate(_other_chips(x, y)):
                landed = outs[i].at[2 * cx + cy, pl.ds(c * h, h), :]
                cp = _remote(landed, landed, send_sems, recv_sems, 3 * i + j, (x, y, 1 - c))
                cp.start()
                sends.append(cp)
        for cp in sends:
            cp.wait()

    return pl.pallas_call(
        body, out_shape=tuple(_sds(a.shape, a.dtype) for a in lands), in_specs=[HBM_SPEC] * n,
        out_specs=tuple([HBM_SPEC] * n), input_output_aliases={i: i for i in range(n)},
        scratch_shapes=[pltpu.SemaphoreType.DMA((3 * n,)), pltpu.SemaphoreType.DMA((3 * n,))],
        name="gather_to_sibling", interpret=False)(*lands)


def add_pair(part, got, half_idx):
    _, a, b = part.shape
    h = a // 2
    tr = _pick(h, (512, 256, 128))

    def body(c_ref, p_ref, g_ref, o_ref):
        o_ref[...] = (p_ref[...].astype(F32) + g_ref[...].astype(F32)).astype(o_ref.dtype)

    spec = pl.BlockSpec((None, tr, b), lambda s_, i, c_ref: (s_, i, 0))
    grid_spec = pltpu.PrefetchScalarGridSpec(
        num_scalar_prefetch=1, grid=(4, h // tr),
        in_specs=[pl.BlockSpec((None, None, tr, b), lambda s_, i, c_ref: (s_, c_ref[0], i, 0)), spec],
        out_specs=spec)
    return pl.pallas_call(
        body, out_shape=_sds((4, h, b), part.dtype), grid_spec=grid_spec, name="rs_add_pair", interpret=False,
        compiler_params=pltpu.CompilerParams(vmem_limit_bytes=VMEM_LIMIT_BYTES))(
            half_idx, part.reshape(4, 2, h, b), got)


def sum_slots(r):
    _, h, b = r.shape
    tr = _pick(h, (512, 256, 128))

    def body(r0, r1, r2, r3, o_ref):
        o_ref[...] = ((r0[...].astype(F32) + r1[...].astype(F32)) + r2[...].astype(F32)) + r3[...].astype(F32)

    specs = [pl.BlockSpec((None, tr, b), functools.partial(lambda i, s_: (s_, i, 0), s_=s_)) for s_ in range(4)]
    return _pcall(body, name="rs_sum_slots", out_shape=_sds((h, b), F32), grid=(h // tr,),
                  in_specs=specs, out_specs=pl.BlockSpec((tr, b), lambda i: (i, 0)))(r, r, r, r)


def reduce_halves_start(parts, *, tag):
    return chip_exchange_start("halves", parts, [_sds((4, p.shape[1] // 2, p.shape[2]), p.dtype) for p in parts],
                               name="halves_start_" + tag)


def reduce_scatter_start(halves_started, after, core, *, tag):
    parts, gots = chip_exchange_wait("halves", halves_started, after, name="halves_wait_" + tag)
    half_idx = core.reshape(1).astype(jnp.int32)
    chip_sums = [add_pair(p, g, half_idx) for p, g in zip(parts, gots)]
    return chip_exchange_start("scatter", chip_sums, chip_sums, name="scatter_start_" + tag)


def reduce_share_start(scatter_started, after, chip, *, tag):
    chip_sums, slots = chip_exchange_wait("scatter", scatter_started, after, name="scatter_wait_" + tag)
    slots = [lax.dynamic_update_slice(s_, lax.dynamic_index_in_dim(q, chip, 0, keepdims=True), (chip, 0, 0))
             for s_, q in zip(slots, chip_sums)]
    halves = [sum_slots(s_) for s_ in slots]
    return chip_exchange_start("share", halves, [_sds(t.shape, t.dtype) for t in halves], name="share_start_" + tag)


def reduce_finish(share_started, after, *, tag):
    halves, theirs = chip_exchange_wait("share", share_started, after, name="share_wait_" + tag)
    return list(zip(halves, theirs))


def allreduce_small(v):
    Rs = v.shape[0]

    def body(v_ref, o_ref, buf, send_sems, recv_sems):
        x, y, c = _place()
        me = 4 * x + 2 * y + c
        buf[me] = v_ref[...]
        flips = [(fx, fy, fc) for fx in (0, 1) for fy in (0, 1) for fc in (0, 1)][1:]
        sends = []
        for k, (fx, fy, fc) in enumerate(flips):
            to = ((1 - x) if fx else x, (1 - y) if fy else y, (1 - c) if fc else c)
            cp = _remote(v_ref, buf.at[me], send_sems, recv_sems, k, to)
            cp.start()
            sends.append(cp)
        for cp in sends:
            cp.wait()
        acc = buf[0]
        for d in range(1, 8):
            acc = acc + buf[d]
        o_ref[...] = acc

    vm = pl.BlockSpec(memory_space=pltpu.VMEM)
    return pl.pallas_call(
        body, out_shape=_sds((Rs, 128), F32), in_specs=[vm], out_specs=vm,
        scratch_shapes=[pltpu.VMEM((8, Rs, 128), F32), pltpu.SemaphoreType.DMA((7,)),
                        pltpu.SemaphoreType.DMA((7,))],
        name="allreduce_small", interpret=False)(v)


INPUT_NAMES = (("x", "mem") + WEIGHT_ORDER + ("loss_target",) + tuple("m_" + n for n in WEIGHT_ORDER)
               + tuple("v_" + n for n in WEIGHT_ORDER))


def _pack_small(vals, n_layers, extra=None):
    flat = jnp.concatenate([vals[n].reshape(n_layers, -1).astype(F32) for n in SMALL_ORDER], axis=1).reshape(-1)
    if extra is not None:
        flat = jnp.concatenate([flat, extra.reshape(-1)])
    n = flat.shape[0]
    rows = -(-n // 1024) * 8
    return jnp.pad(flat, (0, rows * 128 - n)).reshape(rows, 128)


def _unpack_small(packed, like, n_layers):
    per_layer = sum(int(np.prod(like[n].shape[1:])) for n in SMALL_ORDER)
    body = packed.reshape(-1)[:n_layers * per_layer].reshape(n_layers, per_layer)
    out, off = {}, 0
    for n in SMALL_ORDER:
        k = int(np.prod(like[n].shape[1:]))
        out[n] = body[:, off:off + k].reshape(like[n].shape)
        off += k
    return out, packed.reshape(-1)[n_layers * per_layer]


def kernel(x, mem, g_mix, w_in, g_cq, w_uq, g_ckv, w_ukv, g_mla_q, g_mla_k, b_f, g_fox_q, g_fox_k, rel_bias, g_ch_q, g_ch_k, w_br, w_out, g_cross, g_mem, w_xq, w_xkv, g_x_q, g_x_k, w_xo, g_mlp, w_1, w_2, loss_target, m_g_mix, m_w_in, m_g_cq, m_w_uq, m_g_ckv, m_w_ukv, m_g_mla_q, m_g_mla_k, m_b_f, m_g_fox_q, m_g_fox_k, m_rel_bias, m_g_ch_q, m_g_ch_k, m_w_br, m_w_out, m_g_cross, m_g_mem, m_w_xq, m_w_xkv, m_g_x_q, m_g_x_k, m_w_xo, m_g_mlp, m_w_1, m_w_2, v_g_mix, v_w_in, v_g_cq, v_w_uq, v_g_ckv, v_w_ukv, v_g_mla_q, v_g_mla_k, v_b_f, v_g_fox_q, v_g_fox_k, v_rel_bias, v_g_ch_q, v_g_ch_k, v_w_br, v_w_out, v_g_cross, v_g_mem, v_w_xq, v_w_xkv, v_g_x_q, v_g_x_k, v_w_xo, v_g_mlp, v_w_1, v_w_2):
    d = dict(zip(INPUT_NAMES, (x, mem, g_mix, w_in, g_cq, w_uq, g_ckv, w_ukv, g_mla_q, g_mla_k, b_f, g_fox_q, g_fox_k, rel_bias, g_ch_q, g_ch_k, w_br, w_out, g_cross, g_mem, w_xq, w_xkv, g_x_q, g_x_k, w_xo, g_mlp, w_1, w_2, loss_target, m_g_mix, m_w_in, m_g_cq, m_w_uq, m_g_ckv, m_w_ukv, m_g_mla_q, m_g_mla_k, m_b_f, m_g_fox_q, m_g_fox_k, m_rel_bias, m_g_ch_q, m_g_ch_k, m_w_br, m_w_out, m_g_cross, m_g_mem, m_w_xq, m_w_xkv, m_g_x_q, m_g_x_k, m_w_xo, m_g_mlp, m_w_1, m_w_2, v_g_mix, v_w_in, v_g_cq, v_w_uq, v_g_ckv, v_w_ukv, v_g_mla_q, v_g_mla_k, v_b_f, v_g_fox_q, v_g_fox_k, v_rel_bias, v_g_ch_q, v_g_ch_k, v_w_br, v_w_out, v_g_cross, v_g_mem, v_w_xq, v_w_xkv, v_g_x_q, v_g_x_k, v_w_xo, v_g_mlp, v_w_1, v_w_2)))
    n_layers = g_mix.shape[0]
    assert x.shape[0] == 1 and x.shape[2] == D_MODEL and mem.shape[1:] == (MEM_LEN, D_MODEL)
    for n in PACK_ORDER:
        assert d[n].shape[1:] == SHARD_SHAPES[n], (n, d[n].shape)

    chip = 2 * lax.axis_index("x") + lax.axis_index("y")
    core = lax.axis_index("c")

    def start_gather(l, names, after):
        mine = [_rows2d(d[n][l].astype(BF16), n) for n in names]
        return chip_exchange_start("gather", mine, [_sds((4,) + m_.shape, BF16) for m_ in mine],
                                   name=f"gather_start_{l}_{names[0]}", after=after)

    def finish_gather(l, names, started, after):
        mine, lands = chip_exchange_wait("gather", started, after, name=f"gather_wait_{l}_{names[0]}")
        full = [lax.dynamic_update_slice(t, m_[None], (chip, 0, 0)) for t, m_ in zip(gather_to_sibling(lands), mine)]
        return mine, full_weights({n: f.reshape((4,) + SHARD_SHAPES[n]) for n, f in zip(names, full)})

    in_flight = {"early": start_gather(0, EARLY_WEIGHTS, x)}

    def weights_of(l, x_l):
        mine, early = finish_gather(l, EARLY_WEIGHTS, in_flight.pop("early"), x_l)
        in_flight["mid"] = start_gather(l, MID_WEIGHTS, mine[0])
        in_flight["last"] = start_gather(l, LAST_WEIGHTS, in_flight["mid"][4])

        def mid_weights(after):
            mine, mid = finish_gather(l, MID_WEIGHTS, in_flight.pop("mid"), after)
            if l + 1 == n_layers:
                return mid, None
            in_flight["early"] = start_gather(l + 1, EARLY_WEIGHTS, mine[0])
            return mid, in_flight["early"][4]

        def last_weights(after):
            return finish_gather(l, LAST_WEIGHTS, in_flight.pop("last"), after)[1], None

        return early, mid_weights, last_weights, in_flight["mid"][4] + in_flight["last"][4]

    halves, scatters = {}, []

    def grads_done(l, stage, arg):
        names = LATE_WEIGHTS if stage.startswith("late") else EARLY_WEIGHTS
        tag = f"{l}_{names[0]}"
        if not stage.endswith("followup"):
            sg = shard_grads(arg)
            halves[tag] = reduce_halves_start([_rows2d(sg[n], n) for n in names], tag=tag)
            if (l, stage) != (0, "early"):
                return halves[tag][4]
            arg = halves[tag][4]
        elif tag not in halves:
            return None
        started = reduce_scatter_start(halves.pop(tag), arg, core, tag=tag)
        scatters.append((l, names, tag, started))
        return started[4]

    Ps = [layer_params(d, l) for l in range(n_layers)]
    loss, dx, dss = local_step(x[0], mem[0], loss_target[0], Ps, weights_of, grads_done)
    shares = [(l, names, tag, reduce_share_start(started, dx, chip, tag=tag))
              for l, names, tag, started in scatters]
    big = [{} for _ in range(n_layers)]
    for l, names, tag, started in shares:
        big[l].update(zip(names, reduce_finish(started, shares[-1][3][4], tag=tag)))

    small_local = {n: jnp.stack([dss[l][n].reshape(d[n].shape[1:]) for l in range(n_layers)]) for n in SMALL_ORDER}
    small_sum, loss_sum = _unpack_small(allreduce_small(_pack_small(small_local, n_layers, extra=loss)),
                                        {n: d[n] for n in SMALL_ORDER}, n_layers)

    grads, delta, new_m, new_v = {}, {}, {}, {}
    for n in PACK_ORDER:
        outs = adamw_halves(_rows2d(d[n], n), [big[l][n] for l in range(n_layers)], _rows2d(d["m_" + n], n),
                            _rows2d(d["v_" + n], n), core.reshape(1).astype(jnp.int32), name="adamw_" + n)
        grads[n], delta[n], new_m[n], new_v[n] = (o.reshape(d[n].shape) for o in outs)
    like = {n: d[n] for n in SMALL_ORDER}
    sm = adamw(_pack_small(like, n_layers)[None], [_pack_small(small_sum, n_layers)],
               _pack_small({n: d["m_" + n] for n in SMALL_ORDER}, n_layers)[None],
               _pack_small({n: d["v_" + n] for n in SMALL_ORDER}, n_layers)[None], name="adamw_small")
    for res, src in zip((grads, delta, new_m, new_v), sm):
        res.update(_unpack_small(src[0], like, n_layers)[0])

    return (loss_sum, dx[None], *[grads[n] for n in WEIGHT_ORDER], *[delta[n] for n in WEIGHT_ORDER],
            *[new_m[n] for n in WEIGHT_ORDER], *[new_v[n] for n in WEIGHT_ORDER])
```

```python
import functools

import numpy as np
import jax
import jax.numpy as jnp
from jax import lax
from jax.experimental import pallas as pl
from jax.experimental.pallas import tpu as pltpu

F32 = jnp.float32
BF16 = jnp.bfloat16
MXU_DTYPE = jnp.bfloat16
MESH = pl.DeviceIdType.MESH

D_MODEL = 2048
MIX_W = 1024
N_HEADS = 8
DH = 128
MLA_NOPE = 128
MLA_ROPE = 64
MLA_QK = MLA_NOPE + MLA_ROPE
MLA_PAD = 256
Q_LORA = 512
KV_LORA = 256
CHUNK = 64
CHUNK_SHIFT = CHUNK.bit_length() - 1
LEFT_CHUNKS = 8
REL_CLIP = 128
N_REL = 2 * REL_CLIP + 1
X_HEADS = 4
MEM_LEN = 256
D_FF = 8192
ROPE_THETA = 10000.0
EPS = 1e-6
NEG = -1e30

Z_CQ, Z_CKV, Z_KR, Z_FF = 0, 512, 768, 896
ZS_W = 1024
Z_FOX = 1024
Z_CH = Z_FOX + 3 * MIX_W
Z_GATE = Z_CH + 3 * MIX_W
Z_TOT = Z_GATE + 3 * D_MODEL
W_IN_CUTS = (0, 512, 768, 832, 3904, 3912, 6984, 13128)
W_IN_SHARD = W_IN_CUTS[-1] // 4
W_IN_PIECES = ((0, 0, 832), (Z_FF, 3904, 8), (Z_FOX, 832, 3072), (Z_CH, 3912, 9216))

ADAM_LR, ADAM_B1, ADAM_B2, ADAM_EPS, ADAM_WD, ADAM_STEP = 0.001, 0.9, 0.999, 1e-08, 0.01, 10

VMEM_LIMIT_BYTES = 56 * 1024 * 1024
PACK_ORDER = ("w_uq", "w_ukv", "w_br", "w_out", "w_xq", "w_xkv", "w_xo", "w_1", "w_2", "w_in")
SMALL_ORDER = ("g_mix", "g_cq", "g_ckv", "g_mla_q", "g_mla_k", "b_f", "g_fox_q", "g_fox_k", "rel_bias",
               "g_ch_q", "g_ch_k", "g_cross", "g_mem", "g_x_q", "g_x_k", "g_mlp")
WEIGHT_ORDER = ("g_mix", "w_in", "g_cq", "w_uq", "g_ckv", "w_ukv", "g_mla_q", "g_mla_k", "b_f", "g_fox_q",
                "g_fox_k", "rel_bias", "g_ch_q", "g_ch_k", "w_br", "w_out", "g_cross", "g_mem", "w_xq",
                "w_xkv", "g_x_q", "g_x_k", "w_xo", "g_mlp", "w_1", "w_2")


def _pick(n, prefs):
    for p in prefs:
        if n % p == 0:
            return p
    raise ValueError(f"no block size among {prefs} divides {n}")


def _pcall(body, *, name, out_shape, in_specs, out_specs, grid=(), scratch=(), aliases=None):
    return pl.pallas_call(
        body, out_shape=out_shape, grid=grid, in_specs=in_specs, out_specs=out_specs,
        scratch_shapes=scratch, name=name, interpret=False,
        input_output_aliases=aliases or {},
        compiler_params=pltpu.CompilerParams(vmem_limit_bytes=VMEM_LIMIT_BYTES))


def _sds(shape, dtype):
    return jax.ShapeDtypeStruct(tuple(shape), dtype)


def _mx(v):
    return v.astype(MXU_DTYPE)


def mm_nn(a, b3, *, name, out_dtype, a_col0=0, res=None, relu2=False):
    M = a.shape[0]
    nb, K, Ns = b3.shape
    N = nb * Ns
    tm = _pick(M, (1024, 512, 256, 128))
    tk = _pick(K, (2048, 1024, 512, 256))
    tn = _pick(Ns, (1024, 512, 256, 128))
    assert a_col0 % tk == 0
    nk, nbs, ka0 = K // tk, Ns // tn, a_col0 // tk
    n_out = 2 if relu2 else 1

    def body(*refs):
        a_ref, b_ref = refs[0], refs[1]
        pos = 2
        res_ref = None
        if res is not None:
            res_ref = refs[pos]
            pos += 1
        outs = refs[pos:pos + n_out]
        acc_ref = refs[pos + n_out] if nk > 1 else None
        part = jnp.dot(_mx(a_ref[...]), _mx(b_ref[...]), preferred_element_type=F32)

        def finish(acc):
            if res_ref is not None:
                acc = acc + res_ref[...]
            outs[0][...] = acc.astype(outs[0].dtype)
            if relu2:
                r = jnp.maximum(acc, 0.0)
                outs[1][...] = (r * r).astype(outs[1].dtype)

        if nk == 1:
            finish(part)
        else:
            k = pl.program_id(2)

            @pl.when(k == 0)
            def _():
                acc_ref[...] = part

            @pl.when(k > 0)
            def _():
                acc_ref[...] += part

            @pl.when(k == nk - 1)
            def _():
                finish(acc_ref[...])

    in_specs = [pl.BlockSpec((tm, tk), lambda i, j, k: (i, ka0 + k)),
                pl.BlockSpec((None, tk, tn), lambda i, j, k: (j // nbs, k, j % nbs))]
    args = [a, b3]
    if res is not None:
        in_specs.append(pl.BlockSpec((tm, tn), lambda i, j, k: (i, j)))
        args.append(res)
    o_spec = pl.BlockSpec((tm, tn), lambda i, j, k: (i, j))
    if relu2:
        out_shape, out_specs = (_sds((M, N), out_dtype), _sds((M, N), out_dtype)), (o_spec, o_spec)
    else:
        out_shape, out_specs = _sds((M, N), out_dtype), o_spec
    scratch = (pltpu.VMEM((tm, tn), F32),) if nk > 1 else ()
    return _pcall(body, name=name, out_shape=out_shape, grid=(M // tm, N // tn, nk),
                  in_specs=in_specs, out_specs=out_specs, scratch=scratch)(*args)


def mm_nt(a, b3, *, name, out_dtype, a_col0=0, res=None, relu_mul=None, after=None):
    M = a.shape[0]
    nb, K, Ns = b3.shape
    tm = _pick(M, (1024, 512, 256, 128))
    tk = _pick(K, (1024, 512, 256))
    tn = _pick(Ns, (2048, 1024, 512, 256, 128) if nb == 1 and Ns <= 2048 else (1024, 512, 256, 128))
    assert a_col0 % tn == 0
    nbs = Ns // tn
    nn, a0 = nb * nbs, a_col0 // tn

    def body(*refs):
        a_ref, b_ref = refs[0], refs[1]
        pos = 2
        mul_ref = res_ref = None
        if relu_mul is not None:
            mul_ref = refs[pos]
            pos += 1
        if res is not None:
            res_ref = refs[pos]
            pos += 1
        pos += after is not None
        o_ref = refs[pos]
        acc_ref = refs[pos + 1] if nn > 1 else None
        part = lax.dot_general(_mx(a_ref[...]), _mx(b_ref[...]), (((1,), (1,)), ((), ())),
                               preferred_element_type=F32)

        def finish(acc):
            if mul_ref is not None:
                acc = acc * (2.0 * jnp.maximum(mul_ref[...].astype(F32), 0.0))
            if res_ref is not None:
                acc = acc + res_ref[...]
            o_ref[...] = acc.astype(o_ref.dtype)

        if nn == 1:
            finish(part)
        else:
            j = pl.program_id(2)

            @pl.when(j == 0)
            def _():
                acc_ref[...] = part

            @pl.when(j > 0)
            def _():
                acc_ref[...] += part

            @pl.when(j == nn - 1)
            def _():
                finish(acc_ref[...])

    in_specs = [pl.BlockSpec((tm, tn), lambda i, kk, j: (i, a0 + j)),
                pl.BlockSpec((None, tk, tn), lambda i, kk, j: (j // nbs, kk, j % nbs))]
    args = [a, b3]
    for extra in (relu_mul, res):
        if extra is not None:
            in_specs.append(pl.BlockSpec((tm, tk), lambda i, kk, j: (i, kk)))
            args.append(extra)
    if after is not None:
        in_specs.append(pl.BlockSpec(memory_space=pl.ANY))
        args.append(after)
    scratch = (pltpu.VMEM((tm, tk), F32),) if nn > 1 else ()
    return _pcall(body, name=name, out_shape=_sds((M, K), out_dtype), grid=(M // tm, K // tk, nn),
                  in_specs=in_specs, out_specs=pl.BlockSpec((tm, tk), lambda i, kk, j: (i, kk)),
                  scratch=scratch)(*args)


def mm_tn(a, c, *, nb, name, out_dtype, K=None, N=None, a_col0=0, c_col0=0):
    M = a.shape[0]
    K = K or a.shape[1]
    N = N or c.shape[1]
    Ns = N // nb
    tm = _pick(M, (2048, 1024, 512, 256))
    tk = _pick(K, (1024, 512, 256))
    tn = _pick(Ns, (1024, 512, 256, 128))
    assert a_col0 % tk == 0 and c_col0 % tn == 0
    nm, nbs, a0, c0 = M // tm, Ns // tn, a_col0 // tk, c_col0 // tn

    def body(*refs):
        a_ref, c_ref, o_ref = refs[:3]
        acc_ref = refs[3] if nm > 1 else None
        part = lax.dot_general(_mx(a_ref[...]), _mx(c_ref[...]), (((0,), (0,)), ((), ())),
                               preferred_element_type=F32)
        if nm == 1:
            o_ref[...] = part.astype(o_ref.dtype)
        else:
            m = pl.program_id(2)

            @pl.when(m == 0)
            def _():
                acc_ref[...] = part

            @pl.when(m > 0)
            def _():
                acc_ref[...] += part

            @pl.when(m == nm - 1)
            def _():
                o_ref[...] = acc_ref[...].astype(o_ref.dtype)

    scratch = (pltpu.VMEM((tk, tn), F32),) if nm > 1 else ()
    return _pcall(
        body, name=name, out_shape=_sds((nb, K, Ns), out_dtype), grid=(N // tn, K // tk, nm),
        in_specs=[pl.BlockSpec((tm, tk), lambda j, kk, m: (m, a0 + kk)),
                  pl.BlockSpec((tm, tn), lambda j, kk, m: (m, c0 + j))],
        out_specs=pl.BlockSpec((None, tk, tn), lambda j, kk, m: (j // nbs, kk, j % nbs)),
        scratch=scratch)(a, c)


def rms_fwd(x, g, *, name, col0=0, width=None, out_dtype=BF16):
    R = x.shape[0]
    width = width or x.shape[1]
    assert col0 % width == 0
    cb = col0 // width
    tr = _pick(R, (512, 256, 128))

    def body(x_ref, g_ref, o_ref):
        xf = x_ref[...].astype(F32)
        r = lax.rsqrt(jnp.mean(xf * xf, axis=1, keepdims=True) + EPS)
        o_ref[...] = (xf * r * g_ref[...]).astype(o_ref.dtype)

    return _pcall(body, name=name, out_shape=_sds((R, width), out_dtype), grid=(R // tr,),
                  in_specs=[pl.BlockSpec((tr, width), lambda i: (i, cb)),
                            pl.BlockSpec((1, width), lambda i: (0, 0))],
                  out_specs=pl.BlockSpec((tr, width), lambda i: (i, 0)))(x, g)


def rms_bwd(x, g, dy, *, name, col0=0, width=None, res=None, dx_dtype=F32, need_dx=True):
    R = x.shape[0]
    width = width or x.shape[1]
    cb = col0 // width
    tr = _pick(R, (512, 256, 128))

    def body(*refs):
        x_ref, g_ref, dy_ref = refs[:3]
        pos = 3
        res_ref = None
        if res is not None:
            res_ref = refs[pos]
            pos += 1
        dx_ref = None
        if need_dx:
            dx_ref = refs[pos]
            pos += 1
        dg_ref = refs[pos]
        xf = x_ref[...].astype(F32)
        dyf = dy_ref[...].astype(F32)
        r = lax.rsqrt(jnp.mean(xf * xf, axis=1, keepdims=True) + EPS)
        xh = xf * r
        if need_dx:
            gy = dyf * g_ref[...]
            dx = r * (gy - xh * jnp.mean(gy * xh, axis=1, keepdims=True))
            if res_ref is not None:
                dx = dx + res_ref[...]
            dx_ref[...] = dx.astype(dx_ref.dtype)
        part = jnp.sum(dyf * xh, axis=0, keepdims=True)

        @pl.when(pl.program_id(0) == 0)
        def _():
            dg_ref[...] = part

        @pl.when(pl.program_id(0) > 0)
        def _():
            dg_ref[...] += part

    in_specs = [pl.BlockSpec((tr, width), lambda i: (i, cb)),
                pl.BlockSpec((1, width), lambda i: (0, 0)),
                pl.BlockSpec((tr, width), lambda i: (i, 0))]
    args = [x, g, dy]
    if res is not None:
        in_specs.append(pl.BlockSpec((tr, width), lambda i: (i, 0)))
        args.append(res)
    dg_shape, dg_spec = _sds((1, width), F32), pl.BlockSpec((1, width), lambda i: (0, 0))
    if need_dx:
        out_shape = (_sds((R, width), dx_dtype), dg_shape)
        out_specs = (pl.BlockSpec((tr, width), lambda i: (i, 0)), dg_spec)
    else:
        out_shape, out_specs = dg_shape, dg_spec
    out = _pcall(body, name=name, out_shape=out_shape, grid=(R // tr,), in_specs=in_specs,
                 out_specs=out_specs)(*args)
    return out if need_dx else (None, out)


HEAD_ROW_BLOCKS = (2048, 1024, 512, 256, 128)


def _rope_apply(y, c, sa, sb):
    return y * c + pltpu.roll(y, 96, 1) * sa + pltpu.roll(y, 32, 1) * sb


def _rope_transpose(dy, c, sa, sb):
    return dy * c + pltpu.roll(dy * sa, 32, 1) + pltpu.roll(dy * sb, 96, 1)


def rope_tables(seq):
    pos = jnp.arange(seq, dtype=F32)
    inv = ROPE_THETA ** (-jnp.arange(0, MLA_ROPE, 2, dtype=F32) / MLA_ROPE)
    ang = pos[:, None] * inv[None, :]
    cos, sin = jnp.cos(ang), jnp.sin(ang)
    z32, z64 = jnp.zeros_like(cos), jnp.zeros((seq, 64), F32)
    c = jnp.concatenate([cos, cos, z64], axis=1)
    sa = jnp.concatenate([-sin, z32, z64], axis=1)
    sb = jnp.concatenate([z32, sin, z64], axis=1)
    return c, sa, sb


def _head_vec(part_refs):
    xs = [p[...].astype(F32) for p in part_refs]
    return xs[0] if len(xs) == 1 else jnp.concatenate(xs, axis=1)


def prep_fwd(parts, g, *, name, n_heads, n_real, rope=None):
    rows = parts[0][0].shape[0]
    dh = sum(w for _, w, _ in parts)
    tr = _pick(rows, HEAD_ROW_BLOCKS)
    npart = len(parts)

    def body(*refs):
        part_refs, g_ref = refs[:npart], refs[npart]
        pos = npart + 1
        if rope is not None:
            c_ref, sa_ref, sb_ref = refs[pos:pos + 3]
            pos += 3
        o_ref = refs[pos]
        x = _head_vec(part_refs)
        r = lax.rsqrt(jnp.sum(x * x, axis=1, keepdims=True) * (1.0 / n_real) + EPS)
        y = x * r * g_ref[...]
        if rope is not None:
            yr = _rope_apply(y[:, dh - 128:], c_ref[...], sa_ref[...], sb_ref[...])
            y = jnp.concatenate([y[:, :dh - 128], yr], axis=1)
        o_ref[...] = y.astype(o_ref.dtype)

    in_specs, args = [], []
    for arr, w, fn in parts:
        in_specs.append(pl.BlockSpec((tr, w), functools.partial(lambda h, i, fn: (i, fn(h)), fn=fn)))
        args.append(arr)
    in_specs.append(pl.BlockSpec((1, dh), lambda h, i: (0, 0)))
    args.append(g)
    if rope is not None:
        for t in rope:
            in_specs.append(pl.BlockSpec((tr, 128), lambda h, i: (i, 0)))
            args.append(t)
    return _pcall(body, name=name, out_shape=_sds((n_heads, rows, dh), BF16), grid=(n_heads, rows // tr),
                  in_specs=in_specs, out_specs=pl.BlockSpec((None, tr, dh), lambda h, i: (h, i, 0)))(*args)


def _norm_bwd(x, g, dyn, n_real):
    r = lax.rsqrt(jnp.sum(x * x, axis=1, keepdims=True) * (1.0 / n_real) + EPS)
    xh = x * r
    gy = dyn * g
    dx = r * (gy - xh * (jnp.sum(gy * xh, axis=1, keepdims=True) * (1.0 / n_real)))
    return dx, jnp.sum(dyn * xh, axis=0, keepdims=True)


def prep_bwd_q(src, dy, g, *, name, n_heads, dh, n_real, rope=None, out_dtype=BF16):
    rows = src.shape[0]
    tr = _pick(rows, HEAD_ROW_BLOCKS)

    def body(*refs):
        x_ref, dy_ref, g_ref = refs[:3]
        pos = 3
        if rope is not None:
            c_ref, sa_ref, sb_ref = refs[pos:pos + 3]
            pos += 3
        dx_ref, dg_ref = refs[pos], refs[pos + 1]
        dyn = dy_ref[...].astype(F32)
        if rope is not None:
            dr = _rope_transpose(dyn[:, dh - 128:], c_ref[...], sa_ref[...], sb_ref[...])
            dyn = jnp.concatenate([dyn[:, :dh - 128], dr], axis=1)
        dx, dg = _norm_bwd(x_ref[...].astype(F32), g_ref[...], dyn, n_real)
        dx_ref[...] = dx.astype(dx_ref.dtype)
        first = jnp.logical_and(pl.program_id(0) == 0, pl.program_id(1) == 0)

        @pl.when(first)
        def _():
            dg_ref[...] = dg

        @pl.when(jnp.logical_not(first))
        def _():
            dg_ref[...] += dg

    in_specs = [pl.BlockSpec((tr, dh), lambda i, h: (i, h)),
                pl.BlockSpec((None, tr, dh), lambda i, h: (h, i, 0)),
                pl.BlockSpec((1, dh), lambda i, h: (0, 0))]
    args = [src, dy, g]
    if rope is not None:
        for t in rope:
            in_specs.append(pl.BlockSpec((tr, 128), lambda i, h: (i, 0)))
            args.append(t)
    return _pcall(body, name=name, out_shape=(_sds((rows, n_heads * dh), out_dtype), _sds((1, dh), F32)),
                  grid=(rows // tr, n_heads), in_specs=in_specs,
                  out_specs=(pl.BlockSpec((tr, dh), lambda i, h: (i, h)),
                             pl.BlockSpec((1, dh), lambda i, h: (0, 0))))(*args)


def prep_bwd_mla_k(kv_raw, zs, dkf, dv, g, rope, *, name):
    rows = kv_raw.shape[0]
    tr = _pick(rows, HEAD_ROW_BLOCKS)

    def body(kn_ref, kr_ref, dy_ref, dv_ref, g_ref, c_ref, sa_ref, sb_ref, dkv_ref, dkr_ref, dg_ref):
        h = pl.program_id(1)
        x = jnp.concatenate([kn_ref[...].astype(F32), kr_ref[...].astype(F32)], axis=1)
        dyn = dy_ref[...].astype(F32)
        dr = _rope_transpose(dyn[:, 128:], c_ref[...], sa_ref[...], sb_ref[...])
        dyn = jnp.concatenate([dyn[:, :128], dr], axis=1)
        dx, dg = _norm_bwd(x, g_ref[...], dyn, MLA_QK)
        dkv_ref[...] = jnp.concatenate([dx[:, :128], dv_ref[...].astype(F32)], axis=1).astype(dkv_ref.dtype)

        @pl.when(h == 0)
        def _():
            dkr_ref[...] = dx[:, 128:]

        @pl.when(h > 0)
        def _():
            dkr_ref[...] += dx[:, 128:]

        first = jnp.logical_and(pl.program_id(0) == 0, h == 0)

        @pl.when(first)
        def _():
            dg_ref[...] = dg

        @pl.when(jnp.logical_not(first))
        def _():
            dg_ref[...] += dg

    tab = pl.BlockSpec((tr, 128), lambda i, h: (i, 0))
    return _pcall(
        body, name=name,
        out_shape=(_sds((rows, N_HEADS * 256), BF16), _sds((rows, 128), F32), _sds((1, MLA_PAD), F32)),
        grid=(rows // tr, N_HEADS),
        in_specs=[pl.BlockSpec((tr, 128), lambda i, h: (i, 2 * h)),
                  pl.BlockSpec((tr, 128), lambda i, h: (i, Z_KR // 128)),
                  pl.BlockSpec((None, tr, MLA_PAD), lambda i, h: (h, i, 0)),
                  pl.BlockSpec((None, tr, 128), lambda i, h: (h, i, 0)),
                  pl.BlockSpec((1, MLA_PAD), lambda i, h: (0, 0)), tab, tab, tab],
        out_specs=(pl.BlockSpec((tr, 256), lambda i, h: (i, h)),
                   pl.BlockSpec((tr, 128), lambda i, h: (i, 0)),
                   pl.BlockSpec((1, MLA_PAD), lambda i, h: (0, 0))))(kv_raw, zs, dkf, dv, g, *rope)


def prep_bwd_groups(src, base_blk, dys, gs, *, name, n_heads, kinds, out_dtype=BF16):
    rows = src.shape[0]
    ng = len(kinds)
    J = ng * n_heads
    tr = _pick(rows, HEAD_ROW_BLOCKS)
    gstack = jnp.stack([gs[k] if kinds[k] == "norm" else jnp.ones((1, DH), F32) for k in range(ng)])

    def body(*refs):
        x_ref = refs[0]
        dy_refs = refs[1:1 + ng]
        g_ref, dx_ref, dg_ref = refs[1 + ng:4 + ng]
        j, i = pl.program_id(0), pl.program_id(1)
        grp = j // n_heads
        dy = dy_refs[0][...].astype(F32)
        for k in range(1, ng):
            dy = jnp.where(grp == k, dy_refs[k][...].astype(F32), dy)
        dx, dg = _norm_bwd(x_ref[...].astype(F32), g_ref[...], dy, DH)
        is_copy = functools.reduce(jnp.logical_or, [grp == k for k in range(ng) if kinds[k] == "copy"],
                                   jnp.bool_(False))
        dx_ref[...] = jnp.where(is_copy, dy, dx).astype(dx_ref.dtype)
        dg = jnp.where(is_copy, jnp.zeros_like(dg), dg)
        first = jnp.logical_and(j % n_heads == 0, i == 0)

        @pl.when(first)
        def _():
            dg_ref[...] = dg

        @pl.when(jnp.logical_not(first))
        def _():
            dg_ref[...] += dg

    in_specs = [pl.BlockSpec((tr, DH), lambda j, i: (i, base_blk + j))]
    for k in range(ng):
        in_specs.append(pl.BlockSpec(
            (None, tr, DH),
            functools.partial(lambda j, i, k: (jnp.clip(j - k * n_heads, 0, n_heads - 1),
                                               jnp.where(j // n_heads == k, i, 0), 0), k=k)))
    in_specs.append(pl.BlockSpec((None, 1, DH), lambda j, i: (j // n_heads, 0, 0)))
    return _pcall(body, name=name, out_shape=(_sds((rows, J * DH), out_dtype), _sds((ng, 1, DH), F32)),
                  grid=(J, rows // tr), in_specs=in_specs,
                  out_specs=(pl.BlockSpec((tr, DH), lambda j, i: (i, j)),
                             pl.BlockSpec((None, 1, DH), lambda j, i: (j // n_heads, 0, 0))))(src, *dys, gstack)


def _attn_cfg(mode, sq, sk):
    if mode == "chunk":
        tq = 128
        win = min((LEFT_CHUNKS + 2) * CHUNK, sk)
    else:
        tq = _pick(sq, (256, 128))
        win = sk
    scale = (MLA_QK if mode == "mla" else DH) ** -0.5
    return tq, win, scale


def _attn_key_rows(mode, i, tq, win, sk, run):
    if mode == "chunk":
        start = pl.multiple_of(jnp.clip((i - LEFT_CHUNKS // 2) * 128, 0, sk - win), 128)
        run(pl.ds(start, win), start)
    elif mode == "cross":
        run(slice(0, sk), 0)
    else:
        lax.switch(i, [functools.partial(run, slice(0, (b + 1) * tq), 0) for b in range(sk // tq)])


def _attn_scores(mode, i, tq, scale, q, kk, start, cq, ck, t_ref):
    nk = kk.shape[0]
    s = lax.dot_general(q, kk, (((1,), (1,)), ((), ())), preferred_element_type=F32) * scale
    if mode == "cross":
        return s
    t_pos = i * tq + lax.broadcasted_iota(jnp.int32, (tq, nk), 0)
    s_pos = start + lax.broadcasted_iota(jnp.int32, (tq, nk), 1)
    if mode == "fox":
        s = s + cq - ck
        allowed = s_pos <= t_pos
    else:
        qc, kc = lax.shift_right_logical(t_pos, CHUNK_SHIFT), lax.shift_right_logical(s_pos, CHUNK_SHIFT)
        allowed = kc <= qc
        if mode == "chunk":
            allowed = jnp.logical_and(allowed, kc >= qc - LEFT_CHUNKS)
            tiles = []
            for w in range(nk // 128):
                delta = i - (start // 128 + w)
                tiles.append(jnp.where(delta == 0, t_ref[0], jnp.where(delta == 1, t_ref[1], t_ref[2])))
            s = s + jnp.concatenate(tiles, axis=1)
    return jnp.where(allowed, s, NEG)


def attn_fwd(q, k, v_arr, v_blk, *, mode, name, n_heads, cq=None, ck=None, tiles=None):
    _, sq, dk = q.shape
    sk = k.shape[1]
    tq, win, scale = _attn_cfg(mode, sq, sk)

    def body(*refs):
        q_ref, k_ref, v_ref = refs[:3]
        pos = 3
        cq_ref = ck_ref = t_ref = None
        if mode == "fox":
            cq_ref, ck_ref = refs[pos:pos + 2]
            pos += 2
        if mode == "chunk":
            t_ref = refs[pos]
            pos += 1
        o_ref, lse_ref = refs[pos], refs[pos + 1]
        i = pl.program_id(1)

        def run(rows, start):
            cq, ck = (cq_ref[...], ck_ref[:, rows]) if mode == "fox" else (None, None)
            s = _attn_scores(mode, i, tq, scale, q_ref[...], k_ref[rows, :], start, cq, ck, t_ref)
            m = jnp.max(s, axis=1, keepdims=True)
            e = jnp.exp(s - m)
            l = jnp.sum(e, axis=1, keepdims=True)
            p = e * (1.0 / l)
            o_ref[...] = jnp.dot(_mx(p), _mx(v_ref[rows, :]), preferred_element_type=F32).astype(o_ref.dtype)
            lse_ref[...] = m + jnp.log(l)

        _attn_key_rows(mode, i, tq, win, sk, run)

    in_specs = [pl.BlockSpec((None, tq, dk), lambda h, i: (h, i, 0)),
                pl.BlockSpec((None, sk, dk), lambda h, i: (h, 0, 0)),
                pl.BlockSpec((sk, DH), lambda h, i: (0, v_blk(h)))]
    args = [q, k, v_arr]
    if mode == "fox":
        in_specs += [pl.BlockSpec((None, tq, 1), lambda h, i: (h, i, 0)),
                     pl.BlockSpec((None, 1, sk), lambda h, i: (h, 0, 0))]
        args += [cq, ck]
    if mode == "chunk":
        in_specs.append(pl.BlockSpec((3, None, 128, 128), lambda h, i: (0, h, 0, 0)))
        args.append(tiles)
    return _pcall(body, name=name,
                  out_shape=(_sds((sq, n_heads * DH), BF16), _sds((n_heads, sq, 1), F32)),
                  grid=(n_heads, sq // tq), in_specs=in_specs,
                  out_specs=(pl.BlockSpec((tq, DH), lambda h, i: (i, h)),
                             pl.BlockSpec((None, tq, 1), lambda h, i: (h, i, 0))))(*args)


def attn_bwd(q, k, v_arr, v_blk, o, do, lse, *, mode, name, n_heads, cq=None, ck=None, tiles=None):
    _, sq, dk = q.shape
    sk = k.shape[1]
    tq, win, scale = _attn_cfg(mode, sq, sk)
    n_extra = {"fox": 2, "chunk": 1}.get(mode, 0)

    def body(*refs):
        q_ref, k_ref, v_ref, o_ref, do_ref, lse_ref = refs[:6]
        pos = 6
        cq_ref = ck_ref = t_ref = None
        if mode == "fox":
            cq_ref, ck_ref = refs[pos:pos + 2]
            pos += 2
        if mode == "chunk":
            t_ref = refs[pos]
            pos += 1
        dq_ref, dk_ref, dv_ref = refs[pos:pos + 3]
        extra = refs[pos + 3:pos + 3 + n_extra]
        i = pl.program_id(1)

        @pl.when(i == 0)
        def _():
            dk_ref[...] = jnp.zeros_like(dk_ref)
            dv_ref[...] = jnp.zeros_like(dv_ref)
            if mode == "fox":
                extra[1][...] = jnp.zeros_like(extra[1])
            if mode == "chunk":
                extra[0][...] = jnp.zeros_like(extra[0])

        def run(rows, start):
            q = q_ref[...]
            do = do_ref[...]
            kk = k_ref[rows, :]
            cq, ck = (cq_ref[...], ck_ref[:, rows]) if mode == "fox" else (None, None)
            s = _attn_scores(mode, i, tq, scale, q, kk, start, cq, ck, t_ref)
            p = jnp.exp(s - lse_ref[...])
            drow = jnp.sum(do.astype(F32) * o_ref[...].astype(F32), axis=1, keepdims=True)
            dp = lax.dot_general(do, _mx(v_ref[rows, :]), (((1,), (1,)), ((), ())), preferred_element_type=F32)
            ds = p * (dp - drow)
            dsb = _mx(ds)
            dq_ref[...] = (jnp.dot(dsb, kk, preferred_element_type=F32) * scale).astype(dq_ref.dtype)
            dk_ref[rows, :] += lax.dot_general(dsb, q, (((0,), (0,)), ((), ())),
                                               preferred_element_type=F32) * scale
            dv_ref[rows, :] += lax.dot_general(_mx(p), do, (((0,), (0,)), ((), ())), preferred_element_type=F32)
            if mode == "chunk":
                dt_ref = extra[0]
                for w in range(win // 128):
                    delta = i - (start // 128 + w)
                    tile = ds[:, w * 128:(w + 1) * 128]
                    zero = jnp.zeros_like(tile)
                    dt_ref[0] += jnp.where(delta == 0, tile, zero)
                    dt_ref[1] += jnp.where(delta == 1, tile, zero)
                    dt_ref[2] += jnp.where(delta >= 2, tile, zero)
            if mode == "fox":
                extra[0][...] = jnp.sum(ds, axis=1, keepdims=True)
                extra[1][:, rows] -= jnp.sum(ds, axis=0, keepdims=True)

        _attn_key_rows(mode, i, tq, win, sk, run)

    in_specs = [pl.BlockSpec((None, tq, dk), lambda h, i: (h, i, 0)),
                pl.BlockSpec((None, sk, dk), lambda h, i: (h, 0, 0)),
                pl.BlockSpec((sk, DH), lambda h, i: (0, v_blk(h))),
                pl.BlockSpec((tq, DH), lambda h, i: (i, h)),
                pl.BlockSpec((tq, DH), lambda h, i: (i, h)),
                pl.BlockSpec((None, tq, 1), lambda h, i: (h, i, 0))]
    args = [q, k, v_arr, o, do, lse]
    out_shape = [_sds((n_heads, sq, dk), F32), _sds((n_heads, sk, dk), F32), _sds((n_heads, sk, DH), F32)]
    out_specs = [pl.BlockSpec((None, tq, dk), lambda h, i: (h, i, 0)),
                 pl.BlockSpec((None, sk, dk), lambda h, i: (h, 0, 0)),
                 pl.BlockSpec((None, sk, DH), lambda h, i: (h, 0, 0))]
    if mode == "fox":
        in_specs += [pl.BlockSpec((None, tq, 1), lambda h, i: (h, i, 0)),
                     pl.BlockSpec((None, 1, sk), lambda h, i: (h, 0, 0))]
        args += [cq, ck]
        out_shape += [_sds((n_heads, sq, 1), F32), _sds((n_heads, 1, sk), F32)]
        out_specs += [pl.BlockSpec((None, tq, 1), lambda h, i: (h, i, 0)),
                      pl.BlockSpec((None, 1, sk), lambda h, i: (h, 0, 0))]
    if mode == "chunk":
        in_specs.append(pl.BlockSpec((3, None, 128, 128), lambda h, i: (0, h, 0, 0)))
        args.append(tiles)
        out_shape.append(_sds((3, n_heads, 128, 128), F32))
        out_specs.append(pl.BlockSpec((3, None, 128, 128), lambda h, i: (0, h, 0, 0)))
    return _pcall(body, name=name, out_shape=tuple(out_shape), grid=(n_heads, sq // tq),
                  in_specs=in_specs, out_specs=tuple(out_specs))(*args)


REL_LANES = 384
REL_KBLK = 2048


def _rel_onehot(t, k):
    rho = k * REL_KBLK + lax.broadcasted_iota(jnp.int32, (REL_KBLK, REL_LANES), 0)
    lane = lax.broadcasted_iota(jnp.int32, (REL_KBLK, REL_LANES), 1)
    diff = lax.shift_right_logical(rho, 7) - jnp.bitwise_and(rho, 127)
    idx = jnp.where(t == 0, diff + REL_CLIP,
                    jnp.where(t == 1, jnp.minimum(diff + 128, REL_CLIP) + REL_CLIP, N_REL - 1))
    return jnp.where(idx == lane, 1.0, 0.0).astype(F32)


def _split3(x):
    hi = x.astype(BF16)
    rest = x - hi.astype(F32)
    mid = rest.astype(BF16)
    return hi, mid, (rest - mid.astype(F32)).astype(BF16)


def relbias_onehot():
    def body(o_ref):
        o_ref[...] = _rel_onehot(pl.program_id(0), pl.program_id(1)).astype(o_ref.dtype)

    return _pcall(body, name="relbias_onehot", out_shape=_sds((3, 128 * 128, REL_LANES), BF16),
                  grid=(3, 128 * 128 // REL_KBLK), in_specs=[],
                  out_specs=pl.BlockSpec((None, REL_KBLK, REL_LANES), lambda t, k: (t, k, 0)))()


ONEHOT_SPEC = pl.BlockSpec((None, REL_KBLK, REL_LANES), lambda t, k: (t, k, 0))


def relbias_tiles(rel_bias, onehot):
    nh = rel_bias.shape[0]
    rb = jnp.pad(rel_bias, ((0, 0), (0, REL_LANES - N_REL)))

    def body(rb_ref, e_ref, o_ref):
        e = e_ref[...]
        hi, mid, lo = [lax.dot_general(t, e, (((1,), (1,)), ((), ())), preferred_element_type=F32)
                       for t in _split3(rb_ref[...])]
        o_ref[...] = (hi + mid) + lo

    flat = _pcall(body, name="relbias_tiles", out_shape=_sds((3, nh, 128 * 128), F32),
                  grid=(3, 128 * 128 // REL_KBLK),
                  in_specs=[pl.BlockSpec((nh, REL_LANES), lambda t, k: (0, 0)), ONEHOT_SPEC],
                  out_specs=pl.BlockSpec((None, nh, REL_KBLK), lambda t, k: (t, 0, k)))(rb, onehot)
    return flat.reshape(3, nh, 128, 128)


def relbias_tiles_bwd(dtiles, onehot):
    nh = dtiles.shape[1]

    def body(dt_ref, e_ref, o_ref):
        t, k = pl.program_id(0), pl.program_id(1)
        e = e_ref[...]
        hi, mid, lo = [jnp.dot(t_, e, preferred_element_type=F32) for t_ in _split3(dt_ref[...])]
        part = (hi + mid) + lo
        first = jnp.logical_and(t == 0, k == 0)

        @pl.when(first)
        def _():
            o_ref[...] = part

        @pl.when(jnp.logical_not(first))
        def _():
            o_ref[...] += part

    out = _pcall(body, name="relbias_tiles_bwd", out_shape=_sds((nh, REL_LANES), F32),
                 grid=(3, 128 * 128 // REL_KBLK),
                 in_specs=[pl.BlockSpec((None, nh, REL_KBLK), lambda t, k: (t, 0, k)), ONEHOT_SPEC],
                 out_specs=pl.BlockSpec((nh, REL_LANES), lambda t, k: (0, 0)))(
                     dtiles.reshape(3, nh, 128 * 128), onehot)
    return out[:, :N_REL]


CUM_BLK = 256


def _tri(n, lower):
    r = lax.broadcasted_iota(jnp.int32, (n, n), 0)
    c = lax.broadcasted_iota(jnp.int32, (n, n), 1)
    return jnp.where(r >= c if lower else r <= c, 1.0, 0.0).astype(F32)


def fox_cum_fwd(zs, bf):
    S = zs.shape[0]
    tb = min(CUM_BLK, S)

    def body(f_ref, b_ref, cum_ref, cumt_ref, carry_ref):
        @pl.when(pl.program_id(0) == 0)
        def _():
            carry_ref[...] = jnp.zeros_like(carry_ref)

        x = f_ref[...] + b_ref[...]
        lane = lax.broadcasted_iota(jnp.int32, x.shape, 1)
        logf = jnp.where(lane < N_HEADS, jnp.minimum(x, 0.0) - jnp.log(1.0 + jnp.exp(-jnp.abs(x))), 0.0)
        cum = jnp.dot(_tri(tb, True), logf, preferred_element_type=F32,
                      precision=lax.Precision.HIGHEST) + carry_ref[...]
        carry_ref[...] = cum[tb - 1:tb, :]
        cum_ref[...] = cum
        cumt_ref[...] = cum.T

    return _pcall(body, name="fox_cum_fwd", out_shape=(_sds((S, 128), F32), _sds((128, S), F32)),
                  grid=(S // tb,),
                  in_specs=[pl.BlockSpec((tb, 128), lambda i: (i, Z_FF // 128)),
                            pl.BlockSpec((1, 128), lambda i: (0, 0))],
                  out_specs=(pl.BlockSpec((tb, 128), lambda i: (i, 0)),
                             pl.BlockSpec((128, tb), lambda i: (0, i))),
                  scratch=(pltpu.VMEM((1, 128), F32),))(zs, bf)


def fox_cum_bwd(zs, bf, dcum):
    S = zs.shape[0]
    tb = min(CUM_BLK, S)
    nblk = S // tb

    def body(f_ref, b_ref, d_ref, df_ref, db_ref, carry_ref):
        @pl.when(pl.program_id(0) == 0)
        def _():
            carry_ref[...] = jnp.zeros_like(carry_ref)
            db_ref[...] = jnp.zeros_like(db_ref)

        d = d_ref[...]
        dlogf = jnp.dot(_tri(tb, False), d, preferred_element_type=F32,
                        precision=lax.Precision.HIGHEST) + carry_ref[...]
        carry_ref[...] += jnp.sum(d, axis=0, keepdims=True)
        x = f_ref[...] + b_ref[...]
        lane = lax.broadcasted_iota(jnp.int32, x.shape, 1)
        dx = jnp.where(lane < N_HEADS, dlogf / (1.0 + jnp.exp(x)), 0.0)
        df_ref[...] = dx
        db_ref[...] += jnp.sum(dx, axis=0, keepdims=True)

    return _pcall(body, name="fox_cum_bwd", out_shape=(_sds((S, 128), F32), _sds((1, 128), F32)),
                  grid=(nblk,),
                  in_specs=[pl.BlockSpec((tb, 128), lambda i: (nblk - 1 - i, Z_FF // 128)),
                            pl.BlockSpec((1, 128), lambda i: (0, 0)),
                            pl.BlockSpec((tb, 128), lambda i: (nblk - 1 - i, 0))],
                  out_specs=(pl.BlockSpec((tb, 128), lambda i: (nblk - 1 - i, 0)),
                             pl.BlockSpec((1, 128), lambda i: (0, 0))),
                  scratch=(pltpu.VMEM((1, 128), F32),))(zs, bf, dcum)


def _sigmoid(x):
    return 1.0 / (1.0 + jnp.exp(-x))


def merge_fwd(z, projs):
    S = z.shape[0]
    tr, tc = _pick(S, (512, 256, 128)), 512
    nbc = D_MODEL // tc
    g0 = Z_GATE // tc

    def body(g0_ref, g1_ref, g2_ref, p0_ref, p1_ref, p2_ref, o_ref):
        acc = _sigmoid(g0_ref[...]) * p0_ref[...]
        acc += _sigmoid(g1_ref[...]) * p1_ref[...]
        acc += _sigmoid(g2_ref[...]) * p2_ref[...]
        o_ref[...] = acc.astype(o_ref.dtype)

    gspecs = [pl.BlockSpec((tr, tc), functools.partial(lambda i, j, n: (i, g0 + n * nbc + j), n=n))
              for n in range(3)]
    pspec = pl.BlockSpec((tr, tc), lambda i, j: (i, j))
    return _pcall(body, name="merge_fwd", out_shape=_sds((S, D_MODEL), BF16), grid=(S // tr, nbc),
                  in_specs=gspecs + [pspec] * 3, out_specs=pspec)(z, z, z, *projs)


def merge_bwd(z, projs, dmerged):
    S = z.shape[0]
    tr, tc = _pick(S, (512, 256, 128)), 512
    nbc = D_MODEL // tc
    g0 = Z_GATE // tc

    def body(g0_ref, g1_ref, g2_ref, p0_ref, p1_ref, p2_ref, dm_ref, dg0, dg1, dg2, dp0, dp1, dp2):
        dm = dm_ref[...]
        for g_ref, p_ref, dg_ref, dp_ref in ((g0_ref, p0_ref, dg0, dp0), (g1_ref, p1_ref, dg1, dp1),
                                             (g2_ref, p2_ref, dg2, dp2)):
            sg = _sigmoid(g_ref[...])
            dp_ref[...] = (dm * sg).astype(dp_ref.dtype)
            dg_ref[...] = (dm * p_ref[...] * sg * (1.0 - sg)).astype(dg_ref.dtype)

    gspecs = [pl.BlockSpec((tr, tc), functools.partial(lambda i, j, n: (i, g0 + n * nbc + j), n=n))
              for n in range(3)]
    pspec = pl.BlockSpec((tr, tc), lambda i, j: (i, j))
    out = _pcall(body, name="merge_bwd", out_shape=tuple(_sds((S, D_MODEL), BF16) for _ in range(6)),
                 grid=(S // tr, nbc), in_specs=gspecs + [pspec] * 4,
                 out_specs=tuple([pspec] * 6))(z, z, z, *projs, dmerged)
    return out[:3], out[3:]


def loss_head(y, target):
    S, D = y.shape
    tr = _pick(S, (256, 128))

    def body(y_ref, t_ref, dy_ref, l_ref):
        e = y_ref[...] - t_ref[...]
        dy_ref[...] = e * (1.0 / D)
        part = jnp.sum(jnp.sum(e * e, axis=1, keepdims=True), axis=0, keepdims=True) * (0.5 / D)

        @pl.when(pl.program_id(0) == 0)
        def _():
            l_ref[...] = part

        @pl.when(pl.program_id(0) > 0)
        def _():
            l_ref[...] += part

    spec = pl.BlockSpec((tr, D), lambda i: (i, 0))
    return _pcall(body, name="loss_head", out_shape=(_sds((S, D), F32), _sds((1, 1), F32)), grid=(S // tr,),
                  in_specs=[spec, spec], out_specs=(spec, pl.BlockSpec((1, 1), lambda i: (0, 0))))(y, target)


def _adamw_update(w, g, m, v):
    nm = ADAM_B1 * m + (1.0 - ADAM_B1) * g
    nv = ADAM_B2 * v + (1.0 - ADAM_B2) * (g * g)
    delta = -ADAM_LR * ((nm / (1.0 - ADAM_B1 ** ADAM_STEP)) / (jnp.sqrt(nv / (1.0 - ADAM_B2 ** ADAM_STEP)) + ADAM_EPS)
                        + ADAM_WD * w)
    return delta, nm, nv


ADAMW_BLOCK_BYTES = 1024 * 1024


def adamw_halves(w, pairs, m, v, core, *, name):
    L, R, C = w.shape
    h = R // 2
    tr = _pick(h, (512, 256, 128, 64, 32, 16, 8))
    while tr * C * 4 > ADAMW_BLOCK_BYTES and tr % 16 == 0:
        tr //= 2
    nbh = h // tr

    def body(*refs):
        c_ref, w_ref, m_ref, v_ref = refs[:4]
        g_refs = refs[4:4 + 2 * L]
        go_ref, d_ref, nm_ref, nv_ref = refs[4 + 2 * L:]
        l, mine = pl.program_id(0), pl.program_id(1) // nbh == c_ref[0]
        g_ = jnp.where(mine, g_refs[0][...], g_refs[1][...])
        for k in range(1, L):
            g_ = jnp.where(l == k, jnp.where(mine, g_refs[2 * k][...], g_refs[2 * k + 1][...]), g_)
        go_ref[...] = g_
        d_ref[...], nm_ref[...], nv_ref[...] = _adamw_update(w_ref[...], g_, m_ref[...], v_ref[...])

    def g_index(l, i, c_ref, k, own):
        on = jnp.logical_and(l == k, (i // nbh == c_ref[0]) == own)
        return jnp.where(on, i % nbh, 0), 0

    spec = pl.BlockSpec((None, tr, C), lambda l, i, c_ref: (l, i, 0))
    gspecs = [pl.BlockSpec((tr, C), functools.partial(g_index, k=k, own=own)) for k in range(L) for own in (True, False)]
    grid_spec = pltpu.PrefetchScalarGridSpec(num_scalar_prefetch=1, grid=(L, R // tr),
                                             in_specs=[spec] * 3 + gspecs, out_specs=(spec,) * 4)
    return pl.pallas_call(
        body, out_shape=tuple(_sds((L, R, C), F32) for _ in range(4)), grid_spec=grid_spec, name=name,
        interpret=False, compiler_params=pltpu.CompilerParams(vmem_limit_bytes=VMEM_LIMIT_BYTES))(
            core, w, m, v, *[a for pair in pairs for a in pair])


def adamw(w, gs, m, v, *, name):
    L, R, C = w.shape
    tr = _pick(R, (512, 256, 128, 64, 32, 16, 8))
    while tr * C * 4 > ADAMW_BLOCK_BYTES and tr % 16 == 0:
        tr //= 2

    def body(*refs):
        w_ref, m_ref, v_ref = refs[:3]
        g_refs = refs[3:3 + L]
        go_ref, d_ref, nm_ref, nv_ref = refs[3 + L:]
        l = pl.program_id(0)
        g_ = g_refs[0][...]
        for k in range(1, L):
            g_ = jnp.where(l == k, g_refs[k][...], g_)
        go_ref[...] = g_
        d_ref[...], nm_ref[...], nv_ref[...] = _adamw_update(w_ref[...], g_, m_ref[...], v_ref[...])

    spec = pl.BlockSpec((None, tr, C), lambda l, i: (l, i, 0))
    gspecs = [pl.BlockSpec((tr, C), functools.partial(lambda l, i, k: (jnp.where(l == k, i, 0), 0), k=k))
              for k in range(L)]
    return _pcall(body, name=name, out_shape=tuple(_sds((L, R, C), F32) for _ in range(4)),
                  grid=(L, R // tr), in_specs=[spec] * 3 + gspecs, out_specs=(spec,) * 4)(w, m, v, *gs)


SHARD_SHAPES = {
    "w_uq": (Q_LORA, 384), "w_ukv": (KV_LORA, 512), "w_br": (3, MIX_W, 512), "w_out": (512, D_MODEL),
    "w_xq": (512, 512), "w_xkv": (512, 1024), "w_xo": (512, 512), "w_1": (D_MODEL, 2048),
    "w_2": (2048, D_MODEL), "w_in": (D_MODEL, W_IN_SHARD),
}


def _rows2d(a, name):
    shp = SHARD_SHAPES[name]
    return a.reshape(a.shape[:a.ndim - len(shp)] + (-1, shp[-1]))


def _cols_from_shards(g):
    return jnp.transpose(g, (1, 0, 2)).reshape(g.shape[1], 4 * g.shape[2])


def _cols_to_shards(w):
    return jnp.transpose(w.reshape(w.shape[0], 4, w.shape[1] // 4), (1, 0, 2))


EARLY_WEIGHTS = ("w_in", "w_uq", "w_ukv")
MID_WEIGHTS = ("w_br", "w_out", "w_xq", "w_xkv", "w_xo")
LAST_WEIGHTS = ("w_1", "w_2")
LATE_WEIGHTS = MID_WEIGHTS + LAST_WEIGHTS
assert sorted(EARLY_WEIGHTS + LATE_WEIGHTS) == sorted(PACK_ORDER)


def _w_in_relaid(g_in):
    zeros = lambda n: [jnp.zeros((D_MODEL, n), g_in.dtype)] if n else []
    segs, at = [], 0
    for p0, o0, w in sorted(W_IN_PIECES):
        segs += zeros(p0 - at)
        while w > 0:
            s_, a = divmod(o0, W_IN_SHARD)
            take = min(w, W_IN_SHARD - a)
            segs.append(g_in[s_][:, a:a + take])
            o0, p0, w = o0 + take, p0 + take, w - take
        at = p0
    return jnp.concatenate(segs + zeros(Z_TOT - at), axis=1)


def full_weights(g):
    forms = {
        "w_in": lambda a: {"in_p": _w_in_relaid(a)[None]},
        "w_uq": lambda a: {"uq_p": jnp.pad(_cols_from_shards(a).reshape(Q_LORA, N_HEADS, MLA_QK),
                                           ((0, 0), (0, 0), (0, MLA_PAD - MLA_QK))
                                           ).reshape(1, Q_LORA, N_HEADS * MLA_PAD)},
        "w_ukv": lambda a: {"ukv": a},
        "w_br": lambda a: {"br": [a[:, n] for n in range(3)]},
        "w_out": lambda a: {"out": a.reshape(1, D_MODEL, D_MODEL)},
        "w_xq": lambda a: {"xq": a.reshape(1, D_MODEL, 512)},
        "w_xkv": lambda a: {"xkv": a.reshape(1, D_MODEL, 1024)},
        "w_xo": lambda a: {"xo": a},
        "w_1": lambda a: {"w1": a},
        "w_2": lambda a: {"w2": a.reshape(1, D_FF, D_MODEL)},
    }
    out = {}
    for n, a in g.items():
        out.update(forms[n](a))
    return out


def _w_in_grad_shards(dp):
    d_in = []
    for s_ in range(4):
        lo, hi, segs = s_ * W_IN_SHARD, (s_ + 1) * W_IN_SHARD, []
        for o0, p0, w in sorted((o0, p0, w) for p0, o0, w in W_IN_PIECES):
            a, b = max(lo, o0), min(hi, o0 + w)
            if a < b:
                segs.append(dp[:, p0 + a - o0:p0 + b - o0])
        d_in.append(jnp.concatenate(segs, axis=1))
    return jnp.stack(d_in)


def shard_grads(dw):
    forms = {
        "in_p": lambda a: {"w_in": _w_in_grad_shards(a[0])},
        "uq_p": lambda a: {"w_uq": _cols_to_shards(
            a.reshape(Q_LORA, N_HEADS, MLA_PAD)[:, :, :MLA_QK].reshape(Q_LORA, N_HEADS * MLA_QK))},
        "ukv": lambda a: {"w_ukv": a},
        "br": lambda a: {"w_br": jnp.stack(a, axis=1)},
        "out": lambda a: {"w_out": a.reshape(4, 512, D_MODEL)},
        "xq": lambda a: {"w_xq": a.reshape(4, 512, 512)},
        "xkv": lambda a: {"w_xkv": a.reshape(4, 512, 1024)},
        "xo": lambda a: {"w_xo": a},
        "w1": lambda a: {"w_1": a},
        "w2": lambda a: {"w_2": a.reshape(4, 2048, D_MODEL)},
    }
    out = {}
    for k, a in dw.items():
        out.update(forms[k](a))
    return out


def layer_params(d, l):
    row = lambda v: v.reshape(1, -1).astype(F32)
    padto = lambda v, n: jnp.pad(row(v), ((0, 0), (0, n - v.shape[-1])))
    return dict(
        g_mix=row(d["g_mix"][l]), g_cq=row(d["g_cq"][l]), g_ckv=row(d["g_ckv"][l]),
        g_mla_q=padto(d["g_mla_q"][l], MLA_PAD), g_mla_k=padto(d["g_mla_k"][l], MLA_PAD),
        b_f=padto(d["b_f"][l], 128), g_fox_q=row(d["g_fox_q"][l]), g_fox_k=row(d["g_fox_k"][l]),
        rel_bias=d["rel_bias"][l].astype(F32), g_ch_q=row(d["g_ch_q"][l]), g_ch_k=row(d["g_ch_k"][l]),
        g_cross=row(d["g_cross"][l]), g_mem=row(d["g_mem"][l]), g_x_q=row(d["g_x_q"][l]),
        g_x_k=row(d["g_x_k"][l]), g_mlp=row(d["g_mlp"][l]))


FOX_B = Z_FOX // 128
CH_B = Z_CH // 128


def _after(small, token):
    return small if token is None else small + token[:1, :1].reshape((1,) * small.ndim)


def layer_fwd(x, mem, W, P, rope, mid_weights, last_weights):
    S = x.shape[0]
    s = {}
    s["h"] = rms_fwd(x, P["g_mix"], name="rms_d")
    z = s["z"] = mm_nn(s["h"], W["in_p"], name="mm_in", out_dtype=F32)
    s["cq_n"] = rms_fwd(z, P["g_cq"], col0=Z_CQ, width=Q_LORA, name="rms_cq")
    s["ckv_n"] = rms_fwd(z, P["g_ckv"], col0=Z_CKV, width=KV_LORA, name="rms_ckv")
    s["q_raw"] = mm_nn(s["cq_n"], W["uq_p"], name="mm_uq", out_dtype=F32)
    s["kv_raw"] = mm_nn(s["ckv_n"], W["ukv"], name="mm_ukv", out_dtype=F32)
    s["qa"] = prep_fwd([(s["q_raw"], MLA_PAD, lambda h: h)], P["g_mla_q"], name="prep_mla_q",
                       n_heads=N_HEADS, n_real=MLA_QK, rope=rope)
    s["ka"] = prep_fwd([(s["kv_raw"], 128, lambda h: 2 * h), (z, 128, lambda h: Z_KR // 128)], P["g_mla_k"],
                       name="prep_mla_k", n_heads=N_HEADS, n_real=MLA_QK, rope=rope)
    s["o_a"], s["lse_a"] = attn_fwd(s["qa"], s["ka"], s["kv_raw"], lambda h: 2 * h + 1, mode="mla",
                                    name="attn_mla", n_heads=N_HEADS)
    s["qb"] = prep_fwd([(z, DH, lambda h: FOX_B + h)], P["g_fox_q"], name="prep_h", n_heads=N_HEADS, n_real=DH)
    s["kb"] = prep_fwd([(z, DH, lambda h: FOX_B + N_HEADS + h)], P["g_fox_k"], name="prep_h",
                       n_heads=N_HEADS, n_real=DH)
    _, cum_t = fox_cum_fwd(z, P["b_f"])
    s["cq"] = cum_t[:N_HEADS].reshape(N_HEADS, S, 1)
    s["ck"] = cum_t[:N_HEADS].reshape(N_HEADS, 1, S)
    s["o_b"], s["lse_b"] = attn_fwd(s["qb"], s["kb"], z, lambda h: FOX_B + 2 * N_HEADS + h, mode="fox",
                                    name="attn_fox", n_heads=N_HEADS, cq=s["cq"], ck=s["ck"])
    s["qc"] = prep_fwd([(z, DH, lambda h: CH_B + h)], P["g_ch_q"], name="prep_h", n_heads=N_HEADS, n_real=DH)
    s["kc"] = prep_fwd([(z, DH, lambda h: CH_B + N_HEADS + h)], P["g_ch_k"], name="prep_h",
                       n_heads=N_HEADS, n_real=DH)
    s["tiles"] = relbias_tiles(P["rel_bias"], P["rel_onehot"])
    s["o_c"], s["lse_c"] = attn_fwd(s["qc"], s["kc"], z, lambda h: CH_B + 2 * N_HEADS + h, mode="chunk",
                                    name="attn_chunk", n_heads=N_HEADS, tiles=s["tiles"])
    mid, token = mid_weights(s["o_c"])
    W = dict(W, **mid)
    P = dict(P, g_cross=_after(P["g_cross"], token))
    s["projs"] = [mm_nn(o, W["br"][n], name="mm_br", out_dtype=F32)
                  for n, o in enumerate((s["o_a"], s["o_b"], s["o_c"]))]
    s["merged"] = merge_fwd(z, s["projs"])
    x1 = s["x1"] = mm_nn(s["merged"], W["out"], name="mm_out", out_dtype=F32, res=x)
    s["hq"] = rms_fwd(x1, P["g_cross"], name="rms_d")
    s["xq_raw"] = mm_nn(s["hq"], W["xq"], name="mm_xq", out_dtype=F32)
    s["mem_n"] = rms_fwd(mem, P["g_mem"], name="rms_mem")
    s["mkv"] = mm_nn(s["mem_n"], W["xkv"], name="mm_xkv", out_dtype=F32)
    s["qx"] = prep_fwd([(s["xq_raw"], DH, lambda h: h)], P["g_x_q"], name="prep_xq", n_heads=X_HEADS, n_real=DH)
    s["kx"] = prep_fwd([(s["mkv"], DH, lambda h: h)], P["g_x_k"], name="prep_xk", n_heads=X_HEADS, n_real=DH)
    s["o_x"], s["lse_x"] = attn_fwd(s["qx"], s["kx"], s["mkv"], lambda h: X_HEADS + h, mode="cross",
                                    name="attn_cross", n_heads=X_HEADS)
    x2 = s["x2"] = mm_nn(s["o_x"], W["xo"], name="mm_xo", out_dtype=F32, res=x1)
    s["hm"] = rms_fwd(x2, P["g_mlp"], name="rms_d")
    last, _ = last_weights(s["hm"])
    W = dict(W, **last)
    s["a1"], s["act"] = mm_nn(s["hm"], W["w1"], name="mm_w1", out_dtype=BF16, relu2=True)
    x3 = mm_nn(s["act"], W["w2"], name="mm_w2", out_dtype=F32, res=x2)
    return x3, s, W


LATE_GRADS = ("w2", "w1", "xo", "xq", "xkv", "out", "br")


def layer_bwd(g, x, mem, W, P, rope, s, grads_done):
    S = x.shape[0]
    z = s["z"]
    dw, ds = {}, {}
    da1 = mm_nt(g, W["w2"], name="mm_w2_dx", out_dtype=BF16, relu_mul=s["a1"])
    dw["w2"] = mm_tn(s["act"], g, nb=1, name="mm_w2_dw", out_dtype=BF16)
    dhm = mm_nt(da1, W["w1"], name="mm_w1_dx", out_dtype=F32)
    dw["w1"] = mm_tn(s["hm"], da1, nb=4, name="mm_w1_dw", out_dtype=BF16)
    g2, ds["g_mlp"] = rms_bwd(s["x2"], P["g_mlp"], dhm, name="rms_d_bwd", res=g)
    do_x = mm_nt(g2, W["xo"], name="mm_xo_dx", out_dtype=BF16)
    dw["xo"] = mm_tn(s["o_x"], g2, nb=4, name="mm_xo_dw", out_dtype=BF16)
    dqx, dkx, dvx = attn_bwd(s["qx"], s["kx"], s["mkv"], lambda h: X_HEADS + h, s["o_x"], do_x, s["lse_x"],
                             mode="cross", name="attn_cross_bwd", n_heads=X_HEADS)
    dxq_raw, ds["g_x_q"] = prep_bwd_q(s["xq_raw"], dqx, P["g_x_q"], name="prep_xq_bwd", n_heads=X_HEADS,
                                      dh=DH, n_real=DH)
    dmkv, dgk = prep_bwd_groups(s["mkv"], 0, [dkx, dvx], [P["g_x_k"], None], name="prep_xkv_bwd",
                                n_heads=X_HEADS, kinds=("norm", "copy"))
    ds["g_x_k"] = dgk[0]
    dhq = mm_nt(dxq_raw, W["xq"], name="mm_xq_dx", out_dtype=F32)
    dw["xq"] = mm_tn(s["hq"], dxq_raw, nb=1, name="mm_xq_dw", out_dtype=BF16)
    dmem_n = mm_nt(dmkv, W["xkv"], name="mm_xkv_dx", out_dtype=F32)
    dw["xkv"] = mm_tn(s["mem_n"], dmkv, nb=1, name="mm_xkv_dw", out_dtype=BF16)
    _, ds["g_mem"] = rms_bwd(mem, P["g_mem"], dmem_n, name="rms_mem_bwd", need_dx=False)
    g1, ds["g_cross"] = rms_bwd(s["x1"], P["g_cross"], dhq, name="rms_d_bwd", res=g2)
    dmerged = mm_nt(g1, W["out"], name="mm_out_dx", out_dtype=F32)
    dw["out"] = mm_tn(s["merged"], g1, nb=1, name="mm_out_dw", out_dtype=BF16)
    dgl, dproj = merge_bwd(z, s["projs"], dmerged)
    outs = (s["o_a"], s["o_b"], s["o_c"])
    do = [mm_nt(dproj[n], W["br"][n], name="mm_br_dx", out_dtype=BF16) for n in range(3)]
    dw["br"] = [mm_tn(outs[n], dproj[n], nb=4, name="mm_br_dw", out_dtype=BF16) for n in range(3)]
    token = grads_done("late", {k: dw.pop(k) for k in LATE_GRADS})
    dqc, dkc, dvc, dtiles = attn_bwd(s["qc"], s["kc"], z, lambda h: CH_B + 2 * N_HEADS + h, s["o_c"], do[2],
                                     s["lse_c"], mode="chunk", name="attn_chunk_bwd", n_heads=N_HEADS,
                                     tiles=_after(s["tiles"], token))
    d_ch, dg_ch = prep_bwd_groups(z, CH_B, [dqc, dkc, dvc], [P["g_ch_q"], P["g_ch_k"], None],
                                  name="prep_h_bwd", n_heads=N_HEADS, kinds=("norm", "norm", "copy"))
    ds["g_ch_q"], ds["g_ch_k"] = dg_ch[0], dg_ch[1]
    ds["rel_bias"] = relbias_tiles_bwd(dtiles, P["rel_onehot"])
    token = grads_done("late_followup", d_ch)
    dqb, dkb, dvb, dcq, dck = attn_bwd(s["qb"], s["kb"], z, lambda h: FOX_B + 2 * N_HEADS + h, s["o_b"], do[1],
                                       s["lse_b"], mode="fox", name="attn_fox_bwd", n_heads=N_HEADS,
                                       cq=s["cq"], ck=_after(s["ck"], token))
    d_fox, dg_fox = prep_bwd_groups(z, FOX_B, [dqb, dkb, dvb], [P["g_fox_q"], P["g_fox_k"], None],
                                    name="prep_h_bwd", n_heads=N_HEADS, kinds=("norm", "norm", "copy"))
    ds["g_fox_q"], ds["g_fox_k"] = dg_fox[0], dg_fox[1]
    dcum = jnp.pad((dcq[:, :, 0] + dck[:, 0, :]).T, ((0, 0), (0, 128 - N_HEADS)))
    dff, dbf = fox_cum_bwd(z, P["b_f"], dcum)
    ds["b_f"] = dbf[:, :N_HEADS]
    dqa, dka, dva = attn_bwd(s["qa"], s["ka"], s["kv_raw"], lambda h: 2 * h + 1, s["o_a"], do[0], s["lse_a"],
                             mode="mla", name="attn_mla_bwd", n_heads=N_HEADS)
    dq_raw, dgq = prep_bwd_q(s["q_raw"], dqa, P["g_mla_q"], name="prep_mla_q_bwd", n_heads=N_HEADS,
                             dh=MLA_PAD, n_real=MLA_QK, rope=rope)
    dkv_raw, dkr, dgk = prep_bwd_mla_k(s["kv_raw"], z, dka, dva, P["g_mla_k"], rope, name="prep_mla_k_bwd")
    ds["g_mla_q"], ds["g_mla_k"] = dgq[:, :MLA_QK], dgk[:, :MLA_QK]
    dcq_n = mm_nt(dq_raw, W["uq_p"], name="mm_uq_dx", out_dtype=F32)
    dw["uq_p"] = mm_tn(s["cq_n"], dq_raw, nb=1, name="mm_uq_dw", out_dtype=BF16)
    dckv_n = mm_nt(dkv_raw, W["ukv"], name="mm_ukv_dx", out_dtype=F32)
    dw["ukv"] = mm_tn(s["ckv_n"], dkv_raw, nb=4, name="mm_ukv_dw", out_dtype=BF16)
    d_cq, ds["g_cq"] = rms_bwd(z, P["g_cq"], dcq_n, name="rms_cq_bwd", col0=Z_CQ, width=Q_LORA, dx_dtype=BF16)
    d_ckv, ds["g_ckv"] = rms_bwd(z, P["g_ckv"], dckv_n, name="rms_ckv_bwd", col0=Z_CKV, width=KV_LORA,
                                 dx_dtype=BF16)
    dz = jnp.concatenate([d_cq, d_ckv, dkr.astype(BF16), dff.astype(BF16), d_fox, d_ch, *dgl], axis=1)
    dw["in_p"] = mm_tn(s["h"], dz, nb=1, name="mm_in_dw", out_dtype=BF16)
    token = grads_done("early", dw)
    dh = mm_nt(dz, W["in_p"], name="mm_in_dx", out_dtype=F32, after=token)
    g0, ds["g_mix"] = rms_bwd(x, P["g_mix"], dh, name="rms_d_bwd", res=g1)
    return g0, ds


def local_step(x, mem, target, Ps, weights_of, grads_done):
    rope = rope_tables(x.shape[0])
    onehot = relbias_onehot()
    Ps = [dict(P, rel_onehot=onehot) for P in Ps]
    Ws, saved, xs = [], [], [x]
    for l, P in enumerate(Ps):
        W_early, mid_weights, last_weights, token = weights_of(l, xs[-1])
        y, s, W = layer_fwd(xs[-1], mem, W_early, dict(P, g_mix=_after(P["g_mix"], token)), rope,
                            mid_weights, last_weights)
        xs.append(y)
        saved.append(s)
        Ws.append(W)
    g, loss = loss_head(xs[-1], target)
    dss, token = [], None
    for l in reversed(range(len(Ps))):
        g, ds = layer_bwd(g, xs[l], mem, Ws[l], dict(Ps[l], g_mlp=_after(Ps[l]["g_mlp"], token)), rope,
                          saved[l], functools.partial(grads_done, l))
        token = grads_done(l, "early_followup", g)
        dss.append(ds)
    return loss, g, dss[::-1]


HBM_SPEC = pl.BlockSpec(memory_space=pltpu.HBM)


def _place():
    return lax.axis_index("x"), lax.axis_index("y"), lax.axis_index("c")


def _other_chips(x, y):
    return [(1 - x, y), (x, 1 - y), (1 - x, 1 - y)]


def _remote(src, dst, send_sems, recv_sems, k, to):
    return pltpu.make_async_remote_copy(src_ref=src, dst_ref=dst, send_sem=send_sems.at[k],
                                        recv_sem=recv_sems.at[k], device_id=to, device_id_type=MESH)


SEM_SPEC = pl.BlockSpec(memory_space=pltpu.SEMAPHORE)
SPLIT_EFFECT = pltpu.SideEffectType.DATAFLOW_SIDE_EFFECTING


COPIES_PER_ARRAY = {"gather": 3, "scatter": 3, "halves": 1, "share": 1}


def _chip_exchange_copies(kind, srcs, lands, send_sems, recv_sems):
    x, y, c = _place()
    me = 2 * x + y
    cps = []
    for i, (src, land) in enumerate(zip(srcs, lands)):
        if kind == "halves":
            h = src.shape[1] // 2
            cps.append(_remote(src.at[:, pl.ds((1 - c) * h, h), :], land, send_sems, recv_sems, i, (x, y, 1 - c)))
            continue
        if kind == "share":
            cps.append(_remote(src, land, send_sems, recv_sems, i, (x, y, 1 - c)))
            continue
        for j, (cx, cy) in enumerate(_other_chips(x, y)):
            if kind == "gather":
                h = src.shape[0] // 2
                s_ref, d_ref = src.at[pl.ds(c * h, h), :], land.at[me, pl.ds(c * h, h), :]
            else:
                s_ref, d_ref = src.at[2 * cx + cy], land.at[me]
            cps.append(_remote(s_ref, d_ref, send_sems, recv_sems, 3 * i + j, (cx, cy, c)))
    return cps


def chip_exchange_start(kind, srcs, land_shapes, *, name, after=None):
    n = len(srcs)
    n_in = 2 * n + (after is not None)

    def body(*refs):
        send_sems, recv_sems = refs[n_in], refs[n_in + 1]
        for cp in _chip_exchange_copies(kind, refs[:n], refs[n:2 * n], send_sems, recv_sems):
            cp.start()
        refs[-1][...] = jnp.zeros_like(refs[-1])

    hbm = lambda a: pltpu.with_memory_space_constraint(a, pltpu.HBM)
    ins = [hbm(s) for s in srcs] + [hbm(lax.empty(s.shape, s.dtype)) for s in land_shapes]
    sems = pltpu.SemaphoreType.DMA((COPIES_PER_ARRAY[kind] * n,))
    out = pl.pallas_call(
        body, name=name, interpret=False,
        out_shape=(sems, sems, *[pltpu.HBM(a.shape, a.dtype) for a in ins], _sds((8, 128), F32)),
        in_specs=[HBM_SPEC] * (2 * n) + [pl.BlockSpec(memory_space=pl.ANY)] * (after is not None),
        out_specs=(SEM_SPEC, SEM_SPEC, *[HBM_SPEC] * (2 * n), pl.BlockSpec(memory_space=pltpu.VMEM)),
        input_output_aliases={i: 2 + i for i in range(2 * n)},
        compiler_params=pltpu.CompilerParams(has_side_effects=SPLIT_EFFECT))(
            *ins, *([after] if after is not None else []))
    return out[0], out[1], list(out[2:2 + n]), list(out[2 + n:2 + 2 * n]), out[-1]


def chip_exchange_wait(kind, started, after, *, name):
    send_sems, recv_sems, srcs, lands, _ = started
    n = len(srcs)

    def body(*refs):
        send_sems, recv_sems = refs[2 * n], refs[2 * n + 1]
        for cp in _chip_exchange_copies(kind, refs[:n], refs[n:2 * n], send_sems, recv_sems):
            cp.wait_send()
            cp.wait_recv()

    thru = srcs + lands
    out = pl.pallas_call(
        body, name=name, interpret=False,
        out_shape=tuple(pltpu.HBM(a.shape, a.dtype) for a in thru),
        in_specs=[HBM_SPEC] * (2 * n) + [SEM_SPEC, SEM_SPEC, pl.BlockSpec(memory_space=pl.ANY)],
        out_specs=tuple([HBM_SPEC] * (2 * n)),
        input_output_aliases={i: i for i in range(2 * n)},
        compiler_params=pltpu.CompilerParams(has_side_effects=SPLIT_EFFECT))(*thru, send_sems, recv_sems, after)
    return list(out[:n]), list(out[n:])


def gather_to_sibling(lands):
    n = len(lands)

    def body(*refs):
        outs = refs[n:2 * n]
        send_sems, recv_sems = refs[2 * n:]
        x, y, c = _place()
        sends = []
        for i in range(n):
            h = outs[i].shape[1] // 2
            for j, (cx, cy) in enumerate(_other_chips(x, y)):
                landed = outs[i].at[2 * cx + cy, pl.ds(c * h, h), :]
                cp = _remote(landed, landed, send_sems, recv_sems, 3 * i + j, (x, y, 1 - c))
                cp.start()
                sends.append(cp)
        for cp in sends:
            cp.wait()

    return pl.pallas_call(
        body, out_shape=tuple(_sds(a.shape, a.dtype) for a in lands), in_specs=[HBM_SPEC] * n,
        out_specs=tuple([HBM_SPEC] * n), input_output_aliases={i: i for i in range(n)},
        scratch_shapes=[pltpu.SemaphoreType.DMA((3 * n,)), pltpu.SemaphoreType.DMA((3 * n,))],
        name="gather_to_sibling", interpret=False)(*lands)


def add_pair(part, got, half_idx):
    _, a, b = part.shape
    h = a // 2
    tr = _pick(h, (512, 256, 128))

    def body(c_ref, p_ref, g_ref, o_ref):
        o_ref[...] = (p_ref[...].astype(F32) + g_ref[...].astype(F32)).astype(o_ref.dtype)

    spec = pl.BlockSpec((None, tr, b), lambda s_, i, c_ref: (s_, i, 0))
    grid_spec = pltpu.PrefetchScalarGridSpec(
        num_scalar_prefetch=1, grid=(4, h // tr),
        in_specs=[pl.BlockSpec((None, None, tr, b), lambda s_, i, c_ref: (s_, c_ref[0], i, 0)), spec],
        out_specs=spec)
    return pl.pallas_call(
        body, out_shape=_sds((4, h, b), part.dtype), grid_spec=grid_spec, name="rs_add_pair", interpret=False,
        compiler_params=pltpu.CompilerParams(vmem_limit_bytes=VMEM_LIMIT_BYTES))(
            half_idx, part.reshape(4, 2, h, b), got)


def sum_slots(r):
    _, h, b = r.shape
    tr = _pick(h, (512, 256, 128))

    def body(r0, r1, r2, r3, o_ref):
        o_ref[...] = ((r0[...].astype(F32) + r1[...].astype(F32)) + r2[...].astype(F32)) + r3[...].astype(F32)

    specs = [pl.BlockSpec((None, tr, b), functools.partial(lambda i, s_: (s_, i, 0), s_=s_)) for s_ in range(4)]
    return _pcall(body, name="rs_sum_slots", out_shape=_sds((h, b), F32), grid=(h // tr,),
                  in_specs=specs, out_specs=pl.BlockSpec((tr, b), lambda i: (i, 0)))(r, r, r, r)


def reduce_halves_start(parts, *, tag):
    return chip_exchange_start("halves", parts, [_sds((4, p.shape[1] // 2, p.shape[2]), p.dtype) for p in parts],
                               name="halves_start_" + tag)


def reduce_scatter_start(halves_started, after, core, *, tag):
    parts, gots = chip_exchange_wait("halves", halves_started, after, name="halves_wait_" + tag)
    half_idx = core.reshape(1).astype(jnp.int32)
    chip_sums = [add_pair(p, g, half_idx) for p, g in zip(parts, gots)]
    return chip_exchange_start("scatter", chip_sums, chip_sums, name="scatter_start_" + tag)


def reduce_share_start(scatter_started, after, chip, *, tag):
    chip_sums, slots = chip_exchange_wait("scatter", scatter_started, after, name="scatter_wait_" + tag)
    slots = [lax.dynamic_update_slice(s_, lax.dynamic_index_in_dim(q, chip, 0, keepdims=True), (chip, 0, 0))
             for s_, q in zip(slots, chip_sums)]
    halves = [sum_slots(s_) for s_ in slots]
    return chip_exchange_start("share", halves, [_sds(t.shape, t.dtype) for t in halves], name="share_start_" + tag)


def reduce_finish(share_started, after, *, tag):
    halves, theirs = chip_exchange_wait("share", share_started, after, name="share_wait_" + tag)
    return list(zip(halves, theirs))


def allreduce_small(v):
    Rs = v.shape[0]

    def body(v_ref, o_ref, buf, send_sems, recv_sems):
        x, y, c = _place()
        me = 4 * x + 2 * y + c
        buf[me] = v_ref[...]
        flips = [(fx, fy, fc) for fx in (0, 1) for fy in (0, 1) for fc in (0, 1)][1:]
        sends = []
        for k, (fx, fy, fc) in enumerate(flips):
            to = ((1 - x) if fx else x, (1 - y) if fy else y, (1 - c) if fc else c)
            cp = _remote(v_ref, buf.at[me], send_sems, recv_sems, k, to)
            cp.start()
            sends.append(cp)
        for cp in sends:
            cp.wait()
        acc = buf[0]
        for d in range(1, 8):
            acc = acc + buf[d]
        o_ref[...] = acc

    vm = pl.BlockSpec(memory_space=pltpu.VMEM)
    return pl.pallas_call(
        body, out_shape=_sds((Rs, 128), F32), in_specs=[vm], out_specs=vm,
        scratch_shapes=[pltpu.VMEM((8, Rs, 128), F32), pltpu.SemaphoreType.DMA((7,)),
                        pltpu.SemaphoreType.DMA((7,))],
        name="allreduce_small", interpret=False)(v)


INPUT_NAMES = (("x", "mem") + WEIGHT_ORDER + ("loss_target",) + tuple("m_" + n for n in WEIGHT_ORDER)
               + tuple("v_" + n for n in WEIGHT_ORDER))


def _pack_small(vals, n_layers, extra=None):
    flat = jnp.concatenate([vals[n].reshape(n_layers, -1).astype(F32) for n in SMALL_ORDER], axis=1).reshape(-1)
    if extra is not None:
        flat = jnp.concatenate([flat, extra.reshape(-1)])
    n = flat.shape[0]
    rows = -(-n // 1024) * 8
    return jnp.pad(flat, (0, rows * 128 - n)).reshape(rows, 128)


def _unpack_small(packed, like, n_layers):
    per_layer = sum(int(np.prod(like[n].shape[1:])) for n in SMALL_ORDER)
    body = packed.reshape(-1)[:n_layers * per_layer].reshape(n_layers, per_layer)
    out, off = {}, 0
    for n in SMALL_ORDER:
        k = int(np.prod(like[n].shape[1:]))
        out[n] = body[:, off:off + k].reshape(like[n].shape)
        off += k
    return out, packed.reshape(-1)[n_layers * per_layer]


def kernel(x, mem, g_mix, w_in, g_cq, w_uq, g_ckv, w_ukv, g_mla_q, g_mla_k, b_f, g_fox_q, g_fox_k, rel_bias, g_ch_q, g_ch_k, w_br, w_out, g_cross, g_mem, w_xq, w_xkv, g_x_q, g_x_k, w_xo, g_mlp, w_1, w_2, loss_target, m_g_mix, m_w_in, m_g_cq, m_w_uq, m_g_ckv, m_w_ukv, m_g_mla_q, m_g_mla_k, m_b_f, m_g_fox_q, m_g_fox_k, m_rel_bias, m_g_ch_q, m_g_ch_k, m_w_br, m_w_out, m_g_cross, m_g_mem, m_w_xq, m_w_xkv, m_g_x_q, m_g_x_k, m_w_xo, m_g_mlp, m_w_1, m_w_2, v_g_mix, v_w_in, v_g_cq, v_w_uq, v_g_ckv, v_w_ukv, v_g_mla_q, v_g_mla_k, v_b_f, v_g_fox_q, v_g_fox_k, v_rel_bias, v_g_ch_q, v_g_ch_k, v_w_br, v_w_out, v_g_cross, v_g_mem, v_w_xq, v_w_xkv, v_g_x_q, v_g_x_k, v_w_xo, v_g_mlp, v_w_1, v_w_2):
    d = dict(zip(INPUT_NAMES, (x, mem, g_mix, w_in, g_cq, w_uq, g_ckv, w_ukv, g_mla_q, g_mla_k, b_f, g_fox_q, g_fox_k, rel_bias, g_ch_q, g_ch_k, w_br, w_out, g_cross, g_mem, w_xq, w_xkv, g_x_q, g_x_k, w_xo, g_mlp, w_1, w_2, loss_target, m_g_mix, m_w_in, m_g_cq, m_w_uq, m_g_ckv, m_w_ukv, m_g_mla_q, m_g_mla_k, m_b_f, m_g_fox_q, m_g_fox_k, m_rel_bias, m_g_ch_q, m_g_ch_k, m_w_br, m_w_out, m_g_cross, m_g_mem, m_w_xq, m_w_xkv, m_g_x_q, m_g_x_k, m_w_xo, m_g_mlp, m_w_1, m_w_2, v_g_mix, v_w_in, v_g_cq, v_w_uq, v_g_ckv, v_w_ukv, v_g_mla_q, v_g_mla_k, v_b_f, v_g_fox_q, v_g_fox_k, v_rel_bias, v_g_ch_q, v_g_ch_k, v_w_br, v_w_out, v_g_cross, v_g_mem, v_w_xq, v_w_xkv, v_g_x_q, v_g_x_k, v_w_xo, v_g_mlp, v_w_1, v_w_2)))
    n_layers = g_mix.shape[0]
    assert x.shape[0] == 1 and x.shape[2] == D_MODEL and mem.shape[1:] == (MEM_LEN, D_MODEL)
    for n in PACK_ORDER:
        assert d[n].shape[1:] == SHARD_SHAPES[n], (n, d[n].shape)

    chip = 2 * lax.axis_index("x") + lax.axis_index("y")
    core = lax.axis_index("c")

    def start_gather(l, names, after):
        mine = [_rows2d(d[n][l].astype(BF16), n) for n in names]
        return chip_exchange_start("gather", mine, [_sds((4,) + m_.shape, BF16) for m_ in mine],
                                   name=f"gather_start_{l}_{names[0]}", after=after)

    def finish_gather(l, names, started, after):
        mine, lands = chip_exchange_wait("gather", started, after, name=f"gather_wait_{l}_{names[0]}")
        full = [lax.dynamic_update_slice(t, m_[None], (chip, 0, 0)) for t, m_ in zip(gather_to_sibling(lands), mine)]
        return mine, full_weights({n: f.reshape((4,) + SHARD_SHAPES[n]) for n, f in zip(names, full)})

    in_flight = {"early": start_gather(0, EARLY_WEIGHTS, x)}

    def weights_of(l, x_l):
        mine, early = finish_gather(l, EARLY_WEIGHTS, in_flight.pop("early"), x_l)
        in_flight["mid"] = start_gather(l, MID_WEIGHTS, mine[0])
        in_flight["last"] = start_gather(l, LAST_WEIGHTS, in_flight["mid"][4])

        def mid_weights(after):
            mine, mid = finish_gather(l, MID_WEIGHTS, in_flight.pop("mid"), after)
            if l + 1 == n_layers:
                return mid, None
            in_flight["early"] = start_gather(l + 1, EARLY_WEIGHTS, mine[0])
            return mid, in_flight["early"][4]

        def last_weights(after):
            return finish_gather(l, LAST_WEIGHTS, in_flight.pop("last"), after)[1], None

        return early, mid_weights, last_weights, in_flight["mid"][4] + in_flight["last"][4]

    halves, scatters = {}, []

    def grads_done(l, stage, arg):
        names = LATE_WEIGHTS if stage.startswith("late") else EARLY_WEIGHTS
        tag = f"{l}_{names[0]}"
        if not stage.endswith("followup"):
            sg = shard_grads(arg)
            halves[tag] = reduce_halves_start([_rows2d(sg[n], n) for n in names], tag=tag)
            if (l, stage) != (0, "early"):
                return halves[tag][4]
            arg = halves[tag][4]
        elif tag not in halves:
            return None
        started = reduce_scatter_start(halves.pop(tag), arg, core, tag=tag)
        scatters.append((l, names, tag, started))
        return started[4]

    Ps = [layer_params(d, l) for l in range(n_layers)]
    loss, dx, dss = local_step(x[0], mem[0], loss_target[0], Ps, weights_of, grads_done)
    shares = [(l, names, tag, reduce_share_start(started, dx, chip, tag=tag))
              for l, names, tag, started in scatters]
    big = [{} for _ in range(n_layers)]
    for l, names, tag, started in shares:
        big[l].update(zip(names, reduce_finish(started, shares[-1][3][4], tag=tag)))

    small_local = {n: jnp.stack([dss[l][n].reshape(d[n].shape[1:]) for l in range(n_layers)]) for n in SMALL_ORDER}
    small_sum, loss_sum = _unpack_small(allreduce_small(_pack_small(small_local, n_layers, extra=loss)),
                                        {n: d[n] for n in SMALL_ORDER}, n_layers)

    grads, delta, new_m, new_v = {}, {}, {}, {}
    for n in PACK_ORDER:
        outs = adamw_halves(_rows2d(d[n], n), [big[l][n] for l in range(n_layers)], _rows2d(d["m_" + n], n),
                            _rows2d(d["v_" + n], n), core.reshape(1).astype(jnp.int32), name="adamw_" + n)
        grads[n], delta[n], new_m[n], new_v[n] = (o.reshape(d[n].shape) for o in outs)
    like = {n: d[n] for n in SMALL_ORDER}
    sm = adamw(_pack_small(like, n_layers)[None], [_pack_small(small_sum, n_layers)],
               _pack_small({n: d["m_" + n] for n in SMALL_ORDER}, n_layers)[None],
               _pack_small({n: d["v_" + n] for n in SMALL_ORDER}, n_layers)[None], name="adamw_small")
    for res, src in zip((grads, delta, new_m, new_v), sm):
        res.update(_unpack_small(src[0], like, n_layers)[0])

    return (loss_sum, dx[None], *[grads[n] for n in WEIGHT_ORDER], *[delta[n] for n in WEIGHT_ORDER],
            *[new_m[n] for n in WEIGHT_ORDER], *[new_v[n] for n in WEIGHT_ORDER])
```

```python
import functools

import numpy as np
import jax
import jax.numpy as jnp
from jax import lax
from jax.experimental import pallas as pl
from jax.experimental.pallas import tpu as pltpu

F32 = jnp.float32
BF16 = jnp.bfloat16
MXU_DTYPE = jnp.bfloat16
MESH = pl.DeviceIdType.MESH

D_MODEL = 2048
MIX_W = 1024
N_HEADS = 8
DH = 128
MLA_NOPE = 128
MLA_ROPE = 64
MLA_QK = MLA_NOPE + MLA_ROPE
MLA_PAD = 256
Q_LORA = 512
KV_LORA = 256
CHUNK = 64
CHUNK_SHIFT = CHUNK.bit_length() - 1
LEFT_CHUNKS = 8
REL_CLIP = 128
N_REL = 2 * REL_CLIP + 1
X_HEADS = 4
MEM_LEN = 256
D_FF = 8192
ROPE_THETA = 10000.0
EPS = 1e-6
NEG = -1e30

Z_CQ, Z_CKV, Z_KR, Z_FF = 0, 512, 768, 896
ZS_W = 1024
Z_FOX = 1024
Z_CH = Z_FOX + 3 * MIX_W
Z_GATE = Z_CH + 3 * MIX_W
Z_TOT = Z_GATE + 3 * D_MODEL
W_IN_CUTS = (0, 512, 768, 832, 3904, 3912, 6984, 13128)
W_IN_SHARD = W_IN_CUTS[-1] // 4
W_IN_PIECES = ((0, 0, 832), (Z_FF, 3904, 8), (Z_FOX, 832, 3072), (Z_CH, 3912, 9216))

ADAM_LR, ADAM_B1, ADAM_B2, ADAM_EPS, ADAM_WD, ADAM_STEP = 0.001, 0.9, 0.999, 1e-08, 0.01, 10

VMEM_LIMIT_BYTES = 56 * 1024 * 1024
PACK_ORDER = ("w_uq", "w_ukv", "w_br", "w_out", "w_xq", "w_xkv", "w_xo", "w_1", "w_2", "w_in")
SMALL_ORDER = ("g_mix", "g_cq", "g_ckv", "g_mla_q", "g_mla_k", "b_f", "g_fox_q", "g_fox_k", "rel_bias",
               "g_ch_q", "g_ch_k", "g_cross", "g_mem", "g_x_q", "g_x_k", "g_mlp")
WEIGHT_ORDER = ("g_mix", "w_in", "g_cq", "w_uq", "g_ckv", "w_ukv", "g_mla_q", "g_mla_k", "b_f", "g_fox_q",
                "g_fox_k", "rel_bias", "g_ch_q", "g_ch_k", "w_br", "w_out", "g_cross", "g_mem", "w_xq",
                "w_xkv", "g_x_q", "g_x_k", "w_xo", "g_mlp", "w_1", "w_2")


def _pick(n, prefs):
    for p in prefs:
        if n % p == 0:
            return p
    raise ValueError(f"no block size among {prefs} divides {n}")


def _pcall(body, *, name, out_shape, in_specs, out_specs, grid=(), scratch=(), aliases=None):
    return pl.pallas_call(
        body, out_shape=out_shape, grid=grid, in_specs=in_specs, out_specs=out_specs,
        scratch_shapes=scratch, name=name, interpret=False,
        input_output_aliases=aliases or {},
        compiler_params=pltpu.CompilerParams(vmem_limit_bytes=VMEM_LIMIT_BYTES))


def _sds(shape, dtype):
    return jax.ShapeDtypeStruct(tuple(shape), dtype)


def _mx(v):
    return v.astype(MXU_DTYPE)


def mm_nn(a, b3, *, name, out_dtype, a_col0=0, res=None, relu2=False):
    M = a.shape[0]
    nb, K, Ns = b3.shape
    N = nb * Ns
    tm = _pick(M, (1024, 512, 256, 128))
    tk = _pick(K, (2048, 1024, 512, 256))
    tn = _pick(Ns, (1024, 512, 256, 128))
    assert a_col0 % tk == 0
    nk, nbs, ka0 = K // tk, Ns // tn, a_col0 // tk
    n_out = 2 if relu2 else 1

    def body(*refs):
        a_ref, b_ref = refs[0], refs[1]
        pos = 2
        res_ref = None
        if res is not None:
            res_ref = refs[pos]
            pos += 1
        outs = refs[pos:pos + n_out]
        acc_ref = refs[pos + n_out] if nk > 1 else None
        part = jnp.dot(_mx(a_ref[...]), _mx(b_ref[...]), preferred_element_type=F32)

        def finish(acc):
            if res_ref is not None:
                acc = acc + res_ref[...]
            outs[0][...] = acc.astype(outs[0].dtype)
            if relu2:
                r = jnp.maximum(acc, 0.0)
                outs[1][...] = (r * r).astype(outs[1].dtype)

        if nk == 1:
            finish(part)
        else:
            k = pl.program_id(2)

            @pl.when(k == 0)
            def _():
                acc_ref[...] = part

            @pl.when(k > 0)
            def _():
                acc_ref[...] += part

            @pl.when(k == nk - 1)
            def _():
                finish(acc_ref[...])

    in_specs = [pl.BlockSpec((tm, tk), lambda i, j, k: (i, ka0 + k)),
                pl.BlockSpec((None, tk, tn), lambda i, j, k: (j // nbs, k, j % nbs))]
    args = [a, b3]
    if res is not None:
        in_specs.append(pl.BlockSpec((tm, tn), lambda i, j, k: (i, j)))
        args.append(res)
    o_spec = pl.BlockSpec((tm, tn), lambda i, j, k: (i, j))
    if relu2:
        out_shape, out_specs = (_sds((M, N), out_dtype), _sds((M, N), out_dtype)), (o_spec, o_spec)
    else:
        out_shape, out_specs = _sds((M, N), out_dtype), o_spec
    scratch = (pltpu.VMEM((tm, tn), F32),) if nk > 1 else ()
    return _pcall(body, name=name, out_shape=out_shape, grid=(M // tm, N // tn, nk),
                  in_specs=in_specs, out_specs=out_specs, scratch=scratch)(*args)


def mm_nt(a, b3, *, name, out_dtype, a_col0=0, res=None, relu_mul=None, after=None):
    M = a.shape[0]
    nb, K, Ns = b3.shape
    tm = _pick(M, (1024, 512, 256, 128))
    tk = _pick(K, (1024, 512, 256))
    tn = _pick(Ns, (2048, 1024, 512, 256, 128) if nb == 1 and Ns <= 2048 else (1024, 512, 256, 128))
    assert a_col0 % tn == 0
    nbs = Ns // tn
    nn, a0 = nb * nbs, a_col0 // tn

    def body(*refs):
        a_ref, b_ref = refs[0], refs[1]
        pos = 2
        mul_ref = res_ref = None
        if relu_mul is not None:
            mul_ref = refs[pos]
            pos += 1
        if res is not None:
            res_ref = refs[pos]
            pos += 1
        pos += after is not None
        o_ref = refs[pos]
        acc_ref = refs[pos + 1] if nn > 1 else None
        part = lax.dot_general(_mx(a_ref[...]), _mx(b_ref[...]), (((1,), (1,)), ((), ())),
                               preferred_element_type=F32)

        def finish(acc):
            if mul_ref is not None:
                acc = acc * (2.0 * jnp.maximum(mul_ref[...].astype(F32), 0.0))
            if res_ref is not None:
                acc = acc + res_ref[...]
            o_ref[...] = acc.astype(o_ref.dtype)

        if nn == 1:
            finish(part)
        else:
            j = pl.program_id(2)

            @pl.when(j == 0)
            def _():
                acc_ref[...] = part

            @pl.when(j > 0)
            def _():
                acc_ref[...] += part

            @pl.when(j == nn - 1)
            def _():
                finish(acc_ref[...])

    in_specs = [pl.BlockSpec((tm, tn), lambda i, kk, j: (i, a0 + j)),
                pl.BlockSpec((None, tk, tn), lambda i, kk, j: (j // nbs, kk, j % nbs))]
    args = [a, b3]
    for extra in (relu_mul, res):
        if extra is not None:
            in_specs.append(pl.BlockSpec((tm, tk), lambda i, kk, j: (i, kk)))
            args.append(extra)
    if after is not None:
        in_specs.append(pl.BlockSpec(memory_space=pl.ANY))
        args.append(after)
    scratch = (pltpu.VMEM((tm, tk), F32),) if nn > 1 else ()
    return _pcall(body, name=name, out_shape=_sds((M, K), out_dtype), grid=(M // tm, K // tk, nn),
                  in_specs=in_specs, out_specs=pl.BlockSpec((tm, tk), lambda i, kk, j: (i, kk)),
                  scratch=scratch)(*args)


def mm_tn(a, c, *, nb, name, out_dtype, K=None, N=None, a_col0=0, c_col0=0):
    M = a.shape[0]
    K = K or a.shape[1]
    N = N or c.shape[1]
    Ns = N // nb
    tm = _pick(M, (2048, 1024, 512, 256))
    tk = _pick(K, (1024, 512, 256))
    tn = _pick(Ns, (1024, 512, 256, 128))
    assert a_col0 % tk == 0 and c_col0 % tn == 0
    nm, nbs, a0, c0 = M // tm, Ns // tn, a_col0 // tk, c_col0 // tn

    def body(*refs):
        a_ref, c_ref, o_ref = refs[:3]
        acc_ref = refs[3] if nm > 1 else None
        part = lax.dot_general(_mx(a_ref[...]), _mx(c_ref[...]), (((0,), (0,)), ((), ())),
                               preferred_element_type=F32)
        if nm == 1:
            o_ref[...] = part.astype(o_ref.dtype)
        else:
            m = pl.program_id(2)

            @pl.when(m == 0)
            def _():
                acc_ref[...] = part

            @pl.when(m > 0)
            def _():
                acc_ref[...] += part

            @pl.when(m == nm - 1)
            def _():
                o_ref[...] = acc_ref[...].astype(o_ref.dtype)

    scratch = (pltpu.VMEM((tk, tn), F32),) if nm > 1 else ()
    return _pcall(
        body, name=name, out_shape=_sds((nb, K, Ns), out_dtype), grid=(N // tn, K // tk, nm),
        in_specs=[pl.BlockSpec((tm, tk), lambda j, kk, m: (m, a0 + kk)),
                  pl.BlockSpec((tm, tn), lambda j, kk, m: (m, c0 + j))],
        out_specs=pl.BlockSpec((None, tk, tn), lambda j, kk, m: (j // nbs, kk, j % nbs)),
        scratch=scratch)(a, c)


def rms_fwd(x, g, *, name, col0=0, width=None, out_dtype=BF16):
    R = x.shape[0]
    width = width or x.shape[1]
    assert col0 % width == 0
    cb = col0 // width
    tr = _pick(R, (512, 256, 128))

    def body(x_ref, g_ref, o_ref):
        xf = x_ref[...].astype(F32)
        r = lax.rsqrt(jnp.mean(xf * xf, axis=1, keepdims=True) + EPS)
        o_ref[...] = (xf * r * g_ref[...]).astype(o_ref.dtype)

    return _pcall(body, name=name, out_shape=_sds((R, width), out_dtype), grid=(R // tr,),
                  in_specs=[pl.BlockSpec((tr, width), lambda i: (i, cb)),
                            pl.BlockSpec((1, width), lambda i: (0, 0))],
                  out_specs=pl.BlockSpec((tr, width), lambda i: (i, 0)))(x, g)


def rms_bwd(x, g, dy, *, name, col0=0, width=None, res=None, dx_dtype=F32, need_dx=True):
    R = x.shape[0]
    width = width or x.shape[1]
    cb = col0 // width
    tr = _pick(R, (512, 256, 128))

    def body(*refs):
        x_ref, g_ref, dy_ref = refs[:3]
        pos = 3
        res_ref = None
        if res is not None:
            res_ref = refs[pos]
            pos += 1
        dx_ref = None
        if need_dx:
            dx_ref = refs[pos]
            pos += 1
        dg_ref = refs[pos]
        xf = x_ref[...].astype(F32)
        dyf = dy_ref[...].astype(F32)
        r = lax.rsqrt(jnp.mean(xf * xf, axis=1, keepdims=True) + EPS)
        xh = xf * r
        if need_dx:
            gy = dyf * g_ref[...]
            dx = r * (gy - xh * jnp.mean(gy * xh, axis=1, keepdims=True))
            if res_ref is not None:
                dx = dx + res_ref[...]
            dx_ref[...] = dx.astype(dx_ref.dtype)
        part = jnp.sum(dyf * xh, axis=0, keepdims=True)

        @pl.when(pl.program_id(0) == 0)
        def _():
            dg_ref[...] = part

        @pl.when(pl.program_id(0) > 0)
        def _():
            dg_ref[...] += part

    in_specs = [pl.BlockSpec((tr, width), lambda i: (i, cb)),
                pl.BlockSpec((1, width), lambda i: (0, 0)),
                pl.BlockSpec((tr, width), lambda i: (i, 0))]
    args = [x, g, dy]
    if res is not None:
        in_specs.append(pl.BlockSpec((tr, width), lambda i: (i, 0)))
        args.append(res)
    dg_shape, dg_spec = _sds((1, width), F32), pl.BlockSpec((1, width), lambda i: (0, 0))
    if need_dx:
        out_shape = (_sds((R, width), dx_dtype), dg_shape)
        out_specs = (pl.BlockSpec((tr, width), lambda i: (i, 0)), dg_spec)
    else:
        out_shape, out_specs = dg_shape, dg_spec
    out = _pcall(body, name=name, out_shape=out_shape, grid=(R // tr,), in_specs=in_specs,
                 out_specs=out_specs)(*args)
    return out if need_dx else (None, out)


HEAD_ROW_BLOCKS = (2048, 1024, 512, 256, 128)


def _rope_apply(y, c, sa, sb):
    return y * c + pltpu.roll(y, 96, 1) * sa + pltpu.roll(y, 32, 1) * sb


def _rope_transpose(dy, c, sa, sb):
    return dy * c + pltpu.roll(dy * sa, 32, 1) + pltpu.roll(dy * sb, 96, 1)


def rope_tables(seq):
    pos = jnp.arange(seq, dtype=F32)
    inv = ROPE_THETA ** (-jnp.arange(0, MLA_ROPE, 2, dtype=F32) / MLA_ROPE)
    ang = pos[:, None] * inv[None, :]
    cos, sin = jnp.cos(ang), jnp.sin(ang)
    z32, z64 = jnp.zeros_like(cos), jnp.zeros((seq, 64), F32)
    c = jnp.concatenate([cos, cos, z64], axis=1)
    sa = jnp.concatenate([-sin, z32, z64], axis=1)
    sb = jnp.concatenate([z32, sin, z64], axis=1)
    return c, sa, sb


def _head_vec(part_refs):
    xs = [p[...].astype(F32) for p in part_refs]
    return xs[0] if len(xs) == 1 else jnp.concatenate(xs, axis=1)


def prep_fwd(parts, g, *, name, n_heads, n_real, rope=None):
    rows = parts[0][0].shape[0]
    dh = sum(w for _, w, _ in parts)
    tr = _pick(rows, HEAD_ROW_BLOCKS)
    npart = len(parts)

    def body(*refs):
        part_refs, g_ref = refs[:npart], refs[npart]
        pos = npart + 1
        if rope is not None:
            c_ref, sa_ref, sb_ref = refs[pos:pos + 3]
            pos += 3
        o_ref = refs[pos]
        x = _head_vec(part_refs)
        r = lax.rsqrt(jnp.sum(x * x, axis=1, keepdims=True) * (1.0 / n_real) + EPS)
        y = x * r * g_ref[...]
        if rope is not None:
            yr = _rope_apply(y[:, dh - 128:], c_ref[...], sa_ref[...], sb_ref[...])
            y = jnp.concatenate([y[:, :dh - 128], yr], axis=1)
        o_ref[...] = y.astype(o_ref.dtype)

    in_specs, args = [], []
    for arr, w, fn in parts:
        in_specs.append(pl.BlockSpec((tr, w), functools.partial(lambda h, i, fn: (i, fn(h)), fn=fn)))
        args.append(arr)
    in_specs.append(pl.BlockSpec((1, dh), lambda h, i: (0, 0)))
    args.append(g)
    if rope is not None:
        for t in rope:
            in_specs.append(pl.BlockSpec((tr, 128), lambda h, i: (i, 0)))
            args.append(t)
    return _pcall(body, name=name, out_shape=_sds((n_heads, rows, dh), BF16), grid=(n_heads, rows // tr),
                  in_specs=in_specs, out_specs=pl.BlockSpec((None, tr, dh), lambda h, i: (h, i, 0)))(*args)


def _norm_bwd(x, g, dyn, n_real):
    r = lax.rsqrt(jnp.sum(x * x, axis=1, keepdims=True) * (1.0 / n_real) + EPS)
    xh = x * r
    gy = dyn * g
    dx = r * (gy - xh * (jnp.sum(gy * xh, axis=1, keepdims=True) * (1.0 / n_real)))
    return dx, jnp.sum(dyn * xh, axis=0, keepdims=True)


def prep_bwd_q(src, dy, g, *, name, n_heads, dh, n_real, rope=None, out_dtype=BF16):
    rows = src.shape[0]
    tr = _pick(rows, HEAD_ROW_BLOCKS)

    def body(*refs):
        x_ref, dy_ref, g_ref = refs[:3]
        pos = 3
        if rope is not None:
            c_ref, sa_ref, sb_ref = refs[pos:pos + 3]
            pos += 3
        dx_ref, dg_ref = refs[pos], refs[pos + 1]
        dyn = dy_ref[...].astype(F32)
        if rope is not None:
            dr = _rope_transpose(dyn[:, dh - 128:], c_ref[...], sa_ref[...], sb_ref[...])
            dyn = jnp.concatenate([dyn[:, :dh - 128], dr], axis=1)
        dx, dg = _norm_bwd(x_ref[...].astype(F32), g_ref[...], dyn, n_real)
        dx_ref[...] = dx.astype(dx_ref.dtype)
        first = jnp.logical_and(pl.program_id(0) == 0, pl.program_id(1) == 0)

        @pl.when(first)
        def _():
            dg_ref[...] = dg

        @pl.when(jnp.logical_not(first))
        def _():
            dg_ref[...] += dg

    in_specs = [pl.BlockSpec((tr, dh), lambda i, h: (i, h)),
                pl.BlockSpec((None, tr, dh), lambda i, h: (h, i, 0)),
                pl.BlockSpec((1, dh), lambda i, h: (0, 0))]
    args = [src, dy, g]
    if rope is not None:
        for t in rope:
            in_specs.append(pl.BlockSpec((tr, 128), lambda i, h: (i, 0)))
            args.append(t)
    return _pcall(body, name=name, out_shape=(_sds((rows, n_heads * dh), out_dtype), _sds((1, dh), F32)),
                  grid=(rows // tr, n_heads), in_specs=in_specs,
                  out_specs=(pl.BlockSpec((tr, dh), lambda i, h: (i, h)),
                             pl.BlockSpec((1, dh), lambda i, h: (0, 0))))(*args)


def prep_bwd_mla_k(kv_raw, zs, dkf, dv, g, rope, *, name):
    rows = kv_raw.shape[0]
    tr = _pick(rows, HEAD_ROW_BLOCKS)

    def body(kn_ref, kr_ref, dy_ref, dv_ref, g_ref, c_ref, sa_ref, sb_ref, dkv_ref, dkr_ref, dg_ref):
        h = pl.program_id(1)
        x = jnp.concatenate([kn_ref[...].astype(F32), kr_ref[...].astype(F32)], axis=1)
        dyn = dy_ref[...].astype(F32)
        dr = _rope_transpose(dyn[:, 128:], c_ref[...], sa_ref[...], sb_ref[...])
        dyn = jnp.concatenate([dyn[:, :128], dr], axis=1)
        dx, dg = _norm_bwd(x, g_ref[...], dyn, MLA_QK)
        dkv_ref[...] = jnp.concatenate([dx[:, :128], dv_ref[...].astype(F32)], axis=1).astype(dkv_ref.dtype)

        @pl.when(h == 0)
        def _():
            dkr_ref[...] = dx[:, 128:]

        @pl.when(h > 0)
        def _():
            dkr_ref[...] += dx[:, 128:]

        first = jnp.logical_and(pl.program_id(0) == 0, h == 0)

        @pl.when(first)
        def _():
            dg_ref[...] = dg

        @pl.when(jnp.logical_not(first))
        def _():
            dg_ref[...] += dg

    tab = pl.BlockSpec((tr, 128), lambda i, h: (i, 0))
    return _pcall(
        body, name=name,
        out_shape=(_sds((rows, N_HEADS * 256), BF16), _sds((rows, 128), F32), _sds((1, MLA_PAD), F32)),
        grid=(rows // tr, N_HEADS),
        in_specs=[pl.BlockSpec((tr, 128), lambda i, h: (i, 2 * h)),
                  pl.BlockSpec((tr, 128), lambda i, h: (i, Z_KR // 128)),
                  pl.BlockSpec((None, tr, MLA_PAD), lambda i, h: (h, i, 0)),
                  pl.BlockSpec((None, tr, 128), lambda i, h: (h, i, 0)),
                  pl.BlockSpec((1, MLA_PAD), lambda i, h: (0, 0)), tab, tab, tab],
        out_specs=(pl.BlockSpec((tr, 256), lambda i, h: (i, h)),
                   pl.BlockSpec((tr, 128), lambda i, h: (i, 0)),
                   pl.BlockSpec((1, MLA_PAD), lambda i, h: (0, 0))))(kv_raw, zs, dkf, dv, g, *rope)


def prep_bwd_groups(src, base_blk, dys, gs, *, name, n_heads, kinds, out_dtype=BF16):
    rows = src.shape[0]
    ng = len(kinds)
    J = ng * n_heads
    tr = _pick(rows, HEAD_ROW_BLOCKS)
    gstack = jnp.stack([gs[k] if kinds[k] == "norm" else jnp.ones((1, DH), F32) for k in range(ng)])

    def body(*refs):
        x_ref = refs[0]
        dy_refs = refs[1:1 + ng]
        g_ref, dx_ref, dg_ref = refs[1 + ng:4 + ng]
        j, i = pl.program_id(0), pl.program_id(1)
        grp = j // n_heads
        dy = dy_refs[0][...].astype(F32)
        for k in range(1, ng):
            dy = jnp.where(grp == k, dy_refs[k][...].astype(F32), dy)
        dx, dg = _norm_bwd(x_ref[...].astype(F32), g_ref[...], dy, DH)
        is_copy = functools.reduce(jnp.logical_or, [grp == k for k in range(ng) if kinds[k] == "copy"],
                                   jnp.bool_(False))
        dx_ref[...] = jnp.where(is_copy, dy, dx).astype(dx_ref.dtype)
        dg = jnp.where(is_copy, jnp.zeros_like(dg), dg)
        first = jnp.logical_and(j % n_heads == 0, i == 0)

        @pl.when(first)
        def _():
            dg_ref[...] = dg

        @pl.when(jnp.logical_not(first))
        def _():
            dg_ref[...] += dg

    in_specs = [pl.BlockSpec((tr, DH), lambda j, i: (i, base_blk + j))]
    for k in range(ng):
        in_specs.append(pl.BlockSpec(
            (None, tr, DH),
            functools.partial(lambda j, i, k: (jnp.clip(j - k * n_heads, 0, n_heads - 1),
                                               jnp.where(j // n_heads == k, i, 0), 0), k=k)))
    in_specs.append(pl.BlockSpec((None, 1, DH), lambda j, i: (j // n_heads, 0, 0)))
    return _pcall(body, name=name, out_shape=(_sds((rows, J * DH), out_dtype), _sds((ng, 1, DH), F32)),
                  grid=(J, rows // tr), in_specs=in_specs,
                  out_specs=(pl.BlockSpec((tr, DH), lambda j, i: (i, j)),
                             pl.BlockSpec((None, 1, DH), lambda j, i: (j // n_heads, 0, 0))))(src, *dys, gstack)


def _attn_cfg(mode, sq, sk):
    if mode == "chunk":
        tq = 128
        win = min((LEFT_CHUNKS + 2) * CHUNK, sk)
    else:
        tq = _pick(sq, (256, 128))
        win = sk
    scale = (MLA_QK if mode == "mla" else DH) ** -0.5
    return tq, win, scale


def _attn_key_rows(mode, i, tq, win, sk, run):
    if mode == "chunk":
        start = pl.multiple_of(jnp.clip((i - LEFT_CHUNKS // 2) * 128, 0, sk - win), 128)
        run(pl.ds(start, win), start)
    elif mode == "cross":
        run(slice(0, sk), 0)
    else:
        lax.switch(i, [functools.partial(run, slice(0, (b + 1) * tq), 0) for b in range(sk // tq)])


def _attn_scores(mode, i, tq, scale, q, kk, start, cq, ck, t_ref):
    nk = kk.shape[0]
    s = lax.dot_general(q, kk, (((1,), (1,)), ((), ())), preferred_element_type=F32) * scale
    if mode == "cross":
        return s
    t_pos = i * tq + lax.broadcasted_iota(jnp.int32, (tq, nk), 0)
    s_pos = start + lax.broadcasted_iota(jnp.int32, (tq, nk), 1)
    if mode == "fox":
        s = s + cq - ck
        allowed = s_pos <= t_pos
    else:
        qc, kc = lax.shift_right_logical(t_pos, CHUNK_SHIFT), lax.shift_right_logical(s_pos, CHUNK_SHIFT)
        allowed = kc <= qc
        if mode == "chunk":
            allowed = jnp.logical_and(allowed, kc >= qc - LEFT_CHUNKS)
            tiles = []
            for w in range(nk // 128):
                delta = i - (start // 128 + w)
                tiles.append(jnp.where(delta == 0, t_ref[0], jnp.where(delta == 1, t_ref[1], t_ref[2])))
            s = s + jnp.concatenate(tiles, axis=1)
    return jnp.where(allowed, s, NEG)


def attn_fwd(q, k, v_arr, v_blk, *, mode, name, n_heads, cq=None, ck=None, tiles=None):
    _, sq, dk = q.shape
    sk = k.shape[1]
    tq, win, scale = _attn_cfg(mode, sq, sk)

    def body(*refs):
        q_ref, k_ref, v_ref = refs[:3]
        pos = 3
        cq_ref = ck_ref = t_ref = None
        if mode == "fox":
            cq_ref, ck_ref = refs[pos:pos + 2]
            pos += 2
        if mode == "chunk":
            t_ref = refs[pos]
            pos += 1
        o_ref, lse_ref = refs[pos], refs[pos + 1]
        i = pl.program_id(1)

        def run(rows, start):
            cq, ck = (cq_ref[...], ck_ref[:, rows]) if mode == "fox" else (None, None)
            s = _attn_scores(mode, i, tq, scale, q_ref[...], k_ref[rows, :], start, cq, ck, t_ref)
            m = jnp.max(s, axis=1, keepdims=True)
            e = jnp.exp(s - m)
            l = jnp.sum(e, axis=1, keepdims=True)
            p = e * (1.0 / l)
            o_ref[...] = jnp.dot(_mx(p), _mx(v_ref[rows, :]), preferred_element_type=F32).astype(o_ref.dtype)
            lse_ref[...] = m + jnp.log(l)

        _attn_key_rows(mode, i, tq, win, sk, run)

    in_specs = [pl.BlockSpec((None, tq, dk), lambda h, i: (h, i, 0)),
                pl.BlockSpec((None, sk, dk), lambda h, i: (h, 0, 0)),
                pl.BlockSpec((sk, DH), lambda h, i: (0, v_blk(h)))]
    args = [q, k, v_arr]
    if mode == "fox":
        in_specs += [pl.BlockSpec((None, tq, 1), lambda h, i: (h, i, 0)),
                     pl.BlockSpec((None, 1, sk), lambda h, i: (h, 0, 0))]
        args += [cq, ck]
    if mode == "chunk":
        in_specs.append(pl.BlockSpec((3, None, 128, 128), lambda h, i: (0, h, 0, 0)))
        args.append(tiles)
    return _pcall(body, name=name,
                  out_shape=(_sds((sq, n_heads * DH), BF16), _sds((n_heads, sq, 1), F32)),
                  grid=(n_heads, sq // tq), in_specs=in_specs,
                  out_specs=(pl.BlockSpec((tq, DH), lambda h, i: (i, h)),
                             pl.BlockSpec((None, tq, 1), lambda h, i: (h, i, 0))))(*args)


def attn_bwd(q, k, v_arr, v_blk, o, do, lse, *, mode, name, n_heads, cq=None, ck=None, tiles=None):
    _, sq, dk = q.shape
    sk = k.shape[1]
    tq, win, scale = _attn_cfg(mode, sq, sk)
    n_extra = {"fox": 2, "chunk": 1}.get(mode, 0)

    def body(*refs):
        q_ref, k_ref, v_ref, o_ref, do_ref, lse_ref = refs[:6]
        pos = 6
        cq_ref = ck_ref = t_ref = None
        if mode == "fox":
            cq_ref, ck_ref = refs[pos:pos + 2]
            pos += 2
        if mode == "chunk":
            t_ref = refs[pos]
            pos += 1
        dq_ref, dk_ref, dv_ref = refs[pos:pos + 3]
        extra = refs[pos + 3:pos + 3 + n_extra]
        i = pl.program_id(1)

        @pl.when(i == 0)
        def _():
            dk_ref[...] = jnp.zeros_like(dk_ref)
            dv_ref[...] = jnp.zeros_like(dv_ref)
            if mode == "fox":
                extra[1][...] = jnp.zeros_like(extra[1])
            if mode == "chunk":
                extra[0][...] = jnp.zeros_like(extra[0])

        def run(rows, start):
            q = q_ref[...]
            do = do_ref[...]
            kk = k_ref[rows, :]
            cq, ck = (cq_ref[...], ck_ref[:, rows]) if mode == "fox" else (None, None)
            s = _attn_scores(mode, i, tq, scale, q, kk, start, cq, ck, t_ref)
            p = jnp.exp(s - lse_ref[...])
            drow = jnp.sum(do.astype(F32) * o_ref[...].astype(F32), axis=1, keepdims=True)
            dp = lax.dot_general(do, _mx(v_ref[rows, :]), (((1,), (1,)), ((), ())), preferred_element_type=F32)
            ds = p * (dp - drow)
            dsb = _mx(ds)
            dq_ref[...] = (jnp.dot(dsb, kk, preferred_element_type=F32) * scale).astype(dq_ref.dtype)
            dk_ref[rows, :] += lax.dot_general(dsb, q, (((0,), (0,)), ((), ())),
                                               preferred_element_type=F32) * scale
            dv_ref[rows, :] += lax.dot_general(_mx(p), do, (((0,), (0,)), ((), ())), preferred_element_type=F32)
            if mode == "chunk":
                dt_ref = extra[0]
                for w in range(win // 128):
                    delta = i - (start // 128 + w)
                    tile = ds[:, w * 128:(w + 1) * 128]
                    zero = jnp.zeros_like(tile)
                    dt_ref[0] += jnp.where(delta == 0, tile, zero)
                    dt_ref[1] += jnp.where(delta == 1, tile, zero)
                    dt_ref[2] += jnp.where(delta >= 2, tile, zero)
            if mode == "fox":
                extra[0][...] = jnp.sum(ds, axis=1, keepdims=True)
                extra[1][:, rows] -= jnp.sum(ds, axis=0, keepdims=True)

        _attn_key_rows(mode, i, tq, win, sk, run)

    in_specs = [pl.BlockSpec((None, tq, dk), lambda h, i: (h, i, 0)),
                pl.BlockSpec((None, sk, dk), lambda h, i: (h, 0, 0)),
                pl.BlockSpec((sk, DH), lambda h, i: (0, v_blk(h))),
                pl.BlockSpec((tq, DH), lambda h, i: (i, h)),
                pl.BlockSpec((tq, DH), lambda h, i: (i, h)),
                pl.BlockSpec((None, tq, 1), lambda h, i: (h, i, 0))]
    args = [q, k, v_arr, o, do, lse]
    out_shape = [_sds((n_heads, sq, dk), F32), _sds((n_heads, sk, dk), F32), _sds((n_heads, sk, DH), F32)]
    out_specs = [pl.BlockSpec((None, tq, dk), lambda h, i: (h, i, 0)),
                 pl.BlockSpec((None, sk, dk), lambda h, i: (h, 0, 0)),
                 pl.BlockSpec((None, sk, DH), lambda h, i: (h, 0, 0))]
    if mode == "fox":
        in_specs += [pl.BlockSpec((None, tq, 1), lambda h, i: (h, i, 0)),
                     pl.BlockSpec((None, 1, sk), lambda h, i: (h, 0, 0))]
        args += [cq, ck]
        out_shape += [_sds((n_heads, sq, 1), F32), _sds((n_heads, 1, sk), F32)]
        out_specs += [pl.BlockSpec((None, tq, 1), lambda h, i: (h, i, 0)),
                      pl.BlockSpec((None, 1, sk), lambda h, i: (h, 0, 0))]
    if mode == "chunk":
        in_specs.append(pl.BlockSpec((3, None, 128, 128), lambda h, i: (0, h, 0, 0)))
        args.append(tiles)
        out_shape.append(_sds((3, n_heads, 128, 128), F32))
        out_specs.append(pl.BlockSpec((3, None, 128, 128), lambda h, i: (0, h, 0, 0)))
    return _pcall(body, name=name, out_shape=tuple(out_shape), grid=(n_heads, sq // tq),
                  in_specs=in_specs, out_specs=tuple(out_specs))(*args)


REL_LANES = 384
REL_KBLK = 2048


def _rel_onehot(t, k):
    rho = k * REL_KBLK + lax.broadcasted_iota(jnp.int32, (REL_KBLK, REL_LANES), 0)
    lane = lax.broadcasted_iota(jnp.int32, (REL_KBLK, REL_LANES), 1)
    diff = lax.shift_right_logical(rho, 7) - jnp.bitwise_and(rho, 127)
    idx = jnp.where(t == 0, diff + REL_CLIP,
                    jnp.where(t == 1, jnp.minimum(diff + 128, REL_CLIP) + REL_CLIP, N_REL - 1))
    return jnp.where(idx == lane, 1.0, 0.0).astype(F32)


def _split3(x):
    hi = x.astype(BF16)
    rest = x - hi.astype(F32)
    mid = rest.astype(BF16)
    return hi, mid, (rest - mid.astype(F32)).astype(BF16)


def relbias_onehot():
    def body(o_ref):
        o_ref[...] = _rel_onehot(pl.program_id(0), pl.program_id(1)).astype(o_ref.dtype)

    return _pcall(body, name="relbias_onehot", out_shape=_sds((3, 128 * 128, REL_LANES), BF16),
                  grid=(3, 128 * 128 // REL_KBLK), in_specs=[],
                  out_specs=pl.BlockSpec((None, REL_KBLK, REL_LANES), lambda t, k: (t, k, 0)))()


ONEHOT_SPEC = pl.BlockSpec((None, REL_KBLK, REL_LANES), lambda t, k: (t, k, 0))


def relbias_tiles(rel_bias, onehot):
    nh = rel_bias.shape[0]
    rb = jnp.pad(rel_bias, ((0, 0), (0, REL_LANES - N_REL)))

    def body(rb_ref, e_ref, o_ref):
        e = e_ref[...]
        hi, mid, lo = [lax.dot_general(t, e, (((1,), (1,)), ((), ())), preferred_element_type=F32)
                       for t in _split3(rb_ref[...])]
        o_ref[...] = (hi + mid) + lo

    flat = _pcall(body, name="relbias_tiles", out_shape=_sds((3, nh, 128 * 128), F32),
                  grid=(3, 128 * 128 // REL_KBLK),
                  in_specs=[pl.BlockSpec((nh, REL_LANES), lambda t, k: (0, 0)), ONEHOT_SPEC],
                  out_specs=pl.BlockSpec((None, nh, REL_KBLK), lambda t, k: (t, 0, k)))(rb, onehot)
    return flat.reshape(3, nh, 128, 128)


def relbias_tiles_bwd(dtiles, onehot):
    nh = dtiles.shape[1]

    def body(dt_ref, e_ref, o_ref):
        t, k = pl.program_id(0), pl.program_id(1)
        e = e_ref[...]
        hi, mid, lo = [jnp.dot(t_, e, preferred_element_type=F32) for t_ in _split3(dt_ref[...])]
        part = (hi + mid) + lo
        first = jnp.logical_and(t == 0, k == 0)

        @pl.when(first)
        def _():
            o_ref[...] = part

        @pl.when(jnp.logical_not(first))
        def _():
            o_ref[...] += part

    out = _pcall(body, name="relbias_tiles_bwd", out_shape=_sds((nh, REL_LANES), F32),
                 grid=(3, 128 * 128 // REL_KBLK),
                 in_specs=[pl.BlockSpec((None, nh, REL_KBLK), lambda t, k: (t, 0, k)), ONEHOT_SPEC],
                 out_specs=pl.BlockSpec((nh, REL_LANES), lambda t, k: (0, 0)))(
                     dtiles.reshape(3, nh, 128 * 128), onehot)
    return out[:, :N_REL]


CUM_BLK = 256


def _tri(n, lower):
    r = lax.broadcasted_iota(jnp.int32, (n, n), 0)
    c = lax.broadcasted_iota(jnp.int32, (n, n), 1)
    return jnp.where(r >= c if lower else r <= c, 1.0, 0.0).astype(F32)


def fox_cum_fwd(zs, bf):
    S = zs.shape[0]
    tb = min(CUM_BLK, S)

    def body(f_ref, b_ref, cum_ref, cumt_ref, carry_ref):
        @pl.when(pl.program_id(0) == 0)
        def _():
            carry_ref[...] = jnp.zeros_like(carry_ref)

        x = f_ref[...] + b_ref[...]
        lane = lax.broadcasted_iota(jnp.int32, x.shape, 1)
        logf = jnp.where(lane < N_HEADS, jnp.minimum(x, 0.0) - jnp.log(1.0 + jnp.exp(-jnp.abs(x))), 0.0)
        cum = jnp.dot(_tri(tb, True), logf, preferred_element_type=F32,
                      precision=lax.Precision.HIGHEST) + carry_ref[...]
        carry_ref[...] = cum[tb - 1:tb, :]
        cum_ref[...] = cum
        cumt_ref[...] = cum.T

    return _pcall(body, name="fox_cum_fwd", out_shape=(_sds((S, 128), F32), _sds((128, S), F32)),
                  grid=(S // tb,),
                  in_specs=[pl.BlockSpec((tb, 128), lambda i: (i, Z_FF // 128)),
                            pl.BlockSpec((1, 128), lambda i: (0, 0))],
                  out_specs=(pl.BlockSpec((tb, 128), lambda i: (i, 0)),
                             pl.BlockSpec((128, tb), lambda i: (0, i))),
                  scratch=(pltpu.VMEM((1, 128), F32),))(zs, bf)


def fox_cum_bwd(zs, bf, dcum):
    S = zs.shape[0]
    tb = min(CUM_BLK, S)
    nblk = S // tb

    def body(f_ref, b_ref, d_ref, df_ref, db_ref, carry_ref):
        @pl.when(pl.program_id(0) == 0)
        def _():
            carry_ref[...] = jnp.zeros_like(carry_ref)
            db_ref[...] = jnp.zeros_like(db_ref)

        d = d_ref[...]
        dlogf = jnp.dot(_tri(tb, False), d, preferred_element_type=F32,
                        precision=lax.Precision.HIGHEST) + carry_ref[...]
        carry_ref[...] += jnp.sum(d, axis=0, keepdims=True)
        x = f_ref[...] + b_ref[...]
        lane = lax.broadcasted_iota(jnp.int32, x.shape, 1)
        dx = jnp.where(lane < N_HEADS, dlogf / (1.0 + jnp.exp(x)), 0.0)
        df_ref[...] = dx
        db_ref[...] += jnp.sum(dx, axis=0, keepdims=True)

    return _pcall(body, name="fox_cum_bwd", out_shape=(_sds((S, 128), F32), _sds((1, 128), F32)),
                  grid=(nblk,),
                  in_specs=[pl.BlockSpec((tb, 128), lambda i: (nblk - 1 - i, Z_FF // 128)),
                            pl.BlockSpec((1, 128), lambda i: (0, 0)),
                            pl.BlockSpec((tb, 128), lambda i: (nblk - 1 - i, 0))],
                  out_specs=(pl.BlockSpec((tb, 128), lambda i: (nblk - 1 - i, 0)),
                             pl.BlockSpec((1, 128), lambda i: (0, 0))),
                  scratch=(pltpu.VMEM((1, 128), F32),))(zs, bf, dcum)


def _sigmoid(x):
    return 1.0 / (1.0 + jnp.exp(-x))


def merge_fwd(z, projs):
    S = z.shape[0]
    tr, tc = _pick(S, (512, 256, 128)), 512
    nbc = D_MODEL // tc
    g0 = Z_GATE // tc

    def body(g0_ref, g1_ref, g2_ref, p0_ref, p1_ref, p2_ref, o_ref):
        acc = _sigmoid(g0_ref[...]) * p0_ref[...]
        acc += _sigmoid(g1_ref[...]) * p1_ref[...]
        acc += _sigmoid(g2_ref[...]) * p2_ref[...]
        o_ref[...] = acc.astype(o_ref.dtype)

    gspecs = [pl.BlockSpec((tr, tc), functools.partial(lambda i, j, n: (i, g0 + n * nbc + j), n=n))
              for n in range(3)]
    pspec = pl.BlockSpec((tr, tc), lambda i, j: (i, j))
    return _pcall(body, name="merge_fwd", out_shape=_sds((S, D_MODEL), BF16), grid=(S // tr, nbc),
                  in_specs=gspecs + [pspec] * 3, out_specs=pspec)(z, z, z, *projs)


def merge_bwd(z, projs, dmerged):
    S = z.shape[0]
    tr, tc = _pick(S, (512, 256, 128)), 512
    nbc = D_MODEL // tc
    g0 = Z_GATE // tc

    def body(g0_ref, g1_ref, g2_ref, p0_ref, p1_ref, p2_ref, dm_ref, dg0, dg1, dg2, dp0, dp1, dp2):
        dm = dm_ref[...]
        for g_ref, p_ref, dg_ref, dp_ref in ((g0_ref, p0_ref, dg0, dp0), (g1_ref, p1_ref, dg1, dp1),
                                             (g2_ref, p2_ref, dg2, dp2)):
            sg = _sigmoid(g_ref[...])
            dp_ref[...] = (dm * sg).astype(dp_ref.dtype)
            dg_ref[...] = (dm * p_ref[...] * sg * (1.0 - sg)).astype(dg_ref.dtype)

    gspecs = [pl.BlockSpec((tr, tc), functools.partial(lambda i, j, n: (i, g0 + n * nbc + j), n=n))
              for n in range(3)]
    pspec = pl.BlockSpec((tr, tc), lambda i, j: (i, j))
    out = _pcall(body, name="merge_bwd", out_shape=tuple(_sds((S, D_MODEL), BF16) for _ in range(6)),
                 grid=(S // tr, nbc), in_specs=gspecs + [pspec] * 4,
                 out_specs=tuple([pspec] * 6))(z, z, z, *projs, dmerged)
    return out[:3], out[3:]


def loss_head(y, target):
    S, D = y.shape
    tr = _pick(S, (256, 128))

    def body(y_ref, t_ref, dy_ref, l_ref):
        e = y_ref[...] - t_ref[...]
        dy_ref[...] = e * (1.0 / D)
        part = jnp.sum(jnp.sum(e * e, axis=1, keepdims=True), axis=0, keepdims=True) * (0.5 / D)

        @pl.when(pl.program_id(0) == 0)
        def _():
            l_ref[...] = part

        @pl.when(pl.program_id(0) > 0)
        def _():
            l_ref[...] += part

    spec = pl.BlockSpec((tr, D), lambda i: (i, 0))
    return _pcall(body, name="loss_head", out_shape=(_sds((S, D), F32), _sds((1, 1), F32)), grid=(S // tr,),
                  in_specs=[spec, spec], out_specs=(spec, pl.BlockSpec((1, 1), lambda i: (0, 0))))(y, target)


def _adamw_update(w, g, m, v):
    nm = ADAM_B1 * m + (1.0 - ADAM_B1) * g
    nv = ADAM_B2 * v + (1.0 - ADAM_B2) * (g * g)
    delta = -ADAM_LR * ((nm / (1.0 - ADAM_B1 ** ADAM_STEP)) / (jnp.sqrt(nv / (1.0 - ADAM_B2 ** ADAM_STEP)) + ADAM_EPS)
                        + ADAM_WD * w)
    return delta, nm, nv


ADAMW_BLOCK_BYTES = 1024 * 1024


def adamw_halves(w, pairs, m, v, core, *, name):
    L, R, C = w.shape
    h = R // 2
    tr = _pick(h, (512, 256, 128, 64, 32, 16, 8))
    while tr * C * 4 > ADAMW_BLOCK_BYTES and tr % 16 == 0:
        tr //= 2
    nbh = h // tr

    def body(*refs):
        c_ref, w_ref, m_ref, v_ref = refs[:4]
        g_refs = refs[4:4 + 2 * L]
        go_ref, d_ref, nm_ref, nv_ref = refs[4 + 2 * L:]
        l, mine = pl.program_id(0), pl.program_id(1) // nbh == c_ref[0]
        g_ = jnp.where(mine, g_refs[0][...], g_refs[1][...])
        for k in range(1, L):
            g_ = jnp.where(l == k, jnp.where(mine, g_refs[2 * k][...], g_refs[2 * k + 1][...]), g_)
        go_ref[...] = g_
        d_ref[...], nm_ref[...], nv_ref[...] = _adamw_update(w_ref[...], g_, m_ref[...], v_ref[...])

    def g_index(l, i, c_ref, k, own):
        on = jnp.logical_and(l == k, (i // nbh == c_ref[0]) == own)
        return jnp.where(on, i % nbh, 0), 0

    spec = pl.BlockSpec((None, tr, C), lambda l, i, c_ref: (l, i, 0))
    gspecs = [pl.BlockSpec((tr, C), functools.partial(g_index, k=k, own=own)) for k in range(L) for own in (True, False)]
    grid_spec = pltpu.PrefetchScalarGridSpec(num_scalar_prefetch=1, grid=(L, R // tr),
                                             in_specs=[spec] * 3 + gspecs, out_specs=(spec,) * 4)
    return pl.pallas_call(
        body, out_shape=tuple(_sds((L, R, C), F32) for _ in range(4)), grid_spec=grid_spec, name=name,
        interpret=False, compiler_params=pltpu.CompilerParams(vmem_limit_bytes=VMEM_LIMIT_BYTES))(
            core, w, m, v, *[a for pair in pairs for a in pair])


def adamw(w, gs, m, v, *, name):
    L, R, C = w.shape
    tr = _pick(R, (512, 256, 128, 64, 32, 16, 8))
    while tr * C * 4 > ADAMW_BLOCK_BYTES and tr % 16 == 0:
        tr //= 2

    def body(*refs):
        w_ref, m_ref, v_ref = refs[:3]
        g_refs = refs[3:3 + L]
        go_ref, d_ref, nm_ref, nv_ref = refs[3 + L:]
        l = pl.program_id(0)
        g_ = g_refs[0][...]
        for k in range(1, L):
            g_ = jnp.where(l == k, g_refs[k][...], g_)
        go_ref[...] = g_
        d_ref[...], nm_ref[...], nv_ref[...] = _adamw_update(w_ref[...], g_, m_ref[...], v_ref[...])

    spec = pl.BlockSpec((None, tr, C), lambda l, i: (l, i, 0))
    gspecs = [pl.BlockSpec((tr, C), functools.partial(lambda l, i, k: (jnp.where(l == k, i, 0), 0), k=k))
              for k in range(L)]
    return _pcall(body, name=name, out_shape=tuple(_sds((L, R, C), F32) for _ in range(4)),
                  grid=(L, R // tr), in_specs=[spec] * 3 + gspecs, out_specs=(spec,) * 4)(w, m, v, *gs)


SHARD_SHAPES = {
    "w_uq": (Q_LORA, 384), "w_ukv": (KV_LORA, 512), "w_br": (3, MIX_W, 512), "w_out": (512, D_MODEL),
    "w_xq": (512, 512), "w_xkv": (512, 1024), "w_xo": (512, 512), "w_1": (D_MODEL, 2048),
    "w_2": (2048, D_MODEL), "w_in": (D_MODEL, W_IN_SHARD),
}


def _rows2d(a, name):
    shp = SHARD_SHAPES[name]
    return a.reshape(a.shape[:a.ndim - len(shp)] + (-1, shp[-1]))


def _cols_from_shards(g):
    return jnp.transpose(g, (1, 0, 2)).reshape(g.shape[1], 4 * g.shape[2])


def _cols_to_shards(w):
    return jnp.transpose(w.reshape(w.shape[0], 4, w.shape[1] // 4), (1, 0, 2))


EARLY_WEIGHTS = ("w_in", "w_uq", "w_ukv")
MID_WEIGHTS = ("w_br", "w_out", "w_xq", "w_xkv", "w_xo")
LAST_WEIGHTS = ("w_1", "w_2")
LATE_WEIGHTS = MID_WEIGHTS + LAST_WEIGHTS
assert sorted(EARLY_WEIGHTS + LATE_WEIGHTS) == sorted(PACK_ORDER)


def _w_in_relaid(g_in):
    zeros = lambda n: [jnp.zeros((D_MODEL, n), g_in.dtype)] if n else []
    segs, at = [], 0
    for p0, o0, w in sorted(W_IN_PIECES):
        segs += zeros(p0 - at)
        while w > 0:
            s_, a = divmod(o0, W_IN_SHARD)
            take = min(w, W_IN_SHARD - a)
            segs.append(g_in[s_][:, a:a + take])
            o0, p0, w = o0 + take, p0 + take, w - take
        at = p0
    return jnp.concatenate(segs + zeros(Z_TOT - at), axis=1)


def full_weights(g):
    forms = {
        "w_in": lambda a: {"in_p": _w_in_relaid(a)[None]},
        "w_uq": lambda a: {"uq_p": jnp.pad(_cols_from_shards(a).reshape(Q_LORA, N_HEADS, MLA_QK),
                                           ((0, 0), (0, 0), (0, MLA_PAD - MLA_QK))
                                           ).reshape(1, Q_LORA, N_HEADS * MLA_PAD)},
        "w_ukv": lambda a: {"ukv": a},
        "w_br": lambda a: {"br": [a[:, n] for n in range(3)]},
        "w_out": lambda a: {"out": a.reshape(1, D_MODEL, D_MODEL)},
        "w_xq": lambda a: {"xq": a.reshape(1, D_MODEL, 512)},
        "w_xkv": lambda a: {"xkv": a.reshape(1, D_MODEL, 1024)},
        "w_xo": lambda a: {"xo": a},
        "w_1": lambda a: {"w1": a},
        "w_2": lambda a: {"w2": a.reshape(1, D_FF, D_MODEL)},
    }
    out = {}
    for n, a in g.items():
        out.update(forms[n](a))
    return out


def _w_in_grad_shards(dp):
    d_in = []
    for s_ in range(4):
        lo, hi, segs = s_ * W_IN_SHARD, (s_ + 1) * W_IN_SHARD, []
        for o0, p0, w in sorted((o0, p0, w) for p0, o0, w in W_IN_PIECES):
            a, b = max(lo, o0), min(hi, o0 + w)
            if a < b:
                segs.append(dp[:, p0 + a - o0:p0 + b - o0])
        d_in.append(jnp.concatenate(segs, axis=1))
    return jnp.stack(d_in)


def shard_grads(dw):
    forms = {
        "in_p": lambda a: {"w_in": _w_in_grad_shards(a[0])},
        "uq_p": lambda a: {"w_uq": _cols_to_shards(
            a.reshape(Q_LORA, N_HEADS, MLA_PAD)[:, :, :MLA_QK].reshape(Q_LORA, N_HEADS * MLA_QK))},
        "ukv": lambda a: {"w_ukv": a},
        "br": lambda a: {"w_br": jnp.stack(a, axis=1)},
        "out": lambda a: {"w_out": a.reshape(4, 512, D_MODEL)},
        "xq": lambda a: {"w_xq": a.reshape(4, 512, 512)},
        "xkv": lambda a: {"w_xkv": a.reshape(4, 512, 1024)},
        "xo": lambda a: {"w_xo": a},
        "w1": lambda a: {"w_1": a},
        "w2": lambda a: {"w_2": a.reshape(4, 2048, D_MODEL)},
    }
    out = {}
    for k, a in dw.items():
        out.update(forms[k](a))
    return out


def layer_params(d, l):
    row = lambda v: v.reshape(1, -1).astype(F32)
    padto = lambda v, n: jnp.pad(row(v), ((0, 0), (0, n - v.shape[-1])))
    return dict(
        g_mix=row(d["g_mix"][l]), g_cq=row(d["g_cq"][l]), g_ckv=row(d["g_ckv"][l]),
        g_mla_q=padto(d["g_mla_q"][l], MLA_PAD), g_mla_k=padto(d["g_mla_k"][l], MLA_PAD),
        b_f=padto(d["b_f"][l], 128), g_fox_q=row(d["g_fox_q"][l]), g_fox_k=row(d["g_fox_k"][l]),
        rel_bias=d["rel_bias"][l].astype(F32), g_ch_q=row(d["g_ch_q"][l]), g_ch_k=row(d["g_ch_k"][l]),
        g_cross=row(d["g_cross"][l]), g_mem=row(d["g_mem"][l]), g_x_q=row(d["g_x_q"][l]),
        g_x_k=row(d["g_x_k"][l]), g_mlp=row(d["g_mlp"][l]))


FOX_B = Z_FOX // 128
CH_B = Z_CH // 128


def _after(small, token):
    return small if token is None else small + token[:1, :1].reshape((1,) * small.ndim)


def layer_fwd(x, mem, W, P, rope, mid_weights, last_weights):
    S = x.shape[0]
    s = {}
    s["h"] = rms_fwd(x, P["g_mix"], name="rms_d")
    z = s["z"] = mm_nn(s["h"], W["in_p"], name="mm_in", out_dtype=F32)
    s["cq_n"] = rms_fwd(z, P["g_cq"], col0=Z_CQ, width=Q_LORA, name="rms_cq")
    s["ckv_n"] = rms_fwd(z, P["g_ckv"], col0=Z_CKV, width=KV_LORA, name="rms_ckv")
    s["q_raw"] = mm_nn(s["cq_n"], W["uq_p"], name="mm_uq", out_dtype=F32)
    s["kv_raw"] = mm_nn(s["ckv_n"], W["ukv"], name="mm_ukv", out_dtype=F32)
    s["qa"] = prep_fwd([(s["q_raw"], MLA_PAD, lambda h: h)], P["g_mla_q"], name="prep_mla_q",
                       n_heads=N_HEADS, n_real=MLA_QK, rope=rope)
    s["ka"] = prep_fwd([(s["kv_raw"], 128, lambda h: 2 * h), (z, 128, lambda h: Z_KR // 128)], P["g_mla_k"],
                       name="prep_mla_k", n_heads=N_HEADS, n_real=MLA_QK, rope=rope)
    s["o_a"], s["lse_a"] = attn_fwd(s["qa"], s["ka"], s["kv_raw"], lambda h: 2 * h + 1, mode="mla",
                                    name="attn_mla", n_heads=N_HEADS)
    s["qb"] = prep_fwd([(z, DH, lambda h: FOX_B + h)], P["g_fox_q"], name="prep_h", n_heads=N_HEADS, n_real=DH)
    s["kb"] = prep_fwd([(z, DH, lambda h: FOX_B + N_HEADS + h)], P["g_fox_k"], name="prep_h",
                       n_heads=N_HEADS, n_real=DH)
    _, cum_t = fox_cum_fwd(z, P["b_f"])
    s["cq"] = cum_t[:N_HEADS].reshape(N_HEADS, S, 1)
    s["ck"] = cum_t[:N_HEADS].reshape(N_HEADS, 1, S)
    s["o_b"], s["lse_b"] = attn_fwd(s["qb"], s["kb"], z, lambda h: FOX_B + 2 * N_HEADS + h, mode="fox",
                                    name="attn_fox", n_heads=N_HEADS, cq=s["cq"], ck=s["ck"])
    s["qc"] = prep_fwd([(z, DH, lambda h: CH_B + h)], P["g_ch_q"], name="prep_h", n_heads=N_HEADS, n_real=DH)
    s["kc"] = prep_fwd([(z, DH, lambda h: CH_B + N_HEADS + h)], P["g_ch_k"], name="prep_h",
                       n_heads=N_HEADS, n_real=DH)
    s["tiles"] = relbias_tiles(P["rel_bias"], P["rel_onehot"])
    s["o_c"], s["lse_c"] = attn_fwd(s["qc"], s["kc"], z, lambda h: CH_B + 2 * N_HEADS + h, mode="chunk",
                                    name="attn_chunk", n_heads=N_HEADS, tiles=s["tiles"])
    mid, token = mid_weights(s["o_c"])
    W = dict(W, **mid)
    P = dict(P, g_cross=_after(P["g_cross"], token))
    s["projs"] = [mm_nn(o, W["br"][n], name="mm_br", out_dtype=F32)
                  for n, o in enumerate((s["o_a"], s["o_b"], s["o_c"]))]
    s["merged"] = merge_fwd(z, s["projs"])
    x1 = s["x1"] = mm_nn(s["merged"], W["out"], name="mm_out", out_dtype=F32, res=x)
    s["hq"] = rms_fwd(x1, P["g_cross"], name="rms_d")
    s["xq_raw"] = mm_nn(s["hq"], W["xq"], name="mm_xq", out_dtype=F32)
    s["mem_n"] = rms_fwd(mem, P["g_mem"], name="rms_mem")
    s["mkv"] = mm_nn(s["mem_n"], W["xkv"], name="mm_xkv", out_dtype=F32)
    s["qx"] = prep_fwd([(s["xq_raw"], DH, lambda h: h)], P["g_x_q"], name="prep_xq", n_heads=X_HEADS, n_real=DH)
    s["kx"] = prep_fwd([(s["mkv"], DH, lambda h: h)], P["g_x_k"], name="prep_xk", n_heads=X_HEADS, n_real=DH)
    s["o_x"], s["lse_x"] = attn_fwd(s["qx"], s["kx"], s["mkv"], lambda h: X_HEADS + h, mode="cross",
                                    name="attn_cross", n_heads=X_HEADS)
    x2 = s["x2"] = mm_nn(s["o_x"], W["xo"], name="mm_xo", out_dtype=F32, res=x1)
    s["hm"] = rms_fwd(x2, P["g_mlp"], name="rms_d")
    last, _ = last_weights(s["hm"])
    W = dict(W, **last)
    s["a1"], s["act"] = mm_nn(s["hm"], W["w1"], name="mm_w1", out_dtype=BF16, relu2=True)
    x3 = mm_nn(s["act"], W["w2"], name="mm_w2", out_dtype=F32, res=x2)
    return x3, s, W


LATE_GRADS = ("w2", "w1", "xo", "xq", "xkv", "out", "br")


def layer_bwd(g, x, mem, W, P, rope, s, grads_done):
    S = x.shape[0]
    z = s["z"]
    dw, ds = {}, {}
    da1 = mm_nt(g, W["w2"], name="mm_w2_dx", out_dtype=BF16, relu_mul=s["a1"])
    dw["w2"] = mm_tn(s["act"], g, nb=1, name="mm_w2_dw", out_dtype=BF16)
    dhm = mm_nt(da1, W["w1"], name="mm_w1_dx", out_dtype=F32)
    dw["w1"] = mm_tn(s["hm"], da1, nb=4, name="mm_w1_dw", out_dtype=BF16)
    g2, ds["g_mlp"] = rms_bwd(s["x2"], P["g_mlp"], dhm, name="rms_d_bwd", res=g)
    do_x = mm_nt(g2, W["xo"], name="mm_xo_dx", out_dtype=BF16)
    dw["xo"] = mm_tn(s["o_x"], g2, nb=4, name="mm_xo_dw", out_dtype=BF16)
    dqx, dkx, dvx = attn_bwd(s["qx"], s["kx"], s["mkv"], lambda h: X_HEADS + h, s["o_x"], do_x, s["lse_x"],
                             mode="cross", name="attn_cross_bwd", n_heads=X_HEADS)
    dxq_raw, ds["g_x_q"] = prep_bwd_q(s["xq_raw"], dqx, P["g_x_q"], name="prep_xq_bwd", n_heads=X_HEADS,
                                      dh=DH, n_real=DH)
    dmkv, dgk = prep_bwd_groups(s["mkv"], 0, [dkx, dvx], [P["g_x_k"], None], name="prep_xkv_bwd",
                                n_heads=X_HEADS, kinds=("norm", "copy"))
    ds["g_x_k"] = dgk[0]
    dhq = mm_nt(dxq_raw, W["xq"], name="mm_xq_dx", out_dtype=F32)
    dw["xq"] = mm_tn(s["hq"], dxq_raw, nb=1, name="mm_xq_dw", out_dtype=BF16)
    dmem_n = mm_nt(dmkv, W["xkv"], name="mm_xkv_dx", out_dtype=F32)
    dw["xkv"] = mm_tn(s["mem_n"], dmkv, nb=1, name="mm_xkv_dw", out_dtype=BF16)
    _, ds["g_mem"] = rms_bwd(mem, P["g_mem"], dmem_n, name="rms_mem_bwd", need_dx=False)
    g1, ds["g_cross"] = rms_bwd(s["x1"], P["g_cross"], dhq, name="rms_d_bwd", res=g2)
    dmerged = mm_nt(g1, W["out"], name="mm_out_dx", out_dtype=F32)
    dw["out"] = mm_tn(s["merged"], g1, nb=1, name="mm_out_dw", out_dtype=BF16)
    dgl, dproj = merge_bwd(z, s["projs"], dmerged)
    outs = (s["o_a"], s["o_b"], s["o_c"])
    do = [mm_nt(dproj[n], W["br"][n], name="mm_br_dx", out_dtype=BF16) for n in range(3)]
    dw["br"] = [mm_tn(outs[n], dproj[n], nb=4, name="mm_br_dw", out_dtype=BF16) for n in range(3)]
    token = grads_done("late", {k: dw.pop(k) for k in LATE_GRADS})
    dqc, dkc, dvc, dtiles = attn_bwd(s["qc"], s["kc"], z, lambda h: CH_B + 2 * N_HEADS + h, s["o_c"], do[2],
                                     s["lse_c"], mode="chunk", name="attn_chunk_bwd", n_heads=N_HEADS,
                                     tiles=_after(s["tiles"], token))
    d_ch, dg_ch = prep_bwd_groups(z, CH_B, [dqc, dkc, dvc], [P["g_ch_q"], P["g_ch_k"], None],
                                  name="prep_h_bwd", n_heads=N_HEADS, kinds=("norm", "norm", "copy"))
    ds["g_ch_q"], ds["g_ch_k"] = dg_ch[0], dg_ch[1]
    ds["rel_bias"] = relbias_tiles_bwd(dtiles, P["rel_onehot"])
    token = grads_done("late_followup", d_ch)
    dqb, dkb, dvb, dcq, dck = attn_bwd(s["qb"], s["kb"], z, lambda h: FOX_B + 2 * N_HEADS + h, s["o_b"], do[1],
                                       s["lse_b"], mode="fox", name="attn_fox_bwd", n_heads=N_HEADS,
                                       cq=s["cq"], ck=_after(s["ck"], token))
    d_fox, dg_fox = prep_bwd_groups(z, FOX_B, [dqb, dkb, dvb], [P["g_fox_q"], P["g_fox_k"], None],
                                    name="prep_h_bwd", n_heads=N_HEADS, kinds=("norm", "norm", "copy"))
    ds["g_fox_q"], ds["g_fox_k"] = dg_fox[0], dg_fox[1]
    dcum = jnp.pad((dcq[:, :, 0] + dck[:, 0, :]).T, ((0, 0), (0, 128 - N_HEADS)))
    dff, dbf = fox_cum_bwd(z, P["b_f"], dcum)
    ds["b_f"] = dbf[:, :N_HEADS]
    dqa, dka, dva = attn_bwd(s["qa"], s["ka"], s["kv_raw"], lambda h: 2 * h + 1, s["o_a"], do[0], s["lse_a"],
                             mode="mla", name="attn_mla_bwd", n_heads=N_HEADS)
    dq_raw, dgq = prep_bwd_q(s["q_raw"], dqa, P["g_mla_q"], name="prep_mla_q_bwd", n_heads=N_HEADS,
                             dh=MLA_PAD, n_real=MLA_QK, rope=rope)
    dkv_raw, dkr, dgk = prep_bwd_mla_k(s["kv_raw"], z, dka, dva, P["g_mla_k"], rope, name="prep_mla_k_bwd")
    ds["g_mla_q"], ds["g_mla_k"] = dgq[:, :MLA_QK], dgk[:, :MLA_QK]
    dcq_n = mm_nt(dq_raw, W["uq_p"], name="mm_uq_dx", out_dtype=F32)
    dw["uq_p"] = mm_tn(s["cq_n"], dq_raw, nb=1, name="mm_uq_dw", out_dtype=BF16)
    dckv_n = mm_nt(dkv_raw, W["ukv"], name="mm_ukv_dx", out_dtype=F32)
    dw["ukv"] = mm_tn(s["ckv_n"], dkv_raw, nb=4, name="mm_ukv_dw", out_dtype=BF16)
    d_cq, ds["g_cq"] = rms_bwd(z, P["g_cq"], dcq_n, name="rms_cq_bwd", col0=Z_CQ, width=Q_LORA, dx_dtype=BF16)
    d_ckv, ds["g_ckv"] = rms_bwd(z, P["g_ckv"], dckv_n, name="rms_ckv_bwd", col0=Z_CKV, width=KV_LORA,
                                 dx_dtype=BF16)
    dz = jnp.concatenate([d_cq, d_ckv, dkr.astype(BF16), dff.astype(BF16), d_fox, d_ch, *dgl], axis=1)
    dw["in_p"] = mm_tn(s["h"], dz, nb=1, name="mm_in_dw", out_dtype=BF16)
    token = grads_done("early", dw)
    dh = mm_nt(dz, W["in_p"], name="mm_in_dx", out_dtype=F32, after=token)
    g0, ds["g_mix"] = rms_bwd(x, P["g_mix"], dh, name="rms_d_bwd", res=g1)
    return g0, ds


def local_step(x, mem, target, Ps, weights_of, grads_done):
    rope = rope_tables(x.shape[0])
    onehot = relbias_onehot()
    Ps = [dict(P, rel_onehot=onehot) for P in Ps]
    Ws, saved, xs = [], [], [x]
    for l, P in enumerate(Ps):
        W_early, mid_weights, last_weights, token = weights_of(l, xs[-1])
        y, s, W = layer_fwd(xs[-1], mem, W_early, dict(P, g_mix=_after(P["g_mix"], token)), rope,
                            mid_weights, last_weights)
        xs.append(y)
        saved.append(s)
        Ws.append(W)
    g, loss = loss_head(xs[-1], target)
    dss, token = [], None
    for l in reversed(range(len(Ps))):
        g, ds = layer_bwd(g, xs[l], mem, Ws[l], dict(Ps[l], g_mlp=_after(Ps[l]["g_mlp"], token)), rope,
                          saved[l], functools.partial(grads_done, l))
        token = grads_done(l, "early_followup", g)
        dss.append(ds)
    return loss, g, dss[::-1]


HBM_SPEC = pl.BlockSpec(memory_space=pltpu.HBM)


def _place():
    return lax.axis_index("x"), lax.axis_index("y"), lax.axis_index("c")


def _other_chips(x, y):
    return [(1 - x, y), (x, 1 - y), (1 - x, 1 - y)]


def _remote(src, dst, send_sems, recv_sems, k, to):
    return pltpu.make_async_remote_copy(src_ref=src, dst_ref=dst, send_sem=send_sems.at[k],
                                        recv_sem=recv_sems.at[k], device_id=to, device_id_type=MESH)


SEM_SPEC = pl.BlockSpec(memory_space=pltpu.SEMAPHORE)
SPLIT_EFFECT = pltpu.SideEffectType.DATAFLOW_SIDE_EFFECTING


COPIES_PER_ARRAY = {"gather": 3, "scatter": 3, "halves": 1, "share": 1}


def _chip_exchange_copies(kind, srcs, lands, send_sems, recv_sems):
    x, y, c = _place()
    me = 2 * x + y
    cps = []
    for i, (src, land) in enumerate(zip(srcs, lands)):
        if kind == "halves":
            h = src.shape[1] // 2
            cps.append(_remote(src.at[:, pl.ds((1 - c) * h, h), :], land, send_sems, recv_sems, i, (x, y, 1 - c)))
            continue
        if kind == "share":
            cps.append(_remote(src, land, send_sems, recv_sems, i, (x, y, 1 - c)))
            continue
        for j, (cx, cy) in enumerate(_other_chips(x, y)):
            if kind == "gather":
                h = src.shape[0] // 2
                s_ref, d_ref = src.at[pl.ds(c * h, h), :], land.at[me, pl.ds(c * h, h), :]
            else:
                s_ref, d_ref = src.at[2 * cx + cy], land.at[me]
            cps.append(_remote(s_ref, d_ref, send_sems, recv_sems, 3 * i + j, (cx, cy, c)))
    return cps


def chip_exchange_start(kind, srcs, land_shapes, *, name, after=None):
    n = len(srcs)
    n_in = 2 * n + (after is not None)

    def body(*refs):
        send_sems, recv_sems = refs[n_in], refs[n_in + 1]
        for cp in _chip_exchange_copies(kind, refs[:n], refs[n:2 * n], send_sems, recv_sems):
            cp.start()
        refs[-1][...] = jnp.zeros_like(refs[-1])

    hbm = lambda a: pltpu.with_memory_space_constraint(a, pltpu.HBM)
    ins = [hbm(s) for s in srcs] + [hbm(lax.empty(s.shape, s.dtype)) for s in land_shapes]
    sems = pltpu.SemaphoreType.DMA((COPIES_PER_ARRAY[kind] * n,))
    out = pl.pallas_call(
        body, name=name, interpret=False,
        out_shape=(sems, sems, *[pltpu.HBM(a.shape, a.dtype) for a in ins], _sds((8, 128), F32)),
        in_specs=[HBM_SPEC] * (2 * n) + [pl.BlockSpec(memory_space=pl.ANY)] * (after is not None),
        out_specs=(SEM_SPEC, SEM_SPEC, *[HBM_SPEC] * (2 * n), pl.BlockSpec(memory_space=pltpu.VMEM)),
        input_output_aliases={i: 2 + i for i in range(2 * n)},
        compiler_params=pltpu.CompilerParams(has_side_effects=SPLIT_EFFECT))(
            *ins, *([after] if after is not None else []))
    return out[0], out[1], list(out[2:2 + n]), list(out[2 + n:2 + 2 * n]), out[-1]


def chip_exchange_wait(kind, started, after, *, name):
    send_sems, recv_sems, srcs, lands, _ = started
    n = len(srcs)

    def body(*refs):
        send_sems, recv_sems = refs[2 * n], refs[2 * n + 1]
        for cp in _chip_exchange_copies(kind, refs[:n], refs[n:2 * n], send_sems, recv_sems):
            cp.wait_send()
            cp.wait_recv()

    thru = srcs + lands
    out = pl.pallas_call(
        body, name=name, interpret=False,
        out_shape=tuple(pltpu.HBM(a.shape, a.dtype) for a in thru),
        in_specs=[HBM_SPEC] * (2 * n) + [SEM_SPEC, SEM_SPEC, pl.BlockSpec(memory_space=pl.ANY)],
        out_specs=tuple([HBM_SPEC] * (2 * n)),
        input_output_aliases={i: i for i in range(2 * n)},
        compiler_params=pltpu.CompilerParams(has_side_effects=SPLIT_EFFECT))(*thru, send_sems, recv_sems, after)
    return list(out[:n]), list(out[n:])


def gather_to_sibling(lands):
    n = len(lands)

    def body(*refs):
        outs = refs[n:2 * n]
        send_sems, recv_sems = refs[2 * n:]
        x, y, c = _place()
        sends = []
        for i in range(n):
            h = outs[i].shape[1] // 2
            for j, (cx, cy) in enumerate(_other_chips(x, y)):
                landed = outs[i].at[2 * cx + cy, pl.ds(c * h, h), :]
                cp = _remote(landed, landed, send_sems, recv_sems, 3 * i + j, (x, y, 1 - c))
                cp.start()
                sends.append(cp)
        for cp in sends:
            cp.wait()

    return pl.pallas_call(
        body, out_shape=tuple(_sds(a.shape, a.dtype) for a in lands), in_specs=[HBM_SPEC] * n,
        out_specs=tuple([HBM_SPEC] * n), input_output_aliases={i: i for i in range(n)},
        scratch_shapes=[pltpu.SemaphoreType.DMA((3 * n,)), pltpu.SemaphoreType.DMA((3 * n,))],
        name="gather_to_sibling", interpret=False)(*lands)


def add_pair(part, got, half_idx):
    _, a, b = part.shape
    h = a // 2
    tr = _pick(h, (512, 256, 128))

    def body(c_ref, p_ref, g_ref, o_ref):
        o_ref[...] = (p_ref[...].astype(F32) + g_ref[...].astype(F32)).astype(o_ref.dtype)

    spec = pl.BlockSpec((None, tr, b), lambda s_, i, c_ref: (s_, i, 0))
    grid_spec = pltpu.PrefetchScalarGridSpec(
        num_scalar_prefetch=1, grid=(4, h // tr),
        in_specs=[pl.BlockSpec((None, None, tr, b), lambda s_, i, c_ref: (s_, c_ref[0], i, 0)), spec],
        out_specs=spec)
    return pl.pallas_call(
        body, out_shape=_sds((4, h, b), part.dtype), grid_spec=grid_spec, name="rs_add_pair", interpret=False,
        compiler_params=pltpu.CompilerParams(vmem_limit_bytes=VMEM_LIMIT_BYTES))(
            half_idx, part.reshape(4, 2, h, b), got)


def sum_slots(r):
    _, h, b = r.shape
    tr = _pick(h, (512, 256, 128))

    def body(r0, r1, r2, r3, o_ref):
        o_ref[...] = ((r0[...].astype(F32) + r1[...].astype(F32)) + r2[...].astype(F32)) + r3[...].astype(F32)

    specs = [pl.BlockSpec((None, tr, b), functools.partial(lambda i, s_: (s_, i, 0), s_=s_)) for s_ in range(4)]
    return _pcall(body, name="rs_sum_slots", out_shape=_sds((h, b), F32), grid=(h // tr,),
                  in_specs=specs, out_specs=pl.BlockSpec((tr, b), lambda i: (i, 0)))(r, r, r, r)


def reduce_halves_start(parts, *, tag):
    return chip_exchange_start("halves", parts, [_sds((4, p.shape[1] // 2, p.shape[2]), p.dtype) for p in parts],
                               name="halves_start_" + tag)


def reduce_scatter_start(halves_started, after, core, *, tag):
    parts, gots = chip_exchange_wait("halves", halves_started, after, name="halves_wait_" + tag)
    half_idx = core.reshape(1).astype(jnp.int32)
    chip_sums = [add_pair(p, g, half_idx) for p, g in zip(parts, gots)]
    return chip_exchange_start("scatter", chip_sums, chip_sums, name="scatter_start_" + tag)


def reduce_share_start(scatter_started, after, chip, *, tag):
    chip_sums, slots = chip_exchange_wait("scatter", scatter_started, after, name="scatter_wait_" + tag)
    slots = [lax.dynamic_update_slice(s_, lax.dynamic_index_in_dim(q, chip, 0, keepdims=True), (chip, 0, 0))
             for s_, q in zip(slots, chip_sums)]
    halves = [sum_slots(s_) for s_ in slots]
    return chip_exchange_start("share", halves, [_sds(t.shape, t.dtype) for t in halves], name="share_start_" + tag)


def reduce_finish(share_started, after, *, tag):
    halves, theirs = chip_exchange_wait("share", share_started, after, name="share_wait_" + tag)
    return list(zip(halves, theirs))


def allreduce_small(v):
    Rs = v.shape[0]

    def body(v_ref, o_ref, buf, send_sems, recv_sems):
        x, y, c = _place()
        me = 4 * x + 2 * y + c
        buf[me] = v_ref[...]
        flips = [(fx, fy, fc) for fx in (0, 1) for fy in (0, 1) for fc in (0, 1)][1:]
        sends = []
        for k, (fx, fy, fc) in enumerate(flips):
            to = ((1 - x) if fx else x, (1 - y) if fy else y, (1 - c) if fc else c)
            cp = _remote(v_ref, buf.at[me], send_sems, recv_sems, k, to)
            cp.start()
            sends.append(cp)
        for cp in sends:
            cp.wait()
        acc = buf[0]
        for d in range(1, 8):
            acc = acc + buf[d]
        o_ref[...] = acc

    vm = pl.BlockSpec(memory_space=pltpu.VMEM)
    return pl.pallas_call(
        body, out_shape=_sds((Rs, 128), F32), in_specs=[vm], out_specs=vm,
        scratch_shapes=[pltpu.VMEM((8, Rs, 128), F32), pltpu.SemaphoreType.DMA((7,)),
                        pltpu.SemaphoreType.DMA((7,))],
        name="allreduce_small", interpret=False)(v)


INPUT_NAMES = (("x", "mem") + WEIGHT_ORDER + ("loss_target",) + tuple("m_" + n for n in WEIGHT_ORDER)
               + tuple("v_" + n for n in WEIGHT_ORDER))


def _pack_small(vals, n_layers, extra=None):
    flat = jnp.concatenate([vals[n].reshape(n_layers, -1).astype(F32) for n in SMALL_ORDER], axis=1).reshape(-1)
    if extra is not None:
        flat = jnp.concatenate([flat, extra.reshape(-1)])
    n = flat.shape[0]
    rows = -(-n // 1024) * 8
    return jnp.pad(flat, (0, rows * 128 - n)).reshape(rows, 128)


def _unpack_small(packed, like, n_layers):
    per_layer = sum(int(np.prod(like[n].shape[1:])) for n in SMALL_ORDER)
    body = packed.reshape(-1)[:n_layers * per_layer].reshape(n_layers, per_layer)
    out, off = {}, 0
    for n in SMALL_ORDER:
        k = int(np.prod(like[n].shape[1:]))
        out[n] = body[:, off:off + k].reshape(like[n].shape)
        off += k
    return out, packed.reshape(-1)[n_layers * per_layer]


def kernel(x, mem, g_mix, w_in, g_cq, w_uq, g_ckv, w_ukv, g_mla_q, g_mla_k, b_f, g_fox_q, g_fox_k, rel_bias, g_ch_q, g_ch_k, w_br, w_out, g_cross, g_mem, w_xq, w_xkv, g_x_q, g_x_k, w_xo, g_mlp, w_1, w_2, loss_target, m_g_mix, m_w_in, m_g_cq, m_w_uq, m_g_ckv, m_w_ukv, m_g_mla_q, m_g_mla_k, m_b_f, m_g_fox_q, m_g_fox_k, m_rel_bias, m_g_ch_q, m_g_ch_k, m_w_br, m_w_out, m_g_cross, m_g_mem, m_w_xq, m_w_xkv, m_g_x_q, m_g_x_k, m_w_xo, m_g_mlp, m_w_1, m_w_2, v_g_mix, v_w_in, v_g_cq, v_w_uq, v_g_ckv, v_w_ukv, v_g_mla_q, v_g_mla_k, v_b_f, v_g_fox_q, v_g_fox_k, v_rel_bias, v_g_ch_q, v_g_ch_k, v_w_br, v_w_out, v_g_cross, v_g_mem, v_w_xq, v_w_xkv, v_g_x_q, v_g_x_k, v_w_xo, v_g_mlp, v_w_1, v_w_2):
    d = dict(zip(INPUT_NAMES, (x, mem, g_mix, w_in, g_cq, w_uq, g_ckv, w_ukv, g_mla_q, g_mla_k, b_f, g_fox_q, g_fox_k, rel_bias, g_ch_q, g_ch_k, w_br, w_out, g_cross, g_mem, w_xq, w_xkv, g_x_q, g_x_k, w_xo, g_mlp, w_1, w_2, loss_target, m_g_mix, m_w_in, m_g_cq, m_w_uq, m_g_ckv, m_w_ukv, m_g_mla_q, m_g_mla_k, m_b_f, m_g_fox_q, m_g_fox_k, m_rel_bias, m_g_ch_q, m_g_ch_k, m_w_br, m_w_out, m_g_cross, m_g_mem, m_w_xq, m_w_xkv, m_g_x_q, m_g_x_k, m_w_xo, m_g_mlp, m_w_1, m_w_2, v_g_mix, v_w_in, v_g_cq, v_w_uq, v_g_ckv, v_w_ukv, v_g_mla_q, v_g_mla_k, v_b_f, v_g_fox_q, v_g_fox_k, v_rel_bias, v_g_ch_q, v_g_ch_k, v_w_br, v_w_out, v_g_cross, v_g_mem, v_w_xq, v_w_xkv, v_g_x_q, v_g_x_k, v_w_xo, v_g_mlp, v_w_1, v_w_2)))
    n_layers = g_mix.shape[0]
    assert x.shape[0] == 1 and x.shape[2] == D_MODEL and mem.shape[1:] == (MEM_LEN, D_MODEL)
    for n in PACK_ORDER:
        assert d[n].shape[1:] == SHARD_SHAPES[n], (n, d[n].shape)

    chip = 2 * lax.axis_index("x") + lax.axis_index("y")
    core = lax.axis_index("c")

    def start_gather(l, names, after):
        mine = [_rows2d(d[n][l].astype(BF16), n) for n in names]
        return chip_exchange_start("gather", mine, [_sds((4,) + m_.shape, BF16) for m_ in mine],
                                   name=f"gather_start_{l}_{names[0]}", after=after)

    def finish_gather(l, names, started, after):
        mine, lands = chip_exchange_wait("gather", started, after, name=f"gather_wait_{l}_{names[0]}")
        full = [lax.dynamic_update_slice(t, m_[None], (chip, 0, 0)) for t, m_ in zip(gather_to_sibling(lands), mine)]
        return mine, full_weights({n: f.reshape((4,) + SHARD_SHAPES[n]) for n, f in zip(names, full)})

    in_flight = {"early": start_gather(0, EARLY_WEIGHTS, x)}

    def weights_of(l, x_l):
        mine, early = finish_gather(l, EARLY_WEIGHTS, in_flight.pop("early"), x_l)
        in_flight["mid"] = start_gather(l, MID_WEIGHTS, mine[0])
        in_flight["last"] = start_gather(l, LAST_WEIGHTS, in_flight["mid"][4])

        def mid_weights(after):
            mine, mid = finish_gather(l, MID_WEIGHTS, in_flight.pop("mid"), after)
            if l + 1 == n_layers:
                return mid, None
            in_flight["early"] = start_gather(l + 1, EARLY_WEIGHTS, mine[0])
            return mid, in_flight["early"][4]

        def last_weights(after):
            return finish_gather(l, LAST_WEIGHTS, in_flight.pop("last"), after)[1], None

        return early, mid_weights, last_weights, in_flight["mid"][4] + in_flight["last"][4]

    halves, scatters = {}, []

    def grads_done(l, stage, arg):
        names = LATE_WEIGHTS if stage.startswith("late") else EARLY_WEIGHTS
        tag = f"{l}_{names[0]}"
        if not stage.endswith("followup"):
            sg = shard_grads(arg)
            halves[tag] = reduce_halves_start([_rows2d(sg[n], n) for n in names], tag=tag)
            if (l, stage) != (0, "early"):
                return halves[tag][4]
            arg = halves[tag][4]
        elif tag not in halves:
            return None
        started = reduce_scatter_start(halves.pop(tag), arg, core, tag=tag)
        scatters.append((l, names, tag, started))
        return started[4]

    Ps = [layer_params(d, l) for l in range(n_layers)]
    loss, dx, dss = local_step(x[0], mem[0], loss_target[0], Ps, weights_of, grads_done)
    shares = [(l, names, tag, reduce_share_start(started, dx, chip, tag=tag))
              for l, names, tag, started in scatters]

    small_local = {n: jnp.stack([dss[l][n].reshape(d[n].shape[1:]) for l in range(n_layers)]) for n in SMALL_ORDER}
    small_all = allreduce_small(_pack_small(small_local, n_layers, extra=loss))
    small_sum, loss_sum = _unpack_small(small_all, {n: d[n] for n in SMALL_ORDER}, n_layers)

    grads, delta, new_m, new_v = {}, {}, {}, {}
    big, after = [{} for _ in range(n_layers)], small_all
    for group in (LATE_WEIGHTS, EARLY_WEIGHTS):
        for l, names, tag, started in shares:
            if names is group:
                big[l].update(zip(names, reduce_finish(started, after, tag=tag)))
        for n in group:
            outs = adamw_halves(_rows2d(d[n], n), [big[l][n] for l in range(n_layers)], _rows2d(d["m_" + n], n),
                                _rows2d(d["v_" + n], n), core.reshape(1).astype(jnp.int32), name="adamw_" + n)
            grads[n], delta[n], new_m[n], new_v[n] = (o.reshape(d[n].shape) for o in outs)
        after = outs[1]
    like = {n: d[n] for n in SMALL_ORDER}
    sm = adamw(_pack_small(like, n_layers)[None], [_pack_small(small_sum, n_layers)],
               _pack_small({n: d["m_" + n] for n in SMALL_ORDER}, n_layers)[None],
               _pack_small({n: d["v_" + n] for n in SMALL_ORDER}, n_layers)[None], name="adamw_small")
    for res, src in zip((grads, delta, new_m, new_v), sm):
        res.update(_unpack_small(src[0], like, n_layers)[0])

    return (loss_sum, dx[None], *[grads[n] for n in WEIGHT_ORDER], *[delta[n] for n in WEIGHT_ORDER],
            *[new_m[n] for n in WEIGHT_ORDER], *[new_v[n] for n in WEIGHT_ORDER])
```
